```python
import math, functools
import jax, jax.numpy as jnp
from jax import lax
import numpy as np

D_MODEL = 1024
BATCH = 2
SEQ = 8192
DEPTH = 1

GRID_W = 64
D_MIX = D_MODEL
HEAD_DIM = 64
N_Q_HEADS = 8
N_KV_HEADS = 2
GQA_GROUP = N_Q_HEADS // N_KV_HEADS
ATTN_W = N_Q_HEADS * HEAD_DIM
KV_W = N_KV_HEADS * HEAD_DIM
LRU_W = D_MIX - ATTN_W
LRU_BLOCKS = 8
LRU_BW = LRU_W // LRU_BLOCKS
LRU_C = 8.0
CONV_W = 4
CONV_PAD_L = 2
ROPE_THETA = 10000.0
ROPE_HALF = HEAD_DIM // 2
Q_BLOCK = 128
N_EXPERTS = 32
TOP_K = 4
D_FF = D_MODEL
SWIGLU_ALPHA = 1.702
SWIGLU_LIMIT = 7.0
MOE_BLOCK = 256
NORM_EPS = 1e-5
IN_W = ATTN_W + 2 * KV_W + 2 * LRU_W

kernel_name = 'hybrid_attn_rglru_moe_encoder'


def rms_norm(x, g, eps=NORM_EPS):
    xf = x.astype(jnp.float32)
    y = xf * lax.rsqrt(jnp.mean(xf * xf, axis=-1, keepdims=True) + eps)
    return (y * g.astype(jnp.float32)).astype(x.dtype)


def rope_1d(x, pos):
    m = x.shape[-1] // 2
    inv_freq = ROPE_THETA ** (-jnp.arange(m, dtype=jnp.float32) / m)
    ang = pos.astype(jnp.float32)[:, None] * inv_freq[None, :]
    cos = jnp.cos(ang)[None, :, None, :]
    sin = jnp.sin(ang)[None, :, None, :]
    xf = x.astype(jnp.float32)
    x1, x2 = xf[..., :m], xf[..., m:]
    out = jnp.concatenate([x1 * cos - x2 * sin, x2 * cos + x1 * sin], axis=-1)
    return out.astype(x.dtype)


def rope_axial_2d(x, rows, cols):
    return jnp.concatenate([rope_1d(x[..., :ROPE_HALF], rows),
                            rope_1d(x[..., ROPE_HALF:], cols)], axis=-1)


def block_attention(q, k, v):
    B, S = q.shape[0], q.shape[1]
    nblk = S // Q_BLOCK
    qb = q.reshape(B, nblk, Q_BLOCK, N_KV_HEADS, GQA_GROUP, HEAD_DIM)
    qb = jnp.moveaxis(qb, 1, 0)
    scale = HEAD_DIM ** -0.5

    def one_block(qi):
        s = jnp.einsum('bqkgd,bskd->bkgqs', qi, k).astype(jnp.float32) * scale
        p = jax.nn.softmax(s, axis=-1).astype(v.dtype)
        return jnp.einsum('bkgqs,bskd->bqkgd', p, v)

    o = lax.map(one_block, qb)
    return jnp.moveaxis(o, 0, 1).reshape(B, S, ATTN_W)


def centred_depthwise_conv(u, w, b):
    S = u.shape[1]
    up = jnp.pad(u, ((0, 0), (CONV_PAD_L, CONV_W - 1 - CONV_PAD_L), (0, 0)))
    acc = b
    for j in range(CONV_W):
        acc = acc + up[:, j:j + S, :] * w[j]
    return acc


def _lin_combine(c1, c2):
    a1, b1 = c1
    a2, b2 = c2
    return a1 * a2, a2 * b1 + b2


def rglru_direction(xc, wa, ba, wi, bi, lam, reverse):
    B, S, _ = xc.shape
    xb = xc.reshape(B, S, LRU_BLOCKS, LRU_BW)
    r = jax.nn.sigmoid((jnp.einsum('bsni,nij->bsnj', xb, wa).reshape(B, S, LRU_W) + ba).astype(jnp.float32))
    i = jax.nn.sigmoid((jnp.einsum('bsni,nij->bsnj', xb, wi).reshape(B, S, LRU_W) + bi).astype(jnp.float32))
    log_a = -LRU_C * r * jax.nn.softplus(-lam.astype(jnp.float32))
    a = jnp.exp(log_a)
    mult = jnp.sqrt(-jnp.expm1(2.0 * log_a))
    bterm = mult * i * xc.astype(jnp.float32)
    _, h = lax.associative_scan(_lin_combine, (a, bterm), reverse=reverse, axis=1)
    return h


def moe_ffn(xn, w_router, b_router, w_gate, b_gate, w_up, b_up, w_down, b_down):
    B, S, D = xn.shape
    T = B * S
    xf = xn.reshape(T, D)
    logits = (xf @ w_router + b_router).astype(jnp.float32)
    top_v, top_i = lax.top_k(logits, TOP_K)
    gates = jax.nn.softmax(top_v, axis=-1).astype(xn.dtype)

    TK = T * TOP_K
    flat_e = top_i.reshape(TK).astype(jnp.int32)
    flat_tok = jnp.arange(TK, dtype=jnp.int32) // TOP_K
    flat_g = gates.reshape(TK)
    order = jnp.argsort(flat_e, stable=True)
    sorted_e = flat_e[order]
    counts = jnp.bincount(flat_e, length=N_EXPERTS)
    padded = ((counts + MOE_BLOCK - 1) // MOE_BLOCK) * MOE_BLOCK
    start = jnp.cumsum(counts) - counts
    pend = jnp.cumsum(padded)
    pstart = pend - padded
    dest = pstart[sorted_e] + (jnp.arange(TK, dtype=jnp.int32) - start[sorted_e])

    n_rows = TK + N_EXPERTS * MOE_BLOCK
    n_blocks = n_rows // MOE_BLOCK
    row_tok = jnp.full((n_rows,), T, dtype=jnp.int32).at[dest].set(flat_tok[order])
    row_gate = jnp.zeros((n_rows,), dtype=xn.dtype).at[dest].set(flat_g[order])
    block_e = jnp.searchsorted(pend, jnp.arange(n_blocks) * MOE_BLOCK, side='right')
    block_e = jnp.minimum(block_e, N_EXPERTS - 1).astype(jnp.int32)

    xpad = jnp.concatenate([xf, jnp.zeros((1, D), xf.dtype)], axis=0)
    x_rows = xpad[row_tok].reshape(n_blocks, MOE_BLOCK, D)

    def expert_block(args):
        xb, e = args
        g = xb @ w_gate[e] + b_gate[e]
        u = xb @ w_up[e] + b_up[e]
        g = jnp.minimum(g, SWIGLU_LIMIT)
        u = jnp.clip(u, -SWIGLU_LIMIT, SWIGLU_LIMIT)
        glu = g * jax.nn.sigmoid(SWIGLU_ALPHA * g)
        return ((u + 1.0) * glu) @ w_down[e] + b_down[e]

    y_rows = lax.map(expert_block, (x_rows, block_e)).reshape(n_rows, D)
    y_rows = y_rows * row_gate[:, None]
    out = jax.ops.segment_sum(y_rows, row_tok, num_segments=T + 1)[:T]
    return out.reshape(B, S, D)


def setup_inputs(seed: int = 0) -> dict:
    key = jax.random.key(seed)
    ks = jax.random.split(key, 32)
    f32 = jnp.float32
    L = DEPTH

    def nrm(k, shape, fan_in):
        return jax.random.normal(k, shape, f32) * (fan_in ** -0.5)

    def gain(k, shape):
        return 1.0 + 0.05 * jax.random.normal(k, shape, f32)

    def bias(k, shape, s=0.01):
        return s * jax.random.normal(k, shape, f32)

    u = jax.random.uniform(ks[10], (L, 2, LRU_W), f32, minval=0.9, maxval=0.999)
    sgm = u ** (1.0 / LRU_C)
    lru_lam = jnp.log(sgm) - jnp.log1p(-sgm)

    return {
        'x': jax.random.normal(ks[0], (BATCH, SEQ, D_MODEL), f32),
        'norm1_g': gain(ks[1], (L, D_MODEL)),
        'w_in': nrm(ks[2], (L, D_MODEL, IN_W), D_MODEL),
        'q_norm_g': gain(ks[3], (L, HEAD_DIM)),
        'k_norm_g': gain(ks[4], (L, HEAD_DIM)),
        'conv_w': nrm(ks[5], (L, CONV_W, LRU_W), CONV_W),
        'conv_b': bias(ks[6], (L, LRU_W)),
        'lru_wa': nrm(ks[7], (L, 2, LRU_BLOCKS, LRU_BW, LRU_BW), LRU_BW),
        'lru_ba': bias(ks[8], (L, 2, LRU_W), 0.1),
        'lru_wi': nrm(ks[9], (L, 2, LRU_BLOCKS, LRU_BW, LRU_BW), LRU_BW),
        'lru_bi': bias(ks[11], (L, 2, LRU_W), 0.1),
        'lru_lam': lru_lam,
        'attn_out_g': gain(ks[12], (L, ATTN_W)),
        'lru_out_g': gain(ks[13], (L, LRU_W)),
        'w_out': nrm(ks[14], (L, D_MIX, D_MODEL), D_MIX),
        'norm2_g': gain(ks[15], (L, D_MODEL)),
        'w_router': nrm(ks[16], (L, D_MODEL, N_EXPERTS), D_MODEL),
        'b_router': bias(ks[17], (L, N_EXPERTS)),
        'w_gate': nrm(ks[18], (L, N_EXPERTS, D_MODEL, D_FF), D_MODEL),
        'b_gate': bias(ks[19], (L, N_EXPERTS, D_FF)),
        'w_up': nrm(ks[20], (L, N_EXPERTS, D_MODEL, D_FF), D_MODEL),
        'b_up': bias(ks[21], (L, N_EXPERTS, D_FF)),
        'w_down': nrm(ks[22], (L, N_EXPERTS, D_FF, D_MODEL), D_FF),
        'b_down': bias(ks[23], (L, N_EXPERTS, D_MODEL)),
        'final_g': gain(ks[24], (D_MODEL,)),
    }


def reference(x, norm1_g, w_in, q_norm_g, k_norm_g, conv_w, conv_b, lru_wa, lru_ba,
              lru_wi, lru_bi, lru_lam, attn_out_g, lru_out_g, w_out, norm2_g,
              w_router, b_router, w_gate, b_gate, w_up, b_up, w_down, b_down, final_g):
    B, S, _ = x.shape
    n_rows_grid = S // GRID_W
    rows = jnp.broadcast_to(jnp.arange(n_rows_grid)[:, None], (n_rows_grid, GRID_W)).reshape(S)
    cols = jnp.broadcast_to(jnp.arange(GRID_W)[None, :], (n_rows_grid, GRID_W)).reshape(S)

    for l in range(DEPTH):
        xn = rms_norm(x, norm1_g[l])
        h_in = xn @ w_in[l]
        o0, o1, o2, o3 = ATTN_W, ATTN_W + KV_W, ATTN_W + 2 * KV_W, ATTN_W + 2 * KV_W + LRU_W
        q = h_in[..., :o0].reshape(B, S, N_Q_HEADS, HEAD_DIM)
        k = h_in[..., o0:o1].reshape(B, S, N_KV_HEADS, HEAD_DIM)
        v = h_in[..., o1:o2].reshape(B, S, N_KV_HEADS, HEAD_DIM)
        lru_x = h_in[..., o2:o3]
        lru_gate = h_in[..., o3:]

        q = rope_axial_2d(rms_norm(q, q_norm_g[l], 1e-6), rows, cols)
        k = rope_axial_2d(rms_norm(k, k_norm_g[l], 1e-6), rows, cols)
        attn = block_attention(q, k, v)

        xc = centred_depthwise_conv(lru_x, conv_w[l], conv_b[l])
        h_f = rglru_direction(xc, lru_wa[l, 0], lru_ba[l, 0], lru_wi[l, 0], lru_bi[l, 0],
                              lru_lam[l, 0], False)
        h_b = rglru_direction(xc, lru_wa[l, 1], lru_ba[l, 1], lru_wi[l, 1], lru_bi[l, 1],
                              lru_lam[l, 1], True)
        lru = ((h_f + h_b).astype(x.dtype)) * jax.nn.gelu(lru_gate)

        mixed = jnp.concatenate([rms_norm(attn, attn_out_g[l]),
                                 rms_norm(lru, lru_out_g[l])], axis=-1)
        x = x + mixed @ w_out[l]

        x = x + moe_ffn(rms_norm(x, norm2_g[l]), w_router[l], b_router[l], w_gate[l],
                        b_gate[l], w_up[l], b_up[l], w_down[l], b_down[l])

    return rms_norm(x, final_g)
```

```python
import functools

import jax
import jax.numpy as jnp
from jax import lax
from jax.experimental import pallas as pl
from jax.experimental.pallas import tpu as pltpu

F32 = jnp.float32
BF16 = jnp.bfloat16

GRID_W = 64
HEAD_DIM = 64
N_Q_HEADS = 8
N_KV_HEADS = 2
GQA_GROUP = N_Q_HEADS // N_KV_HEADS
ATTN_W = N_Q_HEADS * HEAD_DIM
KV_W = N_KV_HEADS * HEAD_DIM
LRU_BLOCKS = 8
LRU_C = 8.0
CONV_W = 4
CONV_PAD_L = 2
ROPE_THETA = 10000.0
ROPE_HALF = HEAD_DIM // 2
ROPE_M = ROPE_HALF // 2
N_EXPERTS = 32
TOP_K = 4
SWIGLU_ALPHA = 1.702
SWIGLU_LIMIT = 7.0
NORM_EPS = 1e-5
QK_EPS = 1e-6

LANES = 128
SUBLANES = 8
VMEM_LIMIT = 48 * 1024 * 1024

TS_IN = 512
TQ = 256
TK = 512
TC_LRU = 512
TS_OUT = 512
ROW_BLOCK = 256
TS_DISP = 512
TS_COMB = 256


def _cparams(sem):
    return pltpu.CompilerParams(dimension_semantics=sem, vmem_limit_bytes=VMEM_LIMIT)


def _inproj_kernel(x_ref, g1_ref, w_ref, qg_ref, kg_ref, cos_ref, sin_ref,
                   q_ref, k_ref, v_ref, lx_ref, lg_ref, *, lru_w):
    x = x_ref[...]
    ms = jnp.mean(x * x, axis=-1, keepdims=True)
    xn = x * lax.rsqrt(ms + NORM_EPS) * g1_ref[...]
    h = jnp.dot(xn.astype(BF16), w_ref[...], preferred_element_type=F32)

    cos = cos_ref[...]
    sin = sin_ref[...]
    lane = lax.broadcasted_iota(jnp.int32, cos.shape, 1)
    first_half = (lane % ROPE_HALF) < ROPE_M

    def head_norm_rope(xc, g, scale):
        hms = jnp.sum(xc * xc, axis=-1, keepdims=True) * (1.0 / HEAD_DIM)
        xc = xc * lax.rsqrt(hms + QK_EPS) * g
        partner = jnp.where(first_half,
                            pltpu.roll(xc, LANES - ROPE_M, 1),
                            pltpu.roll(xc, ROPE_M, 1))
        return (xc * cos + partner * sin) * scale

    qw = N_Q_HEADS * LANES
    kw = N_KV_HEADS * LANES
    for c in range(N_Q_HEADS):
        sl = slice(c * LANES, (c + 1) * LANES)
        q_ref[:, sl] = head_norm_rope(h[:, sl], qg_ref[...], HEAD_DIM ** -0.5).astype(BF16)
    for c in range(N_KV_HEADS):
        sl = slice(c * LANES, (c + 1) * LANES)
        k_ref[:, sl] = head_norm_rope(h[:, qw + c * LANES: qw + (c + 1) * LANES],
                                      kg_ref[...], 1.0).astype(BF16)
    v = h[:, qw + kw: qw + 2 * kw]
    vlane = lax.broadcasted_iota(jnp.int32, v.shape, 1)
    v_ref[...] = jnp.where((vlane % LANES) >= HEAD_DIM, 1.0, v).astype(BF16)
    o = qw + 2 * kw
    lx_ref[...] = h[:, o: o + lru_w]
    lg_ref[...] = h[:, o + lru_w: o + 2 * lru_w]


def _pad_heads(w, n_heads):
    lead = w.shape[:-1]
    w = w.reshape(lead + (n_heads, HEAD_DIM))
    w = jnp.pad(w, [(0, 0)] * len(lead) + [(0, 0), (0, LANES - HEAD_DIM)])
    return w.reshape(lead + (n_heads * LANES,))


def _rope_tables(S):
    t = jnp.arange(S)
    rows = (t // GRID_W).astype(F32)
    cols = (t % GRID_W).astype(F32)
    inv_freq = ROPE_THETA ** (-jnp.arange(ROPE_M, dtype=F32) / ROPE_M)
    ar = rows[:, None] * inv_freq[None, :]
    ac = cols[:, None] * inv_freq[None, :]
    cos = jnp.concatenate([jnp.cos(ar), jnp.cos(ar), jnp.cos(ac), jnp.cos(ac)], axis=-1)
    sin = jnp.concatenate([-jnp.sin(ar), jnp.sin(ar), -jnp.sin(ac), jnp.sin(ac)], axis=-1)
    pad = [(0, 0), (0, LANES - HEAD_DIM)]
    return jnp.pad(cos, pad), jnp.pad(sin, pad)


def _inproj(x2, norm1_g, w_in, q_norm_g, k_norm_g, S):
    T, D = x2.shape
    lru_w = (w_in.shape[1] - ATTN_W - 2 * KV_W) // 2
    o0, o1, o2 = ATTN_W, ATTN_W + KV_W, ATTN_W + 2 * KV_W
    w_all = jnp.concatenate([
        _pad_heads(w_in[:, :o0], N_Q_HEADS),
        _pad_heads(w_in[:, o0:o1], N_KV_HEADS),
        _pad_heads(w_in[:, o1:o2], N_KV_HEADS),
        w_in[:, o2:],
    ], axis=1).astype(BF16)
    qg = _pad_heads(q_norm_g.reshape(1, HEAD_DIM), 1)
    kg = _pad_heads(k_norm_g.reshape(1, HEAD_DIM), 1)
    cos, sin = _rope_tables(S)
    ts = TS_IN
    n_s = S // ts
    qw, kw = N_Q_HEADS * LANES, N_KV_HEADS * LANES
    const = lambda i: (0, 0)
    tok = lambda i: (i, 0)
    pos = lambda i: (i % n_s, 0)
    return pl.pallas_call(
        functools.partial(_inproj_kernel, lru_w=lru_w),
        grid=(T // ts,),
        in_specs=[
            pl.BlockSpec((ts, D), tok),
            pl.BlockSpec((1, D), const),
            pl.BlockSpec(w_all.shape, const),
            pl.BlockSpec((1, LANES), const),
            pl.BlockSpec((1, LANES), const),
            pl.BlockSpec((ts, LANES), pos),
            pl.BlockSpec((ts, LANES), pos),
        ],
        out_specs=[
            pl.BlockSpec((ts, qw), tok),
            pl.BlockSpec((ts, kw), tok),
            pl.BlockSpec((ts, kw), tok),
            pl.BlockSpec((ts, lru_w), tok),
            pl.BlockSpec((ts, lru_w), tok),
        ],
        out_shape=[
            jax.ShapeDtypeStruct((T, qw), BF16),
            jax.ShapeDtypeStruct((T, kw), BF16),
            jax.ShapeDtypeStruct((T, kw), BF16),
            jax.ShapeDtypeStruct((T, lru_w), F32),
            jax.ShapeDtypeStruct((T, lru_w), F32),
        ],
        compiler_params=_cparams(("parallel",)),
        name="inproj",
    )(x2, norm1_g.reshape(1, D), w_all, qg, kg, cos, sin)


def _attn_kernel(q_ref, k_ref, v_ref, o_ref, acc_ref, m_ref, *, tq, tk, n_kv):
    rows = GQA_GROUP * tq
    q4 = jnp.concatenate([q_ref[0, :, g * LANES:(g + 1) * LANES] for g in range(GQA_GROUP)], axis=0)
    acc_ref[...] = jnp.zeros((rows, LANES), F32)
    m_ref[...] = jnp.full((rows, LANES), -jnp.inf, F32)

    def body(j, carry):
        off = pl.multiple_of(j * tk, tk)
        kt = k_ref[0, pl.ds(off, tk), :]
        vt = v_ref[0, pl.ds(off, tk), :]
        s = lax.dot_general(q4, kt, (((1,), (1,)), ((), ())), preferred_element_type=F32)
        m_prev = m_ref[:, 0:1]
        m_new = jnp.maximum(m_prev, jnp.max(s, axis=1, keepdims=True))
        alpha = jnp.exp(m_prev - m_new)
        p = jnp.exp(s - m_new)
        acc_ref[...] = alpha * acc_ref[...] + jnp.dot(p.astype(BF16), vt, preferred_element_type=F32)
        m_ref[...] = jnp.broadcast_to(m_new, (rows, LANES))
        return carry

    lax.fori_loop(0, n_kv, body, 0)
    acc = acc_ref[...]
    denom = pltpu.roll(acc, HEAD_DIM, 1)
    lane = lax.broadcasted_iota(jnp.int32, acc.shape, 1)
    o = jnp.where(lane < HEAD_DIM, acc / denom, 0.0)
    for g in range(GQA_GROUP):
        o_ref[0, :, g * LANES:(g + 1) * LANES] = o[g * tq:(g + 1) * tq].astype(BF16)


def _attention(q, k, v, B, S):
    tq = min(TQ, S)
    tk = min(TK, S)
    gw = GQA_GROUP * LANES
    return pl.pallas_call(
        functools.partial(_attn_kernel, tq=tq, tk=tk, n_kv=S // tk),
        grid=(B, N_KV_HEADS, S // tq),
        in_specs=[
            pl.BlockSpec((1, tq, gw), lambda b, h, i: (b, i, h)),
            pl.BlockSpec((1, S, LANES), lambda b, h, i: (b, 0, h)),
            pl.BlockSpec((1, S, LANES), lambda b, h, i: (b, 0, h)),
        ],
        out_specs=pl.BlockSpec((1, tq, gw), lambda b, h, i: (b, i, h)),
        out_shape=jax.ShapeDtypeStruct(q.shape, BF16),
        scratch_shapes=[
            pltpu.VMEM((GQA_GROUP * tq, LANES), F32),
            pltpu.VMEM((GQA_GROUP * tq, LANES), F32),
        ],
        compiler_params=_cparams(("parallel", "parallel", "parallel")),
        name="attention",
    )(q, k, v)


def _scan_chunk(a, b, reverse):
    n = a.shape[0]
    row = lax.broadcasted_iota(jnp.int32, a.shape, 0)
    d = 1
    while d < n:
        if reverse:
            keep = row < n - d
            shift = n - d
        else:
            keep = row >= d
            shift = d
        a_sh = jnp.where(keep, pltpu.roll(a, shift, 0), 1.0)
        b_sh = jnp.where(keep, pltpu.roll(b, shift, 0), 0.0)
        b = a * b_sh + b
        a = a * a_sh
        d *= 2
    return a, b


def _lru_kernel(u_ref, gate_ref, cw_ref, cb_ref, w_ref, bias_ref, lam_ref, o_ref,
                up_ref, hf_ref, *, S, tc):
    halo = SUBLANES
    zeros = jnp.zeros((halo, LANES), F32)
    up_ref[0:halo, :] = zeros
    up_ref[S + halo:S + 2 * halo, :] = zeros
    up_ref[halo:S + halo, :] = u_ref[0]
    sp = jax.nn.softplus(-lam_ref[...])
    cw = cw_ref[...]
    cb = cb_ref[...]
    n_chunks = S // tc
    ext = tc + 2 * halo

    def gates(c, d):
        t0 = pl.multiple_of(c * tc, tc)
        ue = up_ref[pl.ds(t0, ext), :]
        xc = cb
        for j in range(CONV_W):
            sh = (CONV_PAD_L - j) % ext
            uj = ue if sh == 0 else pltpu.roll(ue, sh, 0)
            xc = xc + uj[halo:halo + tc] * cw[j:j + 1, :]
        gw = 2 * LANES
        g = jnp.dot(xc.astype(BF16), w_ref[0, :, d * gw:(d + 1) * gw],
                    preferred_element_type=F32) + bias_ref[0, :, d * gw:(d + 1) * gw]
        r = jax.nn.sigmoid(g[:, :LANES])
        i = jax.nn.sigmoid(g[:, LANES:])
        log_a = -LRU_C * r * sp[d:d + 1, :]
        a = jnp.exp(log_a)
        b = jnp.sqrt(1.0 - jnp.exp(2.0 * log_a)) * i * xc
        return t0, a, b

    def fwd(c, h):
        t0, a, b = gates(c, 0)
        pa, hb = _scan_chunk(a, b, False)
        hc = hb + pa * h
        hf_ref[pl.ds(t0, tc), :] = hc
        return hc[tc - 1:tc, :]

    lax.fori_loop(0, n_chunks, fwd, jnp.zeros((1, LANES), F32))

    def bwd(ci, h):
        t0, a, b = gates(n_chunks - 1 - ci, 1)
        pa, hb = _scan_chunk(a, b, True)
        hc = hb + pa * h
        gate = gate_ref[0, pl.ds(t0, tc), :]
        o_ref[0, pl.ds(t0, tc), :] = (hf_ref[pl.ds(t0, tc), :] + hc) * jax.nn.gelu(gate)
        return hc[0:1, :]

    lax.fori_loop(0, n_chunks, bwd, jnp.zeros((1, LANES), F32))


def _block_diag_pairs(w):
    nb, bw, _ = w.shape
    w = w.reshape(nb // 2, 2, bw, bw)
    z = jnp.zeros_like(w[:, 0])
    top = jnp.concatenate([w[:, 0], z], axis=-1)
    bot = jnp.concatenate([z, w[:, 1]], axis=-1)
    return jnp.concatenate([top, bot], axis=-2)


def _lru(lru_x, lru_gate, conv_w, conv_b, wa, ba, wi, bi, lam, B, S):
    C = lru_x.shape[-1]
    nc = C // LANES
    tc = min(TC_LRU, S)
    w = jnp.concatenate([_block_diag_pairs(wa[0]), _block_diag_pairs(wi[0]),
                         _block_diag_pairs(wa[1]), _block_diag_pairs(wi[1])], axis=-1).astype(BF16)
    bias = jnp.stack([ba[0].reshape(nc, LANES), bi[0].reshape(nc, LANES),
                      ba[1].reshape(nc, LANES), bi[1].reshape(nc, LANES)], axis=1)
    bias = bias.reshape(nc, 1, 4 * LANES)
    blk = lambda b, c: (b, 0, c)
    return pl.pallas_call(
        functools.partial(_lru_kernel, S=S, tc=tc),
        grid=(B, nc),
        in_specs=[
            pl.BlockSpec((1, S, LANES), blk),
            pl.BlockSpec((1, S, LANES), blk),
            pl.BlockSpec((CONV_W, LANES), lambda b, c: (0, c)),
            pl.BlockSpec((1, LANES), lambda b, c: (0, c)),
            pl.BlockSpec((1, LANES, 4 * LANES), lambda b, c: (c, 0, 0)),
            pl.BlockSpec((1, 1, 4 * LANES), lambda b, c: (c, 0, 0)),
            pl.BlockSpec((2, LANES), lambda b, c: (0, c)),
        ],
        out_specs=pl.BlockSpec((1, S, LANES), blk),
        out_shape=jax.ShapeDtypeStruct((B, S, C), F32),
        scratch_shapes=[
            pltpu.VMEM((S + 2 * SUBLANES, LANES), F32),
            pltpu.VMEM((S, LANES), F32),
        ],
        compiler_params=_cparams(("parallel", "parallel")),
        name="rglru",
    )(lru_x, lru_gate, conv_w, conv_b.reshape(1, C), w, bias, lam)


def _outproj_kernel(a_ref, l_ref, x_ref, ag_ref, lg_ref, wa_ref, wl_ref, g2_ref,
                    wrh_ref, wrl_ref, br_ref, tri_ref,
                    x1_ref, xn3_ref, route_ref, gates_ref, cnt_ref, carry_ref, *, attn_w, lru_w):
    step = pl.program_id(0)

    @pl.when(step == 0)
    def _():
        carry_ref[...] = jnp.zeros_like(carry_ref)

    a = a_ref[...].astype(F32)
    ams = jnp.sum(a * a, axis=-1, keepdims=True) * (1.0 / attn_w)
    an = a * lax.rsqrt(ams + NORM_EPS) * ag_ref[...]
    l = l_ref[...]
    lms = jnp.sum(l * l, axis=-1, keepdims=True) * (1.0 / lru_w)
    ln = l * lax.rsqrt(lms + NORM_EPS) * lg_ref[...]
    mix = (jnp.dot(an.astype(BF16), wa_ref[...], preferred_element_type=F32)
           + jnp.dot(ln.astype(BF16), wl_ref[...], preferred_element_type=F32))
    x1 = x_ref[...] + mix
    x1_ref[...] = x1
    ms = jnp.mean(x1 * x1, axis=-1, keepdims=True)
    xn = x1 * lax.rsqrt(ms + NORM_EPS) * g2_ref[...]
    for s in range(SUBLANES):
        xn3_ref[:, s, :] = xn[:, s * LANES:(s + 1) * LANES]

    hi = xn.astype(BF16)
    lo = (xn - hi.astype(F32)).astype(BF16)
    logits = (jnp.dot(hi, wrh_ref[...], preferred_element_type=F32)
              + jnp.dot(lo, wrh_ref[...], preferred_element_type=F32)
              + jnp.dot(hi, wrl_ref[...], preferred_element_type=F32)) + br_ref[...]
    lane = lax.broadcasted_iota(jnp.int32, logits.shape, 1)
    neg = -jnp.inf
    work = jnp.where(lane < N_EXPERTS, logits, neg)
    sel = jnp.zeros(logits.shape, F32)
    idxs, vals = [], []
    for _ in range(TOP_K):
        m = jnp.max(work, axis=1, keepdims=True)
        idx = jnp.min(jnp.where(work == m, lane, LANES), axis=1, keepdims=True)
        hit = lane == idx
        work = jnp.where(hit, neg, work)
        sel = sel + hit.astype(F32)
        idxs.append(idx)
        vals.append(m)
    es = [jnp.exp(v - vals[0]) for v in vals]
    den = es[0] + es[1] + es[2] + es[3]

    prefix = jnp.dot(tri_ref[...], sel.astype(BF16), preferred_element_type=F32) + carry_ref[...]
    carry_ref[...] = carry_ref[...] + jnp.sum(sel, axis=0, keepdims=True)
    cnt_ref[...] = carry_ref[...]

    route = jnp.zeros(logits.shape, jnp.int32)
    gates = jnp.zeros(logits.shape, F32)
    for k in range(TOP_K):
        rank = jnp.sum(jnp.where(lane == idxs[k], prefix, 0.0), axis=1, keepdims=True).astype(jnp.int32)
        route = jnp.where(lane == k, idxs[k], route)
        route = jnp.where(lane == TOP_K + k, rank, route)
        gates = jnp.where(lane == k, es[k] / den, gates)
    route_ref[...] = route
    gates_ref[...] = gates


def _outproj_router(attn, lru, x2, attn_out_g, lru_out_g, w_out, norm2_g, w_router, b_router):
    T, D = x2.shape
    lru_w = lru.shape[-1]
    ts = min(TS_OUT, T)
    wa = w_out[:ATTN_W].reshape(N_Q_HEADS, HEAD_DIM, D)
    wa = jnp.pad(wa, ((0, 0), (0, LANES - HEAD_DIM), (0, 0))).reshape(N_Q_HEADS * LANES, D).astype(BF16)
    wl = w_out[ATTN_W:].astype(BF16)
    ag = _pad_heads(attn_out_g.reshape(1, ATTN_W), N_Q_HEADS)
    wr = jnp.pad(w_router, ((0, 0), (0, LANES - N_EXPERTS)))
    wrh = wr.astype(BF16)
    wrl = (wr - wrh.astype(F32)).astype(BF16)
    br = jnp.pad(b_router.reshape(1, N_EXPERTS), ((0, 0), (0, LANES - N_EXPERTS)))
    tri = (jnp.arange(ts)[:, None] > jnp.arange(ts)[None, :]).astype(BF16)
    const = lambda i: (0, 0)
    tok = lambda i: (i, 0)
    aw = N_Q_HEADS * LANES
    return pl.pallas_call(
        functools.partial(_outproj_kernel, attn_w=ATTN_W, lru_w=lru_w),
        grid=(T // ts,),
        in_specs=[
            pl.BlockSpec((ts, aw), tok),
            pl.BlockSpec((ts, lru_w), tok),
            pl.BlockSpec((ts, D), tok),
            pl.BlockSpec((1, aw), const),
            pl.BlockSpec((1, lru_w), const),
            pl.BlockSpec((aw, D), const),
            pl.BlockSpec((lru_w, D), const),
            pl.BlockSpec((1, D), const),
            pl.BlockSpec((D, LANES), const),
            pl.BlockSpec((D, LANES), const),
            pl.BlockSpec((1, LANES), const),
            pl.BlockSpec((ts, ts), const),
        ],
        out_specs=[
            pl.BlockSpec((ts, D), tok),
            pl.BlockSpec((ts, SUBLANES, D // SUBLANES), lambda i: (i, 0, 0)),
            pl.BlockSpec((ts, LANES), tok),
            pl.BlockSpec((ts, LANES), tok),
            pl.BlockSpec((1, LANES), const),
        ],
        out_shape=[
            jax.ShapeDtypeStruct((T, D), F32),
            jax.ShapeDtypeStruct((T, SUBLANES, D // SUBLANES), F32),
            jax.ShapeDtypeStruct((T, LANES), jnp.int32),
            jax.ShapeDtypeStruct((T, LANES), F32),
            jax.ShapeDtypeStruct((1, LANES), F32),
        ],
        scratch_shapes=[pltpu.VMEM((1, LANES), F32)],
        compiler_params=_cparams(("arbitrary",)),
        name="outproj_router",
    )(attn, lru, x2, ag, lru_out_g.reshape(1, lru_w), wa, wl, norm2_g.reshape(1, D),
      wrh, wrl, br, tri)


def _dispatch_kernel(dest_ref, xn_hbm, init_hbm, out_hbm, sem, *, ts):
    del init_hbm
    base = pl.program_id(0) * ts

    def issue(r, carry):
        for k in range(TOP_K):
            d = dest_ref[r * TOP_K + k]
            pltpu.make_async_copy(xn_hbm.at[base + r], out_hbm.at[d], sem).start()
        return carry

    lax.fori_loop(0, ts, issue, 0)
    n = ts * TOP_K
    pltpu.make_async_copy(xn_hbm.at[pl.ds(0, n)], out_hbm.at[pl.ds(0, n)], sem).wait()


def _dispatch(xn3, dest_flat, n_rows):
    T = xn3.shape[0]
    ts = min(TS_DISP, T)
    init = jnp.zeros((n_rows,) + xn3.shape[1:], xn3.dtype)
    return pl.pallas_call(
        functools.partial(_dispatch_kernel, ts=ts),
        grid=(T // ts,),
        in_specs=[
            pl.BlockSpec((ts * TOP_K,), lambda i: (i,), memory_space=pltpu.SMEM),
            pl.BlockSpec(memory_space=pl.ANY),
            pl.BlockSpec(memory_space=pl.ANY),
        ],
        out_specs=pl.BlockSpec(memory_space=pl.ANY),
        out_shape=jax.ShapeDtypeStruct(init.shape, init.dtype),
        scratch_shapes=[pltpu.SemaphoreType.DMA],
        input_output_aliases={2: 0},
        compiler_params=_cparams(("arbitrary",)),
        name="dispatch",
    )(dest_flat, xn3, init)


def _expert_kernel(be_ref, na_ref, x_ref, wg_ref, bg_ref, wu_ref, bu_ref, wd_ref, bd_ref,
                   y_ref, wgb_ref, wub_ref, wdb_ref, prev_ref):
    i = pl.program_id(0)
    e = be_ref[i]

    @pl.when(i == 0)
    def _():
        prev_ref[0] = -1

    active = i < na_ref[0]

    @pl.when(jnp.logical_and(active, e != prev_ref[0]))
    def _():
        wgb_ref[...] = wg_ref[0].astype(BF16)
        wub_ref[...] = wu_ref[0].astype(BF16)
        wdb_ref[...] = wd_ref[0].astype(BF16)
        prev_ref[0] = e

    @pl.when(active)
    def _():
        x = jnp.concatenate([x_ref[:, s, :] for s in range(SUBLANES)], axis=1).astype(BF16)
        g = jnp.dot(x, wgb_ref[...], preferred_element_type=F32) + bg_ref[0]
        u = jnp.dot(x, wub_ref[...], preferred_element_type=F32) + bu_ref[0]
        g = jnp.minimum(g, SWIGLU_LIMIT)
        u = jnp.clip(u, -SWIGLU_LIMIT, SWIGLU_LIMIT)
        glu = g * jax.nn.sigmoid(SWIGLU_ALPHA * g)
        y = jnp.dot(((u + 1.0) * glu).astype(BF16), wdb_ref[...], preferred_element_type=F32) + bd_ref[0]
        for s in range(SUBLANES):
            y_ref[:, s, :] = y[:, s * LANES:(s + 1) * LANES]


def _experts(x_rows, block_e, n_active, w_gate, b_gate, w_up, b_up, w_down, b_down):
    n_rows = x_rows.shape[0]
    E, D, FF = w_gate.shape
    n_blocks = n_rows // ROW_BLOCK

    def row_map(i, be, na):
        return (jnp.minimum(i, na[0] - 1), 0, 0)

    def w_map(i, be, na):
        return (be[jnp.minimum(i, na[0] - 1)], 0, 0)

    grid_spec = pltpu.PrefetchScalarGridSpec(
        num_scalar_prefetch=2,
        grid=(n_blocks,),
        in_specs=[
            pl.BlockSpec((ROW_BLOCK,) + x_rows.shape[1:], row_map),
            pl.BlockSpec((1, D, FF), w_map),
            pl.BlockSpec((1, 1, FF), w_map),
            pl.BlockSpec((1, D, FF), w_map),
            pl.BlockSpec((1, 1, FF), w_map),
            pl.BlockSpec((1, FF, D), w_map),
            pl.BlockSpec((1, 1, D), w_map),
        ],
        out_specs=pl.BlockSpec((ROW_BLOCK,) + x_rows.shape[1:], row_map),
        scratch_shapes=[
            pltpu.VMEM((D, FF), BF16),
            pltpu.VMEM((D, FF), BF16),
            pltpu.VMEM((FF, D), BF16),
            pltpu.SMEM((1,), jnp.int32),
        ],
    )
    return pl.pallas_call(
        _expert_kernel,
        grid_spec=grid_spec,
        out_shape=jax.ShapeDtypeStruct(x_rows.shape, F32),
        input_output_aliases={2: 0},
        compiler_params=_cparams(("arbitrary",)),
        name="experts",
    )(block_e, n_active, x_rows, w_gate, b_gate.reshape(E, 1, FF), w_up, b_up.reshape(E, 1, FF),
      w_down, b_down.reshape(E, 1, D))


def _combine_kernel(dest_ref, y_hbm, x1_ref, gates_ref, fg_ref, o_ref,
                    b0, b1, b2, b3, sem, *, ts):
    bufs = (b0, b1, b2, b3)

    def issue(r, carry):
        for k in range(TOP_K):
            d = dest_ref[r * TOP_K + k]
            pltpu.make_async_copy(y_hbm.at[d], bufs[k].at[r], sem).start()
        return carry

    lax.fori_loop(0, ts, issue, 0)
    for k in range(TOP_K):
        pltpu.make_async_copy(y_hbm.at[pl.ds(0, ts)], bufs[k], sem).wait()

    acc = x1_ref[...]
    gates = gates_ref[...]
    for k in range(TOP_K):
        yk = jnp.concatenate([bufs[k][:, s, :] for s in range(SUBLANES)], axis=1)
        acc = acc + yk * gates[:, k:k + 1]
    ms = jnp.mean(acc * acc, axis=-1, keepdims=True)
    o_ref[...] = acc * lax.rsqrt(ms + NORM_EPS) * fg_ref[...]


def _combine(y_rows, dest_flat, x1, gates, final_g):
    T, D = x1.shape
    ts = min(TS_COMB, T)
    tok = lambda i: (i, 0)
    row_shape = (ts,) + y_rows.shape[1:]
    return pl.pallas_call(
        functools.partial(_combine_kernel, ts=ts),
        grid=(T // ts,),
        in_specs=[
            pl.BlockSpec((ts * TOP_K,), lambda i: (i,), memory_space=pltpu.SMEM),
            pl.BlockSpec(memory_space=pl.ANY),
            pl.BlockSpec((ts, D), tok),
            pl.BlockSpec((ts, LANES), tok),
            pl.BlockSpec((1, D), lambda i: (0, 0)),
        ],
        out_specs=pl.BlockSpec((ts, D), tok),
        out_shape=jax.ShapeDtypeStruct((T, D), F32),
        scratch_shapes=[pltpu.VMEM(row_shape, F32) for _ in range(TOP_K)] + [pltpu.SemaphoreType.DMA],
        compiler_params=_cparams(("arbitrary",)),
        name="combine",
    )(dest_flat, y_rows, x1, gates, final_g.reshape(1, D))


def kernel(x, norm1_g, w_in, q_norm_g, k_norm_g, conv_w, conv_b, lru_wa, lru_ba, lru_wi, lru_bi,
           lru_lam, attn_out_g, lru_out_g, w_out, norm2_g, w_router, b_router, w_gate, b_gate,
           w_up, b_up, w_down, b_down, final_g):
    B, S, D = x.shape
    T = B * S
    assert w_in.shape[0] == 1, "single-layer trunk: the final norm is fused into the layer's combine"
    x2 = x.reshape(T, D)
    for l in range(1):
        q, k, v, lru_x, lru_gate = _inproj(x2, norm1_g[l], w_in[l], q_norm_g[l], k_norm_g[l], S)
        attn = _attention(q.reshape(B, S, -1), k.reshape(B, S, -1), v.reshape(B, S, -1), B, S)
        lru = _lru(lru_x.reshape(B, S, -1), lru_gate.reshape(B, S, -1), conv_w[l], conv_b[l],
                   lru_wa[l], lru_ba[l], lru_wi[l], lru_bi[l], lru_lam[l], B, S)
        x1, xn3, route, gates, cnt = _outproj_router(
            attn.reshape(T, -1), lru.reshape(T, -1), x2, attn_out_g[l], lru_out_g[l], w_out[l],
            norm2_g[l], w_router[l], b_router[l])

        idx = route[:, :TOP_K]
        rank = route[:, TOP_K:2 * TOP_K]
        counts = cnt[0, :N_EXPERTS].astype(jnp.int32)
        padded = ((counts + ROW_BLOCK - 1) // ROW_BLOCK) * ROW_BLOCK
        pend = jnp.cumsum(padded)
        pstart = pend - padded
        dest = (pstart[idx] + rank).reshape(T * TOP_K).astype(jnp.int32)
        n_rows = T * TOP_K + N_EXPERTS * ROW_BLOCK
        n_blocks = n_rows // ROW_BLOCK
        block_e = jnp.searchsorted(pend, jnp.arange(n_blocks, dtype=jnp.int32) * ROW_BLOCK, side='right')
        block_e = jnp.minimum(block_e, N_EXPERTS - 1).astype(jnp.int32)
        n_active = (pend[-1:] // ROW_BLOCK).astype(jnp.int32)

        x_rows = _dispatch(xn3, dest, n_rows)
        y_rows = _experts(x_rows, block_e, n_active, w_gate[l], b_gate[l], w_up[l], b_up[l],
                          w_down[l], b_down[l])
        x2 = _combine(y_rows, dest, x1, gates, final_g)
    return x2.reshape(B, S, D)
```

```python
import functools

import jax
import jax.numpy as jnp
from jax import lax
from jax.experimental import pallas as pl
from jax.experimental.pallas import tpu as pltpu

F32 = jnp.float32
BF16 = jnp.bfloat16

GRID_W = 64
HEAD_DIM = 64
N_Q_HEADS = 8
N_KV_HEADS = 2
GQA_GROUP = N_Q_HEADS // N_KV_HEADS
ATTN_W = N_Q_HEADS * HEAD_DIM
KV_W = N_KV_HEADS * HEAD_DIM
LRU_BLOCKS = 8
LRU_C = 8.0
CONV_W = 4
CONV_PAD_L = 2
ROPE_THETA = 10000.0
ROPE_HALF = HEAD_DIM // 2
ROPE_M = ROPE_HALF // 2
N_EXPERTS = 32
TOP_K = 4
SWIGLU_ALPHA = 1.702
SWIGLU_LIMIT = 7.0
NORM_EPS = 1e-5
QK_EPS = 1e-6

LANES = 128
SUBLANES = 8
VMEM_LIMIT = 48 * 1024 * 1024

TS_IN = 512
TQ = 256
TK = 512
TC_LRU = 512
TS_OUT = 512
ROW_BLOCK = 256
TS_DISP = 512
TS_COMB = 256


def _cparams(sem):
    return pltpu.CompilerParams(dimension_semantics=sem, vmem_limit_bytes=VMEM_LIMIT)


def _inproj_kernel(x_ref, g1_ref, w_ref, qg_ref, kg_ref, cos_ref, sin_ref,
                   q_ref, k_ref, v_ref, lx_ref, lg_ref, *, lru_w):
    x = x_ref[...]
    ms = jnp.mean(x * x, axis=-1, keepdims=True)
    xn = x * lax.rsqrt(ms + NORM_EPS) * g1_ref[...]
    h = jnp.dot(xn.astype(BF16), w_ref[...], preferred_element_type=F32)

    cos = cos_ref[...]
    sin = sin_ref[...]
    lane = lax.broadcasted_iota(jnp.int32, cos.shape, 1)
    first_half = (lane % ROPE_HALF) < ROPE_M

    def head_norm_rope(xc, g, scale):
        hms = jnp.sum(xc * xc, axis=-1, keepdims=True) * (1.0 / HEAD_DIM)
        xc = xc * lax.rsqrt(hms + QK_EPS) * g
        partner = jnp.where(first_half,
                            pltpu.roll(xc, LANES - ROPE_M, 1),
                            pltpu.roll(xc, ROPE_M, 1))
        return (xc * cos + partner * sin) * scale

    qw = N_Q_HEADS * LANES
    kw = N_KV_HEADS * LANES
    for c in range(N_Q_HEADS):
        sl = slice(c * LANES, (c + 1) * LANES)
        q_ref[:, sl] = head_norm_rope(h[:, sl], qg_ref[...], HEAD_DIM ** -0.5).astype(BF16)
    for c in range(N_KV_HEADS):
        sl = slice(c * LANES, (c + 1) * LANES)
        k_ref[:, sl] = head_norm_rope(h[:, qw + c * LANES: qw + (c + 1) * LANES],
                                      kg_ref[...], 1.0).astype(BF16)
    v = h[:, qw + kw: qw + 2 * kw]
    vlane = lax.broadcasted_iota(jnp.int32, v.shape, 1)
    v_ref[...] = jnp.where((vlane % LANES) >= HEAD_DIM, 1.0, v).astype(BF16)
    o = qw + 2 * kw
    lx_ref[...] = h[:, o: o + lru_w]
    lg_ref[...] = h[:, o + lru_w: o + 2 * lru_w]


def _pad_heads(w, n_heads):
    lead = w.shape[:-1]
    w = w.reshape(lead + (n_heads, HEAD_DIM))
    w = jnp.pad(w, [(0, 0)] * len(lead) + [(0, 0), (0, LANES - HEAD_DIM)])
    return w.reshape(lead + (n_heads * LANES,))


def _rope_tables(S):
    t = jnp.arange(S)
    rows = (t // GRID_W).astype(F32)
    cols = (t % GRID_W).astype(F32)
    inv_freq = ROPE_THETA ** (-jnp.arange(ROPE_M, dtype=F32) / ROPE_M)
    ar = rows[:, None] * inv_freq[None, :]
    ac = cols[:, None] * inv_freq[None, :]
    cos = jnp.concatenate([jnp.cos(ar), jnp.cos(ar), jnp.cos(ac), jnp.cos(ac)], axis=-1)
    sin = jnp.concatenate([-jnp.sin(ar), jnp.sin(ar), -jnp.sin(ac), jnp.sin(ac)], axis=-1)
    pad = [(0, 0), (0, LANES - HEAD_DIM)]
    return jnp.pad(cos, pad), jnp.pad(sin, pad)


def _inproj(x2, norm1_g, w_in, q_norm_g, k_norm_g, S):
    T, D = x2.shape
    lru_w = (w_in.shape[1] - ATTN_W - 2 * KV_W) // 2
    o0, o1, o2 = ATTN_W, ATTN_W + KV_W, ATTN_W + 2 * KV_W
    w_all = jnp.concatenate([
        _pad_heads(w_in[:, :o0], N_Q_HEADS),
        _pad_heads(w_in[:, o0:o1], N_KV_HEADS),
        _pad_heads(w_in[:, o1:o2], N_KV_HEADS),
        w_in[:, o2:],
    ], axis=1).astype(BF16)
    qg = _pad_heads(q_norm_g.reshape(1, HEAD_DIM), 1)
    kg = _pad_heads(k_norm_g.reshape(1, HEAD_DIM), 1)
    cos, sin = _rope_tables(S)
    ts = TS_IN
    n_s = S // ts
    qw, kw = N_Q_HEADS * LANES, N_KV_HEADS * LANES
    const = lambda i: (0, 0)
    tok = lambda i: (i, 0)
    pos = lambda i: (i % n_s, 0)
    return pl.pallas_call(
        functools.partial(_inproj_kernel, lru_w=lru_w),
        grid=(T // ts,),
        in_specs=[
            pl.BlockSpec((ts, D), tok),
            pl.BlockSpec((1, D), const),
            pl.BlockSpec(w_all.shape, const),
            pl.BlockSpec((1, LANES), const),
            pl.BlockSpec((1, LANES), const),
            pl.BlockSpec((ts, LANES), pos),
            pl.BlockSpec((ts, LANES), pos),
        ],
        out_specs=[
            pl.BlockSpec((ts, qw), tok),
            pl.BlockSpec((ts, kw), tok),
            pl.BlockSpec((ts, kw), tok),
            pl.BlockSpec((ts, lru_w), tok),
            pl.BlockSpec((ts, lru_w), tok),
        ],
        out_shape=[
            jax.ShapeDtypeStruct((T, qw), BF16),
            jax.ShapeDtypeStruct((T, kw), BF16),
            jax.ShapeDtypeStruct((T, kw), BF16),
            jax.ShapeDtypeStruct((T, lru_w), F32),
            jax.ShapeDtypeStruct((T, lru_w), F32),
        ],
        compiler_params=_cparams(("parallel",)),
        name="inproj",
    )(x2, norm1_g.reshape(1, D), w_all, qg, kg, cos, sin)


def _attn_kernel(q_ref, k_ref, v_ref, o_ref, acc_ref, m_ref, *, tq, tk, n_kv):
    rows = GQA_GROUP * tq
    q4 = jnp.concatenate([q_ref[0, :, g * LANES:(g + 1) * LANES] for g in range(GQA_GROUP)], axis=0)
    acc_ref[...] = jnp.zeros((rows, LANES), F32)
    m_ref[...] = jnp.full((rows, LANES), -jnp.inf, F32)

    def body(j, carry):
        off = pl.multiple_of(j * tk, tk)
        kt = k_ref[0, pl.ds(off, tk), :]
        vt = v_ref[0, pl.ds(off, tk), :]
        s = lax.dot_general(q4, kt, (((1,), (1,)), ((), ())), preferred_element_type=F32)
        m_prev = m_ref[:, 0:1]
        m_new = jnp.maximum(m_prev, jnp.max(s, axis=1, keepdims=True))
        alpha = jnp.exp(m_prev - m_new)
        p = jnp.exp(s - m_new)
        acc_ref[...] = alpha * acc_ref[...] + jnp.dot(p.astype(BF16), vt, preferred_element_type=F32)
        m_ref[...] = jnp.broadcast_to(m_new, (rows, LANES))
        return carry

    lax.fori_loop(0, n_kv, body, 0)
    acc = acc_ref[...]
    denom = pltpu.roll(acc, HEAD_DIM, 1)
    lane = lax.broadcasted_iota(jnp.int32, acc.shape, 1)
    o = jnp.where(lane < HEAD_DIM, acc / denom, 0.0)
    for g in range(GQA_GROUP):
        o_ref[0, :, g * LANES:(g + 1) * LANES] = o[g * tq:(g + 1) * tq].astype(BF16)


def _attention(q, k, v, B, S):
    tq = min(TQ, S)
    tk = min(TK, S)
    gw = GQA_GROUP * LANES
    return pl.pallas_call(
        functools.partial(_attn_kernel, tq=tq, tk=tk, n_kv=S // tk),
        grid=(B, N_KV_HEADS, S // tq),
        in_specs=[
            pl.BlockSpec((1, tq, gw), lambda b, h, i: (b, i, h)),
            pl.BlockSpec((1, S, LANES), lambda b, h, i: (b, 0, h)),
            pl.BlockSpec((1, S, LANES), lambda b, h, i: (b, 0, h)),
        ],
        out_specs=pl.BlockSpec((1, tq, gw), lambda b, h, i: (b, i, h)),
        out_shape=jax.ShapeDtypeStruct(q.shape, BF16),
        scratch_shapes=[
            pltpu.VMEM((GQA_GROUP * tq, LANES), F32),
            pltpu.VMEM((GQA_GROUP * tq, LANES), F32),
        ],
        compiler_params=_cparams(("parallel", "parallel", "parallel")),
        name="attention",
    )(q, k, v)


def _scan_chunk(a, b, reverse):
    n = a.shape[0]
    row = lax.broadcasted_iota(jnp.int32, a.shape, 0)
    d = 1
    while d < n:
        if reverse:
            keep = row < n - d
            shift = n - d
        else:
            keep = row >= d
            shift = d
        a_sh = jnp.where(keep, pltpu.roll(a, shift, 0), 1.0)
        b_sh = jnp.where(keep, pltpu.roll(b, shift, 0), 0.0)
        b = a * b_sh + b
        a = a * a_sh
        d *= 2
    return a, b


def _lru_kernel(u_ref, gate_ref, cw_ref, cb_ref, w_ref, bias_ref, lam_ref, o_ref,
                up_ref, hf_ref, *, S, tc):
    halo = SUBLANES
    zeros = jnp.zeros((halo, LANES), F32)
    up_ref[0:halo, :] = zeros
    up_ref[S + halo:S + 2 * halo, :] = zeros
    up_ref[halo:S + halo, :] = u_ref[0]
    sp = jax.nn.softplus(-lam_ref[...])
    cw = cw_ref[...]
    cb = cb_ref[...]
    n_chunks = S // tc
    ext = tc + 2 * halo

    def gates(c, d):
        t0 = pl.multiple_of(c * tc, tc)
        ue = up_ref[pl.ds(t0, ext), :]
        xc = cb
        for j in range(CONV_W):
            sh = (CONV_PAD_L - j) % ext
            uj = ue if sh == 0 else pltpu.roll(ue, sh, 0)
            xc = xc + uj[halo:halo + tc] * cw[j:j + 1, :]
        gw = 2 * LANES
        g = jnp.dot(xc.astype(BF16), w_ref[0, :, d * gw:(d + 1) * gw],
                    preferred_element_type=F32) + bias_ref[0, :, d * gw:(d + 1) * gw]
        r = jax.nn.sigmoid(g[:, :LANES])
        i = jax.nn.sigmoid(g[:, LANES:])
        log_a = -LRU_C * r * sp[d:d + 1, :]
        a = jnp.exp(log_a)
        b = jnp.sqrt(1.0 - jnp.exp(2.0 * log_a)) * i * xc
        return t0, a, b

    def fwd(c, h):
        t0, a, b = gates(c, 0)
        pa, hb = _scan_chunk(a, b, False)
        hc = hb + pa * h
        hf_ref[pl.ds(t0, tc), :] = hc
        return hc[tc - 1:tc, :]

    lax.fori_loop(0, n_chunks, fwd, jnp.zeros((1, LANES), F32))

    def bwd(ci, h):
        t0, a, b = gates(n_chunks - 1 - ci, 1)
        pa, hb = _scan_chunk(a, b, True)
        hc = hb + pa * h
        gate = gate_ref[0, pl.ds(t0, tc), :]
        o_ref[0, pl.ds(t0, tc), :] = (hf_ref[pl.ds(t0, tc), :] + hc) * jax.nn.gelu(gate)
        return hc[0:1, :]

    lax.fori_loop(0, n_chunks, bwd, jnp.zeros((1, LANES), F32))


def _block_diag_pairs(w):
    nb, bw, _ = w.shape
    w = w.reshape(nb // 2, 2, bw, bw)
    z = jnp.zeros_like(w[:, 0])
    top = jnp.concatenate([w[:, 0], z], axis=-1)
    bot = jnp.concatenate([z, w[:, 1]], axis=-1)
    return jnp.concatenate([top, bot], axis=-2)


def _lru(lru_x, lru_gate, conv_w, conv_b, wa, ba, wi, bi, lam, B, S):
    C = lru_x.shape[-1]
    nc = C // LANES
    tc = min(TC_LRU, S)
    w = jnp.concatenate([_block_diag_pairs(wa[0]), _block_diag_pairs(wi[0]),
                         _block_diag_pairs(wa[1]), _block_diag_pairs(wi[1])], axis=-1).astype(BF16)
    bias = jnp.stack([ba[0].reshape(nc, LANES), bi[0].reshape(nc, LANES),
                      ba[1].reshape(nc, LANES), bi[1].reshape(nc, LANES)], axis=1)
    bias = bias.reshape(nc, 1, 4 * LANES)
    blk = lambda b, c: (b, 0, c)
    return pl.pallas_call(
        functools.partial(_lru_kernel, S=S, tc=tc),
        grid=(B, nc),
        in_specs=[
            pl.BlockSpec((1, S, LANES), blk),
            pl.BlockSpec((1, S, LANES), blk),
            pl.BlockSpec((CONV_W, LANES), lambda b, c: (0, c)),
            pl.BlockSpec((1, LANES), lambda b, c: (0, c)),
            pl.BlockSpec((1, LANES, 4 * LANES), lambda b, c: (c, 0, 0)),
            pl.BlockSpec((1, 1, 4 * LANES), lambda b, c: (c, 0, 0)),
            pl.BlockSpec((2, LANES), lambda b, c: (0, c)),
        ],
        out_specs=pl.BlockSpec((1, S, LANES), blk),
        out_shape=jax.ShapeDtypeStruct((B, S, C), F32),
        scratch_shapes=[
            pltpu.VMEM((S + 2 * SUBLANES, LANES), F32),
            pltpu.VMEM((S, LANES), F32),
        ],
        compiler_params=_cparams(("parallel", "parallel")),
        name="rglru",
    )(lru_x, lru_gate, conv_w, conv_b.reshape(1, C), w, bias, lam)


def _outproj_kernel(a_ref, l_ref, x_ref, ag_ref, lg_ref, wa_ref, wl_ref, g2_ref,
                    wrh_ref, wrl_ref, br_ref, tri_ref,
                    x1_ref, xn3_ref, route_ref, gates_ref, cnt_ref, carry_ref, *, attn_w, lru_w):
    step = pl.program_id(0)

    @pl.when(step == 0)
    def _():
        carry_ref[...] = jnp.zeros_like(carry_ref)

    a = a_ref[...].astype(F32)
    ams = jnp.sum(a * a, axis=-1, keepdims=True) * (1.0 / attn_w)
    an = a * lax.rsqrt(ams + NORM_EPS) * ag_ref[...]
    l = l_ref[...]
    lms = jnp.sum(l * l, axis=-1, keepdims=True) * (1.0 / lru_w)
    ln = l * lax.rsqrt(lms + NORM_EPS) * lg_ref[...]
    mix = (jnp.dot(an.astype(BF16), wa_ref[...], preferred_element_type=F32)
           + jnp.dot(ln.astype(BF16), wl_ref[...], preferred_element_type=F32))
    x1 = x_ref[...] + mix
    x1_ref[...] = x1
    ms = jnp.mean(x1 * x1, axis=-1, keepdims=True)
    xn = x1 * lax.rsqrt(ms + NORM_EPS) * g2_ref[...]
    for s in range(SUBLANES):
        xn3_ref[:, s, :] = xn[:, s * LANES:(s + 1) * LANES]

    hi = xn.astype(BF16)
    lo = (xn - hi.astype(F32)).astype(BF16)
    logits = (jnp.dot(hi, wrh_ref[...], preferred_element_type=F32)
              + jnp.dot(lo, wrh_ref[...], preferred_element_type=F32)
              + jnp.dot(hi, wrl_ref[...], preferred_element_type=F32)) + br_ref[...]
    lane = lax.broadcasted_iota(jnp.int32, logits.shape, 1)
    neg = -jnp.inf
    work = jnp.where(lane < N_EXPERTS, logits, neg)
    sel = jnp.zeros(logits.shape, F32)
    idxs, vals = [], []
    for _ in range(TOP_K):
        m = jnp.max(work, axis=1, keepdims=True)
        idx = jnp.min(jnp.where(work == m, lane, LANES), axis=1, keepdims=True)
        hit = lane == idx
        work = jnp.where(hit, neg, work)
        sel = sel + hit.astype(F32)
        idxs.append(idx)
        vals.append(m)
    es = [jnp.exp(v - vals[0]) for v in vals]
    den = es[0] + es[1] + es[2] + es[3]

    prefix = jnp.dot(tri_ref[...], sel.astype(BF16), preferred_element_type=F32) + carry_ref[...]
    carry_ref[...] = carry_ref[...] + jnp.sum(sel, axis=0, keepdims=True)
    cnt_ref[...] = carry_ref[...]

    route = jnp.zeros(logits.shape, jnp.int32)
    gates = jnp.zeros(logits.shape, F32)
    for k in range(TOP_K):
        rank = jnp.sum(jnp.where(lane == idxs[k], prefix, 0.0), axis=1, keepdims=True).astype(jnp.int32)
        route = jnp.where(lane == k, idxs[k], route)
        route = jnp.where(lane == TOP_K + k, rank, route)
        gates = jnp.where(lane == k, es[k] / den, gates)
    route_ref[...] = route
    gates_ref[...] = gates


def _outproj_router(attn, lru, x2, attn_out_g, lru_out_g, w_out, norm2_g, w_router, b_router):
    T, D = x2.shape
    lru_w = lru.shape[-1]
    ts = min(TS_OUT, T)
    wa = w_out[:ATTN_W].reshape(N_Q_HEADS, HEAD_DIM, D)
    wa = jnp.pad(wa, ((0, 0), (0, LANES - HEAD_DIM), (0, 0))).reshape(N_Q_HEADS * LANES, D).astype(BF16)
    wl = w_out[ATTN_W:].astype(BF16)
    ag = _pad_heads(attn_out_g.reshape(1, ATTN_W), N_Q_HEADS)
    wr = jnp.pad(w_router, ((0, 0), (0, LANES - N_EXPERTS)))
    wrh = wr.astype(BF16)
    wrl = (wr - wrh.astype(F32)).astype(BF16)
    br = jnp.pad(b_router.reshape(1, N_EXPERTS), ((0, 0), (0, LANES - N_EXPERTS)))
    tri = (jnp.arange(ts)[:, None] > jnp.arange(ts)[None, :]).astype(BF16)
    const = lambda i: (0, 0)
    tok = lambda i: (i, 0)
    aw = N_Q_HEADS * LANES
    return pl.pallas_call(
        functools.partial(_outproj_kernel, attn_w=ATTN_W, lru_w=lru_w),
        grid=(T // ts,),
        in_specs=[
            pl.BlockSpec((ts, aw), tok),
            pl.BlockSpec((ts, lru_w), tok),
            pl.BlockSpec((ts, D), tok),
            pl.BlockSpec((1, aw), const),
            pl.BlockSpec((1, lru_w), const),
            pl.BlockSpec((aw, D), const),
            pl.BlockSpec((lru_w, D), const),
            pl.BlockSpec((1, D), const),
            pl.BlockSpec((D, LANES), const),
            pl.BlockSpec((D, LANES), const),
            pl.BlockSpec((1, LANES), const),
            pl.BlockSpec((ts, ts), const),
        ],
        out_specs=[
            pl.BlockSpec((ts, D), tok),
            pl.BlockSpec((ts, SUBLANES, D // SUBLANES), lambda i: (i, 0, 0)),
            pl.BlockSpec((ts, LANES), tok),
            pl.BlockSpec((ts, LANES), tok),
            pl.BlockSpec((1, LANES), const),
        ],
        out_shape=[
            jax.ShapeDtypeStruct((T, D), F32),
            jax.ShapeDtypeStruct((T, SUBLANES, D // SUBLANES), F32),
            jax.ShapeDtypeStruct((T, LANES), jnp.int32),
            jax.ShapeDtypeStruct((T, LANES), F32),
            jax.ShapeDtypeStruct((1, LANES), F32),
        ],
        scratch_shapes=[pltpu.VMEM((1, LANES), F32)],
        compiler_params=_cparams(("arbitrary",)),
        name="outproj_router",
    )(attn, lru, x2, ag, lru_out_g.reshape(1, lru_w), wa, wl, norm2_g.reshape(1, D),
      wrh, wrl, br, tri)


def _dispatch_kernel(dest_ref, x_ref, init_hbm, out_hbm, sem, *, ts):
    del init_hbm

    def issue(r, carry):
        for k in range(TOP_K):
            d = dest_ref[r * TOP_K + k]
            pltpu.make_async_copy(x_ref.at[r], out_hbm.at[d], sem).start()
        return carry

    lax.fori_loop(0, ts, issue, 0)
    for k in range(TOP_K):
        pltpu.make_async_copy(x_ref, out_hbm.at[pl.ds(0, ts)], sem).wait()


def _dispatch(xn3, dest_flat, n_rows):
    T = xn3.shape[0]
    ts = min(TS_DISP, T)
    init = jnp.zeros((n_rows,) + xn3.shape[1:], xn3.dtype)
    return pl.pallas_call(
        functools.partial(_dispatch_kernel, ts=ts),
        grid=(T // ts,),
        in_specs=[
            pl.BlockSpec((ts * TOP_K,), lambda i: (i,), memory_space=pltpu.SMEM),
            pl.BlockSpec((ts,) + xn3.shape[1:], lambda i: (i, 0, 0)),
            pl.BlockSpec(memory_space=pl.ANY),
        ],
        out_specs=pl.BlockSpec(memory_space=pl.ANY),
        out_shape=jax.ShapeDtypeStruct(init.shape, init.dtype),
        scratch_shapes=[pltpu.SemaphoreType.DMA],
        input_output_aliases={2: 0},
        compiler_params=_cparams(("arbitrary",)),
        name="dispatch",
    )(dest_flat, xn3, init)


def _expert_kernel(be_ref, na_ref, x_ref, wg_ref, bg_ref, wu_ref, bu_ref, wd_ref, bd_ref,
                   y_ref, wgb_ref, wub_ref, wdb_ref, prev_ref):
    i = pl.program_id(0)
    e = be_ref[i]

    @pl.when(i == 0)
    def _():
        prev_ref[0] = -1

    active = i < na_ref[0]

    @pl.when(jnp.logical_and(active, e != prev_ref[0]))
    def _():
        wgb_ref[...] = wg_ref[0].astype(BF16)
        wub_ref[...] = wu_ref[0].astype(BF16)
        wdb_ref[...] = wd_ref[0].astype(BF16)
        prev_ref[0] = e

    @pl.when(active)
    def _():
        x = jnp.concatenate([x_ref[:, s, :] for s in range(SUBLANES)], axis=1).astype(BF16)
        g = jnp.dot(x, wgb_ref[...], preferred_element_type=F32) + bg_ref[0]
        u = jnp.dot(x, wub_ref[...], preferred_element_type=F32) + bu_ref[0]
        g = jnp.minimum(g, SWIGLU_LIMIT)
        u = jnp.clip(u, -SWIGLU_LIMIT, SWIGLU_LIMIT)
        glu = g * jax.nn.sigmoid(SWIGLU_ALPHA * g)
        y = jnp.dot(((u + 1.0) * glu).astype(BF16), wdb_ref[...], preferred_element_type=F32) + bd_ref[0]
        for s in range(SUBLANES):
            y_ref[:, s, :] = y[:, s * LANES:(s + 1) * LANES]


def _experts(x_rows, block_e, n_active, w_gate, b_gate, w_up, b_up, w_down, b_down):
    n_rows = x_rows.shape[0]
    E, D, FF = w_gate.shape
    n_blocks = n_rows // ROW_BLOCK

    def row_map(i, be, na):
        return (jnp.minimum(i, na[0] - 1), 0, 0)

    def w_map(i, be, na):
        return (be[jnp.minimum(i, na[0] - 1)], 0, 0)

    grid_spec = pltpu.PrefetchScalarGridSpec(
        num_scalar_prefetch=2,
        grid=(n_blocks,),
        in_specs=[
            pl.BlockSpec((ROW_BLOCK,) + x_rows.shape[1:], row_map),
            pl.BlockSpec((1, D, FF), w_map),
            pl.BlockSpec((1, 1, FF), w_map),
            pl.BlockSpec((1, D, FF), w_map),
            pl.BlockSpec((1, 1, FF), w_map),
            pl.BlockSpec((1, FF, D), w_map),
            pl.BlockSpec((1, 1, D), w_map),
        ],
        out_specs=pl.BlockSpec((ROW_BLOCK,) + x_rows.shape[1:], row_map),
        scratch_shapes=[
            pltpu.VMEM((D, FF), BF16),
            pltpu.VMEM((D, FF), BF16),
            pltpu.VMEM((FF, D), BF16),
            pltpu.SMEM((1,), jnp.int32),
        ],
    )
    return pl.pallas_call(
        _expert_kernel,
        grid_spec=grid_spec,
        out_shape=jax.ShapeDtypeStruct(x_rows.shape, F32),
        input_output_aliases={2: 0},
        compiler_params=_cparams(("arbitrary",)),
        name="experts",
    )(block_e, n_active, x_rows, w_gate, b_gate.reshape(E, 1, FF), w_up, b_up.reshape(E, 1, FF),
      w_down, b_down.reshape(E, 1, D))


def _combine_kernel(dest_ref, y_hbm, x1_ref, gates_ref, fg_ref, o_ref,
                    b0, b1, b2, b3, sem, *, ts):
    bufs = (b0, b1, b2, b3)

    def issue(r, carry):
        for k in range(TOP_K):
            d = dest_ref[r * TOP_K + k]
            pltpu.make_async_copy(y_hbm.at[d], bufs[k].at[r], sem).start()
        return carry

    lax.fori_loop(0, ts, issue, 0)
    for k in range(TOP_K):
        pltpu.make_async_copy(y_hbm.at[pl.ds(0, ts)], bufs[k], sem).wait()

    acc = x1_ref[...]
    gates = gates_ref[...]
    for k in range(TOP_K):
        yk = jnp.concatenate([bufs[k][:, s, :] for s in range(SUBLANES)], axis=1)
        acc = acc + yk * gates[:, k:k + 1]
    ms = jnp.mean(acc * acc, axis=-1, keepdims=True)
    o_ref[...] = acc * lax.rsqrt(ms + NORM_EPS) * fg_ref[...]


def _combine(y_rows, dest_flat, x1, gates, final_g):
    T, D = x1.shape
    ts = min(TS_COMB, T)
    tok = lambda i: (i, 0)
    row_shape = (ts,) + y_rows.shape[1:]
    return pl.pallas_call(
        functools.partial(_combine_kernel, ts=ts),
        grid=(T // ts,),
        in_specs=[
            pl.BlockSpec((ts * TOP_K,), lambda i: (i,), memory_space=pltpu.SMEM),
            pl.BlockSpec(memory_space=pl.ANY),
            pl.BlockSpec((ts, D), tok),
            pl.BlockSpec((ts, LANES), tok),
            pl.BlockSpec((1, D), lambda i: (0, 0)),
        ],
        out_specs=pl.BlockSpec((ts, D), tok),
        out_shape=jax.ShapeDtypeStruct((T, D), F32),
        scratch_shapes=[pltpu.VMEM(row_shape, F32) for _ in range(TOP_K)] + [pltpu.SemaphoreType.DMA],
        compiler_params=_cparams(("arbitrary",)),
        name="combine",
    )(dest_flat, y_rows, x1, gates, final_g.reshape(1, D))


def kernel(x, norm1_g, w_in, q_norm_g, k_norm_g, conv_w, conv_b, lru_wa, lru_ba, lru_wi, lru_bi,
           lru_lam, attn_out_g, lru_out_g, w_out, norm2_g, w_router, b_router, w_gate, b_gate,
           w_up, b_up, w_down, b_down, final_g):
    B, S, D = x.shape
    T = B * S
    assert w_in.shape[0] == 1, "single-layer trunk: the final norm is fused into the layer's combine"
    x2 = x.reshape(T, D)
    for l in range(1):
        q, k, v, lru_x, lru_gate = _inproj(x2, norm1_g[l], w_in[l], q_norm_g[l], k_norm_g[l], S)
        attn = _attention(q.reshape(B, S, -1), k.reshape(B, S, -1), v.reshape(B, S, -1), B, S)
        lru = _lru(lru_x.reshape(B, S, -1), lru_gate.reshape(B, S, -1), conv_w[l], conv_b[l],
                   lru_wa[l], lru_ba[l], lru_wi[l], lru_bi[l], lru_lam[l], B, S)
        x1, xn3, route, gates, cnt = _outproj_router(
            attn.reshape(T, -1), lru.reshape(T, -1), x2, attn_out_g[l], lru_out_g[l], w_out[l],
            norm2_g[l], w_router[l], b_router[l])

        idx = route[:, :TOP_K]
        rank = route[:, TOP_K:2 * TOP_K]
        counts = cnt[0, :N_EXPERTS].astype(jnp.int32)
        padded = ((counts + ROW_BLOCK - 1) // ROW_BLOCK) * ROW_BLOCK
        pend = jnp.cumsum(padded)
        pstart = pend - padded
        dest = (pstart[idx] + rank).reshape(T * TOP_K).astype(jnp.int32)
        n_rows = T * TOP_K + N_EXPERTS * ROW_BLOCK
        n_blocks = n_rows // ROW_BLOCK
        block_e = jnp.searchsorted(pend, jnp.arange(n_blocks, dtype=jnp.int32) * ROW_BLOCK, side='right')
        block_e = jnp.minimum(block_e, N_EXPERTS - 1).astype(jnp.int32)
        n_active = (pend[-1:] // ROW_BLOCK).astype(jnp.int32)

        x_rows = _dispatch(xn3, dest, n_rows)
        y_rows = _experts(x_rows, block_e, n_active, w_gate[l], b_gate[l], w_up[l], b_up[l],
                          w_down[l], b_down[l])
        x2 = _combine(y_rows, dest, x1, gates, final_g)
    return x2.reshape(B, S, D)
```

```python
import functools

import jax
import jax.numpy as jnp
from jax import lax
from jax.experimental import pallas as pl
from jax.experimental.pallas import tpu as pltpu

F32 = jnp.float32
BF16 = jnp.bfloat16

GRID_W = 64
HEAD_DIM = 64
N_Q_HEADS = 8
N_KV_HEADS = 2
GQA_GROUP = N_Q_HEADS // N_KV_HEADS
ATTN_W = N_Q_HEADS * HEAD_DIM
KV_W = N_KV_HEADS * HEAD_DIM
LRU_BLOCKS = 8
LRU_C = 8.0
CONV_W = 4
CONV_PAD_L = 2
ROPE_THETA = 10000.0
ROPE_HALF = HEAD_DIM // 2
ROPE_M = ROPE_HALF // 2
N_EXPERTS = 32
TOP_K = 4
SWIGLU_ALPHA = 1.702
SWIGLU_LIMIT = 7.0
NORM_EPS = 1e-5
QK_EPS = 1e-6

LANES = 128
SUBLANES = 8
VMEM_LIMIT = 48 * 1024 * 1024

TS_IN = 512
TQ = 256
TK = 512
TC_LRU = 512
TS_OUT = 512
ROW_BLOCK = 256
TS_DISP = 512
TS_COMB = 256


def _cparams(sem):
    return pltpu.CompilerParams(dimension_semantics=sem, vmem_limit_bytes=VMEM_LIMIT)


def _inproj_kernel(x_ref, g1_ref, w_ref, qg_ref, kg_ref, cos_ref, sin_ref,
                   q_ref, k_ref, v_ref, lx_ref, lg_ref, *, lru_w):
    x = x_ref[...]
    ms = jnp.mean(x * x, axis=-1, keepdims=True)
    xn = x * lax.rsqrt(ms + NORM_EPS) * g1_ref[...]
    h = jnp.dot(xn.astype(BF16), w_ref[...], preferred_element_type=F32)

    cos = cos_ref[...]
    sin = sin_ref[...]
    lane = lax.broadcasted_iota(jnp.int32, cos.shape, 1)
    first_half = (lane % ROPE_HALF) < ROPE_M

    def head_norm_rope(xc, g, scale):
        hms = jnp.sum(xc * xc, axis=-1, keepdims=True) * (1.0 / HEAD_DIM)
        xc = xc * lax.rsqrt(hms + QK_EPS) * g
        partner = jnp.where(first_half,
                            pltpu.roll(xc, LANES - ROPE_M, 1),
                            pltpu.roll(xc, ROPE_M, 1))
        return (xc * cos + partner * sin) * scale

    qw = N_Q_HEADS * LANES
    kw = N_KV_HEADS * LANES
    for c in range(N_Q_HEADS):
        sl = slice(c * LANES, (c + 1) * LANES)
        q_ref[0, sl, :] = head_norm_rope(h[:, sl], qg_ref[...], HEAD_DIM ** -0.5).T.astype(BF16)
    for c in range(N_KV_HEADS):
        sl = slice(c * LANES, (c + 1) * LANES)
        k_ref[:, sl] = head_norm_rope(h[:, qw + c * LANES: qw + (c + 1) * LANES],
                                      kg_ref[...], 1.0).astype(BF16)
        vc = h[:, qw + kw + c * LANES: qw + kw + (c + 1) * LANES]
        v_ref[0, sl, :] = jnp.where(lane >= HEAD_DIM, 1.0, vc).T.astype(BF16)
    o = qw + 2 * kw
    lx_ref[...] = h[:, o: o + lru_w]
    lg_ref[...] = h[:, o + lru_w: o + 2 * lru_w]


def _pad_heads(w, n_heads):
    lead = w.shape[:-1]
    w = w.reshape(lead + (n_heads, HEAD_DIM))
    w = jnp.pad(w, [(0, 0)] * len(lead) + [(0, 0), (0, LANES - HEAD_DIM)])
    return w.reshape(lead + (n_heads * LANES,))


def _rope_tables(S):
    t = jnp.arange(S)
    rows = (t // GRID_W).astype(F32)
    cols = (t % GRID_W).astype(F32)
    inv_freq = ROPE_THETA ** (-jnp.arange(ROPE_M, dtype=F32) / ROPE_M)
    ar = rows[:, None] * inv_freq[None, :]
    ac = cols[:, None] * inv_freq[None, :]
    cos = jnp.concatenate([jnp.cos(ar), jnp.cos(ar), jnp.cos(ac), jnp.cos(ac)], axis=-1)
    sin = jnp.concatenate([-jnp.sin(ar), jnp.sin(ar), -jnp.sin(ac), jnp.sin(ac)], axis=-1)
    pad = [(0, 0), (0, LANES - HEAD_DIM)]
    return jnp.pad(cos, pad), jnp.pad(sin, pad)


def _inproj(x2, norm1_g, w_in, q_norm_g, k_norm_g, S):
    T, D = x2.shape
    lru_w = (w_in.shape[1] - ATTN_W - 2 * KV_W) // 2
    o0, o1, o2 = ATTN_W, ATTN_W + KV_W, ATTN_W + 2 * KV_W
    w_all = jnp.concatenate([
        _pad_heads(w_in[:, :o0], N_Q_HEADS),
        _pad_heads(w_in[:, o0:o1], N_KV_HEADS),
        _pad_heads(w_in[:, o1:o2], N_KV_HEADS),
        w_in[:, o2:],
    ], axis=1).astype(BF16)
    qg = _pad_heads(q_norm_g.reshape(1, HEAD_DIM), 1)
    kg = _pad_heads(k_norm_g.reshape(1, HEAD_DIM), 1)
    cos, sin = _rope_tables(S)
    ts = TS_IN
    n_s = S // ts
    qw, kw = N_Q_HEADS * LANES, N_KV_HEADS * LANES
    const = lambda i: (0, 0)
    tok = lambda i: (i, 0)
    pos = lambda i: (i % n_s, 0)
    tposed = lambda i: (i // n_s, 0, i % n_s)
    return pl.pallas_call(
        functools.partial(_inproj_kernel, lru_w=lru_w),
        grid=(T // ts,),
        in_specs=[
            pl.BlockSpec((ts, D), tok),
            pl.BlockSpec((1, D), const),
            pl.BlockSpec(w_all.shape, const),
            pl.BlockSpec((1, LANES), const),
            pl.BlockSpec((1, LANES), const),
            pl.BlockSpec((ts, LANES), pos),
            pl.BlockSpec((ts, LANES), pos),
        ],
        out_specs=[
            pl.BlockSpec((1, qw, ts), tposed),
            pl.BlockSpec((ts, kw), tok),
            pl.BlockSpec((1, kw, ts), tposed),
            pl.BlockSpec((ts, lru_w), tok),
            pl.BlockSpec((ts, lru_w), tok),
        ],
        out_shape=[
            jax.ShapeDtypeStruct((T // S, qw, S), BF16),
            jax.ShapeDtypeStruct((T, kw), BF16),
            jax.ShapeDtypeStruct((T // S, kw, S), BF16),
            jax.ShapeDtypeStruct((T, lru_w), F32),
            jax.ShapeDtypeStruct((T, lru_w), F32),
        ],
        compiler_params=_cparams(("parallel",)),
        name="inproj",
    )(x2, norm1_g.reshape(1, D), w_all, qg, kg, cos, sin)


def _attn_kernel(qt_ref, k_ref, vt_ref, o_ref, acc_ref, s_ref, *, tq, tk, n_kv):
    acc_ref[...] = jnp.zeros(acc_ref.shape, F32)

    def scores(j, g):
        off = pl.multiple_of(j * tk, tk)
        return jnp.dot(k_ref[0, pl.ds(off, tk), :], qt_ref[0, g * LANES:(g + 1) * LANES, :],
                       preferred_element_type=F32)

    s_ref[...] = scores(0, 0)

    def body(j, ms):
        off = pl.multiple_of(j * tk, tk)
        vt = vt_ref[0, :, pl.ds(off, tk)]
        s = s_ref[...]
        new_ms = []
        for g in range(GQA_GROUP):
            if g + 1 < GQA_GROUP:
                s_next = scores(j, g + 1)
            else:
                s_next = scores(jnp.minimum(j + 1, n_kv - 1), 0)
            m_new = jnp.maximum(ms[g], jnp.max(s, axis=0, keepdims=True))
            alpha = jnp.exp(ms[g] - m_new)
            p = jnp.exp(s - m_new).astype(BF16)
            acc_ref[g] = alpha * acc_ref[g] + jnp.dot(vt, p, preferred_element_type=F32)
            new_ms.append(m_new)
            s = s_next
        s_ref[...] = s
        return tuple(new_ms)

    m0 = jnp.full((1, tq), -jnp.inf, F32)
    lax.fori_loop(0, n_kv, body, (m0,) * GQA_GROUP)
    row = lax.broadcasted_iota(jnp.int32, (LANES, tq), 0)
    for g in range(GQA_GROUP):
        acc = acc_ref[g]
        o = jnp.where(row < HEAD_DIM, acc / acc[HEAD_DIM:HEAD_DIM + 1, :], 0.0)
        o_ref[0, :, g * LANES:(g + 1) * LANES] = o.T.astype(BF16)


def _attention(qt, k, vt, B, S):
    tq = min(TQ, S)
    tk = min(TK, S)
    gw = GQA_GROUP * LANES
    return pl.pallas_call(
        functools.partial(_attn_kernel, tq=tq, tk=tk, n_kv=S // tk),
        grid=(B, N_KV_HEADS, S // tq),
        in_specs=[
            pl.BlockSpec((1, gw, tq), lambda b, h, i: (b, h, i)),
            pl.BlockSpec((1, S, LANES), lambda b, h, i: (b, 0, h)),
            pl.BlockSpec((1, LANES, S), lambda b, h, i: (b, h, 0)),
        ],
        out_specs=pl.BlockSpec((1, tq, gw), lambda b, h, i: (b, i, h)),
        out_shape=jax.ShapeDtypeStruct((B, S, N_Q_HEADS * LANES), BF16),
        scratch_shapes=[pltpu.VMEM((GQA_GROUP, LANES, tq), F32), pltpu.VMEM((tk, tq), F32)],
        compiler_params=_cparams(("parallel", "parallel", "parallel")),
        name="attention",
    )(qt, k, vt)


def _scan_chunk(a, b, reverse):
    n = a.shape[0]
    row = lax.broadcasted_iota(jnp.int32, a.shape, 0)
    d = 1
    while d < n:
        if reverse:
            keep = row < n - d
            shift = n - d
        else:
            keep = row >= d
            shift = d
        a_sh = jnp.where(keep, pltpu.roll(a, shift, 0), 1.0)
        b_sh = jnp.where(keep, pltpu.roll(b, shift, 0), 0.0)
        b = a * b_sh + b
        a = a * a_sh
        d *= 2
    return a, b


def _lru_kernel(u_ref, gate_ref, cw_ref, cb_ref, w_ref, bias_ref, lam_ref, o_ref,
                up_ref, hf_ref, *, S, tc):
    halo = SUBLANES
    zeros = jnp.zeros((halo, LANES), F32)
    up_ref[0:halo, :] = zeros
    up_ref[S + halo:S + 2 * halo, :] = zeros
    up_ref[halo:S + halo, :] = u_ref[0]
    sp = jax.nn.softplus(-lam_ref[...])
    cw = cw_ref[...]
    cb = cb_ref[...]
    n_chunks = S // tc
    ext = tc + 2 * halo

    def gates(c, d):
        t0 = pl.multiple_of(c * tc, tc)
        ue = up_ref[pl.ds(t0, ext), :]
        xc = cb
        for j in range(CONV_W):
            sh = (CONV_PAD_L - j) % ext
            uj = ue if sh == 0 else pltpu.roll(ue, sh, 0)
            xc = xc + uj[halo:halo + tc] * cw[j:j + 1, :]
        gw = 2 * LANES
        g = jnp.dot(xc.astype(BF16), w_ref[0, :, d * gw:(d + 1) * gw],
                    preferred_element_type=F32) + bias_ref[0, :, d * gw:(d + 1) * gw]
        r = jax.nn.sigmoid(g[:, :LANES])
        i = jax.nn.sigmoid(g[:, LANES:])
        log_a = -LRU_C * r * sp[d:d + 1, :]
        a = jnp.exp(log_a)
        b = jnp.sqrt(1.0 - jnp.exp(2.0 * log_a)) * i * xc
        return t0, a, b

    def fwd(c, h):
        t0, a, b = gates(c, 0)
        pa, hb = _scan_chunk(a, b, False)
        hc = hb + pa * h
        hf_ref[pl.ds(t0, tc), :] = hc
        return hc[tc - 1:tc, :]

    lax.fori_loop(0, n_chunks, fwd, jnp.zeros((1, LANES), F32))

    def bwd(ci, h):
        t0, a, b = gates(n_chunks - 1 - ci, 1)
        pa, hb = _scan_chunk(a, b, True)
        hc = hb + pa * h
        gate = gate_ref[0, pl.ds(t0, tc), :]
        o_ref[0, pl.ds(t0, tc), :] = (hf_ref[pl.ds(t0, tc), :] + hc) * jax.nn.gelu(gate)
        return hc[0:1, :]

    lax.fori_loop(0, n_chunks, bwd, jnp.zeros((1, LANES), F32))


def _block_diag_pairs(w):
    nb, bw, _ = w.shape
    w = w.reshape(nb // 2, 2, bw, bw)
    z = jnp.zeros_like(w[:, 0])
    top = jnp.concatenate([w[:, 0], z], axis=-1)
    bot = jnp.concatenate([z, w[:, 1]], axis=-1)
    return jnp.concatenate([top, bot], axis=-2)


def _lru(lru_x, lru_gate, conv_w, conv_b, wa, ba, wi, bi, lam, B, S):
    C = lru_x.shape[-1]
    nc = C // LANES
    tc = min(TC_LRU, S)
    w = jnp.concatenate([_block_diag_pairs(wa[0]), _block_diag_pairs(wi[0]),
                         _block_diag_pairs(wa[1]), _block_diag_pairs(wi[1])], axis=-1).astype(BF16)
    bias = jnp.stack([ba[0].reshape(nc, LANES), bi[0].reshape(nc, LANES),
                      ba[1].reshape(nc, LANES), bi[1].reshape(nc, LANES)], axis=1)
    bias = bias.reshape(nc, 1, 4 * LANES)
    blk = lambda b, c: (b, 0, c)
    return pl.pallas_call(
        functools.partial(_lru_kernel, S=S, tc=tc),
        grid=(B, nc),
        in_specs=[
            pl.BlockSpec((1, S, LANES), blk),
            pl.BlockSpec((1, S, LANES), blk),
            pl.BlockSpec((CONV_W, LANES), lambda b, c: (0, c)),
            pl.BlockSpec((1, LANES), lambda b, c: (0, c)),
            pl.BlockSpec((1, LANES, 4 * LANES), lambda b, c: (c, 0, 0)),
            pl.BlockSpec((1, 1, 4 * LANES), lambda b, c: (c, 0, 0)),
            pl.BlockSpec((2, LANES), lambda b, c: (0, c)),
        ],
        out_specs=pl.BlockSpec((1, S, LANES), blk),
        out_shape=jax.ShapeDtypeStruct((B, S, C), F32),
        scratch_shapes=[
            pltpu.VMEM((S + 2 * SUBLANES, LANES), F32),
            pltpu.VMEM((S, LANES), F32),
        ],
        compiler_params=_cparams(("parallel", "parallel")),
        name="rglru",
    )(lru_x, lru_gate, conv_w, conv_b.reshape(1, C), w, bias, lam)


def _outproj_kernel(a_ref, l_ref, x_ref, ag_ref, lg_ref, wa_ref, wl_ref, g2_ref,
                    wrh_ref, wrl_ref, br_ref, tri_ref,
                    x1_ref, xn3_ref, route_ref, gates_ref, cnt_ref, carry_ref, *, attn_w, lru_w):
    step = pl.program_id(0)

    @pl.when(step == 0)
    def _():
        carry_ref[...] = jnp.zeros_like(carry_ref)

    a = a_ref[...].astype(F32)
    ams = jnp.sum(a * a, axis=-1, keepdims=True) * (1.0 / attn_w)
    an = a * lax.rsqrt(ams + NORM_EPS) * ag_ref[...]
    l = l_ref[...]
    lms = jnp.sum(l * l, axis=-1, keepdims=True) * (1.0 / lru_w)
    ln = l * lax.rsqrt(lms + NORM_EPS) * lg_ref[...]
    mix = (jnp.dot(an.astype(BF16), wa_ref[...], preferred_element_type=F32)
           + jnp.dot(ln.astype(BF16), wl_ref[...], preferred_element_type=F32))
    x1 = x_ref[...] + mix
    x1_ref[...] = x1
    ms = jnp.mean(x1 * x1, axis=-1, keepdims=True)
    xn = x1 * lax.rsqrt(ms + NORM_EPS) * g2_ref[...]
    for s in range(SUBLANES):
        xn3_ref[:, s, :] = xn[:, s * LANES:(s + 1) * LANES]

    hi = xn.astype(BF16)
    lo = (xn - hi.astype(F32)).astype(BF16)
    logits = (jnp.dot(hi, wrh_ref[...], preferred_element_type=F32)
              + jnp.dot(lo, wrh_ref[...], preferred_element_type=F32)
              + jnp.dot(hi, wrl_ref[...], preferred_element_type=F32)) + br_ref[...]
    lane = lax.broadcasted_iota(jnp.int32, logits.shape, 1)
    neg = -jnp.inf
    work = jnp.where(lane < N_EXPERTS, logits, neg)
    sel = jnp.zeros(logits.shape, F32)
    idxs, vals = [], []
    for _ in range(TOP_K):
        m = jnp.max(work, axis=1, keepdims=True)
        idx = jnp.min(jnp.where(work == m, lane, LANES), axis=1, keepdims=True)
        hit = lane == idx
        work = jnp.where(hit, neg, work)
        sel = sel + hit.astype(F32)
        idxs.append(idx)
        vals.append(m)
    es = [jnp.exp(v - vals[0]) for v in vals]
    den = es[0] + es[1] + es[2] + es[3]

    prefix = jnp.dot(tri_ref[...], sel.astype(BF16), preferred_element_type=F32) + carry_ref[...]
    carry_ref[...] = carry_ref[...] + jnp.sum(sel, axis=0, keepdims=True)
    cnt_ref[...] = carry_ref[...]

    route = jnp.zeros(logits.shape, jnp.int32)
    gates = jnp.zeros(logits.shape, F32)
    for k in range(TOP_K):
        rank = jnp.sum(jnp.where(lane == idxs[k], prefix, 0.0), axis=1, keepdims=True).astype(jnp.int32)
        route = jnp.where(lane == k, idxs[k], route)
        route = jnp.where(lane == TOP_K + k, rank, route)
        gates = jnp.where(lane == k, es[k] / den, gates)
    route_ref[...] = route
    gates_ref[...] = gates


def _outproj_router(attn, lru, x2, attn_out_g, lru_out_g, w_out, norm2_g, w_router, b_router):
    T, D = x2.shape
    lru_w = lru.shape[-1]
    ts = min(TS_OUT, T)
    wa = w_out[:ATTN_W].reshape(N_Q_HEADS, HEAD_DIM, D)
    wa = jnp.pad(wa, ((0, 0), (0, LANES - HEAD_DIM), (0, 0))).reshape(N_Q_HEADS * LANES, D).astype(BF16)
    wl = w_out[ATTN_W:].astype(BF16)
    ag = _pad_heads(attn_out_g.reshape(1, ATTN_W), N_Q_HEADS)
    wr = jnp.pad(w_router, ((0, 0), (0, LANES - N_EXPERTS)))
    wrh = wr.astype(BF16)
    wrl = (wr - wrh.astype(F32)).astype(BF16)
    br = jnp.pad(b_router.reshape(1, N_EXPERTS), ((0, 0), (0, LANES - N_EXPERTS)))
    tri = (jnp.arange(ts)[:, None] > jnp.arange(ts)[None, :]).astype(BF16)
    const = lambda i: (0, 0)
    tok = lambda i: (i, 0)
    aw = N_Q_HEADS * LANES
    return pl.pallas_call(
        functools.partial(_outproj_kernel, attn_w=ATTN_W, lru_w=lru_w),
        grid=(T // ts,),
        in_specs=[
            pl.BlockSpec((ts, aw), tok),
            pl.BlockSpec((ts, lru_w), tok),
            pl.BlockSpec((ts, D), tok),
            pl.BlockSpec((1, aw), const),
            pl.BlockSpec((1, lru_w), const),
            pl.BlockSpec((aw, D), const),
            pl.BlockSpec((lru_w, D), const),
            pl.BlockSpec((1, D), const),
            pl.BlockSpec((D, LANES), const),
            pl.BlockSpec((D, LANES), const),
            pl.BlockSpec((1, LANES), const),
            pl.BlockSpec((ts, ts), const),
        ],
        out_specs=[
            pl.BlockSpec((ts, D), tok),
            pl.BlockSpec((ts, SUBLANES, D // SUBLANES), lambda i: (i, 0, 0)),
            pl.BlockSpec((ts, LANES), tok),
            pl.BlockSpec((ts, LANES), tok),
            pl.BlockSpec((1, LANES), const),
        ],
        out_shape=[
            jax.ShapeDtypeStruct((T, D), F32),
            jax.ShapeDtypeStruct((T, SUBLANES, D // SUBLANES), F32),
            jax.ShapeDtypeStruct((T, LANES), jnp.int32),
            jax.ShapeDtypeStruct((T, LANES), F32),
            jax.ShapeDtypeStruct((1, LANES), F32),
        ],
        scratch_shapes=[pltpu.VMEM((1, LANES), F32)],
        compiler_params=_cparams(("arbitrary",)),
        name="outproj_router",
    )(attn, lru, x2, ag, lru_out_g.reshape(1, lru_w), wa, wl, norm2_g.reshape(1, D),
      wrh, wrl, br, tri)


def _dispatch_kernel(dest_ref, x_ref, init_hbm, out_hbm, sem, *, ts):
    del init_hbm

    def issue(r, carry):
        for k in range(TOP_K):
            d = dest_ref[r * TOP_K + k]
            pltpu.make_async_copy(x_ref.at[r], out_hbm.at[d], sem).start()
        return carry

    lax.fori_loop(0, ts, issue, 0)
    for k in range(TOP_K):
        pltpu.make_async_copy(x_ref, out_hbm.at[pl.ds(0, ts)], sem).wait()


def _dispatch(xn3, dest_flat, n_rows):
    T = xn3.shape[0]
    ts = min(TS_DISP, T)
    init = jnp.zeros((n_rows,) + xn3.shape[1:], xn3.dtype)
    return pl.pallas_call(
        functools.partial(_dispatch_kernel, ts=ts),
        grid=(T // ts,),
        in_specs=[
            pl.BlockSpec((ts * TOP_K,), lambda i: (i,), memory_space=pltpu.SMEM),
            pl.BlockSpec((ts,) + xn3.shape[1:], lambda i: (i, 0, 0)),
            pl.BlockSpec(memory_space=pl.ANY),
        ],
        out_specs=pl.BlockSpec(memory_space=pl.ANY),
        out_shape=jax.ShapeDtypeStruct(init.shape, init.dtype),
        scratch_shapes=[pltpu.SemaphoreType.DMA],
        input_output_aliases={2: 0},
        compiler_params=_cparams(("arbitrary",)),
        name="dispatch",
    )(dest_flat, xn3, init)


def _expert_kernel(be_ref, na_ref, x_ref, wg_ref, bg_ref, wu_ref, bu_ref, wd_ref, bd_ref,
                   y_ref, wgb_ref, wub_ref, wdb_ref, prev_ref):
    i = pl.program_id(0)
    e = be_ref[i]

    @pl.when(i == 0)
    def _():
        prev_ref[0] = -1

    active = i < na_ref[0]

    @pl.when(jnp.logical_and(active, e != prev_ref[0]))
    def _():
        wgb_ref[...] = wg_ref[0].astype(BF16)
        wub_ref[...] = wu_ref[0].astype(BF16)
        wdb_ref[...] = wd_ref[0].astype(BF16)
        prev_ref[0] = e

    @pl.when(active)
    def _():
        x = jnp.concatenate([x_ref[:, s, :] for s in range(SUBLANES)], axis=1).astype(BF16)
        g = jnp.dot(x, wgb_ref[...], preferred_element_type=F32) + bg_ref[0]
        u = jnp.dot(x, wub_ref[...], preferred_element_type=F32) + bu_ref[0]
        g = jnp.minimum(g, SWIGLU_LIMIT)
        u = jnp.clip(u, -SWIGLU_LIMIT, SWIGLU_LIMIT)
        glu = g * jax.nn.sigmoid(SWIGLU_ALPHA * g)
        y = jnp.dot(((u + 1.0) * glu).astype(BF16), wdb_ref[...], preferred_element_type=F32) + bd_ref[0]
        for s in range(SUBLANES):
            y_ref[:, s, :] = y[:, s * LANES:(s + 1) * LANES]


def _experts(x_rows, block_e, n_active, w_gate, b_gate, w_up, b_up, w_down, b_down):
    n_rows = x_rows.shape[0]
    E, D, FF = w_gate.shape
    n_blocks = n_rows // ROW_BLOCK

    def row_map(i, be, na):
        return (jnp.minimum(i, na[0] - 1), 0, 0)

    def w_map(i, be, na):
        return (be[jnp.minimum(i, na[0] - 1)], 0, 0)

    grid_spec = pltpu.PrefetchScalarGridSpec(
        num_scalar_prefetch=2,
        grid=(n_blocks,),
        in_specs=[
            pl.BlockSpec((ROW_BLOCK,) + x_rows.shape[1:], row_map),
            pl.BlockSpec((1, D, FF), w_map),
            pl.BlockSpec((1, 1, FF), w_map),
            pl.BlockSpec((1, D, FF), w_map),
            pl.BlockSpec((1, 1, FF), w_map),
            pl.BlockSpec((1, FF, D), w_map),
            pl.BlockSpec((1, 1, D), w_map),
        ],
        out_specs=pl.BlockSpec((ROW_BLOCK,) + x_rows.shape[1:], row_map),
        scratch_shapes=[
            pltpu.VMEM((D, FF), BF16),
            pltpu.VMEM((D, FF), BF16),
            pltpu.VMEM((FF, D), BF16),
            pltpu.SMEM((1,), jnp.int32),
        ],
    )
    return pl.pallas_call(
        _expert_kernel,
        grid_spec=grid_spec,
        out_shape=jax.ShapeDtypeStruct(x_rows.shape, F32),
        input_output_aliases={2: 0},
        compiler_params=_cparams(("arbitrary",)),
        name="experts",
    )(block_e, n_active, x_rows, w_gate, b_gate.reshape(E, 1, FF), w_up, b_up.reshape(E, 1, FF),
      w_down, b_down.reshape(E, 1, D))


def _combine_kernel(dest_ref, y_hbm, x1_ref, gates_ref, fg_ref, o_ref,
                    b0, b1, b2, b3, sem, *, ts):
    bufs = (b0, b1, b2, b3)

    def issue(r, carry):
        for k in range(TOP_K):
            d = dest_ref[r * TOP_K + k]
            pltpu.make_async_copy(y_hbm.at[d], bufs[k].at[r], sem).start()
        return carry

    lax.fori_loop(0, ts, issue, 0)
    for k in range(TOP_K):
        pltpu.make_async_copy(y_hbm.at[pl.ds(0, ts)], bufs[k], sem).wait()

    acc = x1_ref[...]
    gates = gates_ref[...]
    for k in range(TOP_K):
        yk = jnp.concatenate([bufs[k][:, s, :] for s in range(SUBLANES)], axis=1)
        acc = acc + yk * gates[:, k:k + 1]
    ms = jnp.mean(acc * acc, axis=-1, keepdims=True)
    o_ref[...] = acc * lax.rsqrt(ms + NORM_EPS) * fg_ref[...]


def _combine(y_rows, dest_flat, x1, gates, final_g):
    T, D = x1.shape
    ts = min(TS_COMB, T)
    tok = lambda i: (i, 0)
    row_shape = (ts,) + y_rows.shape[1:]
    return pl.pallas_call(
        functools.partial(_combine_kernel, ts=ts),
        grid=(T // ts,),
        in_specs=[
            pl.BlockSpec((ts * TOP_K,), lambda i: (i,), memory_space=pltpu.SMEM),
            pl.BlockSpec(memory_space=pl.ANY),
            pl.BlockSpec((ts, D), tok),
            pl.BlockSpec((ts, LANES), tok),
            pl.BlockSpec((1, D), lambda i: (0, 0)),
        ],
        out_specs=pl.BlockSpec((ts, D), tok),
        out_shape=jax.ShapeDtypeStruct((T, D), F32),
        scratch_shapes=[pltpu.VMEM(row_shape, F32) for _ in range(TOP_K)] + [pltpu.SemaphoreType.DMA],
        compiler_params=_cparams(("arbitrary",)),
        name="combine",
    )(dest_flat, y_rows, x1, gates, final_g.reshape(1, D))


def kernel(x, norm1_g, w_in, q_norm_g, k_norm_g, conv_w, conv_b, lru_wa, lru_ba, lru_wi, lru_bi,
           lru_lam, attn_out_g, lru_out_g, w_out, norm2_g, w_router, b_router, w_gate, b_gate,
           w_up, b_up, w_down, b_down, final_g):
    B, S, D = x.shape
    T = B * S
    assert w_in.shape[0] == 1, "single-layer trunk: the final norm is fused into the layer's combine"
    x2 = x.reshape(T, D)
    for l in range(1):
        qt, k, vt, lru_x, lru_gate = _inproj(x2, norm1_g[l], w_in[l], q_norm_g[l], k_norm_g[l], S)
        attn = _attention(qt, k.reshape(B, S, -1), vt, B, S)
        lru = _lru(lru_x.reshape(B, S, -1), lru_gate.reshape(B, S, -1), conv_w[l], conv_b[l],
                   lru_wa[l], lru_ba[l], lru_wi[l], lru_bi[l], lru_lam[l], B, S)
        x1, xn3, route, gates, cnt = _outproj_router(
            attn.reshape(T, -1), lru.reshape(T, -1), x2, attn_out_g[l], lru_out_g[l], w_out[l],
            norm2_g[l], w_router[l], b_router[l])

        idx = route[:, :TOP_K]
        rank = route[:, TOP_K:2 * TOP_K]
        counts = cnt[0, :N_EXPERTS].astype(jnp.int32)
        padded = ((counts + ROW_BLOCK - 1) // ROW_BLOCK) * ROW_BLOCK
        pend = jnp.cumsum(padded)
        pstart = pend - padded
        dest = (pstart[idx] + rank).reshape(T * TOP_K).astype(jnp.int32)
        n_rows = T * TOP_K + N_EXPERTS * ROW_BLOCK
        n_blocks = n_rows // ROW_BLOCK
        block_e = jnp.searchsorted(pend, jnp.arange(n_blocks, dtype=jnp.int32) * ROW_BLOCK, side='right')
        block_e = jnp.minimum(block_e, N_EXPERTS - 1).astype(jnp.int32)
        n_active = (pend[-1:] // ROW_BLOCK).astype(jnp.int32)

        x_rows = _dispatch(xn3, dest, n_rows)
        y_rows = _experts(x_rows, block_e, n_active, w_gate[l], b_gate[l], w_up[l], b_up[l],
                          w_down[l], b_down[l])
        x2 = _combine(y_rows, dest, x1, gates, final_g)
    return x2.reshape(B, S, D)
```

```python
import functools

import jax
import jax.numpy as jnp
from jax import lax
from jax.experimental import pallas as pl
from jax.experimental.pallas import tpu as pltpu

F32 = jnp.float32
BF16 = jnp.bfloat16

GRID_W = 64
HEAD_DIM = 64
N_Q_HEADS = 8
N_KV_HEADS = 2
GQA_GROUP = N_Q_HEADS // N_KV_HEADS
ATTN_W = N_Q_HEADS * HEAD_DIM
KV_W = N_KV_HEADS * HEAD_DIM
LRU_BLOCKS = 8
LRU_C = 8.0
CONV_W = 4
CONV_PAD_L = 2
ROPE_THETA = 10000.0
ROPE_HALF = HEAD_DIM // 2
ROPE_M = ROPE_HALF // 2
N_EXPERTS = 32
TOP_K = 4
SWIGLU_ALPHA = 1.702
SWIGLU_LIMIT = 7.0
NORM_EPS = 1e-5
QK_EPS = 1e-6
LOG2_E = 1.4426950408889634
Q_SCALE = HEAD_DIM ** -0.5 * LOG2_E

LANES = 128
SUBLANES = 8
VMEM_LIMIT = 48 * 1024 * 1024

TS_IN = 512
TQ = 256
TK = 1024
TC_LRU = 512
TS_OUT = 512
ROW_BLOCK = 256
TS_DISP = 512
TS_COMB = 256


def _cparams(sem):
    return pltpu.CompilerParams(dimension_semantics=sem, vmem_limit_bytes=VMEM_LIMIT)


def _inproj_kernel(x_ref, g1_ref, w_ref, qg_ref, kg_ref, cos_ref, sin_ref,
                   q_ref, k_ref, v_ref, lx_ref, lg_ref, *, lru_w):
    x = x_ref[...]
    ms = jnp.mean(x * x, axis=-1, keepdims=True)
    xn = x * lax.rsqrt(ms + NORM_EPS) * g1_ref[...]
    h = jnp.dot(xn.astype(BF16), w_ref[...], preferred_element_type=F32)

    cos = cos_ref[...]
    sin = sin_ref[...]
    lane = lax.broadcasted_iota(jnp.int32, cos.shape, 1)
    first_half = (lane % ROPE_HALF) < ROPE_M

    def head_norm_rope(xc, g, scale):
        hms = jnp.sum(xc * xc, axis=-1, keepdims=True) * (1.0 / HEAD_DIM)
        xc = xc * lax.rsqrt(hms + QK_EPS) * g
        partner = jnp.where(first_half,
                            pltpu.roll(xc, LANES - ROPE_M, 1),
                            pltpu.roll(xc, ROPE_M, 1))
        return (xc * cos + partner * sin) * scale

    qw = N_Q_HEADS * LANES
    kw = N_KV_HEADS * LANES
    for c in range(N_Q_HEADS):
        sl = slice(c * LANES, (c + 1) * LANES)
        q_ref[0, sl, :] = head_norm_rope(h[:, sl], qg_ref[...], Q_SCALE).T.astype(BF16)
    for c in range(N_KV_HEADS):
        sl = slice(c * LANES, (c + 1) * LANES)
        k_ref[:, sl] = head_norm_rope(h[:, qw + c * LANES: qw + (c + 1) * LANES],
                                      kg_ref[...], 1.0).astype(BF16)
        vc = h[:, qw + kw + c * LANES: qw + kw + (c + 1) * LANES]
        v_ref[0, sl, :] = jnp.where(lane >= HEAD_DIM, 1.0, vc).T.astype(BF16)
    o = qw + 2 * kw
    lx_ref[...] = h[:, o: o + lru_w]
    lg_ref[...] = h[:, o + lru_w: o + 2 * lru_w]


def _pad_heads(w, n_heads):
    lead = w.shape[:-1]
    w = w.reshape(lead + (n_heads, HEAD_DIM))
    w = jnp.pad(w, [(0, 0)] * len(lead) + [(0, 0), (0, LANES - HEAD_DIM)])
    return w.reshape(lead + (n_heads * LANES,))


def _rope_tables(S):
    t = jnp.arange(S)
    rows = (t // GRID_W).astype(F32)
    cols = (t % GRID_W).astype(F32)
    inv_freq = ROPE_THETA ** (-jnp.arange(ROPE_M, dtype=F32) / ROPE_M)
    ar = rows[:, None] * inv_freq[None, :]
    ac = cols[:, None] * inv_freq[None, :]
    cos = jnp.concatenate([jnp.cos(ar), jnp.cos(ar), jnp.cos(ac), jnp.cos(ac)], axis=-1)
    sin = jnp.concatenate([-jnp.sin(ar), jnp.sin(ar), -jnp.sin(ac), jnp.sin(ac)], axis=-1)
    pad = [(0, 0), (0, LANES - HEAD_DIM)]
    return jnp.pad(cos, pad), jnp.pad(sin, pad)


def _inproj(x2, norm1_g, w_in, q_norm_g, k_norm_g, S):
    T, D = x2.shape
    lru_w = (w_in.shape[1] - ATTN_W - 2 * KV_W) // 2
    o0, o1, o2 = ATTN_W, ATTN_W + KV_W, ATTN_W + 2 * KV_W
    w_all = jnp.concatenate([
        _pad_heads(w_in[:, :o0], N_Q_HEADS),
        _pad_heads(w_in[:, o0:o1], N_KV_HEADS),
        _pad_heads(w_in[:, o1:o2], N_KV_HEADS),
        w_in[:, o2:],
    ], axis=1).astype(BF16)
    qg = _pad_heads(q_norm_g.reshape(1, HEAD_DIM), 1)
    kg = _pad_heads(k_norm_g.reshape(1, HEAD_DIM), 1)
    cos, sin = _rope_tables(S)
    ts = TS_IN
    n_s = S // ts
    qw, kw = N_Q_HEADS * LANES, N_KV_HEADS * LANES
    const = lambda i: (0, 0)
    tok = lambda i: (i, 0)
    pos = lambda i: (i % n_s, 0)
    tposed = lambda i: (i // n_s, 0, i % n_s)
    return pl.pallas_call(
        functools.partial(_inproj_kernel, lru_w=lru_w),
        grid=(T // ts,),
        in_specs=[
            pl.BlockSpec((ts, D), tok),
            pl.BlockSpec((1, D), const),
            pl.BlockSpec(w_all.shape, const),
            pl.BlockSpec((1, LANES), const),
            pl.BlockSpec((1, LANES), const),
            pl.BlockSpec((ts, LANES), pos),
            pl.BlockSpec((ts, LANES), pos),
        ],
        out_specs=[
            pl.BlockSpec((1, qw, ts), tposed),
            pl.BlockSpec((ts, kw), tok),
            pl.BlockSpec((1, kw, ts), tposed),
            pl.BlockSpec((ts, lru_w), tok),
            pl.BlockSpec((ts, lru_w), tok),
        ],
        out_shape=[
            jax.ShapeDtypeStruct((T // S, qw, S), BF16),
            jax.ShapeDtypeStruct((T, kw), BF16),
            jax.ShapeDtypeStruct((T // S, kw, S), BF16),
            jax.ShapeDtypeStruct((T, lru_w), F32),
            jax.ShapeDtypeStruct((T, lru_w), F32),
        ],
        compiler_params=_cparams(("parallel",)),
        name="inproj",
    )(x2, norm1_g.reshape(1, D), w_all, qg, kg, cos, sin)


def _attn_kernel(qt_ref, k_ref, vt_ref, o_ref, acc_ref, s_ref, p_ref, *, tq, tk, n_kv):
    acc_ref[...] = jnp.zeros(acc_ref.shape, F32)

    def scores(j, g):
        off = pl.multiple_of(j * tk, tk)
        return jnp.dot(k_ref[0, pl.ds(off, tk), :], qt_ref[0, g * LANES:(g + 1) * LANES, :],
                       preferred_element_type=F32)

    def softmax_stage(s, m_prev):
        m_new = jnp.maximum(m_prev, jnp.max(s, axis=0, keepdims=True))
        return m_new, jnp.exp2(m_prev - m_new), jnp.exp2(s - m_new).astype(BF16)

    m0 = jnp.full((1, tq), -jnp.inf, F32)
    m_first, alpha_first, p_first = softmax_stage(scores(0, 0), m0)
    p_ref[...] = p_first
    s_ref[...] = scores(0, 1)

    def body(j, carry):
        ms, alpha = list(carry[:GQA_GROUP]), carry[GQA_GROUP]
        off = pl.multiple_of(j * tk, tk)
        vt = vt_ref[0, :, pl.ds(off, tk)]
        s = s_ref[...]
        p = p_ref[...]
        for g in range(GQA_GROUP):
            j_next = jnp.minimum(j + (g + 2) // GQA_GROUP, n_kv - 1)
            s_next = scores(j_next, (g + 2) % GQA_GROUP)
            h = (g + 1) % GQA_GROUP
            ms[h], alpha_next, p_next = softmax_stage(s, ms[h])
            acc_ref[g] = alpha * acc_ref[g] + jnp.dot(vt, p, preferred_element_type=F32)
            s, p, alpha = s_next, p_next, alpha_next
        s_ref[...] = s
        p_ref[...] = p
        return tuple(ms) + (alpha,)

    lax.fori_loop(0, n_kv, body, (m_first,) + (m0,) * (GQA_GROUP - 1) + (alpha_first,))
    row = lax.broadcasted_iota(jnp.int32, (LANES, tq), 0)
    for g in range(GQA_GROUP):
        acc = acc_ref[g]
        o = jnp.where(row < HEAD_DIM, acc / acc[HEAD_DIM:HEAD_DIM + 1, :], 0.0)
        o_ref[0, :, g * LANES:(g + 1) * LANES] = o.T.astype(BF16)


def _attention(qt, k, vt, B, S):
    tq = min(TQ, S)
    tk = min(TK, S)
    gw = GQA_GROUP * LANES
    return pl.pallas_call(
        functools.partial(_attn_kernel, tq=tq, tk=tk, n_kv=S // tk),
        grid=(B, N_KV_HEADS, S // tq),
        in_specs=[
            pl.BlockSpec((1, gw, tq), lambda b, h, i: (b, h, i)),
            pl.BlockSpec((1, S, LANES), lambda b, h, i: (b, 0, h)),
            pl.BlockSpec((1, LANES, S), lambda b, h, i: (b, h, 0)),
        ],
        out_specs=pl.BlockSpec((1, tq, gw), lambda b, h, i: (b, i, h)),
        out_shape=jax.ShapeDtypeStruct((B, S, N_Q_HEADS * LANES), BF16),
        scratch_shapes=[pltpu.VMEM((GQA_GROUP, LANES, tq), F32), pltpu.VMEM((tk, tq), F32),
                        pltpu.VMEM((tk, tq), BF16)],
        compiler_params=_cparams(("parallel", "parallel", "parallel")),
        name="attention",
    )(qt, k, vt)


def _scan_chunk(a, b, reverse):
    n = a.shape[0]
    row = lax.broadcasted_iota(jnp.int32, a.shape, 0)
    d = 1
    while d < n:
        if reverse:
            keep = row < n - d
            shift = n - d
        else:
            keep = row >= d
            shift = d
        a_sh = jnp.where(keep, pltpu.roll(a, shift, 0), 1.0)
        b_sh = jnp.where(keep, pltpu.roll(b, shift, 0), 0.0)
        b = a * b_sh + b
        a = a * a_sh
        d *= 2
    return a, b


def _lru_kernel(u_ref, gate_ref, cw_ref, cb_ref, w_ref, bias_ref, lam_ref, o_ref,
                up_ref, hf_ref, *, S, tc):
    halo = SUBLANES
    zeros = jnp.zeros((halo, LANES), F32)
    up_ref[0:halo, :] = zeros
    up_ref[S + halo:S + 2 * halo, :] = zeros
    up_ref[halo:S + halo, :] = u_ref[0]
    sp = jax.nn.softplus(-lam_ref[...])
    cw = cw_ref[...]
    cb = cb_ref[...]
    n_chunks = S // tc
    ext = tc + 2 * halo

    def gates(c, d):
        t0 = pl.multiple_of(c * tc, tc)
        ue = up_ref[pl.ds(t0, ext), :]
        xc = cb
        for j in range(CONV_W):
            sh = (CONV_PAD_L - j) % ext
            uj = ue if sh == 0 else pltpu.roll(ue, sh, 0)
            xc = xc + uj[halo:halo + tc] * cw[j:j + 1, :]
        gw = 2 * LANES
        g = jnp.dot(xc.astype(BF16), w_ref[0, :, d * gw:(d + 1) * gw],
                    preferred_element_type=F32) + bias_ref[0, :, d * gw:(d + 1) * gw]
        r = jax.nn.sigmoid(g[:, :LANES])
        i = jax.nn.sigmoid(g[:, LANES:])
        log_a = -LRU_C * r * sp[d:d + 1, :]
        a = jnp.exp(log_a)
        b = jnp.sqrt(1.0 - jnp.exp(2.0 * log_a)) * i * xc
        return t0, a, b

    def fwd(c, h):
        t0, a, b = gates(c, 0)
        pa, hb = _scan_chunk(a, b, False)
        hc = hb + pa * h
        hf_ref[pl.ds(t0, tc), :] = hc
        return hc[tc - 1:tc, :]

    lax.fori_loop(0, n_chunks, fwd, jnp.zeros((1, LANES), F32))

    def bwd(ci, h):
        t0, a, b = gates(n_chunks - 1 - ci, 1)
        pa, hb = _scan_chunk(a, b, True)
        hc = hb + pa * h
        gate = gate_ref[0, pl.ds(t0, tc), :]
        o_ref[0, pl.ds(t0, tc), :] = (hf_ref[pl.ds(t0, tc), :] + hc) * jax.nn.gelu(gate)
        return hc[0:1, :]

    lax.fori_loop(0, n_chunks, bwd, jnp.zeros((1, LANES), F32))


def _block_diag_pairs(w):
    nb, bw, _ = w.shape
    w = w.reshape(nb // 2, 2, bw, bw)
    z = jnp.zeros_like(w[:, 0])
    top = jnp.concatenate([w[:, 0], z], axis=-1)
    bot = jnp.concatenate([z, w[:, 1]], axis=-1)
    return jnp.concatenate([top, bot], axis=-2)


def _lru(lru_x, lru_gate, conv_w, conv_b, wa, ba, wi, bi, lam, B, S):
    C = lru_x.shape[-1]
    nc = C // LANES
    tc = min(TC_LRU, S)
    w = jnp.concatenate([_block_diag_pairs(wa[0]), _block_diag_pairs(wi[0]),
                         _block_diag_pairs(wa[1]), _block_diag_pairs(wi[1])], axis=-1).astype(BF16)
    bias = jnp.stack([ba[0].reshape(nc, LANES), bi[0].reshape(nc, LANES),
                      ba[1].reshape(nc, LANES), bi[1].reshape(nc, LANES)], axis=1)
    bias = bias.reshape(nc, 1, 4 * LANES)
    blk = lambda b, c: (b, 0, c)
    return pl.pallas_call(
        functools.partial(_lru_kernel, S=S, tc=tc),
        grid=(B, nc),
        in_specs=[
            pl.BlockSpec((1, S, LANES), blk),
            pl.BlockSpec((1, S, LANES), blk),
            pl.BlockSpec((CONV_W, LANES), lambda b, c: (0, c)),
            pl.BlockSpec((1, LANES), lambda b, c: (0, c)),
            pl.BlockSpec((1, LANES, 4 * LANES), lambda b, c: (c, 0, 0)),
            pl.BlockSpec((1, 1, 4 * LANES), lambda b, c: (c, 0, 0)),
            pl.BlockSpec((2, LANES), lambda b, c: (0, c)),
        ],
        out_specs=pl.BlockSpec((1, S, LANES), blk),
        out_shape=jax.ShapeDtypeStruct((B, S, C), F32),
        scratch_shapes=[
            pltpu.VMEM((S + 2 * SUBLANES, LANES), F32),
            pltpu.VMEM((S, LANES), F32),
        ],
        compiler_params=_cparams(("parallel", "parallel")),
        name="rglru",
    )(lru_x, lru_gate, conv_w, conv_b.reshape(1, C), w, bias, lam)


def _outproj_kernel(a_ref, l_ref, x_ref, ag_ref, lg_ref, wa_ref, wl_ref, g2_ref,
                    wrh_ref, wrl_ref, br_ref, tri_ref,
                    x1_ref, xn3_ref, route_ref, gates_ref, cnt_ref, carry_ref, *, attn_w, lru_w):
    step = pl.program_id(0)

    @pl.when(step == 0)
    def _():
        carry_ref[...] = jnp.zeros_like(carry_ref)

    a = a_ref[...].astype(F32)
    ams = jnp.sum(a * a, axis=-1, keepdims=True) * (1.0 / attn_w)
    an = a * lax.rsqrt(ams + NORM_EPS) * ag_ref[...]
    l = l_ref[...]
    lms = jnp.sum(l * l, axis=-1, keepdims=True) * (1.0 / lru_w)
    ln = l * lax.rsqrt(lms + NORM_EPS) * lg_ref[...]
    mix = (jnp.dot(an.astype(BF16), wa_ref[...], preferred_element_type=F32)
           + jnp.dot(ln.astype(BF16), wl_ref[...], preferred_element_type=F32))
    x1 = x_ref[...] + mix
    x1_ref[...] = x1
    ms = jnp.mean(x1 * x1, axis=-1, keepdims=True)
    xn = x1 * lax.rsqrt(ms + NORM_EPS) * g2_ref[...]
    for s in range(SUBLANES):
        xn3_ref[:, s, :] = xn[:, s * LANES:(s + 1) * LANES]

    hi = xn.astype(BF16)
    lo = (xn - hi.astype(F32)).astype(BF16)
    logits = (jnp.dot(hi, wrh_ref[...], preferred_element_type=F32)
              + jnp.dot(lo, wrh_ref[...], preferred_element_type=F32)
              + jnp.dot(hi, wrl_ref[...], preferred_element_type=F32)) + br_ref[...]
    lane = lax.broadcasted_iota(jnp.int32, logits.shape, 1)
    neg = -jnp.inf
    work = jnp.where(lane < N_EXPERTS, logits, neg)
    sel = jnp.zeros(logits.shape, F32)
    idxs, vals = [], []
    for _ in range(TOP_K):
        m = jnp.max(work, axis=1, keepdims=True)
        idx = jnp.min(jnp.where(work == m, lane, LANES), axis=1, keepdims=True)
        hit = lane == idx
        work = jnp.where(hit, neg, work)
        sel = sel + hit.astype(F32)
        idxs.append(idx)
        vals.append(m)
    es = [jnp.exp(v - vals[0]) for v in vals]
    den = es[0] + es[1] + es[2] + es[3]

    prefix = jnp.dot(tri_ref[...], sel.astype(BF16), preferred_element_type=F32) + carry_ref[...]
    carry_ref[...] = carry_ref[...] + jnp.sum(sel, axis=0, keepdims=True)
    cnt_ref[...] = carry_ref[...]

    route = jnp.zeros(logits.shape, jnp.int32)
    gates = jnp.zeros(logits.shape, F32)
    for k in range(TOP_K):
        rank = jnp.sum(jnp.where(lane == idxs[k], prefix, 0.0), axis=1, keepdims=True).astype(jnp.int32)
        route = jnp.where(lane == k, idxs[k], route)
        route = jnp.where(lane == TOP_K + k, rank, route)
        gates = jnp.where(lane == k, es[k] / den, gates)
    route_ref[...] = route
    gates_ref[...] = gates


def _outproj_router(attn, lru, x2, attn_out_g, lru_out_g, w_out, norm2_g, w_router, b_router):
    T, D = x2.shape
    lru_w = lru.shape[-1]
    ts = min(TS_OUT, T)
    wa = w_out[:ATTN_W].reshape(N_Q_HEADS, HEAD_DIM, D)
    wa = jnp.pad(wa, ((0, 0), (0, LANES - HEAD_DIM), (0, 0))).reshape(N_Q_HEADS * LANES, D).astype(BF16)
    wl = w_out[ATTN_W:].astype(BF16)
    ag = _pad_heads(attn_out_g.reshape(1, ATTN_W), N_Q_HEADS)
    wr = jnp.pad(w_router, ((0, 0), (0, LANES - N_EXPERTS)))
    wrh = wr.astype(BF16)
    wrl = (wr - wrh.astype(F32)).astype(BF16)
    br = jnp.pad(b_router.reshape(1, N_EXPERTS), ((0, 0), (0, LANES - N_EXPERTS)))
    tri = (jnp.arange(ts)[:, None] > jnp.arange(ts)[None, :]).astype(BF16)
    const = lambda i: (0, 0)
    tok = lambda i: (i, 0)
    aw = N_Q_HEADS * LANES
    return pl.pallas_call(
        functools.partial(_outproj_kernel, attn_w=ATTN_W, lru_w=lru_w),
        grid=(T // ts,),
        in_specs=[
            pl.BlockSpec((ts, aw), tok),
            pl.BlockSpec((ts, lru_w), tok),
            pl.BlockSpec((ts, D), tok),
            pl.BlockSpec((1, aw), const),
            pl.BlockSpec((1, lru_w), const),
            pl.BlockSpec((aw, D), const),
            pl.BlockSpec((lru_w, D), const),
            pl.BlockSpec((1, D), const),
            pl.BlockSpec((D, LANES), const),
            pl.BlockSpec((D, LANES), const),
            pl.BlockSpec((1, LANES), const),
            pl.BlockSpec((ts, ts), const),
        ],
        out_specs=[
            pl.BlockSpec((ts, D), tok),
            pl.BlockSpec((ts, SUBLANES, D // SUBLANES), lambda i: (i, 0, 0)),
            pl.BlockSpec((ts, LANES), tok),
            pl.BlockSpec((ts, LANES), tok),
            pl.BlockSpec((1, LANES), const),
        ],
        out_shape=[
            jax.ShapeDtypeStruct((T, D), F32),
            jax.ShapeDtypeStruct((T, SUBLANES, D // SUBLANES), F32),
            jax.ShapeDtypeStruct((T, LANES), jnp.int32),
            jax.ShapeDtypeStruct((T, LANES), F32),
            jax.ShapeDtypeStruct((1, LANES), F32),
        ],
        scratch_shapes=[pltpu.VMEM((1, LANES), F32)],
        compiler_params=_cparams(("arbitrary",)),
        name="outproj_router",
    )(attn, lru, x2, ag, lru_out_g.reshape(1, lru_w), wa, wl, norm2_g.reshape(1, D),
      wrh, wrl, br, tri)


def _dispatch_kernel(dest_ref, x_ref, init_hbm, out_hbm, sem, *, ts):
    del init_hbm

    def issue(r, carry):
        for k in range(TOP_K):
            d = dest_ref[r * TOP_K + k]
            pltpu.make_async_copy(x_ref.at[r], out_hbm.at[d], sem).start()
        return carry

    lax.fori_loop(0, ts, issue, 0)
    for k in range(TOP_K):
        pltpu.make_async_copy(x_ref, out_hbm.at[pl.ds(0, ts)], sem).wait()


def _dispatch(xn3, dest_flat, n_rows):
    T = xn3.shape[0]
    ts = min(TS_DISP, T)
    init = jnp.zeros((n_rows,) + xn3.shape[1:], xn3.dtype)
    return pl.pallas_call(
        functools.partial(_dispatch_kernel, ts=ts),
        grid=(T // ts,),
        in_specs=[
            pl.BlockSpec((ts * TOP_K,), lambda i: (i,), memory_space=pltpu.SMEM),
            pl.BlockSpec((ts,) + xn3.shape[1:], lambda i: (i, 0, 0)),
            pl.BlockSpec(memory_space=pl.ANY),
        ],
        out_specs=pl.BlockSpec(memory_space=pl.ANY),
        out_shape=jax.ShapeDtypeStruct(init.shape, init.dtype),
        scratch_shapes=[pltpu.SemaphoreType.DMA],
        input_output_aliases={2: 0},
        compiler_params=_cparams(("arbitrary",)),
        name="dispatch",
    )(dest_flat, xn3, init)


def _expert_kernel(be_ref, na_ref, x_ref, wg_ref, bg_ref, wu_ref, bu_ref, wd_ref, bd_ref,
                   y_ref, wgb_ref, wub_ref, wdb_ref, prev_ref):
    i = pl.program_id(0)
    e = be_ref[i]

    @pl.when(i == 0)
    def _():
        prev_ref[0] = -1

    active = i < na_ref[0]

    @pl.when(jnp.logical_and(active, e != prev_ref[0]))
    def _():
        wgb_ref[...] = wg_ref[0].astype(BF16)
        wub_ref[...] = wu_ref[0].astype(BF16)
        wdb_ref[...] = wd_ref[0].astype(BF16)
        prev_ref[0] = e

    @pl.when(active)
    def _():
        x = jnp.concatenate([x_ref[:, s, :] for s in range(SUBLANES)], axis=1).astype(BF16)
        g = jnp.dot(x, wgb_ref[...], preferred_element_type=F32) + bg_ref[0]
        u = jnp.dot(x, wub_ref[...], preferred_element_type=F32) + bu_ref[0]
        g = jnp.minimum(g, SWIGLU_LIMIT)
        u = jnp.clip(u, -SWIGLU_LIMIT, SWIGLU_LIMIT)
        glu = g * jax.nn.sigmoid(SWIGLU_ALPHA * g)
        y = jnp.dot(((u + 1.0) * glu).astype(BF16), wdb_ref[...], preferred_element_type=F32) + bd_ref[0]
        for s in range(SUBLANES):
            y_ref[:, s, :] = y[:, s * LANES:(s + 1) * LANES]


def _experts(x_rows, block_e, n_active, w_gate, b_gate, w_up, b_up, w_down, b_down):
    n_rows = x_rows.shape[0]
    E, D, FF = w_gate.shape
    n_blocks = n_rows // ROW_BLOCK

    def row_map(i, be, na):
        return (jnp.minimum(i, na[0] - 1), 0, 0)

    def w_map(i, be, na):
        return (be[jnp.minimum(i, na[0] - 1)], 0, 0)

    grid_spec = pltpu.PrefetchScalarGridSpec(
        num_scalar_prefetch=2,
        grid=(n_blocks,),
        in_specs=[
            pl.BlockSpec((ROW_BLOCK,) + x_rows.shape[1:], row_map),
            pl.BlockSpec((1, D, FF), w_map),
            pl.BlockSpec((1, 1, FF), w_map),
            pl.BlockSpec((1, D, FF), w_map),
            pl.BlockSpec((1, 1, FF), w_map),
            pl.BlockSpec((1, FF, D), w_map),
            pl.BlockSpec((1, 1, D), w_map),
        ],
        out_specs=pl.BlockSpec((ROW_BLOCK,) + x_rows.shape[1:], row_map),
        scratch_shapes=[
            pltpu.VMEM((D, FF), BF16),
            pltpu.VMEM((D, FF), BF16),
            pltpu.VMEM((FF, D), BF16),
            pltpu.SMEM((1,), jnp.int32),
        ],
    )
    return pl.pallas_call(
        _expert_kernel,
        grid_spec=grid_spec,
        out_shape=jax.ShapeDtypeStruct(x_rows.shape, F32),
        input_output_aliases={2: 0},
        compiler_params=_cparams(("arbitrary",)),
        name="experts",
    )(block_e, n_active, x_rows, w_gate, b_gate.reshape(E, 1, FF), w_up, b_up.reshape(E, 1, FF),
      w_down, b_down.reshape(E, 1, D))


def _combine_kernel(dest_ref, y_hbm, x1_ref, gates_ref, fg_ref, o_ref,
                    b0, b1, b2, b3, sem, *, ts):
    bufs = (b0, b1, b2, b3)

    def issue(r, carry):
        for k in range(TOP_K):
            d = dest_ref[r * TOP_K + k]
            pltpu.make_async_copy(y_hbm.at[d], bufs[k].at[r], sem).start()
        return carry

    lax.fori_loop(0, ts, issue, 0)
    for k in range(TOP_K):
        pltpu.make_async_copy(y_hbm.at[pl.ds(0, ts)], bufs[k], sem).wait()

    acc = x1_ref[...]
    gates = gates_ref[...]
    for k in range(TOP_K):
        yk = jnp.concatenate([bufs[k][:, s, :] for s in range(SUBLANES)], axis=1)
        acc = acc + yk * gates[:, k:k + 1]
    ms = jnp.mean(acc * acc, axis=-1, keepdims=True)
    o_ref[...] = acc * lax.rsqrt(ms + NORM_EPS) * fg_ref[...]


def _combine(y_rows, dest_flat, x1, gates, final_g):
    T, D = x1.shape
    ts = min(TS_COMB, T)
    tok = lambda i: (i, 0)
    row_shape = (ts,) + y_rows.shape[1:]
    return pl.pallas_call(
        functools.partial(_combine_kernel, ts=ts),
        grid=(T // ts,),
        in_specs=[
            pl.BlockSpec((ts * TOP_K,), lambda i: (i,), memory_space=pltpu.SMEM),
            pl.BlockSpec(memory_space=pl.ANY),
            pl.BlockSpec((ts, D), tok),
            pl.BlockSpec((ts, LANES), tok),
            pl.BlockSpec((1, D), lambda i: (0, 0)),
        ],
        out_specs=pl.BlockSpec((ts, D), tok),
        out_shape=jax.ShapeDtypeStruct((T, D), F32),
        scratch_shapes=[pltpu.VMEM(row_shape, F32) for _ in range(TOP_K)] + [pltpu.SemaphoreType.DMA],
        compiler_params=_cparams(("arbitrary",)),
        name="combine",
    )(dest_flat, y_rows, x1, gates, final_g.reshape(1, D))


def kernel(x, norm1_g, w_in, q_norm_g, k_norm_g, conv_w, conv_b, lru_wa, lru_ba, lru_wi, lru_bi,
           lru_lam, attn_out_g, lru_out_g, w_out, norm2_g, w_router, b_router, w_gate, b_gate,
           w_up, b_up, w_down, b_down, final_g):
    B, S, D = x.shape
    T = B * S
    assert w_in.shape[0] == 1, "single-layer trunk: the final norm is fused into the layer's combine"
    x2 = x.reshape(T, D)
    for l in range(1):
        qt, k, vt, lru_x, lru_gate = _inproj(x2, norm1_g[l], w_in[l], q_norm_g[l], k_norm_g[l], S)
        attn = _attention(qt, k.reshape(B, S, -1), vt, B, S)
        lru = _lru(lru_x.reshape(B, S, -1), lru_gate.reshape(B, S, -1), conv_w[l], conv_b[l],
                   lru_wa[l], lru_ba[l], lru_wi[l], lru_bi[l], lru_lam[l], B, S)
        x1, xn3, route, gates, cnt = _outproj_router(
            attn.reshape(T, -1), lru.reshape(T, -1), x2, attn_out_g[l], lru_out_g[l], w_out[l],
            norm2_g[l], w_router[l], b_router[l])

        idx = route[:, :TOP_K]
        rank = route[:, TOP_K:2 * TOP_K]
        counts = cnt[0, :N_EXPERTS].astype(jnp.int32)
        padded = ((counts + ROW_BLOCK - 1) // ROW_BLOCK) * ROW_BLOCK
        pend = jnp.cumsum(padded)
        pstart = pend - padded
        dest = (pstart[idx] + rank).reshape(T * TOP_K).astype(jnp.int32)
        n_rows = T * TOP_K + N_EXPERTS * ROW_BLOCK
        n_blocks = n_rows // ROW_BLOCK
        block_e = jnp.searchsorted(pend, jnp.arange(n_blocks, dtype=jnp.int32) * ROW_BLOCK, side='right')
        block_e = jnp.minimum(block_e, N_EXPERTS - 1).astype(jnp.int32)
        n_active = (pend[-1:] // ROW_BLOCK).astype(jnp.int32)

        x_rows = _dispatch(xn3, dest, n_rows)
        y_rows = _experts(x_rows, block_e, n_active, w_gate[l], b_gate[l], w_up[l], b_up[l],
                          w_down[l], b_down[l])
        x2 = _combine(y_rows, dest, x1, gates, final_g)
    return x2.reshape(B, S, D)
```

```python
import functools

import jax
import jax.numpy as jnp
from jax import lax
from jax.experimental import pallas as pl
from jax.experimental.pallas import tpu as pltpu

F32 = jnp.float32
BF16 = jnp.bfloat16

GRID_W = 64
HEAD_DIM = 64
N_Q_HEADS = 8
N_KV_HEADS = 2
GQA_GROUP = N_Q_HEADS // N_KV_HEADS
ATTN_W = N_Q_HEADS * HEAD_DIM
KV_W = N_KV_HEADS * HEAD_DIM
LRU_BLOCKS = 8
LRU_C = 8.0
CONV_W = 4
CONV_PAD_L = 2
ROPE_THETA = 10000.0
ROPE_HALF = HEAD_DIM // 2
ROPE_M = ROPE_HALF // 2
N_EXPERTS = 32
TOP_K = 4
SWIGLU_ALPHA = 1.702
SWIGLU_LIMIT = 7.0
NORM_EPS = 1e-5
QK_EPS = 1e-6
LOG2_E = 1.4426950408889634
Q_SCALE = HEAD_DIM ** -0.5 * LOG2_E

LANES = 128
SUBLANES = 8
VMEM_LIMIT = 48 * 1024 * 1024

TS_IN = 512
TQ = 256
TK = 1024
TC_LRU = 512
TS_OUT = 512
ROW_BLOCK = 256
TS_DISP = 512
TS_COMB = 256


def _cparams(sem):
    return pltpu.CompilerParams(dimension_semantics=sem, vmem_limit_bytes=VMEM_LIMIT)


def _inproj_kernel(x_ref, g1_ref, w_ref, qg_ref, kg_ref, cos_ref, sin_ref,
                   q_ref, k_ref, v_ref, lx_ref, lg_ref, *, lru_w):
    x = x_ref[...]
    ms = jnp.mean(x * x, axis=-1, keepdims=True)
    xn = x * lax.rsqrt(ms + NORM_EPS) * g1_ref[...]
    h = jnp.dot(xn.astype(BF16), w_ref[...], preferred_element_type=F32)

    cos = cos_ref[...]
    sin = sin_ref[...]
    lane = lax.broadcasted_iota(jnp.int32, cos.shape, 1)
    first_half = (lane % ROPE_HALF) < ROPE_M

    def head_norm_rope(xc, g, scale):
        hms = jnp.sum(xc * xc, axis=-1, keepdims=True) * (1.0 / HEAD_DIM)
        xc = xc * lax.rsqrt(hms + QK_EPS) * g
        partner = jnp.where(first_half,
                            pltpu.roll(xc, LANES - ROPE_M, 1),
                            pltpu.roll(xc, ROPE_M, 1))
        return (xc * cos + partner * sin) * scale

    qw = N_Q_HEADS * LANES
    kw = N_KV_HEADS * LANES
    for c in range(N_Q_HEADS):
        sl = slice(c * LANES, (c + 1) * LANES)
        q_ref[0, sl, :] = head_norm_rope(h[:, sl], qg_ref[...], Q_SCALE).T.astype(BF16)
    for c in range(N_KV_HEADS):
        sl = slice(c * LANES, (c + 1) * LANES)
        k_ref[:, sl] = head_norm_rope(h[:, qw + c * LANES: qw + (c + 1) * LANES],
                                      kg_ref[...], 1.0).astype(BF16)
        vc = h[:, qw + kw + c * LANES: qw + kw + (c + 1) * LANES]
        v_ref[0, sl, :] = jnp.where(lane >= HEAD_DIM, 1.0, vc).T.astype(BF16)
    o = qw + 2 * kw
    lx_ref[...] = h[:, o: o + lru_w]
    lg_ref[...] = h[:, o + lru_w: o + 2 * lru_w]


def _pad_heads(w, n_heads):
    lead = w.shape[:-1]
    w = w.reshape(lead + (n_heads, HEAD_DIM))
    w = jnp.pad(w, [(0, 0)] * len(lead) + [(0, 0), (0, LANES - HEAD_DIM)])
    return w.reshape(lead + (n_heads * LANES,))


def _rope_tables(S):
    t = jnp.arange(S)
    rows = (t // GRID_W).astype(F32)
    cols = (t % GRID_W).astype(F32)
    inv_freq = ROPE_THETA ** (-jnp.arange(ROPE_M, dtype=F32) / ROPE_M)
    ar = rows[:, None] * inv_freq[None, :]
    ac = cols[:, None] * inv_freq[None, :]
    cos = jnp.concatenate([jnp.cos(ar), jnp.cos(ar), jnp.cos(ac), jnp.cos(ac)], axis=-1)
    sin = jnp.concatenate([-jnp.sin(ar), jnp.sin(ar), -jnp.sin(ac), jnp.sin(ac)], axis=-1)
    pad = [(0, 0), (0, LANES - HEAD_DIM)]
    return jnp.pad(cos, pad), jnp.pad(sin, pad)


def _inproj(x2, norm1_g, w_in, q_norm_g, k_norm_g, S):
    T, D = x2.shape
    lru_w = (w_in.shape[1] - ATTN_W - 2 * KV_W) // 2
    o0, o1, o2 = ATTN_W, ATTN_W + KV_W, ATTN_W + 2 * KV_W
    w_all = jnp.concatenate([
        _pad_heads(w_in[:, :o0], N_Q_HEADS),
        _pad_heads(w_in[:, o0:o1], N_KV_HEADS),
        _pad_heads(w_in[:, o1:o2], N_KV_HEADS),
        w_in[:, o2:],
    ], axis=1).astype(BF16)
    qg = _pad_heads(q_norm_g.reshape(1, HEAD_DIM), 1)
    kg = _pad_heads(k_norm_g.reshape(1, HEAD_DIM), 1)
    cos, sin = _rope_tables(S)
    ts = TS_IN
    n_s = S // ts
    qw, kw = N_Q_HEADS * LANES, N_KV_HEADS * LANES
    const = lambda i: (0, 0)
    tok = lambda i: (i, 0)
    pos = lambda i: (i % n_s, 0)
    tposed = lambda i: (i // n_s, 0, i % n_s)
    return pl.pallas_call(
        functools.partial(_inproj_kernel, lru_w=lru_w),
        grid=(T // ts,),
        in_specs=[
            pl.BlockSpec((ts, D), tok),
            pl.BlockSpec((1, D), const),
            pl.BlockSpec(w_all.shape, const),
            pl.BlockSpec((1, LANES), const),
            pl.BlockSpec((1, LANES), const),
            pl.BlockSpec((ts, LANES), pos),
            pl.BlockSpec((ts, LANES), pos),
        ],
        out_specs=[
            pl.BlockSpec((1, qw, ts), tposed),
            pl.BlockSpec((ts, kw), tok),
            pl.BlockSpec((1, kw, ts), tposed),
            pl.BlockSpec((ts, lru_w), tok),
            pl.BlockSpec((ts, lru_w), tok),
        ],
        out_shape=[
            jax.ShapeDtypeStruct((T // S, qw, S), BF16),
            jax.ShapeDtypeStruct((T, kw), BF16),
            jax.ShapeDtypeStruct((T // S, kw, S), BF16),
            jax.ShapeDtypeStruct((T, lru_w), F32),
            jax.ShapeDtypeStruct((T, lru_w), F32),
        ],
        compiler_params=_cparams(("parallel",)),
        name="inproj",
    )(x2, norm1_g.reshape(1, D), w_all, qg, kg, cos, sin)


def _attn_kernel(qt_ref, k_ref, vt_ref, o_ref, acc_ref, s_ref, p_ref, *, tq, tk, n_kv):
    acc_ref[...] = jnp.zeros(acc_ref.shape, F32)

    def scores(j, g):
        off = pl.multiple_of(j * tk, tk)
        return jnp.dot(k_ref[0, pl.ds(off, tk), :], qt_ref[0, g * LANES:(g + 1) * LANES, :],
                       preferred_element_type=F32)

    def softmax_stage(s, m_prev):
        m_new = jnp.maximum(m_prev, jnp.max(s, axis=0, keepdims=True))
        return m_new, jnp.exp2(m_prev - m_new), jnp.exp2(s - m_new).astype(BF16)

    m0 = jnp.full((1, tq), -jnp.inf, F32)
    m_first, alpha_first, p_first = softmax_stage(scores(0, 0), m0)
    p_ref[...] = p_first
    s_ref[...] = scores(0, 1)

    def body(j, carry):
        ms, alpha = list(carry[:GQA_GROUP]), carry[GQA_GROUP]
        off = pl.multiple_of(j * tk, tk)
        vt = vt_ref[0, :, pl.ds(off, tk)]
        s = s_ref[...]
        p = p_ref[...]
        for g in range(GQA_GROUP):
            j_next = jnp.minimum(j + (g + 2) // GQA_GROUP, n_kv - 1)
            s_next = scores(j_next, (g + 2) % GQA_GROUP)
            h = (g + 1) % GQA_GROUP
            ms[h], alpha_next, p_next = softmax_stage(s, ms[h])
            acc_ref[g] = alpha * acc_ref[g] + jnp.dot(vt, p, preferred_element_type=F32)
            s, p, alpha = s_next, p_next, alpha_next
        s_ref[...] = s
        p_ref[...] = p
        return tuple(ms) + (alpha,)

    lax.fori_loop(0, n_kv, body, (m_first,) + (m0,) * (GQA_GROUP - 1) + (alpha_first,))
    row = lax.broadcasted_iota(jnp.int32, (LANES, tq), 0)
    for g in range(GQA_GROUP):
        acc = acc_ref[g]
        o = jnp.where(row < HEAD_DIM, acc / acc[HEAD_DIM:HEAD_DIM + 1, :], 0.0)
        o_ref[0, :, g * LANES:(g + 1) * LANES] = o.T.astype(BF16)


def _attention(qt, k, vt, B, S):
    tq = min(TQ, S)
    tk = min(TK, S)
    gw = GQA_GROUP * LANES
    return pl.pallas_call(
        functools.partial(_attn_kernel, tq=tq, tk=tk, n_kv=S // tk),
        grid=(B, N_KV_HEADS, S // tq),
        in_specs=[
            pl.BlockSpec((1, gw, tq), lambda b, h, i: (b, h, i)),
            pl.BlockSpec((1, S, LANES), lambda b, h, i: (b, 0, h)),
            pl.BlockSpec((1, LANES, S), lambda b, h, i: (b, h, 0)),
        ],
        out_specs=pl.BlockSpec((1, tq, gw), lambda b, h, i: (b, i, h)),
        out_shape=jax.ShapeDtypeStruct((B, S, N_Q_HEADS * LANES), BF16),
        scratch_shapes=[pltpu.VMEM((GQA_GROUP, LANES, tq), F32), pltpu.VMEM((tk, tq), F32),
                        pltpu.VMEM((tk, tq), BF16)],
        compiler_params=_cparams(("parallel", "parallel", "parallel")),
        name="attention",
    )(qt, k, vt)


def _scan_chunk(a, b, reverse):
    n = a.shape[0]
    row = lax.broadcasted_iota(jnp.int32, a.shape, 0)
    d = 1
    while d < n:
        if reverse:
            keep = row < n - d
            shift = n - d
        else:
            keep = row >= d
            shift = d
        a_sh = jnp.where(keep, pltpu.roll(a, shift, 0), 1.0)
        b_sh = jnp.where(keep, pltpu.roll(b, shift, 0), 0.0)
        b = a * b_sh + b
        a = a * a_sh
        d *= 2
    return a, b


def _lru_kernel(u_ref, gate_ref, cw_ref, cb_ref, w_ref, bias_ref, lam_ref, o_ref,
                up_ref, hf_ref, *, S, tc):
    halo = SUBLANES
    zeros = jnp.zeros((halo, LANES), F32)
    up_ref[0:halo, :] = zeros
    up_ref[S + halo:S + 2 * halo, :] = zeros
    up_ref[halo:S + halo, :] = u_ref[0]
    sp = jax.nn.softplus(-lam_ref[...])
    cw = cw_ref[...]
    cb = cb_ref[...]
    n_chunks = S // tc
    ext = tc + 2 * halo

    def gates(c, d):
        t0 = pl.multiple_of(c * tc, tc)
        ue = up_ref[pl.ds(t0, ext), :]
        xc = cb
        for j in range(CONV_W):
            sh = (CONV_PAD_L - j) % ext
            uj = ue if sh == 0 else pltpu.roll(ue, sh, 0)
            xc = xc + uj[halo:halo + tc] * cw[j:j + 1, :]
        gw = 2 * LANES
        g = jnp.dot(xc.astype(BF16), w_ref[0, :, d * gw:(d + 1) * gw],
                    preferred_element_type=F32) + bias_ref[0, :, d * gw:(d + 1) * gw]
        r = jax.nn.sigmoid(g[:, :LANES])
        i = jax.nn.sigmoid(g[:, LANES:])
        log_a = -LRU_C * r * sp[d:d + 1, :]
        a = jnp.exp(log_a)
        b = jnp.sqrt(1.0 - jnp.exp(2.0 * log_a)) * i * xc
        return t0, a, b

    def fwd(c, h):
        t0, a, b = gates(c, 0)
        pa, hb = _scan_chunk(a, b, False)
        hc = hb + pa * h
        hf_ref[pl.ds(t0, tc), :] = hc
        return hc[tc - 1:tc, :]

    lax.fori_loop(0, n_chunks, fwd, jnp.zeros((1, LANES), F32))

    def bwd(ci, h):
        t0, a, b = gates(n_chunks - 1 - ci, 1)
        pa, hb = _scan_chunk(a, b, True)
        hc = hb + pa * h
        gate = gate_ref[0, pl.ds(t0, tc), :]
        o_ref[0, pl.ds(t0, tc), :] = (hf_ref[pl.ds(t0, tc), :] + hc) * jax.nn.gelu(gate)
        return hc[0:1, :]

    lax.fori_loop(0, n_chunks, bwd, jnp.zeros((1, LANES), F32))


def _block_diag_pairs(w):
    nb, bw, _ = w.shape
    w = w.reshape(nb // 2, 2, bw, bw)
    z = jnp.zeros_like(w[:, 0])
    top = jnp.concatenate([w[:, 0], z], axis=-1)
    bot = jnp.concatenate([z, w[:, 1]], axis=-1)
    return jnp.concatenate([top, bot], axis=-2)


def _lru(lru_x, lru_gate, conv_w, conv_b, wa, ba, wi, bi, lam, B, S):
    C = lru_x.shape[-1]
    nc = C // LANES
    tc = min(TC_LRU, S)
    w = jnp.concatenate([_block_diag_pairs(wa[0]), _block_diag_pairs(wi[0]),
                         _block_diag_pairs(wa[1]), _block_diag_pairs(wi[1])], axis=-1).astype(BF16)
    bias = jnp.stack([ba[0].reshape(nc, LANES), bi[0].reshape(nc, LANES),
                      ba[1].reshape(nc, LANES), bi[1].reshape(nc, LANES)], axis=1)
    bias = bias.reshape(nc, 1, 4 * LANES)
    blk = lambda b, c: (b, 0, c)
    return pl.pallas_call(
        functools.partial(_lru_kernel, S=S, tc=tc),
        grid=(B, nc),
        in_specs=[
            pl.BlockSpec((1, S, LANES), blk),
            pl.BlockSpec((1, S, LANES), blk),
            pl.BlockSpec((CONV_W, LANES), lambda b, c: (0, c)),
            pl.BlockSpec((1, LANES), lambda b, c: (0, c)),
            pl.BlockSpec((1, LANES, 4 * LANES), lambda b, c: (c, 0, 0)),
            pl.BlockSpec((1, 1, 4 * LANES), lambda b, c: (c, 0, 0)),
            pl.BlockSpec((2, LANES), lambda b, c: (0, c)),
        ],
        out_specs=pl.BlockSpec((1, S, LANES), blk),
        out_shape=jax.ShapeDtypeStruct((B, S, C), F32),
        scratch_shapes=[
            pltpu.VMEM((S + 2 * SUBLANES, LANES), F32),
            pltpu.VMEM((S, LANES), F32),
        ],
        compiler_params=_cparams(("parallel", "parallel")),
        name="rglru",
    )(lru_x, lru_gate, conv_w, conv_b.reshape(1, C), w, bias, lam)


def _rows_to_slabs(ref, x):
    n = x.shape[0]
    for s in range(SUBLANES):
        ref[pl.ds(s, n, stride=SUBLANES), :] = x[:, s * LANES:(s + 1) * LANES]


def _slabs_to_rows(ref, n):
    return jnp.concatenate([ref[pl.ds(s, n, stride=SUBLANES), :] for s in range(SUBLANES)], axis=1)


def _slab(ref, r):
    return ref.at[pl.ds(pl.multiple_of(r * SUBLANES, SUBLANES), SUBLANES)]


def _outproj_kernel(a_ref, l_ref, x_ref, ag_ref, lg_ref, wa_ref, wl_ref, g2_ref,
                    wrh_ref, wrl_ref, br_ref, tri_ref,
                    x1_ref, xn3_ref, route_ref, gates_ref, cnt_ref, carry_ref, *, attn_w, lru_w):
    step = pl.program_id(0)

    @pl.when(step == 0)
    def _():
        carry_ref[...] = jnp.zeros_like(carry_ref)

    a = a_ref[...].astype(F32)
    ams = jnp.sum(a * a, axis=-1, keepdims=True) * (1.0 / attn_w)
    an = a * lax.rsqrt(ams + NORM_EPS) * ag_ref[...]
    l = l_ref[...]
    lms = jnp.sum(l * l, axis=-1, keepdims=True) * (1.0 / lru_w)
    ln = l * lax.rsqrt(lms + NORM_EPS) * lg_ref[...]
    mix = (jnp.dot(an.astype(BF16), wa_ref[...], preferred_element_type=F32)
           + jnp.dot(ln.astype(BF16), wl_ref[...], preferred_element_type=F32))
    x1 = x_ref[...] + mix
    x1_ref[...] = x1
    ms = jnp.mean(x1 * x1, axis=-1, keepdims=True)
    xn = x1 * lax.rsqrt(ms + NORM_EPS) * g2_ref[...]
    _rows_to_slabs(xn3_ref, xn)

    hi = xn.astype(BF16)
    lo = (xn - hi.astype(F32)).astype(BF16)
    logits = (jnp.dot(hi, wrh_ref[...], preferred_element_type=F32)
              + jnp.dot(lo, wrh_ref[...], preferred_element_type=F32)
              + jnp.dot(hi, wrl_ref[...], preferred_element_type=F32)) + br_ref[...]
    lane = lax.broadcasted_iota(jnp.int32, logits.shape, 1)
    neg = -jnp.inf
    work = jnp.where(lane < N_EXPERTS, logits, neg)
    sel = jnp.zeros(logits.shape, F32)
    idxs, vals = [], []
    for _ in range(TOP_K):
        m = jnp.max(work, axis=1, keepdims=True)
        idx = jnp.min(jnp.where(work == m, lane, LANES), axis=1, keepdims=True)
        hit = lane == idx
        work = jnp.where(hit, neg, work)
        sel = sel + hit.astype(F32)
        idxs.append(idx)
        vals.append(m)
    es = [jnp.exp(v - vals[0]) for v in vals]
    den = es[0] + es[1] + es[2] + es[3]

    prefix = jnp.dot(tri_ref[...], sel.astype(BF16), preferred_element_type=F32) + carry_ref[...]
    carry_ref[...] = carry_ref[...] + jnp.sum(sel, axis=0, keepdims=True)
    cnt_ref[...] = carry_ref[...]

    route = jnp.zeros(logits.shape, jnp.int32)
    gates = jnp.zeros(logits.shape, F32)
    for k in range(TOP_K):
        rank = jnp.sum(jnp.where(lane == idxs[k], prefix, 0.0), axis=1, keepdims=True).astype(jnp.int32)
        route = jnp.where(lane == k, idxs[k], route)
        route = jnp.where(lane == TOP_K + k, rank, route)
        gates = jnp.where(lane == k, es[k] / den, gates)
    route_ref[...] = route
    gates_ref[...] = gates


def _outproj_router(attn, lru, x2, attn_out_g, lru_out_g, w_out, norm2_g, w_router, b_router):
    T, D = x2.shape
    lru_w = lru.shape[-1]
    ts = min(TS_OUT, T)
    wa = w_out[:ATTN_W].reshape(N_Q_HEADS, HEAD_DIM, D)
    wa = jnp.pad(wa, ((0, 0), (0, LANES - HEAD_DIM), (0, 0))).reshape(N_Q_HEADS * LANES, D).astype(BF16)
    wl = w_out[ATTN_W:].astype(BF16)
    ag = _pad_heads(attn_out_g.reshape(1, ATTN_W), N_Q_HEADS)
    wr = jnp.pad(w_router, ((0, 0), (0, LANES - N_EXPERTS)))
    wrh = wr.astype(BF16)
    wrl = (wr - wrh.astype(F32)).astype(BF16)
    br = jnp.pad(b_router.reshape(1, N_EXPERTS), ((0, 0), (0, LANES - N_EXPERTS)))
    tri = (jnp.arange(ts)[:, None] > jnp.arange(ts)[None, :]).astype(BF16)
    const = lambda i: (0, 0)
    tok = lambda i: (i, 0)
    aw = N_Q_HEADS * LANES
    return pl.pallas_call(
        functools.partial(_outproj_kernel, attn_w=ATTN_W, lru_w=lru_w),
        grid=(T // ts,),
        in_specs=[
            pl.BlockSpec((ts, aw), tok),
            pl.BlockSpec((ts, lru_w), tok),
            pl.BlockSpec((ts, D), tok),
            pl.BlockSpec((1, aw), const),
            pl.BlockSpec((1, lru_w), const),
            pl.BlockSpec((aw, D), const),
            pl.BlockSpec((lru_w, D), const),
            pl.BlockSpec((1, D), const),
            pl.BlockSpec((D, LANES), const),
            pl.BlockSpec((D, LANES), const),
            pl.BlockSpec((1, LANES), const),
            pl.BlockSpec((ts, ts), const),
        ],
        out_specs=[
            pl.BlockSpec((ts, D), tok),
            pl.BlockSpec((ts * SUBLANES, LANES), tok),
            pl.BlockSpec((ts, LANES), tok),
            pl.BlockSpec((ts, LANES), tok),
            pl.BlockSpec((1, LANES), const),
        ],
        out_shape=[
            jax.ShapeDtypeStruct((T, D), F32),
            jax.ShapeDtypeStruct((T * SUBLANES, LANES), F32),
            jax.ShapeDtypeStruct((T, LANES), jnp.int32),
            jax.ShapeDtypeStruct((T, LANES), F32),
            jax.ShapeDtypeStruct((1, LANES), F32),
        ],
        scratch_shapes=[pltpu.VMEM((1, LANES), F32)],
        compiler_params=_cparams(("arbitrary",)),
        name="outproj_router",
    )(attn, lru, x2, ag, lru_out_g.reshape(1, lru_w), wa, wl, norm2_g.reshape(1, D),
      wrh, wrl, br, tri)


def _dest_row(pstart_ref, idx_ref, rank_ref, r, k):
    return pstart_ref[idx_ref[r * TOP_K + k]] + rank_ref[r * TOP_K + k]


def _dispatch_kernel(pstart_ref, idx_ref, rank_ref, x_ref, init_hbm, out_hbm, sem, *, ts):
    del init_hbm

    def issue(r, carry):
        for k in range(TOP_K):
            d = _dest_row(pstart_ref, idx_ref, rank_ref, r, k)
            pltpu.make_async_copy(_slab(x_ref, r), _slab(out_hbm, d), sem).start()
        return carry

    lax.fori_loop(0, ts, issue, 0)
    for k in range(TOP_K):
        pltpu.make_async_copy(x_ref, out_hbm.at[pl.ds(0, ts * SUBLANES)], sem).wait()


def _dispatch(xn_slabs, pstart, idx_flat, rank_flat, n_rows):
    T = xn_slabs.shape[0] // SUBLANES
    ts = min(TS_DISP, T)
    init = jnp.zeros((n_rows * SUBLANES, LANES), xn_slabs.dtype)
    routing = pl.BlockSpec((ts * TOP_K,), lambda i, ps: (i,), memory_space=pltpu.SMEM)
    grid_spec = pltpu.PrefetchScalarGridSpec(
        num_scalar_prefetch=1,
        grid=(T // ts,),
        in_specs=[
            routing,
            routing,
            pl.BlockSpec((ts * SUBLANES, LANES), lambda i, ps: (i, 0)),
            pl.BlockSpec(memory_space=pl.ANY),
        ],
        out_specs=pl.BlockSpec(memory_space=pl.ANY),
        scratch_shapes=[pltpu.SemaphoreType.DMA],
    )
    return pl.pallas_call(
        functools.partial(_dispatch_kernel, ts=ts),
        grid_spec=grid_spec,
        out_shape=jax.ShapeDtypeStruct(init.shape, init.dtype),
        input_output_aliases={4: 0},
        compiler_params=_cparams(("arbitrary",)),
        name="dispatch",
    )(pstart, idx_flat, rank_flat, xn_slabs, init)


def _expert_kernel(be_ref, na_ref, x_ref, wg_ref, bg_ref, wu_ref, bu_ref, wd_ref, bd_ref,
                   y_ref, wgb_ref, wub_ref, wdb_ref, prev_ref):
    i = pl.program_id(0)
    e = be_ref[i]

    @pl.when(i == 0)
    def _():
        prev_ref[0] = -1

    active = i < na_ref[0]

    @pl.when(jnp.logical_and(active, e != prev_ref[0]))
    def _():
        wgb_ref[...] = wg_ref[0].astype(BF16)
        wub_ref[...] = wu_ref[0].astype(BF16)
        wdb_ref[...] = wd_ref[0].astype(BF16)
        prev_ref[0] = e

    @pl.when(active)
    def _():
        x = _slabs_to_rows(x_ref, ROW_BLOCK).astype(BF16)
        g = jnp.dot(x, wgb_ref[...], preferred_element_type=F32) + bg_ref[0]
        u = jnp.dot(x, wub_ref[...], preferred_element_type=F32) + bu_ref[0]
        g = jnp.minimum(g, SWIGLU_LIMIT)
        u = jnp.clip(u, -SWIGLU_LIMIT, SWIGLU_LIMIT)
        glu = g * jax.nn.sigmoid(SWIGLU_ALPHA * g)
        y = jnp.dot(((u + 1.0) * glu).astype(BF16), wdb_ref[...], preferred_element_type=F32) + bd_ref[0]
        _rows_to_slabs(y_ref, y)


def _experts(x_rows, block_e, n_active, w_gate, b_gate, w_up, b_up, w_down, b_down):
    E, D, FF = w_gate.shape
    block_slabs = ROW_BLOCK * SUBLANES
    n_blocks = x_rows.shape[0] // block_slabs

    def row_map(i, be, na):
        return (jnp.minimum(i, na[0] - 1), 0)

    def w_map(i, be, na):
        return (be[jnp.minimum(i, na[0] - 1)], 0, 0)

    grid_spec = pltpu.PrefetchScalarGridSpec(
        num_scalar_prefetch=2,
        grid=(n_blocks,),
        in_specs=[
            pl.BlockSpec((block_slabs, LANES), row_map),
            pl.BlockSpec((1, D, FF), w_map),
            pl.BlockSpec((1, 1, FF), w_map),
            pl.BlockSpec((1, D, FF), w_map),
            pl.BlockSpec((1, 1, FF), w_map),
            pl.BlockSpec((1, FF, D), w_map),
            pl.BlockSpec((1, 1, D), w_map),
        ],
        out_specs=pl.BlockSpec((block_slabs, LANES), row_map),
        scratch_shapes=[
            pltpu.VMEM((D, FF), BF16),
            pltpu.VMEM((D, FF), BF16),
            pltpu.VMEM((FF, D), BF16),
            pltpu.SMEM((1,), jnp.int32),
        ],
    )
    return pl.pallas_call(
        _expert_kernel,
        grid_spec=grid_spec,
        out_shape=jax.ShapeDtypeStruct(x_rows.shape, F32),
        input_output_aliases={2: 0},
        compiler_params=_cparams(("arbitrary",)),
        name="experts",
    )(block_e, n_active, x_rows, w_gate, b_gate.reshape(E, 1, FF), w_up, b_up.reshape(E, 1, FF),
      w_down, b_down.reshape(E, 1, D))


def _combine_kernel(pstart_ref, idx_ref, rank_ref, y_hbm, x1_ref, gates_ref, fg_ref, o_ref,
                    b0, b1, b2, b3, sem, *, ts):
    bufs = (b0, b1, b2, b3)

    def issue(r, carry):
        for k in range(TOP_K):
            d = _dest_row(pstart_ref, idx_ref, rank_ref, r, k)
            pltpu.make_async_copy(_slab(y_hbm, d), _slab(bufs[k], r), sem).start()
        return carry

    lax.fori_loop(0, ts, issue, 0)
    for k in range(TOP_K):
        pltpu.make_async_copy(y_hbm.at[pl.ds(0, ts * SUBLANES)], bufs[k], sem).wait()

    acc = x1_ref[...]
    gates = gates_ref[...]
    for k in range(TOP_K):
        acc = acc + _slabs_to_rows(bufs[k], ts) * gates[:, k:k + 1]
    ms = jnp.mean(acc * acc, axis=-1, keepdims=True)
    o_ref[...] = acc * lax.rsqrt(ms + NORM_EPS) * fg_ref[...]


def _combine(y_rows, pstart, idx_flat, rank_flat, x1, gates, final_g):
    T, D = x1.shape
    ts = min(TS_COMB, T)
    tok = lambda i, ps: (i, 0)
    routing = pl.BlockSpec((ts * TOP_K,), lambda i, ps: (i,), memory_space=pltpu.SMEM)
    grid_spec = pltpu.PrefetchScalarGridSpec(
        num_scalar_prefetch=1,
        grid=(T // ts,),
        in_specs=[
            routing,
            routing,
            pl.BlockSpec(memory_space=pl.ANY),
            pl.BlockSpec((ts, D), tok),
            pl.BlockSpec((ts, LANES), tok),
            pl.BlockSpec((1, D), lambda i, ps: (0, 0)),
        ],
        out_specs=pl.BlockSpec((ts, D), tok),
        scratch_shapes=[pltpu.VMEM((ts * SUBLANES, LANES), F32) for _ in range(TOP_K)]
        + [pltpu.SemaphoreType.DMA],
    )
    return pl.pallas_call(
        functools.partial(_combine_kernel, ts=ts),
        grid_spec=grid_spec,
        out_shape=jax.ShapeDtypeStruct((T, D), F32),
        compiler_params=_cparams(("arbitrary",)),
        name="combine",
    )(pstart, idx_flat, rank_flat, y_rows, x1, gates, final_g.reshape(1, D))


def kernel(x, norm1_g, w_in, q_norm_g, k_norm_g, conv_w, conv_b, lru_wa, lru_ba, lru_wi, lru_bi,
           lru_lam, attn_out_g, lru_out_g, w_out, norm2_g, w_router, b_router, w_gate, b_gate,
           w_up, b_up, w_down, b_down, final_g):
    B, S, D = x.shape
    T = B * S
    assert w_in.shape[0] == 1, "single-layer trunk: the final norm is fused into the layer's combine"
    x2 = x.reshape(T, D)
    for l in range(1):
        qt, k, vt, lru_x, lru_gate = _inproj(x2, norm1_g[l], w_in[l], q_norm_g[l], k_norm_g[l], S)
        attn = _attention(qt, k.reshape(B, S, -1), vt, B, S)
        lru = _lru(lru_x.reshape(B, S, -1), lru_gate.reshape(B, S, -1), conv_w[l], conv_b[l],
                   lru_wa[l], lru_ba[l], lru_wi[l], lru_bi[l], lru_lam[l], B, S)
        x1, xn3, route, gates, cnt = _outproj_router(
            attn.reshape(T, -1), lru.reshape(T, -1), x2, attn_out_g[l], lru_out_g[l], w_out[l],
            norm2_g[l], w_router[l], b_router[l])

        idx_flat = route[:, :TOP_K].reshape(T * TOP_K)
        rank_flat = route[:, TOP_K:2 * TOP_K].reshape(T * TOP_K)
        counts = cnt[0, :N_EXPERTS].astype(jnp.int32)
        padded = ((counts + ROW_BLOCK - 1) // ROW_BLOCK) * ROW_BLOCK
        pend = jnp.cumsum(padded)
        pstart = (pend - padded).astype(jnp.int32)
        n_rows = T * TOP_K + N_EXPERTS * ROW_BLOCK
        block_start = jnp.arange(n_rows // ROW_BLOCK, dtype=jnp.int32) * ROW_BLOCK
        block_e = jnp.sum((pend[None, :] <= block_start[:, None]).astype(jnp.int32), axis=1)
        block_e = jnp.minimum(block_e, N_EXPERTS - 1)
        n_active = (pend[-1:] // ROW_BLOCK).astype(jnp.int32)

        x_rows = _dispatch(xn3, pstart, idx_flat, rank_flat, n_rows)
        y_rows = _experts(x_rows, block_e, n_active, w_gate[l], b_gate[l], w_up[l], b_up[l],
                          w_down[l], b_down[l])
        x2 = _combine(y_rows, pstart, idx_flat, rank_flat, x1, gates, final_g)
    return x2.reshape(B, S, D)
```

```python
import functools
import math

import jax
import jax.numpy as jnp
from jax import lax
from jax.experimental import pallas as pl
from jax.experimental.pallas import tpu as pltpu

F32 = jnp.float32
BF16 = jnp.bfloat16

GRID_W = 64
HEAD_DIM = 64
N_Q_HEADS = 8
N_KV_HEADS = 2
GQA_GROUP = N_Q_HEADS // N_KV_HEADS
ATTN_W = N_Q_HEADS * HEAD_DIM
KV_W = N_KV_HEADS * HEAD_DIM
LRU_BLOCKS = 8
LRU_C = 8.0
CONV_W = 4
CONV_PAD_L = 2
ROPE_THETA = 10000.0
ROPE_HALF = HEAD_DIM // 2
ROPE_M = ROPE_HALF // 2
N_EXPERTS = 32
TOP_K = 4
SWIGLU_ALPHA = 1.702
SWIGLU_LIMIT = 7.0
NORM_EPS = 1e-5
QK_EPS = 1e-6
LOG2_E = 1.4426950408889634
Q_SCALE = HEAD_DIM ** -0.5 * LOG2_E

LANES = 128
SUBLANES = 8
BF16_SUBLANES = 16
PV_ROWS = HEAD_DIM + BF16_SUBLANES
VMEM_LIMIT = 48 * 1024 * 1024

TS_IN = 512
TQ = 256
TK = 1024
KV_UNROLL = 1
TC_LRU = 512
TS_OUT = 512
ROW_BLOCK = 256
TS_DISP = 512
TS_COMB = 256
ISSUE_UNROLL = 8


def _cparams(sem):
    return pltpu.CompilerParams(dimension_semantics=sem, vmem_limit_bytes=VMEM_LIMIT)


def _inproj_kernel(x_ref, g1_ref, w_ref, qg_ref, kg_ref, cos_ref, sin_ref,
                   q_ref, k_ref, v_ref, lx_ref, lg_ref, *, lru_w):
    x = x_ref[...]
    ms = jnp.mean(x * x, axis=-1, keepdims=True)
    xn = x * lax.rsqrt(ms + NORM_EPS) * g1_ref[...]
    h = jnp.dot(xn.astype(BF16), w_ref[...], preferred_element_type=F32)

    cos = cos_ref[...]
    sin = sin_ref[...]
    lane = lax.broadcasted_iota(jnp.int32, cos.shape, 1)
    first_half = (lane % ROPE_HALF) < ROPE_M

    def head_norm_rope(xc, g, scale):
        hms = jnp.sum(xc * xc, axis=-1, keepdims=True) * (1.0 / HEAD_DIM)
        xc = xc * lax.rsqrt(hms + QK_EPS) * g
        partner = jnp.where(first_half,
                            pltpu.roll(xc, LANES - ROPE_M, 1),
                            pltpu.roll(xc, ROPE_M, 1))
        return (xc * cos + partner * sin) * scale

    qw = N_Q_HEADS * LANES
    kw = N_KV_HEADS * LANES
    for c in range(N_Q_HEADS):
        sl = slice(c * LANES, (c + 1) * LANES)
        q_ref[0, sl, :] = head_norm_rope(h[:, sl], qg_ref[...], Q_SCALE).T.astype(BF16)
    for c in range(N_KV_HEADS):
        sl = slice(c * LANES, (c + 1) * LANES)
        k_ref[:, sl] = head_norm_rope(h[:, qw + c * LANES: qw + (c + 1) * LANES],
                                      kg_ref[...], 1.0).astype(BF16)
        vc = h[:, qw + kw + c * LANES: qw + kw + (c + 1) * LANES]
        v_ref[0, sl, :] = jnp.where(lane >= HEAD_DIM, 1.0, vc).T.astype(BF16)
    o = qw + 2 * kw
    lx_ref[...] = h[:, o: o + lru_w]
    lg_ref[...] = h[:, o + lru_w: o + 2 * lru_w]


def _pad_heads(w, n_heads):
    lead = w.shape[:-1]
    w = w.reshape(lead + (n_heads, HEAD_DIM))
    w = jnp.pad(w, [(0, 0)] * len(lead) + [(0, 0), (0, LANES - HEAD_DIM)])
    return w.reshape(lead + (n_heads * LANES,))


def _rope_tables(S):
    t = jnp.arange(S)
    rows = (t // GRID_W).astype(F32)
    cols = (t % GRID_W).astype(F32)
    inv_freq = ROPE_THETA ** (-jnp.arange(ROPE_M, dtype=F32) / ROPE_M)
    ar = rows[:, None] * inv_freq[None, :]
    ac = cols[:, None] * inv_freq[None, :]
    cos = jnp.concatenate([jnp.cos(ar), jnp.cos(ar), jnp.cos(ac), jnp.cos(ac)], axis=-1)
    sin = jnp.concatenate([-jnp.sin(ar), jnp.sin(ar), -jnp.sin(ac), jnp.sin(ac)], axis=-1)
    pad = [(0, 0), (0, LANES - HEAD_DIM)]
    return jnp.pad(cos, pad), jnp.pad(sin, pad)


def _inproj(x2, norm1_g, w_in, q_norm_g, k_norm_g, S):
    T, D = x2.shape
    lru_w = (w_in.shape[1] - ATTN_W - 2 * KV_W) // 2
    o0, o1, o2 = ATTN_W, ATTN_W + KV_W, ATTN_W + 2 * KV_W
    w_all = jnp.concatenate([
        _pad_heads(w_in[:, :o0], N_Q_HEADS),
        _pad_heads(w_in[:, o0:o1], N_KV_HEADS),
        _pad_heads(w_in[:, o1:o2], N_KV_HEADS),
        w_in[:, o2:],
    ], axis=1).astype(BF16)
    qg = _pad_heads(q_norm_g.reshape(1, HEAD_DIM), 1)
    kg = _pad_heads(k_norm_g.reshape(1, HEAD_DIM), 1)
    cos, sin = _rope_tables(S)
    ts = TS_IN
    n_s = S // ts
    qw, kw = N_Q_HEADS * LANES, N_KV_HEADS * LANES
    const = lambda i: (0, 0)
    tok = lambda i: (i, 0)
    pos = lambda i: (i % n_s, 0)
    tposed = lambda i: (i // n_s, 0, i % n_s)
    return pl.pallas_call(
        functools.partial(_inproj_kernel, lru_w=lru_w),
        grid=(T // ts,),
        in_specs=[
            pl.BlockSpec((ts, D), tok),
            pl.BlockSpec((1, D), const),
            pl.BlockSpec(w_all.shape, const),
            pl.BlockSpec((1, LANES), const),
            pl.BlockSpec((1, LANES), const),
            pl.BlockSpec((ts, LANES), pos),
            pl.BlockSpec((ts, LANES), pos),
        ],
        out_specs=[
            pl.BlockSpec((1, qw, ts), tposed),
            pl.BlockSpec((ts, kw), tok),
            pl.BlockSpec((1, kw, ts), tposed),
            pl.BlockSpec((ts, lru_w), tok),
            pl.BlockSpec((ts, lru_w), tok),
        ],
        out_shape=[
            jax.ShapeDtypeStruct((T // S, qw, S), BF16),
            jax.ShapeDtypeStruct((T, kw), BF16),
            jax.ShapeDtypeStruct((T // S, kw, S), BF16),
            jax.ShapeDtypeStruct((T, lru_w), F32),
            jax.ShapeDtypeStruct((T, lru_w), F32),
        ],
        compiler_params=_cparams(("parallel",)),
        name="inproj",
    )(x2, norm1_g.reshape(1, D), w_all, qg, kg, cos, sin)


def _attn_kernel(qt_ref, k_ref, vt_ref, o_ref, acc_ref, s_ref, p_ref, *, tq, tk, n_kv, kv_unroll):
    acc_ref[...] = jnp.zeros(acc_ref.shape, F32)

    def scores(j, g):
        off = pl.multiple_of(j * tk, tk)
        s = jnp.dot(k_ref[0, pl.ds(off, tk), :], qt_ref[0, g * LANES:(g + 1) * LANES, :],
                    preferred_element_type=F32)
        return s, jnp.max(s, axis=0, keepdims=True)

    def softmax_stage(s, s_max, m_prev):
        m_new = jnp.maximum(m_prev, s_max)
        return m_new, jnp.exp2(m_prev - m_new), jnp.exp2((s - m_new).astype(BF16))

    m0 = jnp.full((1, tq), -jnp.inf, F32)
    m_first, alpha_first, p_first = softmax_stage(*scores(0, 0), m0)
    p_ref[...] = p_first
    s_second, smax_second = scores(0, 1)
    s_ref[...] = s_second

    def body(it, carry):
        ms, alpha, s_max = list(carry[:GQA_GROUP]), carry[GQA_GROUP], carry[GQA_GROUP + 1]
        s = s_ref[...]
        p = p_ref[...]
        for n in range(kv_unroll * GQA_GROUP):
            j = it * kv_unroll + n // GQA_GROUP
            g = n % GQA_GROUP
            off = pl.multiple_of(j * tk, tk)
            j_next = jnp.minimum(it * kv_unroll + (n + 2) // GQA_GROUP, n_kv - 1)
            s_next, smax_next = scores(j_next, (n + 2) % GQA_GROUP)
            h = (n + 1) % GQA_GROUP
            ms[h], alpha_next, p_next = softmax_stage(s, s_max, ms[h])
            acc_ref[g] = alpha * acc_ref[g] + jnp.dot(vt_ref[0, 0:PV_ROWS, pl.ds(off, tk)], p,
                                                      preferred_element_type=F32)
            s, s_max, p, alpha = s_next, smax_next, p_next, alpha_next
        s_ref[...] = s
        p_ref[...] = p
        return tuple(ms) + (alpha, s_max)

    lax.fori_loop(0, n_kv // kv_unroll, body,
                  (m_first,) + (m0,) * (GQA_GROUP - 1) + (alpha_first, smax_second))
    pad = jnp.zeros((LANES - HEAD_DIM, tq), F32)
    for g in range(GQA_GROUP):
        acc = acc_ref[g]
        o = acc[0:HEAD_DIM] / acc[HEAD_DIM:HEAD_DIM + 1, :]
        o_ref[0, :, g * LANES:(g + 1) * LANES] = jnp.concatenate([o, pad], axis=0).T.astype(BF16)


def _attention(qt, k, vt, B, S):
    tq = min(TQ, S)
    tk = min(TK, S)
    gw = GQA_GROUP * LANES
    return pl.pallas_call(
        functools.partial(_attn_kernel, tq=tq, tk=tk, n_kv=S // tk,
                          kv_unroll=math.gcd(S // tk, KV_UNROLL)),
        grid=(B, N_KV_HEADS, S // tq),
        in_specs=[
            pl.BlockSpec((1, gw, tq), lambda b, h, i: (b, h, i)),
            pl.BlockSpec((1, S, LANES), lambda b, h, i: (b, 0, h)),
            pl.BlockSpec((1, LANES, S), lambda b, h, i: (b, h, 0)),
        ],
        out_specs=pl.BlockSpec((1, tq, gw), lambda b, h, i: (b, i, h)),
        out_shape=jax.ShapeDtypeStruct((B, S, N_Q_HEADS * LANES), BF16),
        scratch_shapes=[pltpu.VMEM((GQA_GROUP, PV_ROWS, tq), F32), pltpu.VMEM((tk, tq), F32),
                        pltpu.VMEM((tk, tq), BF16)],
        compiler_params=_cparams(("parallel", "parallel", "parallel")),
        name="attention",
    )(qt, k, vt)


def _scan_chunk(a, b, reverse):
    n = a.shape[0]
    row = lax.broadcasted_iota(jnp.int32, a.shape, 0)
    d = 1
    while d < n:
        if reverse:
            keep = row < n - d
            shift = n - d
        else:
            keep = row >= d
            shift = d
        a_sh = jnp.where(keep, pltpu.roll(a, shift, 0), 1.0)
        b_sh = jnp.where(keep, pltpu.roll(b, shift, 0), 0.0)
        b = a * b_sh + b
        a = a * a_sh
        d *= 2
    return a, b


def _lru_kernel(u_ref, gate_ref, cw_ref, cb_ref, w_ref, bias_ref, lam_ref, o_ref,
                up_ref, hf_ref, *, S, tc):
    halo = SUBLANES
    zeros = jnp.zeros((halo, LANES), F32)
    up_ref[0:halo, :] = zeros
    up_ref[S + halo:S + 2 * halo, :] = zeros
    up_ref[halo:S + halo, :] = u_ref[0]
    sp = jax.nn.softplus(-lam_ref[...])
    cw = cw_ref[...]
    cb = cb_ref[...]
    n_chunks = S // tc
    ext = tc + 2 * halo

    def gates(c, d):
        t0 = pl.multiple_of(c * tc, tc)
        ue = up_ref[pl.ds(t0, ext), :]
        xc = cb
        for j in range(CONV_W):
            sh = (CONV_PAD_L - j) % ext
            uj = ue if sh == 0 else pltpu.roll(ue, sh, 0)
            xc = xc + uj[halo:halo + tc] * cw[j:j + 1, :]
        gw = 2 * LANES
        g = jnp.dot(xc.astype(BF16), w_ref[0, :, d * gw:(d + 1) * gw],
                    preferred_element_type=F32) + bias_ref[0, :, d * gw:(d + 1) * gw]
        r = jax.nn.sigmoid(g[:, :LANES])
        i = jax.nn.sigmoid(g[:, LANES:])
        log_a = -LRU_C * r * sp[d:d + 1, :]
        a = jnp.exp(log_a)
        b = jnp.sqrt(1.0 - jnp.exp(2.0 * log_a)) * i * xc
        return t0, a, b

    def fwd(c, h):
        t0, a, b = gates(c, 0)
        pa, hb = _scan_chunk(a, b, False)
        hc = hb + pa * h
        hf_ref[pl.ds(t0, tc), :] = hc
        return hc[tc - 1:tc, :]

    lax.fori_loop(0, n_chunks, fwd, jnp.zeros((1, LANES), F32))

    def bwd(ci, h):
        t0, a, b = gates(n_chunks - 1 - ci, 1)
        pa, hb = _scan_chunk(a, b, True)
        hc = hb + pa * h
        gate = gate_ref[0, pl.ds(t0, tc), :]
        o_ref[0, pl.ds(t0, tc), :] = (hf_ref[pl.ds(t0, tc), :] + hc) * jax.nn.gelu(gate)
        return hc[0:1, :]

    lax.fori_loop(0, n_chunks, bwd, jnp.zeros((1, LANES), F32))


def _block_diag_pairs(w):
    nb, bw, _ = w.shape
    w = w.reshape(nb // 2, 2, bw, bw)
    z = jnp.zeros_like(w[:, 0])
    top = jnp.concatenate([w[:, 0], z], axis=-1)
    bot = jnp.concatenate([z, w[:, 1]], axis=-1)
    return jnp.concatenate([top, bot], axis=-2)


def _lru(lru_x, lru_gate, conv_w, conv_b, wa, ba, wi, bi, lam, B, S):
    C = lru_x.shape[-1]
    nc = C // LANES
    tc = min(TC_LRU, S)
    w = jnp.concatenate([_block_diag_pairs(wa[0]), _block_diag_pairs(wi[0]),
                         _block_diag_pairs(wa[1]), _block_diag_pairs(wi[1])], axis=-1).astype(BF16)
    bias = jnp.stack([ba[0].reshape(nc, LANES), bi[0].reshape(nc, LANES),
                      ba[1].reshape(nc, LANES), bi[1].reshape(nc, LANES)], axis=1)
    bias = bias.reshape(nc, 1, 4 * LANES)
    blk = lambda b, c: (b, 0, c)
    return pl.pallas_call(
        functools.partial(_lru_kernel, S=S, tc=tc),
        grid=(B, nc),
        in_specs=[
            pl.BlockSpec((1, S, LANES), blk),
            pl.BlockSpec((1, S, LANES), blk),
            pl.BlockSpec((CONV_W, LANES), lambda b, c: (0, c)),
            pl.BlockSpec((1, LANES), lambda b, c: (0, c)),
            pl.BlockSpec((1, LANES, 4 * LANES), lambda b, c: (c, 0, 0)),
            pl.BlockSpec((1, 1, 4 * LANES), lambda b, c: (c, 0, 0)),
            pl.BlockSpec((2, LANES), lambda b, c: (0, c)),
        ],
        out_specs=pl.BlockSpec((1, S, LANES), blk),
        out_shape=jax.ShapeDtypeStruct((B, S, C), F32),
        scratch_shapes=[
            pltpu.VMEM((S + 2 * SUBLANES, LANES), F32),
            pltpu.VMEM((S, LANES), F32),
        ],
        compiler_params=_cparams(("parallel", "parallel")),
        name="rglru",
    )(lru_x, lru_gate, conv_w, conv_b.reshape(1, C), w, bias, lam)


def _rows_to_slabs(ref, x):
    n = x.shape[0]
    for s in range(SUBLANES):
        ref[pl.ds(s, n, stride=SUBLANES), :] = x[:, s * LANES:(s + 1) * LANES]


def _slabs_to_rows(ref, n):
    return jnp.concatenate([ref[pl.ds(s, n, stride=SUBLANES), :] for s in range(SUBLANES)], axis=1)


def _slab(ref, r):
    return ref.at[pl.ds(pl.multiple_of(r * SUBLANES, SUBLANES), SUBLANES)]


def _outproj_kernel(a_ref, l_ref, x_ref, ag_ref, lg_ref, wa_ref, wl_ref, g2_ref,
                    wrh_ref, wrl_ref, br_ref, tri_ref,
                    x1_ref, xn3_ref, route_ref, gates_ref, cnt_ref, carry_ref, *, attn_w, lru_w):
    step = pl.program_id(0)

    @pl.when(step == 0)
    def _():
        carry_ref[...] = jnp.zeros_like(carry_ref)

    a = a_ref[...].astype(F32)
    ams = jnp.sum(a * a, axis=-1, keepdims=True) * (1.0 / attn_w)
    an = a * lax.rsqrt(ams + NORM_EPS) * ag_ref[...]
    l = l_ref[...]
    lms = jnp.sum(l * l, axis=-1, keepdims=True) * (1.0 / lru_w)
    ln = l * lax.rsqrt(lms + NORM_EPS) * lg_ref[...]
    mix = (jnp.dot(an.astype(BF16), wa_ref[...], preferred_element_type=F32)
           + jnp.dot(ln.astype(BF16), wl_ref[...], preferred_element_type=F32))
    x1 = x_ref[...] + mix
    x1_ref[...] = x1
    ms = jnp.mean(x1 * x1, axis=-1, keepdims=True)
    xn = x1 * lax.rsqrt(ms + NORM_EPS) * g2_ref[...]
    _rows_to_slabs(xn3_ref, xn)

    hi = xn.astype(BF16)
    lo = (xn - hi.astype(F32)).astype(BF16)
    logits = (jnp.dot(hi, wrh_ref[...], preferred_element_type=F32)
              + jnp.dot(lo, wrh_ref[...], preferred_element_type=F32)
              + jnp.dot(hi, wrl_ref[...], preferred_element_type=F32)) + br_ref[...]
    lane = lax.broadcasted_iota(jnp.int32, logits.shape, 1)
    neg = -jnp.inf
    work = jnp.where(lane < N_EXPERTS, logits, neg)
    sel = jnp.zeros(logits.shape, F32)
    idxs, vals = [], []
    for _ in range(TOP_K):
        m = jnp.max(work, axis=1, keepdims=True)
        idx = jnp.min(jnp.where(work == m, lane, LANES), axis=1, keepdims=True)
        hit = lane == idx
        work = jnp.where(hit, neg, work)
        sel = sel + hit.astype(F32)
        idxs.append(idx)
        vals.append(m)
    es = [jnp.exp(v - vals[0]) for v in vals]
    den = es[0] + es[1] + es[2] + es[3]

    prefix = jnp.dot(tri_ref[...], sel.astype(BF16), preferred_element_type=F32) + carry_ref[...]
    carry_ref[...] = carry_ref[...] + jnp.sum(sel, axis=0, keepdims=True)
    cnt_ref[...] = carry_ref[...]

    route = jnp.zeros(logits.shape, jnp.int32)
    gates = jnp.zeros(logits.shape, F32)
    for k in range(TOP_K):
        rank = jnp.sum(jnp.where(lane == idxs[k], prefix, 0.0), axis=1, keepdims=True).astype(jnp.int32)
        route = jnp.where(lane == k, idxs[k], route)
        route = jnp.where(lane == TOP_K + k, rank, route)
        gates = jnp.where(lane == k, es[k] / den, gates)
    route_ref[...] = route
    gates_ref[...] = gates


def _outproj_router(attn, lru, x2, attn_out_g, lru_out_g, w_out, norm2_g, w_router, b_router):
    T, D = x2.shape
    lru_w = lru.shape[-1]
    ts = min(TS_OUT, T)
    wa = w_out[:ATTN_W].reshape(N_Q_HEADS, HEAD_DIM, D)
    wa = jnp.pad(wa, ((0, 0), (0, LANES - HEAD_DIM), (0, 0))).reshape(N_Q_HEADS * LANES, D).astype(BF16)
    wl = w_out[ATTN_W:].astype(BF16)
    ag = _pad_heads(attn_out_g.reshape(1, ATTN_W), N_Q_HEADS)
    wr = jnp.pad(w_router, ((0, 0), (0, LANES - N_EXPERTS)))
    wrh = wr.astype(BF16)
    wrl = (wr - wrh.astype(F32)).astype(BF16)
    br = jnp.pad(b_router.reshape(1, N_EXPERTS), ((0, 0), (0, LANES - N_EXPERTS)))
    tri = (jnp.arange(ts)[:, None] > jnp.arange(ts)[None, :]).astype(BF16)
    const = lambda i: (0, 0)
    tok = lambda i: (i, 0)
    aw = N_Q_HEADS * LANES
    return pl.pallas_call(
        functools.partial(_outproj_kernel, attn_w=ATTN_W, lru_w=lru_w),
        grid=(T // ts,),
        in_specs=[
            pl.BlockSpec((ts, aw), tok),
            pl.BlockSpec((ts, lru_w), tok),
            pl.BlockSpec((ts, D), tok),
            pl.BlockSpec((1, aw), const),
            pl.BlockSpec((1, lru_w), const),
            pl.BlockSpec((aw, D), const),
            pl.BlockSpec((lru_w, D), const),
            pl.BlockSpec((1, D), const),
            pl.BlockSpec((D, LANES), const),
            pl.BlockSpec((D, LANES), const),
            pl.BlockSpec((1, LANES), const),
            pl.BlockSpec((ts, ts), const),
        ],
        out_specs=[
            pl.BlockSpec((ts, D), tok),
            pl.BlockSpec((ts * SUBLANES, LANES), tok),
            pl.BlockSpec((ts, LANES), tok),
            pl.BlockSpec((ts, LANES), tok),
            pl.BlockSpec((1, LANES), const),
        ],
        out_shape=[
            jax.ShapeDtypeStruct((T, D), F32),
            jax.ShapeDtypeStruct((T * SUBLANES, LANES), F32),
            jax.ShapeDtypeStruct((T, LANES), jnp.int32),
            jax.ShapeDtypeStruct((T, LANES), F32),
            jax.ShapeDtypeStruct((1, LANES), F32),
        ],
        scratch_shapes=[pltpu.VMEM((1, LANES), F32)],
        compiler_params=_cparams(("arbitrary",)),
        name="outproj_router",
    )(attn, lru, x2, ag, lru_out_g.reshape(1, lru_w), wa, wl, norm2_g.reshape(1, D),
      wrh, wrl, br, tri)


def _dest_row(pstart_ref, idx_ref, rank_ref, r, k):
    return pstart_ref[idx_ref[r * TOP_K + k]] + rank_ref[r * TOP_K + k]


def _dispatch_kernel(pstart_ref, idx_ref, rank_ref, x_ref, init_hbm, out_hbm, sem, *, ts):
    del init_hbm

    def issue(i, carry):
        for j in range(ISSUE_UNROLL):
            r = i * ISSUE_UNROLL + j
            for k in range(TOP_K):
                d = _dest_row(pstart_ref, idx_ref, rank_ref, r, k)
                pltpu.make_async_copy(_slab(x_ref, r), _slab(out_hbm, d), sem).start(priority=k % 2)
        return carry

    lax.fori_loop(0, ts // ISSUE_UNROLL, issue, 0)
    for k in range(TOP_K):
        pltpu.make_async_copy(x_ref, out_hbm.at[pl.ds(0, ts * SUBLANES)], sem).wait()


def _dispatch(xn_slabs, pstart, idx_flat, rank_flat, n_rows):
    T = xn_slabs.shape[0] // SUBLANES
    ts = min(TS_DISP, T)
    init = jnp.zeros((n_rows * SUBLANES, LANES), xn_slabs.dtype)
    routing = pl.BlockSpec((ts * TOP_K,), lambda i, ps: (i,), memory_space=pltpu.SMEM)
    grid_spec = pltpu.PrefetchScalarGridSpec(
        num_scalar_prefetch=1,
        grid=(T // ts,),
        in_specs=[
            routing,
            routing,
            pl.BlockSpec((ts * SUBLANES, LANES), lambda i, ps: (i, 0)),
            pl.BlockSpec(memory_space=pl.ANY),
        ],
        out_specs=pl.BlockSpec(memory_space=pl.ANY),
        scratch_shapes=[pltpu.SemaphoreType.DMA],
    )
    return pl.pallas_call(
        functools.partial(_dispatch_kernel, ts=ts),
        grid_spec=grid_spec,
        out_shape=jax.ShapeDtypeStruct(init.shape, init.dtype),
        input_output_aliases={4: 0},
        compiler_params=_cparams(("arbitrary",)),
        name="dispatch",
    )(pstart, idx_flat, rank_flat, xn_slabs, init)


def _expert_kernel(be_ref, na_ref, x_ref, wg_ref, bg_ref, wu_ref, bu_ref, wd_ref, bd_ref,
                   y_ref, wgb_ref, wub_ref, wdb_ref, prev_ref):
    i = pl.program_id(0)
    e = be_ref[i]

    @pl.when(i == 0)
    def _():
        prev_ref[0] = -1

    active = i < na_ref[0]

    @pl.when(jnp.logical_and(active, e != prev_ref[0]))
    def _():
        wgb_ref[...] = wg_ref[0].astype(BF16)
        wub_ref[...] = wu_ref[0].astype(BF16)
        wdb_ref[...] = wd_ref[0].astype(BF16)
        prev_ref[0] = e

    @pl.when(active)
    def _():
        x = _slabs_to_rows(x_ref, ROW_BLOCK).astype(BF16)
        g = jnp.dot(x, wgb_ref[...], preferred_element_type=F32) + bg_ref[0]
        u = jnp.dot(x, wub_ref[...], preferred_element_type=F32) + bu_ref[0]
        g = jnp.minimum(g, SWIGLU_LIMIT)
        u = jnp.clip(u, -SWIGLU_LIMIT, SWIGLU_LIMIT)
        glu = g * jax.nn.sigmoid(SWIGLU_ALPHA * g)
        y = jnp.dot(((u + 1.0) * glu).astype(BF16), wdb_ref[...], preferred_element_type=F32) + bd_ref[0]
        _rows_to_slabs(y_ref, y)


def _experts(x_rows, block_e, n_active, w_gate, b_gate, w_up, b_up, w_down, b_down):
    E, D, FF = w_gate.shape
    block_slabs = ROW_BLOCK * SUBLANES
    n_blocks = x_rows.shape[0] // block_slabs

    def row_map(i, be, na):
        return (jnp.minimum(i, na[0] - 1), 0)

    def w_map(i, be, na):
        return (be[jnp.minimum(i, na[0] - 1)], 0, 0)

    grid_spec = pltpu.PrefetchScalarGridSpec(
        num_scalar_prefetch=2,
        grid=(n_blocks,),
        in_specs=[
            pl.BlockSpec((block_slabs, LANES), row_map),
            pl.BlockSpec((1, D, FF), w_map),
            pl.BlockSpec((1, 1, FF), w_map),
            pl.BlockSpec((1, D, FF), w_map),
            pl.BlockSpec((1, 1, FF), w_map),
            pl.BlockSpec((1, FF, D), w_map),
            pl.BlockSpec((1, 1, D), w_map),
        ],
        out_specs=pl.BlockSpec((block_slabs, LANES), row_map),
        scratch_shapes=[
            pltpu.VMEM((D, FF), BF16),
            pltpu.VMEM((D, FF), BF16),
            pltpu.VMEM((FF, D), BF16),
            pltpu.SMEM((1,), jnp.int32),
        ],
    )
    return pl.pallas_call(
        _expert_kernel,
        grid_spec=grid_spec,
        out_shape=jax.ShapeDtypeStruct(x_rows.shape, F32),
        input_output_aliases={2: 0},
        compiler_params=_cparams(("arbitrary",)),
        name="experts",
    )(block_e, n_active, x_rows, w_gate, b_gate.reshape(E, 1, FF), w_up, b_up.reshape(E, 1, FF),
      w_down, b_down.reshape(E, 1, D))


def _combine_kernel(pstart_ref, idx_ref, rank_ref, y_hbm, x1_ref, gates_ref, fg_ref, o_ref,
                    b0, b1, b2, b3, sem, *, ts):
    bufs = (b0, b1, b2, b3)

    def issue(i, carry):
        for j in range(ISSUE_UNROLL):
            r = i * ISSUE_UNROLL + j
            for k in range(TOP_K):
                d = _dest_row(pstart_ref, idx_ref, rank_ref, r, k)
                pltpu.make_async_copy(_slab(y_hbm, d), _slab(bufs[k], r), sem).start(priority=k % 2)
        return carry

    lax.fori_loop(0, ts // ISSUE_UNROLL, issue, 0)
    for k in range(TOP_K):
        pltpu.make_async_copy(y_hbm.at[pl.ds(0, ts * SUBLANES)], bufs[k], sem).wait()

    acc = x1_ref[...]
    gates = gates_ref[...]
    for k in range(TOP_K):
        acc = acc + _slabs_to_rows(bufs[k], ts) * gates[:, k:k + 1]
    ms = jnp.mean(acc * acc, axis=-1, keepdims=True)
    o_ref[...] = acc * lax.rsqrt(ms + NORM_EPS) * fg_ref[...]


def _combine(y_rows, pstart, idx_flat, rank_flat, x1, gates, final_g):
    T, D = x1.shape
    ts = min(TS_COMB, T)
    tok = lambda i, ps: (i, 0)
    routing = pl.BlockSpec((ts * TOP_K,), lambda i, ps: (i,), memory_space=pltpu.SMEM)
    grid_spec = pltpu.PrefetchScalarGridSpec(
        num_scalar_prefetch=1,
        grid=(T // ts,),
        in_specs=[
            routing,
            routing,
            pl.BlockSpec(memory_space=pl.ANY),
            pl.BlockSpec((ts, D), tok),
            pl.BlockSpec((ts, LANES), tok),
            pl.BlockSpec((1, D), lambda i, ps: (0, 0)),
        ],
        out_specs=pl.BlockSpec((ts, D), tok),
        scratch_shapes=[pltpu.VMEM((ts * SUBLANES, LANES), F32) for _ in range(TOP_K)]
        + [pltpu.SemaphoreType.DMA],
    )
    return pl.pallas_call(
        functools.partial(_combine_kernel, ts=ts),
        grid_spec=grid_spec,
        out_shape=jax.ShapeDtypeStruct((T, D), F32),
        compiler_params=_cparams(("arbitrary",)),
        name="combine",
    )(pstart, idx_flat, rank_flat, y_rows, x1, gates, final_g.reshape(1, D))


def kernel(x, norm1_g, w_in, q_norm_g, k_norm_g, conv_w, conv_b, lru_wa, lru_ba, lru_wi, lru_bi,
           lru_lam, attn_out_g, lru_out_g, w_out, norm2_g, w_router, b_router, w_gate, b_gate,
           w_up, b_up, w_down, b_down, final_g):
    B, S, D = x.shape
    T = B * S
    assert w_in.shape[0] == 1, "single-layer trunk: the final norm is fused into the layer's combine"
    x2 = x.reshape(T, D)
    for l in range(1):
        qt, k, vt, lru_x, lru_gate = _inproj(x2, norm1_g[l], w_in[l], q_norm_g[l], k_norm_g[l], S)
        attn = _attention(qt, k.reshape(B, S, -1), vt, B, S)
        lru = _lru(lru_x.reshape(B, S, -1), lru_gate.reshape(B, S, -1), conv_w[l], conv_b[l],
                   lru_wa[l], lru_ba[l], lru_wi[l], lru_bi[l], lru_lam[l], B, S)
        x1, xn3, route, gates, cnt = _outproj_router(
            attn.reshape(T, -1), lru.reshape(T, -1), x2, attn_out_g[l], lru_out_g[l], w_out[l],
            norm2_g[l], w_router[l], b_router[l])

        idx_flat = route[:, :TOP_K].reshape(T * TOP_K)
        rank_flat = route[:, TOP_K:2 * TOP_K].reshape(T * TOP_K)
        counts = cnt[0, :N_EXPERTS].astype(jnp.int32)
        padded = ((counts + ROW_BLOCK - 1) // ROW_BLOCK) * ROW_BLOCK
        pend = jnp.cumsum(padded)
        pstart = (pend - padded).astype(jnp.int32)
        n_rows = T * TOP_K + N_EXPERTS * ROW_BLOCK
        block_start = jnp.arange(n_rows // ROW_BLOCK, dtype=jnp.int32) * ROW_BLOCK
        block_e = jnp.sum((pend[None, :] <= block_start[:, None]).astype(jnp.int32), axis=1)
        block_e = jnp.minimum(block_e, N_EXPERTS - 1)
        n_active = (pend[-1:] // ROW_BLOCK).astype(jnp.int32)

        x_rows = _dispatch(xn3, pstart, idx_flat, rank_flat, n_rows)
        y_rows = _experts(x_rows, block_e, n_active, w_gate[l], b_gate[l], w_up[l], b_up[l],
                          w_down[l], b_down[l])
        x2 = _combine(y_rows, pstart, idx_flat, rank_flat, x1, gates, final_g)
    return x2.reshape(B, S, D)
```

```python
import functools
import math

import jax
import jax.numpy as jnp
from jax import lax
from jax.experimental import pallas as pl
from jax.experimental.pallas import tpu as pltpu

F32 = jnp.float32
BF16 = jnp.bfloat16

GRID_W = 64
HEAD_DIM = 64
N_Q_HEADS = 8
N_KV_HEADS = 2
GQA_GROUP = N_Q_HEADS // N_KV_HEADS
ATTN_W = N_Q_HEADS * HEAD_DIM
KV_W = N_KV_HEADS * HEAD_DIM
LRU_BLOCKS = 8
LRU_C = 8.0
CONV_W = 4
CONV_PAD_L = 2
ROPE_THETA = 10000.0
ROPE_HALF = HEAD_DIM // 2
ROPE_M = ROPE_HALF // 2
N_EXPERTS = 32
TOP_K = 4
SWIGLU_ALPHA = 1.702
SWIGLU_LIMIT = 7.0
NORM_EPS = 1e-5
QK_EPS = 1e-6
LOG2_E = 1.4426950408889634
Q_SCALE = HEAD_DIM ** -0.5 * LOG2_E

LANES = 128
SUBLANES = 8
BF16_SUBLANES = 16
PV_ROWS = HEAD_DIM + BF16_SUBLANES
VMEM_LIMIT = 48 * 1024 * 1024
EXPERT_VMEM_LIMIT = 56 * 1024 * 1024

TS_IN = 512
TQ = 256
TK = 1024
KV_UNROLL = 1
TC_LRU = 512
TS_OUT = 512
ROW_BLOCK = 256
TS_DISP = 512
TS_COMB = 256
ISSUE_UNROLL = 8


def _cparams(sem):
    return pltpu.CompilerParams(dimension_semantics=sem, vmem_limit_bytes=VMEM_LIMIT)


def _inproj_kernel(x_ref, g1_ref, w_ref, qg_ref, kg_ref, cos_ref, sin_ref,
                   q_ref, k_ref, v_ref, lx_ref, lg_ref, *, lru_w):
    x = x_ref[...]
    ms = jnp.mean(x * x, axis=-1, keepdims=True)
    xn = x * lax.rsqrt(ms + NORM_EPS) * g1_ref[...]
    h = jnp.dot(xn.astype(BF16), w_ref[...], preferred_element_type=F32)

    cos = cos_ref[...]
    sin = sin_ref[...]
    lane = lax.broadcasted_iota(jnp.int32, cos.shape, 1)
    first_half = (lane % ROPE_HALF) < ROPE_M

    def head_norm_rope(xc, g, scale):
        hms = jnp.sum(xc * xc, axis=-1, keepdims=True) * (1.0 / HEAD_DIM)
        xc = xc * lax.rsqrt(hms + QK_EPS) * g
        partner = jnp.where(first_half,
                            pltpu.roll(xc, LANES - ROPE_M, 1),
                            pltpu.roll(xc, ROPE_M, 1))
        return (xc * cos + partner * sin) * scale

    qw = N_Q_HEADS * LANES
    kw = N_KV_HEADS * LANES
    for c in range(N_Q_HEADS):
        sl = slice(c * LANES, (c + 1) * LANES)
        q_ref[0, sl, :] = head_norm_rope(h[:, sl], qg_ref[...], Q_SCALE).T.astype(BF16)
    for c in range(N_KV_HEADS):
        sl = slice(c * LANES, (c + 1) * LANES)
        k_ref[:, sl] = head_norm_rope(h[:, qw + c * LANES: qw + (c + 1) * LANES],
                                      kg_ref[...], 1.0).astype(BF16)
        vc = h[:, qw + kw + c * LANES: qw + kw + (c + 1) * LANES]
        v_ref[0, sl, :] = jnp.where(lane >= HEAD_DIM, 1.0, vc).T.astype(BF16)
    o = qw + 2 * kw
    lx_ref[...] = h[:, o: o + lru_w]
    lg_ref[...] = h[:, o + lru_w: o + 2 * lru_w]


def _pad_heads(w, n_heads):
    lead = w.shape[:-1]
    w = w.reshape(lead + (n_heads, HEAD_DIM))
    w = jnp.pad(w, [(0, 0)] * len(lead) + [(0, 0), (0, LANES - HEAD_DIM)])
    return w.reshape(lead + (n_heads * LANES,))


def _rope_tables(S):
    t = jnp.arange(S)
    rows = (t // GRID_W).astype(F32)
    cols = (t % GRID_W).astype(F32)
    inv_freq = ROPE_THETA ** (-jnp.arange(ROPE_M, dtype=F32) / ROPE_M)
    ar = rows[:, None] * inv_freq[None, :]
    ac = cols[:, None] * inv_freq[None, :]
    cos = jnp.concatenate([jnp.cos(ar), jnp.cos(ar), jnp.cos(ac), jnp.cos(ac)], axis=-1)
    sin = jnp.concatenate([-jnp.sin(ar), jnp.sin(ar), -jnp.sin(ac), jnp.sin(ac)], axis=-1)
    pad = [(0, 0), (0, LANES - HEAD_DIM)]
    return jnp.pad(cos, pad), jnp.pad(sin, pad)


def _inproj(x2, norm1_g, w_in, q_norm_g, k_norm_g, S):
    T, D = x2.shape
    lru_w = (w_in.shape[1] - ATTN_W - 2 * KV_W) // 2
    o0, o1, o2 = ATTN_W, ATTN_W + KV_W, ATTN_W + 2 * KV_W
    w_all = jnp.concatenate([
        _pad_heads(w_in[:, :o0], N_Q_HEADS),
        _pad_heads(w_in[:, o0:o1], N_KV_HEADS),
        _pad_heads(w_in[:, o1:o2], N_KV_HEADS),
        w_in[:, o2:],
    ], axis=1).astype(BF16)
    qg = _pad_heads(q_norm_g.reshape(1, HEAD_DIM), 1)
    kg = _pad_heads(k_norm_g.reshape(1, HEAD_DIM), 1)
    cos, sin = _rope_tables(S)
    ts = TS_IN
    n_s = S // ts
    qw, kw = N_Q_HEADS * LANES, N_KV_HEADS * LANES
    const = lambda i: (0, 0)
    tok = lambda i: (i, 0)
    pos = lambda i: (i % n_s, 0)
    tposed = lambda i: (i // n_s, 0, i % n_s)
    return pl.pallas_call(
        functools.partial(_inproj_kernel, lru_w=lru_w),
        grid=(T // ts,),
        in_specs=[
            pl.BlockSpec((ts, D), tok),
            pl.BlockSpec((1, D), const),
            pl.BlockSpec(w_all.shape, const),
            pl.BlockSpec((1, LANES), const),
            pl.BlockSpec((1, LANES), const),
            pl.BlockSpec((ts, LANES), pos),
            pl.BlockSpec((ts, LANES), pos),
        ],
        out_specs=[
            pl.BlockSpec((1, qw, ts), tposed),
            pl.BlockSpec((ts, kw), tok),
            pl.BlockSpec((1, kw, ts), tposed),
            pl.BlockSpec((ts, lru_w), tok),
            pl.BlockSpec((ts, lru_w), tok),
        ],
        out_shape=[
            jax.ShapeDtypeStruct((T // S, qw, S), BF16),
            jax.ShapeDtypeStruct((T, kw), BF16),
            jax.ShapeDtypeStruct((T // S, kw, S), BF16),
            jax.ShapeDtypeStruct((T, lru_w), F32),
            jax.ShapeDtypeStruct((T, lru_w), F32),
        ],
        compiler_params=_cparams(("parallel",)),
        name="inproj",
    )(x2, norm1_g.reshape(1, D), w_all, qg, kg, cos, sin)


def _attn_kernel(qt_ref, k_ref, vt_ref, o_ref, acc_ref, s_ref, p_ref, *, tq, tk, n_kv, kv_unroll):
    acc_ref[...] = jnp.zeros(acc_ref.shape, F32)

    def scores(j, g):
        off = pl.multiple_of(j * tk, tk)
        s = jnp.dot(k_ref[0, pl.ds(off, tk), :], qt_ref[0, g * LANES:(g + 1) * LANES, :],
                    preferred_element_type=F32)
        return s, jnp.max(s, axis=0, keepdims=True)

    def softmax_stage(s, s_max, m_prev):
        m_new = jnp.maximum(m_prev, s_max)
        return m_new, jnp.exp2(m_prev - m_new), jnp.exp2(s - m_new).astype(BF16)

    m0 = jnp.full((1, tq), -jnp.inf, F32)
    m_first, alpha_first, p_first = softmax_stage(*scores(0, 0), m0)
    p_ref[...] = p_first
    s_second, smax_second = scores(0, 1)
    s_ref[...] = s_second

    def body(it, carry):
        ms, alpha, s_max = list(carry[:GQA_GROUP]), carry[GQA_GROUP], carry[GQA_GROUP + 1]
        s = s_ref[...]
        p = p_ref[...]
        for n in range(kv_unroll * GQA_GROUP):
            j = it * kv_unroll + n // GQA_GROUP
            g = n % GQA_GROUP
            off = pl.multiple_of(j * tk, tk)
            j_next = jnp.minimum(it * kv_unroll + (n + 2) // GQA_GROUP, n_kv - 1)
            s_next, smax_next = scores(j_next, (n + 2) % GQA_GROUP)
            h = (n + 1) % GQA_GROUP
            ms[h], alpha_next, p_next = softmax_stage(s, s_max, ms[h])
            acc_ref[g] = alpha * acc_ref[g] + jnp.dot(vt_ref[0, 0:PV_ROWS, pl.ds(off, tk)], p,
                                                      preferred_element_type=F32)
            s, s_max, p, alpha = s_next, smax_next, p_next, alpha_next
        s_ref[...] = s
        p_ref[...] = p
        return tuple(ms) + (alpha, s_max)

    lax.fori_loop(0, n_kv // kv_unroll, body,
                  (m_first,) + (m0,) * (GQA_GROUP - 1) + (alpha_first, smax_second))
    pad = jnp.zeros((LANES - HEAD_DIM, tq), F32)
    for g in range(GQA_GROUP):
        acc = acc_ref[g]
        o = acc[0:HEAD_DIM] / acc[HEAD_DIM:HEAD_DIM + 1, :]
        o_ref[0, :, g * LANES:(g + 1) * LANES] = jnp.concatenate([o, pad], axis=0).T.astype(BF16)


def _attention(qt, k, vt, B, S):
    tq = min(TQ, S)
    tk = min(TK, S)
    gw = GQA_GROUP * LANES
    return pl.pallas_call(
        functools.partial(_attn_kernel, tq=tq, tk=tk, n_kv=S // tk,
                          kv_unroll=math.gcd(S // tk, KV_UNROLL)),
        grid=(B, N_KV_HEADS, S // tq),
        in_specs=[
            pl.BlockSpec((1, gw, tq), lambda b, h, i: (b, h, i)),
            pl.BlockSpec((1, S, LANES), lambda b, h, i: (b, 0, h)),
            pl.BlockSpec((1, LANES, S), lambda b, h, i: (b, h, 0)),
        ],
        out_specs=pl.BlockSpec((1, tq, gw), lambda b, h, i: (b, i, h)),
        out_shape=jax.ShapeDtypeStruct((B, S, N_Q_HEADS * LANES), BF16),
        scratch_shapes=[pltpu.VMEM((GQA_GROUP, PV_ROWS, tq), F32), pltpu.VMEM((tk, tq), F32),
                        pltpu.VMEM((tk, tq), BF16)],
        compiler_params=_cparams(("parallel", "parallel", "parallel")),
        name="attention",
    )(qt, k, vt)


def _scan_chunk(a, b, reverse):
    n = a.shape[0]
    row = lax.broadcasted_iota(jnp.int32, a.shape, 0)
    d = 1
    while d < n:
        if reverse:
            keep = row < n - d
            shift = n - d
        else:
            keep = row >= d
            shift = d
        a_sh = jnp.where(keep, pltpu.roll(a, shift, 0), 1.0)
        b_sh = jnp.where(keep, pltpu.roll(b, shift, 0), 0.0)
        b = a * b_sh + b
        a = a * a_sh
        d *= 2
    return a, b


def _lru_kernel(u_ref, gate_ref, cw_ref, cb_ref, w_ref, bias_ref, lam_ref, o_ref,
                up_ref, hf_ref, *, S, tc):
    halo = SUBLANES
    zeros = jnp.zeros((halo, LANES), F32)
    up_ref[0:halo, :] = zeros
    up_ref[S + halo:S + 2 * halo, :] = zeros
    up_ref[halo:S + halo, :] = u_ref[0]
    sp = jax.nn.softplus(-lam_ref[...])
    cw = cw_ref[...]
    cb = cb_ref[...]
    n_chunks = S // tc
    ext = tc + 2 * halo

    def gates(c, d):
        t0 = pl.multiple_of(c * tc, tc)
        ue = up_ref[pl.ds(t0, ext), :]
        xc = cb
        for j in range(CONV_W):
            sh = (CONV_PAD_L - j) % ext
            uj = ue if sh == 0 else pltpu.roll(ue, sh, 0)
            xc = xc + uj[halo:halo + tc] * cw[j:j + 1, :]
        gw = 2 * LANES
        g = jnp.dot(xc.astype(BF16), w_ref[0, :, d * gw:(d + 1) * gw],
                    preferred_element_type=F32) + bias_ref[0, :, d * gw:(d + 1) * gw]
        r = jax.nn.sigmoid(g[:, :LANES])
        i = jax.nn.sigmoid(g[:, LANES:])
        log_a = -LRU_C * r * sp[d:d + 1, :]
        a = jnp.exp(log_a)
        b = jnp.sqrt(1.0 - jnp.exp(2.0 * log_a)) * i * xc
        return t0, a, b

    def fwd(c, h):
        t0, a, b = gates(c, 0)
        pa, hb = _scan_chunk(a, b, False)
        hc = hb + pa * h
        hf_ref[pl.ds(t0, tc), :] = hc
        return hc[tc - 1:tc, :]

    lax.fori_loop(0, n_chunks, fwd, jnp.zeros((1, LANES), F32))

    def bwd(ci, h):
        t0, a, b = gates(n_chunks - 1 - ci, 1)
        pa, hb = _scan_chunk(a, b, True)
        hc = hb + pa * h
        gate = gate_ref[0, pl.ds(t0, tc), :]
        o_ref[0, pl.ds(t0, tc), :] = (hf_ref[pl.ds(t0, tc), :] + hc) * jax.nn.gelu(gate)
        return hc[0:1, :]

    lax.fori_loop(0, n_chunks, bwd, jnp.zeros((1, LANES), F32))


def _block_diag_pairs(w):
    nb, bw, _ = w.shape
    w = w.reshape(nb // 2, 2, bw, bw)
    z = jnp.zeros_like(w[:, 0])
    top = jnp.concatenate([w[:, 0], z], axis=-1)
    bot = jnp.concatenate([z, w[:, 1]], axis=-1)
    return jnp.concatenate([top, bot], axis=-2)


def _lru(lru_x, lru_gate, conv_w, conv_b, wa, ba, wi, bi, lam, B, S):
    C = lru_x.shape[-1]
    nc = C // LANES
    tc = min(TC_LRU, S)
    w = jnp.concatenate([_block_diag_pairs(wa[0]), _block_diag_pairs(wi[0]),
                         _block_diag_pairs(wa[1]), _block_diag_pairs(wi[1])], axis=-1).astype(BF16)
    bias = jnp.stack([ba[0].reshape(nc, LANES), bi[0].reshape(nc, LANES),
                      ba[1].reshape(nc, LANES), bi[1].reshape(nc, LANES)], axis=1)
    bias = bias.reshape(nc, 1, 4 * LANES)
    blk = lambda b, c: (b, 0, c)
    return pl.pallas_call(
        functools.partial(_lru_kernel, S=S, tc=tc),
        grid=(B, nc),
        in_specs=[
            pl.BlockSpec((1, S, LANES), blk),
            pl.BlockSpec((1, S, LANES), blk),
            pl.BlockSpec((CONV_W, LANES), lambda b, c: (0, c)),
            pl.BlockSpec((1, LANES), lambda b, c: (0, c)),
            pl.BlockSpec((1, LANES, 4 * LANES), lambda b, c: (c, 0, 0)),
            pl.BlockSpec((1, 1, 4 * LANES), lambda b, c: (c, 0, 0)),
            pl.BlockSpec((2, LANES), lambda b, c: (0, c)),
        ],
        out_specs=pl.BlockSpec((1, S, LANES), blk),
        out_shape=jax.ShapeDtypeStruct((B, S, C), F32),
        scratch_shapes=[
            pltpu.VMEM((S + 2 * SUBLANES, LANES), F32),
            pltpu.VMEM((S, LANES), F32),
        ],
        compiler_params=_cparams(("parallel", "parallel")),
        name="rglru",
    )(lru_x, lru_gate, conv_w, conv_b.reshape(1, C), w, bias, lam)


def _rows_to_slabs(ref, x):
    n = x.shape[0]
    for s in range(SUBLANES):
        ref[pl.ds(s, n, stride=SUBLANES), :] = x[:, s * LANES:(s + 1) * LANES]


def _slabs_to_rows(ref, n):
    return jnp.concatenate([ref[pl.ds(s, n, stride=SUBLANES), :] for s in range(SUBLANES)], axis=1)


def _slab(ref, r):
    return ref.at[pl.ds(pl.multiple_of(r * SUBLANES, SUBLANES), SUBLANES)]


def _outproj_kernel(a_ref, l_ref, x_ref, ag_ref, lg_ref, wa_ref, wl_ref, g2_ref,
                    wrh_ref, wrl_ref, br_ref, tri_ref,
                    x1_ref, xn3_ref, route_ref, gates_ref, cnt_ref, carry_ref, *, attn_w, lru_w):
    step = pl.program_id(0)

    @pl.when(step == 0)
    def _():
        carry_ref[...] = jnp.zeros_like(carry_ref)

    a = a_ref[...].astype(F32)
    ams = jnp.sum(a * a, axis=-1, keepdims=True) * (1.0 / attn_w)
    an = a * lax.rsqrt(ams + NORM_EPS) * ag_ref[...]
    l = l_ref[...]
    lms = jnp.sum(l * l, axis=-1, keepdims=True) * (1.0 / lru_w)
    ln = l * lax.rsqrt(lms + NORM_EPS) * lg_ref[...]
    mix = (jnp.dot(an.astype(BF16), wa_ref[...], preferred_element_type=F32)
           + jnp.dot(ln.astype(BF16), wl_ref[...], preferred_element_type=F32))
    x1 = x_ref[...] + mix
    x1_ref[...] = x1
    ms = jnp.mean(x1 * x1, axis=-1, keepdims=True)
    xn = x1 * lax.rsqrt(ms + NORM_EPS) * g2_ref[...]
    _rows_to_slabs(xn3_ref, xn)

    hi = xn.astype(BF16)
    lo = (xn - hi.astype(F32)).astype(BF16)
    logits = (jnp.dot(hi, wrh_ref[...], preferred_element_type=F32)
              + jnp.dot(lo, wrh_ref[...], preferred_element_type=F32)
              + jnp.dot(hi, wrl_ref[...], preferred_element_type=F32)) + br_ref[...]
    lane = lax.broadcasted_iota(jnp.int32, logits.shape, 1)
    neg = -jnp.inf
    work = jnp.where(lane < N_EXPERTS, logits, neg)
    sel = jnp.zeros(logits.shape, F32)
    idxs, vals = [], []
    for _ in range(TOP_K):
        m = jnp.max(work, axis=1, keepdims=True)
        idx = jnp.min(jnp.where(work == m, lane, LANES), axis=1, keepdims=True)
        hit = lane == idx
        work = jnp.where(hit, neg, work)
        sel = sel + hit.astype(F32)
        idxs.append(idx)
        vals.append(m)
    es = [jnp.exp(v - vals[0]) for v in vals]
    den = es[0] + es[1] + es[2] + es[3]

    prefix = jnp.dot(tri_ref[...], sel.astype(BF16), preferred_element_type=F32) + carry_ref[...]
    carry_ref[...] = carry_ref[...] + jnp.sum(sel, axis=0, keepdims=True)
    cnt_ref[...] = carry_ref[...]

    route = jnp.zeros(logits.shape, jnp.int32)
    gates = jnp.zeros(logits.shape, F32)
    for k in range(TOP_K):
        rank = jnp.sum(jnp.where(lane == idxs[k], prefix, 0.0), axis=1, keepdims=True).astype(jnp.int32)
        route = jnp.where(lane == k, idxs[k], route)
        route = jnp.where(lane == TOP_K + k, rank, route)
        gates = jnp.where(lane == k, es[k] / den, gates)
    route_ref[...] = route
    gates_ref[...] = gates


def _outproj_router(attn, lru, x2, attn_out_g, lru_out_g, w_out, norm2_g, w_router, b_router):
    T, D = x2.shape
    lru_w = lru.shape[-1]
    ts = min(TS_OUT, T)
    wa = w_out[:ATTN_W].reshape(N_Q_HEADS, HEAD_DIM, D)
    wa = jnp.pad(wa, ((0, 0), (0, LANES - HEAD_DIM), (0, 0))).reshape(N_Q_HEADS * LANES, D).astype(BF16)
    wl = w_out[ATTN_W:].astype(BF16)
    ag = _pad_heads(attn_out_g.reshape(1, ATTN_W), N_Q_HEADS)
    wr = jnp.pad(w_router, ((0, 0), (0, LANES - N_EXPERTS)))
    wrh = wr.astype(BF16)
    wrl = (wr - wrh.astype(F32)).astype(BF16)
    br = jnp.pad(b_router.reshape(1, N_EXPERTS), ((0, 0), (0, LANES - N_EXPERTS)))
    tri = (jnp.arange(ts)[:, None] > jnp.arange(ts)[None, :]).astype(BF16)
    const = lambda i: (0, 0)
    tok = lambda i: (i, 0)
    aw = N_Q_HEADS * LANES
    return pl.pallas_call(
        functools.partial(_outproj_kernel, attn_w=ATTN_W, lru_w=lru_w),
        grid=(T // ts,),
        in_specs=[
            pl.BlockSpec((ts, aw), tok),
            pl.BlockSpec((ts, lru_w), tok),
            pl.BlockSpec((ts, D), tok),
            pl.BlockSpec((1, aw), const),
            pl.BlockSpec((1, lru_w), const),
            pl.BlockSpec((aw, D), const),
            pl.BlockSpec((lru_w, D), const),
            pl.BlockSpec((1, D), const),
            pl.BlockSpec((D, LANES), const),
            pl.BlockSpec((D, LANES), const),
            pl.BlockSpec((1, LANES), const),
            pl.BlockSpec((ts, ts), const),
        ],
        out_specs=[
            pl.BlockSpec((ts, D), tok),
            pl.BlockSpec((ts * SUBLANES, LANES), tok),
            pl.BlockSpec((ts, LANES), tok),
            pl.BlockSpec((ts, LANES), tok),
            pl.BlockSpec((1, LANES), const),
        ],
        out_shape=[
            jax.ShapeDtypeStruct((T, D), F32),
            jax.ShapeDtypeStruct((T * SUBLANES, LANES), F32),
            jax.ShapeDtypeStruct((T, LANES), jnp.int32),
            jax.ShapeDtypeStruct((T, LANES), F32),
            jax.ShapeDtypeStruct((1, LANES), F32),
        ],
        scratch_shapes=[pltpu.VMEM((1, LANES), F32)],
        compiler_params=_cparams(("arbitrary",)),
        name="outproj_router",
    )(attn, lru, x2, ag, lru_out_g.reshape(1, lru_w), wa, wl, norm2_g.reshape(1, D),
      wrh, wrl, br, tri)


def _dest_row(pstart_ref, idx_ref, rank_ref, r, k):
    return pstart_ref[idx_ref[r * TOP_K + k]] + rank_ref[r * TOP_K + k]


def _dispatch_kernel(pstart_ref, fill_ref, idx_ref, rank_ref, x_ref, out_hbm, zero_ref, sem, zero_sem,
                     *, ts, n_blocks):
    block_slabs = ROW_BLOCK * SUBLANES

    def fill_copy(b):
        off = pl.multiple_of(b * block_slabs, block_slabs)
        return pltpu.make_async_copy(zero_ref, out_hbm.at[pl.ds(off, block_slabs)], zero_sem)

    @pl.when(pl.program_id(0) == 0)
    def _():
        zero_ref[...] = jnp.zeros(zero_ref.shape, F32)

        def start(b, carry):
            @pl.when(fill_ref[b] != 0)
            def _():
                fill_copy(b).start()
            return carry

        def wait(b, carry):
            @pl.when(fill_ref[b] != 0)
            def _():
                fill_copy(b).wait()
            return carry

        lax.fori_loop(0, n_blocks, start, 0)
        lax.fori_loop(0, n_blocks, wait, 0)

    def issue(i, carry):
        for j in range(ISSUE_UNROLL):
            r = i * ISSUE_UNROLL + j
            for k in range(TOP_K):
                d = _dest_row(pstart_ref, idx_ref, rank_ref, r, k)
                pltpu.make_async_copy(_slab(x_ref, r), _slab(out_hbm, d), sem).start(priority=k % 2)
        return carry

    lax.fori_loop(0, ts // ISSUE_UNROLL, issue, 0)
    for k in range(TOP_K):
        pltpu.make_async_copy(x_ref, out_hbm.at[pl.ds(0, ts * SUBLANES)], sem).wait()


def _dispatch(xn_slabs, pstart, fill, idx_flat, rank_flat, n_rows):
    T = xn_slabs.shape[0] // SUBLANES
    ts = min(TS_DISP, T)
    routing = pl.BlockSpec((ts * TOP_K,), lambda i, ps, fl: (i,), memory_space=pltpu.SMEM)
    grid_spec = pltpu.PrefetchScalarGridSpec(
        num_scalar_prefetch=2,
        grid=(T // ts,),
        in_specs=[
            routing,
            routing,
            pl.BlockSpec((ts * SUBLANES, LANES), lambda i, ps, fl: (i, 0)),
        ],
        out_specs=pl.BlockSpec(memory_space=pl.ANY),
        scratch_shapes=[pltpu.VMEM((ROW_BLOCK * SUBLANES, LANES), F32),
                        pltpu.SemaphoreType.DMA, pltpu.SemaphoreType.DMA],
    )
    return pl.pallas_call(
        functools.partial(_dispatch_kernel, ts=ts, n_blocks=n_rows // ROW_BLOCK),
        grid_spec=grid_spec,
        out_shape=jax.ShapeDtypeStruct((n_rows * SUBLANES, LANES), xn_slabs.dtype),
        compiler_params=_cparams(("arbitrary",)),
        name="dispatch",
    )(pstart, fill, idx_flat, rank_flat, xn_slabs)


def _expert_kernel(be_ref, na_ref, nxt_ref, x_ref, wg_hbm, bg_ref, wu_hbm, bu_ref, wd_hbm, bd_ref,
                   y_ref, stage_ref, wb_ref, slot_ref, sems):
    i = pl.program_id(0)
    e = be_ref[i]
    w_hbm = (wg_hbm, wu_hbm, wd_hbm)

    def fetch(expert, slot, m):
        return pltpu.make_async_copy(w_hbm[m].at[expert], stage_ref.at[slot, m], sems.at[slot, m])

    @pl.when(i == 0)
    def _():
        slot_ref[0] = 0
        for m in range(3):
            fetch(e, 0, m).start()

    active = i < na_ref[0]
    first = jnp.logical_or(i == 0, e != be_ref[jnp.maximum(i - 1, 0)])

    @pl.when(jnp.logical_and(active, first))
    def _():
        slot = slot_ref[0]
        for m in range(3):
            fetch(e, slot, m).wait()
            wb_ref[m] = stage_ref[slot, m].astype(BF16)

        @pl.when(nxt_ref[i] >= 0)
        def _():
            for m in range(3):
                fetch(nxt_ref[i], 1 - slot, m).start()

        slot_ref[0] = 1 - slot

    @pl.when(active)
    def _():
        x = _slabs_to_rows(x_ref, ROW_BLOCK).astype(BF16)
        g = jnp.dot(x, wb_ref[0], preferred_element_type=F32) + bg_ref[0]
        u = jnp.dot(x, wb_ref[1], preferred_element_type=F32) + bu_ref[0]
        g = jnp.minimum(g, SWIGLU_LIMIT)
        u = jnp.clip(u, -SWIGLU_LIMIT, SWIGLU_LIMIT)
        glu = g * jax.nn.sigmoid(SWIGLU_ALPHA * g)
        y = jnp.dot(((u + 1.0) * glu).astype(BF16), wb_ref[2], preferred_element_type=F32) + bd_ref[0]
        _rows_to_slabs(y_ref, y)


def _experts(x_rows, block_e, n_active, next_e, w_gate, b_gate, w_up, b_up, w_down, b_down):
    E, D, FF = w_gate.shape
    assert D == FF, "the three expert matrices share one staging shape"
    block_slabs = ROW_BLOCK * SUBLANES
    n_blocks = x_rows.shape[0] // block_slabs

    def row_map(i, be, na, nx):
        return (jnp.minimum(i, na[0] - 1), 0)

    def b_map(i, be, na, nx):
        return (be[jnp.minimum(i, na[0] - 1)], 0, 0)

    grid_spec = pltpu.PrefetchScalarGridSpec(
        num_scalar_prefetch=3,
        grid=(n_blocks,),
        in_specs=[
            pl.BlockSpec((block_slabs, LANES), row_map),
            pl.BlockSpec(memory_space=pl.ANY),
            pl.BlockSpec((1, 1, FF), b_map),
            pl.BlockSpec(memory_space=pl.ANY),
            pl.BlockSpec((1, 1, FF), b_map),
            pl.BlockSpec(memory_space=pl.ANY),
            pl.BlockSpec((1, 1, D), b_map),
        ],
        out_specs=pl.BlockSpec((block_slabs, LANES), row_map),
        scratch_shapes=[
            pltpu.VMEM((2, 3, D, FF), F32),
            pltpu.VMEM((3, D, FF), BF16),
            pltpu.SMEM((1,), jnp.int32),
            pltpu.SemaphoreType.DMA((2, 3)),
        ],
    )
    return pl.pallas_call(
        _expert_kernel,
        grid_spec=grid_spec,
        out_shape=jax.ShapeDtypeStruct(x_rows.shape, F32),
        input_output_aliases={3: 0},
        compiler_params=pltpu.CompilerParams(dimension_semantics=("arbitrary",),
                                             vmem_limit_bytes=EXPERT_VMEM_LIMIT),
        name="experts",
    )(block_e, n_active, next_e, x_rows, w_gate, b_gate.reshape(E, 1, FF), w_up,
      b_up.reshape(E, 1, FF), w_down, b_down.reshape(E, 1, D))


def _combine_kernel(pstart_ref, idx_ref, rank_ref, y_hbm, x1_ref, gates_ref, fg_ref, o_ref,
                    b0, b1, b2, b3, sem, *, ts):
    bufs = (b0, b1, b2, b3)

    def issue(i, carry):
        for j in range(ISSUE_UNROLL):
            r = i * ISSUE_UNROLL + j
            for k in range(TOP_K):
                d = _dest_row(pstart_ref, idx_ref, rank_ref, r, k)
                pltpu.make_async_copy(_slab(y_hbm, d), _slab(bufs[k], r), sem).start(priority=k % 2)
        return carry

    lax.fori_loop(0, ts // ISSUE_UNROLL, issue, 0)
    for k in range(TOP_K):
        pltpu.make_async_copy(y_hbm.at[pl.ds(0, ts * SUBLANES)], bufs[k], sem).wait()

    acc = x1_ref[...]
    gates = gates_ref[...]
    for k in range(TOP_K):
        acc = acc + _slabs_to_rows(bufs[k], ts) * gates[:, k:k + 1]
    ms = jnp.mean(acc * acc, axis=-1, keepdims=True)
    o_ref[...] = acc * lax.rsqrt(ms + NORM_EPS) * fg_ref[...]


def _combine(y_rows, pstart, idx_flat, rank_flat, x1, gates, final_g):
    T, D = x1.shape
    ts = min(TS_COMB, T)
    tok = lambda i, ps: (i, 0)
    routing = pl.BlockSpec((ts * TOP_K,), lambda i, ps: (i,), memory_space=pltpu.SMEM)
    grid_spec = pltpu.PrefetchScalarGridSpec(
        num_scalar_prefetch=1,
        grid=(T // ts,),
        in_specs=[
            routing,
            routing,
            pl.BlockSpec(memory_space=pl.ANY),
            pl.BlockSpec((ts, D), tok),
            pl.BlockSpec((ts, LANES), tok),
            pl.BlockSpec((1, D), lambda i, ps: (0, 0)),
        ],
        out_specs=pl.BlockSpec((ts, D), tok),
        scratch_shapes=[pltpu.VMEM((ts * SUBLANES, LANES), F32) for _ in range(TOP_K)]
        + [pltpu.SemaphoreType.DMA],
    )
    return pl.pallas_call(
        functools.partial(_combine_kernel, ts=ts),
        grid_spec=grid_spec,
        out_shape=jax.ShapeDtypeStruct((T, D), F32),
        compiler_params=_cparams(("arbitrary",)),
        name="combine",
    )(pstart, idx_flat, rank_flat, y_rows, x1, gates, final_g.reshape(1, D))


def kernel(x, norm1_g, w_in, q_norm_g, k_norm_g, conv_w, conv_b, lru_wa, lru_ba, lru_wi, lru_bi,
           lru_lam, attn_out_g, lru_out_g, w_out, norm2_g, w_router, b_router, w_gate, b_gate,
           w_up, b_up, w_down, b_down, final_g):
    B, S, D = x.shape
    T = B * S
    assert w_in.shape[0] == 1, "single-layer trunk: the final norm is fused into the layer's combine"
    x2 = x.reshape(T, D)
    for l in range(1):
        qt, k, vt, lru_x, lru_gate = _inproj(x2, norm1_g[l], w_in[l], q_norm_g[l], k_norm_g[l], S)
        attn = _attention(qt, k.reshape(B, S, -1), vt, B, S)
        lru = _lru(lru_x.reshape(B, S, -1), lru_gate.reshape(B, S, -1), conv_w[l], conv_b[l],
                   lru_wa[l], lru_ba[l], lru_wi[l], lru_bi[l], lru_lam[l], B, S)
        x1, xn3, route, gates, cnt = _outproj_router(
            attn.reshape(T, -1), lru.reshape(T, -1), x2, attn_out_g[l], lru_out_g[l], w_out[l],
            norm2_g[l], w_router[l], b_router[l])

        idx_flat = route[:, :TOP_K].reshape(T * TOP_K)
        rank_flat = route[:, TOP_K:2 * TOP_K].reshape(T * TOP_K)
        counts = cnt[0, :N_EXPERTS].astype(jnp.int32)
        padded = ((counts + ROW_BLOCK - 1) // ROW_BLOCK) * ROW_BLOCK
        pend = jnp.cumsum(padded)
        pstart = (pend - padded).astype(jnp.int32)
        n_rows = T * TOP_K + N_EXPERTS * ROW_BLOCK
        block_start = jnp.arange(n_rows // ROW_BLOCK, dtype=jnp.int32) * ROW_BLOCK
        block_e = jnp.sum((pend[None, :] <= block_start[:, None]).astype(jnp.int32), axis=1)
        block_e = jnp.minimum(block_e, N_EXPERTS - 1)
        n_active = (pend[-1:] // ROW_BLOCK).astype(jnp.int32)
        fill = jnp.logical_or(block_start + ROW_BLOCK == pend[block_e],
                              block_start >= pend[-1]).astype(jnp.int32)

        next_block = pend[block_e] // ROW_BLOCK
        next_e = jnp.where(next_block < n_active[0],
                           block_e[jnp.minimum(next_block, block_e.shape[0] - 1)], -1).astype(jnp.int32)

        x_rows = _dispatch(xn3, pstart, fill, idx_flat, rank_flat, n_rows)
        y_rows = _experts(x_rows, block_e, n_active, next_e, w_gate[l], b_gate[l], w_up[l], b_up[l],
                          w_down[l], b_down[l])
        x2 = _combine(y_rows, pstart, idx_flat, rank_flat, x1, gates, final_g)
    return x2.reshape(B, S, D)
```

```python
import functools
import math

import jax
import jax.numpy as jnp
from jax import lax
from jax.experimental import pallas as pl
from jax.experimental.pallas import tpu as pltpu

F32 = jnp.float32
BF16 = jnp.bfloat16

GRID_W = 64
HEAD_DIM = 64
N_Q_HEADS = 8
N_KV_HEADS = 2
GQA_GROUP = N_Q_HEADS // N_KV_HEADS
ATTN_W = N_Q_HEADS * HEAD_DIM
KV_W = N_KV_HEADS * HEAD_DIM
LRU_BLOCKS = 8
LRU_C = 8.0
CONV_W = 4
CONV_PAD_L = 2
ROPE_THETA = 10000.0
ROPE_HALF = HEAD_DIM // 2
ROPE_M = ROPE_HALF // 2
N_EXPERTS = 32
TOP_K = 4
SWIGLU_ALPHA = 1.702
SWIGLU_LIMIT = 7.0
NORM_EPS = 1e-5
QK_EPS = 1e-6
LOG2_E = 1.4426950408889634
Q_SCALE = HEAD_DIM ** -0.5 * LOG2_E

LANES = 128
SUBLANES = 8
BF16_SUBLANES = 16
PV_ROWS = HEAD_DIM + BF16_SUBLANES
VMEM_LIMIT = 48 * 1024 * 1024
EXPERT_VMEM_LIMIT = 56 * 1024 * 1024

TS_IN = 512
TQ = 256
TK = 1024
KV_UNROLL = 2
HEADS_PER_STEP = 2
TC_LRU = 512
TS_OUT = 512
ROW_BLOCK = 256
TS_DISP = 512
TS_COMB = 256
ISSUE_UNROLL = 8


def _cparams(sem):
    return pltpu.CompilerParams(dimension_semantics=sem, vmem_limit_bytes=VMEM_LIMIT)


def _inproj_kernel(x_ref, g1_ref, w_ref, qg_ref, kg_ref, cos_ref, sin_ref,
                   q_ref, k_ref, v_ref, lx_ref, lg_ref, *, lru_w):
    x = x_ref[...]
    ms = jnp.mean(x * x, axis=-1, keepdims=True)
    xn = x * lax.rsqrt(ms + NORM_EPS) * g1_ref[...]
    h = jnp.dot(xn.astype(BF16), w_ref[...], preferred_element_type=F32)

    cos = cos_ref[...]
    sin = sin_ref[...]
    lane = lax.broadcasted_iota(jnp.int32, cos.shape, 1)
    first_half = (lane % ROPE_HALF) < ROPE_M

    def head_norm_rope(xc, g, scale):
        hms = jnp.sum(xc * xc, axis=-1, keepdims=True) * (1.0 / HEAD_DIM)
        xc = xc * lax.rsqrt(hms + QK_EPS) * g
        partner = jnp.where(first_half,
                            pltpu.roll(xc, LANES - ROPE_M, 1),
                            pltpu.roll(xc, ROPE_M, 1))
        return (xc * cos + partner * sin) * scale

    qw = N_Q_HEADS * LANES
    kw = N_KV_HEADS * LANES
    for c in range(N_Q_HEADS):
        sl = slice(c * LANES, (c + 1) * LANES)
        q_ref[0, sl, :] = head_norm_rope(h[:, sl], qg_ref[...], Q_SCALE).T.astype(BF16)
    for c in range(N_KV_HEADS):
        sl = slice(c * LANES, (c + 1) * LANES)
        k_ref[:, sl] = head_norm_rope(h[:, qw + c * LANES: qw + (c + 1) * LANES],
                                      kg_ref[...], 1.0).astype(BF16)
        vc = h[:, qw + kw + c * LANES: qw + kw + (c + 1) * LANES]
        v_ref[0, sl, :] = jnp.where(lane >= HEAD_DIM, 1.0, vc).T.astype(BF16)
    o = qw + 2 * kw
    lx_ref[...] = h[:, o: o + lru_w]
    lg_ref[...] = h[:, o + lru_w: o + 2 * lru_w]


def _pad_heads(w, n_heads):
    lead = w.shape[:-1]
    w = w.reshape(lead + (n_heads, HEAD_DIM))
    w = jnp.pad(w, [(0, 0)] * len(lead) + [(0, 0), (0, LANES - HEAD_DIM)])
    return w.reshape(lead + (n_heads * LANES,))


def _rope_tables(S):
    t = jnp.arange(S)
    rows = (t // GRID_W).astype(F32)
    cols = (t % GRID_W).astype(F32)
    inv_freq = ROPE_THETA ** (-jnp.arange(ROPE_M, dtype=F32) / ROPE_M)
    ar = rows[:, None] * inv_freq[None, :]
    ac = cols[:, None] * inv_freq[None, :]
    cos = jnp.concatenate([jnp.cos(ar), jnp.cos(ar), jnp.cos(ac), jnp.cos(ac)], axis=-1)
    sin = jnp.concatenate([-jnp.sin(ar), jnp.sin(ar), -jnp.sin(ac), jnp.sin(ac)], axis=-1)
    pad = [(0, 0), (0, LANES - HEAD_DIM)]
    return jnp.pad(cos, pad), jnp.pad(sin, pad)


def _inproj(x2, norm1_g, w_in, q_norm_g, k_norm_g, S):
    T, D = x2.shape
    lru_w = (w_in.shape[1] - ATTN_W - 2 * KV_W) // 2
    o0, o1, o2 = ATTN_W, ATTN_W + KV_W, ATTN_W + 2 * KV_W
    w_all = jnp.concatenate([
        _pad_heads(w_in[:, :o0], N_Q_HEADS),
        _pad_heads(w_in[:, o0:o1], N_KV_HEADS),
        _pad_heads(w_in[:, o1:o2], N_KV_HEADS),
        w_in[:, o2:],
    ], axis=1).astype(BF16)
    qg = _pad_heads(q_norm_g.reshape(1, HEAD_DIM), 1)
    kg = _pad_heads(k_norm_g.reshape(1, HEAD_DIM), 1)
    cos, sin = _rope_tables(S)
    ts = TS_IN
    n_s = S // ts
    qw, kw = N_Q_HEADS * LANES, N_KV_HEADS * LANES
    const = lambda i: (0, 0)
    tok = lambda i: (i, 0)
    pos = lambda i: (i % n_s, 0)
    tposed = lambda i: (i // n_s, 0, i % n_s)
    return pl.pallas_call(
        functools.partial(_inproj_kernel, lru_w=lru_w),
        grid=(T // ts,),
        in_specs=[
            pl.BlockSpec((ts, D), tok),
            pl.BlockSpec((1, D), const),
            pl.BlockSpec(w_all.shape, const),
            pl.BlockSpec((1, LANES), const),
            pl.BlockSpec((1, LANES), const),
            pl.BlockSpec((ts, LANES), pos),
            pl.BlockSpec((ts, LANES), pos),
        ],
        out_specs=[
            pl.BlockSpec((1, qw, ts), tposed),
            pl.BlockSpec((ts, kw), tok),
            pl.BlockSpec((1, kw, ts), tposed),
            pl.BlockSpec((ts, lru_w), tok),
            pl.BlockSpec((ts, lru_w), tok),
        ],
        out_shape=[
            jax.ShapeDtypeStruct((T // S, qw, S), BF16),
            jax.ShapeDtypeStruct((T, kw), BF16),
            jax.ShapeDtypeStruct((T // S, kw, S), BF16),
            jax.ShapeDtypeStruct((T, lru_w), F32),
            jax.ShapeDtypeStruct((T, lru_w), F32),
        ],
        compiler_params=_cparams(("parallel",)),
        name="inproj",
    )(x2, norm1_g.reshape(1, D), w_all, qg, kg, cos, sin)


def _attn_kernel(qt_ref, k_ref, vt_ref, o_ref, acc_ref, s_ref, p_ref, *, tq, tk, n_kv, kv_unroll):
    hp = HEADS_PER_STEP
    spt = GQA_GROUP // hp
    acc_ref[...] = jnp.zeros(acc_ref.shape, F32)

    def scores(j, sp):
        kt = k_ref[0, pl.ds(pl.multiple_of(j * tk, tk), tk), :]
        out = []
        for u in range(hp):
            g = sp * hp + u
            s = jnp.dot(kt, qt_ref[0, g * LANES:(g + 1) * LANES, :], preferred_element_type=F32)
            out.append((s, jnp.max(s, axis=0, keepdims=True)))
        return out

    def softmax_stage(sc, ms, sp):
        out = []
        for u, (s, s_max) in enumerate(sc):
            h = sp * hp + u
            m_new = jnp.maximum(ms[h], s_max)
            out.append((jnp.exp2(ms[h] - m_new), jnp.exp2(s - m_new).astype(BF16)))
            ms[h] = m_new
        return out

    def pv_stage(j, sp, ap):
        vt = vt_ref[0, 0:PV_ROWS, pl.ds(pl.multiple_of(j * tk, tk), tk)]
        for u, (alpha, p) in enumerate(ap):
            g = sp * hp + u
            acc_ref[g] = alpha * acc_ref[g] + jnp.dot(vt, p, preferred_element_type=F32)

    ms = [jnp.full((1, tq), -jnp.inf, F32)] * GQA_GROUP
    ap = softmax_stage(scores(0, 0), ms, 0)
    sc = scores(min(1 // spt, n_kv - 1), 1 % spt)
    for u in range(hp):
        s_ref[u] = sc[u][0]
        p_ref[u] = ap[u][1]

    def body(it, carry):
        ms = list(carry[:GQA_GROUP])
        ap = [(carry[GQA_GROUP + u], p_ref[u]) for u in range(hp)]
        sc = [(s_ref[u], carry[GQA_GROUP + hp + u]) for u in range(hp)]
        for n in range(kv_unroll * spt):
            j = it * kv_unroll + n // spt
            j_next = jnp.minimum(it * kv_unroll + (n + 2) // spt, n_kv - 1)
            sc_next = scores(j_next, (n + 2) % spt)
            ap_next = softmax_stage(sc, ms, (n + 1) % spt)
            pv_stage(j, n % spt, ap)
            sc, ap = sc_next, ap_next
        for u in range(hp):
            s_ref[u] = sc[u][0]
            p_ref[u] = ap[u][1]
        return tuple(ms) + tuple(a for a, _ in ap) + tuple(m for _, m in sc)

    lax.fori_loop(0, n_kv // kv_unroll, body,
                  tuple(ms) + tuple(a for a, _ in ap) + tuple(m for _, m in sc))
    pad = jnp.zeros((LANES - HEAD_DIM, tq), F32)
    for g in range(GQA_GROUP):
        acc = acc_ref[g]
        o = acc[0:HEAD_DIM] / acc[HEAD_DIM:HEAD_DIM + 1, :]
        o_ref[0, :, g * LANES:(g + 1) * LANES] = jnp.concatenate([o, pad], axis=0).T.astype(BF16)


def _attention(qt, k, vt, B, S):
    tq = min(TQ, S)
    tk = min(TK, S)
    gw = GQA_GROUP * LANES
    return pl.pallas_call(
        functools.partial(_attn_kernel, tq=tq, tk=tk, n_kv=S // tk,
                          kv_unroll=math.gcd(S // tk, KV_UNROLL)),
        grid=(B, N_KV_HEADS, S // tq),
        in_specs=[
            pl.BlockSpec((1, gw, tq), lambda b, h, i: (b, h, i)),
            pl.BlockSpec((1, S, LANES), lambda b, h, i: (b, 0, h)),
            pl.BlockSpec((1, LANES, S), lambda b, h, i: (b, h, 0)),
        ],
        out_specs=pl.BlockSpec((1, tq, gw), lambda b, h, i: (b, i, h)),
        out_shape=jax.ShapeDtypeStruct((B, S, N_Q_HEADS * LANES), BF16),
        scratch_shapes=[pltpu.VMEM((GQA_GROUP, PV_ROWS, tq), F32),
                        pltpu.VMEM((HEADS_PER_STEP, tk, tq), F32),
                        pltpu.VMEM((HEADS_PER_STEP, tk, tq), BF16)],
        compiler_params=_cparams(("parallel", "parallel", "parallel")),
        name="attention",
    )(qt, k, vt)


def _scan_chunk(a, b, reverse):
    n = a.shape[0]
    row = lax.broadcasted_iota(jnp.int32, a.shape, 0)
    d = 1
    while d < n:
        if reverse:
            keep = row < n - d
            shift = n - d
        else:
            keep = row >= d
            shift = d
        a_sh = jnp.where(keep, pltpu.roll(a, shift, 0), 1.0)
        b_sh = jnp.where(keep, pltpu.roll(b, shift, 0), 0.0)
        b = a * b_sh + b
        a = a * a_sh
        d *= 2
    return a, b


def _lru_kernel(u_ref, gate_ref, cw_ref, cb_ref, w_ref, bias_ref, lam_ref, o_ref,
                up_ref, hf_ref, *, S, tc):
    halo = SUBLANES
    zeros = jnp.zeros((halo, LANES), F32)
    up_ref[0:halo, :] = zeros
    up_ref[S + halo:S + 2 * halo, :] = zeros
    up_ref[halo:S + halo, :] = u_ref[0]
    sp = jax.nn.softplus(-lam_ref[...])
    cw = cw_ref[...]
    cb = cb_ref[...]
    n_chunks = S // tc
    ext = tc + 2 * halo

    def gates(c, d):
        t0 = pl.multiple_of(c * tc, tc)
        ue = up_ref[pl.ds(t0, ext), :]
        xc = cb
        for j in range(CONV_W):
            sh = (CONV_PAD_L - j) % ext
            uj = ue if sh == 0 else pltpu.roll(ue, sh, 0)
            xc = xc + uj[halo:halo + tc] * cw[j:j + 1, :]
        gw = 2 * LANES
        g = jnp.dot(xc.astype(BF16), w_ref[0, :, d * gw:(d + 1) * gw],
                    preferred_element_type=F32) + bias_ref[0, :, d * gw:(d + 1) * gw]
        r = jax.nn.sigmoid(g[:, :LANES])
        i = jax.nn.sigmoid(g[:, LANES:])
        log_a = -LRU_C * r * sp[d:d + 1, :]
        a = jnp.exp(log_a)
        b = jnp.sqrt(1.0 - jnp.exp(2.0 * log_a)) * i * xc
        return t0, a, b

    def fwd(c, h):
        t0, a, b = gates(c, 0)
        pa, hb = _scan_chunk(a, b, False)
        hc = hb + pa * h
        hf_ref[pl.ds(t0, tc), :] = hc
        return hc[tc - 1:tc, :]

    lax.fori_loop(0, n_chunks, fwd, jnp.zeros((1, LANES), F32))

    def bwd(ci, h):
        t0, a, b = gates(n_chunks - 1 - ci, 1)
        pa, hb = _scan_chunk(a, b, True)
        hc = hb + pa * h
        gate = gate_ref[0, pl.ds(t0, tc), :]
        o_ref[0, pl.ds(t0, tc), :] = (hf_ref[pl.ds(t0, tc), :] + hc) * jax.nn.gelu(gate)
        return hc[0:1, :]

    lax.fori_loop(0, n_chunks, bwd, jnp.zeros((1, LANES), F32))


def _block_diag_pairs(w):
    nb, bw, _ = w.shape
    w = w.reshape(nb // 2, 2, bw, bw)
    z = jnp.zeros_like(w[:, 0])
    top = jnp.concatenate([w[:, 0], z], axis=-1)
    bot = jnp.concatenate([z, w[:, 1]], axis=-1)
    return jnp.concatenate([top, bot], axis=-2)


def _lru(lru_x, lru_gate, conv_w, conv_b, wa, ba, wi, bi, lam, B, S):
    C = lru_x.shape[-1]
    nc = C // LANES
    tc = min(TC_LRU, S)
    w = jnp.concatenate([_block_diag_pairs(wa[0]), _block_diag_pairs(wi[0]),
                         _block_diag_pairs(wa[1]), _block_diag_pairs(wi[1])], axis=-1).astype(BF16)
    bias = jnp.stack([ba[0].reshape(nc, LANES), bi[0].reshape(nc, LANES),
                      ba[1].reshape(nc, LANES), bi[1].reshape(nc, LANES)], axis=1)
    bias = bias.reshape(nc, 1, 4 * LANES)
    blk = lambda b, c: (b, 0, c)
    return pl.pallas_call(
        functools.partial(_lru_kernel, S=S, tc=tc),
        grid=(B, nc),
        in_specs=[
            pl.BlockSpec((1, S, LANES), blk),
            pl.BlockSpec((1, S, LANES), blk),
            pl.BlockSpec((CONV_W, LANES), lambda b, c: (0, c)),
            pl.BlockSpec((1, LANES), lambda b, c: (0, c)),
            pl.BlockSpec((1, LANES, 4 * LANES), lambda b, c: (c, 0, 0)),
            pl.BlockSpec((1, 1, 4 * LANES), lambda b, c: (c, 0, 0)),
            pl.BlockSpec((2, LANES), lambda b, c: (0, c)),
        ],
        out_specs=pl.BlockSpec((1, S, LANES), blk),
        out_shape=jax.ShapeDtypeStruct((B, S, C), F32),
        scratch_shapes=[
            pltpu.VMEM((S + 2 * SUBLANES, LANES), F32),
            pltpu.VMEM((S, LANES), F32),
        ],
        compiler_params=_cparams(("parallel", "parallel")),
        name="rglru",
    )(lru_x, lru_gate, conv_w, conv_b.reshape(1, C), w, bias, lam)


def _rows_to_slabs(ref, x):
    n = x.shape[0]
    for s in range(SUBLANES):
        ref[pl.ds(s, n, stride=SUBLANES), :] = x[:, s * LANES:(s + 1) * LANES]


def _slabs_to_rows(ref, n):
    return jnp.concatenate([ref[pl.ds(s, n, stride=SUBLANES), :] for s in range(SUBLANES)], axis=1)


def _slab(ref, r):
    return ref.at[pl.ds(pl.multiple_of(r * SUBLANES, SUBLANES), SUBLANES)]


def _outproj_kernel(a_ref, l_ref, x_ref, ag_ref, lg_ref, wa_ref, wl_ref, g2_ref,
                    wrh_ref, wrl_ref, br_ref, tri_ref,
                    x1_ref, xn3_ref, route_ref, gates_ref, cnt_ref, carry_ref, *, attn_w, lru_w):
    step = pl.program_id(0)

    @pl.when(step == 0)
    def _():
        carry_ref[...] = jnp.zeros_like(carry_ref)

    a = a_ref[...].astype(F32)
    ams = jnp.sum(a * a, axis=-1, keepdims=True) * (1.0 / attn_w)
    an = a * lax.rsqrt(ams + NORM_EPS) * ag_ref[...]
    l = l_ref[...]
    lms = jnp.sum(l * l, axis=-1, keepdims=True) * (1.0 / lru_w)
    ln = l * lax.rsqrt(lms + NORM_EPS) * lg_ref[...]
    mix = (jnp.dot(an.astype(BF16), wa_ref[...], preferred_element_type=F32)
           + jnp.dot(ln.astype(BF16), wl_ref[...], preferred_element_type=F32))
    x1 = x_ref[...] + mix
    x1_ref[...] = x1
    ms = jnp.mean(x1 * x1, axis=-1, keepdims=True)
    xn = x1 * lax.rsqrt(ms + NORM_EPS) * g2_ref[...]
    _rows_to_slabs(xn3_ref, xn)

    hi = xn.astype(BF16)
    lo = (xn - hi.astype(F32)).astype(BF16)
    logits = (jnp.dot(hi, wrh_ref[...], preferred_element_type=F32)
              + jnp.dot(lo, wrh_ref[...], preferred_element_type=F32)
              + jnp.dot(hi, wrl_ref[...], preferred_element_type=F32)) + br_ref[...]
    lane = lax.broadcasted_iota(jnp.int32, logits.shape, 1)
    neg = -jnp.inf
    work = jnp.where(lane < N_EXPERTS, logits, neg)
    sel = jnp.zeros(logits.shape, F32)
    idxs, vals = [], []
    for _ in range(TOP_K):
        m = jnp.max(work, axis=1, keepdims=True)
        idx = jnp.min(jnp.where(work == m, lane, LANES), axis=1, keepdims=True)
        hit = lane == idx
        work = jnp.where(hit, neg, work)
        sel = sel + hit.astype(F32)
        idxs.append(idx)
        vals.append(m)
    es = [jnp.exp(v - vals[0]) for v in vals]
    den = es[0] + es[1] + es[2] + es[3]

    prefix = jnp.dot(tri_ref[...], sel.astype(BF16), preferred_element_type=F32) + carry_ref[...]
    carry_ref[...] = carry_ref[...] + jnp.sum(sel, axis=0, keepdims=True)
    cnt_ref[...] = carry_ref[...]

    route = jnp.zeros(logits.shape, jnp.int32)
    gates = jnp.zeros(logits.shape, F32)
    for k in range(TOP_K):
        rank = jnp.sum(jnp.where(lane == idxs[k], prefix, 0.0), axis=1, keepdims=True).astype(jnp.int32)
        route = jnp.where(lane == k, idxs[k], route)
        route = jnp.where(lane == TOP_K + k, rank, route)
        gates = jnp.where(lane == k, es[k] / den, gates)
    route_ref[...] = route
    gates_ref[...] = gates


def _outproj_router(attn, lru, x2, attn_out_g, lru_out_g, w_out, norm2_g, w_router, b_router):
    T, D = x2.shape
    lru_w = lru.shape[-1]
    ts = min(TS_OUT, T)
    wa = w_out[:ATTN_W].reshape(N_Q_HEADS, HEAD_DIM, D)
    wa = jnp.pad(wa, ((0, 0), (0, LANES - HEAD_DIM), (0, 0))).reshape(N_Q_HEADS * LANES, D).astype(BF16)
    wl = w_out[ATTN_W:].astype(BF16)
    ag = _pad_heads(attn_out_g.reshape(1, ATTN_W), N_Q_HEADS)
    wr = jnp.pad(w_router, ((0, 0), (0, LANES - N_EXPERTS)))
    wrh = wr.astype(BF16)
    wrl = (wr - wrh.astype(F32)).astype(BF16)
    br = jnp.pad(b_router.reshape(1, N_EXPERTS), ((0, 0), (0, LANES - N_EXPERTS)))
    tri = (jnp.arange(ts)[:, None] > jnp.arange(ts)[None, :]).astype(BF16)
    const = lambda i: (0, 0)
    tok = lambda i: (i, 0)
    aw = N_Q_HEADS * LANES
    return pl.pallas_call(
        functools.partial(_outproj_kernel, attn_w=ATTN_W, lru_w=lru_w),
        grid=(T // ts,),
        in_specs=[
            pl.BlockSpec((ts, aw), tok),
            pl.BlockSpec((ts, lru_w), tok),
            pl.BlockSpec((ts, D), tok),
            pl.BlockSpec((1, aw), const),
            pl.BlockSpec((1, lru_w), const),
            pl.BlockSpec((aw, D), const),
            pl.BlockSpec((lru_w, D), const),
            pl.BlockSpec((1, D), const),
            pl.BlockSpec((D, LANES), const),
            pl.BlockSpec((D, LANES), const),
            pl.BlockSpec((1, LANES), const),
            pl.BlockSpec((ts, ts), const),
        ],
        out_specs=[
            pl.BlockSpec((ts, D), tok),
            pl.BlockSpec((ts * SUBLANES, LANES), tok),
            pl.BlockSpec((ts, LANES), tok),
            pl.BlockSpec((ts, LANES), tok),
            pl.BlockSpec((1, LANES), const),
        ],
        out_shape=[
            jax.ShapeDtypeStruct((T, D), F32),
            jax.ShapeDtypeStruct((T * SUBLANES, LANES), F32),
            jax.ShapeDtypeStruct((T, LANES), jnp.int32),
            jax.ShapeDtypeStruct((T, LANES), F32),
            jax.ShapeDtypeStruct((1, LANES), F32),
        ],
        scratch_shapes=[pltpu.VMEM((1, LANES), F32)],
        compiler_params=_cparams(("arbitrary",)),
        name="outproj_router",
    )(attn, lru, x2, ag, lru_out_g.reshape(1, lru_w), wa, wl, norm2_g.reshape(1, D),
      wrh, wrl, br, tri)


def _dest_row(pstart_ref, idx_ref, rank_ref, r, k):
    return pstart_ref[idx_ref[r * TOP_K + k]] + rank_ref[r * TOP_K + k]


def _dispatch_kernel(pstart_ref, fill_ref, idx_ref, rank_ref, x_ref, out_hbm, zero_ref, sem, zero_sem,
                     *, ts, n_blocks):
    block_slabs = ROW_BLOCK * SUBLANES

    def fill_copy(b):
        off = pl.multiple_of(b * block_slabs, block_slabs)
        return pltpu.make_async_copy(zero_ref, out_hbm.at[pl.ds(off, block_slabs)], zero_sem)

    @pl.when(pl.program_id(0) == 0)
    def _():
        zero_ref[...] = jnp.zeros(zero_ref.shape, F32)

        def start(b, carry):
            @pl.when(fill_ref[b] != 0)
            def _():
                fill_copy(b).start()
            return carry

        def wait(b, carry):
            @pl.when(fill_ref[b] != 0)
            def _():
                fill_copy(b).wait()
            return carry

        lax.fori_loop(0, n_blocks, start, 0)
        lax.fori_loop(0, n_blocks, wait, 0)

    def issue(i, carry):
        for j in range(ISSUE_UNROLL):
            r = i * ISSUE_UNROLL + j
            for k in range(TOP_K):
                d = _dest_row(pstart_ref, idx_ref, rank_ref, r, k)
                pltpu.make_async_copy(_slab(x_ref, r), _slab(out_hbm, d), sem).start(priority=k % 2)
        return carry

    lax.fori_loop(0, ts // ISSUE_UNROLL, issue, 0)
    for k in range(TOP_K):
        pltpu.make_async_copy(x_ref, out_hbm.at[pl.ds(0, ts * SUBLANES)], sem).wait()


def _dispatch(xn_slabs, pstart, fill, idx_flat, rank_flat, n_rows):
    T = xn_slabs.shape[0] // SUBLANES
    ts = min(TS_DISP, T)
    routing = pl.BlockSpec((ts * TOP_K,), lambda i, ps, fl: (i,), memory_space=pltpu.SMEM)
    grid_spec = pltpu.PrefetchScalarGridSpec(
        num_scalar_prefetch=2,
        grid=(T // ts,),
        in_specs=[
            routing,
            routing,
            pl.BlockSpec((ts * SUBLANES, LANES), lambda i, ps, fl: (i, 0)),
        ],
        out_specs=pl.BlockSpec(memory_space=pl.ANY),
        scratch_shapes=[pltpu.VMEM((ROW_BLOCK * SUBLANES, LANES), F32),
                        pltpu.SemaphoreType.DMA, pltpu.SemaphoreType.DMA],
    )
    return pl.pallas_call(
        functools.partial(_dispatch_kernel, ts=ts, n_blocks=n_rows // ROW_BLOCK),
        grid_spec=grid_spec,
        out_shape=jax.ShapeDtypeStruct((n_rows * SUBLANES, LANES), xn_slabs.dtype),
        compiler_params=_cparams(("arbitrary",)),
        name="dispatch",
    )(pstart, fill, idx_flat, rank_flat, xn_slabs)


def _expert_kernel(be_ref, na_ref, nxt_ref, x_ref, wg_hbm, bg_ref, wu_hbm, bu_ref, wd_hbm, bd_ref,
                   y_ref, stage_ref, wb_ref, slot_ref, sems):
    i = pl.program_id(0)
    e = be_ref[i]
    w_hbm = (wg_hbm, wu_hbm, wd_hbm)

    def fetch(expert, slot, m):
        return pltpu.make_async_copy(w_hbm[m].at[expert], stage_ref.at[slot, m], sems.at[slot, m])

    @pl.when(i == 0)
    def _():
        slot_ref[0] = 0
        for m in range(3):
            fetch(e, 0, m).start()

    active = i < na_ref[0]
    first = jnp.logical_or(i == 0, e != be_ref[jnp.maximum(i - 1, 0)])

    @pl.when(jnp.logical_and(active, first))
    def _():
        slot = slot_ref[0]
        for m in range(3):
            fetch(e, slot, m).wait()
            wb_ref[m] = stage_ref[slot, m].astype(BF16)

        @pl.when(nxt_ref[i] >= 0)
        def _():
            for m in range(3):
                fetch(nxt_ref[i], 1 - slot, m).start()

        slot_ref[0] = 1 - slot

    @pl.when(active)
    def _():
        x = _slabs_to_rows(x_ref, ROW_BLOCK).astype(BF16)
        g = jnp.dot(x, wb_ref[0], preferred_element_type=F32) + bg_ref[0]
        u = jnp.dot(x, wb_ref[1], preferred_element_type=F32) + bu_ref[0]
        g = jnp.minimum(g, SWIGLU_LIMIT)
        u = jnp.clip(u, -SWIGLU_LIMIT, SWIGLU_LIMIT)
        glu = g * jax.nn.sigmoid(SWIGLU_ALPHA * g)
        y = jnp.dot(((u + 1.0) * glu).astype(BF16), wb_ref[2], preferred_element_type=F32) + bd_ref[0]
        _rows_to_slabs(y_ref, y)


def _experts(x_rows, block_e, n_active, next_e, w_gate, b_gate, w_up, b_up, w_down, b_down):
    E, D, FF = w_gate.shape
    assert D == FF, "the three expert matrices share one staging shape"
    block_slabs = ROW_BLOCK * SUBLANES
    n_blocks = x_rows.shape[0] // block_slabs

    def row_map(i, be, na, nx):
        return (jnp.minimum(i, na[0] - 1), 0)

    def b_map(i, be, na, nx):
        return (be[jnp.minimum(i, na[0] - 1)], 0, 0)

    grid_spec = pltpu.PrefetchScalarGridSpec(
        num_scalar_prefetch=3,
        grid=(n_blocks,),
        in_specs=[
            pl.BlockSpec((block_slabs, LANES), row_map),
            pl.BlockSpec(memory_space=pl.ANY),
            pl.BlockSpec((1, 1, FF), b_map),
            pl.BlockSpec(memory_space=pl.ANY),
            pl.BlockSpec((1, 1, FF), b_map),
            pl.BlockSpec(memory_space=pl.ANY),
            pl.BlockSpec((1, 1, D), b_map),
        ],
        out_specs=pl.BlockSpec((block_slabs, LANES), row_map),
        scratch_shapes=[
            pltpu.VMEM((2, 3, D, FF), F32),
            pltpu.VMEM((3, D, FF), BF16),
            pltpu.SMEM((1,), jnp.int32),
            pltpu.SemaphoreType.DMA((2, 3)),
        ],
    )
    return pl.pallas_call(
        _expert_kernel,
        grid_spec=grid_spec,
        out_shape=jax.ShapeDtypeStruct(x_rows.shape, F32),
        input_output_aliases={3: 0},
        compiler_params=pltpu.CompilerParams(dimension_semantics=("arbitrary",),
                                             vmem_limit_bytes=EXPERT_VMEM_LIMIT),
        name="experts",
    )(block_e, n_active, next_e, x_rows, w_gate, b_gate.reshape(E, 1, FF), w_up,
      b_up.reshape(E, 1, FF), w_down, b_down.reshape(E, 1, D))


def _combine_kernel(pstart_ref, idx_ref, rank_ref, y_hbm, x1_ref, gates_ref, fg_ref, o_ref,
                    b0, b1, b2, b3, sem, *, ts):
    bufs = (b0, b1, b2, b3)

    def issue(i, carry):
        for j in range(ISSUE_UNROLL):
            r = i * ISSUE_UNROLL + j
            for k in range(TOP_K):
                d = _dest_row(pstart_ref, idx_ref, rank_ref, r, k)
                pltpu.make_async_copy(_slab(y_hbm, d), _slab(bufs[k], r), sem).start(priority=k % 2)
        return carry

    lax.fori_loop(0, ts // ISSUE_UNROLL, issue, 0)
    for k in range(TOP_K):
        pltpu.make_async_copy(y_hbm.at[pl.ds(0, ts * SUBLANES)], bufs[k], sem).wait()

    acc = x1_ref[...]
    gates = gates_ref[...]
    for k in range(TOP_K):
        acc = acc + _slabs_to_rows(bufs[k], ts) * gates[:, k:k + 1]
    ms = jnp.mean(acc * acc, axis=-1, keepdims=True)
    o_ref[...] = acc * lax.rsqrt(ms + NORM_EPS) * fg_ref[...]


def _combine(y_rows, pstart, idx_flat, rank_flat, x1, gates, final_g):
    T, D = x1.shape
    ts = min(TS_COMB, T)
    tok = lambda i, ps: (i, 0)
    routing = pl.BlockSpec((ts * TOP_K,), lambda i, ps: (i,), memory_space=pltpu.SMEM)
    grid_spec = pltpu.PrefetchScalarGridSpec(
        num_scalar_prefetch=1,
        grid=(T // ts,),
        in_specs=[
            routing,
            routing,
            pl.BlockSpec(memory_space=pl.ANY),
            pl.BlockSpec((ts, D), tok),
            pl.BlockSpec((ts, LANES), tok),
            pl.BlockSpec((1, D), lambda i, ps: (0, 0)),
        ],
        out_specs=pl.BlockSpec((ts, D), tok),
        scratch_shapes=[pltpu.VMEM((ts * SUBLANES, LANES), F32) for _ in range(TOP_K)]
        + [pltpu.SemaphoreType.DMA],
    )
    return pl.pallas_call(
        functools.partial(_combine_kernel, ts=ts),
        grid_spec=grid_spec,
        out_shape=jax.ShapeDtypeStruct((T, D), F32),
        compiler_params=_cparams(("arbitrary",)),
        name="combine",
    )(pstart, idx_flat, rank_flat, y_rows, x1, gates, final_g.reshape(1, D))


def kernel(x, norm1_g, w_in, q_norm_g, k_norm_g, conv_w, conv_b, lru_wa, lru_ba, lru_wi, lru_bi,
           lru_lam, attn_out_g, lru_out_g, w_out, norm2_g, w_router, b_router, w_gate, b_gate,
           w_up, b_up, w_down, b_down, final_g):
    B, S, D = x.shape
    T = B * S
    assert w_in.shape[0] == 1, "single-layer trunk: the final norm is fused into the layer's combine"
    x2 = x.reshape(T, D)
    for l in range(1):
        qt, k, vt, lru_x, lru_gate = _inproj(x2, norm1_g[l], w_in[l], q_norm_g[l], k_norm_g[l], S)
        attn = _attention(qt, k.reshape(B, S, -1), vt, B, S)
        lru = _lru(lru_x.reshape(B, S, -1), lru_gate.reshape(B, S, -1), conv_w[l], conv_b[l],
                   lru_wa[l], lru_ba[l], lru_wi[l], lru_bi[l], lru_lam[l], B, S)
        x1, xn3, route, gates, cnt = _outproj_router(
            attn.reshape(T, -1), lru.reshape(T, -1), x2, attn_out_g[l], lru_out_g[l], w_out[l],
            norm2_g[l], w_router[l], b_router[l])

        idx_flat = route[:, :TOP_K].reshape(T * TOP_K)
        rank_flat = route[:, TOP_K:2 * TOP_K].reshape(T * TOP_K)
        counts = cnt[0, :N_EXPERTS].astype(jnp.int32)
        padded = ((counts + ROW_BLOCK - 1) // ROW_BLOCK) * ROW_BLOCK
        pend = jnp.cumsum(padded)
        pstart = (pend - padded).astype(jnp.int32)
        n_rows = T * TOP_K + N_EXPERTS * ROW_BLOCK
        block_start = jnp.arange(n_rows // ROW_BLOCK, dtype=jnp.int32) * ROW_BLOCK
        block_e = jnp.sum((pend[None, :] <= block_start[:, None]).astype(jnp.int32), axis=1)
        block_e = jnp.minimum(block_e, N_EXPERTS - 1)
        n_active = (pend[-1:] // ROW_BLOCK).astype(jnp.int32)
        fill = jnp.logical_or(block_start + ROW_BLOCK == pend[block_e],
                              block_start >= pend[-1]).astype(jnp.int32)

        next_block = pend[block_e] // ROW_BLOCK
        next_e = jnp.where(next_block < n_active[0],
                           block_e[jnp.minimum(next_block, block_e.shape[0] - 1)], -1).astype(jnp.int32)

        x_rows = _dispatch(xn3, pstart, fill, idx_flat, rank_flat, n_rows)
        y_rows = _experts(x_rows, block_e, n_active, next_e, w_gate[l], b_gate[l], w_up[l], b_up[l],
                          w_down[l], b_down[l])
        x2 = _combine(y_rows, pstart, idx_flat, rank_flat, x1, gates, final_g)
    return x2.reshape(B, S, D)
```

```python
import functools
import math

import jax
import jax.numpy as jnp
from jax import lax
from jax.experimental import pallas as pl
from jax.experimental.pallas import tpu as pltpu

F32 = jnp.float32
BF16 = jnp.bfloat16

GRID_W = 64
HEAD_DIM = 64
N_Q_HEADS = 8
N_KV_HEADS = 2
GQA_GROUP = N_Q_HEADS // N_KV_HEADS
ATTN_W = N_Q_HEADS * HEAD_DIM
KV_W = N_KV_HEADS * HEAD_DIM
LRU_BLOCKS = 8
LRU_C = 8.0
CONV_W = 4
CONV_PAD_L = 2
ROPE_THETA = 10000.0
ROPE_HALF = HEAD_DIM // 2
ROPE_M = ROPE_HALF // 2
N_EXPERTS = 32
TOP_K = 4
SWIGLU_ALPHA = 1.702
SWIGLU_LIMIT = 7.0
NORM_EPS = 1e-5
QK_EPS = 1e-6
LOG2_E = 1.4426950408889634
Q_SCALE = HEAD_DIM ** -0.5 * LOG2_E

LANES = 128
SUBLANES = 8
BF16_SUBLANES = 16
PV_ROWS = LANES
VMEM_LIMIT = 48 * 1024 * 1024
EXPERT_VMEM_LIMIT = 56 * 1024 * 1024

TS_IN = 512
TQ = 256
TK = 1024
KV_UNROLL = 2
HEADS_PER_STEP = 2
TC_LRU = 512
TS_OUT = 512
ROW_BLOCK = 256
TS_DISP = 512
TS_COMB = 256
ISSUE_UNROLL = 8


def _cparams(sem):
    return pltpu.CompilerParams(dimension_semantics=sem, vmem_limit_bytes=VMEM_LIMIT)


def _inproj_kernel(x_ref, g1_ref, w_ref, qg_ref, kg_ref, cos_ref, sin_ref,
                   q_ref, k_ref, v_ref, lx_ref, lg_ref, *, lru_w):
    x = x_ref[...]
    ms = jnp.mean(x * x, axis=-1, keepdims=True)
    xn = x * lax.rsqrt(ms + NORM_EPS) * g1_ref[...]
    h = jnp.dot(xn.astype(BF16), w_ref[...], preferred_element_type=F32)

    cos = cos_ref[...]
    sin = sin_ref[...]
    lane = lax.broadcasted_iota(jnp.int32, cos.shape, 1)
    first_half = (lane % ROPE_HALF) < ROPE_M

    def head_norm_rope(xc, g, scale):
        hms = jnp.sum(xc * xc, axis=-1, keepdims=True) * (1.0 / HEAD_DIM)
        xc = xc * lax.rsqrt(hms + QK_EPS) * g
        partner = jnp.where(first_half,
                            pltpu.roll(xc, LANES - ROPE_M, 1),
                            pltpu.roll(xc, ROPE_M, 1))
        return (xc * cos + partner * sin) * scale

    qw = N_Q_HEADS * LANES
    kw = N_KV_HEADS * LANES
    for c in range(N_Q_HEADS):
        sl = slice(c * LANES, (c + 1) * LANES)
        q_ref[0, sl, :] = head_norm_rope(h[:, sl], qg_ref[...], Q_SCALE).T.astype(BF16)
    for c in range(N_KV_HEADS):
        sl = slice(c * LANES, (c + 1) * LANES)
        k_ref[:, sl] = head_norm_rope(h[:, qw + c * LANES: qw + (c + 1) * LANES],
                                      kg_ref[...], 1.0).astype(BF16)
        vc = h[:, qw + kw + c * LANES: qw + kw + (c + 1) * LANES]
        v_ref[0, sl, :] = jnp.where(lane >= HEAD_DIM, 1.0, vc).T.astype(BF16)
    o = qw + 2 * kw
    lx_ref[...] = h[:, o: o + lru_w]
    lg_ref[...] = h[:, o + lru_w: o + 2 * lru_w]


def _pad_heads(w, n_heads):
    lead = w.shape[:-1]
    w = w.reshape(lead + (n_heads, HEAD_DIM))
    w = jnp.pad(w, [(0, 0)] * len(lead) + [(0, 0), (0, LANES - HEAD_DIM)])
    return w.reshape(lead + (n_heads * LANES,))


def _rope_tables(S):
    t = jnp.arange(S)
    rows = (t // GRID_W).astype(F32)
    cols = (t % GRID_W).astype(F32)
    inv_freq = ROPE_THETA ** (-jnp.arange(ROPE_M, dtype=F32) / ROPE_M)
    ar = rows[:, None] * inv_freq[None, :]
    ac = cols[:, None] * inv_freq[None, :]
    cos = jnp.concatenate([jnp.cos(ar), jnp.cos(ar), jnp.cos(ac), jnp.cos(ac)], axis=-1)
    sin = jnp.concatenate([-jnp.sin(ar), jnp.sin(ar), -jnp.sin(ac), jnp.sin(ac)], axis=-1)
    pad = [(0, 0), (0, LANES - HEAD_DIM)]
    return jnp.pad(cos, pad), jnp.pad(sin, pad)


def _inproj(x2, norm1_g, w_in, q_norm_g, k_norm_g, S):
    T, D = x2.shape
    lru_w = (w_in.shape[1] - ATTN_W - 2 * KV_W) // 2
    o0, o1, o2 = ATTN_W, ATTN_W + KV_W, ATTN_W + 2 * KV_W
    w_all = jnp.concatenate([
        _pad_heads(w_in[:, :o0], N_Q_HEADS),
        _pad_heads(w_in[:, o0:o1], N_KV_HEADS),
        _pad_heads(w_in[:, o1:o2], N_KV_HEADS),
        w_in[:, o2:],
    ], axis=1).astype(BF16)
    qg = _pad_heads(q_norm_g.reshape(1, HEAD_DIM), 1)
    kg = _pad_heads(k_norm_g.reshape(1, HEAD_DIM), 1)
    cos, sin = _rope_tables(S)
    ts = TS_IN
    n_s = S // ts
    qw, kw = N_Q_HEADS * LANES, N_KV_HEADS * LANES
    const = lambda i: (0, 0)
    tok = lambda i: (i, 0)
    pos = lambda i: (i % n_s, 0)
    tposed = lambda i: (i // n_s, 0, i % n_s)
    return pl.pallas_call(
        functools.partial(_inproj_kernel, lru_w=lru_w),
        grid=(T // ts,),
        in_specs=[
            pl.BlockSpec((ts, D), tok),
            pl.BlockSpec((1, D), const),
            pl.BlockSpec(w_all.shape, const),
            pl.BlockSpec((1, LANES), const),
            pl.BlockSpec((1, LANES), const),
            pl.BlockSpec((ts, LANES), pos),
            pl.BlockSpec((ts, LANES), pos),
        ],
        out_specs=[
            pl.BlockSpec((1, qw, ts), tposed),
            pl.BlockSpec((ts, kw), tok),
            pl.BlockSpec((1, kw, ts), tposed),
            pl.BlockSpec((ts, lru_w), tok),
            pl.BlockSpec((ts, lru_w), tok),
        ],
        out_shape=[
            jax.ShapeDtypeStruct((T // S, qw, S), BF16),
            jax.ShapeDtypeStruct((T, kw), BF16),
            jax.ShapeDtypeStruct((T // S, kw, S), BF16),
            jax.ShapeDtypeStruct((T, lru_w), F32),
            jax.ShapeDtypeStruct((T, lru_w), F32),
        ],
        compiler_params=_cparams(("parallel",)),
        name="inproj",
    )(x2, norm1_g.reshape(1, D), w_all, qg, kg, cos, sin)


def _attn_kernel(qt_ref, k_ref, vt_ref, o_ref, acc_ref, s_ref, p_ref, *, tq, tk, n_kv, kv_unroll):
    hp = HEADS_PER_STEP
    spt = GQA_GROUP // hp
    acc_ref[...] = jnp.zeros(acc_ref.shape, F32)

    def scores(j, sp):
        kt = k_ref[0, pl.ds(pl.multiple_of(j * tk, tk), tk), :]
        out = []
        for u in range(hp):
            g = sp * hp + u
            s = jnp.dot(kt, qt_ref[0, g * LANES:(g + 1) * LANES, :], preferred_element_type=F32)
            out.append((s, jnp.max(s, axis=0, keepdims=True)))
        return out

    def softmax_stage(sc, ms, sp):
        out = []
        for u, (s, s_max) in enumerate(sc):
            h = sp * hp + u
            m_new = jnp.maximum(ms[h], s_max)
            out.append((jnp.exp2(ms[h] - m_new), jnp.exp2(s - m_new).astype(BF16)))
            ms[h] = m_new
        return out

    def pv_stage(j, sp, ap):
        vt = vt_ref[0, 0:PV_ROWS, pl.ds(pl.multiple_of(j * tk, tk), tk)]
        for u, (alpha, p) in enumerate(ap):
            g = sp * hp + u
            acc_ref[g] = alpha * acc_ref[g] + jnp.dot(vt, p, preferred_element_type=F32)

    ms = [jnp.full((1, tq), -jnp.inf, F32)] * GQA_GROUP
    ap = softmax_stage(scores(0, 0), ms, 0)
    sc = scores(min(1 // spt, n_kv - 1), 1 % spt)
    for u in range(hp):
        s_ref[u] = sc[u][0]
        p_ref[u] = ap[u][1]

    def body(it, carry):
        ms = list(carry[:GQA_GROUP])
        ap = [(carry[GQA_GROUP + u], p_ref[u]) for u in range(hp)]
        sc = [(s_ref[u], carry[GQA_GROUP + hp + u]) for u in range(hp)]
        for n in range(kv_unroll * spt):
            j = it * kv_unroll + n // spt
            j_next = jnp.minimum(it * kv_unroll + (n + 2) // spt, n_kv - 1)
            sc_next = scores(j_next, (n + 2) % spt)
            ap_next = softmax_stage(sc, ms, (n + 1) % spt)
            pv_stage(j, n % spt, ap)
            sc, ap = sc_next, ap_next
        for u in range(hp):
            s_ref[u] = sc[u][0]
            p_ref[u] = ap[u][1]
        return tuple(ms) + tuple(a for a, _ in ap) + tuple(m for _, m in sc)

    lax.fori_loop(0, n_kv // kv_unroll, body,
                  tuple(ms) + tuple(a for a, _ in ap) + tuple(m for _, m in sc))
    pad = jnp.zeros((LANES - HEAD_DIM, tq), F32)
    for g in range(GQA_GROUP):
        acc = acc_ref[g]
        o = acc[0:HEAD_DIM] / acc[HEAD_DIM:HEAD_DIM + 1, :]
        o_ref[0, :, g * LANES:(g + 1) * LANES] = jnp.concatenate([o, pad], axis=0).T.astype(BF16)


def _attention(qt, k, vt, B, S):
    tq = min(TQ, S)
    tk = min(TK, S)
    gw = GQA_GROUP * LANES
    return pl.pallas_call(
        functools.partial(_attn_kernel, tq=tq, tk=tk, n_kv=S // tk,
                          kv_unroll=math.gcd(S // tk, KV_UNROLL)),
        grid=(B, N_KV_HEADS, S // tq),
        in_specs=[
            pl.BlockSpec((1, gw, tq), lambda b, h, i: (b, h, i)),
            pl.BlockSpec((1, S, LANES), lambda b, h, i: (b, 0, h)),
            pl.BlockSpec((1, LANES, S), lambda b, h, i: (b, h, 0)),
        ],
        out_specs=pl.BlockSpec((1, tq, gw), lambda b, h, i: (b, i, h)),
        out_shape=jax.ShapeDtypeStruct((B, S, N_Q_HEADS * LANES), BF16),
        scratch_shapes=[pltpu.VMEM((GQA_GROUP, PV_ROWS, tq), F32),
                        pltpu.VMEM((HEADS_PER_STEP, tk, tq), F32),
                        pltpu.VMEM((HEADS_PER_STEP, tk, tq), BF16)],
        compiler_params=_cparams(("parallel", "parallel", "parallel")),
        name="attention",
    )(qt, k, vt)


def _scan_chunk(a, b, reverse):
    n = a.shape[0]
    row = lax.broadcasted_iota(jnp.int32, a.shape, 0)
    d = 1
    while d < n:
        if reverse:
            keep = row < n - d
            shift = n - d
        else:
            keep = row >= d
            shift = d
        a_sh = jnp.where(keep, pltpu.roll(a, shift, 0), 1.0)
        b_sh = jnp.where(keep, pltpu.roll(b, shift, 0), 0.0)
        b = a * b_sh + b
        a = a * a_sh
        d *= 2
    return a, b


def _lru_kernel(u_ref, gate_ref, cw_ref, cb_ref, w_ref, bias_ref, lam_ref, o_ref,
                up_ref, hf_ref, *, S, tc):
    halo = SUBLANES
    zeros = jnp.zeros((halo, LANES), F32)
    up_ref[0:halo, :] = zeros
    up_ref[S + halo:S + 2 * halo, :] = zeros
    up_ref[halo:S + halo, :] = u_ref[0]
    sp = jax.nn.softplus(-lam_ref[...])
    cw = cw_ref[...]
    cb = cb_ref[...]
    n_chunks = S // tc
    ext = tc + 2 * halo

    def gates(c, d):
        t0 = pl.multiple_of(c * tc, tc)
        ue = up_ref[pl.ds(t0, ext), :]
        xc = cb
        for j in range(CONV_W):
            sh = (CONV_PAD_L - j) % ext
            uj = ue if sh == 0 else pltpu.roll(ue, sh, 0)
            xc = xc + uj[halo:halo + tc] * cw[j:j + 1, :]
        gw = 2 * LANES
        g = jnp.dot(xc.astype(BF16), w_ref[0, :, d * gw:(d + 1) * gw],
                    preferred_element_type=F32) + bias_ref[0, :, d * gw:(d + 1) * gw]
        r = jax.nn.sigmoid(g[:, :LANES])
        i = jax.nn.sigmoid(g[:, LANES:])
        log_a = -LRU_C * r * sp[d:d + 1, :]
        a = jnp.exp(log_a)
        b = jnp.sqrt(1.0 - jnp.exp(2.0 * log_a)) * i * xc
        return t0, a, b

    def fwd(c, h):
        t0, a, b = gates(c, 0)
        pa, hb = _scan_chunk(a, b, False)
        hc = hb + pa * h
        hf_ref[pl.ds(t0, tc), :] = hc
        return hc[tc - 1:tc, :]

    lax.fori_loop(0, n_chunks, fwd, jnp.zeros((1, LANES), F32))

    def bwd(ci, h):
        t0, a, b = gates(n_chunks - 1 - ci, 1)
        pa, hb = _scan_chunk(a, b, True)
        hc = hb + pa * h
        gate = gate_ref[0, pl.ds(t0, tc), :]
        o_ref[0, pl.ds(t0, tc), :] = (hf_ref[pl.ds(t0, tc), :] + hc) * jax.nn.gelu(gate)
        return hc[0:1, :]

    lax.fori_loop(0, n_chunks, bwd, jnp.zeros((1, LANES), F32))


def _block_diag_pairs(w):
    nb, bw, _ = w.shape
    w = w.reshape(nb // 2, 2, bw, bw)
    z = jnp.zeros_like(w[:, 0])
    top = jnp.concatenate([w[:, 0], z], axis=-1)
    bot = jnp.concatenate([z, w[:, 1]], axis=-1)
    return jnp.concatenate([top, bot], axis=-2)


def _lru(lru_x, lru_gate, conv_w, conv_b, wa, ba, wi, bi, lam, B, S):
    C = lru_x.shape[-1]
    nc = C // LANES
    tc = min(TC_LRU, S)
    w = jnp.concatenate([_block_diag_pairs(wa[0]), _block_diag_pairs(wi[0]),
                         _block_diag_pairs(wa[1]), _block_diag_pairs(wi[1])], axis=-1).astype(BF16)
    bias = jnp.stack([ba[0].reshape(nc, LANES), bi[0].reshape(nc, LANES),
                      ba[1].reshape(nc, LANES), bi[1].reshape(nc, LANES)], axis=1)
    bias = bias.reshape(nc, 1, 4 * LANES)
    blk = lambda b, c: (b, 0, c)
    return pl.pallas_call(
        functools.partial(_lru_kernel, S=S, tc=tc),
        grid=(B, nc),
        in_specs=[
            pl.BlockSpec((1, S, LANES), blk),
            pl.BlockSpec((1, S, LANES), blk),
            pl.BlockSpec((CONV_W, LANES), lambda b, c: (0, c)),
            pl.BlockSpec((1, LANES), lambda b, c: (0, c)),
            pl.BlockSpec((1, LANES, 4 * LANES), lambda b, c: (c, 0, 0)),
            pl.BlockSpec((1, 1, 4 * LANES), lambda b, c: (c, 0, 0)),
            pl.BlockSpec((2, LANES), lambda b, c: (0, c)),
        ],
        out_specs=pl.BlockSpec((1, S, LANES), blk),
        out_shape=jax.ShapeDtypeStruct((B, S, C), F32),
        scratch_shapes=[
            pltpu.VMEM((S + 2 * SUBLANES, LANES), F32),
            pltpu.VMEM((S, LANES), F32),
        ],
        compiler_params=_cparams(("parallel", "parallel")),
        name="rglru",
    )(lru_x, lru_gate, conv_w, conv_b.reshape(1, C), w, bias, lam)


def _rows_to_slabs(ref, x):
    n = x.shape[0]
    for s in range(SUBLANES):
        ref[pl.ds(s, n, stride=SUBLANES), :] = x[:, s * LANES:(s + 1) * LANES]


def _slabs_to_rows(ref, n):
    return jnp.concatenate([ref[pl.ds(s, n, stride=SUBLANES), :] for s in range(SUBLANES)], axis=1)


def _slab(ref, r):
    return ref.at[pl.ds(pl.multiple_of(r * SUBLANES, SUBLANES), SUBLANES)]


def _outproj_kernel(a_ref, l_ref, x_ref, ag_ref, lg_ref, wa_ref, wl_ref, g2_ref,
                    wrh_ref, wrl_ref, br_ref, tri_ref,
                    x1_ref, xn3_ref, route_ref, gates_ref, cnt_ref, carry_ref, *, attn_w, lru_w):
    step = pl.program_id(0)

    @pl.when(step == 0)
    def _():
        carry_ref[...] = jnp.zeros_like(carry_ref)

    a = a_ref[...].astype(F32)
    ams = jnp.sum(a * a, axis=-1, keepdims=True) * (1.0 / attn_w)
    an = a * lax.rsqrt(ams + NORM_EPS) * ag_ref[...]
    l = l_ref[...]
    lms = jnp.sum(l * l, axis=-1, keepdims=True) * (1.0 / lru_w)
    ln = l * lax.rsqrt(lms + NORM_EPS) * lg_ref[...]
    mix = (jnp.dot(an.astype(BF16), wa_ref[...], preferred_element_type=F32)
           + jnp.dot(ln.astype(BF16), wl_ref[...], preferred_element_type=F32))
    x1 = x_ref[...] + mix
    x1_ref[...] = x1
    ms = jnp.mean(x1 * x1, axis=-1, keepdims=True)
    xn = x1 * lax.rsqrt(ms + NORM_EPS) * g2_ref[...]
    _rows_to_slabs(xn3_ref, xn)

    hi = xn.astype(BF16)
    lo = (xn - hi.astype(F32)).astype(BF16)
    logits = (jnp.dot(hi, wrh_ref[...], preferred_element_type=F32)
              + jnp.dot(lo, wrh_ref[...], preferred_element_type=F32)
              + jnp.dot(hi, wrl_ref[...], preferred_element_type=F32)) + br_ref[...]
    lane = lax.broadcasted_iota(jnp.int32, logits.shape, 1)
    neg = -jnp.inf
    work = jnp.where(lane < N_EXPERTS, logits, neg)
    sel = jnp.zeros(logits.shape, F32)
    idxs, vals = [], []
    for _ in range(TOP_K):
        m = jnp.max(work, axis=1, keepdims=True)
        idx = jnp.min(jnp.where(work == m, lane, LANES), axis=1, keepdims=True)
        hit = lane == idx
        work = jnp.where(hit, neg, work)
        sel = sel + hit.astype(F32)
        idxs.append(idx)
        vals.append(m)
    es = [jnp.exp(v - vals[0]) for v in vals]
    den = es[0] + es[1] + es[2] + es[3]

    prefix = jnp.dot(tri_ref[...], sel.astype(BF16), preferred_element_type=F32) + carry_ref[...]
    carry_ref[...] = carry_ref[...] + jnp.sum(sel, axis=0, keepdims=True)
    cnt_ref[...] = carry_ref[...]

    route = jnp.zeros(logits.shape, jnp.int32)
    gates = jnp.zeros(logits.shape, F32)
    for k in range(TOP_K):
        rank = jnp.sum(jnp.where(lane == idxs[k], prefix, 0.0), axis=1, keepdims=True).astype(jnp.int32)
        route = jnp.where(lane == k, idxs[k], route)
        route = jnp.where(lane == TOP_K + k, rank, route)
        gates = jnp.where(lane == k, es[k] / den, gates)
    route_ref[...] = route
    gates_ref[...] = gates


def _outproj_router(attn, lru, x2, attn_out_g, lru_out_g, w_out, norm2_g, w_router, b_router):
    T, D = x2.shape
    lru_w = lru.shape[-1]
    ts = min(TS_OUT, T)
    wa = w_out[:ATTN_W].reshape(N_Q_HEADS, HEAD_DIM, D)
    wa = jnp.pad(wa, ((0, 0), (0, LANES - HEAD_DIM), (0, 0))).reshape(N_Q_HEADS * LANES, D).astype(BF16)
    wl = w_out[ATTN_W:].astype(BF16)
    ag = _pad_heads(attn_out_g.reshape(1, ATTN_W), N_Q_HEADS)
    wr = jnp.pad(w_router, ((0, 0), (0, LANES - N_EXPERTS)))
    wrh = wr.astype(BF16)
    wrl = (wr - wrh.astype(F32)).astype(BF16)
    br = jnp.pad(b_router.reshape(1, N_EXPERTS), ((0, 0), (0, LANES - N_EXPERTS)))
    tri = (jnp.arange(ts)[:, None] > jnp.arange(ts)[None, :]).astype(BF16)
    const = lambda i: (0, 0)
    tok = lambda i: (i, 0)
    aw = N_Q_HEADS * LANES
    return pl.pallas_call(
        functools.partial(_outproj_kernel, attn_w=ATTN_W, lru_w=lru_w),
        grid=(T // ts,),
        in_specs=[
            pl.BlockSpec((ts, aw), tok),
            pl.BlockSpec((ts, lru_w), tok),
            pl.BlockSpec((ts, D), tok),
            pl.BlockSpec((1, aw), const),
            pl.BlockSpec((1, lru_w), const),
            pl.BlockSpec((aw, D), const),
            pl.BlockSpec((lru_w, D), const),
            pl.BlockSpec((1, D), const),
            pl.BlockSpec((D, LANES), const),
            pl.BlockSpec((D, LANES), const),
            pl.BlockSpec((1, LANES), const),
            pl.BlockSpec((ts, ts), const),
        ],
        out_specs=[
            pl.BlockSpec((ts, D), tok),
            pl.BlockSpec((ts * SUBLANES, LANES), tok),
            pl.BlockSpec((ts, LANES), tok),
            pl.BlockSpec((ts, LANES), tok),
            pl.BlockSpec((1, LANES), const),
        ],
        out_shape=[
            jax.ShapeDtypeStruct((T, D), F32),
            jax.ShapeDtypeStruct((T * SUBLANES, LANES), F32),
            jax.ShapeDtypeStruct((T, LANES), jnp.int32),
            jax.ShapeDtypeStruct((T, LANES), F32),
            jax.ShapeDtypeStruct((1, LANES), F32),
        ],
        scratch_shapes=[pltpu.VMEM((1, LANES), F32)],
        compiler_params=_cparams(("arbitrary",)),
        name="outproj_router",
    )(attn, lru, x2, ag, lru_out_g.reshape(1, lru_w), wa, wl, norm2_g.reshape(1, D),
      wrh, wrl, br, tri)


def _dest_kernel(route_ref, pstart_ref, dest_ref):
    route = route_ref[...]
    lane = lax.broadcasted_iota(jnp.int32, route.shape, 1)
    pstart = pstart_ref[...]
    dest = jnp.zeros(route.shape, jnp.int32)
    for k in range(TOP_K):
        start = jnp.sum(jnp.where(lane == route[:, k:k + 1], pstart, 0.0), axis=1, keepdims=True)
        dest = jnp.where(lane == k, start.astype(jnp.int32) + route[:, TOP_K + k:TOP_K + k + 1], dest)
    dest_ref[...] = dest


def _dest_rows(route, pstart):
    T = route.shape[0]
    ts = min(TS_OUT, T)
    row = jnp.pad(pstart.astype(F32).reshape(1, N_EXPERTS), ((0, 0), (0, LANES - N_EXPERTS)))
    dest = pl.pallas_call(
        _dest_kernel,
        grid=(T // ts,),
        in_specs=[pl.BlockSpec((ts, LANES), lambda i: (i, 0)),
                  pl.BlockSpec((1, LANES), lambda i: (0, 0))],
        out_specs=pl.BlockSpec((ts, LANES), lambda i: (i, 0)),
        out_shape=jax.ShapeDtypeStruct((T, LANES), jnp.int32),
        compiler_params=_cparams(("parallel",)),
        name="dest_rows",
    )(route, row)
    return dest[:, :TOP_K].reshape(T * TOP_K)


def _dispatch_kernel(fill_ref, dest_ref, x_ref, out_hbm, zero_ref, sem, zero_sem, *, ts, n_blocks):
    block_slabs = ROW_BLOCK * SUBLANES

    def fill_copy(b):
        off = pl.multiple_of(b * block_slabs, block_slabs)
        return pltpu.make_async_copy(zero_ref, out_hbm.at[pl.ds(off, block_slabs)], zero_sem)

    @pl.when(pl.program_id(0) == 0)
    def _():
        zero_ref[...] = jnp.zeros(zero_ref.shape, F32)

        def start(b, carry):
            @pl.when(fill_ref[b] != 0)
            def _():
                fill_copy(b).start()
            return carry

        def wait(b, carry):
            @pl.when(fill_ref[b] != 0)
            def _():
                fill_copy(b).wait()
            return carry

        lax.fori_loop(0, n_blocks, start, 0)
        lax.fori_loop(0, n_blocks, wait, 0)

    def issue(i, carry):
        for j in range(ISSUE_UNROLL):
            r = i * ISSUE_UNROLL + j
            for k in range(TOP_K):
                d = dest_ref[r * TOP_K + k]
                pltpu.make_async_copy(_slab(x_ref, r), _slab(out_hbm, d), sem).start(priority=k % 2)
        return carry

    lax.fori_loop(0, ts // ISSUE_UNROLL, issue, 0)
    for k in range(TOP_K):
        pltpu.make_async_copy(x_ref, out_hbm.at[pl.ds(0, ts * SUBLANES)], sem).wait()


def _dispatch(xn_slabs, fill, dest_flat, n_rows):
    T = xn_slabs.shape[0] // SUBLANES
    ts = min(TS_DISP, T)
    grid_spec = pltpu.PrefetchScalarGridSpec(
        num_scalar_prefetch=1,
        grid=(T // ts,),
        in_specs=[
            pl.BlockSpec((ts * TOP_K,), lambda i, fl: (i,), memory_space=pltpu.SMEM),
            pl.BlockSpec((ts * SUBLANES, LANES), lambda i, fl: (i, 0)),
        ],
        out_specs=pl.BlockSpec(memory_space=pl.ANY),
        scratch_shapes=[pltpu.VMEM((ROW_BLOCK * SUBLANES, LANES), F32),
                        pltpu.SemaphoreType.DMA, pltpu.SemaphoreType.DMA],
    )
    return pl.pallas_call(
        functools.partial(_dispatch_kernel, ts=ts, n_blocks=n_rows // ROW_BLOCK),
        grid_spec=grid_spec,
        out_shape=jax.ShapeDtypeStruct((n_rows * SUBLANES, LANES), xn_slabs.dtype),
        compiler_params=_cparams(("arbitrary",)),
        name="dispatch",
    )(fill, dest_flat, xn_slabs)


def _expert_kernel(be_ref, na_ref, nxt_ref, x_ref, wg_hbm, bg_ref, wu_hbm, bu_ref, wd_hbm, bd_ref,
                   y_ref, stage_ref, wb_ref, slot_ref, sems):
    i = pl.program_id(0)
    e = be_ref[i]
    w_hbm = (wg_hbm, wu_hbm, wd_hbm)

    def fetch(expert, slot, m):
        return pltpu.make_async_copy(w_hbm[m].at[expert], stage_ref.at[slot, m], sems.at[slot, m])

    @pl.when(i == 0)
    def _():
        slot_ref[0] = 0
        for m in range(3):
            fetch(e, 0, m).start()

    active = i < na_ref[0]
    first = jnp.logical_or(i == 0, e != be_ref[jnp.maximum(i - 1, 0)])

    @pl.when(jnp.logical_and(active, first))
    def _():
        slot = slot_ref[0]
        for m in range(3):
            fetch(e, slot, m).wait()
            wb_ref[m] = stage_ref[slot, m].astype(BF16)

        @pl.when(nxt_ref[i] >= 0)
        def _():
            for m in range(3):
                fetch(nxt_ref[i], 1 - slot, m).start()

        slot_ref[0] = 1 - slot

    @pl.when(active)
    def _():
        x = _slabs_to_rows(x_ref, ROW_BLOCK).astype(BF16)
        g = jnp.dot(x, wb_ref[0], preferred_element_type=F32) + bg_ref[0]
        u = jnp.dot(x, wb_ref[1], preferred_element_type=F32) + bu_ref[0]
        g = jnp.minimum(g, SWIGLU_LIMIT)
        u = jnp.clip(u, -SWIGLU_LIMIT, SWIGLU_LIMIT)
        glu = g * jax.nn.sigmoid(SWIGLU_ALPHA * g)
        y = jnp.dot(((u + 1.0) * glu).astype(BF16), wb_ref[2], preferred_element_type=F32) + bd_ref[0]
        _rows_to_slabs(y_ref, y)


def _experts(x_rows, block_e, n_active, next_e, w_gate, b_gate, w_up, b_up, w_down, b_down):
    E, D, FF = w_gate.shape
    assert D == FF, "the three expert matrices share one staging shape"
    block_slabs = ROW_BLOCK * SUBLANES
    n_blocks = x_rows.shape[0] // block_slabs

    def row_map(i, be, na, nx):
        return (jnp.minimum(i, na[0] - 1), 0)

    def b_map(i, be, na, nx):
        return (be[jnp.minimum(i, na[0] - 1)], 0, 0)

    grid_spec = pltpu.PrefetchScalarGridSpec(
        num_scalar_prefetch=3,
        grid=(n_blocks,),
        in_specs=[
            pl.BlockSpec((block_slabs, LANES), row_map),
            pl.BlockSpec(memory_space=pl.ANY),
            pl.BlockSpec((1, 1, FF), b_map),
            pl.BlockSpec(memory_space=pl.ANY),
            pl.BlockSpec((1, 1, FF), b_map),
            pl.BlockSpec(memory_space=pl.ANY),
            pl.BlockSpec((1, 1, D), b_map),
        ],
        out_specs=pl.BlockSpec((block_slabs, LANES), row_map),
        scratch_shapes=[
            pltpu.VMEM((2, 3, D, FF), F32),
            pltpu.VMEM((3, D, FF), BF16),
            pltpu.SMEM((1,), jnp.int32),
            pltpu.SemaphoreType.DMA((2, 3)),
        ],
    )
    return pl.pallas_call(
        _expert_kernel,
        grid_spec=grid_spec,
        out_shape=jax.ShapeDtypeStruct(x_rows.shape, F32),
        input_output_aliases={3: 0},
        compiler_params=pltpu.CompilerParams(dimension_semantics=("arbitrary",),
                                             vmem_limit_bytes=EXPERT_VMEM_LIMIT),
        name="experts",
    )(block_e, n_active, next_e, x_rows, w_gate, b_gate.reshape(E, 1, FF), w_up,
      b_up.reshape(E, 1, FF), w_down, b_down.reshape(E, 1, D))


def _combine_kernel(dest_ref, dest_next_ref, y_hbm, x1_ref, gates_ref, fg_ref, o_ref, bufs, sems,
                    *, ts, n_steps):
    i = pl.program_id(0)
    slot = i % 2

    def gather_tile(d_ref, s):
        def issue(it, carry):
            for j in range(ISSUE_UNROLL):
                r = it * ISSUE_UNROLL + j
                for k in range(TOP_K):
                    d = d_ref[r * TOP_K + k]
                    pltpu.make_async_copy(_slab(y_hbm, d), _slab(bufs.at[s, k], r),
                                          sems.at[s]).start(priority=k % 2)
            return carry

        lax.fori_loop(0, ts // ISSUE_UNROLL, issue, 0)

    @pl.when(i == 0)
    def _():
        gather_tile(dest_ref, 0)

    @pl.when(i + 1 < n_steps)
    def _():
        gather_tile(dest_next_ref, 1 - slot)

    for k in range(TOP_K):
        pltpu.make_async_copy(y_hbm.at[pl.ds(0, ts * SUBLANES)], bufs.at[slot, k], sems.at[slot]).wait()

    acc = x1_ref[...]
    gates = gates_ref[...]
    for k in range(TOP_K):
        acc = acc + _slabs_to_rows(bufs.at[slot, k], ts) * gates[:, k:k + 1]
    ms = jnp.mean(acc * acc, axis=-1, keepdims=True)
    o_ref[...] = acc * lax.rsqrt(ms + NORM_EPS) * fg_ref[...]


def _combine(y_rows, dest_flat, x1, gates, final_g):
    T, D = x1.shape
    ts = min(TS_COMB, T)
    n_steps = T // ts
    tok = lambda i: (i, 0)
    return pl.pallas_call(
        functools.partial(_combine_kernel, ts=ts, n_steps=n_steps),
        grid=(n_steps,),
        in_specs=[
            pl.BlockSpec((ts * TOP_K,), lambda i: (i,), memory_space=pltpu.SMEM),
            pl.BlockSpec((ts * TOP_K,), lambda i: (jnp.minimum(i + 1, n_steps - 1),),
                         memory_space=pltpu.SMEM),
            pl.BlockSpec(memory_space=pl.ANY),
            pl.BlockSpec((ts, D), tok),
            pl.BlockSpec((ts, LANES), tok),
            pl.BlockSpec((1, D), lambda i: (0, 0)),
        ],
        out_specs=pl.BlockSpec((ts, D), tok),
        out_shape=jax.ShapeDtypeStruct((T, D), F32),
        scratch_shapes=[pltpu.VMEM((2, TOP_K, ts * SUBLANES, LANES), F32),
                        pltpu.SemaphoreType.DMA((2,))],
        compiler_params=_cparams(("arbitrary",)),
        name="combine",
    )(dest_flat, dest_flat, y_rows, x1, gates, final_g.reshape(1, D))


def kernel(x, norm1_g, w_in, q_norm_g, k_norm_g, conv_w, conv_b, lru_wa, lru_ba, lru_wi, lru_bi,
           lru_lam, attn_out_g, lru_out_g, w_out, norm2_g, w_router, b_router, w_gate, b_gate,
           w_up, b_up, w_down, b_down, final_g):
    B, S, D = x.shape
    T = B * S
    assert w_in.shape[0] == 1, "single-layer trunk: the final norm is fused into the layer's combine"
    x2 = x.reshape(T, D)
    for l in range(1):
        qt, k, vt, lru_x, lru_gate = _inproj(x2, norm1_g[l], w_in[l], q_norm_g[l], k_norm_g[l], S)
        attn = _attention(qt, k.reshape(B, S, -1), vt, B, S)
        lru = _lru(lru_x.reshape(B, S, -1), lru_gate.reshape(B, S, -1), conv_w[l], conv_b[l],
                   lru_wa[l], lru_ba[l], lru_wi[l], lru_bi[l], lru_lam[l], B, S)
        x1, xn3, route, gates, cnt = _outproj_router(
            attn.reshape(T, -1), lru.reshape(T, -1), x2, attn_out_g[l], lru_out_g[l], w_out[l],
            norm2_g[l], w_router[l], b_router[l])

        counts = cnt[0, :N_EXPERTS].astype(jnp.int32)
        padded = ((counts + ROW_BLOCK - 1) // ROW_BLOCK) * ROW_BLOCK
        pend = jnp.cumsum(padded)
        pstart = (pend - padded).astype(jnp.int32)
        n_rows = T * TOP_K + N_EXPERTS * ROW_BLOCK
        block_start = jnp.arange(n_rows // ROW_BLOCK, dtype=jnp.int32) * ROW_BLOCK
        block_e = jnp.sum((pend[None, :] <= block_start[:, None]).astype(jnp.int32), axis=1)
        block_e = jnp.minimum(block_e, N_EXPERTS - 1)
        n_active = (pend[-1:] // ROW_BLOCK).astype(jnp.int32)
        fill = jnp.logical_or(block_start + ROW_BLOCK == pend[block_e],
                              block_start >= pend[-1]).astype(jnp.int32)

        next_block = pend[block_e] // ROW_BLOCK
        next_e = jnp.where(next_block < n_active[0],
                           block_e[jnp.minimum(next_block, block_e.shape[0] - 1)], -1).astype(jnp.int32)

        dest_flat = _dest_rows(route, pstart)
        x_rows = _dispatch(xn3, fill, dest_flat, n_rows)
        y_rows = _experts(x_rows, block_e, n_active, next_e, w_gate[l], b_gate[l], w_up[l], b_up[l],
                          w_down[l], b_down[l])
        x2 = _combine(y_rows, dest_flat, x1, gates, final_g)
    return x2.reshape(B, S, D)
```

```python
import functools
import math

import jax
import jax.numpy as jnp
from jax import lax
from jax.experimental import pallas as pl
from jax.experimental.pallas import tpu as pltpu

F32 = jnp.float32
BF16 = jnp.bfloat16

GRID_W = 64
HEAD_DIM = 64
N_Q_HEADS = 8
N_KV_HEADS = 2
GQA_GROUP = N_Q_HEADS // N_KV_HEADS
ATTN_W = N_Q_HEADS * HEAD_DIM
KV_W = N_KV_HEADS * HEAD_DIM
LRU_BLOCKS = 8
LRU_C = 8.0
CONV_W = 4
CONV_PAD_L = 2
ROPE_THETA = 10000.0
ROPE_HALF = HEAD_DIM // 2
ROPE_M = ROPE_HALF // 2
N_EXPERTS = 32
TOP_K = 4
SWIGLU_ALPHA = 1.702
SWIGLU_LIMIT = 7.0
NORM_EPS = 1e-5
QK_EPS = 1e-6
LOG2_E = 1.4426950408889634
Q_SCALE = HEAD_DIM ** -0.5 * LOG2_E
SAFE_SCORE_LOG2 = 96.0

LANES = 128
SUBLANES = 8
BF16_SUBLANES = 16
PV_ROWS = LANES
VMEM_LIMIT = 48 * 1024 * 1024
EXPERT_VMEM_LIMIT = 56 * 1024 * 1024

TS_IN = 512
TQ = 256
TK = 1024
KV_UNROLL = 2
HEADS_PER_STEP = 2
TC_LRU = 512
TS_OUT = 512
ROW_BLOCK = 256
TS_DISP = 512
TS_COMB = 256
ISSUE_UNROLL = 8


def _cparams(sem):
    return pltpu.CompilerParams(dimension_semantics=sem, vmem_limit_bytes=VMEM_LIMIT)


def _inproj_kernel(x_ref, g1_ref, w_ref, qg_ref, kg_ref, cos_ref, sin_ref,
                   q_ref, k_ref, v_ref, lx_ref, lg_ref, *, lru_w):
    x = x_ref[...]
    ms = jnp.mean(x * x, axis=-1, keepdims=True)
    xn = x * lax.rsqrt(ms + NORM_EPS) * g1_ref[...]
    h = jnp.dot(xn.astype(BF16), w_ref[...], preferred_element_type=F32)

    cos = cos_ref[...]
    sin = sin_ref[...]
    lane = lax.broadcasted_iota(jnp.int32, cos.shape, 1)
    first_half = (lane % ROPE_HALF) < ROPE_M

    def head_norm_rope(xc, g, scale):
        hms = jnp.sum(xc * xc, axis=-1, keepdims=True) * (1.0 / HEAD_DIM)
        xc = xc * lax.rsqrt(hms + QK_EPS) * g
        partner = jnp.where(first_half,
                            pltpu.roll(xc, LANES - ROPE_M, 1),
                            pltpu.roll(xc, ROPE_M, 1))
        return (xc * cos + partner * sin) * scale

    qw = N_Q_HEADS * LANES
    kw = N_KV_HEADS * LANES
    for c in range(N_Q_HEADS):
        sl = slice(c * LANES, (c + 1) * LANES)
        q_ref[0, sl, :] = head_norm_rope(h[:, sl], qg_ref[...], Q_SCALE).T.astype(BF16)
    for c in range(N_KV_HEADS):
        sl = slice(c * LANES, (c + 1) * LANES)
        k_ref[:, sl] = head_norm_rope(h[:, qw + c * LANES: qw + (c + 1) * LANES],
                                      kg_ref[...], 1.0).astype(BF16)
        vc = h[:, qw + kw + c * LANES: qw + kw + (c + 1) * LANES]
        v_ref[0, sl, :] = jnp.where(lane >= HEAD_DIM, 1.0, vc).T.astype(BF16)
    o = qw + 2 * kw
    lx_ref[...] = h[:, o: o + lru_w]
    lg_ref[...] = h[:, o + lru_w: o + 2 * lru_w]


def _pad_heads(w, n_heads):
    lead = w.shape[:-1]
    w = w.reshape(lead + (n_heads, HEAD_DIM))
    w = jnp.pad(w, [(0, 0)] * len(lead) + [(0, 0), (0, LANES - HEAD_DIM)])
    return w.reshape(lead + (n_heads * LANES,))


def _rope_tables(S):
    t = jnp.arange(S)
    rows = (t // GRID_W).astype(F32)
    cols = (t % GRID_W).astype(F32)
    inv_freq = ROPE_THETA ** (-jnp.arange(ROPE_M, dtype=F32) / ROPE_M)
    ar = rows[:, None] * inv_freq[None, :]
    ac = cols[:, None] * inv_freq[None, :]
    cos = jnp.concatenate([jnp.cos(ar), jnp.cos(ar), jnp.cos(ac), jnp.cos(ac)], axis=-1)
    sin = jnp.concatenate([-jnp.sin(ar), jnp.sin(ar), -jnp.sin(ac), jnp.sin(ac)], axis=-1)
    pad = [(0, 0), (0, LANES - HEAD_DIM)]
    return jnp.pad(cos, pad), jnp.pad(sin, pad)


def _inproj(x2, norm1_g, w_in, q_norm_g, k_norm_g, S):
    T, D = x2.shape
    lru_w = (w_in.shape[1] - ATTN_W - 2 * KV_W) // 2
    o0, o1, o2 = ATTN_W, ATTN_W + KV_W, ATTN_W + 2 * KV_W
    w_all = jnp.concatenate([
        _pad_heads(w_in[:, :o0], N_Q_HEADS),
        _pad_heads(w_in[:, o0:o1], N_KV_HEADS),
        _pad_heads(w_in[:, o1:o2], N_KV_HEADS),
        w_in[:, o2:],
    ], axis=1).astype(BF16)
    qg = _pad_heads(q_norm_g.reshape(1, HEAD_DIM), 1)
    kg = _pad_heads(k_norm_g.reshape(1, HEAD_DIM), 1)
    cos, sin = _rope_tables(S)
    ts = TS_IN
    n_s = S // ts
    qw, kw = N_Q_HEADS * LANES, N_KV_HEADS * LANES
    const = lambda i: (0, 0)
    tok = lambda i: (i, 0)
    pos = lambda i: (i % n_s, 0)
    tposed = lambda i: (i // n_s, 0, i % n_s)
    return pl.pallas_call(
        functools.partial(_inproj_kernel, lru_w=lru_w),
        grid=(T // ts,),
        in_specs=[
            pl.BlockSpec((ts, D), tok),
            pl.BlockSpec((1, D), const),
            pl.BlockSpec(w_all.shape, const),
            pl.BlockSpec((1, LANES), const),
            pl.BlockSpec((1, LANES), const),
            pl.BlockSpec((ts, LANES), pos),
            pl.BlockSpec((ts, LANES), pos),
        ],
        out_specs=[
            pl.BlockSpec((1, qw, ts), tposed),
            pl.BlockSpec((ts, kw), tok),
            pl.BlockSpec((1, kw, ts), tposed),
            pl.BlockSpec((ts, lru_w), tok),
            pl.BlockSpec((ts, lru_w), tok),
        ],
        out_shape=[
            jax.ShapeDtypeStruct((T // S, qw, S), BF16),
            jax.ShapeDtypeStruct((T, kw), BF16),
            jax.ShapeDtypeStruct((T // S, kw, S), BF16),
            jax.ShapeDtypeStruct((T, lru_w), F32),
            jax.ShapeDtypeStruct((T, lru_w), F32),
        ],
        compiler_params=_cparams(("parallel",)),
        name="inproj",
    )(x2, norm1_g.reshape(1, D), w_all, qg, kg, cos, sin)


def _attn_kernel(qt_ref, k_ref, vt_ref, o_ref, acc_ref, s_ref, p_ref, *, tq, tk, n_kv, kv_unroll):
    hp = HEADS_PER_STEP
    spt = GQA_GROUP // hp
    acc_ref[...] = jnp.zeros(acc_ref.shape, F32)

    def scores(j, sp):
        kt = k_ref[0, pl.ds(pl.multiple_of(j * tk, tk), tk), :]
        out = []
        for u in range(hp):
            g = sp * hp + u
            s = jnp.dot(kt, qt_ref[0, g * LANES:(g + 1) * LANES, :], preferred_element_type=F32)
            out.append((s, jnp.max(s, axis=0, keepdims=True)))
        return out

    def softmax_stage(sc, ms, sp):
        out = []
        for u, (s, s_max) in enumerate(sc):
            h = sp * hp + u
            m_new = jnp.maximum(ms[h], s_max)
            out.append((jnp.exp2(ms[h] - m_new), jnp.exp2(s - m_new).astype(BF16)))
            ms[h] = m_new
        return out

    def pv_stage(j, sp, ap):
        vt = vt_ref[0, 0:PV_ROWS, pl.ds(pl.multiple_of(j * tk, tk), tk)]
        for u, (alpha, p) in enumerate(ap):
            g = sp * hp + u
            acc_ref[g] = alpha * acc_ref[g] + jnp.dot(vt, p, preferred_element_type=F32)

    ms = [jnp.full((1, tq), -jnp.inf, F32)] * GQA_GROUP
    ap = softmax_stage(scores(0, 0), ms, 0)
    sc = scores(min(1 // spt, n_kv - 1), 1 % spt)
    for u in range(hp):
        s_ref[u] = sc[u][0]
        p_ref[u] = ap[u][1]

    def body(it, carry):
        ms = list(carry[:GQA_GROUP])
        ap = [(carry[GQA_GROUP + u], p_ref[u]) for u in range(hp)]
        sc = [(s_ref[u], carry[GQA_GROUP + hp + u]) for u in range(hp)]
        for n in range(kv_unroll * spt):
            j = it * kv_unroll + n // spt
            j_next = jnp.minimum(it * kv_unroll + (n + 2) // spt, n_kv - 1)
            sc_next = scores(j_next, (n + 2) % spt)
            ap_next = softmax_stage(sc, ms, (n + 1) % spt)
            pv_stage(j, n % spt, ap)
            sc, ap = sc_next, ap_next
        for u in range(hp):
            s_ref[u] = sc[u][0]
            p_ref[u] = ap[u][1]
        return tuple(ms) + tuple(a for a, _ in ap) + tuple(m for _, m in sc)

    lax.fori_loop(0, n_kv // kv_unroll, body,
                  tuple(ms) + tuple(a for a, _ in ap) + tuple(m for _, m in sc))
    _attn_finalize(acc_ref, o_ref, tq)


def _attn_finalize(acc_ref, o_ref, tq):
    pad = jnp.zeros((LANES - HEAD_DIM, tq), F32)
    for g in range(GQA_GROUP):
        acc = acc_ref[g]
        o = acc[0:HEAD_DIM] / acc[HEAD_DIM:HEAD_DIM + 1, :]
        o_ref[0, :, g * LANES:(g + 1) * LANES] = jnp.concatenate([o, pad], axis=0).T.astype(BF16)


def _attn_bounded_kernel(qt_ref, k_ref, vt_ref, o_ref, acc_ref, s_ref, p_ref, *, tq, tk, n_kv, kv_unroll):
    hp = HEADS_PER_STEP
    spt = GQA_GROUP // hp
    acc_ref[...] = jnp.zeros(acc_ref.shape, F32)

    def scores(j, sp):
        kt = k_ref[0, pl.ds(pl.multiple_of(j * tk, tk), tk), :]
        return [jnp.dot(kt, qt_ref[0, (sp * hp + u) * LANES:(sp * hp + u + 1) * LANES, :],
                        preferred_element_type=F32) for u in range(hp)]

    def probs(sc):
        return [jnp.exp2(s).astype(BF16) for s in sc]

    def pv_stage(j, sp, ps):
        vt = vt_ref[0, 0:PV_ROWS, pl.ds(pl.multiple_of(j * tk, tk), tk)]
        for u, p in enumerate(ps):
            acc_ref[sp * hp + u] += jnp.dot(vt, p, preferred_element_type=F32)

    ps = probs(scores(0, 0))
    sc = scores(min(1 // spt, n_kv - 1), 1 % spt)
    for u in range(hp):
        s_ref[u] = sc[u]
        p_ref[u] = ps[u]

    def body(it, carry):
        ps = [p_ref[u] for u in range(hp)]
        sc = [s_ref[u] for u in range(hp)]
        for n in range(kv_unroll * spt):
            j = it * kv_unroll + n // spt
            j_next = jnp.minimum(it * kv_unroll + (n + 2) // spt, n_kv - 1)
            sc_next = scores(j_next, (n + 2) % spt)
            ps_next = probs(sc)
            pv_stage(j, n % spt, ps)
            sc, ps = sc_next, ps_next
        for u in range(hp):
            s_ref[u] = sc[u]
            p_ref[u] = ps[u]
        return carry

    lax.fori_loop(0, n_kv // kv_unroll, body, 0)
    _attn_finalize(acc_ref, o_ref, tq)


def _attention(qt, k, vt, score_bound, B, S):
    tq = min(TQ, S)
    tk = min(TK, S)
    gw = GQA_GROUP * LANES

    def call(body, name):
        return pl.pallas_call(
            functools.partial(body, tq=tq, tk=tk, n_kv=S // tk,
                              kv_unroll=math.gcd(S // tk, KV_UNROLL)),
            grid=(B, N_KV_HEADS, S // tq),
            in_specs=[
                pl.BlockSpec((1, gw, tq), lambda b, h, i: (b, h, i)),
                pl.BlockSpec((1, S, LANES), lambda b, h, i: (b, 0, h)),
                pl.BlockSpec((1, LANES, S), lambda b, h, i: (b, h, 0)),
            ],
            out_specs=pl.BlockSpec((1, tq, gw), lambda b, h, i: (b, i, h)),
            out_shape=jax.ShapeDtypeStruct((B, S, N_Q_HEADS * LANES), BF16),
            scratch_shapes=[pltpu.VMEM((GQA_GROUP, PV_ROWS, tq), F32),
                            pltpu.VMEM((HEADS_PER_STEP, tk, tq), F32),
                            pltpu.VMEM((HEADS_PER_STEP, tk, tq), BF16)],
            compiler_params=_cparams(("parallel", "parallel", "parallel")),
            name=name,
        )

    return lax.cond(score_bound <= SAFE_SCORE_LOG2,
                    call(_attn_bounded_kernel, "attention_bounded"),
                    call(_attn_kernel, "attention"), qt, k, vt)


def _scan_chunk(a, b, reverse):
    n = a.shape[0]
    row = lax.broadcasted_iota(jnp.int32, a.shape, 0)
    d = 1
    while d < n:
        if reverse:
            keep = row < n - d
            shift = n - d
        else:
            keep = row >= d
            shift = d
        a_sh = jnp.where(keep, pltpu.roll(a, shift, 0), 1.0)
        b_sh = jnp.where(keep, pltpu.roll(b, shift, 0), 0.0)
        b = a * b_sh + b
        a = a * a_sh
        d *= 2
    return a, b


def _lru_kernel(u_ref, gate_ref, cw_ref, cb_ref, w_ref, bias_ref, lam_ref, o_ref,
                up_ref, hf_ref, *, S, tc):
    halo = SUBLANES
    zeros = jnp.zeros((halo, LANES), F32)
    up_ref[0:halo, :] = zeros
    up_ref[S + halo:S + 2 * halo, :] = zeros
    up_ref[halo:S + halo, :] = u_ref[0]
    sp = jax.nn.softplus(-lam_ref[...])
    cw = cw_ref[...]
    cb = cb_ref[...]
    n_chunks = S // tc
    ext = tc + 2 * halo

    def gates(c, d):
        t0 = pl.multiple_of(c * tc, tc)
        ue = up_ref[pl.ds(t0, ext), :]
        xc = cb
        for j in range(CONV_W):
            sh = (CONV_PAD_L - j) % ext
            uj = ue if sh == 0 else pltpu.roll(ue, sh, 0)
            xc = xc + uj[halo:halo + tc] * cw[j:j + 1, :]
        gw = 2 * LANES
        g = jnp.dot(xc.astype(BF16), w_ref[0, :, d * gw:(d + 1) * gw],
                    preferred_element_type=F32) + bias_ref[0, :, d * gw:(d + 1) * gw]
        r = jax.nn.sigmoid(g[:, :LANES])
        i = jax.nn.sigmoid(g[:, LANES:])
        log_a = -LRU_C * r * sp[d:d + 1, :]
        a = jnp.exp(log_a)
        b = jnp.sqrt(1.0 - jnp.exp(2.0 * log_a)) * i * xc
        return t0, a, b

    def fwd(c, h):
        t0, a, b = gates(c, 0)
        pa, hb = _scan_chunk(a, b, False)
        hc = hb + pa * h
        hf_ref[pl.ds(t0, tc), :] = hc
        return hc[tc - 1:tc, :]

    lax.fori_loop(0, n_chunks, fwd, jnp.zeros((1, LANES), F32))

    def bwd(ci, h):
        t0, a, b = gates(n_chunks - 1 - ci, 1)
        pa, hb = _scan_chunk(a, b, True)
        hc = hb + pa * h
        gate = gate_ref[0, pl.ds(t0, tc), :]
        o_ref[0, pl.ds(t0, tc), :] = (hf_ref[pl.ds(t0, tc), :] + hc) * jax.nn.gelu(gate)
        return hc[0:1, :]

    lax.fori_loop(0, n_chunks, bwd, jnp.zeros((1, LANES), F32))


def _block_diag_pairs(w):
    nb, bw, _ = w.shape
    w = w.reshape(nb // 2, 2, bw, bw)
    z = jnp.zeros_like(w[:, 0])
    top = jnp.concatenate([w[:, 0], z], axis=-1)
    bot = jnp.concatenate([z, w[:, 1]], axis=-1)
    return jnp.concatenate([top, bot], axis=-2)


def _lru(lru_x, lru_gate, conv_w, conv_b, wa, ba, wi, bi, lam, B, S):
    C = lru_x.shape[-1]
    nc = C // LANES
    tc = min(TC_LRU, S)
    w = jnp.concatenate([_block_diag_pairs(wa[0]), _block_diag_pairs(wi[0]),
                         _block_diag_pairs(wa[1]), _block_diag_pairs(wi[1])], axis=-1).astype(BF16)
    bias = jnp.stack([ba[0].reshape(nc, LANES), bi[0].reshape(nc, LANES),
                      ba[1].reshape(nc, LANES), bi[1].reshape(nc, LANES)], axis=1)
    bias = bias.reshape(nc, 1, 4 * LANES)
    blk = lambda b, c: (b, 0, c)
    return pl.pallas_call(
        functools.partial(_lru_kernel, S=S, tc=tc),
        grid=(B, nc),
        in_specs=[
            pl.BlockSpec((1, S, LANES), blk),
            pl.BlockSpec((1, S, LANES), blk),
            pl.BlockSpec((CONV_W, LANES), lambda b, c: (0, c)),
            pl.BlockSpec((1, LANES), lambda b, c: (0, c)),
            pl.BlockSpec((1, LANES, 4 * LANES), lambda b, c: (c, 0, 0)),
            pl.BlockSpec((1, 1, 4 * LANES), lambda b, c: (c, 0, 0)),
            pl.BlockSpec((2, LANES), lambda b, c: (0, c)),
        ],
        out_specs=pl.BlockSpec((1, S, LANES), blk),
        out_shape=jax.ShapeDtypeStruct((B, S, C), F32),
        scratch_shapes=[
            pltpu.VMEM((S + 2 * SUBLANES, LANES), F32),
            pltpu.VMEM((S, LANES), F32),
        ],
        compiler_params=_cparams(("parallel", "parallel")),
        name="rglru",
    )(lru_x, lru_gate, conv_w, conv_b.reshape(1, C), w, bias, lam)


def _rows_to_slabs(ref, x):
    n = x.shape[0]
    for s in range(SUBLANES):
        ref[pl.ds(s, n, stride=SUBLANES), :] = x[:, s * LANES:(s + 1) * LANES]


def _slabs_to_rows(ref, n):
    return jnp.concatenate([ref[pl.ds(s, n, stride=SUBLANES), :] for s in range(SUBLANES)], axis=1)


def _slab(ref, r):
    return ref.at[pl.ds(pl.multiple_of(r * SUBLANES, SUBLANES), SUBLANES)]


def _outproj_kernel(a_ref, l_ref, x_ref, ag_ref, lg_ref, wa_ref, wl_ref, g2_ref,
                    wrh_ref, wrl_ref, br_ref, tri_ref,
                    x1_ref, xn3_ref, route_ref, gates_ref, cnt_ref, carry_ref, *, attn_w, lru_w):
    step = pl.program_id(0)

    @pl.when(step == 0)
    def _():
        carry_ref[...] = jnp.zeros_like(carry_ref)

    a = a_ref[...].astype(F32)
    ams = jnp.sum(a * a, axis=-1, keepdims=True) * (1.0 / attn_w)
    an = a * lax.rsqrt(ams + NORM_EPS) * ag_ref[...]
    l = l_ref[...]
    lms = jnp.sum(l * l, axis=-1, keepdims=True) * (1.0 / lru_w)
    ln = l * lax.rsqrt(lms + NORM_EPS) * lg_ref[...]
    mix = (jnp.dot(an.astype(BF16), wa_ref[...], preferred_element_type=F32)
           + jnp.dot(ln.astype(BF16), wl_ref[...], preferred_element_type=F32))
    x1 = x_ref[...] + mix
    x1_ref[...] = x1
    ms = jnp.mean(x1 * x1, axis=-1, keepdims=True)
    xn = x1 * lax.rsqrt(ms + NORM_EPS) * g2_ref[...]
    _rows_to_slabs(xn3_ref, xn)

    hi = xn.astype(BF16)
    lo = (xn - hi.astype(F32)).astype(BF16)
    logits = (jnp.dot(hi, wrh_ref[...], preferred_element_type=F32)
              + jnp.dot(lo, wrh_ref[...], preferred_element_type=F32)
              + jnp.dot(hi, wrl_ref[...], preferred_element_type=F32)) + br_ref[...]
    lane = lax.broadcasted_iota(jnp.int32, logits.shape, 1)
    neg = -jnp.inf
    work = jnp.where(lane < N_EXPERTS, logits, neg)
    sel = jnp.zeros(logits.shape, F32)
    idxs, vals = [], []
    for _ in range(TOP_K):
        m = jnp.max(work, axis=1, keepdims=True)
        idx = jnp.min(jnp.where(work == m, lane, LANES), axis=1, keepdims=True)
        hit = lane == idx
        work = jnp.where(hit, neg, work)
        sel = sel + hit.astype(F32)
        idxs.append(idx)
        vals.append(m)
    es = [jnp.exp(v - vals[0]) for v in vals]
    den = es[0] + es[1] + es[2] + es[3]

    prefix = jnp.dot(tri_ref[...], sel.astype(BF16), preferred_element_type=F32) + carry_ref[...]
    carry_ref[...] = carry_ref[...] + jnp.sum(sel, axis=0, keepdims=True)
    cnt_ref[...] = carry_ref[...]

    route = jnp.zeros(logits.shape, jnp.int32)
    gates = jnp.zeros(logits.shape, F32)
    for k in range(TOP_K):
        rank = jnp.sum(jnp.where(lane == idxs[k], prefix, 0.0), axis=1, keepdims=True).astype(jnp.int32)
        route = jnp.where(lane == k, idxs[k], route)
        route = jnp.where(lane == TOP_K + k, rank, route)
        gates = jnp.where(lane == k, es[k] / den, gates)
    route_ref[...] = route
    gates_ref[...] = gates


def _outproj_router(attn, lru, x2, attn_out_g, lru_out_g, w_out, norm2_g, w_router, b_router):
    T, D = x2.shape
    lru_w = lru.shape[-1]
    ts = min(TS_OUT, T)
    wa = w_out[:ATTN_W].reshape(N_Q_HEADS, HEAD_DIM, D)
    wa = jnp.pad(wa, ((0, 0), (0, LANES - HEAD_DIM), (0, 0))).reshape(N_Q_HEADS * LANES, D).astype(BF16)
    wl = w_out[ATTN_W:].astype(BF16)
    ag = _pad_heads(attn_out_g.reshape(1, ATTN_W), N_Q_HEADS)
    wr = jnp.pad(w_router, ((0, 0), (0, LANES - N_EXPERTS)))
    wrh = wr.astype(BF16)
    wrl = (wr - wrh.astype(F32)).astype(BF16)
    br = jnp.pad(b_router.reshape(1, N_EXPERTS), ((0, 0), (0, LANES - N_EXPERTS)))
    tri = (jnp.arange(ts)[:, None] > jnp.arange(ts)[None, :]).astype(BF16)
    const = lambda i: (0, 0)
    tok = lambda i: (i, 0)
    aw = N_Q_HEADS * LANES
    return pl.pallas_call(
        functools.partial(_outproj_kernel, attn_w=ATTN_W, lru_w=lru_w),
        grid=(T // ts,),
        in_specs=[
            pl.BlockSpec((ts, aw), tok),
            pl.BlockSpec((ts, lru_w), tok),
            pl.BlockSpec((ts, D), tok),
            pl.BlockSpec((1, aw), const),
            pl.BlockSpec((1, lru_w), const),
            pl.BlockSpec((aw, D), const),
            pl.BlockSpec((lru_w, D), const),
            pl.BlockSpec((1, D), const),
            pl.BlockSpec((D, LANES), const),
            pl.BlockSpec((D, LANES), const),
            pl.BlockSpec((1, LANES), const),
            pl.BlockSpec((ts, ts), const),
        ],
        out_specs=[
            pl.BlockSpec((ts, D), tok),
            pl.BlockSpec((ts * SUBLANES, LANES), tok),
            pl.BlockSpec((ts, LANES), tok),
            pl.BlockSpec((ts, LANES), tok),
            pl.BlockSpec((1, LANES), const),
        ],
        out_shape=[
            jax.ShapeDtypeStruct((T, D), F32),
            jax.ShapeDtypeStruct((T * SUBLANES, LANES), F32),
            jax.ShapeDtypeStruct((T, LANES), jnp.int32),
            jax.ShapeDtypeStruct((T, LANES), F32),
            jax.ShapeDtypeStruct((1, LANES), F32),
        ],
        scratch_shapes=[pltpu.VMEM((1, LANES), F32)],
        compiler_params=_cparams(("arbitrary",)),
        name="outproj_router",
    )(attn, lru, x2, ag, lru_out_g.reshape(1, lru_w), wa, wl, norm2_g.reshape(1, D),
      wrh, wrl, br, tri)


def _dest_kernel(route_ref, pstart_ref, dest_ref):
    route = route_ref[...]
    lane = lax.broadcasted_iota(jnp.int32, route.shape, 1)
    pstart = pstart_ref[...]
    dest = jnp.zeros(route.shape, jnp.int32)
    for k in range(TOP_K):
        start = jnp.sum(jnp.where(lane == route[:, k:k + 1], pstart, 0.0), axis=1, keepdims=True)
        dest = jnp.where(lane == k, start.astype(jnp.int32) + route[:, TOP_K + k:TOP_K + k + 1], dest)
    dest_ref[...] = dest


def _dest_rows(route, pstart):
    T = route.shape[0]
    ts = min(TS_OUT, T)
    row = jnp.pad(pstart.astype(F32).reshape(1, N_EXPERTS), ((0, 0), (0, LANES - N_EXPERTS)))
    dest = pl.pallas_call(
        _dest_kernel,
        grid=(T // ts,),
        in_specs=[pl.BlockSpec((ts, LANES), lambda i: (i, 0)),
                  pl.BlockSpec((1, LANES), lambda i: (0, 0))],
        out_specs=pl.BlockSpec((ts, LANES), lambda i: (i, 0)),
        out_shape=jax.ShapeDtypeStruct((T, LANES), jnp.int32),
        compiler_params=_cparams(("parallel",)),
        name="dest_rows",
    )(route, row)
    return dest[:, :TOP_K].reshape(T * TOP_K)


def _dispatch_kernel(fill_ref, dest_ref, x_ref, out_hbm, zero_ref, sem, zero_sem, *, ts, n_blocks):
    block_slabs = ROW_BLOCK * SUBLANES

    def fill_copy(b):
        off = pl.multiple_of(b * block_slabs, block_slabs)
        return pltpu.make_async_copy(zero_ref, out_hbm.at[pl.ds(off, block_slabs)], zero_sem)

    @pl.when(pl.program_id(0) == 0)
    def _():
        zero_ref[...] = jnp.zeros(zero_ref.shape, F32)

        def start(b, carry):
            @pl.when(fill_ref[b] != 0)
            def _():
                fill_copy(b).start()
            return carry

        def wait(b, carry):
            @pl.when(fill_ref[b] != 0)
            def _():
                fill_copy(b).wait()
            return carry

        lax.fori_loop(0, n_blocks, start, 0)
        lax.fori_loop(0, n_blocks, wait, 0)

    def issue(i, carry):
        for j in range(ISSUE_UNROLL):
            r = i * ISSUE_UNROLL + j
            for k in range(TOP_K):
                d = dest_ref[r * TOP_K + k]
                pltpu.make_async_copy(_slab(x_ref, r), _slab(out_hbm, d), sem).start(priority=k % 2)
        return carry

    lax.fori_loop(0, ts // ISSUE_UNROLL, issue, 0)
    for k in range(TOP_K):
        pltpu.make_async_copy(x_ref, out_hbm.at[pl.ds(0, ts * SUBLANES)], sem).wait()


def _dispatch(xn_slabs, fill, dest_flat, n_rows):
    T = xn_slabs.shape[0] // SUBLANES
    ts = min(TS_DISP, T)
    grid_spec = pltpu.PrefetchScalarGridSpec(
        num_scalar_prefetch=1,
        grid=(T // ts,),
        in_specs=[
            pl.BlockSpec((ts * TOP_K,), lambda i, fl: (i,), memory_space=pltpu.SMEM),
            pl.BlockSpec((ts * SUBLANES, LANES), lambda i, fl: (i, 0)),
        ],
        out_specs=pl.BlockSpec(memory_space=pl.ANY),
        scratch_shapes=[pltpu.VMEM((ROW_BLOCK * SUBLANES, LANES), F32),
                        pltpu.SemaphoreType.DMA, pltpu.SemaphoreType.DMA],
    )
    return pl.pallas_call(
        functools.partial(_dispatch_kernel, ts=ts, n_blocks=n_rows // ROW_BLOCK),
        grid_spec=grid_spec,
        out_shape=jax.ShapeDtypeStruct((n_rows * SUBLANES, LANES), xn_slabs.dtype),
        compiler_params=_cparams(("arbitrary",)),
        name="dispatch",
    )(fill, dest_flat, xn_slabs)


def _expert_kernel(be_ref, na_ref, nxt_ref, x_ref, wg_hbm, bg_ref, wu_hbm, bu_ref, wd_hbm, bd_ref,
                   y_ref, stage_ref, wb_ref, slot_ref, sems):
    i = pl.program_id(0)
    e = be_ref[i]
    w_hbm = (wg_hbm, wu_hbm, wd_hbm)

    def fetch(expert, slot, m):
        return pltpu.make_async_copy(w_hbm[m].at[expert], stage_ref.at[slot, m], sems.at[slot, m])

    @pl.when(i == 0)
    def _():
        slot_ref[0] = 0
        for m in range(3):
            fetch(e, 0, m).start()

    active = i < na_ref[0]
    first = jnp.logical_or(i == 0, e != be_ref[jnp.maximum(i - 1, 0)])

    @pl.when(jnp.logical_and(active, first))
    def _():
        slot = slot_ref[0]
        for m in range(3):
            fetch(e, slot, m).wait()
            wb_ref[m] = stage_ref[slot, m].astype(BF16)

        @pl.when(nxt_ref[i] >= 0)
        def _():
            for m in range(3):
                fetch(nxt_ref[i], 1 - slot, m).start()

        slot_ref[0] = 1 - slot

    @pl.when(active)
    def _():
        x = _slabs_to_rows(x_ref, ROW_BLOCK).astype(BF16)
        g = jnp.dot(x, wb_ref[0], preferred_element_type=F32) + bg_ref[0]
        u = jnp.dot(x, wb_ref[1], preferred_element_type=F32) + bu_ref[0]
        g = jnp.minimum(g, SWIGLU_LIMIT)
        u = jnp.clip(u, -SWIGLU_LIMIT, SWIGLU_LIMIT)
        glu = g * jax.nn.sigmoid(SWIGLU_ALPHA * g)
        y = jnp.dot(((u + 1.0) * glu).astype(BF16), wb_ref[2], preferred_element_type=F32) + bd_ref[0]
        _rows_to_slabs(y_ref, y)


def _experts(x_rows, block_e, n_active, next_e, w_gate, b_gate, w_up, b_up, w_down, b_down):
    E, D, FF = w_gate.shape
    assert D == FF, "the three expert matrices share one staging shape"
    block_slabs = ROW_BLOCK * SUBLANES
    n_blocks = x_rows.shape[0] // block_slabs

    def row_map(i, be, na, nx):
        return (jnp.minimum(i, na[0] - 1), 0)

    def b_map(i, be, na, nx):
        return (be[jnp.minimum(i, na[0] - 1)], 0, 0)

    grid_spec = pltpu.PrefetchScalarGridSpec(
        num_scalar_prefetch=3,
        grid=(n_blocks,),
        in_specs=[
            pl.BlockSpec((block_slabs, LANES), row_map),
            pl.BlockSpec(memory_space=pl.ANY),
            pl.BlockSpec((1, 1, FF), b_map),
            pl.BlockSpec(memory_space=pl.ANY),
            pl.BlockSpec((1, 1, FF), b_map),
            pl.BlockSpec(memory_space=pl.ANY),
            pl.BlockSpec((1, 1, D), b_map),
        ],
        out_specs=pl.BlockSpec((block_slabs, LANES), row_map),
        scratch_shapes=[
            pltpu.VMEM((2, 3, D, FF), F32),
            pltpu.VMEM((3, D, FF), BF16),
            pltpu.SMEM((1,), jnp.int32),
            pltpu.SemaphoreType.DMA((2, 3)),
        ],
    )
    return pl.pallas_call(
        _expert_kernel,
        grid_spec=grid_spec,
        out_shape=jax.ShapeDtypeStruct(x_rows.shape, F32),
        input_output_aliases={3: 0},
        compiler_params=pltpu.CompilerParams(dimension_semantics=("arbitrary",),
                                             vmem_limit_bytes=EXPERT_VMEM_LIMIT),
        name="experts",
    )(block_e, n_active, next_e, x_rows, w_gate, b_gate.reshape(E, 1, FF), w_up,
      b_up.reshape(E, 1, FF), w_down, b_down.reshape(E, 1, D))


def _combine_kernel(dest_ref, dest_next_ref, y_hbm, x1_ref, gates_ref, fg_ref, o_ref, bufs, sems,
                    *, ts, n_steps):
    i = pl.program_id(0)
    slot = i % 2

    def gather_tile(d_ref, s):
        def issue(it, carry):
            for j in range(ISSUE_UNROLL):
                r = it * ISSUE_UNROLL + j
                for k in range(TOP_K):
                    d = d_ref[r * TOP_K + k]
                    pltpu.make_async_copy(_slab(y_hbm, d), _slab(bufs.at[s, k], r),
                                          sems.at[s]).start(priority=k % 2)
            return carry

        lax.fori_loop(0, ts // ISSUE_UNROLL, issue, 0)

    @pl.when(i == 0)
    def _():
        gather_tile(dest_ref, 0)

    @pl.when(i + 1 < n_steps)
    def _():
        gather_tile(dest_next_ref, 1 - slot)

    for k in range(TOP_K):
        pltpu.make_async_copy(y_hbm.at[pl.ds(0, ts * SUBLANES)], bufs.at[slot, k], sems.at[slot]).wait()

    acc = x1_ref[...]
    gates = gates_ref[...]
    for k in range(TOP_K):
        acc = acc + _slabs_to_rows(bufs.at[slot, k], ts) * gates[:, k:k + 1]
    ms = jnp.mean(acc * acc, axis=-1, keepdims=True)
    o_ref[...] = acc * lax.rsqrt(ms + NORM_EPS) * fg_ref[...]


def _combine(y_rows, dest_flat, x1, gates, final_g):
    T, D = x1.shape
    ts = min(TS_COMB, T)
    n_steps = T // ts
    tok = lambda i: (i, 0)
    return pl.pallas_call(
        functools.partial(_combine_kernel, ts=ts, n_steps=n_steps),
        grid=(n_steps,),
        in_specs=[
            pl.BlockSpec((ts * TOP_K,), lambda i: (i,), memory_space=pltpu.SMEM),
            pl.BlockSpec((ts * TOP_K,), lambda i: (jnp.minimum(i + 1, n_steps - 1),),
                         memory_space=pltpu.SMEM),
            pl.BlockSpec(memory_space=pl.ANY),
            pl.BlockSpec((ts, D), tok),
            pl.BlockSpec((ts, LANES), tok),
            pl.BlockSpec((1, D), lambda i: (0, 0)),
        ],
        out_specs=pl.BlockSpec((ts, D), tok),
        out_shape=jax.ShapeDtypeStruct((T, D), F32),
        scratch_shapes=[pltpu.VMEM((2, TOP_K, ts * SUBLANES, LANES), F32),
                        pltpu.SemaphoreType.DMA((2,))],
        compiler_params=_cparams(("arbitrary",)),
        name="combine",
    )(dest_flat, dest_flat, y_rows, x1, gates, final_g.reshape(1, D))


def kernel(x, norm1_g, w_in, q_norm_g, k_norm_g, conv_w, conv_b, lru_wa, lru_ba, lru_wi, lru_bi,
           lru_lam, attn_out_g, lru_out_g, w_out, norm2_g, w_router, b_router, w_gate, b_gate,
           w_up, b_up, w_down, b_down, final_g):
    B, S, D = x.shape
    T = B * S
    assert w_in.shape[0] == 1, "single-layer trunk: the final norm is fused into the layer's combine"
    x2 = x.reshape(T, D)
    for l in range(1):
        qt, k, vt, lru_x, lru_gate = _inproj(x2, norm1_g[l], w_in[l], q_norm_g[l], k_norm_g[l], S)
        score_bound = (HEAD_DIM * Q_SCALE * jnp.max(jnp.abs(q_norm_g[l]))
                       * jnp.max(jnp.abs(k_norm_g[l])))
        attn = _attention(qt, k.reshape(B, S, -1), vt, score_bound, B, S)
        lru = _lru(lru_x.reshape(B, S, -1), lru_gate.reshape(B, S, -1), conv_w[l], conv_b[l],
                   lru_wa[l], lru_ba[l], lru_wi[l], lru_bi[l], lru_lam[l], B, S)
        x1, xn3, route, gates, cnt = _outproj_router(
            attn.reshape(T, -1), lru.reshape(T, -1), x2, attn_out_g[l], lru_out_g[l], w_out[l],
            norm2_g[l], w_router[l], b_router[l])

        counts = cnt[0, :N_EXPERTS].astype(jnp.int32)
        padded = ((counts + ROW_BLOCK - 1) // ROW_BLOCK) * ROW_BLOCK
        pend = jnp.cumsum(padded)
        pstart = (pend - padded).astype(jnp.int32)
        n_rows = T * TOP_K + N_EXPERTS * ROW_BLOCK
        block_start = jnp.arange(n_rows // ROW_BLOCK, dtype=jnp.int32) * ROW_BLOCK
        block_e = jnp.sum((pend[None, :] <= block_start[:, None]).astype(jnp.int32), axis=1)
        block_e = jnp.minimum(block_e, N_EXPERTS - 1)
        n_active = (pend[-1:] // ROW_BLOCK).astype(jnp.int32)
        fill = jnp.logical_or(block_start + ROW_BLOCK == pend[block_e],
                              block_start >= pend[-1]).astype(jnp.int32)

        next_block = pend[block_e] // ROW_BLOCK
        next_e = jnp.where(next_block < n_active[0],
                           block_e[jnp.minimum(next_block, block_e.shape[0] - 1)], -1).astype(jnp.int32)

        dest_flat = _dest_rows(route, pstart)
        x_rows = _dispatch(xn3, fill, dest_flat, n_rows)
        y_rows = _experts(x_rows, block_e, n_active, next_e, w_gate[l], b_gate[l], w_up[l], b_up[l],
                          w_down[l], b_down[l])
        x2 = _combine(y_rows, dest_flat, x1, gates, final_g)
    return x2.reshape(B, S, D)
```

```python
import functools
import math

import jax
import jax.numpy as jnp
from jax import lax
from jax.experimental import pallas as pl
from jax.experimental.pallas import tpu as pltpu

F32 = jnp.float32
BF16 = jnp.bfloat16

GRID_W = 64
HEAD_DIM = 64
N_Q_HEADS = 8
N_KV_HEADS = 2
GQA_GROUP = N_Q_HEADS // N_KV_HEADS
ATTN_W = N_Q_HEADS * HEAD_DIM
KV_W = N_KV_HEADS * HEAD_DIM
LRU_BLOCKS = 8
LRU_C = 8.0
CONV_W = 4
CONV_PAD_L = 2
ROPE_THETA = 10000.0
ROPE_HALF = HEAD_DIM // 2
ROPE_M = ROPE_HALF // 2
N_EXPERTS = 32
TOP_K = 4
SWIGLU_ALPHA = 1.702
SWIGLU_LIMIT = 7.0
NORM_EPS = 1e-5
QK_EPS = 1e-6
LOG2_E = 1.4426950408889634
Q_SCALE = HEAD_DIM ** -0.5 * LOG2_E
SAFE_SCORE_LOG2 = 96.0

LANES = 128
SUBLANES = 8
BF16_SUBLANES = 16
PV_ROWS = LANES
VMEM_LIMIT = 48 * 1024 * 1024
EXPERT_VMEM_LIMIT = 56 * 1024 * 1024

TS_IN = 512
TQ = 256
TK = 256
KV_UNROLL = 8
HEADS_PER_STEP = 2
TC_LRU = 512
TS_OUT = 512
ROW_BLOCK = 256
TS_DISP = 512
TS_COMB = 256
ISSUE_UNROLL = 8


def _cparams(sem):
    return pltpu.CompilerParams(dimension_semantics=sem, vmem_limit_bytes=VMEM_LIMIT)


def _inproj_kernel(x_ref, g1_ref, w_ref, qg_ref, kg_ref, cos_ref, sin_ref,
                   q_ref, k_ref, v_ref, lx_ref, lg_ref, *, lru_w):
    x = x_ref[...]
    ms = jnp.mean(x * x, axis=-1, keepdims=True)
    xn = x * lax.rsqrt(ms + NORM_EPS) * g1_ref[...]
    h = jnp.dot(xn.astype(BF16), w_ref[...], preferred_element_type=F32)

    cos = cos_ref[...]
    sin = sin_ref[...]
    lane = lax.broadcasted_iota(jnp.int32, cos.shape, 1)
    first_half = (lane % ROPE_HALF) < ROPE_M

    def head_norm_rope(xc, g, scale):
        hms = jnp.sum(xc * xc, axis=-1, keepdims=True) * (1.0 / HEAD_DIM)
        xc = xc * lax.rsqrt(hms + QK_EPS) * g
        partner = jnp.where(first_half,
                            pltpu.roll(xc, LANES - ROPE_M, 1),
                            pltpu.roll(xc, ROPE_M, 1))
        return (xc * cos + partner * sin) * scale

    qw = N_Q_HEADS * LANES
    kw = N_KV_HEADS * LANES
    for c in range(N_Q_HEADS):
        sl = slice(c * LANES, (c + 1) * LANES)
        q_ref[0, sl, :] = head_norm_rope(h[:, sl], qg_ref[...], Q_SCALE).T.astype(BF16)
    for c in range(N_KV_HEADS):
        sl = slice(c * LANES, (c + 1) * LANES)
        k_ref[:, sl] = head_norm_rope(h[:, qw + c * LANES: qw + (c + 1) * LANES],
                                      kg_ref[...], 1.0).astype(BF16)
        vc = h[:, qw + kw + c * LANES: qw + kw + (c + 1) * LANES]
        v_ref[0, sl, :] = jnp.where(lane >= HEAD_DIM, 1.0, vc).T.astype(BF16)
    o = qw + 2 * kw
    lx_ref[...] = h[:, o: o + lru_w]
    lg_ref[...] = h[:, o + lru_w: o + 2 * lru_w]


def _pad_heads(w, n_heads):
    lead = w.shape[:-1]
    w = w.reshape(lead + (n_heads, HEAD_DIM))
    w = jnp.pad(w, [(0, 0)] * len(lead) + [(0, 0), (0, LANES - HEAD_DIM)])
    return w.reshape(lead + (n_heads * LANES,))


def _rope_tables(S):
    t = jnp.arange(S)
    rows = (t // GRID_W).astype(F32)
    cols = (t % GRID_W).astype(F32)
    inv_freq = ROPE_THETA ** (-jnp.arange(ROPE_M, dtype=F32) / ROPE_M)
    ar = rows[:, None] * inv_freq[None, :]
    ac = cols[:, None] * inv_freq[None, :]
    cos = jnp.concatenate([jnp.cos(ar), jnp.cos(ar), jnp.cos(ac), jnp.cos(ac)], axis=-1)
    sin = jnp.concatenate([-jnp.sin(ar), jnp.sin(ar), -jnp.sin(ac), jnp.sin(ac)], axis=-1)
    pad = [(0, 0), (0, LANES - HEAD_DIM)]
    return jnp.pad(cos, pad), jnp.pad(sin, pad)


def _inproj(x2, norm1_g, w_in, q_norm_g, k_norm_g, S):
    T, D = x2.shape
    lru_w = (w_in.shape[1] - ATTN_W - 2 * KV_W) // 2
    o0, o1, o2 = ATTN_W, ATTN_W + KV_W, ATTN_W + 2 * KV_W
    w_all = jnp.concatenate([
        _pad_heads(w_in[:, :o0], N_Q_HEADS),
        _pad_heads(w_in[:, o0:o1], N_KV_HEADS),
        _pad_heads(w_in[:, o1:o2], N_KV_HEADS),
        w_in[:, o2:],
    ], axis=1).astype(BF16)
    qg = _pad_heads(q_norm_g.reshape(1, HEAD_DIM), 1)
    kg = _pad_heads(k_norm_g.reshape(1, HEAD_DIM), 1)
    cos, sin = _rope_tables(S)
    ts = TS_IN
    n_s = S // ts
    qw, kw = N_Q_HEADS * LANES, N_KV_HEADS * LANES
    const = lambda i: (0, 0)
    tok = lambda i: (i, 0)
    pos = lambda i: (i % n_s, 0)
    tposed = lambda i: (i // n_s, 0, i % n_s)
    return pl.pallas_call(
        functools.partial(_inproj_kernel, lru_w=lru_w),
        grid=(T // ts,),
        in_specs=[
            pl.BlockSpec((ts, D), tok),
            pl.BlockSpec((1, D), const),
            pl.BlockSpec(w_all.shape, const),
            pl.BlockSpec((1, LANES), const),
            pl.BlockSpec((1, LANES), const),
            pl.BlockSpec((ts, LANES), pos),
            pl.BlockSpec((ts, LANES), pos),
        ],
        out_specs=[
            pl.BlockSpec((1, qw, ts), tposed),
            pl.BlockSpec((ts, kw), tok),
            pl.BlockSpec((1, kw, ts), tposed),
            pl.BlockSpec((ts, lru_w), tok),
            pl.BlockSpec((ts, lru_w), tok),
        ],
        out_shape=[
            jax.ShapeDtypeStruct((T // S, qw, S), BF16),
            jax.ShapeDtypeStruct((T, kw), BF16),
            jax.ShapeDtypeStruct((T // S, kw, S), BF16),
            jax.ShapeDtypeStruct((T, lru_w), F32),
            jax.ShapeDtypeStruct((T, lru_w), F32),
        ],
        compiler_params=_cparams(("parallel",)),
        name="inproj",
    )(x2, norm1_g.reshape(1, D), w_all, qg, kg, cos, sin)


def _attn_kernel(qt_ref, k_ref, vt_ref, o_ref, acc_ref, s_ref, p_ref, *, tq, tk, n_kv, kv_unroll):
    hp = HEADS_PER_STEP
    spt = GQA_GROUP // hp
    acc_ref[...] = jnp.zeros(acc_ref.shape, F32)

    def scores(j, sp):
        kt = k_ref[0, pl.ds(pl.multiple_of(j * tk, tk), tk), :]
        out = []
        for u in range(hp):
            g = sp * hp + u
            s = jnp.dot(kt, qt_ref[0, g * LANES:(g + 1) * LANES, :], preferred_element_type=F32)
            out.append((s, jnp.max(s, axis=0, keepdims=True)))
        return out

    def softmax_stage(sc, ms, sp):
        out = []
        for u, (s, s_max) in enumerate(sc):
            h = sp * hp + u
            m_new = jnp.maximum(ms[h], s_max)
            out.append((jnp.exp2(ms[h] - m_new), jnp.exp2(s - m_new).astype(BF16)))
            ms[h] = m_new
        return out

    def pv_stage(j, sp, ap):
        vt = vt_ref[0, 0:PV_ROWS, pl.ds(pl.multiple_of(j * tk, tk), tk)]
        for u, (alpha, p) in enumerate(ap):
            g = sp * hp + u
            acc_ref[g] = alpha * acc_ref[g] + jnp.dot(vt, p, preferred_element_type=F32)

    ms = [jnp.full((1, tq), -jnp.inf, F32)] * GQA_GROUP
    ap = softmax_stage(scores(0, 0), ms, 0)
    sc = scores(min(1 // spt, n_kv - 1), 1 % spt)
    for u in range(hp):
        s_ref[u] = sc[u][0]
        p_ref[u] = ap[u][1]

    def body(it, carry):
        ms = list(carry[:GQA_GROUP])
        ap = [(carry[GQA_GROUP + u], p_ref[u]) for u in range(hp)]
        sc = [(s_ref[u], carry[GQA_GROUP + hp + u]) for u in range(hp)]
        for n in range(kv_unroll * spt):
            j = it * kv_unroll + n // spt
            j_next = jnp.minimum(it * kv_unroll + (n + 2) // spt, n_kv - 1)
            sc_next = scores(j_next, (n + 2) % spt)
            ap_next = softmax_stage(sc, ms, (n + 1) % spt)
            pv_stage(j, n % spt, ap)
            sc, ap = sc_next, ap_next
        for u in range(hp):
            s_ref[u] = sc[u][0]
            p_ref[u] = ap[u][1]
        return tuple(ms) + tuple(a for a, _ in ap) + tuple(m for _, m in sc)

    lax.fori_loop(0, n_kv // kv_unroll, body,
                  tuple(ms) + tuple(a for a, _ in ap) + tuple(m for _, m in sc))
    _attn_finalize(acc_ref, o_ref, tq)


def _attn_finalize(acc_ref, o_ref, tq):
    pad = jnp.zeros((LANES - HEAD_DIM, tq), F32)
    for g in range(GQA_GROUP):
        acc = acc_ref[g]
        o = acc[0:HEAD_DIM] / acc[HEAD_DIM:HEAD_DIM + 1, :]
        o_ref[0, :, g * LANES:(g + 1) * LANES] = jnp.concatenate([o, pad], axis=0).T.astype(BF16)


def _attn_bounded_kernel(qt_ref, k_ref, vt_ref, o_ref, acc_ref, s_ref, p_ref, *, tq, tk, n_kv, kv_unroll):
    hp = HEADS_PER_STEP
    spt = GQA_GROUP // hp
    acc_ref[...] = jnp.zeros(acc_ref.shape, F32)

    def scores(j, sp):
        kt = k_ref[0, pl.ds(pl.multiple_of(j * tk, tk), tk), :]
        return [jnp.dot(kt, qt_ref[0, (sp * hp + u) * LANES:(sp * hp + u + 1) * LANES, :],
                        preferred_element_type=F32) for u in range(hp)]

    def probs(sc):
        return [jnp.exp2(s).astype(BF16) for s in sc]

    def pv_stage(j, sp, ps):
        vt = vt_ref[0, 0:PV_ROWS, pl.ds(pl.multiple_of(j * tk, tk), tk)]
        for u, p in enumerate(ps):
            acc_ref[sp * hp + u] += jnp.dot(vt, p, preferred_element_type=F32)

    ps = probs(scores(0, 0))
    sc = scores(min(1 // spt, n_kv - 1), 1 % spt)
    for u in range(hp):
        s_ref[u] = sc[u]
        p_ref[u] = ps[u]

    def body(it, carry):
        ps = [p_ref[u] for u in range(hp)]
        sc = [s_ref[u] for u in range(hp)]
        for n in range(kv_unroll * spt):
            j = it * kv_unroll + n // spt
            j_next = jnp.minimum(it * kv_unroll + (n + 2) // spt, n_kv - 1)
            sc_next = scores(j_next, (n + 2) % spt)
            ps_next = probs(sc)
            pv_stage(j, n % spt, ps)
            sc, ps = sc_next, ps_next
        for u in range(hp):
            s_ref[u] = sc[u]
            p_ref[u] = ps[u]
        return carry

    lax.fori_loop(0, n_kv // kv_unroll, body, 0)
    _attn_finalize(acc_ref, o_ref, tq)


def _attention(qt, k, vt, score_bound, B, S):
    tq = min(TQ, S)
    tk = min(TK, S)
    gw = GQA_GROUP * LANES

    def call(body, name):
        return pl.pallas_call(
            functools.partial(body, tq=tq, tk=tk, n_kv=S // tk,
                              kv_unroll=math.gcd(S // tk, KV_UNROLL)),
            grid=(B, N_KV_HEADS, S // tq),
            in_specs=[
                pl.BlockSpec((1, gw, tq), lambda b, h, i: (b, h, i)),
                pl.BlockSpec((1, S, LANES), lambda b, h, i: (b, 0, h)),
                pl.BlockSpec((1, LANES, S), lambda b, h, i: (b, h, 0)),
            ],
            out_specs=pl.BlockSpec((1, tq, gw), lambda b, h, i: (b, i, h)),
            out_shape=jax.ShapeDtypeStruct((B, S, N_Q_HEADS * LANES), BF16),
            scratch_shapes=[pltpu.VMEM((GQA_GROUP, PV_ROWS, tq), F32),
                            pltpu.VMEM((HEADS_PER_STEP, tk, tq), F32),
                            pltpu.VMEM((HEADS_PER_STEP, tk, tq), BF16)],
            compiler_params=_cparams(("parallel", "parallel", "parallel")),
            name=name,
        )

    return lax.cond(score_bound <= SAFE_SCORE_LOG2,
                    call(_attn_bounded_kernel, "attention_bounded"),
                    call(_attn_kernel, "attention"), qt, k, vt)


def _scan_chunk(a, b, reverse):
    n = a.shape[0]
    row = lax.broadcasted_iota(jnp.int32, a.shape, 0)
    d = 1
    while d < n:
        if reverse:
            keep = row < n - d
            shift = n - d
        else:
            keep = row >= d
            shift = d
        a_sh = jnp.where(keep, pltpu.roll(a, shift, 0), 1.0)
        b_sh = jnp.where(keep, pltpu.roll(b, shift, 0), 0.0)
        b = a * b_sh + b
        a = a * a_sh
        d *= 2
    return a, b


def _lru_kernel(u_ref, gate_ref, cw_ref, cb_ref, w_ref, bias_ref, lam_ref, o_ref,
                up_ref, hf_ref, *, S, tc):
    halo = SUBLANES
    zeros = jnp.zeros((halo, LANES), F32)
    up_ref[0:halo, :] = zeros
    up_ref[S + halo:S + 2 * halo, :] = zeros
    up_ref[halo:S + halo, :] = u_ref[0]
    sp = jax.nn.softplus(-lam_ref[...])
    cw = cw_ref[...]
    cb = cb_ref[...]
    n_chunks = S // tc
    ext = tc + 2 * halo

    def gates(c, d):
        t0 = pl.multiple_of(c * tc, tc)
        ue = up_ref[pl.ds(t0, ext), :]
        xc = cb
        for j in range(CONV_W):
            sh = (CONV_PAD_L - j) % ext
            uj = ue if sh == 0 else pltpu.roll(ue, sh, 0)
            xc = xc + uj[halo:halo + tc] * cw[j:j + 1, :]
        gw = 2 * LANES
        g = jnp.dot(xc.astype(BF16), w_ref[0, :, d * gw:(d + 1) * gw],
                    preferred_element_type=F32) + bias_ref[0, :, d * gw:(d + 1) * gw]
        r = jax.nn.sigmoid(g[:, :LANES])
        i = jax.nn.sigmoid(g[:, LANES:])
        log_a = -LRU_C * r * sp[d:d + 1, :]
        a = jnp.exp(log_a)
        b = jnp.sqrt(1.0 - jnp.exp(2.0 * log_a)) * i * xc
        return t0, a, b

    def fwd(c, h):
        t0, a, b = gates(c, 0)
        pa, hb = _scan_chunk(a, b, False)
        hc = hb + pa * h
        hf_ref[pl.ds(t0, tc), :] = hc
        return hc[tc - 1:tc, :]

    lax.fori_loop(0, n_chunks, fwd, jnp.zeros((1, LANES), F32))

    def bwd(ci, h):
        t0, a, b = gates(n_chunks - 1 - ci, 1)
        pa, hb = _scan_chunk(a, b, True)
        hc = hb + pa * h
        gate = gate_ref[0, pl.ds(t0, tc), :]
        o_ref[0, pl.ds(t0, tc), :] = (hf_ref[pl.ds(t0, tc), :] + hc) * jax.nn.gelu(gate)
        return hc[0:1, :]

    lax.fori_loop(0, n_chunks, bwd, jnp.zeros((1, LANES), F32))


def _block_diag_pairs(w):
    nb, bw, _ = w.shape
    w = w.reshape(nb // 2, 2, bw, bw)
    z = jnp.zeros_like(w[:, 0])
    top = jnp.concatenate([w[:, 0], z], axis=-1)
    bot = jnp.concatenate([z, w[:, 1]], axis=-1)
    return jnp.concatenate([top, bot], axis=-2)


def _lru(lru_x, lru_gate, conv_w, conv_b, wa, ba, wi, bi, lam, B, S):
    C = lru_x.shape[-1]
    nc = C // LANES
    tc = min(TC_LRU, S)
    w = jnp.concatenate([_block_diag_pairs(wa[0]), _block_diag_pairs(wi[0]),
                         _block_diag_pairs(wa[1]), _block_diag_pairs(wi[1])], axis=-1).astype(BF16)
    bias = jnp.stack([ba[0].reshape(nc, LANES), bi[0].reshape(nc, LANES),
                      ba[1].reshape(nc, LANES), bi[1].reshape(nc, LANES)], axis=1)
    bias = bias.reshape(nc, 1, 4 * LANES)
    blk = lambda b, c: (b, 0, c)
    return pl.pallas_call(
        functools.partial(_lru_kernel, S=S, tc=tc),
        grid=(B, nc),
        in_specs=[
            pl.BlockSpec((1, S, LANES), blk),
            pl.BlockSpec((1, S, LANES), blk),
            pl.BlockSpec((CONV_W, LANES), lambda b, c: (0, c)),
            pl.BlockSpec((1, LANES), lambda b, c: (0, c)),
            pl.BlockSpec((1, LANES, 4 * LANES), lambda b, c: (c, 0, 0)),
            pl.BlockSpec((1, 1, 4 * LANES), lambda b, c: (c, 0, 0)),
            pl.BlockSpec((2, LANES), lambda b, c: (0, c)),
        ],
        out_specs=pl.BlockSpec((1, S, LANES), blk),
        out_shape=jax.ShapeDtypeStruct((B, S, C), F32),
        scratch_shapes=[
            pltpu.VMEM((S + 2 * SUBLANES, LANES), F32),
            pltpu.VMEM((S, LANES), F32),
        ],
        compiler_params=_cparams(("parallel", "parallel")),
        name="rglru",
    )(lru_x, lru_gate, conv_w, conv_b.reshape(1, C), w, bias, lam)


def _rows_to_slabs(ref, x):
    n = x.shape[0]
    for s in range(SUBLANES):
        ref[pl.ds(s, n, stride=SUBLANES), :] = x[:, s * LANES:(s + 1) * LANES]


def _slabs_to_rows(ref, n):
    return jnp.concatenate([ref[pl.ds(s, n, stride=SUBLANES), :] for s in range(SUBLANES)], axis=1)


def _slab(ref, r):
    return ref.at[pl.ds(pl.multiple_of(r * SUBLANES, SUBLANES), SUBLANES)]


def _outproj_kernel(a_ref, l_ref, x_ref, ag_ref, lg_ref, wa_ref, wl_ref, g2_ref,
                    wrh_ref, wrl_ref, br_ref, tri_ref,
                    x1_ref, xn3_ref, route_ref, gates_ref, cnt_ref, carry_ref, *, attn_w, lru_w):
    step = pl.program_id(0)

    @pl.when(step == 0)
    def _():
        carry_ref[...] = jnp.zeros_like(carry_ref)

    a = a_ref[...].astype(F32)
    ams = jnp.sum(a * a, axis=-1, keepdims=True) * (1.0 / attn_w)
    an = a * lax.rsqrt(ams + NORM_EPS) * ag_ref[...]
    l = l_ref[...]
    lms = jnp.sum(l * l, axis=-1, keepdims=True) * (1.0 / lru_w)
    ln = l * lax.rsqrt(lms + NORM_EPS) * lg_ref[...]
    mix = (jnp.dot(an.astype(BF16), wa_ref[...], preferred_element_type=F32)
           + jnp.dot(ln.astype(BF16), wl_ref[...], preferred_element_type=F32))
    x1 = x_ref[...] + mix
    x1_ref[...] = x1
    ms = jnp.mean(x1 * x1, axis=-1, keepdims=True)
    xn = x1 * lax.rsqrt(ms + NORM_EPS) * g2_ref[...]
    _rows_to_slabs(xn3_ref, xn)

    hi = xn.astype(BF16)
    lo = (xn - hi.astype(F32)).astype(BF16)
    logits = (jnp.dot(hi, wrh_ref[...], preferred_element_type=F32)
              + jnp.dot(lo, wrh_ref[...], preferred_element_type=F32)
              + jnp.dot(hi, wrl_ref[...], preferred_element_type=F32)) + br_ref[...]
    lane = lax.broadcasted_iota(jnp.int32, logits.shape, 1)
    neg = -jnp.inf
    work = jnp.where(lane < N_EXPERTS, logits, neg)
    sel = jnp.zeros(logits.shape, F32)
    idxs, vals = [], []
    for _ in range(TOP_K):
        m = jnp.max(work, axis=1, keepdims=True)
        idx = jnp.min(jnp.where(work == m, lane, LANES), axis=1, keepdims=True)
        hit = lane == idx
        work = jnp.where(hit, neg, work)
        sel = sel + hit.astype(F32)
        idxs.append(idx)
        vals.append(m)
    es = [jnp.exp(v - vals[0]) for v in vals]
    den = es[0] + es[1] + es[2] + es[3]

    prefix = jnp.dot(tri_ref[...], sel.astype(BF16), preferred_element_type=F32) + carry_ref[...]
    carry_ref[...] = carry_ref[...] + jnp.sum(sel, axis=0, keepdims=True)
    cnt_ref[...] = carry_ref[...]

    route = jnp.zeros(logits.shape, jnp.int32)
    gates = jnp.zeros(logits.shape, F32)
    for k in range(TOP_K):
        rank = jnp.sum(jnp.where(lane == idxs[k], prefix, 0.0), axis=1, keepdims=True).astype(jnp.int32)
        route = jnp.where(lane == k, idxs[k], route)
        route = jnp.where(lane == TOP_K + k, rank, route)
        gates = jnp.where(lane == k, es[k] / den, gates)
    route_ref[...] = route
    gates_ref[...] = gates


def _outproj_router(attn, lru, x2, attn_out_g, lru_out_g, w_out, norm2_g, w_router, b_router):
    T, D = x2.shape
    lru_w = lru.shape[-1]
    ts = min(TS_OUT, T)
    wa = w_out[:ATTN_W].reshape(N_Q_HEADS, HEAD_DIM, D)
    wa = jnp.pad(wa, ((0, 0), (0, LANES - HEAD_DIM), (0, 0))).reshape(N_Q_HEADS * LANES, D).astype(BF16)
    wl = w_out[ATTN_W:].astype(BF16)
    ag = _pad_heads(attn_out_g.reshape(1, ATTN_W), N_Q_HEADS)
    wr = jnp.pad(w_router, ((0, 0), (0, LANES - N_EXPERTS)))
    wrh = wr.astype(BF16)
    wrl = (wr - wrh.astype(F32)).astype(BF16)
    br = jnp.pad(b_router.reshape(1, N_EXPERTS), ((0, 0), (0, LANES - N_EXPERTS)))
    tri = (jnp.arange(ts)[:, None] > jnp.arange(ts)[None, :]).astype(BF16)
    const = lambda i: (0, 0)
    tok = lambda i: (i, 0)
    aw = N_Q_HEADS * LANES
    return pl.pallas_call(
        functools.partial(_outproj_kernel, attn_w=ATTN_W, lru_w=lru_w),
        grid=(T // ts,),
        in_specs=[
            pl.BlockSpec((ts, aw), tok),
            pl.BlockSpec((ts, lru_w), tok),
            pl.BlockSpec((ts, D), tok),
            pl.BlockSpec((1, aw), const),
            pl.BlockSpec((1, lru_w), const),
            pl.BlockSpec((aw, D), const),
            pl.BlockSpec((lru_w, D), const),
            pl.BlockSpec((1, D), const),
            pl.BlockSpec((D, LANES), const),
            pl.BlockSpec((D, LANES), const),
            pl.BlockSpec((1, LANES), const),
            pl.BlockSpec((ts, ts), const),
        ],
        out_specs=[
            pl.BlockSpec((ts, D), tok),
            pl.BlockSpec((ts * SUBLANES, LANES), tok),
            pl.BlockSpec((ts, LANES), tok),
            pl.BlockSpec((ts, LANES), tok),
            pl.BlockSpec((1, LANES), const),
        ],
        out_shape=[
            jax.ShapeDtypeStruct((T, D), F32),
            jax.ShapeDtypeStruct((T * SUBLANES, LANES), F32),
            jax.ShapeDtypeStruct((T, LANES), jnp.int32),
            jax.ShapeDtypeStruct((T, LANES), F32),
            jax.ShapeDtypeStruct((1, LANES), F32),
        ],
        scratch_shapes=[pltpu.VMEM((1, LANES), F32)],
        compiler_params=_cparams(("arbitrary",)),
        name="outproj_router",
    )(attn, lru, x2, ag, lru_out_g.reshape(1, lru_w), wa, wl, norm2_g.reshape(1, D),
      wrh, wrl, br, tri)


def _dest_kernel(route_ref, pstart_ref, dest_ref):
    route = route_ref[...]
    lane = lax.broadcasted_iota(jnp.int32, route.shape, 1)
    pstart = pstart_ref[...]
    dest = jnp.zeros(route.shape, jnp.int32)
    for k in range(TOP_K):
        start = jnp.sum(jnp.where(lane == route[:, k:k + 1], pstart, 0.0), axis=1, keepdims=True)
        dest = jnp.where(lane == k, start.astype(jnp.int32) + route[:, TOP_K + k:TOP_K + k + 1], dest)
    dest_ref[...] = dest


def _dest_rows(route, pstart):
    T = route.shape[0]
    ts = min(TS_OUT, T)
    row = jnp.pad(pstart.astype(F32).reshape(1, N_EXPERTS), ((0, 0), (0, LANES - N_EXPERTS)))
    dest = pl.pallas_call(
        _dest_kernel,
        grid=(T // ts,),
        in_specs=[pl.BlockSpec((ts, LANES), lambda i: (i, 0)),
                  pl.BlockSpec((1, LANES), lambda i: (0, 0))],
        out_specs=pl.BlockSpec((ts, LANES), lambda i: (i, 0)),
        out_shape=jax.ShapeDtypeStruct((T, LANES), jnp.int32),
        compiler_params=_cparams(("parallel",)),
        name="dest_rows",
    )(route, row)
    return dest[:, :TOP_K].reshape(T * TOP_K)


def _dispatch_kernel(fill_ref, dest_ref, x_ref, out_hbm, zero_ref, sem, zero_sem, *, ts, n_blocks):
    block_slabs = ROW_BLOCK * SUBLANES

    def fill_copy(b):
        off = pl.multiple_of(b * block_slabs, block_slabs)
        return pltpu.make_async_copy(zero_ref, out_hbm.at[pl.ds(off, block_slabs)], zero_sem)

    @pl.when(pl.program_id(0) == 0)
    def _():
        zero_ref[...] = jnp.zeros(zero_ref.shape, F32)

        def start(b, carry):
            @pl.when(fill_ref[b] != 0)
            def _():
                fill_copy(b).start()
            return carry

        def wait(b, carry):
            @pl.when(fill_ref[b] != 0)
            def _():
                fill_copy(b).wait()
            return carry

        lax.fori_loop(0, n_blocks, start, 0)
        lax.fori_loop(0, n_blocks, wait, 0)

    def issue(i, carry):
        for j in range(ISSUE_UNROLL):
            r = i * ISSUE_UNROLL + j
            for k in range(TOP_K):
                d = dest_ref[r * TOP_K + k]
                pltpu.make_async_copy(_slab(x_ref, r), _slab(out_hbm, d), sem).start(priority=k % 2)
        return carry

    lax.fori_loop(0, ts // ISSUE_UNROLL, issue, 0)
    for k in range(TOP_K):
        pltpu.make_async_copy(x_ref, out_hbm.at[pl.ds(0, ts * SUBLANES)], sem).wait()


def _dispatch(xn_slabs, fill, dest_flat, n_rows):
    T = xn_slabs.shape[0] // SUBLANES
    ts = min(TS_DISP, T)
    grid_spec = pltpu.PrefetchScalarGridSpec(
        num_scalar_prefetch=1,
        grid=(T // ts,),
        in_specs=[
            pl.BlockSpec((ts * TOP_K,), lambda i, fl: (i,), memory_space=pltpu.SMEM),
            pl.BlockSpec((ts * SUBLANES, LANES), lambda i, fl: (i, 0)),
        ],
        out_specs=pl.BlockSpec(memory_space=pl.ANY),
        scratch_shapes=[pltpu.VMEM((ROW_BLOCK * SUBLANES, LANES), F32),
                        pltpu.SemaphoreType.DMA, pltpu.SemaphoreType.DMA],
    )
    return pl.pallas_call(
        functools.partial(_dispatch_kernel, ts=ts, n_blocks=n_rows // ROW_BLOCK),
        grid_spec=grid_spec,
        out_shape=jax.ShapeDtypeStruct((n_rows * SUBLANES, LANES), xn_slabs.dtype),
        compiler_params=_cparams(("arbitrary",)),
        name="dispatch",
    )(fill, dest_flat, xn_slabs)


def _expert_kernel(be_ref, na_ref, nxt_ref, x_ref, wg_hbm, bg_ref, wu_hbm, bu_ref, wd_hbm, bd_ref,
                   y_ref, stage_ref, wb_ref, slot_ref, sems):
    i = pl.program_id(0)
    e = be_ref[i]
    w_hbm = (wg_hbm, wu_hbm, wd_hbm)

    def fetch(expert, slot, m):
        return pltpu.make_async_copy(w_hbm[m].at[expert], stage_ref.at[slot, m], sems.at[slot, m])

    @pl.when(i == 0)
    def _():
        slot_ref[0] = 0
        for m in range(3):
            fetch(e, 0, m).start()

    active = i < na_ref[0]
    first = jnp.logical_or(i == 0, e != be_ref[jnp.maximum(i - 1, 0)])

    @pl.when(jnp.logical_and(active, first))
    def _():
        slot = slot_ref[0]
        for m in range(3):
            fetch(e, slot, m).wait()
            wb_ref[m] = stage_ref[slot, m].astype(BF16)

        @pl.when(nxt_ref[i] >= 0)
        def _():
            for m in range(3):
                fetch(nxt_ref[i], 1 - slot, m).start()

        slot_ref[0] = 1 - slot

    @pl.when(active)
    def _():
        x = _slabs_to_rows(x_ref, ROW_BLOCK).astype(BF16)
        g = jnp.dot(x, wb_ref[0], preferred_element_type=F32) + bg_ref[0]
        u = jnp.dot(x, wb_ref[1], preferred_element_type=F32) + bu_ref[0]
        g = jnp.minimum(g, SWIGLU_LIMIT)
        u = jnp.clip(u, -SWIGLU_LIMIT, SWIGLU_LIMIT)
        glu = g * jax.nn.sigmoid(SWIGLU_ALPHA * g)
        y = jnp.dot(((u + 1.0) * glu).astype(BF16), wb_ref[2], preferred_element_type=F32) + bd_ref[0]
        _rows_to_slabs(y_ref, y)


def _experts(x_rows, block_e, n_active, next_e, w_gate, b_gate, w_up, b_up, w_down, b_down):
    E, D, FF = w_gate.shape
    assert D == FF, "the three expert matrices share one staging shape"
    block_slabs = ROW_BLOCK * SUBLANES
    n_blocks = x_rows.shape[0] // block_slabs

    def row_map(i, be, na, nx):
        return (jnp.minimum(i, na[0] - 1), 0)

    def b_map(i, be, na, nx):
        return (be[jnp.minimum(i, na[0] - 1)], 0, 0)

    grid_spec = pltpu.PrefetchScalarGridSpec(
        num_scalar_prefetch=3,
        grid=(n_blocks,),
        in_specs=[
            pl.BlockSpec((block_slabs, LANES), row_map),
            pl.BlockSpec(memory_space=pl.ANY),
            pl.BlockSpec((1, 1, FF), b_map),
            pl.BlockSpec(memory_space=pl.ANY),
            pl.BlockSpec((1, 1, FF), b_map),
            pl.BlockSpec(memory_space=pl.ANY),
            pl.BlockSpec((1, 1, D), b_map),
        ],
        out_specs=pl.BlockSpec((block_slabs, LANES), row_map),
        scratch_shapes=[
            pltpu.VMEM((2, 3, D, FF), F32),
            pltpu.VMEM((3, D, FF), BF16),
            pltpu.SMEM((1,), jnp.int32),
            pltpu.SemaphoreType.DMA((2, 3)),
        ],
    )
    return pl.pallas_call(
        _expert_kernel,
        grid_spec=grid_spec,
        out_shape=jax.ShapeDtypeStruct(x_rows.shape, F32),
        input_output_aliases={3: 0},
        compiler_params=pltpu.CompilerParams(dimension_semantics=("arbitrary",),
                                             vmem_limit_bytes=EXPERT_VMEM_LIMIT),
        name="experts",
    )(block_e, n_active, next_e, x_rows, w_gate, b_gate.reshape(E, 1, FF), w_up,
      b_up.reshape(E, 1, FF), w_down, b_down.reshape(E, 1, D))


def _combine_kernel(dest_ref, dest_next_ref, y_hbm, x1_ref, gates_ref, fg_ref, o_ref, bufs, sems,
                    *, ts, n_steps):
    i = pl.program_id(0)
    slot = i % 2

    def gather_tile(d_ref, s):
        def issue(it, carry):
            for j in range(ISSUE_UNROLL):
                r = it * ISSUE_UNROLL + j
                for k in range(TOP_K):
                    d = d_ref[r * TOP_K + k]
                    pltpu.make_async_copy(_slab(y_hbm, d), _slab(bufs.at[s, k], r),
                                          sems.at[s]).start(priority=k % 2)
            return carry

        lax.fori_loop(0, ts // ISSUE_UNROLL, issue, 0)

    @pl.when(i == 0)
    def _():
        gather_tile(dest_ref, 0)

    @pl.when(i + 1 < n_steps)
    def _():
        gather_tile(dest_next_ref, 1 - slot)

    for k in range(TOP_K):
        pltpu.make_async_copy(y_hbm.at[pl.ds(0, ts * SUBLANES)], bufs.at[slot, k], sems.at[slot]).wait()

    acc = x1_ref[...]
    gates = gates_ref[...]
    for k in range(TOP_K):
        acc = acc + _slabs_to_rows(bufs.at[slot, k], ts) * gates[:, k:k + 1]
    ms = jnp.mean(acc * acc, axis=-1, keepdims=True)
    o_ref[...] = acc * lax.rsqrt(ms + NORM_EPS) * fg_ref[...]


def _combine(y_rows, dest_flat, x1, gates, final_g):
    T, D = x1.shape
    ts = min(TS_COMB, T)
    n_steps = T // ts
    tok = lambda i: (i, 0)
    return pl.pallas_call(
        functools.partial(_combine_kernel, ts=ts, n_steps=n_steps),
        grid=(n_steps,),
        in_specs=[
            pl.BlockSpec((ts * TOP_K,), lambda i: (i,), memory_space=pltpu.SMEM),
            pl.BlockSpec((ts * TOP_K,), lambda i: (jnp.minimum(i + 1, n_steps - 1),),
                         memory_space=pltpu.SMEM),
            pl.BlockSpec(memory_space=pl.ANY),
            pl.BlockSpec((ts, D), tok),
            pl.BlockSpec((ts, LANES), tok),
            pl.BlockSpec((1, D), lambda i: (0, 0)),
        ],
        out_specs=pl.BlockSpec((ts, D), tok),
        out_shape=jax.ShapeDtypeStruct((T, D), F32),
        scratch_shapes=[pltpu.VMEM((2, TOP_K, ts * SUBLANES, LANES), F32),
                        pltpu.SemaphoreType.DMA((2,))],
        compiler_params=_cparams(("arbitrary",)),
        name="combine",
    )(dest_flat, dest_flat, y_rows, x1, gates, final_g.reshape(1, D))


def kernel(x, norm1_g, w_in, q_norm_g, k_norm_g, conv_w, conv_b, lru_wa, lru_ba, lru_wi, lru_bi,
           lru_lam, attn_out_g, lru_out_g, w_out, norm2_g, w_router, b_router, w_gate, b_gate,
           w_up, b_up, w_down, b_down, final_g):
    B, S, D = x.shape
    T = B * S
    assert w_in.shape[0] == 1, "single-layer trunk: the final norm is fused into the layer's combine"
    x2 = x.reshape(T, D)
    for l in range(1):
        qt, k, vt, lru_x, lru_gate = _inproj(x2, norm1_g[l], w_in[l], q_norm_g[l], k_norm_g[l], S)
        score_bound = (HEAD_DIM * Q_SCALE * jnp.max(jnp.abs(q_norm_g[l]))
                       * jnp.max(jnp.abs(k_norm_g[l])))
        attn = _attention(qt, k.reshape(B, S, -1), vt, score_bound, B, S)
        lru = _lru(lru_x.reshape(B, S, -1), lru_gate.reshape(B, S, -1), conv_w[l], conv_b[l],
                   lru_wa[l], lru_ba[l], lru_wi[l], lru_bi[l], lru_lam[l], B, S)
        x1, xn3, route, gates, cnt = _outproj_router(
            attn.reshape(T, -1), lru.reshape(T, -1), x2, attn_out_g[l], lru_out_g[l], w_out[l],
            norm2_g[l], w_router[l], b_router[l])

        counts = cnt[0, :N_EXPERTS].astype(jnp.int32)
        padded = ((counts + ROW_BLOCK - 1) // ROW_BLOCK) * ROW_BLOCK
        pend = jnp.cumsum(padded)
        pstart = (pend - padded).astype(jnp.int32)
        n_rows = T * TOP_K + N_EXPERTS * ROW_BLOCK
        block_start = jnp.arange(n_rows // ROW_BLOCK, dtype=jnp.int32) * ROW_BLOCK
        block_e = jnp.sum((pend[None, :] <= block_start[:, None]).astype(jnp.int32), axis=1)
        block_e = jnp.minimum(block_e, N_EXPERTS - 1)
        n_active = (pend[-1:] // ROW_BLOCK).astype(jnp.int32)
        fill = jnp.logical_or(block_start + ROW_BLOCK == pend[block_e],
                              block_start >= pend[-1]).astype(jnp.int32)

        next_block = pend[block_e] // ROW_BLOCK
        next_e = jnp.where(next_block < n_active[0],
                           block_e[jnp.minimum(next_block, block_e.shape[0] - 1)], -1).astype(jnp.int32)

        dest_flat = _dest_rows(route, pstart)
        x_rows = _dispatch(xn3, fill, dest_flat, n_rows)
        y_rows = _experts(x_rows, block_e, n_active, next_e, w_gate[l], b_gate[l], w_up[l], b_up[l],
                          w_down[l], b_down[l])
        x2 = _combine(y_rows, dest_flat, x1, gates, final_g)
    return x2.reshape(B, S, D)
```

```python
import functools
import math

import jax
import jax.numpy as jnp
from jax import lax
from jax.experimental import pallas as pl
from jax.experimental.pallas import tpu as pltpu

F32 = jnp.float32
BF16 = jnp.bfloat16

GRID_W = 64
HEAD_DIM = 64
N_Q_HEADS = 8
N_KV_HEADS = 2
GQA_GROUP = N_Q_HEADS // N_KV_HEADS
ATTN_W = N_Q_HEADS * HEAD_DIM
KV_W = N_KV_HEADS * HEAD_DIM
LRU_BLOCKS = 8
LRU_C = 8.0
CONV_W = 4
CONV_PAD_L = 2
ROPE_THETA = 10000.0
ROPE_HALF = HEAD_DIM // 2
ROPE_M = ROPE_HALF // 2
N_EXPERTS = 32
TOP_K = 4
SWIGLU_ALPHA = 1.702
SWIGLU_LIMIT = 7.0
NORM_EPS = 1e-5
QK_EPS = 1e-6
LOG2_E = 1.4426950408889634
Q_SCALE = HEAD_DIM ** -0.5 * LOG2_E
SAFE_SCORE_LOG2 = 96.0

LANES = 128
SUBLANES = 8
BF16_SUBLANES = 16
PV_ROWS = LANES
VMEM_LIMIT = 48 * 1024 * 1024
EXPERT_VMEM_LIMIT = 56 * 1024 * 1024

TS_IN = 512
TQ = 256
TK = 256
KV_UNROLL = 8
HEADS_PER_STEP = 2
TC_LRU = 512
TS_OUT = 512
ROW_BLOCK = 256
TS_DISP = 512
TS_COMB = 256
ISSUE_UNROLL = 8


def _cparams(sem):
    return pltpu.CompilerParams(dimension_semantics=sem, vmem_limit_bytes=VMEM_LIMIT)


def _inproj_kernel(x_ref, g1_ref, w_ref, qg_ref, kg_ref, cos_ref, sin_ref,
                   q_ref, k_ref, v_ref, lx_ref, lg_ref, *, lru_w):
    x = x_ref[...]
    ms = jnp.mean(x * x, axis=-1, keepdims=True)
    xn = x * lax.rsqrt(ms + NORM_EPS) * g1_ref[...]
    h = jnp.dot(xn.astype(BF16), w_ref[...], preferred_element_type=F32)

    cos = cos_ref[...]
    sin = sin_ref[...]
    lane = lax.broadcasted_iota(jnp.int32, cos.shape, 1)
    first_half = (lane % ROPE_HALF) < ROPE_M

    def head_norm_rope(xc, g, scale):
        hms = jnp.sum(xc * xc, axis=-1, keepdims=True) * (1.0 / HEAD_DIM)
        xc = xc * lax.rsqrt(hms + QK_EPS) * g
        partner = jnp.where(first_half,
                            pltpu.roll(xc, LANES - ROPE_M, 1),
                            pltpu.roll(xc, ROPE_M, 1))
        return (xc * cos + partner * sin) * scale

    qw = N_Q_HEADS * LANES
    kw = N_KV_HEADS * LANES
    for c in range(N_Q_HEADS):
        sl = slice(c * LANES, (c + 1) * LANES)
        q_ref[0, sl, :] = head_norm_rope(h[:, sl], qg_ref[...], Q_SCALE).T.astype(BF16)
    for c in range(N_KV_HEADS):
        sl = slice(c * LANES, (c + 1) * LANES)
        k_ref[:, sl] = head_norm_rope(h[:, qw + c * LANES: qw + (c + 1) * LANES],
                                      kg_ref[...], 1.0).astype(BF16)
        vc = h[:, qw + kw + c * LANES: qw + kw + (c + 1) * LANES]
        v_ref[0, sl, :] = jnp.where(lane >= HEAD_DIM, 1.0, vc).T.astype(BF16)
    o = qw + 2 * kw
    lx_ref[...] = h[:, o: o + lru_w]
    lg_ref[...] = h[:, o + lru_w: o + 2 * lru_w]


def _pad_heads(w, n_heads):
    lead = w.shape[:-1]
    w = w.reshape(lead + (n_heads, HEAD_DIM))
    w = jnp.pad(w, [(0, 0)] * len(lead) + [(0, 0), (0, LANES - HEAD_DIM)])
    return w.reshape(lead + (n_heads * LANES,))


def _rope_tables(S):
    t = jnp.arange(S)
    rows = (t // GRID_W).astype(F32)
    cols = (t % GRID_W).astype(F32)
    inv_freq = ROPE_THETA ** (-jnp.arange(ROPE_M, dtype=F32) / ROPE_M)
    ar = rows[:, None] * inv_freq[None, :]
    ac = cols[:, None] * inv_freq[None, :]
    cos = jnp.concatenate([jnp.cos(ar), jnp.cos(ar), jnp.cos(ac), jnp.cos(ac)], axis=-1)
    sin = jnp.concatenate([-jnp.sin(ar), jnp.sin(ar), -jnp.sin(ac), jnp.sin(ac)], axis=-1)
    pad = [(0, 0), (0, LANES - HEAD_DIM)]
    return jnp.pad(cos, pad), jnp.pad(sin, pad)


def _inproj(x2, norm1_g, w_in, q_norm_g, k_norm_g, S):
    T, D = x2.shape
    lru_w = (w_in.shape[1] - ATTN_W - 2 * KV_W) // 2
    o0, o1, o2 = ATTN_W, ATTN_W + KV_W, ATTN_W + 2 * KV_W
    w_all = jnp.concatenate([
        _pad_heads(w_in[:, :o0], N_Q_HEADS),
        _pad_heads(w_in[:, o0:o1], N_KV_HEADS),
        _pad_heads(w_in[:, o1:o2], N_KV_HEADS),
        w_in[:, o2:],
    ], axis=1).astype(BF16)
    qg = _pad_heads(q_norm_g.reshape(1, HEAD_DIM), 1)
    kg = _pad_heads(k_norm_g.reshape(1, HEAD_DIM), 1)
    cos, sin = _rope_tables(S)
    ts = TS_IN
    n_s = S // ts
    qw, kw = N_Q_HEADS * LANES, N_KV_HEADS * LANES
    const = lambda i: (0, 0)
    tok = lambda i: (i, 0)
    pos = lambda i: (i % n_s, 0)
    tposed = lambda i: (i // n_s, 0, i % n_s)
    return pl.pallas_call(
        functools.partial(_inproj_kernel, lru_w=lru_w),
        grid=(T // ts,),
        in_specs=[
            pl.BlockSpec((ts, D), tok),
            pl.BlockSpec((1, D), const),
            pl.BlockSpec(w_all.shape, const),
            pl.BlockSpec((1, LANES), const),
            pl.BlockSpec((1, LANES), const),
            pl.BlockSpec((ts, LANES), pos),
            pl.BlockSpec((ts, LANES), pos),
        ],
        out_specs=[
            pl.BlockSpec((1, qw, ts), tposed),
            pl.BlockSpec((ts, kw), tok),
            pl.BlockSpec((1, kw, ts), tposed),
            pl.BlockSpec((ts, lru_w), tok),
            pl.BlockSpec((ts, lru_w), tok),
        ],
        out_shape=[
            jax.ShapeDtypeStruct((T // S, qw, S), BF16),
            jax.ShapeDtypeStruct((T, kw), BF16),
            jax.ShapeDtypeStruct((T // S, kw, S), BF16),
            jax.ShapeDtypeStruct((T, lru_w), F32),
            jax.ShapeDtypeStruct((T, lru_w), F32),
        ],
        compiler_params=_cparams(("parallel",)),
        name="inproj",
    )(x2, norm1_g.reshape(1, D), w_all, qg, kg, cos, sin)


def _attn_kernel(qt_ref, k_ref, vt_ref, o_ref, acc_ref, s_ref, p_ref, *, tq, tk, n_kv, kv_unroll):
    hp = HEADS_PER_STEP
    spt = GQA_GROUP // hp
    acc_ref[...] = jnp.zeros(acc_ref.shape, F32)

    def scores(j, sp):
        kt = k_ref[0, pl.ds(pl.multiple_of(j * tk, tk), tk), :]
        out = []
        for u in range(hp):
            g = sp * hp + u
            s = jnp.dot(kt, qt_ref[0, g * LANES:(g + 1) * LANES, :], preferred_element_type=F32)
            out.append((s, jnp.max(s, axis=0, keepdims=True)))
        return out

    def softmax_stage(sc, ms, sp):
        out = []
        for u, (s, s_max) in enumerate(sc):
            h = sp * hp + u
            m_new = jnp.maximum(ms[h], s_max)
            out.append((jnp.exp2(ms[h] - m_new), jnp.exp2(s - m_new).astype(BF16)))
            ms[h] = m_new
        return out

    def pv_stage(j, sp, ap):
        vt = vt_ref[0, 0:PV_ROWS, pl.ds(pl.multiple_of(j * tk, tk), tk)]
        for u, (alpha, p) in enumerate(ap):
            g = sp * hp + u
            acc_ref[g] = alpha * acc_ref[g] + jnp.dot(vt, p, preferred_element_type=F32)

    ms = [jnp.full((1, tq), -jnp.inf, F32)] * GQA_GROUP
    ap = softmax_stage(scores(0, 0), ms, 0)
    sc = scores(min(1 // spt, n_kv - 1), 1 % spt)
    for u in range(hp):
        s_ref[u] = sc[u][0]
        p_ref[u] = ap[u][1]

    def body(it, carry):
        ms = list(carry[:GQA_GROUP])
        ap = [(carry[GQA_GROUP + u], p_ref[u]) for u in range(hp)]
        sc = [(s_ref[u], carry[GQA_GROUP + hp + u]) for u in range(hp)]
        for n in range(kv_unroll * spt):
            j = it * kv_unroll + n // spt
            j_next = jnp.minimum(it * kv_unroll + (n + 2) // spt, n_kv - 1)
            sc_next = scores(j_next, (n + 2) % spt)
            ap_next = softmax_stage(sc, ms, (n + 1) % spt)
            pv_stage(j, n % spt, ap)
            sc, ap = sc_next, ap_next
        for u in range(hp):
            s_ref[u] = sc[u][0]
            p_ref[u] = ap[u][1]
        return tuple(ms) + tuple(a for a, _ in ap) + tuple(m for _, m in sc)

    lax.fori_loop(0, n_kv // kv_unroll, body,
                  tuple(ms) + tuple(a for a, _ in ap) + tuple(m for _, m in sc))
    _attn_finalize(acc_ref, o_ref, tq)


def _attn_finalize(acc_ref, o_ref, tq):
    pad = jnp.zeros((LANES - HEAD_DIM, tq), F32)
    for g in range(GQA_GROUP):
        acc = acc_ref[g]
        o = acc[0:HEAD_DIM] / acc[HEAD_DIM:HEAD_DIM + 1, :]
        o_ref[0, :, g * LANES:(g + 1) * LANES] = jnp.concatenate([o, pad], axis=0).T.astype(BF16)


def _attn_bounded_kernel(qt_ref, k_ref, vt_ref, o_ref, acc_ref, s_ref, p_ref, *, tq, tk, n_kv, kv_unroll):
    hp = HEADS_PER_STEP
    spt = GQA_GROUP // hp
    acc_ref[...] = jnp.zeros(acc_ref.shape, F32)

    def scores(j, sp):
        kt = k_ref[0, pl.ds(pl.multiple_of(j * tk, tk), tk), :]
        return [jnp.dot(kt, qt_ref[0, (sp * hp + u) * LANES:(sp * hp + u + 1) * LANES, :],
                        preferred_element_type=F32) for u in range(hp)]

    def probs(sc):
        return [jnp.exp2(s).astype(BF16) for s in sc]

    def pv_stage(j, sp, ps):
        vt = vt_ref[0, 0:PV_ROWS, pl.ds(pl.multiple_of(j * tk, tk), tk)]
        for u, p in enumerate(ps):
            acc_ref[sp * hp + u] += jnp.dot(vt, p, preferred_element_type=F32)

    ps = probs(scores(0, 0))
    sc = scores(min(1 // spt, n_kv - 1), 1 % spt)
    for u in range(hp):
        s_ref[u] = sc[u]
        p_ref[u] = ps[u]

    def body(it, carry):
        ps = [p_ref[u] for u in range(hp)]
        sc = [s_ref[u] for u in range(hp)]
        for n in range(kv_unroll * spt):
            j = it * kv_unroll + n // spt
            j_next = jnp.minimum(it * kv_unroll + (n + 2) // spt, n_kv - 1)
            sc_next = scores(j_next, (n + 2) % spt)
            ps_next = probs(sc)
            pv_stage(j, n % spt, ps)
            sc, ps = sc_next, ps_next
        for u in range(hp):
            s_ref[u] = sc[u]
            p_ref[u] = ps[u]
        return carry

    lax.fori_loop(0, n_kv // kv_unroll, body, 0)
    _attn_finalize(acc_ref, o_ref, tq)


def _attention(qt, k, vt, score_bound, B, S):
    tq = min(TQ, S)
    tk = min(TK, S)
    gw = GQA_GROUP * LANES

    def call(body, name):
        return pl.pallas_call(
            functools.partial(body, tq=tq, tk=tk, n_kv=S // tk,
                              kv_unroll=math.gcd(S // tk, KV_UNROLL)),
            grid=(B, N_KV_HEADS, S // tq),
            in_specs=[
                pl.BlockSpec((1, gw, tq), lambda b, h, i: (b, h, i)),
                pl.BlockSpec((1, S, LANES), lambda b, h, i: (b, 0, h)),
                pl.BlockSpec((1, LANES, S), lambda b, h, i: (b, h, 0)),
            ],
            out_specs=pl.BlockSpec((1, tq, gw), lambda b, h, i: (b, i, h)),
            out_shape=jax.ShapeDtypeStruct((B, S, N_Q_HEADS * LANES), BF16),
            scratch_shapes=[pltpu.VMEM((GQA_GROUP, PV_ROWS, tq), F32),
                            pltpu.VMEM((HEADS_PER_STEP, tk, tq), F32),
                            pltpu.VMEM((HEADS_PER_STEP, tk, tq), BF16)],
            compiler_params=_cparams(("parallel", "parallel", "parallel")),
            name=name,
        )

    return lax.cond(score_bound <= SAFE_SCORE_LOG2,
                    call(_attn_bounded_kernel, "attention_bounded"),
                    call(_attn_kernel, "attention"), qt, k, vt)


def _scan_chunk(a, b, h_in, reverse):
    n = a.shape[0]
    n_groups = n // SUBLANES
    a = a.reshape(n_groups, SUBLANES, LANES)
    b = b.reshape(n_groups, SUBLANES, LANES)
    sub = lax.broadcasted_iota(jnp.int32, a.shape, 1)
    d = 1
    while d < SUBLANES:
        if reverse:
            keep = sub < SUBLANES - d
            shift = SUBLANES - d
        else:
            keep = sub >= d
            shift = d
        a_sh = jnp.where(keep, pltpu.roll(a, shift, 1), 1.0)
        b_sh = jnp.where(keep, pltpu.roll(b, shift, 1), 0.0)
        b = a * b_sh + b
        a = a * a_sh
        d *= 2
    a = a.reshape(n, LANES)
    b = b.reshape(n, LANES)
    order = range(n_groups - 1, -1, -1) if reverse else range(n_groups)
    edge = h_in
    out = [None] * n_groups
    for v in order:
        rows = slice(v * SUBLANES, (v + 1) * SUBLANES)
        hv = b[rows] + a[rows] * jnp.broadcast_to(edge, (SUBLANES, LANES))
        out[v] = hv
        edge = hv[0:1] if reverse else hv[SUBLANES - 1:SUBLANES]
    return jnp.concatenate(out, axis=0), edge


def _lru_kernel(u_ref, gate_ref, cw_ref, cb_ref, w_ref, bias_ref, lam_ref, o_ref,
                up_ref, hf_ref, *, S, tc):
    halo = SUBLANES
    zeros = jnp.zeros((halo, LANES), F32)
    up_ref[0:halo, :] = zeros
    up_ref[S + halo:S + 2 * halo, :] = zeros
    up_ref[halo:S + halo, :] = u_ref[0]
    sp = jax.nn.softplus(-lam_ref[...])
    cw = cw_ref[...]
    cb = cb_ref[...]
    n_chunks = S // tc
    ext = tc + 2 * halo

    def gates(c, d):
        t0 = pl.multiple_of(c * tc, tc)
        ue = up_ref[pl.ds(t0, ext), :]
        xc = cb
        for j in range(CONV_W):
            sh = (CONV_PAD_L - j) % ext
            uj = ue if sh == 0 else pltpu.roll(ue, sh, 0)
            xc = xc + uj[halo:halo + tc] * cw[j:j + 1, :]
        gw = 2 * LANES
        g = jnp.dot(xc.astype(BF16), w_ref[0, :, d * gw:(d + 1) * gw],
                    preferred_element_type=F32) + bias_ref[0, :, d * gw:(d + 1) * gw]
        r = jax.nn.sigmoid(g[:, :LANES])
        i = jax.nn.sigmoid(g[:, LANES:])
        log_a = -LRU_C * r * sp[d:d + 1, :]
        a = jnp.exp(log_a)
        y = 1.0 - a * a
        b = jnp.where(y > 0.0, y * lax.rsqrt(y), 0.0) * i * xc
        return t0, a, b

    def fwd(c, h):
        t0, a, b = gates(c, 0)
        hc, h_last = _scan_chunk(a, b, h, False)
        hf_ref[pl.ds(t0, tc), :] = hc
        return h_last

    lax.fori_loop(0, n_chunks, fwd, jnp.zeros((1, LANES), F32))

    def bwd(ci, h):
        t0, a, b = gates(n_chunks - 1 - ci, 1)
        hc, h_last = _scan_chunk(a, b, h, True)
        gate = gate_ref[0, pl.ds(t0, tc), :]
        o_ref[0, pl.ds(t0, tc), :] = (hf_ref[pl.ds(t0, tc), :] + hc) * jax.nn.gelu(gate)
        return h_last

    lax.fori_loop(0, n_chunks, bwd, jnp.zeros((1, LANES), F32))


def _block_diag_pairs(w):
    nb, bw, _ = w.shape
    w = w.reshape(nb // 2, 2, bw, bw)
    z = jnp.zeros_like(w[:, 0])
    top = jnp.concatenate([w[:, 0], z], axis=-1)
    bot = jnp.concatenate([z, w[:, 1]], axis=-1)
    return jnp.concatenate([top, bot], axis=-2)


def _lru(lru_x, lru_gate, conv_w, conv_b, wa, ba, wi, bi, lam, B, S):
    C = lru_x.shape[-1]
    nc = C // LANES
    tc = min(TC_LRU, S)
    w = jnp.concatenate([_block_diag_pairs(wa[0]), _block_diag_pairs(wi[0]),
                         _block_diag_pairs(wa[1]), _block_diag_pairs(wi[1])], axis=-1).astype(BF16)
    bias = jnp.stack([ba[0].reshape(nc, LANES), bi[0].reshape(nc, LANES),
                      ba[1].reshape(nc, LANES), bi[1].reshape(nc, LANES)], axis=1)
    bias = bias.reshape(nc, 1, 4 * LANES)
    blk = lambda b, c: (b, 0, c)
    return pl.pallas_call(
        functools.partial(_lru_kernel, S=S, tc=tc),
        grid=(B, nc),
        in_specs=[
            pl.BlockSpec((1, S, LANES), blk),
            pl.BlockSpec((1, S, LANES), blk),
            pl.BlockSpec((CONV_W, LANES), lambda b, c: (0, c)),
            pl.BlockSpec((1, LANES), lambda b, c: (0, c)),
            pl.BlockSpec((1, LANES, 4 * LANES), lambda b, c: (c, 0, 0)),
            pl.BlockSpec((1, 1, 4 * LANES), lambda b, c: (c, 0, 0)),
            pl.BlockSpec((2, LANES), lambda b, c: (0, c)),
        ],
        out_specs=pl.BlockSpec((1, S, LANES), blk),
        out_shape=jax.ShapeDtypeStruct((B, S, C), F32),
        scratch_shapes=[
            pltpu.VMEM((S + 2 * SUBLANES, LANES), F32),
            pltpu.VMEM((S, LANES), F32),
        ],
        compiler_params=_cparams(("parallel", "parallel")),
        name="rglru",
    )(lru_x, lru_gate, conv_w, conv_b.reshape(1, C), w, bias, lam)


def _rows_to_slabs(ref, x):
    n = x.shape[0]
    for s in range(SUBLANES):
        ref[pl.ds(s, n, stride=SUBLANES), :] = x[:, s * LANES:(s + 1) * LANES]


def _slabs_to_rows(ref, n):
    return jnp.concatenate([ref[pl.ds(s, n, stride=SUBLANES), :] for s in range(SUBLANES)], axis=1)


def _slab(ref, r):
    return ref.at[pl.ds(pl.multiple_of(r * SUBLANES, SUBLANES), SUBLANES)]


def _outproj_kernel(a_ref, l_ref, x_ref, ag_ref, lg_ref, wa_ref, wl_ref, g2_ref,
                    wrh_ref, wrl_ref, br_ref, tri_ref,
                    x1_ref, xn3_ref, route_ref, gates_ref, cnt_ref, carry_ref, *, attn_w, lru_w):
    step = pl.program_id(0)

    @pl.when(step == 0)
    def _():
        carry_ref[...] = jnp.zeros_like(carry_ref)

    a = a_ref[...].astype(F32)
    ams = jnp.sum(a * a, axis=-1, keepdims=True) * (1.0 / attn_w)
    an = a * lax.rsqrt(ams + NORM_EPS) * ag_ref[...]
    l = l_ref[...]
    lms = jnp.sum(l * l, axis=-1, keepdims=True) * (1.0 / lru_w)
    ln = l * lax.rsqrt(lms + NORM_EPS) * lg_ref[...]
    mix = (jnp.dot(an.astype(BF16), wa_ref[...], preferred_element_type=F32)
           + jnp.dot(ln.astype(BF16), wl_ref[...], preferred_element_type=F32))
    x1 = x_ref[...] + mix
    x1_ref[...] = x1
    ms = jnp.mean(x1 * x1, axis=-1, keepdims=True)
    xn = x1 * lax.rsqrt(ms + NORM_EPS) * g2_ref[...]
    _rows_to_slabs(xn3_ref, xn)

    hi = xn.astype(BF16)
    lo = (xn - hi.astype(F32)).astype(BF16)
    logits = (jnp.dot(hi, wrh_ref[...], preferred_element_type=F32)
              + jnp.dot(lo, wrh_ref[...], preferred_element_type=F32)
              + jnp.dot(hi, wrl_ref[...], preferred_element_type=F32)) + br_ref[...]
    lane = lax.broadcasted_iota(jnp.int32, logits.shape, 1)
    neg = -jnp.inf
    work = jnp.where(lane < N_EXPERTS, logits, neg)
    sel = jnp.zeros(logits.shape, F32)
    idxs, vals = [], []
    for _ in range(TOP_K):
        m = jnp.max(work, axis=1, keepdims=True)
        idx = jnp.min(jnp.where(work == m, lane, LANES), axis=1, keepdims=True)
        hit = lane == idx
        work = jnp.where(hit, neg, work)
        sel = sel + hit.astype(F32)
        idxs.append(idx)
        vals.append(m)
    es = [jnp.exp(v - vals[0]) for v in vals]
    den = es[0] + es[1] + es[2] + es[3]

    prefix = jnp.dot(tri_ref[...], sel.astype(BF16), preferred_element_type=F32) + carry_ref[...]
    carry_ref[...] = carry_ref[...] + jnp.sum(sel, axis=0, keepdims=True)
    cnt_ref[...] = carry_ref[...]

    route = jnp.zeros(logits.shape, jnp.int32)
    gates = jnp.zeros(logits.shape, F32)
    for k in range(TOP_K):
        rank = jnp.sum(jnp.where(lane == idxs[k], prefix, 0.0), axis=1, keepdims=True).astype(jnp.int32)
        route = jnp.where(lane == k, idxs[k], route)
        route = jnp.where(lane == TOP_K + k, rank, route)
        gates = jnp.where(lane == k, es[k] / den, gates)
    route_ref[...] = route
    gates_ref[...] = gates


def _outproj_router(attn, lru, x2, attn_out_g, lru_out_g, w_out, norm2_g, w_router, b_router):
    T, D = x2.shape
    lru_w = lru.shape[-1]
    ts = min(TS_OUT, T)
    wa = w_out[:ATTN_W].reshape(N_Q_HEADS, HEAD_DIM, D)
    wa = jnp.pad(wa, ((0, 0), (0, LANES - HEAD_DIM), (0, 0))).reshape(N_Q_HEADS * LANES, D).astype(BF16)
    wl = w_out[ATTN_W:].astype(BF16)
    ag = _pad_heads(attn_out_g.reshape(1, ATTN_W), N_Q_HEADS)
    wr = jnp.pad(w_router, ((0, 0), (0, LANES - N_EXPERTS)))
    wrh = wr.astype(BF16)
    wrl = (wr - wrh.astype(F32)).astype(BF16)
    br = jnp.pad(b_router.reshape(1, N_EXPERTS), ((0, 0), (0, LANES - N_EXPERTS)))
    tri = (jnp.arange(ts)[:, None] > jnp.arange(ts)[None, :]).astype(BF16)
    const = lambda i: (0, 0)
    tok = lambda i: (i, 0)
    aw = N_Q_HEADS * LANES
    return pl.pallas_call(
        functools.partial(_outproj_kernel, attn_w=ATTN_W, lru_w=lru_w),
        grid=(T // ts,),
        in_specs=[
            pl.BlockSpec((ts, aw), tok),
            pl.BlockSpec((ts, lru_w), tok),
            pl.BlockSpec((ts, D), tok),
            pl.BlockSpec((1, aw), const),
            pl.BlockSpec((1, lru_w), const),
            pl.BlockSpec((aw, D), const),
            pl.BlockSpec((lru_w, D), const),
            pl.BlockSpec((1, D), const),
            pl.BlockSpec((D, LANES), const),
            pl.BlockSpec((D, LANES), const),
            pl.BlockSpec((1, LANES), const),
            pl.BlockSpec((ts, ts), const),
        ],
        out_specs=[
            pl.BlockSpec((ts, D), tok),
            pl.BlockSpec((ts * SUBLANES, LANES), tok),
            pl.BlockSpec((ts, LANES), tok),
            pl.BlockSpec((ts, LANES), tok),
            pl.BlockSpec((1, LANES), const),
        ],
        out_shape=[
            jax.ShapeDtypeStruct((T, D), F32),
            jax.ShapeDtypeStruct((T * SUBLANES, LANES), F32),
            jax.ShapeDtypeStruct((T, LANES), jnp.int32),
            jax.ShapeDtypeStruct((T, LANES), F32),
            jax.ShapeDtypeStruct((1, LANES), F32),
        ],
        scratch_shapes=[pltpu.VMEM((1, LANES), F32)],
        compiler_params=_cparams(("arbitrary",)),
        name="outproj_router",
    )(attn, lru, x2, ag, lru_out_g.reshape(1, lru_w), wa, wl, norm2_g.reshape(1, D),
      wrh, wrl, br, tri)


def _dest_kernel(route_ref, pstart_ref, dest_ref):
    route = route_ref[...]
    lane = lax.broadcasted_iota(jnp.int32, route.shape, 1)
    pstart = pstart_ref[...]
    dest = jnp.zeros(route.shape, jnp.int32)
    for k in range(TOP_K):
        start = jnp.sum(jnp.where(lane == route[:, k:k + 1], pstart, 0.0), axis=1, keepdims=True)
        dest = jnp.where(lane == k, start.astype(jnp.int32) + route[:, TOP_K + k:TOP_K + k + 1], dest)
    dest_ref[...] = dest


def _dest_rows(route, pstart):
    T = route.shape[0]
    ts = min(TS_OUT, T)
    row = jnp.pad(pstart.astype(F32).reshape(1, N_EXPERTS), ((0, 0), (0, LANES - N_EXPERTS)))
    dest = pl.pallas_call(
        _dest_kernel,
        grid=(T // ts,),
        in_specs=[pl.BlockSpec((ts, LANES), lambda i: (i, 0)),
                  pl.BlockSpec((1, LANES), lambda i: (0, 0))],
        out_specs=pl.BlockSpec((ts, LANES), lambda i: (i, 0)),
        out_shape=jax.ShapeDtypeStruct((T, LANES), jnp.int32),
        compiler_params=_cparams(("parallel",)),
        name="dest_rows",
    )(route, row)
    return dest[:, :TOP_K].reshape(T * TOP_K)


def _dispatch_kernel(fill_ref, dest_ref, x_ref, out_hbm, zero_ref, sem, zero_sem, *, ts, n_blocks):
    block_slabs = ROW_BLOCK * SUBLANES

    def fill_copy(b):
        off = pl.multiple_of(b * block_slabs, block_slabs)
        return pltpu.make_async_copy(zero_ref, out_hbm.at[pl.ds(off, block_slabs)], zero_sem)

    @pl.when(pl.program_id(0) == 0)
    def _():
        zero_ref[...] = jnp.zeros(zero_ref.shape, F32)

        def start(b, carry):
            @pl.when(fill_ref[b] != 0)
            def _():
                fill_copy(b).start()
            return carry

        def wait(b, carry):
            @pl.when(fill_ref[b] != 0)
            def _():
                fill_copy(b).wait()
            return carry

        lax.fori_loop(0, n_blocks, start, 0)
        lax.fori_loop(0, n_blocks, wait, 0)

    def issue(i, carry):
        for j in range(ISSUE_UNROLL):
            r = i * ISSUE_UNROLL + j
            for k in range(TOP_K):
                d = dest_ref[r * TOP_K + k]
                pltpu.make_async_copy(_slab(x_ref, r), _slab(out_hbm, d), sem).start(priority=k % 2)
        return carry

    lax.fori_loop(0, ts // ISSUE_UNROLL, issue, 0)
    for k in range(TOP_K):
        pltpu.make_async_copy(x_ref, out_hbm.at[pl.ds(0, ts * SUBLANES)], sem).wait()


def _dispatch(xn_slabs, fill, dest_flat, n_rows):
    T = xn_slabs.shape[0] // SUBLANES
    ts = min(TS_DISP, T)
    grid_spec = pltpu.PrefetchScalarGridSpec(
        num_scalar_prefetch=1,
        grid=(T // ts,),
        in_specs=[
            pl.BlockSpec((ts * TOP_K,), lambda i, fl: (i,), memory_space=pltpu.SMEM),
            pl.BlockSpec((ts * SUBLANES, LANES), lambda i, fl: (i, 0)),
        ],
        out_specs=pl.BlockSpec(memory_space=pl.ANY),
        scratch_shapes=[pltpu.VMEM((ROW_BLOCK * SUBLANES, LANES), F32),
                        pltpu.SemaphoreType.DMA, pltpu.SemaphoreType.DMA],
    )
    return pl.pallas_call(
        functools.partial(_dispatch_kernel, ts=ts, n_blocks=n_rows // ROW_BLOCK),
        grid_spec=grid_spec,
        out_shape=jax.ShapeDtypeStruct((n_rows * SUBLANES, LANES), xn_slabs.dtype),
        compiler_params=_cparams(("arbitrary",)),
        name="dispatch",
    )(fill, dest_flat, xn_slabs)


def _expert_kernel(be_ref, na_ref, nxt_ref, x_ref, wg_hbm, bg_ref, wu_hbm, bu_ref, wd_hbm, bd_ref,
                   y_ref, stage_ref, wb_ref, slot_ref, sems):
    i = pl.program_id(0)
    e = be_ref[i]
    w_hbm = (wg_hbm, wu_hbm, wd_hbm)

    def fetch(expert, slot, m):
        return pltpu.make_async_copy(w_hbm[m].at[expert], stage_ref.at[slot, m], sems.at[slot, m])

    @pl.when(i == 0)
    def _():
        slot_ref[0] = 0
        for m in range(3):
            fetch(e, 0, m).start()

    active = i < na_ref[0]
    first = jnp.logical_or(i == 0, e != be_ref[jnp.maximum(i - 1, 0)])

    @pl.when(jnp.logical_and(active, first))
    def _():
        slot = slot_ref[0]
        for m in range(3):
            fetch(e, slot, m).wait()
            wb_ref[m] = stage_ref[slot, m].astype(BF16)

        @pl.when(nxt_ref[i] >= 0)
        def _():
            for m in range(3):
                fetch(nxt_ref[i], 1 - slot, m).start()

        slot_ref[0] = 1 - slot

    @pl.when(active)
    def _():
        x = _slabs_to_rows(x_ref, ROW_BLOCK).astype(BF16)
        g = jnp.dot(x, wb_ref[0], preferred_element_type=F32) + bg_ref[0]
        u = jnp.dot(x, wb_ref[1], preferred_element_type=F32) + bu_ref[0]
        g = jnp.minimum(g, SWIGLU_LIMIT)
        u = jnp.clip(u, -SWIGLU_LIMIT, SWIGLU_LIMIT)
        glu = g * jax.nn.sigmoid(SWIGLU_ALPHA * g)
        y = jnp.dot(((u + 1.0) * glu).astype(BF16), wb_ref[2], preferred_element_type=F32) + bd_ref[0]
        _rows_to_slabs(y_ref, y)


def _experts(x_rows, block_e, n_active, next_e, w_gate, b_gate, w_up, b_up, w_down, b_down):
    E, D, FF = w_gate.shape
    assert D == FF, "the three expert matrices share one staging shape"
    block_slabs = ROW_BLOCK * SUBLANES
    n_blocks = x_rows.shape[0] // block_slabs

    def row_map(i, be, na, nx):
        return (jnp.minimum(i, na[0] - 1), 0)

    def b_map(i, be, na, nx):
        return (be[jnp.minimum(i, na[0] - 1)], 0, 0)

    grid_spec = pltpu.PrefetchScalarGridSpec(
        num_scalar_prefetch=3,
        grid=(n_blocks,),
        in_specs=[
            pl.BlockSpec((block_slabs, LANES), row_map),
            pl.BlockSpec(memory_space=pl.ANY),
            pl.BlockSpec((1, 1, FF), b_map),
            pl.BlockSpec(memory_space=pl.ANY),
            pl.BlockSpec((1, 1, FF), b_map),
            pl.BlockSpec(memory_space=pl.ANY),
            pl.BlockSpec((1, 1, D), b_map),
        ],
        out_specs=pl.BlockSpec((block_slabs, LANES), row_map),
        scratch_shapes=[
            pltpu.VMEM((2, 3, D, FF), F32),
            pltpu.VMEM((3, D, FF), BF16),
            pltpu.SMEM((1,), jnp.int32),
            pltpu.SemaphoreType.DMA((2, 3)),
        ],
    )
    return pl.pallas_call(
        _expert_kernel,
        grid_spec=grid_spec,
        out_shape=jax.ShapeDtypeStruct(x_rows.shape, F32),
        input_output_aliases={3: 0},
        compiler_params=pltpu.CompilerParams(dimension_semantics=("arbitrary",),
                                             vmem_limit_bytes=EXPERT_VMEM_LIMIT),
        name="experts",
    )(block_e, n_active, next_e, x_rows, w_gate, b_gate.reshape(E, 1, FF), w_up,
      b_up.reshape(E, 1, FF), w_down, b_down.reshape(E, 1, D))


def _combine_kernel(dest_ref, dest_next_ref, y_hbm, x1_ref, gates_ref, fg_ref, o_ref, bufs, sems,
                    *, ts, n_steps):
    i = pl.program_id(0)
    slot = i % 2

    def gather_tile(d_ref, s):
        def issue(it, carry):
            for j in range(ISSUE_UNROLL):
                r = it * ISSUE_UNROLL + j
                for k in range(TOP_K):
                    d = d_ref[r * TOP_K + k]
                    pltpu.make_async_copy(_slab(y_hbm, d), _slab(bufs.at[s, k], r),
                                          sems.at[s]).start(priority=k % 2)
            return carry

        lax.fori_loop(0, ts // ISSUE_UNROLL, issue, 0)

    @pl.when(i == 0)
    def _():
        gather_tile(dest_ref, 0)

    @pl.when(i + 1 < n_steps)
    def _():
        gather_tile(dest_next_ref, 1 - slot)

    for k in range(TOP_K):
        pltpu.make_async_copy(y_hbm.at[pl.ds(0, ts * SUBLANES)], bufs.at[slot, k], sems.at[slot]).wait()

    acc = x1_ref[...]
    gates = gates_ref[...]
    for k in range(TOP_K):
        acc = acc + _slabs_to_rows(bufs.at[slot, k], ts) * gates[:, k:k + 1]
    ms = jnp.mean(acc * acc, axis=-1, keepdims=True)
    o_ref[...] = acc * lax.rsqrt(ms + NORM_EPS) * fg_ref[...]


def _combine(y_rows, dest_flat, x1, gates, final_g):
    T, D = x1.shape
    ts = min(TS_COMB, T)
    n_steps = T // ts
    tok = lambda i: (i, 0)
    return pl.pallas_call(
        functools.partial(_combine_kernel, ts=ts, n_steps=n_steps),
        grid=(n_steps,),
        in_specs=[
            pl.BlockSpec((ts * TOP_K,), lambda i: (i,), memory_space=pltpu.SMEM),
            pl.BlockSpec((ts * TOP_K,), lambda i: (jnp.minimum(i + 1, n_steps - 1),),
                         memory_space=pltpu.SMEM),
            pl.BlockSpec(memory_space=pl.ANY),
            pl.BlockSpec((ts, D), tok),
            pl.BlockSpec((ts, LANES), tok),
            pl.BlockSpec((1, D), lambda i: (0, 0)),
        ],
        out_specs=pl.BlockSpec((ts, D), tok),
        out_shape=jax.ShapeDtypeStruct((T, D), F32),
        scratch_shapes=[pltpu.VMEM((2, TOP_K, ts * SUBLANES, LANES), F32),
                        pltpu.SemaphoreType.DMA((2,))],
        compiler_params=_cparams(("arbitrary",)),
        name="combine",
    )(dest_flat, dest_flat, y_rows, x1, gates, final_g.reshape(1, D))


def kernel(x, norm1_g, w_in, q_norm_g, k_norm_g, conv_w, conv_b, lru_wa, lru_ba, lru_wi, lru_bi,
           lru_lam, attn_out_g, lru_out_g, w_out, norm2_g, w_router, b_router, w_gate, b_gate,
           w_up, b_up, w_down, b_down, final_g):
    B, S, D = x.shape
    T = B * S
    assert w_in.shape[0] == 1, "single-layer trunk: the final norm is fused into the layer's combine"
    x2 = x.reshape(T, D)
    for l in range(1):
        qt, k, vt, lru_x, lru_gate = _inproj(x2, norm1_g[l], w_in[l], q_norm_g[l], k_norm_g[l], S)
        score_bound = (HEAD_DIM * Q_SCALE * jnp.max(jnp.abs(q_norm_g[l]))
                       * jnp.max(jnp.abs(k_norm_g[l])))
        attn = _attention(qt, k.reshape(B, S, -1), vt, score_bound, B, S)
        lru = _lru(lru_x.reshape(B, S, -1), lru_gate.reshape(B, S, -1), conv_w[l], conv_b[l],
                   lru_wa[l], lru_ba[l], lru_wi[l], lru_bi[l], lru_lam[l], B, S)
        x1, xn3, route, gates, cnt = _outproj_router(
            attn.reshape(T, -1), lru.reshape(T, -1), x2, attn_out_g[l], lru_out_g[l], w_out[l],
            norm2_g[l], w_router[l], b_router[l])

        counts = cnt[0, :N_EXPERTS].astype(jnp.int32)
        padded = ((counts + ROW_BLOCK - 1) // ROW_BLOCK) * ROW_BLOCK
        pend = jnp.cumsum(padded)
        pstart = (pend - padded).astype(jnp.int32)
        n_rows = T * TOP_K + N_EXPERTS * ROW_BLOCK
        block_start = jnp.arange(n_rows // ROW_BLOCK, dtype=jnp.int32) * ROW_BLOCK
        block_e = jnp.sum((pend[None, :] <= block_start[:, None]).astype(jnp.int32), axis=1)
        block_e = jnp.minimum(block_e, N_EXPERTS - 1)
        n_active = (pend[-1:] // ROW_BLOCK).astype(jnp.int32)
        fill = jnp.logical_or(block_start + ROW_BLOCK == pend[block_e],
                              block_start >= pend[-1]).astype(jnp.int32)

        next_block = pend[block_e] // ROW_BLOCK
        next_e = jnp.where(next_block < n_active[0],
                           block_e[jnp.minimum(next_block, block_e.shape[0] - 1)], -1).astype(jnp.int32)

        dest_flat = _dest_rows(route, pstart)
        x_rows = _dispatch(xn3, fill, dest_flat, n_rows)
        y_rows = _experts(x_rows, block_e, n_active, next_e, w_gate[l], b_gate[l], w_up[l], b_up[l],
                          w_down[l], b_down[l])
        x2 = _combine(y_rows, dest_flat, x1, gates, final_g)
    return x2.reshape(B, S, D)
```

```python
import functools
import math

import jax
import jax.numpy as jnp
from jax import lax
from jax.experimental import pallas as pl
from jax.experimental.pallas import tpu as pltpu

F32 = jnp.float32
BF16 = jnp.bfloat16

GRID_W = 64
HEAD_DIM = 64
N_Q_HEADS = 8
N_KV_HEADS = 2
GQA_GROUP = N_Q_HEADS // N_KV_HEADS
ATTN_W = N_Q_HEADS * HEAD_DIM
KV_W = N_KV_HEADS * HEAD_DIM
LRU_BLOCKS = 8
LRU_C = 8.0
CONV_W = 4
CONV_PAD_L = 2
ROPE_THETA = 10000.0
ROPE_HALF = HEAD_DIM // 2
ROPE_M = ROPE_HALF // 2
N_EXPERTS = 32
TOP_K = 4
SWIGLU_ALPHA = 1.702
SWIGLU_LIMIT = 7.0
NORM_EPS = 1e-5
QK_EPS = 1e-6
LOG2_E = 1.4426950408889634
Q_SCALE = HEAD_DIM ** -0.5 * LOG2_E
SAFE_SCORE_LOG2 = 96.0

LANES = 128
SUBLANES = 8
BF16_SUBLANES = 16
PV_ROWS = HEAD_DIM + BF16_SUBLANES
VMEM_LIMIT = 48 * 1024 * 1024
EXPERT_VMEM_LIMIT = 56 * 1024 * 1024

TS_IN = 512
TQ = 256
TK = 256
KV_UNROLL = 8
HEADS_PER_STEP = 2
TC_LRU = 512
TS_OUT = 512
ROW_BLOCK = 256
TS_DISP = 512
TS_COMB = 256
ISSUE_UNROLL = 8


def _cparams(sem):
    return pltpu.CompilerParams(dimension_semantics=sem, vmem_limit_bytes=VMEM_LIMIT)


def _inproj_kernel(x_ref, g1_ref, w_ref, qg_ref, kg_ref, cos_ref, sin_ref,
                   q_ref, k_ref, v_ref, lx_ref, lg_ref, *, lru_w):
    x = x_ref[...]
    ms = jnp.mean(x * x, axis=-1, keepdims=True)
    xn = x * lax.rsqrt(ms + NORM_EPS) * g1_ref[...]
    h = jnp.dot(xn.astype(BF16), w_ref[...], preferred_element_type=F32)

    cos = cos_ref[...]
    sin = sin_ref[...]
    lane = lax.broadcasted_iota(jnp.int32, cos.shape, 1)
    first_half = (lane % ROPE_HALF) < ROPE_M

    def head_norm_rope(xc, g, scale):
        hms = jnp.sum(xc * xc, axis=-1, keepdims=True) * (1.0 / HEAD_DIM)
        xc = xc * lax.rsqrt(hms + QK_EPS) * g
        partner = jnp.where(first_half,
                            pltpu.roll(xc, LANES - ROPE_M, 1),
                            pltpu.roll(xc, ROPE_M, 1))
        return (xc * cos + partner * sin) * scale

    qw = N_Q_HEADS * LANES
    kw = N_KV_HEADS * LANES
    for c in range(N_Q_HEADS):
        sl = slice(c * LANES, (c + 1) * LANES)
        q_ref[0, sl, :] = head_norm_rope(h[:, sl], qg_ref[...], Q_SCALE).T.astype(BF16)
    for c in range(N_KV_HEADS):
        sl = slice(c * LANES, (c + 1) * LANES)
        k_ref[:, sl] = head_norm_rope(h[:, qw + c * LANES: qw + (c + 1) * LANES],
                                      kg_ref[...], 1.0).astype(BF16)
        vc = h[:, qw + kw + c * LANES: qw + kw + (c + 1) * LANES]
        v_ref[0, sl, :] = jnp.where(lane >= HEAD_DIM, 1.0, vc).T.astype(BF16)
    o = qw + 2 * kw
    lx_ref[...] = h[:, o: o + lru_w]
    lg_ref[...] = h[:, o + lru_w: o + 2 * lru_w]


def _pad_heads(w, n_heads):
    lead = w.shape[:-1]
    w = w.reshape(lead + (n_heads, HEAD_DIM))
    w = jnp.pad(w, [(0, 0)] * len(lead) + [(0, 0), (0, LANES - HEAD_DIM)])
    return w.reshape(lead + (n_heads * LANES,))


def _rope_tables(S):
    t = jnp.arange(S)
    rows = (t // GRID_W).astype(F32)
    cols = (t % GRID_W).astype(F32)
    inv_freq = ROPE_THETA ** (-jnp.arange(ROPE_M, dtype=F32) / ROPE_M)
    ar = rows[:, None] * inv_freq[None, :]
    ac = cols[:, None] * inv_freq[None, :]
    cos = jnp.concatenate([jnp.cos(ar), jnp.cos(ar), jnp.cos(ac), jnp.cos(ac)], axis=-1)
    sin = jnp.concatenate([-jnp.sin(ar), jnp.sin(ar), -jnp.sin(ac), jnp.sin(ac)], axis=-1)
    pad = [(0, 0), (0, LANES - HEAD_DIM)]
    return jnp.pad(cos, pad), jnp.pad(sin, pad)


def _inproj(x2, norm1_g, w_in, q_norm_g, k_norm_g, S):
    T, D = x2.shape
    lru_w = (w_in.shape[1] - ATTN_W - 2 * KV_W) // 2
    o0, o1, o2 = ATTN_W, ATTN_W + KV_W, ATTN_W + 2 * KV_W
    w_all = jnp.concatenate([
        _pad_heads(w_in[:, :o0], N_Q_HEADS),
        _pad_heads(w_in[:, o0:o1], N_KV_HEADS),
        _pad_heads(w_in[:, o1:o2], N_KV_HEADS),
        w_in[:, o2:],
    ], axis=1).astype(BF16)
    qg = _pad_heads(q_norm_g.reshape(1, HEAD_DIM), 1)
    kg = _pad_heads(k_norm_g.reshape(1, HEAD_DIM), 1)
    cos, sin = _rope_tables(S)
    ts = TS_IN
    n_s = S // ts
    qw, kw = N_Q_HEADS * LANES, N_KV_HEADS * LANES
    const = lambda i: (0, 0)
    tok = lambda i: (i, 0)
    pos = lambda i: (i % n_s, 0)
    tposed = lambda i: (i // n_s, 0, i % n_s)
    return pl.pallas_call(
        functools.partial(_inproj_kernel, lru_w=lru_w),
        grid=(T // ts,),
        in_specs=[
            pl.BlockSpec((ts, D), tok),
            pl.BlockSpec((1, D), const),
            pl.BlockSpec(w_all.shape, const),
            pl.BlockSpec((1, LANES), const),
            pl.BlockSpec((1, LANES), const),
            pl.BlockSpec((ts, LANES), pos),
            pl.BlockSpec((ts, LANES), pos),
        ],
        out_specs=[
            pl.BlockSpec((1, qw, ts), tposed),
            pl.BlockSpec((ts, kw), tok),
            pl.BlockSpec((1, kw, ts), tposed),
            pl.BlockSpec((ts, lru_w), tok),
            pl.BlockSpec((ts, lru_w), tok),
        ],
        out_shape=[
            jax.ShapeDtypeStruct((T // S, qw, S), BF16),
            jax.ShapeDtypeStruct((T, kw), BF16),
            jax.ShapeDtypeStruct((T // S, kw, S), BF16),
            jax.ShapeDtypeStruct((T, lru_w), F32),
            jax.ShapeDtypeStruct((T, lru_w), F32),
        ],
        compiler_params=_cparams(("parallel",)),
        name="inproj",
    )(x2, norm1_g.reshape(1, D), w_all, qg, kg, cos, sin)


def _attn_kernel(qt_ref, k_ref, vt_ref, o_ref, acc_ref, s_ref, p_ref, *, tq, tk, n_kv, kv_unroll):
    hp = HEADS_PER_STEP
    spt = GQA_GROUP // hp
    acc_ref[...] = jnp.zeros(acc_ref.shape, F32)

    def scores(j, sp):
        kt = k_ref[0, pl.ds(pl.multiple_of(j * tk, tk), tk), :]
        out = []
        for u in range(hp):
            g = sp * hp + u
            s = jnp.dot(kt, qt_ref[0, g * LANES:(g + 1) * LANES, :], preferred_element_type=F32)
            out.append((s, jnp.max(s, axis=0, keepdims=True)))
        return out

    def softmax_stage(sc, ms, sp):
        out = []
        for u, (s, s_max) in enumerate(sc):
            h = sp * hp + u
            m_new = jnp.maximum(ms[h], s_max)
            out.append((jnp.exp2(ms[h] - m_new), jnp.exp2(s - m_new).astype(BF16)))
            ms[h] = m_new
        return out

    def pv_stage(j, sp, ap):
        vt = vt_ref[0, 0:PV_ROWS, pl.ds(pl.multiple_of(j * tk, tk), tk)]
        for u, (alpha, p) in enumerate(ap):
            g = sp * hp + u
            acc_ref[g] = alpha * acc_ref[g] + jnp.dot(vt, p, preferred_element_type=F32)

    ms = [jnp.full((1, tq), -jnp.inf, F32)] * GQA_GROUP
    ap = softmax_stage(scores(0, 0), ms, 0)
    sc = scores(min(1 // spt, n_kv - 1), 1 % spt)
    for u in range(hp):
        s_ref[u] = sc[u][0]
        p_ref[u] = ap[u][1]

    def body(it, carry):
        ms = list(carry[:GQA_GROUP])
        ap = [(carry[GQA_GROUP + u], p_ref[u]) for u in range(hp)]
        sc = [(s_ref[u], carry[GQA_GROUP + hp + u]) for u in range(hp)]
        for n in range(kv_unroll * spt):
            j = it * kv_unroll + n // spt
            j_next = jnp.minimum(it * kv_unroll + (n + 2) // spt, n_kv - 1)
            sc_next = scores(j_next, (n + 2) % spt)
            ap_next = softmax_stage(sc, ms, (n + 1) % spt)
            pv_stage(j, n % spt, ap)
            sc, ap = sc_next, ap_next
        for u in range(hp):
            s_ref[u] = sc[u][0]
            p_ref[u] = ap[u][1]
        return tuple(ms) + tuple(a for a, _ in ap) + tuple(m for _, m in sc)

    lax.fori_loop(0, n_kv // kv_unroll, body,
                  tuple(ms) + tuple(a for a, _ in ap) + tuple(m for _, m in sc))
    _attn_finalize(acc_ref, o_ref, tq)


def _attn_finalize(acc_ref, o_ref, tq):
    pad = jnp.zeros((LANES - HEAD_DIM, tq), F32)
    for g in range(GQA_GROUP):
        acc = acc_ref[g]
        o = acc[0:HEAD_DIM] / acc[HEAD_DIM:HEAD_DIM + 1, :]
        o_ref[0, :, g * LANES:(g + 1) * LANES] = jnp.concatenate([o, pad], axis=0).T.astype(BF16)


def _attn_bounded_kernel(qt_ref, k_ref, vt_ref, o_ref, acc_ref, s_ref, p_ref, *, tq, tk, n_kv, kv_unroll):
    hp = HEADS_PER_STEP
    spt = GQA_GROUP // hp
    acc_ref[...] = jnp.zeros(acc_ref.shape, F32)

    def scores(j, sp):
        kt = k_ref[0, pl.ds(pl.multiple_of(j * tk, tk), tk), :]
        return [jnp.dot(kt, qt_ref[0, (sp * hp + u) * LANES:(sp * hp + u + 1) * LANES, :],
                        preferred_element_type=F32) for u in range(hp)]

    def probs(sc):
        return [jnp.exp2(s).astype(BF16) for s in sc]

    def pv_stage(j, sp, ps):
        vt = vt_ref[0, 0:PV_ROWS, pl.ds(pl.multiple_of(j * tk, tk), tk)]
        for u, p in enumerate(ps):
            acc_ref[sp * hp + u] += jnp.dot(vt, p, preferred_element_type=F32)

    ps = probs(scores(0, 0))
    sc = scores(min(1 // spt, n_kv - 1), 1 % spt)
    for u in range(hp):
        s_ref[u] = sc[u]
        p_ref[u] = ps[u]

    def body(it, carry):
        ps = [p_ref[u] for u in range(hp)]
        sc = [s_ref[u] for u in range(hp)]
        for n in range(kv_unroll * spt):
            j = it * kv_unroll + n // spt
            j_next = jnp.minimum(it * kv_unroll + (n + 2) // spt, n_kv - 1)
            sc_next = scores(j_next, (n + 2) % spt)
            ps_next = probs(sc)
            pv_stage(j, n % spt, ps)
            sc, ps = sc_next, ps_next
        for u in range(hp):
            s_ref[u] = sc[u]
            p_ref[u] = ps[u]
        return carry

    lax.fori_loop(0, n_kv // kv_unroll, body, 0)
    _attn_finalize(acc_ref, o_ref, tq)


def _attention(qt, k, vt, score_bound, B, S):
    tq = min(TQ, S)
    tk = min(TK, S)
    gw = GQA_GROUP * LANES

    def call(body, name):
        return pl.pallas_call(
            functools.partial(body, tq=tq, tk=tk, n_kv=S // tk,
                              kv_unroll=math.gcd(S // tk, KV_UNROLL)),
            grid=(B, N_KV_HEADS, S // tq),
            in_specs=[
                pl.BlockSpec((1, gw, tq), lambda b, h, i: (b, h, i)),
                pl.BlockSpec((1, S, LANES), lambda b, h, i: (b, 0, h)),
                pl.BlockSpec((1, LANES, S), lambda b, h, i: (b, h, 0)),
            ],
            out_specs=pl.BlockSpec((1, tq, gw), lambda b, h, i: (b, i, h)),
            out_shape=jax.ShapeDtypeStruct((B, S, N_Q_HEADS * LANES), BF16),
            scratch_shapes=[pltpu.VMEM((GQA_GROUP, PV_ROWS, tq), F32),
                            pltpu.VMEM((HEADS_PER_STEP, tk, tq), F32),
                            pltpu.VMEM((HEADS_PER_STEP, tk, tq), BF16)],
            compiler_params=_cparams(("parallel", "parallel", "parallel")),
            name=name,
        )

    return lax.cond(score_bound <= SAFE_SCORE_LOG2,
                    call(_attn_bounded_kernel, "attention_bounded"),
                    call(_attn_kernel, "attention"), qt, k, vt)


def _scan_chunk(a, b, h_in, reverse):
    n = a.shape[0]
    n_groups = n // SUBLANES
    a = a.reshape(n_groups, SUBLANES, LANES)
    b = b.reshape(n_groups, SUBLANES, LANES)
    sub = lax.broadcasted_iota(jnp.int32, a.shape, 1)
    d = 1
    while d < SUBLANES:
        if reverse:
            keep = sub < SUBLANES - d
            shift = SUBLANES - d
        else:
            keep = sub >= d
            shift = d
        a_sh = jnp.where(keep, pltpu.roll(a, shift, 1), 1.0)
        b_sh = jnp.where(keep, pltpu.roll(b, shift, 1), 0.0)
        b = a * b_sh + b
        a = a * a_sh
        d *= 2
    a = a.reshape(n, LANES)
    b = b.reshape(n, LANES)
    order = range(n_groups - 1, -1, -1) if reverse else range(n_groups)
    edge = h_in
    out = [None] * n_groups
    for v in order:
        rows = slice(v * SUBLANES, (v + 1) * SUBLANES)
        hv = b[rows] + a[rows] * jnp.broadcast_to(edge, (SUBLANES, LANES))
        out[v] = hv
        edge = hv[0:1] if reverse else hv[SUBLANES - 1:SUBLANES]
    return jnp.concatenate(out, axis=0), edge


def _lru_kernel(u_ref, gate_ref, cw_ref, cb_ref, w_ref, bias_ref, lam_ref, o_ref,
                up_ref, hf_ref, *, S, tc):
    halo = SUBLANES
    zeros = jnp.zeros((halo, LANES), F32)
    up_ref[0:halo, :] = zeros
    up_ref[S + halo:S + 2 * halo, :] = zeros
    up_ref[halo:S + halo, :] = u_ref[0]
    sp = jax.nn.softplus(-lam_ref[...])
    cw = cw_ref[...]
    cb = cb_ref[...]
    n_chunks = S // tc
    ext = tc + 2 * halo

    def gates(c, d):
        t0 = pl.multiple_of(c * tc, tc)
        ue = up_ref[pl.ds(t0, ext), :]
        xc = cb
        for j in range(CONV_W):
            sh = (CONV_PAD_L - j) % ext
            uj = ue if sh == 0 else pltpu.roll(ue, sh, 0)
            xc = xc + uj[halo:halo + tc] * cw[j:j + 1, :]
        gw = 2 * LANES
        g = jnp.dot(xc.astype(BF16), w_ref[0, :, d * gw:(d + 1) * gw],
                    preferred_element_type=F32) + bias_ref[0, :, d * gw:(d + 1) * gw]
        r = jax.nn.sigmoid(g[:, :LANES])
        i = jax.nn.sigmoid(g[:, LANES:])
        log_a = -LRU_C * r * sp[d:d + 1, :]
        a = jnp.exp(log_a)
        y = 1.0 - a * a
        b = jnp.where(y > 0.0, y * lax.rsqrt(y), 0.0) * i * xc
        return t0, a, b

    def fwd(c, h):
        t0, a, b = gates(c, 0)
        hc, h_last = _scan_chunk(a, b, h, False)
        hf_ref[pl.ds(t0, tc), :] = hc
        return h_last

    lax.fori_loop(0, n_chunks, fwd, jnp.zeros((1, LANES), F32))

    def bwd(ci, h):
        t0, a, b = gates(n_chunks - 1 - ci, 1)
        hc, h_last = _scan_chunk(a, b, h, True)
        gate = gate_ref[0, pl.ds(t0, tc), :]
        o_ref[0, pl.ds(t0, tc), :] = (hf_ref[pl.ds(t0, tc), :] + hc) * jax.nn.gelu(gate)
        return h_last

    lax.fori_loop(0, n_chunks, bwd, jnp.zeros((1, LANES), F32))


def _block_diag_pairs(w):
    nb, bw, _ = w.shape
    w = w.reshape(nb // 2, 2, bw, bw)
    z = jnp.zeros_like(w[:, 0])
    top = jnp.concatenate([w[:, 0], z], axis=-1)
    bot = jnp.concatenate([z, w[:, 1]], axis=-1)
    return jnp.concatenate([top, bot], axis=-2)


def _lru(lru_x, lru_gate, conv_w, conv_b, wa, ba, wi, bi, lam, B, S):
    C = lru_x.shape[-1]
    nc = C // LANES
    tc = min(TC_LRU, S)
    w = jnp.concatenate([_block_diag_pairs(wa[0]), _block_diag_pairs(wi[0]),
                         _block_diag_pairs(wa[1]), _block_diag_pairs(wi[1])], axis=-1).astype(BF16)
    bias = jnp.stack([ba[0].reshape(nc, LANES), bi[0].reshape(nc, LANES),
                      ba[1].reshape(nc, LANES), bi[1].reshape(nc, LANES)], axis=1)
    bias = bias.reshape(nc, 1, 4 * LANES)
    blk = lambda b, c: (b, 0, c)
    return pl.pallas_call(
        functools.partial(_lru_kernel, S=S, tc=tc),
        grid=(B, nc),
        in_specs=[
            pl.BlockSpec((1, S, LANES), blk),
            pl.BlockSpec((1, S, LANES), blk),
            pl.BlockSpec((CONV_W, LANES), lambda b, c: (0, c)),
            pl.BlockSpec((1, LANES), lambda b, c: (0, c)),
            pl.BlockSpec((1, LANES, 4 * LANES), lambda b, c: (c, 0, 0)),
            pl.BlockSpec((1, 1, 4 * LANES), lambda b, c: (c, 0, 0)),
            pl.BlockSpec((2, LANES), lambda b, c: (0, c)),
        ],
        out_specs=pl.BlockSpec((1, S, LANES), blk),
        out_shape=jax.ShapeDtypeStruct((B, S, C), F32),
        scratch_shapes=[
            pltpu.VMEM((S + 2 * SUBLANES, LANES), F32),
            pltpu.VMEM((S, LANES), F32),
        ],
        compiler_params=_cparams(("parallel", "parallel")),
        name="rglru",
    )(lru_x, lru_gate, conv_w, conv_b.reshape(1, C), w, bias, lam)


def _rows_to_slabs(ref, x):
    n = x.shape[0]
    for s in range(SUBLANES):
        ref[pl.ds(s, n, stride=SUBLANES), :] = x[:, s * LANES:(s + 1) * LANES]


def _slabs_to_rows(ref, n):
    return jnp.concatenate([ref[pl.ds(s, n, stride=SUBLANES), :] for s in range(SUBLANES)], axis=1)


def _slab(ref, r):
    return ref.at[pl.ds(pl.multiple_of(r * SUBLANES, SUBLANES), SUBLANES)]


def _outproj_kernel(a_ref, l_ref, x_ref, ag_ref, lg_ref, wa_ref, wl_ref, g2_ref,
                    wrh_ref, wrl_ref, br_ref, tri_ref,
                    x1_ref, xn3_ref, route_ref, gates_ref, cnt_ref, carry_ref, *, attn_w, lru_w):
    step = pl.program_id(0)

    @pl.when(step == 0)
    def _():
        carry_ref[...] = jnp.zeros_like(carry_ref)

    a = a_ref[...].astype(F32)
    ams = jnp.sum(a * a, axis=-1, keepdims=True) * (1.0 / attn_w)
    an = a * lax.rsqrt(ams + NORM_EPS) * ag_ref[...]
    l = l_ref[...]
    lms = jnp.sum(l * l, axis=-1, keepdims=True) * (1.0 / lru_w)
    ln = l * lax.rsqrt(lms + NORM_EPS) * lg_ref[...]
    mix = (jnp.dot(an.astype(BF16), wa_ref[...], preferred_element_type=F32)
           + jnp.dot(ln.astype(BF16), wl_ref[...], preferred_element_type=F32))
    x1 = x_ref[...] + mix
    x1_ref[...] = x1
    ms = jnp.mean(x1 * x1, axis=-1, keepdims=True)
    xn = x1 * lax.rsqrt(ms + NORM_EPS) * g2_ref[...]
    _rows_to_slabs(xn3_ref, xn)

    hi = xn.astype(BF16)
    lo = (xn - hi.astype(F32)).astype(BF16)
    logits = (jnp.dot(hi, wrh_ref[...], preferred_element_type=F32)
              + jnp.dot(lo, wrh_ref[...], preferred_element_type=F32)
              + jnp.dot(hi, wrl_ref[...], preferred_element_type=F32)) + br_ref[...]
    lane = lax.broadcasted_iota(jnp.int32, logits.shape, 1)
    neg = -jnp.inf
    work = jnp.where(lane < N_EXPERTS, logits, neg)
    sel = jnp.zeros(logits.shape, F32)
    idxs, vals = [], []
    for _ in range(TOP_K):
        m = jnp.max(work, axis=1, keepdims=True)
        idx = jnp.min(jnp.where(work == m, lane, LANES), axis=1, keepdims=True)
        hit = lane == idx
        work = jnp.where(hit, neg, work)
        sel = sel + hit.astype(F32)
        idxs.append(idx)
        vals.append(m)
    es = [jnp.exp(v - vals[0]) for v in vals]
    den = es[0] + es[1] + es[2] + es[3]

    prefix = jnp.dot(tri_ref[...], sel.astype(BF16), preferred_element_type=F32) + carry_ref[...]
    carry_ref[...] = carry_ref[...] + jnp.sum(sel, axis=0, keepdims=True)
    cnt_ref[...] = carry_ref[...]

    route = jnp.zeros(logits.shape, jnp.int32)
    gates = jnp.zeros(logits.shape, F32)
    for k in range(TOP_K):
        rank = jnp.sum(jnp.where(lane == idxs[k], prefix, 0.0), axis=1, keepdims=True).astype(jnp.int32)
        route = jnp.where(lane == k, idxs[k], route)
        route = jnp.where(lane == TOP_K + k, rank, route)
        gates = jnp.where(lane == k, es[k] / den, gates)
    route_ref[...] = route
    gates_ref[...] = gates


def _outproj_router(attn, lru, x2, attn_out_g, lru_out_g, w_out, norm2_g, w_router, b_router):
    T, D = x2.shape
    lru_w = lru.shape[-1]
    ts = min(TS_OUT, T)
    wa = w_out[:ATTN_W].reshape(N_Q_HEADS, HEAD_DIM, D)
    wa = jnp.pad(wa, ((0, 0), (0, LANES - HEAD_DIM), (0, 0))).reshape(N_Q_HEADS * LANES, D).astype(BF16)
    wl = w_out[ATTN_W:].astype(BF16)
    ag = _pad_heads(attn_out_g.reshape(1, ATTN_W), N_Q_HEADS)
    wr = jnp.pad(w_router, ((0, 0), (0, LANES - N_EXPERTS)))
    wrh = wr.astype(BF16)
    wrl = (wr - wrh.astype(F32)).astype(BF16)
    br = jnp.pad(b_router.reshape(1, N_EXPERTS), ((0, 0), (0, LANES - N_EXPERTS)))
    tri = (jnp.arange(ts)[:, None] > jnp.arange(ts)[None, :]).astype(BF16)
    const = lambda i: (0, 0)
    tok = lambda i: (i, 0)
    aw = N_Q_HEADS * LANES
    return pl.pallas_call(
        functools.partial(_outproj_kernel, attn_w=ATTN_W, lru_w=lru_w),
        grid=(T // ts,),
        in_specs=[
            pl.BlockSpec((ts, aw), tok),
            pl.BlockSpec((ts, lru_w), tok),
            pl.BlockSpec((ts, D), tok),
            pl.BlockSpec((1, aw), const),
            pl.BlockSpec((1, lru_w), const),
            pl.BlockSpec((aw, D), const),
            pl.BlockSpec((lru_w, D), const),
            pl.BlockSpec((1, D), const),
            pl.BlockSpec((D, LANES), const),
            pl.BlockSpec((D, LANES), const),
            pl.BlockSpec((1, LANES), const),
            pl.BlockSpec((ts, ts), const),
        ],
        out_specs=[
            pl.BlockSpec((ts, D), tok),
            pl.BlockSpec((ts * SUBLANES, LANES), tok),
            pl.BlockSpec((ts, LANES), tok),
            pl.BlockSpec((ts, LANES), tok),
            pl.BlockSpec((1, LANES), const),
        ],
        out_shape=[
            jax.ShapeDtypeStruct((T, D), F32),
            jax.ShapeDtypeStruct((T * SUBLANES, LANES), F32),
            jax.ShapeDtypeStruct((T, LANES), jnp.int32),
            jax.ShapeDtypeStruct((T, LANES), F32),
            jax.ShapeDtypeStruct((1, LANES), F32),
        ],
        scratch_shapes=[pltpu.VMEM((1, LANES), F32)],
        compiler_params=_cparams(("arbitrary",)),
        name="outproj_router",
    )(attn, lru, x2, ag, lru_out_g.reshape(1, lru_w), wa, wl, norm2_g.reshape(1, D),
      wrh, wrl, br, tri)


def _dest_kernel(route_ref, pstart_ref, dest_ref):
    route = route_ref[...]
    lane = lax.broadcasted_iota(jnp.int32, route.shape, 1)
    pstart = pstart_ref[...]
    dest = jnp.zeros(route.shape, jnp.int32)
    for k in range(TOP_K):
        start = jnp.sum(jnp.where(lane == route[:, k:k + 1], pstart, 0.0), axis=1, keepdims=True)
        dest = jnp.where(lane == k, start.astype(jnp.int32) + route[:, TOP_K + k:TOP_K + k + 1], dest)
    dest_ref[...] = dest


def _dest_rows(route, pstart):
    T = route.shape[0]
    ts = min(TS_OUT, T)
    row = jnp.pad(pstart.astype(F32).reshape(1, N_EXPERTS), ((0, 0), (0, LANES - N_EXPERTS)))
    dest = pl.pallas_call(
        _dest_kernel,
        grid=(T // ts,),
        in_specs=[pl.BlockSpec((ts, LANES), lambda i: (i, 0)),
                  pl.BlockSpec((1, LANES), lambda i: (0, 0))],
        out_specs=pl.BlockSpec((ts, LANES), lambda i: (i, 0)),
        out_shape=jax.ShapeDtypeStruct((T, LANES), jnp.int32),
        compiler_params=_cparams(("parallel",)),
        name="dest_rows",
    )(route, row)
    return dest[:, :TOP_K].reshape(T * TOP_K)


def _dispatch_kernel(fill_ref, dest_ref, x_ref, out_hbm, zero_ref, sem, zero_sem, *, ts, n_blocks):
    block_slabs = ROW_BLOCK * SUBLANES

    def fill_copy(b):
        off = pl.multiple_of(b * block_slabs, block_slabs)
        return pltpu.make_async_copy(zero_ref, out_hbm.at[pl.ds(off, block_slabs)], zero_sem)

    @pl.when(pl.program_id(0) == 0)
    def _():
        zero_ref[...] = jnp.zeros(zero_ref.shape, F32)

        def start(b, carry):
            @pl.when(fill_ref[b] != 0)
            def _():
                fill_copy(b).start()
            return carry

        def wait(b, carry):
            @pl.when(fill_ref[b] != 0)
            def _():
                fill_copy(b).wait()
            return carry

        lax.fori_loop(0, n_blocks, start, 0)
        lax.fori_loop(0, n_blocks, wait, 0)

    def issue(i, carry):
        for j in range(ISSUE_UNROLL):
            r = i * ISSUE_UNROLL + j
            for k in range(TOP_K):
                d = dest_ref[r * TOP_K + k]
                pltpu.make_async_copy(_slab(x_ref, r), _slab(out_hbm, d), sem).start(priority=k % 2)
        return carry

    lax.fori_loop(0, ts // ISSUE_UNROLL, issue, 0)
    for k in range(TOP_K):
        pltpu.make_async_copy(x_ref, out_hbm.at[pl.ds(0, ts * SUBLANES)], sem).wait()


def _dispatch(xn_slabs, fill, dest_flat, n_rows):
    T = xn_slabs.shape[0] // SUBLANES
    ts = min(TS_DISP, T)
    grid_spec = pltpu.PrefetchScalarGridSpec(
        num_scalar_prefetch=1,
        grid=(T // ts,),
        in_specs=[
            pl.BlockSpec((ts * TOP_K,), lambda i, fl: (i,), memory_space=pltpu.SMEM),
            pl.BlockSpec((ts * SUBLANES, LANES), lambda i, fl: (i, 0)),
        ],
        out_specs=pl.BlockSpec(memory_space=pl.ANY),
        scratch_shapes=[pltpu.VMEM((ROW_BLOCK * SUBLANES, LANES), F32),
                        pltpu.SemaphoreType.DMA, pltpu.SemaphoreType.DMA],
    )
    return pl.pallas_call(
        functools.partial(_dispatch_kernel, ts=ts, n_blocks=n_rows // ROW_BLOCK),
        grid_spec=grid_spec,
        out_shape=jax.ShapeDtypeStruct((n_rows * SUBLANES, LANES), xn_slabs.dtype),
        compiler_params=_cparams(("arbitrary",)),
        name="dispatch",
    )(fill, dest_flat, xn_slabs)


def _expert_kernel(be_ref, na_ref, nxt_ref, x_ref, wg_hbm, bg_ref, wu_hbm, bu_ref, wd_hbm, bd_ref,
                   y_ref, stage_ref, wb_ref, slot_ref, sems):
    i = pl.program_id(0)
    e = be_ref[i]
    w_hbm = (wg_hbm, wu_hbm, wd_hbm)

    def fetch(expert, slot, m):
        return pltpu.make_async_copy(w_hbm[m].at[expert], stage_ref.at[slot, m], sems.at[slot, m])

    @pl.when(i == 0)
    def _():
        slot_ref[0] = 0
        for m in range(3):
            fetch(e, 0, m).start()

    active = i < na_ref[0]
    first = jnp.logical_or(i == 0, e != be_ref[jnp.maximum(i - 1, 0)])

    @pl.when(jnp.logical_and(active, first))
    def _():
        slot = slot_ref[0]
        for m in range(3):
            fetch(e, slot, m).wait()
            wb_ref[m] = stage_ref[slot, m].astype(BF16)

        @pl.when(nxt_ref[i] >= 0)
        def _():
            for m in range(3):
                fetch(nxt_ref[i], 1 - slot, m).start()

        slot_ref[0] = 1 - slot

    @pl.when(active)
    def _():
        x = _slabs_to_rows(x_ref, ROW_BLOCK).astype(BF16)
        g = jnp.dot(x, wb_ref[0], preferred_element_type=F32) + bg_ref[0]
        u = jnp.dot(x, wb_ref[1], preferred_element_type=F32) + bu_ref[0]
        g = jnp.minimum(g, SWIGLU_LIMIT)
        u = jnp.clip(u, -SWIGLU_LIMIT, SWIGLU_LIMIT)
        glu = g * jax.nn.sigmoid(SWIGLU_ALPHA * g)
        y = jnp.dot(((u + 1.0) * glu).astype(BF16), wb_ref[2], preferred_element_type=F32) + bd_ref[0]
        _rows_to_slabs(y_ref, y)


def _experts(x_rows, block_e, n_active, next_e, w_gate, b_gate, w_up, b_up, w_down, b_down):
    E, D, FF = w_gate.shape
    assert D == FF, "the three expert matrices share one staging shape"
    block_slabs = ROW_BLOCK * SUBLANES
    n_blocks = x_rows.shape[0] // block_slabs

    def row_map(i, be, na, nx):
        return (jnp.minimum(i, na[0] - 1), 0)

    def b_map(i, be, na, nx):
        return (be[jnp.minimum(i, na[0] - 1)], 0, 0)

    grid_spec = pltpu.PrefetchScalarGridSpec(
        num_scalar_prefetch=3,
        grid=(n_blocks,),
        in_specs=[
            pl.BlockSpec((block_slabs, LANES), row_map),
            pl.BlockSpec(memory_space=pl.ANY),
            pl.BlockSpec((1, 1, FF), b_map),
            pl.BlockSpec(memory_space=pl.ANY),
            pl.BlockSpec((1, 1, FF), b_map),
            pl.BlockSpec(memory_space=pl.ANY),
            pl.BlockSpec((1, 1, D), b_map),
        ],
        out_specs=pl.BlockSpec((block_slabs, LANES), row_map),
        scratch_shapes=[
            pltpu.VMEM((2, 3, D, FF), F32),
            pltpu.VMEM((3, D, FF), BF16),
            pltpu.SMEM((1,), jnp.int32),
            pltpu.SemaphoreType.DMA((2, 3)),
        ],
    )
    return pl.pallas_call(
        _expert_kernel,
        grid_spec=grid_spec,
        out_shape=jax.ShapeDtypeStruct(x_rows.shape, F32),
        input_output_aliases={3: 0},
        compiler_params=pltpu.CompilerParams(dimension_semantics=("arbitrary",),
                                             vmem_limit_bytes=EXPERT_VMEM_LIMIT),
        name="experts",
    )(block_e, n_active, next_e, x_rows, w_gate, b_gate.reshape(E, 1, FF), w_up,
      b_up.reshape(E, 1, FF), w_down, b_down.reshape(E, 1, D))


def _combine_kernel(dest_ref, dest_next_ref, y_hbm, x1_ref, gates_ref, fg_ref, o_ref, bufs, sems,
                    *, ts, n_steps):
    i = pl.program_id(0)
    slot = i % 2

    def gather_tile(d_ref, s):
        def issue(it, carry):
            for j in range(ISSUE_UNROLL):
                r = it * ISSUE_UNROLL + j
                for k in range(TOP_K):
                    d = d_ref[r * TOP_K + k]
                    pltpu.make_async_copy(_slab(y_hbm, d), _slab(bufs.at[s, k], r),
                                          sems.at[s]).start(priority=k % 2)
            return carry

        lax.fori_loop(0, ts // ISSUE_UNROLL, issue, 0)

    @pl.when(i == 0)
    def _():
        gather_tile(dest_ref, 0)

    @pl.when(i + 1 < n_steps)
    def _():
        gather_tile(dest_next_ref, 1 - slot)

    for k in range(TOP_K):
        pltpu.make_async_copy(y_hbm.at[pl.ds(0, ts * SUBLANES)], bufs.at[slot, k], sems.at[slot]).wait()

    acc = x1_ref[...]
    gates = gates_ref[...]
    for k in range(TOP_K):
        acc = acc + _slabs_to_rows(bufs.at[slot, k], ts) * gates[:, k:k + 1]
    ms = jnp.mean(acc * acc, axis=-1, keepdims=True)
    o_ref[...] = acc * lax.rsqrt(ms + NORM_EPS) * fg_ref[...]


def _combine(y_rows, dest_flat, x1, gates, final_g):
    T, D = x1.shape
    ts = min(TS_COMB, T)
    n_steps = T // ts
    tok = lambda i: (i, 0)
    return pl.pallas_call(
        functools.partial(_combine_kernel, ts=ts, n_steps=n_steps),
        grid=(n_steps,),
        in_specs=[
            pl.BlockSpec((ts * TOP_K,), lambda i: (i,), memory_space=pltpu.SMEM),
            pl.BlockSpec((ts * TOP_K,), lambda i: (jnp.minimum(i + 1, n_steps - 1),),
                         memory_space=pltpu.SMEM),
            pl.BlockSpec(memory_space=pl.ANY),
            pl.BlockSpec((ts, D), tok),
            pl.BlockSpec((ts, LANES), tok),
            pl.BlockSpec((1, D), lambda i: (0, 0)),
        ],
        out_specs=pl.BlockSpec((ts, D), tok),
        out_shape=jax.ShapeDtypeStruct((T, D), F32),
        scratch_shapes=[pltpu.VMEM((2, TOP_K, ts * SUBLANES, LANES), F32),
                        pltpu.SemaphoreType.DMA((2,))],
        compiler_params=_cparams(("arbitrary",)),
        name="combine",
    )(dest_flat, dest_flat, y_rows, x1, gates, final_g.reshape(1, D))


def kernel(x, norm1_g, w_in, q_norm_g, k_norm_g, conv_w, conv_b, lru_wa, lru_ba, lru_wi, lru_bi,
           lru_lam, attn_out_g, lru_out_g, w_out, norm2_g, w_router, b_router, w_gate, b_gate,
           w_up, b_up, w_down, b_down, final_g):
    B, S, D = x.shape
    T = B * S
    assert w_in.shape[0] == 1, "single-layer trunk: the final norm is fused into the layer's combine"
    x2 = x.reshape(T, D)
    for l in range(1):
        qt, k, vt, lru_x, lru_gate = _inproj(x2, norm1_g[l], w_in[l], q_norm_g[l], k_norm_g[l], S)
        score_bound = (HEAD_DIM * Q_SCALE * jnp.max(jnp.abs(q_norm_g[l]))
                       * jnp.max(jnp.abs(k_norm_g[l])))
        attn = _attention(qt, k.reshape(B, S, -1), vt, score_bound, B, S)
        lru = _lru(lru_x.reshape(B, S, -1), lru_gate.reshape(B, S, -1), conv_w[l], conv_b[l],
                   lru_wa[l], lru_ba[l], lru_wi[l], lru_bi[l], lru_lam[l], B, S)
        x1, xn3, route, gates, cnt = _outproj_router(
            attn.reshape(T, -1), lru.reshape(T, -1), x2, attn_out_g[l], lru_out_g[l], w_out[l],
            norm2_g[l], w_router[l], b_router[l])

        counts = cnt[0, :N_EXPERTS].astype(jnp.int32)
        padded = ((counts + ROW_BLOCK - 1) // ROW_BLOCK) * ROW_BLOCK
        pend = jnp.cumsum(padded)
        pstart = (pend - padded).astype(jnp.int32)
        n_rows = T * TOP_K + N_EXPERTS * ROW_BLOCK
        block_start = jnp.arange(n_rows // ROW_BLOCK, dtype=jnp.int32) * ROW_BLOCK
        block_e = jnp.sum((pend[None, :] <= block_start[:, None]).astype(jnp.int32), axis=1)
        block_e = jnp.minimum(block_e, N_EXPERTS - 1)
        n_active = (pend[-1:] // ROW_BLOCK).astype(jnp.int32)
        fill = jnp.logical_or(block_start + ROW_BLOCK == pend[block_e],
                              block_start >= pend[-1]).astype(jnp.int32)

        next_block = pend[block_e] // ROW_BLOCK
        next_e = jnp.where(next_block < n_active[0],
                           block_e[jnp.minimum(next_block, block_e.shape[0] - 1)], -1).astype(jnp.int32)

        dest_flat = _dest_rows(route, pstart)
        x_rows = _dispatch(xn3, fill, dest_flat, n_rows)
        y_rows = _experts(x_rows, block_e, n_active, next_e, w_gate[l], b_gate[l], w_up[l], b_up[l],
                          w_down[l], b_down[l])
        x2 = _combine(y_rows, dest_flat, x1, gates, final_g)
    return x2.reshape(B, S, D)
```

```python
import functools
import math

import jax
import jax.numpy as jnp
from jax import lax
from jax.experimental import pallas as pl
from jax.experimental.pallas import tpu as pltpu

F32 = jnp.float32
BF16 = jnp.bfloat16

GRID_W = 64
HEAD_DIM = 64
N_Q_HEADS = 8
N_KV_HEADS = 2
GQA_GROUP = N_Q_HEADS // N_KV_HEADS
ATTN_W = N_Q_HEADS * HEAD_DIM
KV_W = N_KV_HEADS * HEAD_DIM
LRU_BLOCKS = 8
LRU_C = 8.0
CONV_W = 4
CONV_PAD_L = 2
ROPE_THETA = 10000.0
ROPE_HALF = HEAD_DIM // 2
ROPE_M = ROPE_HALF // 2
N_EXPERTS = 32
TOP_K = 4
SWIGLU_ALPHA = 1.702
SWIGLU_LIMIT = 7.0
NORM_EPS = 1e-5
QK_EPS = 1e-6
LOG2_E = 1.4426950408889634
Q_SCALE = HEAD_DIM ** -0.5 * LOG2_E
SAFE_SCORE_LOG2 = 96.0

LANES = 128
SUBLANES = 8
BF16_SUBLANES = 16
PV_ROWS = HEAD_DIM + BF16_SUBLANES
VMEM_LIMIT = 48 * 1024 * 1024
EXPERT_VMEM_LIMIT = 56 * 1024 * 1024

TS_IN = 512
TQ = 256
TK = 256
KV_UNROLL = 8
HEADS_PER_STEP = 2
TC_LRU = 512
TS_OUT = 512
ROW_BLOCK = 512
TS_DISP = 512
TS_COMB = 256
ISSUE_UNROLL = 8


def _cparams(sem):
    return pltpu.CompilerParams(dimension_semantics=sem, vmem_limit_bytes=VMEM_LIMIT)


def _inproj_kernel(x_ref, g1_ref, wt_ref, w_ref, qg_ref, kg_ref, cos_ref, sin_ref, cost_ref, sint_ref,
                   q_ref, k_ref, v_ref, lx_ref, lg_ref, *, lru_w):
    x = x_ref[...]
    ms = jnp.mean(x * x, axis=-1, keepdims=True)
    xn = (x * lax.rsqrt(ms + NORM_EPS) * g1_ref[...]).astype(BF16)
    ht = lax.dot_general(wt_ref[...], xn, (((1,), (1,)), ((), ())), preferred_element_type=F32)
    h = jnp.dot(xn, w_ref[...], preferred_element_type=F32)

    qw = N_Q_HEADS * LANES
    kw = N_KV_HEADS * LANES
    cost = cost_ref[...]
    sint = sint_ref[...]
    row = lax.broadcasted_iota(jnp.int32, cost.shape, 0)
    first_half_t = (row % ROPE_HALF) < ROPE_M
    qg = qg_ref[...]
    for c in range(N_Q_HEADS):
        sl = slice(c * LANES, (c + 1) * LANES)
        xc = ht[sl]
        hms = jnp.sum(xc * xc, axis=0, keepdims=True) * (1.0 / HEAD_DIM)
        xc = xc * lax.rsqrt(hms + QK_EPS) * qg
        partner = jnp.where(first_half_t, pltpu.roll(xc, LANES - ROPE_M, 0), pltpu.roll(xc, ROPE_M, 0))
        q_ref[0, sl, :] = ((xc * cost + partner * sint) * Q_SCALE).astype(BF16)
    for c in range(N_KV_HEADS):
        sl = slice(c * LANES, (c + 1) * LANES)
        v_ref[0, sl, :] = jnp.where(row >= HEAD_DIM, 1.0, ht[qw + c * LANES: qw + (c + 1) * LANES]).astype(BF16)

    cos = cos_ref[...]
    sin = sin_ref[...]
    lane = lax.broadcasted_iota(jnp.int32, cos.shape, 1)
    first_half = (lane % ROPE_HALF) < ROPE_M
    for c in range(N_KV_HEADS):
        sl = slice(c * LANES, (c + 1) * LANES)
        xc = h[:, sl]
        hms = jnp.sum(xc * xc, axis=-1, keepdims=True) * (1.0 / HEAD_DIM)
        xc = xc * lax.rsqrt(hms + QK_EPS) * kg_ref[...]
        partner = jnp.where(first_half, pltpu.roll(xc, LANES - ROPE_M, 1), pltpu.roll(xc, ROPE_M, 1))
        k_ref[:, sl] = (xc * cos + partner * sin).astype(BF16)
    lx_ref[...] = h[:, kw: kw + lru_w]
    lg_ref[...] = h[:, kw + lru_w: kw + 2 * lru_w]


def _pad_heads(w, n_heads):
    lead = w.shape[:-1]
    w = w.reshape(lead + (n_heads, HEAD_DIM))
    w = jnp.pad(w, [(0, 0)] * len(lead) + [(0, 0), (0, LANES - HEAD_DIM)])
    return w.reshape(lead + (n_heads * LANES,))


def _rope_tables(S):
    t = jnp.arange(S)
    rows = (t // GRID_W).astype(F32)
    cols = (t % GRID_W).astype(F32)
    inv_freq = ROPE_THETA ** (-jnp.arange(ROPE_M, dtype=F32) / ROPE_M)
    ar = rows[:, None] * inv_freq[None, :]
    ac = cols[:, None] * inv_freq[None, :]
    cos = jnp.concatenate([jnp.cos(ar), jnp.cos(ar), jnp.cos(ac), jnp.cos(ac)], axis=-1)
    sin = jnp.concatenate([-jnp.sin(ar), jnp.sin(ar), -jnp.sin(ac), jnp.sin(ac)], axis=-1)
    pad = [(0, 0), (0, LANES - HEAD_DIM)]
    return jnp.pad(cos, pad), jnp.pad(sin, pad)


def _inproj(x2, norm1_g, w_in, q_norm_g, k_norm_g, S):
    T, D = x2.shape
    lru_w = (w_in.shape[1] - ATTN_W - 2 * KV_W) // 2
    o0, o1, o2 = ATTN_W, ATTN_W + KV_W, ATTN_W + 2 * KV_W
    w_t = jnp.concatenate([_pad_heads(w_in[:, :o0], N_Q_HEADS),
                           _pad_heads(w_in[:, o1:o2], N_KV_HEADS)], axis=1).T.astype(BF16)
    w_rest = jnp.concatenate([_pad_heads(w_in[:, o0:o1], N_KV_HEADS), w_in[:, o2:]],
                             axis=1).astype(BF16)
    qg = _pad_heads(q_norm_g.reshape(1, HEAD_DIM), 1).reshape(LANES, 1)
    kg = _pad_heads(k_norm_g.reshape(1, HEAD_DIM), 1)
    cos, sin = _rope_tables(S)
    ts = TS_IN
    n_s = S // ts
    qw, kw = N_Q_HEADS * LANES, N_KV_HEADS * LANES
    const = lambda i: (0, 0)
    tok = lambda i: (i, 0)
    pos = lambda i: (i % n_s, 0)
    pos_t = lambda i: (0, i % n_s)
    tposed = lambda i: (i // n_s, 0, i % n_s)
    return pl.pallas_call(
        functools.partial(_inproj_kernel, lru_w=lru_w),
        grid=(T // ts,),
        in_specs=[
            pl.BlockSpec((ts, D), tok),
            pl.BlockSpec((1, D), const),
            pl.BlockSpec(w_t.shape, const),
            pl.BlockSpec(w_rest.shape, const),
            pl.BlockSpec((LANES, 1), const),
            pl.BlockSpec((1, LANES), const),
            pl.BlockSpec((ts, LANES), pos),
            pl.BlockSpec((ts, LANES), pos),
            pl.BlockSpec((LANES, ts), pos_t),
            pl.BlockSpec((LANES, ts), pos_t),
        ],
        out_specs=[
            pl.BlockSpec((1, qw, ts), tposed),
            pl.BlockSpec((ts, kw), tok),
            pl.BlockSpec((1, kw, ts), tposed),
            pl.BlockSpec((ts, lru_w), tok),
            pl.BlockSpec((ts, lru_w), tok),
        ],
        out_shape=[
            jax.ShapeDtypeStruct((T // S, qw, S), BF16),
            jax.ShapeDtypeStruct((T, kw), BF16),
            jax.ShapeDtypeStruct((T // S, kw, S), BF16),
            jax.ShapeDtypeStruct((T, lru_w), F32),
            jax.ShapeDtypeStruct((T, lru_w), F32),
        ],
        compiler_params=_cparams(("parallel",)),
        name="inproj",
    )(x2, norm1_g.reshape(1, D), w_t, w_rest, qg, kg, cos, sin, cos.T, sin.T)


def _attn_kernel(qt_ref, k_ref, vt_ref, o_ref, acc_ref, s_ref, p_ref, *, tq, tk, n_kv, kv_unroll):
    hp = HEADS_PER_STEP
    spt = GQA_GROUP // hp
    acc_ref[...] = jnp.zeros(acc_ref.shape, F32)

    def scores(j, sp):
        kt = k_ref[0, pl.ds(pl.multiple_of(j * tk, tk), tk), :]
        out = []
        for u in range(hp):
            g = sp * hp + u
            s = jnp.dot(kt, qt_ref[0, g * LANES:(g + 1) * LANES, :], preferred_element_type=F32)
            out.append((s, jnp.max(s, axis=0, keepdims=True)))
        return out

    def softmax_stage(sc, ms, sp):
        out = []
        for u, (s, s_max) in enumerate(sc):
            h = sp * hp + u
            m_new = jnp.maximum(ms[h], s_max)
            out.append((jnp.exp2(ms[h] - m_new), jnp.exp2(s - m_new).astype(BF16)))
            ms[h] = m_new
        return out

    def pv_stage(j, sp, ap):
        vt = vt_ref[0, 0:PV_ROWS, pl.ds(pl.multiple_of(j * tk, tk), tk)]
        for u, (alpha, p) in enumerate(ap):
            g = sp * hp + u
            acc_ref[g] = alpha * acc_ref[g] + jnp.dot(vt, p, preferred_element_type=F32)

    ms = [jnp.full((1, tq), -jnp.inf, F32)] * GQA_GROUP
    ap = softmax_stage(scores(0, 0), ms, 0)
    sc = scores(min(1 // spt, n_kv - 1), 1 % spt)
    for u in range(hp):
        s_ref[u] = sc[u][0]
        p_ref[u] = ap[u][1]

    def body(it, carry):
        ms = list(carry[:GQA_GROUP])
        ap = [(carry[GQA_GROUP + u], p_ref[u]) for u in range(hp)]
        sc = [(s_ref[u], carry[GQA_GROUP + hp + u]) for u in range(hp)]
        for n in range(kv_unroll * spt):
            j = it * kv_unroll + n // spt
            j_next = jnp.minimum(it * kv_unroll + (n + 2) // spt, n_kv - 1)
            sc_next = scores(j_next, (n + 2) % spt)
            ap_next = softmax_stage(sc, ms, (n + 1) % spt)
            pv_stage(j, n % spt, ap)
            sc, ap = sc_next, ap_next
        for u in range(hp):
            s_ref[u] = sc[u][0]
            p_ref[u] = ap[u][1]
        return tuple(ms) + tuple(a for a, _ in ap) + tuple(m for _, m in sc)

    lax.fori_loop(0, n_kv // kv_unroll, body,
                  tuple(ms) + tuple(a for a, _ in ap) + tuple(m for _, m in sc))
    _attn_finalize(acc_ref, o_ref, tq)


def _attn_finalize(acc_ref, o_ref, tq):
    pad = jnp.zeros((LANES - HEAD_DIM, tq), F32)
    for g in range(GQA_GROUP):
        acc = acc_ref[g]
        o = acc[0:HEAD_DIM] / acc[HEAD_DIM:HEAD_DIM + 1, :]
        o_ref[0, :, g * LANES:(g + 1) * LANES] = jnp.concatenate([o, pad], axis=0).T.astype(BF16)


def _attn_bounded_kernel(qt_ref, k_ref, vt_ref, o_ref, acc_ref, s_ref, p_ref, *, tq, tk, n_kv, kv_unroll):
    hp = HEADS_PER_STEP
    spt = GQA_GROUP // hp
    acc_ref[...] = jnp.zeros(acc_ref.shape, F32)

    def scores(j, sp):
        kt = k_ref[0, pl.ds(pl.multiple_of(j * tk, tk), tk), :]
        return [jnp.dot(kt, qt_ref[0, (sp * hp + u) * LANES:(sp * hp + u + 1) * LANES, :],
                        preferred_element_type=F32) for u in range(hp)]

    def probs(sc):
        return [jnp.exp2(s).astype(BF16) for s in sc]

    def pv_stage(j, sp, ps):
        vt = vt_ref[0, 0:PV_ROWS, pl.ds(pl.multiple_of(j * tk, tk), tk)]
        for u, p in enumerate(ps):
            acc_ref[sp * hp + u] += jnp.dot(vt, p, preferred_element_type=F32)

    ps = probs(scores(0, 0))
    sc = scores(min(1 // spt, n_kv - 1), 1 % spt)
    for u in range(hp):
        s_ref[u] = sc[u]
        p_ref[u] = ps[u]

    def body(it, carry):
        ps = [p_ref[u] for u in range(hp)]
        sc = [s_ref[u] for u in range(hp)]
        for n in range(kv_unroll * spt):
            j = it * kv_unroll + n // spt
            j_next = jnp.minimum(it * kv_unroll + (n + 2) // spt, n_kv - 1)
            sc_next = scores(j_next, (n + 2) % spt)
            ps_next = probs(sc)
            pv_stage(j, n % spt, ps)
            sc, ps = sc_next, ps_next
        for u in range(hp):
            s_ref[u] = sc[u]
            p_ref[u] = ps[u]
        return carry

    lax.fori_loop(0, n_kv // kv_unroll, body, 0)
    _attn_finalize(acc_ref, o_ref, tq)


def _attention(qt, k, vt, score_bound, B, S):
    tq = min(TQ, S)
    tk = min(TK, S)
    gw = GQA_GROUP * LANES

    def call(body, name):
        return pl.pallas_call(
            functools.partial(body, tq=tq, tk=tk, n_kv=S // tk,
                              kv_unroll=math.gcd(S // tk, KV_UNROLL)),
            grid=(B, N_KV_HEADS, S // tq),
            in_specs=[
                pl.BlockSpec((1, gw, tq), lambda b, h, i: (b, h, i)),
                pl.BlockSpec((1, S, LANES), lambda b, h, i: (b, 0, h)),
                pl.BlockSpec((1, LANES, S), lambda b, h, i: (b, h, 0)),
            ],
            out_specs=pl.BlockSpec((1, tq, gw), lambda b, h, i: (b, i, h)),
            out_shape=jax.ShapeDtypeStruct((B, S, N_Q_HEADS * LANES), BF16),
            scratch_shapes=[pltpu.VMEM((GQA_GROUP, PV_ROWS, tq), F32),
                            pltpu.VMEM((HEADS_PER_STEP, tk, tq), F32),
                            pltpu.VMEM((HEADS_PER_STEP, tk, tq), BF16)],
            compiler_params=_cparams(("parallel", "parallel", "parallel")),
            name=name,
        )

    return lax.cond(score_bound <= SAFE_SCORE_LOG2,
                    call(_attn_bounded_kernel, "attention_bounded"),
                    call(_attn_kernel, "attention"), qt, k, vt)


def _scan_chunk(a, b, h_in, reverse):
    n = a.shape[0]
    n_groups = n // SUBLANES
    a = a.reshape(n_groups, SUBLANES, LANES)
    b = b.reshape(n_groups, SUBLANES, LANES)
    sub = lax.broadcasted_iota(jnp.int32, a.shape, 1)
    d = 1
    while d < SUBLANES:
        if reverse:
            keep = sub < SUBLANES - d
            shift = SUBLANES - d
        else:
            keep = sub >= d
            shift = d
        a_sh = jnp.where(keep, pltpu.roll(a, shift, 1), 1.0)
        b_sh = jnp.where(keep, pltpu.roll(b, shift, 1), 0.0)
        b = a * b_sh + b
        a = a * a_sh
        d *= 2
    a = a.reshape(n, LANES)
    b = b.reshape(n, LANES)
    order = range(n_groups - 1, -1, -1) if reverse else range(n_groups)
    edge = h_in
    out = [None] * n_groups
    for v in order:
        rows = slice(v * SUBLANES, (v + 1) * SUBLANES)
        hv = b[rows] + a[rows] * jnp.broadcast_to(edge, (SUBLANES, LANES))
        out[v] = hv
        edge = hv[0:1] if reverse else hv[SUBLANES - 1:SUBLANES]
    return jnp.concatenate(out, axis=0), edge


def _lru_kernel(u_ref, gate_ref, cw_ref, cb_ref, w_ref, bias_ref, lam_ref, o_ref,
                up_ref, hf_ref, *, S, tc):
    halo = SUBLANES
    zeros = jnp.zeros((halo, LANES), F32)
    up_ref[0:halo, :] = zeros
    up_ref[S + halo:S + 2 * halo, :] = zeros
    up_ref[halo:S + halo, :] = u_ref[0]
    sp = jax.nn.softplus(-lam_ref[...])
    cw = cw_ref[...]
    cb = cb_ref[...]
    n_chunks = S // tc
    ext = tc + 2 * halo

    def gates(c, d):
        t0 = pl.multiple_of(c * tc, tc)
        ue = up_ref[pl.ds(t0, ext), :]
        xc = cb
        for j in range(CONV_W):
            sh = (CONV_PAD_L - j) % ext
            uj = ue if sh == 0 else pltpu.roll(ue, sh, 0)
            xc = xc + uj[halo:halo + tc] * cw[j:j + 1, :]
        gw = 2 * LANES
        g = jnp.dot(xc.astype(BF16), w_ref[0, :, d * gw:(d + 1) * gw],
                    preferred_element_type=F32) + bias_ref[0, :, d * gw:(d + 1) * gw]
        r = jax.nn.sigmoid(g[:, :LANES])
        i = jax.nn.sigmoid(g[:, LANES:])
        log_a = -LRU_C * r * sp[d:d + 1, :]
        a = jnp.exp(log_a)
        y = 1.0 - a * a
        b = jnp.where(y > 0.0, y * lax.rsqrt(y), 0.0) * i * xc
        return t0, a, b

    def fwd(c, h):
        t0, a, b = gates(c, 0)
        hc, h_last = _scan_chunk(a, b, h, False)
        hf_ref[pl.ds(t0, tc), :] = hc
        return h_last

    lax.fori_loop(0, n_chunks, fwd, jnp.zeros((1, LANES), F32))

    def bwd(ci, h):
        t0, a, b = gates(n_chunks - 1 - ci, 1)
        hc, h_last = _scan_chunk(a, b, h, True)
        gate = gate_ref[0, pl.ds(t0, tc), :]
        o_ref[0, pl.ds(t0, tc), :] = (hf_ref[pl.ds(t0, tc), :] + hc) * jax.nn.gelu(gate)
        return h_last

    lax.fori_loop(0, n_chunks, bwd, jnp.zeros((1, LANES), F32))


def _block_diag_pairs(w):
    nb, bw, _ = w.shape
    w = w.reshape(nb // 2, 2, bw, bw)
    z = jnp.zeros_like(w[:, 0])
    top = jnp.concatenate([w[:, 0], z], axis=-1)
    bot = jnp.concatenate([z, w[:, 1]], axis=-1)
    return jnp.concatenate([top, bot], axis=-2)


def _lru(lru_x, lru_gate, conv_w, conv_b, wa, ba, wi, bi, lam, B, S):
    C = lru_x.shape[-1]
    nc = C // LANES
    tc = min(TC_LRU, S)
    w = jnp.concatenate([_block_diag_pairs(wa[0]), _block_diag_pairs(wi[0]),
                         _block_diag_pairs(wa[1]), _block_diag_pairs(wi[1])], axis=-1).astype(BF16)
    bias = jnp.stack([ba[0].reshape(nc, LANES), bi[0].reshape(nc, LANES),
                      ba[1].reshape(nc, LANES), bi[1].reshape(nc, LANES)], axis=1)
    bias = bias.reshape(nc, 1, 4 * LANES)
    blk = lambda b, c: (b, 0, c)
    return pl.pallas_call(
        functools.partial(_lru_kernel, S=S, tc=tc),
        grid=(B, nc),
        in_specs=[
            pl.BlockSpec((1, S, LANES), blk),
            pl.BlockSpec((1, S, LANES), blk),
            pl.BlockSpec((CONV_W, LANES), lambda b, c: (0, c)),
            pl.BlockSpec((1, LANES), lambda b, c: (0, c)),
            pl.BlockSpec((1, LANES, 4 * LANES), lambda b, c: (c, 0, 0)),
            pl.BlockSpec((1, 1, 4 * LANES), lambda b, c: (c, 0, 0)),
            pl.BlockSpec((2, LANES), lambda b, c: (0, c)),
        ],
        out_specs=pl.BlockSpec((1, S, LANES), blk),
        out_shape=jax.ShapeDtypeStruct((B, S, C), F32),
        scratch_shapes=[
            pltpu.VMEM((S + 2 * SUBLANES, LANES), F32),
            pltpu.VMEM((S, LANES), F32),
        ],
        compiler_params=_cparams(("parallel", "parallel")),
        name="rglru",
    )(lru_x, lru_gate, conv_w, conv_b.reshape(1, C), w, bias, lam)


def _rows_to_slabs(ref, x):
    n = x.shape[0]
    for s in range(SUBLANES):
        ref[pl.ds(s, n, stride=SUBLANES), :] = x[:, s * LANES:(s + 1) * LANES]


def _slabs_to_rows(ref, n):
    return jnp.concatenate([ref[pl.ds(s, n, stride=SUBLANES), :] for s in range(SUBLANES)], axis=1)


def _slab(ref, r):
    return ref.at[pl.ds(pl.multiple_of(r * SUBLANES, SUBLANES), SUBLANES)]


def _outproj_kernel(a_ref, l_ref, x_ref, ag_ref, lg_ref, wa_ref, wl_ref, g2_ref,
                    wrh_ref, wrl_ref, br_ref, tri_ref,
                    x1_ref, xn3_ref, route_ref, gates_ref, cnt_ref, carry_ref, *, attn_w, lru_w):
    step = pl.program_id(0)

    @pl.when(step == 0)
    def _():
        carry_ref[...] = jnp.zeros_like(carry_ref)

    a = a_ref[...].astype(F32)
    ams = jnp.sum(a * a, axis=-1, keepdims=True) * (1.0 / attn_w)
    an = a * lax.rsqrt(ams + NORM_EPS) * ag_ref[...]
    l = l_ref[...]
    lms = jnp.sum(l * l, axis=-1, keepdims=True) * (1.0 / lru_w)
    ln = l * lax.rsqrt(lms + NORM_EPS) * lg_ref[...]
    mix = (jnp.dot(an.astype(BF16), wa_ref[...], preferred_element_type=F32)
           + jnp.dot(ln.astype(BF16), wl_ref[...], preferred_element_type=F32))
    x1 = x_ref[...] + mix
    x1_ref[...] = x1
    ms = jnp.mean(x1 * x1, axis=-1, keepdims=True)
    xn = x1 * lax.rsqrt(ms + NORM_EPS) * g2_ref[...]
    _rows_to_slabs(xn3_ref, xn)

    hi = xn.astype(BF16)
    lo = (xn - hi.astype(F32)).astype(BF16)
    logits = (jnp.dot(hi, wrh_ref[...], preferred_element_type=F32)
              + jnp.dot(lo, wrh_ref[...], preferred_element_type=F32)
              + jnp.dot(hi, wrl_ref[...], preferred_element_type=F32)) + br_ref[...]
    lane = lax.broadcasted_iota(jnp.int32, logits.shape, 1)
    neg = -jnp.inf
    work = jnp.where(lane < N_EXPERTS, logits, neg)
    sel = jnp.zeros(logits.shape, F32)
    idxs, vals = [], []
    for _ in range(TOP_K):
        m = jnp.max(work, axis=1, keepdims=True)
        idx = jnp.min(jnp.where(work == m, lane, LANES), axis=1, keepdims=True)
        hit = lane == idx
        work = jnp.where(hit, neg, work)
        sel = sel + hit.astype(F32)
        idxs.append(idx)
        vals.append(m)
    es = [jnp.exp(v - vals[0]) for v in vals]
    den = es[0] + es[1] + es[2] + es[3]

    prefix = jnp.dot(tri_ref[...], sel.astype(BF16), preferred_element_type=F32) + carry_ref[...]
    carry_ref[...] = carry_ref[...] + jnp.sum(sel, axis=0, keepdims=True)
    cnt_ref[...] = carry_ref[...]

    route = jnp.zeros(logits.shape, jnp.int32)
    gates = jnp.zeros(logits.shape, F32)
    for k in range(TOP_K):
        rank = jnp.sum(jnp.where(lane == idxs[k], prefix, 0.0), axis=1, keepdims=True).astype(jnp.int32)
        route = jnp.where(lane == k, idxs[k], route)
        route = jnp.where(lane == TOP_K + k, rank, route)
        gates = jnp.where(lane == k, es[k] / den, gates)
    route_ref[...] = route
    gates_ref[...] = gates


def _outproj_router(attn, lru, x2, attn_out_g, lru_out_g, w_out, norm2_g, w_router, b_router):
    T, D = x2.shape
    lru_w = lru.shape[-1]
    ts = min(TS_OUT, T)
    wa = w_out[:ATTN_W].reshape(N_Q_HEADS, HEAD_DIM, D)
    wa = jnp.pad(wa, ((0, 0), (0, LANES - HEAD_DIM), (0, 0))).reshape(N_Q_HEADS * LANES, D).astype(BF16)
    wl = w_out[ATTN_W:].astype(BF16)
    ag = _pad_heads(attn_out_g.reshape(1, ATTN_W), N_Q_HEADS)
    wr = jnp.pad(w_router, ((0, 0), (0, LANES - N_EXPERTS)))
    wrh = wr.astype(BF16)
    wrl = (wr - wrh.astype(F32)).astype(BF16)
    br = jnp.pad(b_router.reshape(1, N_EXPERTS), ((0, 0), (0, LANES - N_EXPERTS)))
    tri = (jnp.arange(ts)[:, None] > jnp.arange(ts)[None, :]).astype(BF16)
    const = lambda i: (0, 0)
    tok = lambda i: (i, 0)
    aw = N_Q_HEADS * LANES
    return pl.pallas_call(
        functools.partial(_outproj_kernel, attn_w=ATTN_W, lru_w=lru_w),
        grid=(T // ts,),
        in_specs=[
            pl.BlockSpec((ts, aw), tok),
            pl.BlockSpec((ts, lru_w), tok),
            pl.BlockSpec((ts, D), tok),
            pl.BlockSpec((1, aw), const),
            pl.BlockSpec((1, lru_w), const),
            pl.BlockSpec((aw, D), const),
            pl.BlockSpec((lru_w, D), const),
            pl.BlockSpec((1, D), const),
            pl.BlockSpec((D, LANES), const),
            pl.BlockSpec((D, LANES), const),
            pl.BlockSpec((1, LANES), const),
            pl.BlockSpec((ts, ts), const),
        ],
        out_specs=[
            pl.BlockSpec((ts, D), tok),
            pl.BlockSpec((ts * SUBLANES, LANES), tok),
            pl.BlockSpec((ts, LANES), tok),
            pl.BlockSpec((ts, LANES), tok),
            pl.BlockSpec((1, LANES), const),
        ],
        out_shape=[
            jax.ShapeDtypeStruct((T, D), F32),
            jax.ShapeDtypeStruct((T * SUBLANES, LANES), F32),
            jax.ShapeDtypeStruct((T, LANES), jnp.int32),
            jax.ShapeDtypeStruct((T, LANES), F32),
            jax.ShapeDtypeStruct((1, LANES), F32),
        ],
        scratch_shapes=[pltpu.VMEM((1, LANES), F32)],
        compiler_params=_cparams(("arbitrary",)),
        name="outproj_router",
    )(attn, lru, x2, ag, lru_out_g.reshape(1, lru_w), wa, wl, norm2_g.reshape(1, D),
      wrh, wrl, br, tri)


def _dest_kernel(route_ref, pstart_ref, dest_ref):
    route = route_ref[...]
    lane = lax.broadcasted_iota(jnp.int32, route.shape, 1)
    pstart = pstart_ref[...]
    dest = jnp.zeros(route.shape, jnp.int32)
    for k in range(TOP_K):
        start = jnp.sum(jnp.where(lane == route[:, k:k + 1], pstart, 0.0), axis=1, keepdims=True)
        dest = jnp.where(lane == k, start.astype(jnp.int32) + route[:, TOP_K + k:TOP_K + k + 1], dest)
    dest_ref[...] = dest


def _dest_rows(route, pstart):
    T = route.shape[0]
    ts = min(TS_OUT, T)
    row = jnp.pad(pstart.astype(F32).reshape(1, N_EXPERTS), ((0, 0), (0, LANES - N_EXPERTS)))
    dest = pl.pallas_call(
        _dest_kernel,
        grid=(T // ts,),
        in_specs=[pl.BlockSpec((ts, LANES), lambda i: (i, 0)),
                  pl.BlockSpec((1, LANES), lambda i: (0, 0))],
        out_specs=pl.BlockSpec((ts, LANES), lambda i: (i, 0)),
        out_shape=jax.ShapeDtypeStruct((T, LANES), jnp.int32),
        compiler_params=_cparams(("parallel",)),
        name="dest_rows",
    )(route, row)
    return dest[:, :TOP_K].reshape(T * TOP_K)


def _dispatch_kernel(fill_ref, dest_ref, x_ref, out_hbm, zero_ref, sem, zero_sem, *, ts, n_blocks):
    block_slabs = ROW_BLOCK * SUBLANES

    def fill_copy(b):
        off = pl.multiple_of(b * block_slabs, block_slabs)
        return pltpu.make_async_copy(zero_ref, out_hbm.at[pl.ds(off, block_slabs)], zero_sem)

    @pl.when(pl.program_id(0) == 0)
    def _():
        zero_ref[...] = jnp.zeros(zero_ref.shape, F32)

        def start(b, carry):
            @pl.when(fill_ref[b] != 0)
            def _():
                fill_copy(b).start()
            return carry

        def wait(b, carry):
            @pl.when(fill_ref[b] != 0)
            def _():
                fill_copy(b).wait()
            return carry

        lax.fori_loop(0, n_blocks, start, 0)
        lax.fori_loop(0, n_blocks, wait, 0)

    def issue(i, carry):
        for j in range(ISSUE_UNROLL):
            r = i * ISSUE_UNROLL + j
            for k in range(TOP_K):
                d = dest_ref[r * TOP_K + k]
                pltpu.make_async_copy(_slab(x_ref, r), _slab(out_hbm, d), sem).start(priority=k % 2)
        return carry

    lax.fori_loop(0, ts // ISSUE_UNROLL, issue, 0)
    for k in range(TOP_K):
        pltpu.make_async_copy(x_ref, out_hbm.at[pl.ds(0, ts * SUBLANES)], sem).wait()


def _dispatch(xn_slabs, fill, dest_flat, n_rows):
    T = xn_slabs.shape[0] // SUBLANES
    ts = min(TS_DISP, T)
    grid_spec = pltpu.PrefetchScalarGridSpec(
        num_scalar_prefetch=1,
        grid=(T // ts,),
        in_specs=[
            pl.BlockSpec((ts * TOP_K,), lambda i, fl: (i,), memory_space=pltpu.SMEM),
            pl.BlockSpec((ts * SUBLANES, LANES), lambda i, fl: (i, 0)),
        ],
        out_specs=pl.BlockSpec(memory_space=pl.ANY),
        scratch_shapes=[pltpu.VMEM((ROW_BLOCK * SUBLANES, LANES), F32),
                        pltpu.SemaphoreType.DMA, pltpu.SemaphoreType.DMA],
    )
    return pl.pallas_call(
        functools.partial(_dispatch_kernel, ts=ts, n_blocks=n_rows // ROW_BLOCK),
        grid_spec=grid_spec,
        out_shape=jax.ShapeDtypeStruct((n_rows * SUBLANES, LANES), xn_slabs.dtype),
        compiler_params=_cparams(("arbitrary",)),
        name="dispatch",
    )(fill, dest_flat, xn_slabs)


def _expert_kernel(be_ref, na_ref, nxt_ref, x_ref, wg_hbm, bg_ref, wu_hbm, bu_ref, wd_hbm, bd_ref,
                   y_ref, stage_ref, wb_ref, slot_ref, sems):
    i = pl.program_id(0)
    e = be_ref[i]
    w_hbm = (wg_hbm, wu_hbm, wd_hbm)

    def fetch(expert, slot, m):
        return pltpu.make_async_copy(w_hbm[m].at[expert], stage_ref.at[slot, m], sems.at[slot, m])

    @pl.when(i == 0)
    def _():
        slot_ref[0] = 0
        for m in range(3):
            fetch(e, 0, m).start()

    active = i < na_ref[0]
    first = jnp.logical_or(i == 0, e != be_ref[jnp.maximum(i - 1, 0)])

    @pl.when(jnp.logical_and(active, first))
    def _():
        slot = slot_ref[0]
        for m in range(3):
            fetch(e, slot, m).wait()
            wb_ref[m] = stage_ref[slot, m].astype(BF16)

        @pl.when(nxt_ref[i] >= 0)
        def _():
            for m in range(3):
                fetch(nxt_ref[i], 1 - slot, m).start()

        slot_ref[0] = 1 - slot

    @pl.when(active)
    def _():
        x = _slabs_to_rows(x_ref, ROW_BLOCK).astype(BF16)
        g = jnp.dot(x, wb_ref[0], preferred_element_type=F32) + bg_ref[0]
        u = jnp.dot(x, wb_ref[1], preferred_element_type=F32) + bu_ref[0]
        g = jnp.minimum(g, SWIGLU_LIMIT)
        u = jnp.clip(u, -SWIGLU_LIMIT, SWIGLU_LIMIT)
        glu = g * jax.nn.sigmoid(SWIGLU_ALPHA * g)
        y = jnp.dot(((u + 1.0) * glu).astype(BF16), wb_ref[2], preferred_element_type=F32) + bd_ref[0]
        _rows_to_slabs(y_ref, y)


def _experts(x_rows, block_e, n_active, next_e, w_gate, b_gate, w_up, b_up, w_down, b_down):
    E, D, FF = w_gate.shape
    assert D == FF, "the three expert matrices share one staging shape"
    block_slabs = ROW_BLOCK * SUBLANES
    n_blocks = x_rows.shape[0] // block_slabs

    def row_map(i, be, na, nx):
        return (jnp.minimum(i, na[0] - 1), 0)

    def b_map(i, be, na, nx):
        return (be[jnp.minimum(i, na[0] - 1)], 0, 0)

    grid_spec = pltpu.PrefetchScalarGridSpec(
        num_scalar_prefetch=3,
        grid=(n_blocks,),
        in_specs=[
            pl.BlockSpec((block_slabs, LANES), row_map),
            pl.BlockSpec(memory_space=pl.ANY),
            pl.BlockSpec((1, 1, FF), b_map),
            pl.BlockSpec(memory_space=pl.ANY),
            pl.BlockSpec((1, 1, FF), b_map),
            pl.BlockSpec(memory_space=pl.ANY),
            pl.BlockSpec((1, 1, D), b_map),
        ],
        out_specs=pl.BlockSpec((block_slabs, LANES), row_map),
        scratch_shapes=[
            pltpu.VMEM((2, 3, D, FF), F32),
            pltpu.VMEM((3, D, FF), BF16),
            pltpu.SMEM((1,), jnp.int32),
            pltpu.SemaphoreType.DMA((2, 3)),
        ],
    )
    return pl.pallas_call(
        _expert_kernel,
        grid_spec=grid_spec,
        out_shape=jax.ShapeDtypeStruct(x_rows.shape, F32),
        input_output_aliases={3: 0},
        compiler_params=pltpu.CompilerParams(dimension_semantics=("arbitrary",),
                                             vmem_limit_bytes=EXPERT_VMEM_LIMIT),
        name="experts",
    )(block_e, n_active, next_e, x_rows, w_gate, b_gate.reshape(E, 1, FF), w_up,
      b_up.reshape(E, 1, FF), w_down, b_down.reshape(E, 1, D))


def _combine_kernel(dest_ref, dest_next_ref, y_hbm, x1_ref, gates_ref, fg_ref, o_ref, bufs, sems,
                    *, ts, n_steps):
    i = pl.program_id(0)
    slot = i % 2

    def gather_tile(d_ref, s):
        def issue(it, carry):
            for j in range(ISSUE_UNROLL):
                r = it * ISSUE_UNROLL + j
                for k in range(TOP_K):
                    d = d_ref[r * TOP_K + k]
                    pltpu.make_async_copy(_slab(y_hbm, d), _slab(bufs.at[s, k], r),
                                          sems.at[s]).start(priority=k % 2)
            return carry

        lax.fori_loop(0, ts // ISSUE_UNROLL, issue, 0)

    @pl.when(i == 0)
    def _():
        gather_tile(dest_ref, 0)

    @pl.when(i + 1 < n_steps)
    def _():
        gather_tile(dest_next_ref, 1 - slot)

    for k in range(TOP_K):
        pltpu.make_async_copy(y_hbm.at[pl.ds(0, ts * SUBLANES)], bufs.at[slot, k], sems.at[slot]).wait()

    acc = x1_ref[...]
    gates = gates_ref[...]
    for k in range(TOP_K):
        acc = acc + _slabs_to_rows(bufs.at[slot, k], ts) * gates[:, k:k + 1]
    ms = jnp.mean(acc * acc, axis=-1, keepdims=True)
    o_ref[...] = acc * lax.rsqrt(ms + NORM_EPS) * fg_ref[...]


def _combine(y_rows, dest_flat, x1, gates, final_g):
    T, D = x1.shape
    ts = min(TS_COMB, T)
    n_steps = T // ts
    tok = lambda i: (i, 0)
    return pl.pallas_call(
        functools.partial(_combine_kernel, ts=ts, n_steps=n_steps),
        grid=(n_steps,),
        in_specs=[
            pl.BlockSpec((ts * TOP_K,), lambda i: (i,), memory_space=pltpu.SMEM),
            pl.BlockSpec((ts * TOP_K,), lambda i: (jnp.minimum(i + 1, n_steps - 1),),
                         memory_space=pltpu.SMEM),
            pl.BlockSpec(memory_space=pl.ANY),
            pl.BlockSpec((ts, D), tok),
            pl.BlockSpec((ts, LANES), tok),
            pl.BlockSpec((1, D), lambda i: (0, 0)),
        ],
        out_specs=pl.BlockSpec((ts, D), tok),
        out_shape=jax.ShapeDtypeStruct((T, D), F32),
        scratch_shapes=[pltpu.VMEM((2, TOP_K, ts * SUBLANES, LANES), F32),
                        pltpu.SemaphoreType.DMA((2,))],
        compiler_params=_cparams(("arbitrary",)),
        name="combine",
    )(dest_flat, dest_flat, y_rows, x1, gates, final_g.reshape(1, D))


def kernel(x, norm1_g, w_in, q_norm_g, k_norm_g, conv_w, conv_b, lru_wa, lru_ba, lru_wi, lru_bi,
           lru_lam, attn_out_g, lru_out_g, w_out, norm2_g, w_router, b_router, w_gate, b_gate,
           w_up, b_up, w_down, b_down, final_g):
    B, S, D = x.shape
    T = B * S
    assert w_in.shape[0] == 1, "single-layer trunk: the final norm is fused into the layer's combine"
    x2 = x.reshape(T, D)
    for l in range(1):
        qt, k, vt, lru_x, lru_gate = _inproj(x2, norm1_g[l], w_in[l], q_norm_g[l], k_norm_g[l], S)
        score_bound = (HEAD_DIM * Q_SCALE * jnp.max(jnp.abs(q_norm_g[l]))
                       * jnp.max(jnp.abs(k_norm_g[l])))
        attn = _attention(qt, k.reshape(B, S, -1), vt, score_bound, B, S)
        lru = _lru(lru_x.reshape(B, S, -1), lru_gate.reshape(B, S, -1), conv_w[l], conv_b[l],
                   lru_wa[l], lru_ba[l], lru_wi[l], lru_bi[l], lru_lam[l], B, S)
        x1, xn3, route, gates, cnt = _outproj_router(
            attn.reshape(T, -1), lru.reshape(T, -1), x2, attn_out_g[l], lru_out_g[l], w_out[l],
            norm2_g[l], w_router[l], b_router[l])

        counts = cnt[0, :N_EXPERTS].astype(jnp.int32)
        padded = ((counts + ROW_BLOCK - 1) // ROW_BLOCK) * ROW_BLOCK
        pend = jnp.cumsum(padded)
        pstart = (pend - padded).astype(jnp.int32)
        n_rows = T * TOP_K + N_EXPERTS * ROW_BLOCK
        block_start = jnp.arange(n_rows // ROW_BLOCK, dtype=jnp.int32) * ROW_BLOCK
        block_e = jnp.sum((pend[None, :] <= block_start[:, None]).astype(jnp.int32), axis=1)
        block_e = jnp.minimum(block_e, N_EXPERTS - 1)
        n_active = (pend[-1:] // ROW_BLOCK).astype(jnp.int32)
        fill = jnp.logical_or(block_start + ROW_BLOCK == pend[block_e],
                              block_start >= pend[-1]).astype(jnp.int32)

        next_block = pend[block_e] // ROW_BLOCK
        next_e = jnp.where(next_block < n_active[0],
                           block_e[jnp.minimum(next_block, block_e.shape[0] - 1)], -1).astype(jnp.int32)

        dest_flat = _dest_rows(route, pstart)
        x_rows = _dispatch(xn3, fill, dest_flat, n_rows)
        y_rows = _experts(x_rows, block_e, n_active, next_e, w_gate[l], b_gate[l], w_up[l], b_up[l],
                          w_down[l], b_down[l])
        x2 = _combine(y_rows, dest_flat, x1, gates, final_g)
    return x2.reshape(B, S, D)
```

```python
import functools
import math

import jax
import jax.numpy as jnp
from jax import lax
from jax.experimental import pallas as pl
from jax.experimental.pallas import tpu as pltpu

F32 = jnp.float32
BF16 = jnp.bfloat16

GRID_W = 64
HEAD_DIM = 64
N_Q_HEADS = 8
N_KV_HEADS = 2
GQA_GROUP = N_Q_HEADS // N_KV_HEADS
ATTN_W = N_Q_HEADS * HEAD_DIM
KV_W = N_KV_HEADS * HEAD_DIM
LRU_BLOCKS = 8
LRU_C = 8.0
CONV_W = 4
CONV_PAD_L = 2
ROPE_THETA = 10000.0
ROPE_HALF = HEAD_DIM // 2
ROPE_M = ROPE_HALF // 2
N_EXPERTS = 32
TOP_K = 4
SWIGLU_ALPHA = 1.702
SWIGLU_LIMIT = 7.0
NORM_EPS = 1e-5
QK_EPS = 1e-6
LOG2_E = 1.4426950408889634
Q_SCALE = HEAD_DIM ** -0.5 * LOG2_E
SAFE_SCORE_LOG2 = 96.0

LANES = 128
SUBLANES = 8
BF16_SUBLANES = 16
PV_ROWS = HEAD_DIM + BF16_SUBLANES
VMEM_LIMIT = 48 * 1024 * 1024
EXPERT_VMEM_LIMIT = 56 * 1024 * 1024

TS_IN = 512
TQ = 256
TK = 256
KV_UNROLL = 8
HEADS_PER_STEP = 2
TC_LRU = 512
TS_OUT = 512
ROW_BLOCK = 512
TS_DISP = 512
TS_COMB = 256
ISSUE_UNROLL = 8


def _cparams(sem):
    return pltpu.CompilerParams(dimension_semantics=sem, vmem_limit_bytes=VMEM_LIMIT)


def _inproj_kernel(x_ref, g1_ref, wt_ref, w_ref, qg_ref, kg_ref, cos_ref, sin_ref, cost_ref, sint_ref,
                   q_ref, k_ref, v_ref, lx_ref, lg_ref, *, lru_w):
    x = x_ref[...]
    ms = jnp.mean(x * x, axis=-1, keepdims=True)
    xn = (x * lax.rsqrt(ms + NORM_EPS) * g1_ref[...]).astype(BF16)
    ht = lax.dot_general(wt_ref[...], xn, (((1,), (1,)), ((), ())), preferred_element_type=F32)
    h = jnp.dot(xn, w_ref[...], preferred_element_type=F32)

    qw = N_Q_HEADS * LANES
    kw = N_KV_HEADS * LANES
    cost = cost_ref[...]
    sint = sint_ref[...]
    row = lax.broadcasted_iota(jnp.int32, cost.shape, 0)
    first_half_t = (row % ROPE_HALF) < ROPE_M
    qg = qg_ref[...]
    for c in range(N_Q_HEADS):
        sl = slice(c * LANES, (c + 1) * LANES)
        xc = ht[sl]
        hms = jnp.sum(xc * xc, axis=0, keepdims=True) * (1.0 / HEAD_DIM)
        xc = xc * lax.rsqrt(hms + QK_EPS) * qg
        partner = jnp.where(first_half_t, pltpu.roll(xc, LANES - ROPE_M, 0), pltpu.roll(xc, ROPE_M, 0))
        q_ref[0, sl, :] = ((xc * cost + partner * sint) * Q_SCALE).astype(BF16)
    for c in range(N_KV_HEADS):
        sl = slice(c * LANES, (c + 1) * LANES)
        v_ref[0, sl, :] = jnp.where(row >= HEAD_DIM, 1.0, ht[qw + c * LANES: qw + (c + 1) * LANES]).astype(BF16)

    cos = cos_ref[...]
    sin = sin_ref[...]
    lane = lax.broadcasted_iota(jnp.int32, cos.shape, 1)
    first_half = (lane % ROPE_HALF) < ROPE_M
    for c in range(N_KV_HEADS):
        sl = slice(c * LANES, (c + 1) * LANES)
        xc = h[:, sl]
        hms = jnp.sum(xc * xc, axis=-1, keepdims=True) * (1.0 / HEAD_DIM)
        xc = xc * lax.rsqrt(hms + QK_EPS) * kg_ref[...]
        partner = jnp.where(first_half, pltpu.roll(xc, LANES - ROPE_M, 1), pltpu.roll(xc, ROPE_M, 1))
        k_ref[:, sl] = (xc * cos + partner * sin).astype(BF16)
    lx_ref[...] = h[:, kw: kw + lru_w]
    lg_ref[...] = h[:, kw + lru_w: kw + 2 * lru_w]


def _pad_heads(w, n_heads):
    lead = w.shape[:-1]
    w = w.reshape(lead + (n_heads, HEAD_DIM))
    w = jnp.pad(w, [(0, 0)] * len(lead) + [(0, 0), (0, LANES - HEAD_DIM)])
    return w.reshape(lead + (n_heads * LANES,))


def _rope_tables(S):
    t = jnp.arange(S)
    rows = (t // GRID_W).astype(F32)
    cols = (t % GRID_W).astype(F32)
    inv_freq = ROPE_THETA ** (-jnp.arange(ROPE_M, dtype=F32) / ROPE_M)
    ar = rows[:, None] * inv_freq[None, :]
    ac = cols[:, None] * inv_freq[None, :]
    cos = jnp.concatenate([jnp.cos(ar), jnp.cos(ar), jnp.cos(ac), jnp.cos(ac)], axis=-1)
    sin = jnp.concatenate([-jnp.sin(ar), jnp.sin(ar), -jnp.sin(ac), jnp.sin(ac)], axis=-1)
    pad = [(0, 0), (0, LANES - HEAD_DIM)]
    return jnp.pad(cos, pad), jnp.pad(sin, pad)


def _inproj(x2, norm1_g, w_in, q_norm_g, k_norm_g, S):
    T, D = x2.shape
    lru_w = (w_in.shape[1] - ATTN_W - 2 * KV_W) // 2
    o0, o1, o2 = ATTN_W, ATTN_W + KV_W, ATTN_W + 2 * KV_W
    w_t = jnp.concatenate([_pad_heads(w_in[:, :o0], N_Q_HEADS),
                           _pad_heads(w_in[:, o1:o2], N_KV_HEADS)], axis=1).T.astype(BF16)
    w_rest = jnp.concatenate([_pad_heads(w_in[:, o0:o1], N_KV_HEADS), w_in[:, o2:]],
                             axis=1).astype(BF16)
    qg = _pad_heads(q_norm_g.reshape(1, HEAD_DIM), 1).reshape(LANES, 1)
    kg = _pad_heads(k_norm_g.reshape(1, HEAD_DIM), 1)
    cos, sin = _rope_tables(S)
    ts = TS_IN
    n_s = S // ts
    qw, kw = N_Q_HEADS * LANES, N_KV_HEADS * LANES
    const = lambda i: (0, 0)
    tok = lambda i: (i, 0)
    pos = lambda i: (i % n_s, 0)
    pos_t = lambda i: (0, i % n_s)
    tposed = lambda i: (i // n_s, 0, i % n_s)
    return pl.pallas_call(
        functools.partial(_inproj_kernel, lru_w=lru_w),
        grid=(T // ts,),
        in_specs=[
            pl.BlockSpec((ts, D), tok),
            pl.BlockSpec((1, D), const),
            pl.BlockSpec(w_t.shape, const),
            pl.BlockSpec(w_rest.shape, const),
            pl.BlockSpec((LANES, 1), const),
            pl.BlockSpec((1, LANES), const),
            pl.BlockSpec((ts, LANES), pos),
            pl.BlockSpec((ts, LANES), pos),
            pl.BlockSpec((LANES, ts), pos_t),
            pl.BlockSpec((LANES, ts), pos_t),
        ],
        out_specs=[
            pl.BlockSpec((1, qw, ts), tposed),
            pl.BlockSpec((ts, kw), tok),
            pl.BlockSpec((1, kw, ts), tposed),
            pl.BlockSpec((ts, lru_w), tok),
            pl.BlockSpec((ts, lru_w), tok),
        ],
        out_shape=[
            jax.ShapeDtypeStruct((T // S, qw, S), BF16),
            jax.ShapeDtypeStruct((T, kw), BF16),
            jax.ShapeDtypeStruct((T // S, kw, S), BF16),
            jax.ShapeDtypeStruct((T, lru_w), F32),
            jax.ShapeDtypeStruct((T, lru_w), F32),
        ],
        compiler_params=_cparams(("parallel",)),
        name="inproj",
    )(x2, norm1_g.reshape(1, D), w_t, w_rest, qg, kg, cos, sin, cos.T, sin.T)


def _attn_kernel(qt_ref, k_ref, vt_ref, o_ref, acc_ref, s_ref, p_ref, *, tq, tk, n_kv, kv_unroll):
    hp = HEADS_PER_STEP
    spt = GQA_GROUP // hp
    acc_ref[...] = jnp.zeros(acc_ref.shape, F32)

    def scores(j, sp):
        kt = k_ref[0, pl.ds(pl.multiple_of(j * tk, tk), tk), :]
        out = []
        for u in range(hp):
            g = sp * hp + u
            s = jnp.dot(kt, qt_ref[0, g * LANES:(g + 1) * LANES, :], preferred_element_type=F32)
            out.append((s, jnp.max(s, axis=0, keepdims=True)))
        return out

    def softmax_stage(sc, ms, sp):
        out = []
        for u, (s, s_max) in enumerate(sc):
            h = sp * hp + u
            m_new = jnp.maximum(ms[h], s_max)
            out.append((jnp.exp2(ms[h] - m_new), jnp.exp2(s - m_new).astype(BF16)))
            ms[h] = m_new
        return out

    def pv_stage(j, sp, ap):
        vt = vt_ref[0, 0:PV_ROWS, pl.ds(pl.multiple_of(j * tk, tk), tk)]
        for u, (alpha, p) in enumerate(ap):
            g = sp * hp + u
            acc_ref[g] = alpha * acc_ref[g] + jnp.dot(vt, p, preferred_element_type=F32)

    ms = [jnp.full((1, tq), -jnp.inf, F32)] * GQA_GROUP
    ap = softmax_stage(scores(0, 0), ms, 0)
    sc = scores(min(1 // spt, n_kv - 1), 1 % spt)
    for u in range(hp):
        s_ref[u] = sc[u][0]
        p_ref[u] = ap[u][1]

    def body(it, carry):
        ms = list(carry[:GQA_GROUP])
        ap = [(carry[GQA_GROUP + u], p_ref[u]) for u in range(hp)]
        sc = [(s_ref[u], carry[GQA_GROUP + hp + u]) for u in range(hp)]
        for n in range(kv_unroll * spt):
            j = it * kv_unroll + n // spt
            j_next = jnp.minimum(it * kv_unroll + (n + 2) // spt, n_kv - 1)
            sc_next = scores(j_next, (n + 2) % spt)
            ap_next = softmax_stage(sc, ms, (n + 1) % spt)
            pv_stage(j, n % spt, ap)
            sc, ap = sc_next, ap_next
        for u in range(hp):
            s_ref[u] = sc[u][0]
            p_ref[u] = ap[u][1]
        return tuple(ms) + tuple(a for a, _ in ap) + tuple(m for _, m in sc)

    lax.fori_loop(0, n_kv // kv_unroll, body,
                  tuple(ms) + tuple(a for a, _ in ap) + tuple(m for _, m in sc))
    _attn_finalize(acc_ref, o_ref, tq)


def _attn_finalize(acc_ref, o_ref, tq):
    pad = jnp.zeros((LANES - HEAD_DIM, tq), F32)
    for g in range(GQA_GROUP):
        acc = acc_ref[g]
        o = acc[0:HEAD_DIM] / acc[HEAD_DIM:HEAD_DIM + 1, :]
        o_ref[0, :, g * LANES:(g + 1) * LANES] = jnp.concatenate([o, pad], axis=0).T.astype(BF16)


def _attn_bounded_body(qt_ref, k_ref, vt_ref, o_ref, acc_ref, s_ref, p_ref, *, tq, tk, n_kv, kv_unroll,
                       side_work=None, side_carry=0):
    hp = HEADS_PER_STEP
    spt = GQA_GROUP // hp
    acc_ref[...] = jnp.zeros(acc_ref.shape, F32)

    def scores(j, sp):
        kt = k_ref[0, pl.ds(pl.multiple_of(j * tk, tk), tk), :]
        return [jnp.dot(kt, qt_ref[0, (sp * hp + u) * LANES:(sp * hp + u + 1) * LANES, :],
                        preferred_element_type=F32) for u in range(hp)]

    def probs(sc):
        return [jnp.exp2(s).astype(BF16) for s in sc]

    def pv_stage(j, sp, ps):
        vt = vt_ref[0, 0:PV_ROWS, pl.ds(pl.multiple_of(j * tk, tk), tk)]
        for u, p in enumerate(ps):
            acc_ref[sp * hp + u] += jnp.dot(vt, p, preferred_element_type=F32)

    ps = probs(scores(0, 0))
    sc = scores(min(1 // spt, n_kv - 1), 1 % spt)
    for u in range(hp):
        s_ref[u] = sc[u]
        p_ref[u] = ps[u]

    def body(it, carry):
        ps = [p_ref[u] for u in range(hp)]
        sc = [s_ref[u] for u in range(hp)]
        for n in range(kv_unroll * spt):
            j = it * kv_unroll + n // spt
            j_next = jnp.minimum(it * kv_unroll + (n + 2) // spt, n_kv - 1)
            sc_next = scores(j_next, (n + 2) % spt)
            ps_next = probs(sc)
            pv_stage(j, n % spt, ps)
            sc, ps = sc_next, ps_next
        for u in range(hp):
            s_ref[u] = sc[u]
            p_ref[u] = ps[u]
        return carry if side_work is None else side_work(it, carry)

    carry = lax.fori_loop(0, n_kv // kv_unroll, body, side_carry)
    _attn_finalize(acc_ref, o_ref, tq)
    return carry


def _attention(qt, k, vt, B, S):
    tq = min(TQ, S)
    tk = min(TK, S)
    gw = GQA_GROUP * LANES
    return pl.pallas_call(
        functools.partial(_attn_kernel, tq=tq, tk=tk, n_kv=S // tk,
                          kv_unroll=math.gcd(S // tk, KV_UNROLL)),
        grid=(B, N_KV_HEADS, S // tq),
        in_specs=[
            pl.BlockSpec((1, gw, tq), lambda b, h, i: (b, h, i)),
            pl.BlockSpec((1, S, LANES), lambda b, h, i: (b, 0, h)),
            pl.BlockSpec((1, LANES, S), lambda b, h, i: (b, h, 0)),
        ],
        out_specs=pl.BlockSpec((1, tq, gw), lambda b, h, i: (b, i, h)),
        out_shape=jax.ShapeDtypeStruct((B, S, N_Q_HEADS * LANES), BF16),
        scratch_shapes=[pltpu.VMEM((GQA_GROUP, PV_ROWS, tq), F32),
                        pltpu.VMEM((HEADS_PER_STEP, tk, tq), F32),
                        pltpu.VMEM((HEADS_PER_STEP, tk, tq), BF16)],
        compiler_params=_cparams(("parallel", "parallel", "parallel")),
        name="attention",
    )(qt, k, vt)


def _scan_chunk(a, b, h_in, reverse):
    n = a.shape[0]
    n_groups = n // SUBLANES
    a = a.reshape(n_groups, SUBLANES, LANES)
    b = b.reshape(n_groups, SUBLANES, LANES)
    sub = lax.broadcasted_iota(jnp.int32, a.shape, 1)
    d = 1
    while d < SUBLANES:
        if reverse:
            keep = sub < SUBLANES - d
            shift = SUBLANES - d
        else:
            keep = sub >= d
            shift = d
        a_sh = jnp.where(keep, pltpu.roll(a, shift, 1), 1.0)
        b_sh = jnp.where(keep, pltpu.roll(b, shift, 1), 0.0)
        b = a * b_sh + b
        a = a * a_sh
        d *= 2
    a = a.reshape(n, LANES)
    b = b.reshape(n, LANES)
    order = range(n_groups - 1, -1, -1) if reverse else range(n_groups)
    edge = h_in
    out = [None] * n_groups
    for v in order:
        rows = slice(v * SUBLANES, (v + 1) * SUBLANES)
        hv = b[rows] + a[rows] * jnp.broadcast_to(edge, (SUBLANES, LANES))
        out[v] = hv
        edge = hv[0:1] if reverse else hv[SUBLANES - 1:SUBLANES]
    return jnp.concatenate(out, axis=0), edge


def _lru_pad_input(u_ref, up_ref, S):
    zeros = jnp.zeros((SUBLANES, LANES), F32)
    up_ref[0:SUBLANES, :] = zeros
    up_ref[S + SUBLANES:S + 2 * SUBLANES, :] = zeros
    up_ref[SUBLANES:S + SUBLANES, :] = u_ref[0]


def _lru_gates(up_ref, cw_ref, cb_ref, w_ref, bias_ref, lam_ref, t0, tc, d):
    cw = cw_ref[...]
    xc = cb_ref[...]
    for j in range(CONV_W):
        xc = xc + up_ref[pl.ds(t0 + SUBLANES + j - CONV_PAD_L, tc), :] * cw[j:j + 1, :]
    gw = 2 * LANES
    g = jnp.dot(xc.astype(BF16), w_ref[0, :, d * gw:(d + 1) * gw],
                preferred_element_type=F32) + bias_ref[0, :, d * gw:(d + 1) * gw]
    r = jax.nn.sigmoid(g[:, :LANES])
    i = jax.nn.sigmoid(g[:, LANES:])
    a = jnp.exp(-LRU_C * r * jax.nn.softplus(-lam_ref[d:d + 1, :]))
    y = 1.0 - a * a
    b = jnp.where(y > 0.0, y * lax.rsqrt(y), 0.0) * i * xc
    return a, b


def _lru_kernel(u_ref, gate_ref, cw_ref, cb_ref, w_ref, bias_ref, lam_ref, o_ref,
                up_ref, hf_ref, *, S, tc):
    _lru_pad_input(u_ref, up_ref, S)
    n_chunks = S // tc
    params = (up_ref, cw_ref, cb_ref, w_ref, bias_ref, lam_ref)

    def fwd(c, h):
        t0 = pl.multiple_of(c * tc, tc)
        hc, h_last = _scan_chunk(*_lru_gates(*params, t0, tc, 0), h, False)
        hf_ref[pl.ds(t0, tc), :] = hc
        return h_last

    lax.fori_loop(0, n_chunks, fwd, jnp.zeros((1, LANES), F32))

    def bwd(ci, h):
        t0 = pl.multiple_of((n_chunks - 1 - ci) * tc, tc)
        hc, h_last = _scan_chunk(*_lru_gates(*params, t0, tc, 1), h, True)
        gate = gate_ref[0, pl.ds(t0, tc), :]
        o_ref[0, pl.ds(t0, tc), :] = (hf_ref[pl.ds(t0, tc), :] + hc) * jax.nn.gelu(gate)
        return h_last

    lax.fori_loop(0, n_chunks, bwd, jnp.zeros((1, LANES), F32))


def _block_diag_pairs(w):
    nb, bw, _ = w.shape
    w = w.reshape(nb // 2, 2, bw, bw)
    z = jnp.zeros_like(w[:, 0])
    top = jnp.concatenate([w[:, 0], z], axis=-1)
    bot = jnp.concatenate([z, w[:, 1]], axis=-1)
    return jnp.concatenate([top, bot], axis=-2)


def _lru_operands(conv_w, conv_b, wa, ba, wi, bi, lam):
    C = conv_b.shape[0]
    nc = C // LANES
    w = jnp.concatenate([_block_diag_pairs(wa[0]), _block_diag_pairs(wi[0]),
                         _block_diag_pairs(wa[1]), _block_diag_pairs(wi[1])], axis=-1).astype(BF16)
    bias = jnp.stack([ba[0].reshape(nc, LANES), bi[0].reshape(nc, LANES),
                      ba[1].reshape(nc, LANES), bi[1].reshape(nc, LANES)], axis=1)
    return conv_w, conv_b.reshape(1, C), w, bias.reshape(nc, 1, 4 * LANES), lam


def _lru_specs(S, unit):
    seq = lambda *g: (unit(*g)[0], 0, unit(*g)[1])
    chan = lambda *g: (0, unit(*g)[1])
    blk = lambda *g: (unit(*g)[1], 0, 0)
    in_specs = [
        pl.BlockSpec((1, S, LANES), seq),
        pl.BlockSpec((1, S, LANES), seq),
        pl.BlockSpec((CONV_W, LANES), chan),
        pl.BlockSpec((1, LANES), chan),
        pl.BlockSpec((1, LANES, 4 * LANES), blk),
        pl.BlockSpec((1, 1, 4 * LANES), blk),
        pl.BlockSpec((2, LANES), chan),
    ]
    return in_specs, pl.BlockSpec((1, S, LANES), seq)


def _lru(lru_x, lru_gate, lru_ops, B, S):
    C = lru_x.shape[-1]
    tc = min(TC_LRU, S)
    in_specs, out_spec = _lru_specs(S, lambda b, c: (b, c))
    return pl.pallas_call(
        functools.partial(_lru_kernel, S=S, tc=tc),
        grid=(B, C // LANES),
        in_specs=in_specs,
        out_specs=out_spec,
        out_shape=jax.ShapeDtypeStruct((B, S, C), F32),
        scratch_shapes=[
            pltpu.VMEM((S + 2 * SUBLANES, LANES), F32),
            pltpu.VMEM((S, LANES), F32),
        ],
        compiler_params=_cparams(("parallel", "parallel")),
        name="rglru",
    )(lru_x, lru_gate, *lru_ops)


def _attn_lru_kernel(qt_ref, k_ref, vt_ref, u_ref, gate_ref, cw_ref, cb_ref, w_ref, bias_ref, lam_ref,
                     o_ref, lru_ref, acc_ref, s_ref, p_ref, up_ref, hf_ref, hb_ref, carry_ref,
                     *, tq, tk, n_kv, kv_unroll, S, steps_per_unit):
    n_it = n_kv // kv_unroll
    n_chunks = steps_per_unit * n_it
    tc = S // n_chunks
    step = ((pl.program_id(0) * pl.num_programs(1) + pl.program_id(1)) * pl.num_programs(2)
            + pl.program_id(2))
    local = step % steps_per_unit

    @pl.when(local == 0)
    def _():
        _lru_pad_input(u_ref, up_ref, S)
        hf_ref[...] = jnp.zeros(hf_ref.shape, F32)
        hb_ref[...] = jnp.zeros(hb_ref.shape, F32)
        carry_ref[...] = jnp.zeros(carry_ref.shape, F32)

    params = (up_ref, cw_ref, cb_ref, w_ref, bias_ref, lam_ref)

    def scan_slice(it, carry):
        h_f, h_b = carry
        t = local * n_it + it
        tf = pl.multiple_of(t * tc, tc)
        tb = pl.multiple_of((n_chunks - 1 - t) * tc, tc)
        hf, h_f = _scan_chunk(*_lru_gates(*params, tf, tc, 0), h_f, False)
        hb, h_b = _scan_chunk(*_lru_gates(*params, tb, tc, 1), h_b, True)
        hf_ref[pl.ds(tf, tc), :] = hf
        hb_ref[pl.ds(tb, tc), :] = hb
        for t0 in (tf, tb):
            rows = pl.ds(t0, tc)
            lru_ref[0, rows, :] = (hf_ref[rows, :] + hb_ref[rows, :]) * jax.nn.gelu(gate_ref[0, rows, :])
        return h_f, h_b

    h_f, h_b = _attn_bounded_body(
        qt_ref, k_ref, vt_ref, o_ref, acc_ref, s_ref, p_ref, tq=tq, tk=tk, n_kv=n_kv,
        kv_unroll=kv_unroll, side_work=scan_slice, side_carry=(carry_ref[0:1, :], carry_ref[1:2, :]))
    carry_ref[0:1, :] = h_f
    carry_ref[1:2, :] = h_b


def _mixers(qt, k, vt, lru_x, lru_gate, lru_ops, score_bound, B, S):
    C = lru_x.shape[-1]
    nc = C // LANES
    tq = min(TQ, S)
    tk = min(TK, S)
    n_q = S // tq
    n_kv = S // tk
    kv_unroll = math.gcd(n_kv, KV_UNROLL)
    gw = GQA_GROUP * LANES
    n_steps = B * N_KV_HEADS * n_q
    steps_per_unit = n_steps // (B * nc)
    n_chunks = steps_per_unit * (n_kv // kv_unroll)
    assert n_steps % (B * nc) == 0 and S % (n_chunks * SUBLANES) == 0

    def unit(b, h, i):
        u = ((b * N_KV_HEADS + h) * n_q + i) // steps_per_unit
        return u // nc, u % nc

    lru_in_specs, lru_out_spec = _lru_specs(S, unit)

    def fused(qt, k, vt, lru_x, lru_gate, *ops):
        return tuple(pl.pallas_call(
            functools.partial(_attn_lru_kernel, tq=tq, tk=tk, n_kv=n_kv, kv_unroll=kv_unroll, S=S,
                              steps_per_unit=steps_per_unit),
            grid=(B, N_KV_HEADS, n_q),
            in_specs=[
                pl.BlockSpec((1, gw, tq), lambda b, h, i: (b, h, i)),
                pl.BlockSpec((1, S, LANES), lambda b, h, i: (b, 0, h)),
                pl.BlockSpec((1, LANES, S), lambda b, h, i: (b, h, 0)),
            ] + lru_in_specs,
            out_specs=[pl.BlockSpec((1, tq, gw), lambda b, h, i: (b, i, h)), lru_out_spec],
            out_shape=[jax.ShapeDtypeStruct((B, S, N_Q_HEADS * LANES), BF16),
                       jax.ShapeDtypeStruct((B, S, C), F32)],
            scratch_shapes=[pltpu.VMEM((GQA_GROUP, PV_ROWS, tq), F32),
                            pltpu.VMEM((HEADS_PER_STEP, tk, tq), F32),
                            pltpu.VMEM((HEADS_PER_STEP, tk, tq), BF16),
                            pltpu.VMEM((S + 2 * SUBLANES, LANES), F32),
                            pltpu.VMEM((S, LANES), F32),
                            pltpu.VMEM((S, LANES), F32),
                            pltpu.VMEM((SUBLANES, LANES), F32)],
            compiler_params=pltpu.CompilerParams(
                dimension_semantics=("arbitrary", "arbitrary", "arbitrary"),
                vmem_limit_bytes=EXPERT_VMEM_LIMIT),
            name="attention_bounded_rglru",
        )(qt, k, vt, lru_x, lru_gate, *ops))

    def separate(qt, k, vt, lru_x, lru_gate, *ops):
        return _attention(qt, k, vt, B, S), _lru(lru_x, lru_gate, ops, B, S)

    return lax.cond(score_bound <= SAFE_SCORE_LOG2, fused, separate,
                    qt, k, vt, lru_x, lru_gate, *lru_ops)


def _rows_to_slabs(ref, x):
    n = x.shape[0]
    for s in range(SUBLANES):
        ref[pl.ds(s, n, stride=SUBLANES), :] = x[:, s * LANES:(s + 1) * LANES]


def _slabs_to_rows(ref, n):
    return jnp.concatenate([ref[pl.ds(s, n, stride=SUBLANES), :] for s in range(SUBLANES)], axis=1)


def _slab(ref, r):
    return ref.at[pl.ds(pl.multiple_of(r * SUBLANES, SUBLANES), SUBLANES)]


def _outproj_kernel(a_ref, l_ref, x_ref, ag_ref, lg_ref, wa_ref, wl_ref, g2_ref,
                    wrh_ref, wrl_ref, br_ref, tri_ref,
                    x1_ref, xn3_ref, route_ref, gates_ref, cnt_ref, carry_ref, *, attn_w, lru_w):
    step = pl.program_id(0)

    @pl.when(step == 0)
    def _():
        carry_ref[...] = jnp.zeros_like(carry_ref)

    a = a_ref[...].astype(F32)
    ams = jnp.sum(a * a, axis=-1, keepdims=True) * (1.0 / attn_w)
    an = a * lax.rsqrt(ams + NORM_EPS) * ag_ref[...]
    l = l_ref[...]
    lms = jnp.sum(l * l, axis=-1, keepdims=True) * (1.0 / lru_w)
    ln = l * lax.rsqrt(lms + NORM_EPS) * lg_ref[...]
    mix = (jnp.dot(an.astype(BF16), wa_ref[...], preferred_element_type=F32)
           + jnp.dot(ln.astype(BF16), wl_ref[...], preferred_element_type=F32))
    x1 = x_ref[...] + mix
    x1_ref[...] = x1
    ms = jnp.mean(x1 * x1, axis=-1, keepdims=True)
    xn = x1 * lax.rsqrt(ms + NORM_EPS) * g2_ref[...]
    _rows_to_slabs(xn3_ref, xn)

    hi = xn.astype(BF16)
    lo = (xn - hi.astype(F32)).astype(BF16)
    logits = (jnp.dot(hi, wrh_ref[...], preferred_element_type=F32)
              + jnp.dot(lo, wrh_ref[...], preferred_element_type=F32)
              + jnp.dot(hi, wrl_ref[...], preferred_element_type=F32)) + br_ref[...]
    lane = lax.broadcasted_iota(jnp.int32, logits.shape, 1)
    neg = -jnp.inf
    work = jnp.where(lane < N_EXPERTS, logits, neg)
    sel = jnp.zeros(logits.shape, F32)
    idxs, vals = [], []
    for _ in range(TOP_K):
        m = jnp.max(work, axis=1, keepdims=True)
        idx = jnp.min(jnp.where(work == m, lane, LANES), axis=1, keepdims=True)
        hit = lane == idx
        work = jnp.where(hit, neg, work)
        sel = sel + hit.astype(F32)
        idxs.append(idx)
        vals.append(m)
    es = [jnp.exp(v - vals[0]) for v in vals]
    den = es[0] + es[1] + es[2] + es[3]

    prefix = jnp.dot(tri_ref[...], sel.astype(BF16), preferred_element_type=F32) + carry_ref[...]
    carry_ref[...] = carry_ref[...] + jnp.sum(sel, axis=0, keepdims=True)
    cnt_ref[...] = carry_ref[...]

    route = jnp.zeros(logits.shape, jnp.int32)
    gates = jnp.zeros(logits.shape, F32)
    for k in range(TOP_K):
        rank = jnp.sum(jnp.where(lane == idxs[k], prefix, 0.0), axis=1, keepdims=True).astype(jnp.int32)
        route = jnp.where(lane == k, idxs[k], route)
        route = jnp.where(lane == TOP_K + k, rank, route)
        gates = jnp.where(lane == k, es[k] / den, gates)
    route_ref[...] = route
    gates_ref[...] = gates


def _outproj_router(attn, lru, x2, attn_out_g, lru_out_g, w_out, norm2_g, w_router, b_router):
    T, D = x2.shape
    lru_w = lru.shape[-1]
    ts = min(TS_OUT, T)
    wa = w_out[:ATTN_W].reshape(N_Q_HEADS, HEAD_DIM, D)
    wa = jnp.pad(wa, ((0, 0), (0, LANES - HEAD_DIM), (0, 0))).reshape(N_Q_HEADS * LANES, D).astype(BF16)
    wl = w_out[ATTN_W:].astype(BF16)
    ag = _pad_heads(attn_out_g.reshape(1, ATTN_W), N_Q_HEADS)
    wr = jnp.pad(w_router, ((0, 0), (0, LANES - N_EXPERTS)))
    wrh = wr.astype(BF16)
    wrl = (wr - wrh.astype(F32)).astype(BF16)
    br = jnp.pad(b_router.reshape(1, N_EXPERTS), ((0, 0), (0, LANES - N_EXPERTS)))
    tri = (jnp.arange(ts)[:, None] > jnp.arange(ts)[None, :]).astype(BF16)
    const = lambda i: (0, 0)
    tok = lambda i: (i, 0)
    aw = N_Q_HEADS * LANES
    return pl.pallas_call(
        functools.partial(_outproj_kernel, attn_w=ATTN_W, lru_w=lru_w),
        grid=(T // ts,),
        in_specs=[
            pl.BlockSpec((ts, aw), tok),
            pl.BlockSpec((ts, lru_w), tok),
            pl.BlockSpec((ts, D), tok),
            pl.BlockSpec((1, aw), const),
            pl.BlockSpec((1, lru_w), const),
            pl.BlockSpec((aw, D), const),
            pl.BlockSpec((lru_w, D), const),
            pl.BlockSpec((1, D), const),
            pl.BlockSpec((D, LANES), const),
            pl.BlockSpec((D, LANES), const),
            pl.BlockSpec((1, LANES), const),
            pl.BlockSpec((ts, ts), const),
        ],
        out_specs=[
            pl.BlockSpec((ts, D), tok),
            pl.BlockSpec((ts * SUBLANES, LANES), tok),
            pl.BlockSpec((ts, LANES), tok),
            pl.BlockSpec((ts, LANES), tok),
            pl.BlockSpec((1, LANES), const),
        ],
        out_shape=[
            jax.ShapeDtypeStruct((T, D), F32),
            jax.ShapeDtypeStruct((T * SUBLANES, LANES), F32),
            jax.ShapeDtypeStruct((T, LANES), jnp.int32),
            jax.ShapeDtypeStruct((T, LANES), F32),
            jax.ShapeDtypeStruct((1, LANES), F32),
        ],
        scratch_shapes=[pltpu.VMEM((1, LANES), F32)],
        compiler_params=_cparams(("arbitrary",)),
        name="outproj_router",
    )(attn, lru, x2, ag, lru_out_g.reshape(1, lru_w), wa, wl, norm2_g.reshape(1, D),
      wrh, wrl, br, tri)


def _dest_kernel(route_ref, pstart_ref, dest_ref):
    route = route_ref[...]
    lane = lax.broadcasted_iota(jnp.int32, route.shape, 1)
    pstart = pstart_ref[...]
    dest = jnp.zeros(route.shape, jnp.int32)
    for k in range(TOP_K):
        start = jnp.sum(jnp.where(lane == route[:, k:k + 1], pstart, 0.0), axis=1, keepdims=True)
        dest = jnp.where(lane == k, start.astype(jnp.int32) + route[:, TOP_K + k:TOP_K + k + 1], dest)
    dest_ref[...] = dest


def _dest_rows(route, pstart):
    T = route.shape[0]
    ts = min(TS_OUT, T)
    row = jnp.pad(pstart.astype(F32).reshape(1, N_EXPERTS), ((0, 0), (0, LANES - N_EXPERTS)))
    dest = pl.pallas_call(
        _dest_kernel,
        grid=(T // ts,),
        in_specs=[pl.BlockSpec((ts, LANES), lambda i: (i, 0)),
                  pl.BlockSpec((1, LANES), lambda i: (0, 0))],
        out_specs=pl.BlockSpec((ts, LANES), lambda i: (i, 0)),
        out_shape=jax.ShapeDtypeStruct((T, LANES), jnp.int32),
        compiler_params=_cparams(("parallel",)),
        name="dest_rows",
    )(route, row)
    return dest[:, :TOP_K].reshape(T * TOP_K)


def _dispatch_kernel(fill_ref, dest_ref, x_ref, out_hbm, zero_ref, sem, zero_sem, *, ts, n_blocks):
    block_slabs = ROW_BLOCK * SUBLANES

    def fill_copy(b):
        off = pl.multiple_of(b * block_slabs, block_slabs)
        return pltpu.make_async_copy(zero_ref, out_hbm.at[pl.ds(off, block_slabs)], zero_sem)

    @pl.when(pl.program_id(0) == 0)
    def _():
        zero_ref[...] = jnp.zeros(zero_ref.shape, F32)

        def start(b, carry):
            @pl.when(fill_ref[b] != 0)
            def _():
                fill_copy(b).start()
            return carry

        def wait(b, carry):
            @pl.when(fill_ref[b] != 0)
            def _():
                fill_copy(b).wait()
            return carry

        lax.fori_loop(0, n_blocks, start, 0)
        lax.fori_loop(0, n_blocks, wait, 0)

    def issue(i, carry):
        for j in range(ISSUE_UNROLL):
            r = i * ISSUE_UNROLL + j
            for k in range(TOP_K):
                d = dest_ref[r * TOP_K + k]
                pltpu.make_async_copy(_slab(x_ref, r), _slab(out_hbm, d), sem).start(priority=k % 2)
        return carry

    lax.fori_loop(0, ts // ISSUE_UNROLL, issue, 0)
    for k in range(TOP_K):
        pltpu.make_async_copy(x_ref, out_hbm.at[pl.ds(0, ts * SUBLANES)], sem).wait()


def _dispatch(xn_slabs, fill, dest_flat, n_rows):
    T = xn_slabs.shape[0] // SUBLANES
    ts = min(TS_DISP, T)
    grid_spec = pltpu.PrefetchScalarGridSpec(
        num_scalar_prefetch=1,
        grid=(T // ts,),
        in_specs=[
            pl.BlockSpec((ts * TOP_K,), lambda i, fl: (i,), memory_space=pltpu.SMEM),
            pl.BlockSpec((ts * SUBLANES, LANES), lambda i, fl: (i, 0)),
        ],
        out_specs=pl.BlockSpec(memory_space=pl.ANY),
        scratch_shapes=[pltpu.VMEM((ROW_BLOCK * SUBLANES, LANES), F32),
                        pltpu.SemaphoreType.DMA, pltpu.SemaphoreType.DMA],
    )
    return pl.pallas_call(
        functools.partial(_dispatch_kernel, ts=ts, n_blocks=n_rows // ROW_BLOCK),
        grid_spec=grid_spec,
        out_shape=jax.ShapeDtypeStruct((n_rows * SUBLANES, LANES), xn_slabs.dtype),
        compiler_params=_cparams(("arbitrary",)),
        name="dispatch",
    )(fill, dest_flat, xn_slabs)


def _expert_kernel(be_ref, na_ref, nxt_ref, x_ref, wg_hbm, bg_ref, wu_hbm, bu_ref, wd_hbm, bd_ref,
                   y_ref, stage_ref, wb_ref, slot_ref, sems):
    i = pl.program_id(0)
    e = be_ref[i]
    w_hbm = (wg_hbm, wu_hbm, wd_hbm)

    def fetch(expert, slot, m):
        return pltpu.make_async_copy(w_hbm[m].at[expert], stage_ref.at[slot, m], sems.at[slot, m])

    @pl.when(i == 0)
    def _():
        slot_ref[0] = 0
        for m in range(3):
            fetch(e, 0, m).start()

    active = i < na_ref[0]
    first = jnp.logical_or(i == 0, e != be_ref[jnp.maximum(i - 1, 0)])

    @pl.when(jnp.logical_and(active, first))
    def _():
        slot = slot_ref[0]
        for m in range(3):
            fetch(e, slot, m).wait()
            wb_ref[m] = stage_ref[slot, m].astype(BF16)

        @pl.when(nxt_ref[i] >= 0)
        def _():
            for m in range(3):
                fetch(nxt_ref[i], 1 - slot, m).start()

        slot_ref[0] = 1 - slot

    @pl.when(active)
    def _():
        x = _slabs_to_rows(x_ref, ROW_BLOCK).astype(BF16)
        g = jnp.dot(x, wb_ref[0], preferred_element_type=F32) + bg_ref[0]
        u = jnp.dot(x, wb_ref[1], preferred_element_type=F32) + bu_ref[0]
        g = jnp.minimum(g, SWIGLU_LIMIT)
        u = jnp.clip(u, -SWIGLU_LIMIT, SWIGLU_LIMIT)
        glu = g * jax.nn.sigmoid(SWIGLU_ALPHA * g)
        y = jnp.dot(((u + 1.0) * glu).astype(BF16), wb_ref[2], preferred_element_type=F32) + bd_ref[0]
        _rows_to_slabs(y_ref, y)


def _experts(x_rows, block_e, n_active, next_e, w_gate, b_gate, w_up, b_up, w_down, b_down):
    E, D, FF = w_gate.shape
    assert D == FF, "the three expert matrices share one staging shape"
    block_slabs = ROW_BLOCK * SUBLANES
    n_blocks = x_rows.shape[0] // block_slabs

    def row_map(i, be, na, nx):
        return (jnp.minimum(i, na[0] - 1), 0)

    def b_map(i, be, na, nx):
        return (be[jnp.minimum(i, na[0] - 1)], 0, 0)

    grid_spec = pltpu.PrefetchScalarGridSpec(
        num_scalar_prefetch=3,
        grid=(n_blocks,),
        in_specs=[
            pl.BlockSpec((block_slabs, LANES), row_map),
            pl.BlockSpec(memory_space=pl.ANY),
            pl.BlockSpec((1, 1, FF), b_map),
            pl.BlockSpec(memory_space=pl.ANY),
            pl.BlockSpec((1, 1, FF), b_map),
            pl.BlockSpec(memory_space=pl.ANY),
            pl.BlockSpec((1, 1, D), b_map),
        ],
        out_specs=pl.BlockSpec((block_slabs, LANES), row_map),
        scratch_shapes=[
            pltpu.VMEM((2, 3, D, FF), F32),
            pltpu.VMEM((3, D, FF), BF16),
            pltpu.SMEM((1,), jnp.int32),
            pltpu.SemaphoreType.DMA((2, 3)),
        ],
    )
    return pl.pallas_call(
        _expert_kernel,
        grid_spec=grid_spec,
        out_shape=jax.ShapeDtypeStruct(x_rows.shape, F32),
        input_output_aliases={3: 0},
        compiler_params=pltpu.CompilerParams(dimension_semantics=("arbitrary",),
                                             vmem_limit_bytes=EXPERT_VMEM_LIMIT),
        name="experts",
    )(block_e, n_active, next_e, x_rows, w_gate, b_gate.reshape(E, 1, FF), w_up,
      b_up.reshape(E, 1, FF), w_down, b_down.reshape(E, 1, D))


def _combine_kernel(dest_ref, dest_next_ref, y_hbm, x1_ref, gates_ref, fg_ref, o_ref, bufs, sems,
                    *, ts, n_steps):
    i = pl.program_id(0)
    slot = i % 2

    def gather_tile(d_ref, s):
        def issue(it, carry):
            for j in range(ISSUE_UNROLL):
                r = it * ISSUE_UNROLL + j
                for k in range(TOP_K):
                    d = d_ref[r * TOP_K + k]
                    pltpu.make_async_copy(_slab(y_hbm, d), _slab(bufs.at[s, k], r),
                                          sems.at[s]).start(priority=k % 2)
            return carry

        lax.fori_loop(0, ts // ISSUE_UNROLL, issue, 0)

    @pl.when(i == 0)
    def _():
        gather_tile(dest_ref, 0)

    @pl.when(i + 1 < n_steps)
    def _():
        gather_tile(dest_next_ref, 1 - slot)

    for k in range(TOP_K):
        pltpu.make_async_copy(y_hbm.at[pl.ds(0, ts * SUBLANES)], bufs.at[slot, k], sems.at[slot]).wait()

    acc = x1_ref[...]
    gates = gates_ref[...]
    for k in range(TOP_K):
        acc = acc + _slabs_to_rows(bufs.at[slot, k], ts) * gates[:, k:k + 1]
    ms = jnp.mean(acc * acc, axis=-1, keepdims=True)
    o_ref[...] = acc * lax.rsqrt(ms + NORM_EPS) * fg_ref[...]


def _combine(y_rows, dest_flat, x1, gates, final_g):
    T, D = x1.shape
    ts = min(TS_COMB, T)
    n_steps = T // ts
    tok = lambda i: (i, 0)
    return pl.pallas_call(
        functools.partial(_combine_kernel, ts=ts, n_steps=n_steps),
        grid=(n_steps,),
        in_specs=[
            pl.BlockSpec((ts * TOP_K,), lambda i: (i,), memory_space=pltpu.SMEM),
            pl.BlockSpec((ts * TOP_K,), lambda i: (jnp.minimum(i + 1, n_steps - 1),),
                         memory_space=pltpu.SMEM),
            pl.BlockSpec(memory_space=pl.ANY),
            pl.BlockSpec((ts, D), tok),
            pl.BlockSpec((ts, LANES), tok),
            pl.BlockSpec((1, D), lambda i: (0, 0)),
        ],
        out_specs=pl.BlockSpec((ts, D), tok),
        out_shape=jax.ShapeDtypeStruct((T, D), F32),
        scratch_shapes=[pltpu.VMEM((2, TOP_K, ts * SUBLANES, LANES), F32),
                        pltpu.SemaphoreType.DMA((2,))],
        compiler_params=_cparams(("arbitrary",)),
        name="combine",
    )(dest_flat, dest_flat, y_rows, x1, gates, final_g.reshape(1, D))


def kernel(x, norm1_g, w_in, q_norm_g, k_norm_g, conv_w, conv_b, lru_wa, lru_ba, lru_wi, lru_bi,
           lru_lam, attn_out_g, lru_out_g, w_out, norm2_g, w_router, b_router, w_gate, b_gate,
           w_up, b_up, w_down, b_down, final_g):
    B, S, D = x.shape
    T = B * S
    assert w_in.shape[0] == 1, "single-layer trunk: the final norm is fused into the layer's combine"
    x2 = x.reshape(T, D)
    for l in range(1):
        qt, k, vt, lru_x, lru_gate = _inproj(x2, norm1_g[l], w_in[l], q_norm_g[l], k_norm_g[l], S)
        score_bound = (HEAD_DIM * Q_SCALE * jnp.max(jnp.abs(q_norm_g[l]))
                       * jnp.max(jnp.abs(k_norm_g[l])))
        lru_ops = _lru_operands(conv_w[l], conv_b[l], lru_wa[l], lru_ba[l], lru_wi[l], lru_bi[l],
                                lru_lam[l])
        attn, lru = _mixers(qt, k.reshape(B, S, -1), vt, lru_x.reshape(B, S, -1),
                            lru_gate.reshape(B, S, -1), lru_ops, score_bound, B, S)
        x1, xn3, route, gates, cnt = _outproj_router(
            attn.reshape(T, -1), lru.reshape(T, -1), x2, attn_out_g[l], lru_out_g[l], w_out[l],
            norm2_g[l], w_router[l], b_router[l])

        counts = cnt[0, :N_EXPERTS].astype(jnp.int32)
        padded = ((counts + ROW_BLOCK - 1) // ROW_BLOCK) * ROW_BLOCK
        pend = jnp.cumsum(padded)
        pstart = (pend - padded).astype(jnp.int32)
        n_rows = T * TOP_K + N_EXPERTS * ROW_BLOCK
        block_start = jnp.arange(n_rows // ROW_BLOCK, dtype=jnp.int32) * ROW_BLOCK
        block_e = jnp.sum((pend[None, :] <= block_start[:, None]).astype(jnp.int32), axis=1)
        block_e = jnp.minimum(block_e, N_EXPERTS - 1)
        n_active = (pend[-1:] // ROW_BLOCK).astype(jnp.int32)
        fill = jnp.logical_or(block_start + ROW_BLOCK == pend[block_e],
                              block_start >= pend[-1]).astype(jnp.int32)

        next_block = pend[block_e] // ROW_BLOCK
        next_e = jnp.where(next_block < n_active[0],
                           block_e[jnp.minimum(next_block, block_e.shape[0] - 1)], -1).astype(jnp.int32)

        dest_flat = _dest_rows(route, pstart)
        x_rows = _dispatch(xn3, fill, dest_flat, n_rows)
        y_rows = _experts(x_rows, block_e, n_active, next_e, w_gate[l], b_gate[l], w_up[l], b_up[l],
                          w_down[l], b_down[l])
        x2 = _combine(y_rows, dest_flat, x1, gates, final_g)
    return x2.reshape(B, S, D)
```

```python
import functools
import math

import jax
import jax.numpy as jnp
from jax import lax
from jax.experimental import pallas as pl
from jax.experimental.pallas import tpu as pltpu

F32 = jnp.float32
BF16 = jnp.bfloat16

GRID_W = 64
HEAD_DIM = 64
N_Q_HEADS = 8
N_KV_HEADS = 2
GQA_GROUP = N_Q_HEADS // N_KV_HEADS
ATTN_W = N_Q_HEADS * HEAD_DIM
KV_W = N_KV_HEADS * HEAD_DIM
LRU_BLOCKS = 8
LRU_C = 8.0
CONV_W = 4
CONV_PAD_L = 2
ROPE_THETA = 10000.0
ROPE_HALF = HEAD_DIM // 2
ROPE_M = ROPE_HALF // 2
N_EXPERTS = 32
TOP_K = 4
SWIGLU_ALPHA = 1.702
SWIGLU_LIMIT = 7.0
NORM_EPS = 1e-5
QK_EPS = 1e-6
LOG2_E = 1.4426950408889634
Q_SCALE = HEAD_DIM ** -0.5 * LOG2_E
SAFE_SCORE_LOG2 = 96.0

LANES = 128
SUBLANES = 8
BF16_SUBLANES = 16
PV_ROWS = HEAD_DIM + BF16_SUBLANES
VMEM_LIMIT = 48 * 1024 * 1024
EXPERT_VMEM_LIMIT = 56 * 1024 * 1024

TS_IN = 512
TQ = 256
TK = 256
KV_UNROLL = 8
HEADS_PER_STEP = 2
TC_LRU = 512
TS_OUT = 512
TS_DEST = 2048
ROW_BLOCK = 512
TS_DISP = 512
TS_COMB = 256
ISSUE_UNROLL = 8


def _cparams(sem):
    return pltpu.CompilerParams(dimension_semantics=sem, vmem_limit_bytes=VMEM_LIMIT)


def _inproj_kernel(x_ref, g1_ref, wt_ref, w_ref, qg_ref, kg_ref, cos_ref, sin_ref, cost_ref, sint_ref,
                   q_ref, k_ref, v_ref, lx_ref, lg_ref, *, lru_w):
    x = x_ref[...]
    ms = jnp.mean(x * x, axis=-1, keepdims=True)
    xn = (x * lax.rsqrt(ms + NORM_EPS) * g1_ref[...]).astype(BF16)
    ht = lax.dot_general(wt_ref[...], xn, (((1,), (1,)), ((), ())), preferred_element_type=F32)
    h = jnp.dot(xn, w_ref[...], preferred_element_type=F32)

    qw = N_Q_HEADS * LANES
    kw = N_KV_HEADS * LANES
    cost = cost_ref[...]
    sint = sint_ref[...]
    row = lax.broadcasted_iota(jnp.int32, cost.shape, 0)
    first_half_t = (row % ROPE_HALF) < ROPE_M
    qg = qg_ref[...]
    for c in range(N_Q_HEADS):
        sl = slice(c * LANES, (c + 1) * LANES)
        xc = ht[sl]
        hms = jnp.sum(xc * xc, axis=0, keepdims=True) * (1.0 / HEAD_DIM)
        xc = xc * lax.rsqrt(hms + QK_EPS) * qg
        partner = jnp.where(first_half_t, pltpu.roll(xc, LANES - ROPE_M, 0), pltpu.roll(xc, ROPE_M, 0))
        q_ref[0, sl, :] = ((xc * cost + partner * sint) * Q_SCALE).astype(BF16)
    for c in range(N_KV_HEADS):
        sl = slice(c * LANES, (c + 1) * LANES)
        v_ref[0, sl, :] = jnp.where(row >= HEAD_DIM, 1.0, ht[qw + c * LANES: qw + (c + 1) * LANES]).astype(BF16)

    cos = cos_ref[...]
    sin = sin_ref[...]
    lane = lax.broadcasted_iota(jnp.int32, cos.shape, 1)
    first_half = (lane % ROPE_HALF) < ROPE_M
    for c in range(N_KV_HEADS):
        sl = slice(c * LANES, (c + 1) * LANES)
        xc = h[:, sl]
        hms = jnp.sum(xc * xc, axis=-1, keepdims=True) * (1.0 / HEAD_DIM)
        xc = xc * lax.rsqrt(hms + QK_EPS) * kg_ref[...]
        partner = jnp.where(first_half, pltpu.roll(xc, LANES - ROPE_M, 1), pltpu.roll(xc, ROPE_M, 1))
        k_ref[:, sl] = (xc * cos + partner * sin).astype(BF16)
    lx_ref[...] = h[:, kw: kw + lru_w]
    lg_ref[...] = h[:, kw + lru_w: kw + 2 * lru_w]


def _pad_heads(w, n_heads):
    lead = w.shape[:-1]
    w = w.reshape(lead + (n_heads, HEAD_DIM))
    w = jnp.pad(w, [(0, 0)] * len(lead) + [(0, 0), (0, LANES - HEAD_DIM)])
    return w.reshape(lead + (n_heads * LANES,))


def _rope_tables(S):
    t = jnp.arange(S)
    rows = (t // GRID_W).astype(F32)
    cols = (t % GRID_W).astype(F32)
    inv_freq = ROPE_THETA ** (-jnp.arange(ROPE_M, dtype=F32) / ROPE_M)
    ar = rows[:, None] * inv_freq[None, :]
    ac = cols[:, None] * inv_freq[None, :]
    cos = jnp.concatenate([jnp.cos(ar), jnp.cos(ar), jnp.cos(ac), jnp.cos(ac)], axis=-1)
    sin = jnp.concatenate([-jnp.sin(ar), jnp.sin(ar), -jnp.sin(ac), jnp.sin(ac)], axis=-1)
    pad = [(0, 0), (0, LANES - HEAD_DIM)]
    return jnp.pad(cos, pad), jnp.pad(sin, pad)


def _inproj(x2, norm1_g, w_in, q_norm_g, k_norm_g, S):
    T, D = x2.shape
    lru_w = (w_in.shape[1] - ATTN_W - 2 * KV_W) // 2
    o0, o1, o2 = ATTN_W, ATTN_W + KV_W, ATTN_W + 2 * KV_W
    w_t = jnp.concatenate([_pad_heads(w_in[:, :o0], N_Q_HEADS),
                           _pad_heads(w_in[:, o1:o2], N_KV_HEADS)], axis=1).T.astype(BF16)
    w_rest = jnp.concatenate([_pad_heads(w_in[:, o0:o1], N_KV_HEADS), w_in[:, o2:]],
                             axis=1).astype(BF16)
    qg = _pad_heads(q_norm_g.reshape(1, HEAD_DIM), 1).reshape(LANES, 1)
    kg = _pad_heads(k_norm_g.reshape(1, HEAD_DIM), 1)
    cos, sin = _rope_tables(S)
    ts = TS_IN
    n_s = S // ts
    qw, kw = N_Q_HEADS * LANES, N_KV_HEADS * LANES
    const = lambda i: (0, 0)
    tok = lambda i: (i, 0)
    pos = lambda i: (i % n_s, 0)
    pos_t = lambda i: (0, i % n_s)
    tposed = lambda i: (i // n_s, 0, i % n_s)
    return pl.pallas_call(
        functools.partial(_inproj_kernel, lru_w=lru_w),
        grid=(T // ts,),
        in_specs=[
            pl.BlockSpec((ts, D), tok),
            pl.BlockSpec((1, D), const),
            pl.BlockSpec(w_t.shape, const),
            pl.BlockSpec(w_rest.shape, const),
            pl.BlockSpec((LANES, 1), const),
            pl.BlockSpec((1, LANES), const),
            pl.BlockSpec((ts, LANES), pos),
            pl.BlockSpec((ts, LANES), pos),
            pl.BlockSpec((LANES, ts), pos_t),
            pl.BlockSpec((LANES, ts), pos_t),
        ],
        out_specs=[
            pl.BlockSpec((1, qw, ts), tposed),
            pl.BlockSpec((ts, kw), tok),
            pl.BlockSpec((1, kw, ts), tposed),
            pl.BlockSpec((ts, lru_w), tok),
            pl.BlockSpec((ts, lru_w), tok),
        ],
        out_shape=[
            jax.ShapeDtypeStruct((T // S, qw, S), BF16),
            jax.ShapeDtypeStruct((T, kw), BF16),
            jax.ShapeDtypeStruct((T // S, kw, S), BF16),
            jax.ShapeDtypeStruct((T, lru_w), F32),
            jax.ShapeDtypeStruct((T, lru_w), F32),
        ],
        compiler_params=_cparams(("parallel",)),
        name="inproj",
    )(x2, norm1_g.reshape(1, D), w_t, w_rest, qg, kg, cos, sin, cos.T, sin.T)


def _attn_kernel(qt_ref, k_ref, vt_ref, o_ref, acc_ref, s_ref, p_ref, *, tq, tk, n_kv, kv_unroll):
    hp = HEADS_PER_STEP
    spt = GQA_GROUP // hp
    acc_ref[...] = jnp.zeros(acc_ref.shape, F32)

    def scores(j, sp):
        kt = k_ref[0, pl.ds(pl.multiple_of(j * tk, tk), tk), :]
        out = []
        for u in range(hp):
            g = sp * hp + u
            s = jnp.dot(kt, qt_ref[0, g * LANES:(g + 1) * LANES, :], preferred_element_type=F32)
            out.append((s, jnp.max(s, axis=0, keepdims=True)))
        return out

    def softmax_stage(sc, ms, sp):
        out = []
        for u, (s, s_max) in enumerate(sc):
            h = sp * hp + u
            m_new = jnp.maximum(ms[h], s_max)
            out.append((jnp.exp2(ms[h] - m_new), jnp.exp2(s - m_new).astype(BF16)))
            ms[h] = m_new
        return out

    def pv_stage(j, sp, ap):
        vt = vt_ref[0, 0:PV_ROWS, pl.ds(pl.multiple_of(j * tk, tk), tk)]
        for u, (alpha, p) in enumerate(ap):
            g = sp * hp + u
            acc_ref[g] = alpha * acc_ref[g] + jnp.dot(vt, p, preferred_element_type=F32)

    ms = [jnp.full((1, tq), -jnp.inf, F32)] * GQA_GROUP
    ap = softmax_stage(scores(0, 0), ms, 0)
    sc = scores(min(1 // spt, n_kv - 1), 1 % spt)
    for u in range(hp):
        s_ref[u] = sc[u][0]
        p_ref[u] = ap[u][1]

    def body(it, carry):
        ms = list(carry[:GQA_GROUP])
        ap = [(carry[GQA_GROUP + u], p_ref[u]) for u in range(hp)]
        sc = [(s_ref[u], carry[GQA_GROUP + hp + u]) for u in range(hp)]
        for n in range(kv_unroll * spt):
            j = it * kv_unroll + n // spt
            j_next = jnp.minimum(it * kv_unroll + (n + 2) // spt, n_kv - 1)
            sc_next = scores(j_next, (n + 2) % spt)
            ap_next = softmax_stage(sc, ms, (n + 1) % spt)
            pv_stage(j, n % spt, ap)
            sc, ap = sc_next, ap_next
        for u in range(hp):
            s_ref[u] = sc[u][0]
            p_ref[u] = ap[u][1]
        return tuple(ms) + tuple(a for a, _ in ap) + tuple(m for _, m in sc)

    lax.fori_loop(0, n_kv // kv_unroll, body,
                  tuple(ms) + tuple(a for a, _ in ap) + tuple(m for _, m in sc))
    _attn_finalize(acc_ref, o_ref, tq)


def _attn_finalize(acc_ref, o_ref, tq):
    pad = jnp.zeros((LANES - HEAD_DIM, tq), F32)
    for g in range(GQA_GROUP):
        acc = acc_ref[g]
        o = acc[0:HEAD_DIM] / acc[HEAD_DIM:HEAD_DIM + 1, :]
        o_ref[0, :, g * LANES:(g + 1) * LANES] = jnp.concatenate([o, pad], axis=0).T.astype(BF16)


def _attn_bounded_kernel(qt_ref, k_ref, vt_ref, o_ref, acc_ref, s_ref, p_ref, *, tq, tk, n_kv, kv_unroll):
    hp = HEADS_PER_STEP
    spt = GQA_GROUP // hp
    acc_ref[...] = jnp.zeros(acc_ref.shape, F32)

    def scores(j, sp):
        kt = k_ref[0, pl.ds(pl.multiple_of(j * tk, tk), tk), :]
        return [jnp.dot(kt, qt_ref[0, (sp * hp + u) * LANES:(sp * hp + u + 1) * LANES, :],
                        preferred_element_type=F32) for u in range(hp)]

    def probs(sc):
        return [jnp.exp2(s).astype(BF16) for s in sc]

    def pv_stage(j, sp, ps):
        vt = vt_ref[0, 0:PV_ROWS, pl.ds(pl.multiple_of(j * tk, tk), tk)]
        for u, p in enumerate(ps):
            acc_ref[sp * hp + u] += jnp.dot(vt, p, preferred_element_type=F32)

    ps = probs(scores(0, 0))
    sc = scores(min(1 // spt, n_kv - 1), 1 % spt)
    for u in range(hp):
        s_ref[u] = sc[u]
        p_ref[u] = ps[u]

    def body(it, carry):
        ps = [p_ref[u] for u in range(hp)]
        sc = [s_ref[u] for u in range(hp)]
        for n in range(kv_unroll * spt):
            j = it * kv_unroll + n // spt
            j_next = jnp.minimum(it * kv_unroll + (n + 2) // spt, n_kv - 1)
            sc_next = scores(j_next, (n + 2) % spt)
            ps_next = probs(sc)
            pv_stage(j, n % spt, ps)
            sc, ps = sc_next, ps_next
        for u in range(hp):
            s_ref[u] = sc[u]
            p_ref[u] = ps[u]
        return carry

    lax.fori_loop(0, n_kv // kv_unroll, body, 0)
    _attn_finalize(acc_ref, o_ref, tq)


def _attention(qt, k, vt, *, kernel, B, S):
    tq = min(TQ, S)
    tk = min(TK, S)
    gw = GQA_GROUP * LANES
    return pl.pallas_call(
        functools.partial(kernel, tq=tq, tk=tk, n_kv=S // tk,
                          kv_unroll=math.gcd(S // tk, KV_UNROLL)),
        grid=(B, N_KV_HEADS, S // tq),
        in_specs=[
            pl.BlockSpec((1, gw, tq), lambda b, h, i: (b, h, i)),
            pl.BlockSpec((1, S, LANES), lambda b, h, i: (b, 0, h)),
            pl.BlockSpec((1, LANES, S), lambda b, h, i: (b, h, 0)),
        ],
        out_specs=pl.BlockSpec((1, tq, gw), lambda b, h, i: (b, i, h)),
        out_shape=jax.ShapeDtypeStruct((B, S, N_Q_HEADS * LANES), BF16),
        scratch_shapes=[pltpu.VMEM((GQA_GROUP, PV_ROWS, tq), F32),
                        pltpu.VMEM((HEADS_PER_STEP, tk, tq), F32),
                        pltpu.VMEM((HEADS_PER_STEP, tk, tq), BF16)],
        compiler_params=_cparams(("parallel", "parallel", "parallel")),
        name=kernel.__name__.strip("_"),
    )(qt, k, vt)


def _scan_chunk(a, b, h_in, reverse):
    n = a.shape[0]
    n_groups = n // SUBLANES
    a = a.reshape(n_groups, SUBLANES, LANES)
    b = b.reshape(n_groups, SUBLANES, LANES)
    sub = lax.broadcasted_iota(jnp.int32, a.shape, 1)
    d = 1
    while d < SUBLANES:
        if reverse:
            keep = sub < SUBLANES - d
            shift = SUBLANES - d
        else:
            keep = sub >= d
            shift = d
        a_sh = jnp.where(keep, pltpu.roll(a, shift, 1), 1.0)
        b_sh = jnp.where(keep, pltpu.roll(b, shift, 1), 0.0)
        b = a * b_sh + b
        a = a * a_sh
        d *= 2
    a = a.reshape(n, LANES)
    b = b.reshape(n, LANES)
    order = range(n_groups - 1, -1, -1) if reverse else range(n_groups)
    edge = h_in
    out = [None] * n_groups
    for v in order:
        rows = slice(v * SUBLANES, (v + 1) * SUBLANES)
        hv = b[rows] + a[rows] * jnp.broadcast_to(edge, (SUBLANES, LANES))
        out[v] = hv
        edge = hv[0:1] if reverse else hv[SUBLANES - 1:SUBLANES]
    return jnp.concatenate(out, axis=0), edge


def _lru_pad_input(u_ref, up_ref, S):
    zeros = jnp.zeros((SUBLANES, LANES), F32)
    up_ref[0:SUBLANES, :] = zeros
    up_ref[S + SUBLANES:S + 2 * SUBLANES, :] = zeros
    up_ref[SUBLANES:S + SUBLANES, :] = u_ref[0]


def _lru_gates(up_ref, cw_ref, cb_ref, w_ref, bias_ref, lam_ref, t0, tc, d):
    cw = cw_ref[...]
    xc = cb_ref[...]
    for j in range(CONV_W):
        xc = xc + up_ref[pl.ds(t0 + SUBLANES + j - CONV_PAD_L, tc), :] * cw[j:j + 1, :]
    gw = 2 * LANES
    g = jnp.dot(xc.astype(BF16), w_ref[0, :, d * gw:(d + 1) * gw],
                preferred_element_type=F32) + bias_ref[0, :, d * gw:(d + 1) * gw]
    r = jax.nn.sigmoid(g[:, :LANES])
    i = jax.nn.sigmoid(g[:, LANES:])
    a = jnp.exp(-LRU_C * r * jax.nn.softplus(-lam_ref[d:d + 1, :]))
    y = 1.0 - a * a
    b = jnp.where(y > 0.0, y * lax.rsqrt(y), 0.0) * i * xc
    return a, b


def _lru_kernel(u_ref, gate_ref, cw_ref, cb_ref, w_ref, bias_ref, lam_ref, o_ref,
                up_ref, hf_ref, *, S, tc):
    _lru_pad_input(u_ref, up_ref, S)
    n_chunks = S // tc
    params = (up_ref, cw_ref, cb_ref, w_ref, bias_ref, lam_ref)

    def fwd(c, h):
        t0 = pl.multiple_of(c * tc, tc)
        hc, h_last = _scan_chunk(*_lru_gates(*params, t0, tc, 0), h, False)
        hf_ref[pl.ds(t0, tc), :] = hc
        return h_last

    lax.fori_loop(0, n_chunks, fwd, jnp.zeros((1, LANES), F32))

    def bwd(ci, h):
        t0 = pl.multiple_of((n_chunks - 1 - ci) * tc, tc)
        hc, h_last = _scan_chunk(*_lru_gates(*params, t0, tc, 1), h, True)
        gate = gate_ref[0, pl.ds(t0, tc), :]
        o_ref[0, pl.ds(t0, tc), :] = (hf_ref[pl.ds(t0, tc), :] + hc) * jax.nn.gelu(gate)
        return h_last

    lax.fori_loop(0, n_chunks, bwd, jnp.zeros((1, LANES), F32))


def _block_diag_pairs(w):
    nb, bw, _ = w.shape
    w = w.reshape(nb // 2, 2, bw, bw)
    z = jnp.zeros_like(w[:, 0])
    top = jnp.concatenate([w[:, 0], z], axis=-1)
    bot = jnp.concatenate([z, w[:, 1]], axis=-1)
    return jnp.concatenate([top, bot], axis=-2)


def _lru_operands(conv_w, conv_b, wa, ba, wi, bi, lam):
    C = conv_b.shape[0]
    nc = C // LANES
    w = jnp.concatenate([_block_diag_pairs(wa[0]), _block_diag_pairs(wi[0]),
                         _block_diag_pairs(wa[1]), _block_diag_pairs(wi[1])], axis=-1).astype(BF16)
    bias = jnp.stack([ba[0].reshape(nc, LANES), bi[0].reshape(nc, LANES),
                      ba[1].reshape(nc, LANES), bi[1].reshape(nc, LANES)], axis=1)
    return conv_w, conv_b.reshape(1, C), w, bias.reshape(nc, 1, 4 * LANES), lam


def _lru_specs(S, unit):
    seq = lambda *g: (unit(*g)[0], 0, unit(*g)[1])
    chan = lambda *g: (0, unit(*g)[1])
    blk = lambda *g: (unit(*g)[1], 0, 0)
    in_specs = [
        pl.BlockSpec((1, S, LANES), seq),
        pl.BlockSpec((1, S, LANES), seq),
        pl.BlockSpec((CONV_W, LANES), chan),
        pl.BlockSpec((1, LANES), chan),
        pl.BlockSpec((1, LANES, 4 * LANES), blk),
        pl.BlockSpec((1, 1, 4 * LANES), blk),
        pl.BlockSpec((2, LANES), chan),
    ]
    return in_specs, pl.BlockSpec((1, S, LANES), seq)


def _lru(lru_x, lru_gate, lru_ops, B, S):
    C = lru_x.shape[-1]
    tc = min(TC_LRU, S)
    in_specs, out_spec = _lru_specs(S, lambda b, c: (b, c))
    return pl.pallas_call(
        functools.partial(_lru_kernel, S=S, tc=tc),
        grid=(B, C // LANES),
        in_specs=in_specs,
        out_specs=out_spec,
        out_shape=jax.ShapeDtypeStruct((B, S, C), F32),
        scratch_shapes=[
            pltpu.VMEM((S + 2 * SUBLANES, LANES), F32),
            pltpu.VMEM((S, LANES), F32),
        ],
        compiler_params=_cparams(("parallel", "parallel")),
        name="rglru",
    )(lru_x, lru_gate, *lru_ops)


def _mixers(qt, k, vt, lru_x, lru_gate, lru_ops, score_bound, B, S):
    attn = lax.cond(score_bound <= SAFE_SCORE_LOG2,
                    functools.partial(_attention, kernel=_attn_bounded_kernel, B=B, S=S),
                    functools.partial(_attention, kernel=_attn_kernel, B=B, S=S), qt, k, vt)
    return attn, _lru(lru_x, lru_gate, lru_ops, B, S)


def _rows_to_slabs(ref, x):
    n = x.shape[0]
    for s in range(SUBLANES):
        ref[pl.ds(s, n, stride=SUBLANES), :] = x[:, s * LANES:(s + 1) * LANES]


def _slabs_to_rows(ref, n):
    return jnp.concatenate([ref[pl.ds(s, n, stride=SUBLANES), :] for s in range(SUBLANES)], axis=1)


def _slab(ref, r):
    return ref.at[pl.ds(pl.multiple_of(r * SUBLANES, SUBLANES), SUBLANES)]


def _outproj_kernel(a_ref, l_ref, x_ref, ag_ref, lg_ref, wa_ref, wl_ref, g2_ref,
                    wrh_ref, wrl_ref, br_ref, tri_ref,
                    x1_ref, xn3_ref, route_ref, gates_ref, cnt_ref, carry_ref, *, attn_w, lru_w):
    step = pl.program_id(0)

    @pl.when(step == 0)
    def _():
        carry_ref[...] = jnp.zeros_like(carry_ref)

    a = a_ref[...].astype(F32)
    ams = jnp.sum(a * a, axis=-1, keepdims=True) * (1.0 / attn_w)
    an = a * lax.rsqrt(ams + NORM_EPS) * ag_ref[...]
    l = l_ref[...]
    lms = jnp.sum(l * l, axis=-1, keepdims=True) * (1.0 / lru_w)
    ln = l * lax.rsqrt(lms + NORM_EPS) * lg_ref[...]
    mix = (jnp.dot(an.astype(BF16), wa_ref[...], preferred_element_type=F32)
           + jnp.dot(ln.astype(BF16), wl_ref[...], preferred_element_type=F32))
    x1 = x_ref[...] + mix
    x1_ref[...] = x1
    ms = jnp.mean(x1 * x1, axis=-1, keepdims=True)
    xn = x1 * lax.rsqrt(ms + NORM_EPS) * g2_ref[...]
    _rows_to_slabs(xn3_ref, xn)

    hi = xn.astype(BF16)
    lo = (xn - hi.astype(F32)).astype(BF16)
    logits = (jnp.dot(hi, wrh_ref[...], preferred_element_type=F32)
              + jnp.dot(lo, wrh_ref[...], preferred_element_type=F32)
              + jnp.dot(hi, wrl_ref[...], preferred_element_type=F32)) + br_ref[...]
    lane = lax.broadcasted_iota(jnp.int32, logits.shape, 1)
    neg = -jnp.inf
    work = jnp.where(lane < N_EXPERTS, logits, neg)
    sel = jnp.zeros(logits.shape, F32)
    idxs, vals = [], []
    for _ in range(TOP_K):
        m = jnp.max(work, axis=1, keepdims=True)
        idx = jnp.min(jnp.where(work == m, lane, LANES), axis=1, keepdims=True)
        hit = lane == idx
        work = jnp.where(hit, neg, work)
        sel = sel + hit.astype(F32)
        idxs.append(idx)
        vals.append(m)
    es = [jnp.exp(v - vals[0]) for v in vals]
    den = es[0] + es[1] + es[2] + es[3]

    prefix = jnp.dot(tri_ref[...], sel.astype(BF16), preferred_element_type=F32) + carry_ref[...]
    carry_ref[...] = carry_ref[...] + jnp.sum(sel, axis=0, keepdims=True)
    cnt_ref[...] = carry_ref[...]

    route = jnp.zeros(logits.shape, jnp.int32)
    gates = jnp.zeros(logits.shape, F32)
    for k in range(TOP_K):
        rank = jnp.sum(jnp.where(lane == idxs[k], prefix, 0.0), axis=1, keepdims=True).astype(jnp.int32)
        route = jnp.where(lane == k, idxs[k], route)
        route = jnp.where(lane == TOP_K + k, rank, route)
        gates = jnp.where(lane == k, es[k] / den, gates)
    route_ref[...] = route
    gates_ref[...] = gates


def _outproj_router(attn, lru, x2, attn_out_g, lru_out_g, w_out, norm2_g, w_router, b_router):
    T, D = x2.shape
    lru_w = lru.shape[-1]
    ts = min(TS_OUT, T)
    wa = w_out[:ATTN_W].reshape(N_Q_HEADS, HEAD_DIM, D)
    wa = jnp.pad(wa, ((0, 0), (0, LANES - HEAD_DIM), (0, 0))).reshape(N_Q_HEADS * LANES, D).astype(BF16)
    wl = w_out[ATTN_W:].astype(BF16)
    ag = _pad_heads(attn_out_g.reshape(1, ATTN_W), N_Q_HEADS)
    wr = jnp.pad(w_router, ((0, 0), (0, LANES - N_EXPERTS)))
    wrh = wr.astype(BF16)
    wrl = (wr - wrh.astype(F32)).astype(BF16)
    br = jnp.pad(b_router.reshape(1, N_EXPERTS), ((0, 0), (0, LANES - N_EXPERTS)))
    tri = (jnp.arange(ts)[:, None] > jnp.arange(ts)[None, :]).astype(BF16)
    const = lambda i: (0, 0)
    tok = lambda i: (i, 0)
    aw = N_Q_HEADS * LANES
    return pl.pallas_call(
        functools.partial(_outproj_kernel, attn_w=ATTN_W, lru_w=lru_w),
        grid=(T // ts,),
        in_specs=[
            pl.BlockSpec((ts, aw), tok),
            pl.BlockSpec((ts, lru_w), tok),
            pl.BlockSpec((ts, D), tok),
            pl.BlockSpec((1, aw), const),
            pl.BlockSpec((1, lru_w), const),
            pl.BlockSpec((aw, D), const),
            pl.BlockSpec((lru_w, D), const),
            pl.BlockSpec((1, D), const),
            pl.BlockSpec((D, LANES), const),
            pl.BlockSpec((D, LANES), const),
            pl.BlockSpec((1, LANES), const),
            pl.BlockSpec((ts, ts), const),
        ],
        out_specs=[
            pl.BlockSpec((ts, D), tok),
            pl.BlockSpec((ts * SUBLANES, LANES), tok),
            pl.BlockSpec((ts, LANES), tok),
            pl.BlockSpec((ts, LANES), tok),
            pl.BlockSpec((1, LANES), const),
        ],
        out_shape=[
            jax.ShapeDtypeStruct((T, D), F32),
            jax.ShapeDtypeStruct((T * SUBLANES, LANES), F32),
            jax.ShapeDtypeStruct((T, LANES), jnp.int32),
            jax.ShapeDtypeStruct((T, LANES), F32),
            jax.ShapeDtypeStruct((1, LANES), F32),
        ],
        scratch_shapes=[pltpu.VMEM((1, LANES), F32)],
        compiler_params=_cparams(("arbitrary",)),
        name="outproj_router",
    )(attn, lru, x2, ag, lru_out_g.reshape(1, lru_w), wa, wl, norm2_g.reshape(1, D),
      wrh, wrl, br, tri)


def _dest_kernel(route_ref, pstart_ref, dest_ref):
    route = route_ref[...]
    lane = lax.broadcasted_iota(jnp.int32, route.shape, 1)
    pstart = pstart_ref[...]
    dest = jnp.zeros(route.shape, jnp.int32)
    for k in range(TOP_K):
        start = jnp.sum(jnp.where(lane == route[:, k:k + 1], pstart, 0.0), axis=1, keepdims=True)
        dest = jnp.where(lane == k, start.astype(jnp.int32) + route[:, TOP_K + k:TOP_K + k + 1], dest)
    dest_ref[...] = dest


def _dest_rows(route, pstart):
    T = route.shape[0]
    ts = math.gcd(TS_DEST, T)
    row = jnp.pad(pstart.astype(F32).reshape(1, N_EXPERTS), ((0, 0), (0, LANES - N_EXPERTS)))
    dest = pl.pallas_call(
        _dest_kernel,
        grid=(T // ts,),
        in_specs=[pl.BlockSpec((ts, LANES), lambda i: (i, 0)),
                  pl.BlockSpec((1, LANES), lambda i: (0, 0))],
        out_specs=pl.BlockSpec((ts, LANES), lambda i: (i, 0)),
        out_shape=jax.ShapeDtypeStruct((T, LANES), jnp.int32),
        compiler_params=_cparams(("parallel",)),
        name="dest_rows",
    )(route, row)
    return dest[:, :TOP_K].reshape(T * TOP_K)


def _dispatch_kernel(fill_ref, dest_ref, x_ref, out_hbm, zero_ref, sem, zero_sem, *, ts, n_blocks):
    block_slabs = ROW_BLOCK * SUBLANES

    def fill_copy(b):
        off = pl.multiple_of(b * block_slabs, block_slabs)
        return pltpu.make_async_copy(zero_ref, out_hbm.at[pl.ds(off, block_slabs)], zero_sem)

    @pl.when(pl.program_id(0) == 0)
    def _():
        zero_ref[...] = jnp.zeros(zero_ref.shape, F32)

        def start(b, carry):
            @pl.when(fill_ref[b] != 0)
            def _():
                fill_copy(b).start()
            return carry

        def wait(b, carry):
            @pl.when(fill_ref[b] != 0)
            def _():
                fill_copy(b).wait()
            return carry

        lax.fori_loop(0, n_blocks, start, 0)
        lax.fori_loop(0, n_blocks, wait, 0)

    def issue(i, carry):
        for j in range(ISSUE_UNROLL):
            r = i * ISSUE_UNROLL + j
            for k in range(TOP_K):
                d = dest_ref[r * TOP_K + k]
                pltpu.make_async_copy(_slab(x_ref, r), _slab(out_hbm, d), sem).start(priority=k % 2)
        return carry

    lax.fori_loop(0, ts // ISSUE_UNROLL, issue, 0)
    for k in range(TOP_K):
        pltpu.make_async_copy(x_ref, out_hbm.at[pl.ds(0, ts * SUBLANES)], sem).wait()


def _dispatch(xn_slabs, fill, dest_flat, n_rows):
    T = xn_slabs.shape[0] // SUBLANES
    ts = min(TS_DISP, T)
    grid_spec = pltpu.PrefetchScalarGridSpec(
        num_scalar_prefetch=1,
        grid=(T // ts,),
        in_specs=[
            pl.BlockSpec((ts * TOP_K,), lambda i, fl: (i,), memory_space=pltpu.SMEM),
            pl.BlockSpec((ts * SUBLANES, LANES), lambda i, fl: (i, 0)),
        ],
        out_specs=pl.BlockSpec(memory_space=pl.ANY),
        scratch_shapes=[pltpu.VMEM((ROW_BLOCK * SUBLANES, LANES), F32),
                        pltpu.SemaphoreType.DMA, pltpu.SemaphoreType.DMA],
    )
    return pl.pallas_call(
        functools.partial(_dispatch_kernel, ts=ts, n_blocks=n_rows // ROW_BLOCK),
        grid_spec=grid_spec,
        out_shape=jax.ShapeDtypeStruct((n_rows * SUBLANES, LANES), xn_slabs.dtype),
        compiler_params=_cparams(("arbitrary",)),
        name="dispatch",
    )(fill, dest_flat, xn_slabs)


def _expert_kernel(be_ref, na_ref, nxt_ref, x_ref, wg_hbm, bg_ref, wu_hbm, bu_ref, wd_hbm, bd_ref,
                   y_ref, stage_ref, wb_ref, slot_ref, sems):
    i = pl.program_id(0)
    e = be_ref[i]
    w_hbm = (wg_hbm, wu_hbm, wd_hbm)

    def fetch(expert, slot, m):
        return pltpu.make_async_copy(w_hbm[m].at[expert], stage_ref.at[slot, m], sems.at[slot, m])

    @pl.when(i == 0)
    def _():
        slot_ref[0] = 0
        for m in range(3):
            fetch(e, 0, m).start()

    active = i < na_ref[0]
    first = jnp.logical_or(i == 0, e != be_ref[jnp.maximum(i - 1, 0)])

    @pl.when(jnp.logical_and(active, first))
    def _():
        slot = slot_ref[0]
        for m in range(3):
            fetch(e, slot, m).wait()
            wb_ref[m] = stage_ref[slot, m].astype(BF16)

        @pl.when(nxt_ref[i] >= 0)
        def _():
            for m in range(3):
                fetch(nxt_ref[i], 1 - slot, m).start()

        slot_ref[0] = 1 - slot

    @pl.when(active)
    def _():
        x = _slabs_to_rows(x_ref, ROW_BLOCK).astype(BF16)
        g = jnp.dot(x, wb_ref[0], preferred_element_type=F32) + bg_ref[0]
        u = jnp.dot(x, wb_ref[1], preferred_element_type=F32) + bu_ref[0]
        g = jnp.minimum(g, SWIGLU_LIMIT)
        u = jnp.clip(u, -SWIGLU_LIMIT, SWIGLU_LIMIT)
        glu = g * jax.nn.sigmoid(SWIGLU_ALPHA * g)
        y = jnp.dot(((u + 1.0) * glu).astype(BF16), wb_ref[2], preferred_element_type=F32) + bd_ref[0]
        _rows_to_slabs(y_ref, y)


def _experts(x_rows, block_e, n_active, next_e, w_gate, b_gate, w_up, b_up, w_down, b_down):
    E, D, FF = w_gate.shape
    assert D == FF, "the three expert matrices share one staging shape"
    block_slabs = ROW_BLOCK * SUBLANES
    n_blocks = x_rows.shape[0] // block_slabs

    def row_map(i, be, na, nx):
        return (jnp.minimum(i, na[0] - 1), 0)

    def b_map(i, be, na, nx):
        return (be[jnp.minimum(i, na[0] - 1)], 0, 0)

    grid_spec = pltpu.PrefetchScalarGridSpec(
        num_scalar_prefetch=3,
        grid=(n_blocks,),
        in_specs=[
            pl.BlockSpec((block_slabs, LANES), row_map),
            pl.BlockSpec(memory_space=pl.ANY),
            pl.BlockSpec((1, 1, FF), b_map),
            pl.BlockSpec(memory_space=pl.ANY),
            pl.BlockSpec((1, 1, FF), b_map),
            pl.BlockSpec(memory_space=pl.ANY),
            pl.BlockSpec((1, 1, D), b_map),
        ],
        out_specs=pl.BlockSpec((block_slabs, LANES), row_map),
        scratch_shapes=[
            pltpu.VMEM((2, 3, D, FF), F32),
            pltpu.VMEM((3, D, FF), BF16),
            pltpu.SMEM((1,), jnp.int32),
            pltpu.SemaphoreType.DMA((2, 3)),
        ],
    )
    return pl.pallas_call(
        _expert_kernel,
        grid_spec=grid_spec,
        out_shape=jax.ShapeDtypeStruct(x_rows.shape, F32),
        input_output_aliases={3: 0},
        compiler_params=pltpu.CompilerParams(dimension_semantics=("arbitrary",),
                                             vmem_limit_bytes=EXPERT_VMEM_LIMIT),
        name="experts",
    )(block_e, n_active, next_e, x_rows, w_gate, b_gate.reshape(E, 1, FF), w_up,
      b_up.reshape(E, 1, FF), w_down, b_down.reshape(E, 1, D))


def _combine_kernel(dest_ref, dest_next_ref, y_hbm, x1_ref, gates_ref, fg_ref, o_ref, bufs, sems,
                    *, ts, n_steps):
    i = pl.program_id(0)
    slot = i % 2

    def gather_tile(d_ref, s):
        def issue(it, carry):
            for j in range(ISSUE_UNROLL):
                r = it * ISSUE_UNROLL + j
                for k in range(TOP_K):
                    d = d_ref[r * TOP_K + k]
                    pltpu.make_async_copy(_slab(y_hbm, d), _slab(bufs.at[s, k], r),
                                          sems.at[s]).start(priority=k % 2)
            return carry

        lax.fori_loop(0, ts // ISSUE_UNROLL, issue, 0)

    @pl.when(i == 0)
    def _():
        gather_tile(dest_ref, 0)

    @pl.when(i + 1 < n_steps)
    def _():
        gather_tile(dest_next_ref, 1 - slot)

    for k in range(TOP_K):
        pltpu.make_async_copy(y_hbm.at[pl.ds(0, ts * SUBLANES)], bufs.at[slot, k], sems.at[slot]).wait()

    acc = x1_ref[...]
    gates = gates_ref[...]
    for k in range(TOP_K):
        acc = acc + _slabs_to_rows(bufs.at[slot, k], ts) * gates[:, k:k + 1]
    ms = jnp.mean(acc * acc, axis=-1, keepdims=True)
    o_ref[...] = acc * lax.rsqrt(ms + NORM_EPS) * fg_ref[...]


def _combine(y_rows, dest_flat, x1, gates, final_g):
    T, D = x1.shape
    ts = min(TS_COMB, T)
    n_steps = T // ts
    tok = lambda i: (i, 0)
    return pl.pallas_call(
        functools.partial(_combine_kernel, ts=ts, n_steps=n_steps),
        grid=(n_steps,),
        in_specs=[
            pl.BlockSpec((ts * TOP_K,), lambda i: (i,), memory_space=pltpu.SMEM),
            pl.BlockSpec((ts * TOP_K,), lambda i: (jnp.minimum(i + 1, n_steps - 1),),
                         memory_space=pltpu.SMEM),
            pl.BlockSpec(memory_space=pl.ANY),
            pl.BlockSpec((ts, D), tok),
            pl.BlockSpec((ts, LANES), tok),
            pl.BlockSpec((1, D), lambda i: (0, 0)),
        ],
        out_specs=pl.BlockSpec((ts, D), tok),
        out_shape=jax.ShapeDtypeStruct((T, D), F32),
        scratch_shapes=[pltpu.VMEM((2, TOP_K, ts * SUBLANES, LANES), F32),
                        pltpu.SemaphoreType.DMA((2,))],
        compiler_params=_cparams(("arbitrary",)),
        name="combine",
    )(dest_flat, dest_flat, y_rows, x1, gates, final_g.reshape(1, D))


def kernel(x, norm1_g, w_in, q_norm_g, k_norm_g, conv_w, conv_b, lru_wa, lru_ba, lru_wi, lru_bi,
           lru_lam, attn_out_g, lru_out_g, w_out, norm2_g, w_router, b_router, w_gate, b_gate,
           w_up, b_up, w_down, b_down, final_g):
    B, S, D = x.shape
    T = B * S
    assert w_in.shape[0] == 1, "single-layer trunk: the final norm is fused into the layer's combine"
    x2 = x.reshape(T, D)
    for l in range(1):
        qt, k, vt, lru_x, lru_gate = _inproj(x2, norm1_g[l], w_in[l], q_norm_g[l], k_norm_g[l], S)
        score_bound = (HEAD_DIM * Q_SCALE * jnp.max(jnp.abs(q_norm_g[l]))
                       * jnp.max(jnp.abs(k_norm_g[l])))
        lru_ops = _lru_operands(conv_w[l], conv_b[l], lru_wa[l], lru_ba[l], lru_wi[l], lru_bi[l],
                                lru_lam[l])
        attn, lru = _mixers(qt, k.reshape(B, S, -1), vt, lru_x.reshape(B, S, -1),
                            lru_gate.reshape(B, S, -1), lru_ops, score_bound, B, S)
        x1, xn3, route, gates, cnt = _outproj_router(
            attn.reshape(T, -1), lru.reshape(T, -1), x2, attn_out_g[l], lru_out_g[l], w_out[l],
            norm2_g[l], w_router[l], b_router[l])

        counts = cnt[0, :N_EXPERTS].astype(jnp.int32)
        padded = ((counts + ROW_BLOCK - 1) // ROW_BLOCK) * ROW_BLOCK
        pend = jnp.cumsum(padded)
        pstart = (pend - padded).astype(jnp.int32)
        n_rows = T * TOP_K + N_EXPERTS * ROW_BLOCK
        block_start = jnp.arange(n_rows // ROW_BLOCK, dtype=jnp.int32) * ROW_BLOCK
        block_e = jnp.sum((pend[None, :] <= block_start[:, None]).astype(jnp.int32), axis=1)
        block_e = jnp.minimum(block_e, N_EXPERTS - 1)
        n_active = (pend[-1:] // ROW_BLOCK).astype(jnp.int32)
        fill = jnp.logical_or(block_start + ROW_BLOCK == pend[block_e],
                              block_start >= pend[-1]).astype(jnp.int32)

        next_block = pend[block_e] // ROW_BLOCK
        next_e = jnp.where(next_block < n_active[0],
                           block_e[jnp.minimum(next_block, block_e.shape[0] - 1)], -1).astype(jnp.int32)

        dest_flat = _dest_rows(route, pstart)
        x_rows = _dispatch(xn3, fill, dest_flat, n_rows)
        y_rows = _experts(x_rows, block_e, n_active, next_e, w_gate[l], b_gate[l], w_up[l], b_up[l],
                          w_down[l], b_down[l])
        x2 = _combine(y_rows, dest_flat, x1, gates, final_g)
    return x2.reshape(B, S, D)
```

```python
import functools
import math

import jax
import jax.numpy as jnp
from jax import lax
from jax.experimental import pallas as pl
from jax.experimental.pallas import tpu as pltpu

F32 = jnp.float32
BF16 = jnp.bfloat16

GRID_W = 64
HEAD_DIM = 64
N_Q_HEADS = 8
N_KV_HEADS = 2
GQA_GROUP = N_Q_HEADS // N_KV_HEADS
ATTN_W = N_Q_HEADS * HEAD_DIM
KV_W = N_KV_HEADS * HEAD_DIM
LRU_BLOCKS = 8
LRU_C = 8.0
CONV_W = 4
CONV_PAD_L = 2
ROPE_THETA = 10000.0
ROPE_HALF = HEAD_DIM // 2
ROPE_M = ROPE_HALF // 2
N_EXPERTS = 32
TOP_K = 4
SWIGLU_ALPHA = 1.702
SWIGLU_LIMIT = 7.0
NORM_EPS = 1e-5
QK_EPS = 1e-6
LOG2_E = 1.4426950408889634
Q_SCALE = HEAD_DIM ** -0.5 * LOG2_E
SAFE_SCORE_LOG2 = 96.0

LANES = 128
SUBLANES = 8
BF16_SUBLANES = 16
PV_ROWS = HEAD_DIM + BF16_SUBLANES
VMEM_LIMIT = 48 * 1024 * 1024
EXPERT_VMEM_LIMIT = 56 * 1024 * 1024

TS_IN = 512
TQ = 512
TK = 256
KV_UNROLL = 8
HEADS_PER_STEP = 2
TC_LRU = 512
TS_OUT = 512
TS_DEST = 2048
ROW_BLOCK = 512
TS_DISP = 512
TS_COMB = 256
ISSUE_UNROLL = 8


def _cparams(sem):
    return pltpu.CompilerParams(dimension_semantics=sem, vmem_limit_bytes=VMEM_LIMIT)


def _inproj_kernel(x_ref, g1_ref, wt_ref, w_ref, qg_ref, kg_ref, cos_ref, sin_ref, cost_ref, sint_ref,
                   q_ref, k_ref, v_ref, lx_ref, lg_ref, *, lru_w):
    x = x_ref[...]
    ms = jnp.mean(x * x, axis=-1, keepdims=True)
    xn = (x * lax.rsqrt(ms + NORM_EPS) * g1_ref[...]).astype(BF16)
    ht = lax.dot_general(wt_ref[...], xn, (((1,), (1,)), ((), ())), preferred_element_type=F32)
    h = jnp.dot(xn, w_ref[...], preferred_element_type=F32)

    qw = N_Q_HEADS * LANES
    kw = N_KV_HEADS * LANES
    cost = cost_ref[...]
    sint = sint_ref[...]
    row = lax.broadcasted_iota(jnp.int32, cost.shape, 0)
    first_half_t = (row % ROPE_HALF) < ROPE_M
    qg = qg_ref[...]
    for c in range(N_Q_HEADS):
        sl = slice(c * LANES, (c + 1) * LANES)
        xc = ht[sl]
        hms = jnp.sum(xc * xc, axis=0, keepdims=True) * (1.0 / HEAD_DIM)
        xc = xc * lax.rsqrt(hms + QK_EPS) * qg
        partner = jnp.where(first_half_t, pltpu.roll(xc, LANES - ROPE_M, 0), pltpu.roll(xc, ROPE_M, 0))
        q_ref[0, sl, :] = ((xc * cost + partner * sint) * Q_SCALE).astype(BF16)
    for c in range(N_KV_HEADS):
        sl = slice(c * LANES, (c + 1) * LANES)
        v_ref[0, sl, :] = jnp.where(row >= HEAD_DIM, 1.0, ht[qw + c * LANES: qw + (c + 1) * LANES]).astype(BF16)

    cos = cos_ref[...]
    sin = sin_ref[...]
    lane = lax.broadcasted_iota(jnp.int32, cos.shape, 1)
    first_half = (lane % ROPE_HALF) < ROPE_M
    for c in range(N_KV_HEADS):
        sl = slice(c * LANES, (c + 1) * LANES)
        xc = h[:, sl]
        hms = jnp.sum(xc * xc, axis=-1, keepdims=True) * (1.0 / HEAD_DIM)
        xc = xc * lax.rsqrt(hms + QK_EPS) * kg_ref[...]
        partner = jnp.where(first_half, pltpu.roll(xc, LANES - ROPE_M, 1), pltpu.roll(xc, ROPE_M, 1))
        k_ref[:, sl] = (xc * cos + partner * sin).astype(BF16)
    lx_ref[...] = h[:, kw: kw + lru_w]
    lg_ref[...] = h[:, kw + lru_w: kw + 2 * lru_w]


def _pad_heads(w, n_heads):
    lead = w.shape[:-1]
    w = w.reshape(lead + (n_heads, HEAD_DIM))
    w = jnp.pad(w, [(0, 0)] * len(lead) + [(0, 0), (0, LANES - HEAD_DIM)])
    return w.reshape(lead + (n_heads * LANES,))


def _rope_tables(S):
    t = jnp.arange(S)
    rows = (t // GRID_W).astype(F32)
    cols = (t % GRID_W).astype(F32)
    inv_freq = ROPE_THETA ** (-jnp.arange(ROPE_M, dtype=F32) / ROPE_M)
    ar = rows[:, None] * inv_freq[None, :]
    ac = cols[:, None] * inv_freq[None, :]
    cos = jnp.concatenate([jnp.cos(ar), jnp.cos(ar), jnp.cos(ac), jnp.cos(ac)], axis=-1)
    sin = jnp.concatenate([-jnp.sin(ar), jnp.sin(ar), -jnp.sin(ac), jnp.sin(ac)], axis=-1)
    pad = [(0, 0), (0, LANES - HEAD_DIM)]
    return jnp.pad(cos, pad), jnp.pad(sin, pad)


def _inproj(x2, norm1_g, w_in, q_norm_g, k_norm_g, S):
    T, D = x2.shape
    lru_w = (w_in.shape[1] - ATTN_W - 2 * KV_W) // 2
    o0, o1, o2 = ATTN_W, ATTN_W + KV_W, ATTN_W + 2 * KV_W
    w_t = jnp.concatenate([_pad_heads(w_in[:, :o0], N_Q_HEADS),
                           _pad_heads(w_in[:, o1:o2], N_KV_HEADS)], axis=1).T.astype(BF16)
    w_rest = jnp.concatenate([_pad_heads(w_in[:, o0:o1], N_KV_HEADS), w_in[:, o2:]],
                             axis=1).astype(BF16)
    qg = _pad_heads(q_norm_g.reshape(1, HEAD_DIM), 1).reshape(LANES, 1)
    kg = _pad_heads(k_norm_g.reshape(1, HEAD_DIM), 1)
    cos, sin = _rope_tables(S)
    ts = TS_IN
    n_s = S // ts
    qw, kw = N_Q_HEADS * LANES, N_KV_HEADS * LANES
    const = lambda i: (0, 0)
    tok = lambda i: (i, 0)
    pos = lambda i: (i % n_s, 0)
    pos_t = lambda i: (0, i % n_s)
    tposed = lambda i: (i // n_s, 0, i % n_s)
    return pl.pallas_call(
        functools.partial(_inproj_kernel, lru_w=lru_w),
        grid=(T // ts,),
        in_specs=[
            pl.BlockSpec((ts, D), tok),
            pl.BlockSpec((1, D), const),
            pl.BlockSpec(w_t.shape, const),
            pl.BlockSpec(w_rest.shape, const),
            pl.BlockSpec((LANES, 1), const),
            pl.BlockSpec((1, LANES), const),
            pl.BlockSpec((ts, LANES), pos),
            pl.BlockSpec((ts, LANES), pos),
            pl.BlockSpec((LANES, ts), pos_t),
            pl.BlockSpec((LANES, ts), pos_t),
        ],
        out_specs=[
            pl.BlockSpec((1, qw, ts), tposed),
            pl.BlockSpec((ts, kw), tok),
            pl.BlockSpec((1, kw, ts), tposed),
            pl.BlockSpec((ts, lru_w), tok),
            pl.BlockSpec((ts, lru_w), tok),
        ],
        out_shape=[
            jax.ShapeDtypeStruct((T // S, qw, S), BF16),
            jax.ShapeDtypeStruct((T, kw), BF16),
            jax.ShapeDtypeStruct((T // S, kw, S), BF16),
            jax.ShapeDtypeStruct((T, lru_w), F32),
            jax.ShapeDtypeStruct((T, lru_w), F32),
        ],
        compiler_params=_cparams(("parallel",)),
        name="inproj",
    )(x2, norm1_g.reshape(1, D), w_t, w_rest, qg, kg, cos, sin, cos.T, sin.T)


def _attn_kernel(qt_ref, k_ref, vt_ref, o_ref, acc_ref, s_ref, p_ref, *, tq, tk, n_kv, kv_unroll):
    hp = HEADS_PER_STEP
    spt = GQA_GROUP // hp
    acc_ref[...] = jnp.zeros(acc_ref.shape, F32)

    def scores(j, sp):
        kt = k_ref[0, pl.ds(pl.multiple_of(j * tk, tk), tk), :]
        out = []
        for u in range(hp):
            g = sp * hp + u
            s = jnp.dot(kt, qt_ref[0, g * LANES:(g + 1) * LANES, :], preferred_element_type=F32)
            out.append((s, jnp.max(s, axis=0, keepdims=True)))
        return out

    def softmax_stage(sc, ms, sp):
        out = []
        for u, (s, s_max) in enumerate(sc):
            h = sp * hp + u
            m_new = jnp.maximum(ms[h], s_max)
            out.append((jnp.exp2(ms[h] - m_new), jnp.exp2(s - m_new).astype(BF16)))
            ms[h] = m_new
        return out

    def pv_stage(j, sp, ap):
        vt = vt_ref[0, 0:PV_ROWS, pl.ds(pl.multiple_of(j * tk, tk), tk)]
        for u, (alpha, p) in enumerate(ap):
            g = sp * hp + u
            acc_ref[g] = alpha * acc_ref[g] + jnp.dot(vt, p, preferred_element_type=F32)

    ms = [jnp.full((1, tq), -jnp.inf, F32)] * GQA_GROUP
    ap = softmax_stage(scores(0, 0), ms, 0)
    sc = scores(min(1 // spt, n_kv - 1), 1 % spt)
    for u in range(hp):
        s_ref[u] = sc[u][0]
        p_ref[u] = ap[u][1]

    def body(it, carry):
        ms = list(carry[:GQA_GROUP])
        ap = [(carry[GQA_GROUP + u], p_ref[u]) for u in range(hp)]
        sc = [(s_ref[u], carry[GQA_GROUP + hp + u]) for u in range(hp)]
        for n in range(kv_unroll * spt):
            j = it * kv_unroll + n // spt
            j_next = jnp.minimum(it * kv_unroll + (n + 2) // spt, n_kv - 1)
            sc_next = scores(j_next, (n + 2) % spt)
            ap_next = softmax_stage(sc, ms, (n + 1) % spt)
            pv_stage(j, n % spt, ap)
            sc, ap = sc_next, ap_next
        for u in range(hp):
            s_ref[u] = sc[u][0]
            p_ref[u] = ap[u][1]
        return tuple(ms) + tuple(a for a, _ in ap) + tuple(m for _, m in sc)

    lax.fori_loop(0, n_kv // kv_unroll, body,
                  tuple(ms) + tuple(a for a, _ in ap) + tuple(m for _, m in sc))
    _attn_finalize(acc_ref, o_ref, tq)


def _attn_finalize(acc_ref, o_ref, tq):
    pad = jnp.zeros((LANES - HEAD_DIM, tq), F32)
    for g in range(GQA_GROUP):
        acc = acc_ref[g]
        o = acc[0:HEAD_DIM] / acc[HEAD_DIM:HEAD_DIM + 1, :]
        o_ref[0, :, g * LANES:(g + 1) * LANES] = jnp.concatenate([o, pad], axis=0).T.astype(BF16)


def _attn_bounded_kernel(qt_ref, k_ref, vt_ref, o_ref, acc_ref, s_ref, p_ref, *, tq, tk, n_kv, kv_unroll):
    hp = HEADS_PER_STEP
    spt = GQA_GROUP // hp
    acc_ref[...] = jnp.zeros(acc_ref.shape, F32)

    def scores(j, sp):
        kt = k_ref[0, pl.ds(pl.multiple_of(j * tk, tk), tk), :]
        return [jnp.dot(kt, qt_ref[0, (sp * hp + u) * LANES:(sp * hp + u + 1) * LANES, :],
                        preferred_element_type=F32) for u in range(hp)]

    def probs(sc):
        return [jnp.exp2(s).astype(BF16) for s in sc]

    def pv_stage(j, sp, ps):
        vt = vt_ref[0, 0:PV_ROWS, pl.ds(pl.multiple_of(j * tk, tk), tk)]
        for u, p in enumerate(ps):
            acc_ref[sp * hp + u] += jnp.dot(vt, p, preferred_element_type=F32)

    ps = probs(scores(0, 0))
    sc = scores(min(1 // spt, n_kv - 1), 1 % spt)
    for u in range(hp):
        s_ref[u] = sc[u]
        p_ref[u] = ps[u]

    def body(it, carry):
        ps = [p_ref[u] for u in range(hp)]
        sc = [s_ref[u] for u in range(hp)]
        for n in range(kv_unroll * spt):
            j = it * kv_unroll + n // spt
            j_next = jnp.minimum(it * kv_unroll + (n + 2) // spt, n_kv - 1)
            sc_next = scores(j_next, (n + 2) % spt)
            ps_next = probs(sc)
            pv_stage(j, n % spt, ps)
            sc, ps = sc_next, ps_next
        for u in range(hp):
            s_ref[u] = sc[u]
            p_ref[u] = ps[u]
        return carry

    lax.fori_loop(0, n_kv // kv_unroll, body, 0)
    _attn_finalize(acc_ref, o_ref, tq)


def _attention(qt, k, vt, *, kernel, B, S):
    tq = min(TQ, S)
    tk = min(TK, S)
    gw = GQA_GROUP * LANES
    return pl.pallas_call(
        functools.partial(kernel, tq=tq, tk=tk, n_kv=S // tk,
                          kv_unroll=math.gcd(S // tk, KV_UNROLL)),
        grid=(B, N_KV_HEADS, S // tq),
        in_specs=[
            pl.BlockSpec((1, gw, tq), lambda b, h, i: (b, h, i)),
            pl.BlockSpec((1, S, LANES), lambda b, h, i: (b, 0, h)),
            pl.BlockSpec((1, LANES, S), lambda b, h, i: (b, h, 0)),
        ],
        out_specs=pl.BlockSpec((1, tq, gw), lambda b, h, i: (b, i, h)),
        out_shape=jax.ShapeDtypeStruct((B, S, N_Q_HEADS * LANES), BF16),
        scratch_shapes=[pltpu.VMEM((GQA_GROUP, PV_ROWS, tq), F32),
                        pltpu.VMEM((HEADS_PER_STEP, tk, tq), F32),
                        pltpu.VMEM((HEADS_PER_STEP, tk, tq), BF16)],
        compiler_params=_cparams(("parallel", "parallel", "parallel")),
        name=kernel.__name__.strip("_"),
    )(qt, k, vt)


def _scan_chunk(a, b, h_in, reverse):
    n = a.shape[0]
    n_groups = n // SUBLANES
    a = a.reshape(n_groups, SUBLANES, LANES)
    b = b.reshape(n_groups, SUBLANES, LANES)
    sub = lax.broadcasted_iota(jnp.int32, a.shape, 1)
    d = 1
    while d < SUBLANES:
        if reverse:
            keep = sub < SUBLANES - d
            shift = SUBLANES - d
        else:
            keep = sub >= d
            shift = d
        a_sh = jnp.where(keep, pltpu.roll(a, shift, 1), 1.0)
        b_sh = jnp.where(keep, pltpu.roll(b, shift, 1), 0.0)
        b = a * b_sh + b
        a = a * a_sh
        d *= 2
    a = a.reshape(n, LANES)
    b = b.reshape(n, LANES)
    order = range(n_groups - 1, -1, -1) if reverse else range(n_groups)
    edge = h_in
    out = [None] * n_groups
    for v in order:
        rows = slice(v * SUBLANES, (v + 1) * SUBLANES)
        hv = b[rows] + a[rows] * jnp.broadcast_to(edge, (SUBLANES, LANES))
        out[v] = hv
        edge = hv[0:1] if reverse else hv[SUBLANES - 1:SUBLANES]
    return jnp.concatenate(out, axis=0), edge


def _lru_pad_input(u_ref, up_ref, S):
    zeros = jnp.zeros((SUBLANES, LANES), F32)
    up_ref[0:SUBLANES, :] = zeros
    up_ref[S + SUBLANES:S + 2 * SUBLANES, :] = zeros
    up_ref[SUBLANES:S + SUBLANES, :] = u_ref[0]


def _lru_gates(up_ref, cw_ref, cb_ref, w_ref, bias_ref, lam_ref, t0, tc, d):
    cw = cw_ref[...]
    xc = cb_ref[...]
    for j in range(CONV_W):
        xc = xc + up_ref[pl.ds(t0 + SUBLANES + j - CONV_PAD_L, tc), :] * cw[j:j + 1, :]
    gw = 2 * LANES
    g = jnp.dot(xc.astype(BF16), w_ref[0, :, d * gw:(d + 1) * gw],
                preferred_element_type=F32) + bias_ref[0, :, d * gw:(d + 1) * gw]
    r = jax.nn.sigmoid(g[:, :LANES])
    i = jax.nn.sigmoid(g[:, LANES:])
    a = jnp.exp(-LRU_C * r * jax.nn.softplus(-lam_ref[d:d + 1, :]))
    y = 1.0 - a * a
    b = jnp.where(y > 0.0, y * lax.rsqrt(y), 0.0) * i * xc
    return a, b


def _lru_kernel(u_ref, gate_ref, cw_ref, cb_ref, w_ref, bias_ref, lam_ref, o_ref,
                up_ref, hf_ref, *, S, tc):
    _lru_pad_input(u_ref, up_ref, S)
    n_chunks = S // tc
    params = (up_ref, cw_ref, cb_ref, w_ref, bias_ref, lam_ref)

    def fwd(c, h):
        t0 = pl.multiple_of(c * tc, tc)
        hc, h_last = _scan_chunk(*_lru_gates(*params, t0, tc, 0), h, False)
        hf_ref[pl.ds(t0, tc), :] = hc
        return h_last

    lax.fori_loop(0, n_chunks, fwd, jnp.zeros((1, LANES), F32))

    def bwd(ci, h):
        t0 = pl.multiple_of((n_chunks - 1 - ci) * tc, tc)
        hc, h_last = _scan_chunk(*_lru_gates(*params, t0, tc, 1), h, True)
        gate = gate_ref[0, pl.ds(t0, tc), :]
        o_ref[0, pl.ds(t0, tc), :] = (hf_ref[pl.ds(t0, tc), :] + hc) * jax.nn.gelu(gate)
        return h_last

    lax.fori_loop(0, n_chunks, bwd, jnp.zeros((1, LANES), F32))


def _block_diag_pairs(w):
    nb, bw, _ = w.shape
    w = w.reshape(nb // 2, 2, bw, bw)
    z = jnp.zeros_like(w[:, 0])
    top = jnp.concatenate([w[:, 0], z], axis=-1)
    bot = jnp.concatenate([z, w[:, 1]], axis=-1)
    return jnp.concatenate([top, bot], axis=-2)


def _lru_operands(conv_w, conv_b, wa, ba, wi, bi, lam):
    C = conv_b.shape[0]
    nc = C // LANES
    w = jnp.concatenate([_block_diag_pairs(wa[0]), _block_diag_pairs(wi[0]),
                         _block_diag_pairs(wa[1]), _block_diag_pairs(wi[1])], axis=-1).astype(BF16)
    bias = jnp.stack([ba[0].reshape(nc, LANES), bi[0].reshape(nc, LANES),
                      ba[1].reshape(nc, LANES), bi[1].reshape(nc, LANES)], axis=1)
    return conv_w, conv_b.reshape(1, C), w, bias.reshape(nc, 1, 4 * LANES), lam


def _lru_specs(S, unit):
    seq = lambda *g: (unit(*g)[0], 0, unit(*g)[1])
    chan = lambda *g: (0, unit(*g)[1])
    blk = lambda *g: (unit(*g)[1], 0, 0)
    in_specs = [
        pl.BlockSpec((1, S, LANES), seq),
        pl.BlockSpec((1, S, LANES), seq),
        pl.BlockSpec((CONV_W, LANES), chan),
        pl.BlockSpec((1, LANES), chan),
        pl.BlockSpec((1, LANES, 4 * LANES), blk),
        pl.BlockSpec((1, 1, 4 * LANES), blk),
        pl.BlockSpec((2, LANES), chan),
    ]
    return in_specs, pl.BlockSpec((1, S, LANES), seq)


def _lru(lru_x, lru_gate, lru_ops, B, S):
    C = lru_x.shape[-1]
    tc = min(TC_LRU, S)
    in_specs, out_spec = _lru_specs(S, lambda b, c: (b, c))
    return pl.pallas_call(
        functools.partial(_lru_kernel, S=S, tc=tc),
        grid=(B, C // LANES),
        in_specs=in_specs,
        out_specs=out_spec,
        out_shape=jax.ShapeDtypeStruct((B, S, C), F32),
        scratch_shapes=[
            pltpu.VMEM((S + 2 * SUBLANES, LANES), F32),
            pltpu.VMEM((S, LANES), F32),
        ],
        compiler_params=_cparams(("parallel", "parallel")),
        name="rglru",
    )(lru_x, lru_gate, *lru_ops)


def _mixers(qt, k, vt, lru_x, lru_gate, lru_ops, score_bound, B, S):
    attn = lax.cond(score_bound <= SAFE_SCORE_LOG2,
                    functools.partial(_attention, kernel=_attn_bounded_kernel, B=B, S=S),
                    functools.partial(_attention, kernel=_attn_kernel, B=B, S=S), qt, k, vt)
    return attn, _lru(lru_x, lru_gate, lru_ops, B, S)


def _rows_to_slabs(ref, x):
    n = x.shape[0]
    for s in range(SUBLANES):
        ref[pl.ds(s, n, stride=SUBLANES), :] = x[:, s * LANES:(s + 1) * LANES]


def _slabs_to_rows(ref, n):
    return jnp.concatenate([ref[pl.ds(s, n, stride=SUBLANES), :] for s in range(SUBLANES)], axis=1)


def _slab(ref, r):
    return ref.at[pl.ds(pl.multiple_of(r * SUBLANES, SUBLANES), SUBLANES)]


def _outproj_kernel(a_ref, l_ref, x_ref, ag_ref, lg_ref, wa_ref, wl_ref, g2_ref,
                    wrh_ref, wrl_ref, br_ref, tri_ref,
                    x1_ref, xn3_ref, route_ref, gates_ref, cnt_ref, carry_ref, *, attn_w, lru_w):
    step = pl.program_id(0)

    @pl.when(step == 0)
    def _():
        carry_ref[...] = jnp.zeros_like(carry_ref)

    a = a_ref[...].astype(F32)
    ams = jnp.sum(a * a, axis=-1, keepdims=True) * (1.0 / attn_w)
    an = a * lax.rsqrt(ams + NORM_EPS) * ag_ref[...]
    l = l_ref[...]
    lms = jnp.sum(l * l, axis=-1, keepdims=True) * (1.0 / lru_w)
    ln = l * lax.rsqrt(lms + NORM_EPS) * lg_ref[...]
    mix = (jnp.dot(an.astype(BF16), wa_ref[...], preferred_element_type=F32)
           + jnp.dot(ln.astype(BF16), wl_ref[...], preferred_element_type=F32))
    x1 = x_ref[...] + mix
    x1_ref[...] = x1
    ms = jnp.mean(x1 * x1, axis=-1, keepdims=True)
    xn = x1 * lax.rsqrt(ms + NORM_EPS) * g2_ref[...]
    _rows_to_slabs(xn3_ref, xn)

    hi = xn.astype(BF16)
    lo = (xn - hi.astype(F32)).astype(BF16)
    logits = (jnp.dot(hi, wrh_ref[...], preferred_element_type=F32)
              + jnp.dot(lo, wrh_ref[...], preferred_element_type=F32)
              + jnp.dot(hi, wrl_ref[...], preferred_element_type=F32)) + br_ref[...]
    lane = lax.broadcasted_iota(jnp.int32, logits.shape, 1)
    neg = -jnp.inf
    work = jnp.where(lane < N_EXPERTS, logits, neg)
    sel = jnp.zeros(logits.shape, F32)
    idxs, vals = [], []
    for _ in range(TOP_K):
        m = jnp.max(work, axis=1, keepdims=True)
        idx = jnp.min(jnp.where(work == m, lane, LANES), axis=1, keepdims=True)
        hit = lane == idx
        work = jnp.where(hit, neg, work)
        sel = sel + hit.astype(F32)
        idxs.append(idx)
        vals.append(m)
    es = [jnp.exp(v - vals[0]) for v in vals]
    den = es[0] + es[1] + es[2] + es[3]

    prefix = jnp.dot(tri_ref[...], sel.astype(BF16), preferred_element_type=F32) + carry_ref[...]
    carry_ref[...] = carry_ref[...] + jnp.sum(sel, axis=0, keepdims=True)
    cnt_ref[...] = carry_ref[...]

    route = jnp.zeros(logits.shape, jnp.int32)
    gates = jnp.zeros(logits.shape, F32)
    for k in range(TOP_K):
        rank = jnp.sum(jnp.where(lane == idxs[k], prefix, 0.0), axis=1, keepdims=True).astype(jnp.int32)
        route = jnp.where(lane == k, idxs[k], route)
        route = jnp.where(lane == TOP_K + k, rank, route)
        gates = jnp.where(lane == k, es[k] / den, gates)
    route_ref[...] = route
    gates_ref[...] = gates


def _outproj_router(attn, lru, x2, attn_out_g, lru_out_g, w_out, norm2_g, w_router, b_router):
    T, D = x2.shape
    lru_w = lru.shape[-1]
    ts = min(TS_OUT, T)
    wa = w_out[:ATTN_W].reshape(N_Q_HEADS, HEAD_DIM, D)
    wa = jnp.pad(wa, ((0, 0), (0, LANES - HEAD_DIM), (0, 0))).reshape(N_Q_HEADS * LANES, D).astype(BF16)
    wl = w_out[ATTN_W:].astype(BF16)
    ag = _pad_heads(attn_out_g.reshape(1, ATTN_W), N_Q_HEADS)
    wr = jnp.pad(w_router, ((0, 0), (0, LANES - N_EXPERTS)))
    wrh = wr.astype(BF16)
    wrl = (wr - wrh.astype(F32)).astype(BF16)
    br = jnp.pad(b_router.reshape(1, N_EXPERTS), ((0, 0), (0, LANES - N_EXPERTS)))
    tri = (jnp.arange(ts)[:, None] > jnp.arange(ts)[None, :]).astype(BF16)
    const = lambda i: (0, 0)
    tok = lambda i: (i, 0)
    aw = N_Q_HEADS * LANES
    return pl.pallas_call(
        functools.partial(_outproj_kernel, attn_w=ATTN_W, lru_w=lru_w),
        grid=(T // ts,),
        in_specs=[
            pl.BlockSpec((ts, aw), tok),
            pl.BlockSpec((ts, lru_w), tok),
            pl.BlockSpec((ts, D), tok),
            pl.BlockSpec((1, aw), const),
            pl.BlockSpec((1, lru_w), const),
            pl.BlockSpec((aw, D), const),
            pl.BlockSpec((lru_w, D), const),
            pl.BlockSpec((1, D), const),
            pl.BlockSpec((D, LANES), const),
            pl.BlockSpec((D, LANES), const),
            pl.BlockSpec((1, LANES), const),
            pl.BlockSpec((ts, ts), const),
        ],
        out_specs=[
            pl.BlockSpec((ts, D), tok),
            pl.BlockSpec((ts * SUBLANES, LANES), tok),
            pl.BlockSpec((ts, LANES), tok),
            pl.BlockSpec((ts, LANES), tok),
            pl.BlockSpec((1, LANES), const),
        ],
        out_shape=[
            jax.ShapeDtypeStruct((T, D), F32),
            jax.ShapeDtypeStruct((T * SUBLANES, LANES), F32),
            jax.ShapeDtypeStruct((T, LANES), jnp.int32),
            jax.ShapeDtypeStruct((T, LANES), F32),
            jax.ShapeDtypeStruct((1, LANES), F32),
        ],
        scratch_shapes=[pltpu.VMEM((1, LANES), F32)],
        compiler_params=_cparams(("arbitrary",)),
        name="outproj_router",
    )(attn, lru, x2, ag, lru_out_g.reshape(1, lru_w), wa, wl, norm2_g.reshape(1, D),
      wrh, wrl, br, tri)


def _dest_kernel(route_ref, pstart_ref, dest_ref):
    route = route_ref[...]
    lane = lax.broadcasted_iota(jnp.int32, route.shape, 1)
    pstart = pstart_ref[...]
    dest = jnp.zeros(route.shape, jnp.int32)
    for k in range(TOP_K):
        start = jnp.sum(jnp.where(lane == route[:, k:k + 1], pstart, 0.0), axis=1, keepdims=True)
        dest = jnp.where(lane == k, start.astype(jnp.int32) + route[:, TOP_K + k:TOP_K + k + 1], dest)
    dest_ref[...] = dest


def _dest_rows(route, pstart):
    T = route.shape[0]
    ts = math.gcd(TS_DEST, T)
    row = jnp.pad(pstart.astype(F32).reshape(1, N_EXPERTS), ((0, 0), (0, LANES - N_EXPERTS)))
    dest = pl.pallas_call(
        _dest_kernel,
        grid=(T // ts,),
        in_specs=[pl.BlockSpec((ts, LANES), lambda i: (i, 0)),
                  pl.BlockSpec((1, LANES), lambda i: (0, 0))],
        out_specs=pl.BlockSpec((ts, LANES), lambda i: (i, 0)),
        out_shape=jax.ShapeDtypeStruct((T, LANES), jnp.int32),
        compiler_params=_cparams(("parallel",)),
        name="dest_rows",
    )(route, row)
    return dest[:, :TOP_K].reshape(T * TOP_K)


def _dispatch_kernel(fill_ref, dest_ref, x_ref, out_hbm, zero_ref, sem, zero_sem, *, ts, n_blocks):
    block_slabs = ROW_BLOCK * SUBLANES

    def fill_copy(b):
        off = pl.multiple_of(b * block_slabs, block_slabs)
        return pltpu.make_async_copy(zero_ref, out_hbm.at[pl.ds(off, block_slabs)], zero_sem)

    @pl.when(pl.program_id(0) == 0)
    def _():
        zero_ref[...] = jnp.zeros(zero_ref.shape, F32)

        def start(b, carry):
            @pl.when(fill_ref[b] != 0)
            def _():
                fill_copy(b).start()
            return carry

        def wait(b, carry):
            @pl.when(fill_ref[b] != 0)
            def _():
                fill_copy(b).wait()
            return carry

        lax.fori_loop(0, n_blocks, start, 0)
        lax.fori_loop(0, n_blocks, wait, 0)

    def issue(i, carry):
        for j in range(ISSUE_UNROLL):
            r = i * ISSUE_UNROLL + j
            for k in range(TOP_K):
                d = dest_ref[r * TOP_K + k]
                pltpu.make_async_copy(_slab(x_ref, r), _slab(out_hbm, d), sem).start(priority=k % 2)
        return carry

    lax.fori_loop(0, ts // ISSUE_UNROLL, issue, 0)
    for k in range(TOP_K):
        pltpu.make_async_copy(x_ref, out_hbm.at[pl.ds(0, ts * SUBLANES)], sem).wait()


def _dispatch(xn_slabs, fill, dest_flat, n_rows):
    T = xn_slabs.shape[0] // SUBLANES
    ts = min(TS_DISP, T)
    grid_spec = pltpu.PrefetchScalarGridSpec(
        num_scalar_prefetch=1,
        grid=(T // ts,),
        in_specs=[
            pl.BlockSpec((ts * TOP_K,), lambda i, fl: (i,), memory_space=pltpu.SMEM),
            pl.BlockSpec((ts * SUBLANES, LANES), lambda i, fl: (i, 0)),
        ],
        out_specs=pl.BlockSpec(memory_space=pl.ANY),
        scratch_shapes=[pltpu.VMEM((ROW_BLOCK * SUBLANES, LANES), F32),
                        pltpu.SemaphoreType.DMA, pltpu.SemaphoreType.DMA],
    )
    return pl.pallas_call(
        functools.partial(_dispatch_kernel, ts=ts, n_blocks=n_rows // ROW_BLOCK),
        grid_spec=grid_spec,
        out_shape=jax.ShapeDtypeStruct((n_rows * SUBLANES, LANES), xn_slabs.dtype),
        compiler_params=_cparams(("arbitrary",)),
        name="dispatch",
    )(fill, dest_flat, xn_slabs)


def _expert_kernel(be_ref, na_ref, nxt_ref, x_ref, wg_hbm, bg_ref, wu_hbm, bu_ref, wd_hbm, bd_ref,
                   y_ref, stage_ref, wb_ref, slot_ref, sems):
    i = pl.program_id(0)
    e = be_ref[i]
    w_hbm = (wg_hbm, wu_hbm, wd_hbm)

    def fetch(expert, slot, m):
        return pltpu.make_async_copy(w_hbm[m].at[expert], stage_ref.at[slot, m], sems.at[slot, m])

    @pl.when(i == 0)
    def _():
        slot_ref[0] = 0
        for m in range(3):
            fetch(e, 0, m).start()

    active = i < na_ref[0]
    first = jnp.logical_or(i == 0, e != be_ref[jnp.maximum(i - 1, 0)])

    @pl.when(jnp.logical_and(active, first))
    def _():
        slot = slot_ref[0]
        for m in range(3):
            fetch(e, slot, m).wait()
            wb_ref[m] = stage_ref[slot, m].astype(BF16)

        @pl.when(nxt_ref[i] >= 0)
        def _():
            for m in range(3):
                fetch(nxt_ref[i], 1 - slot, m).start(priority=1)

        slot_ref[0] = 1 - slot

    @pl.when(active)
    def _():
        x = _slabs_to_rows(x_ref, ROW_BLOCK).astype(BF16)
        g = jnp.dot(x, wb_ref[0], preferred_element_type=F32) + bg_ref[0]
        u = jnp.dot(x, wb_ref[1], preferred_element_type=F32) + bu_ref[0]
        g = jnp.minimum(g, SWIGLU_LIMIT)
        u = jnp.clip(u, -SWIGLU_LIMIT, SWIGLU_LIMIT)
        glu = g * jax.nn.sigmoid(SWIGLU_ALPHA * g)
        y = jnp.dot(((u + 1.0) * glu).astype(BF16), wb_ref[2], preferred_element_type=F32) + bd_ref[0]
        _rows_to_slabs(y_ref, y)


def _experts(x_rows, block_e, n_active, next_e, w_gate, b_gate, w_up, b_up, w_down, b_down):
    E, D, FF = w_gate.shape
    assert D == FF, "the three expert matrices share one staging shape"
    block_slabs = ROW_BLOCK * SUBLANES
    n_blocks = x_rows.shape[0] // block_slabs

    def row_map(i, be, na, nx):
        return (jnp.minimum(i, na[0] - 1), 0)

    def b_map(i, be, na, nx):
        return (be[jnp.minimum(i, na[0] - 1)], 0, 0)

    grid_spec = pltpu.PrefetchScalarGridSpec(
        num_scalar_prefetch=3,
        grid=(n_blocks,),
        in_specs=[
            pl.BlockSpec((block_slabs, LANES), row_map),
            pl.BlockSpec(memory_space=pl.ANY),
            pl.BlockSpec((1, 1, FF), b_map),
            pl.BlockSpec(memory_space=pl.ANY),
            pl.BlockSpec((1, 1, FF), b_map),
            pl.BlockSpec(memory_space=pl.ANY),
            pl.BlockSpec((1, 1, D), b_map),
        ],
        out_specs=pl.BlockSpec((block_slabs, LANES), row_map),
        scratch_shapes=[
            pltpu.VMEM((2, 3, D, FF), F32),
            pltpu.VMEM((3, D, FF), BF16),
            pltpu.SMEM((1,), jnp.int32),
            pltpu.SemaphoreType.DMA((2, 3)),
        ],
    )
    return pl.pallas_call(
        _expert_kernel,
        grid_spec=grid_spec,
        out_shape=jax.ShapeDtypeStruct(x_rows.shape, F32),
        input_output_aliases={3: 0},
        compiler_params=pltpu.CompilerParams(dimension_semantics=("arbitrary",),
                                             vmem_limit_bytes=EXPERT_VMEM_LIMIT),
        name="experts",
    )(block_e, n_active, next_e, x_rows, w_gate, b_gate.reshape(E, 1, FF), w_up,
      b_up.reshape(E, 1, FF), w_down, b_down.reshape(E, 1, D))


def _combine_kernel(dest_ref, dest_next_ref, y_hbm, x1_ref, gates_ref, fg_ref, o_ref, bufs, sems,
                    *, ts, n_steps):
    i = pl.program_id(0)
    slot = i % 2

    def gather_tile(d_ref, s):
        def issue(it, carry):
            for j in range(ISSUE_UNROLL):
                r = it * ISSUE_UNROLL + j
                for k in range(TOP_K):
                    d = d_ref[r * TOP_K + k]
                    pltpu.make_async_copy(_slab(y_hbm, d), _slab(bufs.at[s, k], r),
                                          sems.at[s]).start(priority=k % 2)
            return carry

        lax.fori_loop(0, ts // ISSUE_UNROLL, issue, 0)

    @pl.when(i == 0)
    def _():
        gather_tile(dest_ref, 0)

    @pl.when(i + 1 < n_steps)
    def _():
        gather_tile(dest_next_ref, 1 - slot)

    for k in range(TOP_K):
        pltpu.make_async_copy(y_hbm.at[pl.ds(0, ts * SUBLANES)], bufs.at[slot, k], sems.at[slot]).wait()

    acc = x1_ref[...]
    gates = gates_ref[...]
    for k in range(TOP_K):
        acc = acc + _slabs_to_rows(bufs.at[slot, k], ts) * gates[:, k:k + 1]
    ms = jnp.mean(acc * acc, axis=-1, keepdims=True)
    o_ref[...] = acc * lax.rsqrt(ms + NORM_EPS) * fg_ref[...]


def _combine(y_rows, dest_flat, x1, gates, final_g):
    T, D = x1.shape
    ts = min(TS_COMB, T)
    n_steps = T // ts
    tok = lambda i: (i, 0)
    return pl.pallas_call(
        functools.partial(_combine_kernel, ts=ts, n_steps=n_steps),
        grid=(n_steps,),
        in_specs=[
            pl.BlockSpec((ts * TOP_K,), lambda i: (i,), memory_space=pltpu.SMEM),
            pl.BlockSpec((ts * TOP_K,), lambda i: (jnp.minimum(i + 1, n_steps - 1),),
                         memory_space=pltpu.SMEM),
            pl.BlockSpec(memory_space=pl.ANY),
            pl.BlockSpec((ts, D), tok),
            pl.BlockSpec((ts, LANES), tok),
            pl.BlockSpec((1, D), lambda i: (0, 0)),
        ],
        out_specs=pl.BlockSpec((ts, D), tok),
        out_shape=jax.ShapeDtypeStruct((T, D), F32),
        scratch_shapes=[pltpu.VMEM((2, TOP_K, ts * SUBLANES, LANES), F32),
                        pltpu.SemaphoreType.DMA((2,))],
        compiler_params=_cparams(("arbitrary",)),
        name="combine",
    )(dest_flat, dest_flat, y_rows, x1, gates, final_g.reshape(1, D))


def kernel(x, norm1_g, w_in, q_norm_g, k_norm_g, conv_w, conv_b, lru_wa, lru_ba, lru_wi, lru_bi,
           lru_lam, attn_out_g, lru_out_g, w_out, norm2_g, w_router, b_router, w_gate, b_gate,
           w_up, b_up, w_down, b_down, final_g):
    B, S, D = x.shape
    T = B * S
    assert w_in.shape[0] == 1, "single-layer trunk: the final norm is fused into the layer's combine"
    x2 = x.reshape(T, D)
    for l in range(1):
        qt, k, vt, lru_x, lru_gate = _inproj(x2, norm1_g[l], w_in[l], q_norm_g[l], k_norm_g[l], S)
        score_bound = (HEAD_DIM * Q_SCALE * jnp.max(jnp.abs(q_norm_g[l]))
                       * jnp.max(jnp.abs(k_norm_g[l])))
        lru_ops = _lru_operands(conv_w[l], conv_b[l], lru_wa[l], lru_ba[l], lru_wi[l], lru_bi[l],
                                lru_lam[l])
        attn, lru = _mixers(qt, k.reshape(B, S, -1), vt, lru_x.reshape(B, S, -1),
                            lru_gate.reshape(B, S, -1), lru_ops, score_bound, B, S)
        x1, xn3, route, gates, cnt = _outproj_router(
            attn.reshape(T, -1), lru.reshape(T, -1), x2, attn_out_g[l], lru_out_g[l], w_out[l],
            norm2_g[l], w_router[l], b_router[l])

        counts = cnt[0, :N_EXPERTS].astype(jnp.int32)
        padded = ((counts + ROW_BLOCK - 1) // ROW_BLOCK) * ROW_BLOCK
        pend = jnp.cumsum(padded)
        pstart = (pend - padded).astype(jnp.int32)
        n_rows = T * TOP_K + N_EXPERTS * ROW_BLOCK
        block_start = jnp.arange(n_rows // ROW_BLOCK, dtype=jnp.int32) * ROW_BLOCK
        block_e = jnp.sum((pend[None, :] <= block_start[:, None]).astype(jnp.int32), axis=1)
        block_e = jnp.minimum(block_e, N_EXPERTS - 1)
        n_active = (pend[-1:] // ROW_BLOCK).astype(jnp.int32)
        fill = jnp.logical_or(block_start + ROW_BLOCK == pend[block_e],
                              block_start >= pend[-1]).astype(jnp.int32)

        next_block = pend[block_e] // ROW_BLOCK
        next_e = jnp.where(next_block < n_active[0],
                           block_e[jnp.minimum(next_block, block_e.shape[0] - 1)], -1).astype(jnp.int32)

        dest_flat = _dest_rows(route, pstart)
        x_rows = _dispatch(xn3, fill, dest_flat, n_rows)
        y_rows = _experts(x_rows, block_e, n_active, next_e, w_gate[l], b_gate[l], w_up[l], b_up[l],
                          w_down[l], b_down[l])
        x2 = _combine(y_rows, dest_flat, x1, gates, final_g)
    return x2.reshape(B, S, D)
```

```python
import functools
import math

import jax
import jax.numpy as jnp
from jax import lax
from jax.experimental import pallas as pl
from jax.experimental.pallas import tpu as pltpu

F32 = jnp.float32
BF16 = jnp.bfloat16

GRID_W = 64
HEAD_DIM = 64
N_Q_HEADS = 8
N_KV_HEADS = 2
GQA_GROUP = N_Q_HEADS // N_KV_HEADS
ATTN_W = N_Q_HEADS * HEAD_DIM
KV_W = N_KV_HEADS * HEAD_DIM
LRU_BLOCKS = 8
LRU_C = 8.0
CONV_W = 4
CONV_PAD_L = 2
ROPE_THETA = 10000.0
ROPE_HALF = HEAD_DIM // 2
ROPE_M = ROPE_HALF // 2
N_EXPERTS = 32
TOP_K = 4
SWIGLU_ALPHA = 1.702
SWIGLU_LIMIT = 7.0
NORM_EPS = 1e-5
QK_EPS = 1e-6
LOG2_E = 1.4426950408889634
Q_SCALE = HEAD_DIM ** -0.5 * LOG2_E
SAFE_SCORE_LOG2 = 96.0

LANES = 128
SUBLANES = 8
BF16_SUBLANES = 16
PV_ROWS = HEAD_DIM + BF16_SUBLANES
VMEM_LIMIT = 48 * 1024 * 1024
EXPERT_VMEM_LIMIT = 56 * 1024 * 1024

TS_IN = 512
TQ = 256
TK = 256
KV_UNROLL = 8
HEADS_PER_STEP = 2
TC_LRU = 512
TS_OUT = 512
TS_DEST = 2048
ROW_BLOCK = 512
TS_DISP = 512
TS_COMB = 256
ISSUE_UNROLL = 8


def _cparams(sem):
    return pltpu.CompilerParams(dimension_semantics=sem, vmem_limit_bytes=VMEM_LIMIT)


def _inproj_kernel(x_ref, g1_ref, wt_ref, w_ref, qg_ref, kg_ref, cos_ref, sin_ref, cost_ref, sint_ref,
                   q_ref, k_ref, v_ref, lx_ref, lg_ref, *, lru_w):
    x = x_ref[...]
    ms = jnp.mean(x * x, axis=-1, keepdims=True)
    xn = (x * lax.rsqrt(ms + NORM_EPS) * g1_ref[...]).astype(BF16)
    ht = lax.dot_general(wt_ref[...], xn, (((1,), (1,)), ((), ())), preferred_element_type=F32)
    h = jnp.dot(xn, w_ref[...], preferred_element_type=F32)

    qw = N_Q_HEADS * LANES
    kw = N_KV_HEADS * LANES
    cost = cost_ref[...]
    sint = sint_ref[...]
    row = lax.broadcasted_iota(jnp.int32, cost.shape, 0)
    first_half_t = (row % ROPE_HALF) < ROPE_M
    qg = qg_ref[...]
    for c in range(N_Q_HEADS):
        sl = slice(c * LANES, (c + 1) * LANES)
        xc = ht[sl]
        hms = jnp.sum(xc * xc, axis=0, keepdims=True) * (1.0 / HEAD_DIM)
        xc = xc * lax.rsqrt(hms + QK_EPS) * qg
        partner = jnp.where(first_half_t, pltpu.roll(xc, LANES - ROPE_M, 0), pltpu.roll(xc, ROPE_M, 0))
        q_ref[0, sl, :] = ((xc * cost + partner * sint) * Q_SCALE).astype(BF16)
    for c in range(N_KV_HEADS):
        sl = slice(c * LANES, (c + 1) * LANES)
        v_ref[0, sl, :] = jnp.where(row >= HEAD_DIM, 1.0, ht[qw + c * LANES: qw + (c + 1) * LANES]).astype(BF16)

    cos = cos_ref[...]
    sin = sin_ref[...]
    lane = lax.broadcasted_iota(jnp.int32, cos.shape, 1)
    first_half = (lane % ROPE_HALF) < ROPE_M
    for c in range(N_KV_HEADS):
        sl = slice(c * LANES, (c + 1) * LANES)
        xc = h[:, sl]
        hms = jnp.sum(xc * xc, axis=-1, keepdims=True) * (1.0 / HEAD_DIM)
        xc = xc * lax.rsqrt(hms + QK_EPS) * kg_ref[...]
        partner = jnp.where(first_half, pltpu.roll(xc, LANES - ROPE_M, 1), pltpu.roll(xc, ROPE_M, 1))
        k_ref[:, sl] = (xc * cos + partner * sin).astype(BF16)
    lx_ref[...] = h[:, kw: kw + lru_w]
    lg_ref[...] = h[:, kw + lru_w: kw + 2 * lru_w]


def _pad_heads(w, n_heads):
    lead = w.shape[:-1]
    w = w.reshape(lead + (n_heads, HEAD_DIM))
    w = jnp.pad(w, [(0, 0)] * len(lead) + [(0, 0), (0, LANES - HEAD_DIM)])
    return w.reshape(lead + (n_heads * LANES,))


def _rope_tables(S):
    t = jnp.arange(S)
    rows = (t // GRID_W).astype(F32)
    cols = (t % GRID_W).astype(F32)
    inv_freq = ROPE_THETA ** (-jnp.arange(ROPE_M, dtype=F32) / ROPE_M)
    ar = rows[:, None] * inv_freq[None, :]
    ac = cols[:, None] * inv_freq[None, :]
    cos = jnp.concatenate([jnp.cos(ar), jnp.cos(ar), jnp.cos(ac), jnp.cos(ac)], axis=-1)
    sin = jnp.concatenate([-jnp.sin(ar), jnp.sin(ar), -jnp.sin(ac), jnp.sin(ac)], axis=-1)
    pad = [(0, 0), (0, LANES - HEAD_DIM)]
    return jnp.pad(cos, pad), jnp.pad(sin, pad)


def _inproj(x2, norm1_g, w_in, q_norm_g, k_norm_g, S):
    T, D = x2.shape
    lru_w = (w_in.shape[1] - ATTN_W - 2 * KV_W) // 2
    o0, o1, o2 = ATTN_W, ATTN_W + KV_W, ATTN_W + 2 * KV_W
    w_t = jnp.concatenate([_pad_heads(w_in[:, :o0], N_Q_HEADS),
                           _pad_heads(w_in[:, o1:o2], N_KV_HEADS)], axis=1).T.astype(BF16)
    w_rest = jnp.concatenate([_pad_heads(w_in[:, o0:o1], N_KV_HEADS), w_in[:, o2:]],
                             axis=1).astype(BF16)
    qg = _pad_heads(q_norm_g.reshape(1, HEAD_DIM), 1).reshape(LANES, 1)
    kg = _pad_heads(k_norm_g.reshape(1, HEAD_DIM), 1)
    cos, sin = _rope_tables(S)
    ts = TS_IN
    n_s = S // ts
    qw, kw = N_Q_HEADS * LANES, N_KV_HEADS * LANES
    const = lambda i: (0, 0)
    tok = lambda i: (i, 0)
    pos = lambda i: (i % n_s, 0)
    pos_t = lambda i: (0, i % n_s)
    tposed = lambda i: (i // n_s, 0, i % n_s)
    return pl.pallas_call(
        functools.partial(_inproj_kernel, lru_w=lru_w),
        grid=(T // ts,),
        in_specs=[
            pl.BlockSpec((ts, D), tok),
            pl.BlockSpec((1, D), const),
            pl.BlockSpec(w_t.shape, const),
            pl.BlockSpec(w_rest.shape, const),
            pl.BlockSpec((LANES, 1), const),
            pl.BlockSpec((1, LANES), const),
            pl.BlockSpec((ts, LANES), pos),
            pl.BlockSpec((ts, LANES), pos),
            pl.BlockSpec((LANES, ts), pos_t),
            pl.BlockSpec((LANES, ts), pos_t),
        ],
        out_specs=[
            pl.BlockSpec((1, qw, ts), tposed),
            pl.BlockSpec((ts, kw), tok),
            pl.BlockSpec((1, kw, ts), tposed),
            pl.BlockSpec((ts, lru_w), tok),
            pl.BlockSpec((ts, lru_w), tok),
        ],
        out_shape=[
            jax.ShapeDtypeStruct((T // S, qw, S), BF16),
            jax.ShapeDtypeStruct((T, kw), BF16),
            jax.ShapeDtypeStruct((T // S, kw, S), BF16),
            jax.ShapeDtypeStruct((T, lru_w), F32),
            jax.ShapeDtypeStruct((T, lru_w), F32),
        ],
        compiler_params=_cparams(("parallel",)),
        name="inproj",
    )(x2, norm1_g.reshape(1, D), w_t, w_rest, qg, kg, cos, sin, cos.T, sin.T)


def _attn_kernel(qt_ref, k_ref, vt_ref, o_ref, acc_ref, s_ref, p_ref, *, tq, tk, n_kv, kv_unroll):
    hp = HEADS_PER_STEP
    spt = GQA_GROUP // hp
    acc_ref[...] = jnp.zeros(acc_ref.shape, F32)

    def scores(j, sp):
        kt = k_ref[0, pl.ds(pl.multiple_of(j * tk, tk), tk), :]
        out = []
        for u in range(hp):
            g = sp * hp + u
            s = jnp.dot(kt, qt_ref[0, g * LANES:(g + 1) * LANES, :], preferred_element_type=F32)
            out.append((s, jnp.max(s, axis=0, keepdims=True)))
        return out

    def softmax_stage(sc, ms, sp):
        out = []
        for u, (s, s_max) in enumerate(sc):
            h = sp * hp + u
            m_new = jnp.maximum(ms[h], s_max)
            out.append((jnp.exp2(ms[h] - m_new), jnp.exp2(s - m_new).astype(BF16)))
            ms[h] = m_new
        return out

    def pv_stage(j, sp, ap):
        vt = vt_ref[0, 0:PV_ROWS, pl.ds(pl.multiple_of(j * tk, tk), tk)]
        for u, (alpha, p) in enumerate(ap):
            g = sp * hp + u
            acc_ref[g] = alpha * acc_ref[g] + jnp.dot(vt, p, preferred_element_type=F32)

    ms = [jnp.full((1, tq), -jnp.inf, F32)] * GQA_GROUP
    ap = softmax_stage(scores(0, 0), ms, 0)
    sc = scores(min(1 // spt, n_kv - 1), 1 % spt)
    for u in range(hp):
        s_ref[u] = sc[u][0]
        p_ref[u] = ap[u][1]

    def body(it, carry):
        ms = list(carry[:GQA_GROUP])
        ap = [(carry[GQA_GROUP + u], p_ref[u]) for u in range(hp)]
        sc = [(s_ref[u], carry[GQA_GROUP + hp + u]) for u in range(hp)]
        for n in range(kv_unroll * spt):
            j = it * kv_unroll + n // spt
            j_next = jnp.minimum(it * kv_unroll + (n + 2) // spt, n_kv - 1)
            sc_next = scores(j_next, (n + 2) % spt)
            ap_next = softmax_stage(sc, ms, (n + 1) % spt)
            pv_stage(j, n % spt, ap)
            sc, ap = sc_next, ap_next
        for u in range(hp):
            s_ref[u] = sc[u][0]
            p_ref[u] = ap[u][1]
        return tuple(ms) + tuple(a for a, _ in ap) + tuple(m for _, m in sc)

    lax.fori_loop(0, n_kv // kv_unroll, body,
                  tuple(ms) + tuple(a for a, _ in ap) + tuple(m for _, m in sc))
    _attn_finalize(acc_ref, o_ref, tq)


def _attn_finalize(acc_ref, o_ref, tq):
    pad = jnp.zeros((LANES - HEAD_DIM, tq), F32)
    for g in range(GQA_GROUP):
        acc = acc_ref[g]
        o = acc[0:HEAD_DIM] / acc[HEAD_DIM:HEAD_DIM + 1, :]
        o_ref[0, :, g * LANES:(g + 1) * LANES] = jnp.concatenate([o, pad], axis=0).T.astype(BF16)


def _attn_bounded_kernel(qt_ref, k_ref, vt_ref, o_ref, acc_ref, s_ref, p_ref, *, tq, tk, n_kv, kv_unroll):
    hp = HEADS_PER_STEP
    spt = GQA_GROUP // hp
    acc_ref[...] = jnp.zeros(acc_ref.shape, F32)

    def scores(j, sp):
        kt = k_ref[0, pl.ds(pl.multiple_of(j * tk, tk), tk), :]
        return [jnp.dot(kt, qt_ref[0, (sp * hp + u) * LANES:(sp * hp + u + 1) * LANES, :],
                        preferred_element_type=F32) for u in range(hp)]

    def probs(sc):
        return [jnp.exp2(s).astype(BF16) for s in sc]

    def pv_stage(j, sp, ps):
        vt = vt_ref[0, 0:PV_ROWS, pl.ds(pl.multiple_of(j * tk, tk), tk)]
        for u, p in enumerate(ps):
            acc_ref[sp * hp + u] += jnp.dot(vt, p, preferred_element_type=F32)

    ps = probs(scores(0, 0))
    sc = scores(min(1 // spt, n_kv - 1), 1 % spt)
    for u in range(hp):
        s_ref[u] = sc[u]
        p_ref[u] = ps[u]

    def body(it, carry):
        ps = [p_ref[u] for u in range(hp)]
        sc = [s_ref[u] for u in range(hp)]
        for n in range(kv_unroll * spt):
            j = it * kv_unroll + n // spt
            j_next = jnp.minimum(it * kv_unroll + (n + 2) // spt, n_kv - 1)
            sc_next = scores(j_next, (n + 2) % spt)
            ps_next = probs(sc)
            pv_stage(j, n % spt, ps)
            sc, ps = sc_next, ps_next
        for u in range(hp):
            s_ref[u] = sc[u]
            p_ref[u] = ps[u]
        return carry

    lax.fori_loop(0, n_kv // kv_unroll, body, 0)
    _attn_finalize(acc_ref, o_ref, tq)


def _attention(qt, k, vt, *, kernel, B, S):
    tq = min(TQ, S)
    tk = min(TK, S)
    gw = GQA_GROUP * LANES
    return pl.pallas_call(
        functools.partial(kernel, tq=tq, tk=tk, n_kv=S // tk,
                          kv_unroll=math.gcd(S // tk, KV_UNROLL)),
        grid=(B, N_KV_HEADS, S // tq),
        in_specs=[
            pl.BlockSpec((1, gw, tq), lambda b, h, i: (b, h, i)),
            pl.BlockSpec((1, S, LANES), lambda b, h, i: (b, 0, h)),
            pl.BlockSpec((1, LANES, S), lambda b, h, i: (b, h, 0)),
        ],
        out_specs=pl.BlockSpec((1, tq, gw), lambda b, h, i: (b, i, h)),
        out_shape=jax.ShapeDtypeStruct((B, S, N_Q_HEADS * LANES), BF16),
        scratch_shapes=[pltpu.VMEM((GQA_GROUP, PV_ROWS, tq), F32),
                        pltpu.VMEM((HEADS_PER_STEP, tk, tq), F32),
                        pltpu.VMEM((HEADS_PER_STEP, tk, tq), BF16)],
        compiler_params=_cparams(("parallel", "parallel", "parallel")),
        name=kernel.__name__.strip("_"),
    )(qt, k, vt)


def _scan_chunk(a, b, h_in, reverse):
    n = a.shape[0]
    n_groups = n // SUBLANES
    a = a.reshape(n_groups, SUBLANES, LANES)
    b = b.reshape(n_groups, SUBLANES, LANES)
    sub = lax.broadcasted_iota(jnp.int32, a.shape, 1)
    d = 1
    while d < SUBLANES:
        if reverse:
            keep = sub < SUBLANES - d
            shift = SUBLANES - d
        else:
            keep = sub >= d
            shift = d
        a_sh = jnp.where(keep, pltpu.roll(a, shift, 1), 1.0)
        b_sh = jnp.where(keep, pltpu.roll(b, shift, 1), 0.0)
        b = a * b_sh + b
        a = a * a_sh
        d *= 2
    a = a.reshape(n, LANES)
    b = b.reshape(n, LANES)
    order = range(n_groups - 1, -1, -1) if reverse else range(n_groups)
    edge = h_in
    out = [None] * n_groups
    for v in order:
        rows = slice(v * SUBLANES, (v + 1) * SUBLANES)
        hv = b[rows] + a[rows] * jnp.broadcast_to(edge, (SUBLANES, LANES))
        out[v] = hv
        edge = hv[0:1] if reverse else hv[SUBLANES - 1:SUBLANES]
    return jnp.concatenate(out, axis=0), edge


def _lru_pad_input(u_ref, up_ref, S):
    zeros = jnp.zeros((SUBLANES, LANES), F32)
    up_ref[0:SUBLANES, :] = zeros
    up_ref[S + SUBLANES:S + 2 * SUBLANES, :] = zeros
    up_ref[SUBLANES:S + SUBLANES, :] = u_ref[0]


def _lru_gates(up_ref, cw_ref, cb_ref, w_ref, bias_ref, lam_ref, t0, tc, d):
    cw = cw_ref[...]
    xc = cb_ref[...]
    for j in range(CONV_W):
        xc = xc + up_ref[pl.ds(t0 + SUBLANES + j - CONV_PAD_L, tc), :] * cw[j:j + 1, :]
    gw = 2 * LANES
    g = jnp.dot(xc.astype(BF16), w_ref[0, :, d * gw:(d + 1) * gw],
                preferred_element_type=F32) + bias_ref[0, :, d * gw:(d + 1) * gw]
    r = jax.nn.sigmoid(g[:, :LANES])
    i = jax.nn.sigmoid(g[:, LANES:])
    a = jnp.exp(-LRU_C * r * jax.nn.softplus(-lam_ref[d:d + 1, :]))
    y = 1.0 - a * a
    b = jnp.where(y > 0.0, y * lax.rsqrt(y), 0.0) * i * xc
    return a, b


def _lru_kernel(u_ref, gate_ref, cw_ref, cb_ref, w_ref, bias_ref, lam_ref, o_ref,
                up_ref, hf_ref, *, S, tc):
    _lru_pad_input(u_ref, up_ref, S)
    n_chunks = S // tc
    params = (up_ref, cw_ref, cb_ref, w_ref, bias_ref, lam_ref)

    def fwd(c, h):
        t0 = pl.multiple_of(c * tc, tc)
        hc, h_last = _scan_chunk(*_lru_gates(*params, t0, tc, 0), h, False)
        hf_ref[pl.ds(t0, tc), :] = hc
        return h_last

    lax.fori_loop(0, n_chunks, fwd, jnp.zeros((1, LANES), F32))

    def bwd(ci, h):
        t0 = pl.multiple_of((n_chunks - 1 - ci) * tc, tc)
        hc, h_last = _scan_chunk(*_lru_gates(*params, t0, tc, 1), h, True)
        gate = gate_ref[0, pl.ds(t0, tc), :]
        o_ref[0, pl.ds(t0, tc), :] = (hf_ref[pl.ds(t0, tc), :] + hc) * jax.nn.gelu(gate)
        return h_last

    lax.fori_loop(0, n_chunks, bwd, jnp.zeros((1, LANES), F32))


def _block_diag_pairs(w):
    nb, bw, _ = w.shape
    w = w.reshape(nb // 2, 2, bw, bw)
    z = jnp.zeros_like(w[:, 0])
    top = jnp.concatenate([w[:, 0], z], axis=-1)
    bot = jnp.concatenate([z, w[:, 1]], axis=-1)
    return jnp.concatenate([top, bot], axis=-2)


def _lru_operands(conv_w, conv_b, wa, ba, wi, bi, lam):
    C = conv_b.shape[0]
    nc = C // LANES
    w = jnp.concatenate([_block_diag_pairs(wa[0]), _block_diag_pairs(wi[0]),
                         _block_diag_pairs(wa[1]), _block_diag_pairs(wi[1])], axis=-1).astype(BF16)
    bias = jnp.stack([ba[0].reshape(nc, LANES), bi[0].reshape(nc, LANES),
                      ba[1].reshape(nc, LANES), bi[1].reshape(nc, LANES)], axis=1)
    return conv_w, conv_b.reshape(1, C), w, bias.reshape(nc, 1, 4 * LANES), lam


def _lru_specs(S, unit):
    seq = lambda *g: (unit(*g)[0], 0, unit(*g)[1])
    chan = lambda *g: (0, unit(*g)[1])
    blk = lambda *g: (unit(*g)[1], 0, 0)
    in_specs = [
        pl.BlockSpec((1, S, LANES), seq),
        pl.BlockSpec((1, S, LANES), seq),
        pl.BlockSpec((CONV_W, LANES), chan),
        pl.BlockSpec((1, LANES), chan),
        pl.BlockSpec((1, LANES, 4 * LANES), blk),
        pl.BlockSpec((1, 1, 4 * LANES), blk),
        pl.BlockSpec((2, LANES), chan),
    ]
    return in_specs, pl.BlockSpec((1, S, LANES), seq)


def _lru(lru_x, lru_gate, lru_ops, B, S):
    C = lru_x.shape[-1]
    tc = min(TC_LRU, S)
    in_specs, out_spec = _lru_specs(S, lambda b, c: (b, c))
    return pl.pallas_call(
        functools.partial(_lru_kernel, S=S, tc=tc),
        grid=(B, C // LANES),
        in_specs=in_specs,
        out_specs=out_spec,
        out_shape=jax.ShapeDtypeStruct((B, S, C), F32),
        scratch_shapes=[
            pltpu.VMEM((S + 2 * SUBLANES, LANES), F32),
            pltpu.VMEM((S, LANES), F32),
        ],
        compiler_params=_cparams(("parallel", "parallel")),
        name="rglru",
    )(lru_x, lru_gate, *lru_ops)


def _mixers(qt, k, vt, lru_x, lru_gate, lru_ops, score_bound, B, S):
    attn = lax.cond(score_bound <= SAFE_SCORE_LOG2,
                    functools.partial(_attention, kernel=_attn_bounded_kernel, B=B, S=S),
                    functools.partial(_attention, kernel=_attn_kernel, B=B, S=S), qt, k, vt)
    return attn, _lru(lru_x, lru_gate, lru_ops, B, S)


def _rows_to_slabs(ref, x):
    n = x.shape[0]
    for s in range(SUBLANES):
        ref[pl.ds(s, n, stride=SUBLANES), :] = x[:, s * LANES:(s + 1) * LANES]


def _slabs_to_rows(ref, n):
    return jnp.concatenate([ref[pl.ds(s, n, stride=SUBLANES), :] for s in range(SUBLANES)], axis=1)


def _slab(ref, r):
    return ref.at[pl.ds(pl.multiple_of(r * SUBLANES, SUBLANES), SUBLANES)]


def _outproj_kernel(a_ref, l_ref, x_ref, ag_ref, lg_ref, wa_ref, wl_ref, g2_ref,
                    wrh_ref, wrl_ref, br_ref, tri_ref,
                    x1_ref, xn3_ref, route_ref, gates_ref, cnt_ref, carry_ref, *, attn_w, lru_w):
    step = pl.program_id(0)

    @pl.when(step == 0)
    def _():
        carry_ref[...] = jnp.zeros_like(carry_ref)

    a = a_ref[...].astype(F32)
    ams = jnp.sum(a * a, axis=-1, keepdims=True) * (1.0 / attn_w)
    an = a * lax.rsqrt(ams + NORM_EPS) * ag_ref[...]
    l = l_ref[...]
    lms = jnp.sum(l * l, axis=-1, keepdims=True) * (1.0 / lru_w)
    ln = l * lax.rsqrt(lms + NORM_EPS) * lg_ref[...]
    mix = (jnp.dot(an.astype(BF16), wa_ref[...], preferred_element_type=F32)
           + jnp.dot(ln.astype(BF16), wl_ref[...], preferred_element_type=F32))
    x1 = x_ref[...] + mix
    x1_ref[...] = x1
    ms = jnp.mean(x1 * x1, axis=-1, keepdims=True)
    xn = x1 * lax.rsqrt(ms + NORM_EPS) * g2_ref[...]
    _rows_to_slabs(xn3_ref, xn)

    hi = xn.astype(BF16)
    lo = (xn - hi.astype(F32)).astype(BF16)
    logits = (jnp.dot(hi, wrh_ref[...], preferred_element_type=F32)
              + jnp.dot(lo, wrh_ref[...], preferred_element_type=F32)
              + jnp.dot(hi, wrl_ref[...], preferred_element_type=F32)) + br_ref[...]
    lane = lax.broadcasted_iota(jnp.int32, logits.shape, 1)
    neg = -jnp.inf
    work = jnp.where(lane < N_EXPERTS, logits, neg)
    sel = jnp.zeros(logits.shape, F32)
    idxs, vals = [], []
    for _ in range(TOP_K):
        m = jnp.max(work, axis=1, keepdims=True)
        idx = jnp.min(jnp.where(work == m, lane, LANES), axis=1, keepdims=True)
        hit = lane == idx
        work = jnp.where(hit, neg, work)
        sel = sel + hit.astype(F32)
        idxs.append(idx)
        vals.append(m)
    es = [jnp.exp(v - vals[0]) for v in vals]
    den = es[0] + es[1] + es[2] + es[3]

    prefix = jnp.dot(tri_ref[...], sel.astype(BF16), preferred_element_type=F32) + carry_ref[...]
    carry_ref[...] = carry_ref[...] + jnp.sum(sel, axis=0, keepdims=True)
    cnt_ref[...] = carry_ref[...]

    route = jnp.zeros(logits.shape, jnp.int32)
    gates = jnp.zeros(logits.shape, F32)
    for k in range(TOP_K):
        rank = jnp.sum(jnp.where(lane == idxs[k], prefix, 0.0), axis=1, keepdims=True).astype(jnp.int32)
        route = jnp.where(lane == k, idxs[k], route)
        route = jnp.where(lane == TOP_K + k, rank, route)
        gates = jnp.where(lane == k, es[k] / den, gates)
    route_ref[...] = route
    gates_ref[...] = gates


def _outproj_router(attn, lru, x2, attn_out_g, lru_out_g, w_out, norm2_g, w_router, b_router):
    T, D = x2.shape
    lru_w = lru.shape[-1]
    ts = min(TS_OUT, T)
    wa = w_out[:ATTN_W].reshape(N_Q_HEADS, HEAD_DIM, D)
    wa = jnp.pad(wa, ((0, 0), (0, LANES - HEAD_DIM), (0, 0))).reshape(N_Q_HEADS * LANES, D).astype(BF16)
    wl = w_out[ATTN_W:].astype(BF16)
    ag = _pad_heads(attn_out_g.reshape(1, ATTN_W), N_Q_HEADS)
    wr = jnp.pad(w_router, ((0, 0), (0, LANES - N_EXPERTS)))
    wrh = wr.astype(BF16)
    wrl = (wr - wrh.astype(F32)).astype(BF16)
    br = jnp.pad(b_router.reshape(1, N_EXPERTS), ((0, 0), (0, LANES - N_EXPERTS)))
    tri = (jnp.arange(ts)[:, None] > jnp.arange(ts)[None, :]).astype(BF16)
    const = lambda i: (0, 0)
    tok = lambda i: (i, 0)
    aw = N_Q_HEADS * LANES
    return pl.pallas_call(
        functools.partial(_outproj_kernel, attn_w=ATTN_W, lru_w=lru_w),
        grid=(T // ts,),
        in_specs=[
            pl.BlockSpec((ts, aw), tok),
            pl.BlockSpec((ts, lru_w), tok),
            pl.BlockSpec((ts, D), tok),
            pl.BlockSpec((1, aw), const),
            pl.BlockSpec((1, lru_w), const),
            pl.BlockSpec((aw, D), const),
            pl.BlockSpec((lru_w, D), const),
            pl.BlockSpec((1, D), const),
            pl.BlockSpec((D, LANES), const),
            pl.BlockSpec((D, LANES), const),
            pl.BlockSpec((1, LANES), const),
            pl.BlockSpec((ts, ts), const),
        ],
        out_specs=[
            pl.BlockSpec((ts, D), tok),
            pl.BlockSpec((ts * SUBLANES, LANES), tok),
            pl.BlockSpec((ts, LANES), tok),
            pl.BlockSpec((ts, LANES), tok),
            pl.BlockSpec((1, LANES), const),
        ],
        out_shape=[
            jax.ShapeDtypeStruct((T, D), F32),
            jax.ShapeDtypeStruct((T * SUBLANES, LANES), F32),
            jax.ShapeDtypeStruct((T, LANES), jnp.int32),
            jax.ShapeDtypeStruct((T, LANES), F32),
            jax.ShapeDtypeStruct((1, LANES), F32),
        ],
        scratch_shapes=[pltpu.VMEM((1, LANES), F32)],
        compiler_params=_cparams(("arbitrary",)),
        name="outproj_router",
    )(attn, lru, x2, ag, lru_out_g.reshape(1, lru_w), wa, wl, norm2_g.reshape(1, D),
      wrh, wrl, br, tri)


def _plan_kernel(cnt_ref, pstart_ref, plan_ref):
    cnt = cnt_ref[...]
    lane = lax.broadcasted_iota(jnp.int32, cnt.shape, 1)
    padded = jnp.floor((cnt + (ROW_BLOCK - 1)) * (1.0 / ROW_BLOCK)) * ROW_BLOCK
    pend = padded
    d = 1
    while d < N_EXPERTS:
        pend = pend + jnp.where(lane >= d, pltpu.roll(pend, d, 1), 0.0)
        d *= 2
    pstart_ref[...] = pend - padded
    total = jnp.max(pend, axis=1, keepdims=True)

    shape = plan_ref.shape
    lanes = lax.broadcasted_iota(jnp.int32, shape, 1)
    is_expert = lanes < N_EXPERTS
    start = lax.broadcasted_iota(jnp.int32, shape, 0).astype(F32) * ROW_BLOCK

    def groups_ending_by(row):
        return jnp.sum(jnp.where(jnp.logical_and(pend <= row, is_expert), 1.0, 0.0), axis=1, keepdims=True)

    block_e = jnp.minimum(groups_ending_by(start), N_EXPERTS - 1.0)
    tail = jnp.max(jnp.where(jnp.logical_and(jnp.logical_and(pend == start + ROW_BLOCK, padded > 0.0),
                                             is_expert), 1.0, 0.0), axis=1, keepdims=True)
    fill = jnp.maximum(tail, jnp.where(start[:, 0:1] >= total, 1.0, 0.0))
    group_end = jnp.sum(jnp.where(lanes.astype(F32) == block_e, pend, 0.0), axis=1, keepdims=True)
    next_e = jnp.where(group_end < total,
                       jnp.minimum(groups_ending_by(group_end), N_EXPERTS - 1.0), -1.0)
    plan = jnp.where(lanes == 0, block_e,
                     jnp.where(lanes == 1, fill,
                               jnp.where(lanes == 2, next_e, total * (1.0 / ROW_BLOCK))))
    plan_ref[...] = plan.astype(jnp.int32)


def _routing_plan(cnt, n_blocks):
    assert ROW_BLOCK & (ROW_BLOCK - 1) == 0, "exact f32 division by the row block size"
    rows = -(-n_blocks // SUBLANES) * SUBLANES
    pstart, plan = pl.pallas_call(
        _plan_kernel,
        out_shape=[jax.ShapeDtypeStruct((1, LANES), F32),
                   jax.ShapeDtypeStruct((rows, LANES), jnp.int32)],
        name="routing_plan",
    )(cnt)
    return pstart, plan[:n_blocks, 0], plan[:n_blocks, 1], plan[:n_blocks, 2], plan[0:1, 3]


def _dest_kernel(route_ref, pstart_ref, dest_ref):
    route = route_ref[...]
    lane = lax.broadcasted_iota(jnp.int32, route.shape, 1)
    pstart = pstart_ref[...]
    dest = jnp.zeros(route.shape, jnp.int32)
    for k in range(TOP_K):
        start = jnp.sum(jnp.where(lane == route[:, k:k + 1], pstart, 0.0), axis=1, keepdims=True)
        dest = jnp.where(lane == k, start.astype(jnp.int32) + route[:, TOP_K + k:TOP_K + k + 1], dest)
    dest_ref[...] = dest


def _dest_rows(route, pstart):
    T = route.shape[0]
    ts = math.gcd(TS_DEST, T)
    dest = pl.pallas_call(
        _dest_kernel,
        grid=(T // ts,),
        in_specs=[pl.BlockSpec((ts, LANES), lambda i: (i, 0)),
                  pl.BlockSpec((1, LANES), lambda i: (0, 0))],
        out_specs=pl.BlockSpec((ts, LANES), lambda i: (i, 0)),
        out_shape=jax.ShapeDtypeStruct((T, LANES), jnp.int32),
        compiler_params=_cparams(("parallel",)),
        name="dest_rows",
    )(route, pstart)
    return dest[:, :TOP_K].reshape(T * TOP_K)


def _dispatch_kernel(fill_ref, dest_ref, x_ref, out_hbm, zero_ref, sem, zero_sem, *, ts, n_blocks):
    block_slabs = ROW_BLOCK * SUBLANES

    def fill_copy(b):
        off = pl.multiple_of(b * block_slabs, block_slabs)
        return pltpu.make_async_copy(zero_ref, out_hbm.at[pl.ds(off, block_slabs)], zero_sem)

    @pl.when(pl.program_id(0) == 0)
    def _():
        zero_ref[...] = jnp.zeros(zero_ref.shape, F32)

        def start(b, carry):
            @pl.when(fill_ref[b] != 0)
            def _():
                fill_copy(b).start()
            return carry

        def wait(b, carry):
            @pl.when(fill_ref[b] != 0)
            def _():
                fill_copy(b).wait()
            return carry

        lax.fori_loop(0, n_blocks, start, 0)
        lax.fori_loop(0, n_blocks, wait, 0)

    def issue(i, carry):
        for j in range(ISSUE_UNROLL):
            r = i * ISSUE_UNROLL + j
            for k in range(TOP_K):
                d = dest_ref[r * TOP_K + k]
                pltpu.make_async_copy(_slab(x_ref, r), _slab(out_hbm, d), sem).start(priority=k % 2)
        return carry

    lax.fori_loop(0, ts // ISSUE_UNROLL, issue, 0)
    for k in range(TOP_K):
        pltpu.make_async_copy(x_ref, out_hbm.at[pl.ds(0, ts * SUBLANES)], sem).wait()


def _dispatch(xn_slabs, fill, dest_flat, n_rows):
    T = xn_slabs.shape[0] // SUBLANES
    ts = min(TS_DISP, T)
    grid_spec = pltpu.PrefetchScalarGridSpec(
        num_scalar_prefetch=1,
        grid=(T // ts,),
        in_specs=[
            pl.BlockSpec((ts * TOP_K,), lambda i, fl: (i,), memory_space=pltpu.SMEM),
            pl.BlockSpec((ts * SUBLANES, LANES), lambda i, fl: (i, 0)),
        ],
        out_specs=pl.BlockSpec(memory_space=pl.ANY),
        scratch_shapes=[pltpu.VMEM((ROW_BLOCK * SUBLANES, LANES), F32),
                        pltpu.SemaphoreType.DMA, pltpu.SemaphoreType.DMA],
    )
    return pl.pallas_call(
        functools.partial(_dispatch_kernel, ts=ts, n_blocks=n_rows // ROW_BLOCK),
        grid_spec=grid_spec,
        out_shape=jax.ShapeDtypeStruct((n_rows * SUBLANES, LANES), xn_slabs.dtype),
        compiler_params=_cparams(("arbitrary",)),
        name="dispatch",
    )(fill, dest_flat, xn_slabs)


def _expert_kernel(be_ref, na_ref, nxt_ref, x_ref, wg_hbm, bg_ref, wu_hbm, bu_ref, wd_hbm, bd_ref,
                   y_ref, stage_ref, wb_ref, slot_ref, sems):
    i = pl.program_id(0)
    e = be_ref[i]
    w_hbm = (wg_hbm, wu_hbm, wd_hbm)

    def fetch(expert, slot, m):
        return pltpu.make_async_copy(w_hbm[m].at[expert], stage_ref.at[slot, m], sems.at[slot, m])

    @pl.when(i == 0)
    def _():
        slot_ref[0] = 0
        for m in range(3):
            fetch(e, 0, m).start()

    active = i < na_ref[0]
    first = jnp.logical_or(i == 0, e != be_ref[jnp.maximum(i - 1, 0)])

    @pl.when(jnp.logical_and(active, first))
    def _():
        slot = slot_ref[0]
        for m in range(3):
            fetch(e, slot, m).wait()
            wb_ref[m] = stage_ref[slot, m].astype(BF16)

        @pl.when(nxt_ref[i] >= 0)
        def _():
            for m in range(3):
                fetch(nxt_ref[i], 1 - slot, m).start()

        slot_ref[0] = 1 - slot

    @pl.when(active)
    def _():
        x = _slabs_to_rows(x_ref, ROW_BLOCK).astype(BF16)
        g = jnp.dot(x, wb_ref[0], preferred_element_type=F32) + bg_ref[0]
        u = jnp.dot(x, wb_ref[1], preferred_element_type=F32) + bu_ref[0]
        g = jnp.minimum(g, SWIGLU_LIMIT)
        u = jnp.clip(u, -SWIGLU_LIMIT, SWIGLU_LIMIT)
        glu = g * jax.nn.sigmoid(SWIGLU_ALPHA * g)
        y = jnp.dot(((u + 1.0) * glu).astype(BF16), wb_ref[2], preferred_element_type=F32) + bd_ref[0]
        _rows_to_slabs(y_ref, y)


def _experts(x_rows, block_e, n_active, next_e, w_gate, b_gate, w_up, b_up, w_down, b_down):
    E, D, FF = w_gate.shape
    assert D == FF, "the three expert matrices share one staging shape"
    block_slabs = ROW_BLOCK * SUBLANES
    n_blocks = x_rows.shape[0] // block_slabs

    def row_map(i, be, na, nx):
        return (jnp.minimum(i, na[0] - 1), 0)

    def b_map(i, be, na, nx):
        return (be[jnp.minimum(i, na[0] - 1)], 0, 0)

    grid_spec = pltpu.PrefetchScalarGridSpec(
        num_scalar_prefetch=3,
        grid=(n_blocks,),
        in_specs=[
            pl.BlockSpec((block_slabs, LANES), row_map),
            pl.BlockSpec(memory_space=pl.ANY),
            pl.BlockSpec((1, 1, FF), b_map),
            pl.BlockSpec(memory_space=pl.ANY),
            pl.BlockSpec((1, 1, FF), b_map),
            pl.BlockSpec(memory_space=pl.ANY),
            pl.BlockSpec((1, 1, D), b_map),
        ],
        out_specs=pl.BlockSpec((block_slabs, LANES), row_map),
        scratch_shapes=[
            pltpu.VMEM((2, 3, D, FF), F32),
            pltpu.VMEM((3, D, FF), BF16),
            pltpu.SMEM((1,), jnp.int32),
            pltpu.SemaphoreType.DMA((2, 3)),
        ],
    )
    return pl.pallas_call(
        _expert_kernel,
        grid_spec=grid_spec,
        out_shape=jax.ShapeDtypeStruct(x_rows.shape, F32),
        input_output_aliases={3: 0},
        compiler_params=pltpu.CompilerParams(dimension_semantics=("arbitrary",),
                                             vmem_limit_bytes=EXPERT_VMEM_LIMIT),
        name="experts",
    )(block_e, n_active, next_e, x_rows, w_gate, b_gate.reshape(E, 1, FF), w_up,
      b_up.reshape(E, 1, FF), w_down, b_down.reshape(E, 1, D))


def _combine_kernel(dest_ref, dest_next_ref, y_hbm, x1_ref, gates_ref, fg_ref, o_ref, bufs, sems,
                    *, ts, n_steps):
    i = pl.program_id(0)
    slot = i % 2

    def gather_tile(d_ref, s):
        def issue(it, carry):
            for j in range(ISSUE_UNROLL):
                r = it * ISSUE_UNROLL + j
                for k in range(TOP_K):
                    d = d_ref[r * TOP_K + k]
                    pltpu.make_async_copy(_slab(y_hbm, d), _slab(bufs.at[s, k], r),
                                          sems.at[s]).start(priority=k % 2)
            return carry

        lax.fori_loop(0, ts // ISSUE_UNROLL, issue, 0)

    @pl.when(i == 0)
    def _():
        gather_tile(dest_ref, 0)

    @pl.when(i + 1 < n_steps)
    def _():
        gather_tile(dest_next_ref, 1 - slot)

    for k in range(TOP_K):
        pltpu.make_async_copy(y_hbm.at[pl.ds(0, ts * SUBLANES)], bufs.at[slot, k], sems.at[slot]).wait()

    acc = x1_ref[...]
    gates = gates_ref[...]
    for k in range(TOP_K):
        acc = acc + _slabs_to_rows(bufs.at[slot, k], ts) * gates[:, k:k + 1]
    ms = jnp.mean(acc * acc, axis=-1, keepdims=True)
    o_ref[...] = acc * lax.rsqrt(ms + NORM_EPS) * fg_ref[...]


def _combine(y_rows, dest_flat, x1, gates, final_g):
    T, D = x1.shape
    ts = min(TS_COMB, T)
    n_steps = T // ts
    tok = lambda i: (i, 0)
    return pl.pallas_call(
        functools.partial(_combine_kernel, ts=ts, n_steps=n_steps),
        grid=(n_steps,),
        in_specs=[
            pl.BlockSpec((ts * TOP_K,), lambda i: (i,), memory_space=pltpu.SMEM),
            pl.BlockSpec((ts * TOP_K,), lambda i: (jnp.minimum(i + 1, n_steps - 1),),
                         memory_space=pltpu.SMEM),
            pl.BlockSpec(memory_space=pl.ANY),
            pl.BlockSpec((ts, D), tok),
            pl.BlockSpec((ts, LANES), tok),
            pl.BlockSpec((1, D), lambda i: (0, 0)),
        ],
        out_specs=pl.BlockSpec((ts, D), tok),
        out_shape=jax.ShapeDtypeStruct((T, D), F32),
        scratch_shapes=[pltpu.VMEM((2, TOP_K, ts * SUBLANES, LANES), F32),
                        pltpu.SemaphoreType.DMA((2,))],
        compiler_params=_cparams(("arbitrary",)),
        name="combine",
    )(dest_flat, dest_flat, y_rows, x1, gates, final_g.reshape(1, D))


def kernel(x, norm1_g, w_in, q_norm_g, k_norm_g, conv_w, conv_b, lru_wa, lru_ba, lru_wi, lru_bi,
           lru_lam, attn_out_g, lru_out_g, w_out, norm2_g, w_router, b_router, w_gate, b_gate,
           w_up, b_up, w_down, b_down, final_g):
    B, S, D = x.shape
    T = B * S
    assert w_in.shape[0] == 1, "single-layer trunk: the final norm is fused into the layer's combine"
    x2 = x.reshape(T, D)
    for l in range(1):
        qt, k, vt, lru_x, lru_gate = _inproj(x2, norm1_g[l], w_in[l], q_norm_g[l], k_norm_g[l], S)
        score_bound = (HEAD_DIM * Q_SCALE * jnp.max(jnp.abs(q_norm_g[l]))
                       * jnp.max(jnp.abs(k_norm_g[l])))
        lru_ops = _lru_operands(conv_w[l], conv_b[l], lru_wa[l], lru_ba[l], lru_wi[l], lru_bi[l],
                                lru_lam[l])
        attn, lru = _mixers(qt, k.reshape(B, S, -1), vt, lru_x.reshape(B, S, -1),
                            lru_gate.reshape(B, S, -1), lru_ops, score_bound, B, S)
        x1, xn3, route, gates, cnt = _outproj_router(
            attn.reshape(T, -1), lru.reshape(T, -1), x2, attn_out_g[l], lru_out_g[l], w_out[l],
            norm2_g[l], w_router[l], b_router[l])

        n_rows = T * TOP_K + N_EXPERTS * ROW_BLOCK
        pstart, block_e, fill, next_e, n_active = _routing_plan(cnt, n_rows // ROW_BLOCK)
        dest_flat = _dest_rows(route, pstart)
        x_rows = _dispatch(xn3, fill, dest_flat, n_rows)
        y_rows = _experts(x_rows, block_e, n_active, next_e, w_gate[l], b_gate[l], w_up[l], b_up[l],
                          w_down[l], b_down[l])
        x2 = _combine(y_rows, dest_flat, x1, gates, final_g)
    return x2.reshape(B, S, D)
```

```python
import functools
import math

import jax
import jax.numpy as jnp
import numpy as np
from jax import lax
from jax.experimental import pallas as pl
from jax.experimental.pallas import tpu as pltpu

F32 = jnp.float32
BF16 = jnp.bfloat16

GRID_W = 64
HEAD_DIM = 64
N_Q_HEADS = 8
N_KV_HEADS = 2
GQA_GROUP = N_Q_HEADS // N_KV_HEADS
ATTN_W = N_Q_HEADS * HEAD_DIM
KV_W = N_KV_HEADS * HEAD_DIM
LRU_BLOCKS = 8
LRU_C = 8.0
CONV_W = 4
CONV_PAD_L = 2
ROPE_THETA = 10000.0
ROPE_HALF = HEAD_DIM // 2
ROPE_M = ROPE_HALF // 2
N_EXPERTS = 32
TOP_K = 4
SWIGLU_ALPHA = 1.702
SWIGLU_LIMIT = 7.0
NORM_EPS = 1e-5
QK_EPS = 1e-6
LOG2_E = 1.4426950408889634
Q_SCALE = HEAD_DIM ** -0.5 * LOG2_E
SAFE_SCORE_LOG2 = 96.0

LANES = 128
SUBLANES = 8
BF16_SUBLANES = 16
PV_ROWS = HEAD_DIM + BF16_SUBLANES
VMEM_LIMIT = 48 * 1024 * 1024
EXPERT_VMEM_LIMIT = 56 * 1024 * 1024

TS_IN = 512
TQ = 256
TK = 256
KV_UNROLL = 8
HEADS_PER_STEP = 2
TC_LRU = 512
TS_OUT = 512
TS_DEST = 2048
ROW_BLOCK = 512
TS_DISP = 512
TS_COMB = 256
ISSUE_UNROLL = 8


def _cparams(sem):
    return pltpu.CompilerParams(dimension_semantics=sem, vmem_limit_bytes=VMEM_LIMIT)


def _inproj_kernel(x_ref, g1_ref, wt_ref, w_ref, qg_ref, kg_ref, cos_ref, sin_ref, cost_ref, sint_ref,
                   q_ref, k_ref, v_ref, lx_ref, lg_ref, *, lru_w):
    x = x_ref[...]
    ms = jnp.mean(x * x, axis=-1, keepdims=True)
    xn = (x * lax.rsqrt(ms + NORM_EPS) * g1_ref[...]).astype(BF16)
    ht = lax.dot_general(wt_ref[...], xn, (((1,), (1,)), ((), ())), preferred_element_type=F32)
    h = jnp.dot(xn, w_ref[...], preferred_element_type=F32)

    qw = N_Q_HEADS * LANES
    kw = N_KV_HEADS * LANES
    cost = cost_ref[...]
    sint = sint_ref[...]
    row = lax.broadcasted_iota(jnp.int32, cost.shape, 0)
    first_half_t = (row % ROPE_HALF) < ROPE_M
    qg = qg_ref[...]
    for c in range(N_Q_HEADS):
        sl = slice(c * LANES, (c + 1) * LANES)
        xc = ht[sl]
        hms = jnp.sum(xc * xc, axis=0, keepdims=True) * (1.0 / HEAD_DIM)
        xc = xc * lax.rsqrt(hms + QK_EPS) * qg
        partner = jnp.where(first_half_t, pltpu.roll(xc, LANES - ROPE_M, 0), pltpu.roll(xc, ROPE_M, 0))
        q_ref[0, sl, :] = ((xc * cost + partner * sint) * Q_SCALE).astype(BF16)
    for c in range(N_KV_HEADS):
        sl = slice(c * LANES, (c + 1) * LANES)
        v_ref[0, sl, :] = jnp.where(row >= HEAD_DIM, 1.0, ht[qw + c * LANES: qw + (c + 1) * LANES]).astype(BF16)

    cos = cos_ref[...]
    sin = sin_ref[...]
    lane = lax.broadcasted_iota(jnp.int32, cos.shape, 1)
    first_half = (lane % ROPE_HALF) < ROPE_M
    for c in range(N_KV_HEADS):
        sl = slice(c * LANES, (c + 1) * LANES)
        xc = h[:, sl]
        hms = jnp.sum(xc * xc, axis=-1, keepdims=True) * (1.0 / HEAD_DIM)
        xc = xc * lax.rsqrt(hms + QK_EPS) * kg_ref[...]
        partner = jnp.where(first_half, pltpu.roll(xc, LANES - ROPE_M, 1), pltpu.roll(xc, ROPE_M, 1))
        k_ref[:, sl] = (xc * cos + partner * sin).astype(BF16)
    lx_ref[...] = h[:, kw: kw + lru_w]
    lg_ref[...] = h[:, kw + lru_w: kw + 2 * lru_w]


def _pad_heads(w, n_heads):
    lead = w.shape[:-1]
    w = w.reshape(lead + (n_heads, HEAD_DIM))
    w = jnp.pad(w, [(0, 0)] * len(lead) + [(0, 0), (0, LANES - HEAD_DIM)])
    return w.reshape(lead + (n_heads * LANES,))


def _rope_tables(S):
    t = np.arange(S)
    rows = (t // GRID_W).astype(np.float32)
    cols = (t % GRID_W).astype(np.float32)
    inv_freq = (ROPE_THETA ** (-np.arange(ROPE_M, dtype=np.float32) / ROPE_M)).astype(np.float32)
    ar = rows[:, None] * inv_freq[None, :]
    ac = cols[:, None] * inv_freq[None, :]
    cos = np.concatenate([np.cos(ar), np.cos(ar), np.cos(ac), np.cos(ac)], axis=-1)
    sin = np.concatenate([-np.sin(ar), np.sin(ar), -np.sin(ac), np.sin(ac)], axis=-1)
    pad = [(0, 0), (0, LANES - HEAD_DIM)]
    cos = np.pad(cos, pad).astype(np.float32)
    sin = np.pad(sin, pad).astype(np.float32)
    return cos, sin, np.ascontiguousarray(cos.T), np.ascontiguousarray(sin.T)


def _inproj(x2, norm1_g, w_in, q_norm_g, k_norm_g, S):
    T, D = x2.shape
    lru_w = (w_in.shape[1] - ATTN_W - 2 * KV_W) // 2
    o0, o1, o2 = ATTN_W, ATTN_W + KV_W, ATTN_W + 2 * KV_W
    w_t = jnp.concatenate([_pad_heads(w_in[:, :o0], N_Q_HEADS),
                           _pad_heads(w_in[:, o1:o2], N_KV_HEADS)], axis=1).T.astype(BF16)
    w_rest = jnp.concatenate([_pad_heads(w_in[:, o0:o1], N_KV_HEADS), w_in[:, o2:]],
                             axis=1).astype(BF16)
    qg = _pad_heads(q_norm_g.reshape(1, HEAD_DIM), 1).reshape(LANES, 1)
    kg = _pad_heads(k_norm_g.reshape(1, HEAD_DIM), 1)
    cos, sin, cos_t, sin_t = _rope_tables(S)
    ts = TS_IN
    n_s = S // ts
    qw, kw = N_Q_HEADS * LANES, N_KV_HEADS * LANES
    const = lambda i: (0, 0)
    tok = lambda i: (i, 0)
    pos = lambda i: (i % n_s, 0)
    pos_t = lambda i: (0, i % n_s)
    tposed = lambda i: (i // n_s, 0, i % n_s)
    return pl.pallas_call(
        functools.partial(_inproj_kernel, lru_w=lru_w),
        grid=(T // ts,),
        in_specs=[
            pl.BlockSpec((ts, D), tok),
            pl.BlockSpec((1, D), const),
            pl.BlockSpec(w_t.shape, const),
            pl.BlockSpec(w_rest.shape, const),
            pl.BlockSpec((LANES, 1), const),
            pl.BlockSpec((1, LANES), const),
            pl.BlockSpec((ts, LANES), pos),
            pl.BlockSpec((ts, LANES), pos),
            pl.BlockSpec((LANES, ts), pos_t),
            pl.BlockSpec((LANES, ts), pos_t),
        ],
        out_specs=[
            pl.BlockSpec((1, qw, ts), tposed),
            pl.BlockSpec((ts, kw), tok),
            pl.BlockSpec((1, kw, ts), tposed),
            pl.BlockSpec((ts, lru_w), tok),
            pl.BlockSpec((ts, lru_w), tok),
        ],
        out_shape=[
            jax.ShapeDtypeStruct((T // S, qw, S), BF16),
            jax.ShapeDtypeStruct((T, kw), BF16),
            jax.ShapeDtypeStruct((T // S, kw, S), BF16),
            jax.ShapeDtypeStruct((T, lru_w), F32),
            jax.ShapeDtypeStruct((T, lru_w), F32),
        ],
        compiler_params=_cparams(("parallel",)),
        name="inproj",
    )(x2, norm1_g.reshape(1, D), w_t, w_rest, qg, kg, cos, sin, cos_t, sin_t)


def _attn_kernel(qt_ref, k_ref, vt_ref, o_ref, acc_ref, s_ref, p_ref, *, tq, tk, n_kv, kv_unroll):
    hp = HEADS_PER_STEP
    spt = GQA_GROUP // hp
    acc_ref[...] = jnp.zeros(acc_ref.shape, F32)

    def scores(j, sp):
        kt = k_ref[0, pl.ds(pl.multiple_of(j * tk, tk), tk), :]
        out = []
        for u in range(hp):
            g = sp * hp + u
            s = jnp.dot(kt, qt_ref[0, g * LANES:(g + 1) * LANES, :], preferred_element_type=F32)
            out.append((s, jnp.max(s, axis=0, keepdims=True)))
        return out

    def softmax_stage(sc, ms, sp):
        out = []
        for u, (s, s_max) in enumerate(sc):
            h = sp * hp + u
            m_new = jnp.maximum(ms[h], s_max)
            out.append((jnp.exp2(ms[h] - m_new), jnp.exp2(s - m_new).astype(BF16)))
            ms[h] = m_new
        return out

    def pv_stage(j, sp, ap):
        vt = vt_ref[0, 0:PV_ROWS, pl.ds(pl.multiple_of(j * tk, tk), tk)]
        for u, (alpha, p) in enumerate(ap):
            g = sp * hp + u
            acc_ref[g] = alpha * acc_ref[g] + jnp.dot(vt, p, preferred_element_type=F32)

    ms = [jnp.full((1, tq), -jnp.inf, F32)] * GQA_GROUP
    ap = softmax_stage(scores(0, 0), ms, 0)
    sc = scores(min(1 // spt, n_kv - 1), 1 % spt)
    for u in range(hp):
        s_ref[u] = sc[u][0]
        p_ref[u] = ap[u][1]

    def body(it, carry):
        ms = list(carry[:GQA_GROUP])
        ap = [(carry[GQA_GROUP + u], p_ref[u]) for u in range(hp)]
        sc = [(s_ref[u], carry[GQA_GROUP + hp + u]) for u in range(hp)]
        for n in range(kv_unroll * spt):
            j = it * kv_unroll + n // spt
            j_next = jnp.minimum(it * kv_unroll + (n + 2) // spt, n_kv - 1)
            sc_next = scores(j_next, (n + 2) % spt)
            ap_next = softmax_stage(sc, ms, (n + 1) % spt)
            pv_stage(j, n % spt, ap)
            sc, ap = sc_next, ap_next
        for u in range(hp):
            s_ref[u] = sc[u][0]
            p_ref[u] = ap[u][1]
        return tuple(ms) + tuple(a for a, _ in ap) + tuple(m for _, m in sc)

    lax.fori_loop(0, n_kv // kv_unroll, body,
                  tuple(ms) + tuple(a for a, _ in ap) + tuple(m for _, m in sc))
    _attn_finalize(acc_ref, o_ref, tq)


def _attn_finalize(acc_ref, o_ref, tq):
    pad = jnp.zeros((LANES - HEAD_DIM, tq), F32)
    for g in range(GQA_GROUP):
        acc = acc_ref[g]
        o = acc[0:HEAD_DIM] / acc[HEAD_DIM:HEAD_DIM + 1, :]
        o_ref[0, :, g * LANES:(g + 1) * LANES] = jnp.concatenate([o, pad], axis=0).T.astype(BF16)


def _attn_bounded_kernel(qt_ref, k_ref, vt_ref, o_ref, acc_ref, s_ref, p_ref, *, tq, tk, n_kv, kv_unroll):
    hp = HEADS_PER_STEP
    spt = GQA_GROUP // hp
    acc_ref[...] = jnp.zeros(acc_ref.shape, F32)

    def scores(j, sp):
        kt = k_ref[0, pl.ds(pl.multiple_of(j * tk, tk), tk), :]
        return [jnp.dot(kt, qt_ref[0, (sp * hp + u) * LANES:(sp * hp + u + 1) * LANES, :],
                        preferred_element_type=F32) for u in range(hp)]

    def probs(sc):
        return [jnp.exp2(s).astype(BF16) for s in sc]

    def pv_stage(j, sp, ps):
        vt = vt_ref[0, 0:PV_ROWS, pl.ds(pl.multiple_of(j * tk, tk), tk)]
        for u, p in enumerate(ps):
            acc_ref[sp * hp + u] += jnp.dot(vt, p, preferred_element_type=F32)

    ps = probs(scores(0, 0))
    sc = scores(min(1 // spt, n_kv - 1), 1 % spt)
    for u in range(hp):
        s_ref[u] = sc[u]
        p_ref[u] = ps[u]

    def body(it, carry):
        ps = [p_ref[u] for u in range(hp)]
        sc = [s_ref[u] for u in range(hp)]
        for n in range(kv_unroll * spt):
            j = it * kv_unroll + n // spt
            j_next = jnp.minimum(it * kv_unroll + (n + 2) // spt, n_kv - 1)
            sc_next = scores(j_next, (n + 2) % spt)
            ps_next = probs(sc)
            pv_stage(j, n % spt, ps)
            sc, ps = sc_next, ps_next
        for u in range(hp):
            s_ref[u] = sc[u]
            p_ref[u] = ps[u]
        return carry

    lax.fori_loop(0, n_kv // kv_unroll, body, 0)
    _attn_finalize(acc_ref, o_ref, tq)


def _attention(qt, k, vt, *, kernel, B, S):
    tq = min(TQ, S)
    tk = min(TK, S)
    gw = GQA_GROUP * LANES
    return pl.pallas_call(
        functools.partial(kernel, tq=tq, tk=tk, n_kv=S // tk,
                          kv_unroll=math.gcd(S // tk, KV_UNROLL)),
        grid=(B, N_KV_HEADS, S // tq),
        in_specs=[
            pl.BlockSpec((1, gw, tq), lambda b, h, i: (b, h, i)),
            pl.BlockSpec((1, S, LANES), lambda b, h, i: (b, 0, h)),
            pl.BlockSpec((1, LANES, S), lambda b, h, i: (b, h, 0)),
        ],
        out_specs=pl.BlockSpec((1, tq, gw), lambda b, h, i: (b, i, h)),
        out_shape=jax.ShapeDtypeStruct((B, S, N_Q_HEADS * LANES), BF16),
        scratch_shapes=[pltpu.VMEM((GQA_GROUP, PV_ROWS, tq), F32),
                        pltpu.VMEM((HEADS_PER_STEP, tk, tq), F32),
                        pltpu.VMEM((HEADS_PER_STEP, tk, tq), BF16)],
        compiler_params=_cparams(("parallel", "parallel", "parallel")),
        name=kernel.__name__.strip("_"),
    )(qt, k, vt)


def _scan_chunk(a, b, h_in, reverse):
    n = a.shape[0]
    n_groups = n // SUBLANES
    a = a.reshape(n_groups, SUBLANES, LANES)
    b = b.reshape(n_groups, SUBLANES, LANES)
    sub = lax.broadcasted_iota(jnp.int32, a.shape, 1)
    d = 1
    while d < SUBLANES:
        if reverse:
            keep = sub < SUBLANES - d
            shift = SUBLANES - d
        else:
            keep = sub >= d
            shift = d
        a_sh = jnp.where(keep, pltpu.roll(a, shift, 1), 1.0)
        b_sh = jnp.where(keep, pltpu.roll(b, shift, 1), 0.0)
        b = a * b_sh + b
        a = a * a_sh
        d *= 2
    a = a.reshape(n, LANES)
    b = b.reshape(n, LANES)
    order = range(n_groups - 1, -1, -1) if reverse else range(n_groups)
    edge = h_in
    out = [None] * n_groups
    for v in order:
        rows = slice(v * SUBLANES, (v + 1) * SUBLANES)
        hv = b[rows] + a[rows] * jnp.broadcast_to(edge, (SUBLANES, LANES))
        out[v] = hv
        edge = hv[0:1] if reverse else hv[SUBLANES - 1:SUBLANES]
    return jnp.concatenate(out, axis=0), edge


def _lru_pad_input(u_ref, up_ref, S):
    zeros = jnp.zeros((SUBLANES, LANES), F32)
    up_ref[0:SUBLANES, :] = zeros
    up_ref[S + SUBLANES:S + 2 * SUBLANES, :] = zeros
    up_ref[SUBLANES:S + SUBLANES, :] = u_ref[0]


def _lru_gates(up_ref, cw_ref, cb_ref, w_ref, bias_ref, lam_ref, t0, tc, d):
    cw = cw_ref[...]
    xc = cb_ref[...]
    for j in range(CONV_W):
        xc = xc + up_ref[pl.ds(t0 + SUBLANES + j - CONV_PAD_L, tc), :] * cw[j:j + 1, :]
    gw = 2 * LANES
    g = jnp.dot(xc.astype(BF16), w_ref[0, :, d * gw:(d + 1) * gw],
                preferred_element_type=F32) + bias_ref[0, :, d * gw:(d + 1) * gw]
    r = jax.nn.sigmoid(g[:, :LANES])
    i = jax.nn.sigmoid(g[:, LANES:])
    a = jnp.exp(-LRU_C * r * jax.nn.softplus(-lam_ref[d:d + 1, :]))
    y = 1.0 - a * a
    b = jnp.where(y > 0.0, y * lax.rsqrt(y), 0.0) * i * xc
    return a, b


def _lru_kernel(u_ref, gate_ref, cw_ref, cb_ref, w_ref, bias_ref, lam_ref, o_ref,
                up_ref, hf_ref, *, S, tc):
    _lru_pad_input(u_ref, up_ref, S)
    n_chunks = S // tc
    params = (up_ref, cw_ref, cb_ref, w_ref, bias_ref, lam_ref)

    def fwd(c, h):
        t0 = pl.multiple_of(c * tc, tc)
        hc, h_last = _scan_chunk(*_lru_gates(*params, t0, tc, 0), h, False)
        hf_ref[pl.ds(t0, tc), :] = hc
        return h_last

    lax.fori_loop(0, n_chunks, fwd, jnp.zeros((1, LANES), F32))

    def bwd(ci, h):
        t0 = pl.multiple_of((n_chunks - 1 - ci) * tc, tc)
        hc, h_last = _scan_chunk(*_lru_gates(*params, t0, tc, 1), h, True)
        gate = gate_ref[0, pl.ds(t0, tc), :]
        o_ref[0, pl.ds(t0, tc), :] = (hf_ref[pl.ds(t0, tc), :] + hc) * jax.nn.gelu(gate)
        return h_last

    lax.fori_loop(0, n_chunks, bwd, jnp.zeros((1, LANES), F32))


def _block_diag_pairs(w):
    nb, bw, _ = w.shape
    w = w.reshape(nb // 2, 2, bw, bw)
    z = jnp.zeros_like(w[:, 0])
    top = jnp.concatenate([w[:, 0], z], axis=-1)
    bot = jnp.concatenate([z, w[:, 1]], axis=-1)
    return jnp.concatenate([top, bot], axis=-2)


def _lru_operands(conv_w, conv_b, wa, ba, wi, bi, lam):
    C = conv_b.shape[0]
    nc = C // LANES
    w = jnp.concatenate([_block_diag_pairs(wa[0]), _block_diag_pairs(wi[0]),
                         _block_diag_pairs(wa[1]), _block_diag_pairs(wi[1])], axis=-1).astype(BF16)
    bias = jnp.stack([ba[0].reshape(nc, LANES), bi[0].reshape(nc, LANES),
                      ba[1].reshape(nc, LANES), bi[1].reshape(nc, LANES)], axis=1)
    return conv_w, conv_b.reshape(1, C), w, bias.reshape(nc, 1, 4 * LANES), lam


def _lru_specs(S, unit):
    seq = lambda *g: (unit(*g)[0], 0, unit(*g)[1])
    chan = lambda *g: (0, unit(*g)[1])
    blk = lambda *g: (unit(*g)[1], 0, 0)
    in_specs = [
        pl.BlockSpec((1, S, LANES), seq),
        pl.BlockSpec((1, S, LANES), seq),
        pl.BlockSpec((CONV_W, LANES), chan),
        pl.BlockSpec((1, LANES), chan),
        pl.BlockSpec((1, LANES, 4 * LANES), blk),
        pl.BlockSpec((1, 1, 4 * LANES), blk),
        pl.BlockSpec((2, LANES), chan),
    ]
    return in_specs, pl.BlockSpec((1, S, LANES), seq)


def _lru(lru_x, lru_gate, lru_ops, B, S):
    C = lru_x.shape[-1]
    tc = min(TC_LRU, S)
    in_specs, out_spec = _lru_specs(S, lambda b, c: (b, c))
    return pl.pallas_call(
        functools.partial(_lru_kernel, S=S, tc=tc),
        grid=(B, C // LANES),
        in_specs=in_specs,
        out_specs=out_spec,
        out_shape=jax.ShapeDtypeStruct((B, S, C), F32),
        scratch_shapes=[
            pltpu.VMEM((S + 2 * SUBLANES, LANES), F32),
            pltpu.VMEM((S, LANES), F32),
        ],
        compiler_params=_cparams(("parallel", "parallel")),
        name="rglru",
    )(lru_x, lru_gate, *lru_ops)


def _mixers(qt, k, vt, lru_x, lru_gate, lru_ops, score_bound, B, S):
    attn = lax.cond(score_bound <= SAFE_SCORE_LOG2,
                    functools.partial(_attention, kernel=_attn_bounded_kernel, B=B, S=S),
                    functools.partial(_attention, kernel=_attn_kernel, B=B, S=S), qt, k, vt)
    return attn, _lru(lru_x, lru_gate, lru_ops, B, S)


def _rows_to_slabs(ref, x):
    n = x.shape[0]
    for s in range(SUBLANES):
        ref[pl.ds(s, n, stride=SUBLANES), :] = x[:, s * LANES:(s + 1) * LANES]


def _slabs_to_rows(ref, n):
    return jnp.concatenate([ref[pl.ds(s, n, stride=SUBLANES), :] for s in range(SUBLANES)], axis=1)


def _slab(ref, r):
    return ref.at[pl.ds(pl.multiple_of(r * SUBLANES, SUBLANES), SUBLANES)]


def _outproj_kernel(a_ref, l_ref, x_ref, ag_ref, lg_ref, wa_ref, wl_ref, g2_ref,
                    wrh_ref, wrl_ref, br_ref, tri_ref,
                    x1_ref, xn3_ref, route_ref, gates_ref, cnt_ref, carry_ref, *, attn_w, lru_w):
    step = pl.program_id(0)

    @pl.when(step == 0)
    def _():
        carry_ref[...] = jnp.zeros_like(carry_ref)

    a = a_ref[...].astype(F32)
    ams = jnp.sum(a * a, axis=-1, keepdims=True) * (1.0 / attn_w)
    an = a * lax.rsqrt(ams + NORM_EPS) * ag_ref[...]
    l = l_ref[...]
    lms = jnp.sum(l * l, axis=-1, keepdims=True) * (1.0 / lru_w)
    ln = l * lax.rsqrt(lms + NORM_EPS) * lg_ref[...]
    mix = (jnp.dot(an.astype(BF16), wa_ref[...], preferred_element_type=F32)
           + jnp.dot(ln.astype(BF16), wl_ref[...], preferred_element_type=F32))
    x1 = x_ref[...] + mix
    x1_ref[...] = x1
    ms = jnp.mean(x1 * x1, axis=-1, keepdims=True)
    xn = x1 * lax.rsqrt(ms + NORM_EPS) * g2_ref[...]
    _rows_to_slabs(xn3_ref, xn)

    hi = xn.astype(BF16)
    lo = (xn - hi.astype(F32)).astype(BF16)
    logits = (jnp.dot(hi, wrh_ref[...], preferred_element_type=F32)
              + jnp.dot(lo, wrh_ref[...], preferred_element_type=F32)
              + jnp.dot(hi, wrl_ref[...], preferred_element_type=F32)) + br_ref[...]
    lane = lax.broadcasted_iota(jnp.int32, logits.shape, 1)
    neg = -jnp.inf
    work = jnp.where(lane < N_EXPERTS, logits, neg)
    sel = jnp.zeros(logits.shape, F32)
    idxs, vals = [], []
    for _ in range(TOP_K):
        m = jnp.max(work, axis=1, keepdims=True)
        idx = jnp.min(jnp.where(work == m, lane, LANES), axis=1, keepdims=True)
        hit = lane == idx
        work = jnp.where(hit, neg, work)
        sel = sel + hit.astype(F32)
        idxs.append(idx)
        vals.append(m)
    es = [jnp.exp(v - vals[0]) for v in vals]
    den = es[0] + es[1] + es[2] + es[3]

    prefix = jnp.dot(tri_ref[...], sel.astype(BF16), preferred_element_type=F32) + carry_ref[...]
    carry_ref[...] = carry_ref[...] + jnp.sum(sel, axis=0, keepdims=True)
    cnt_ref[...] = carry_ref[...]

    route = jnp.zeros(logits.shape, jnp.int32)
    gates = jnp.zeros(logits.shape, F32)
    for k in range(TOP_K):
        rank = jnp.sum(jnp.where(lane == idxs[k], prefix, 0.0), axis=1, keepdims=True).astype(jnp.int32)
        route = jnp.where(lane == k, idxs[k], route)
        route = jnp.where(lane == TOP_K + k, rank, route)
        gates = jnp.where(lane == k, es[k] / den, gates)
    route_ref[...] = route
    gates_ref[...] = gates


def _outproj_router(attn, lru, x2, attn_out_g, lru_out_g, w_out, norm2_g, w_router, b_router):
    T, D = x2.shape
    lru_w = lru.shape[-1]
    ts = min(TS_OUT, T)
    wa = w_out[:ATTN_W].reshape(N_Q_HEADS, HEAD_DIM, D)
    wa = jnp.pad(wa, ((0, 0), (0, LANES - HEAD_DIM), (0, 0))).reshape(N_Q_HEADS * LANES, D).astype(BF16)
    wl = w_out[ATTN_W:].astype(BF16)
    ag = _pad_heads(attn_out_g.reshape(1, ATTN_W), N_Q_HEADS)
    wr = jnp.pad(w_router, ((0, 0), (0, LANES - N_EXPERTS)))
    wrh = wr.astype(BF16)
    wrl = (wr - wrh.astype(F32)).astype(BF16)
    br = jnp.pad(b_router.reshape(1, N_EXPERTS), ((0, 0), (0, LANES - N_EXPERTS)))
    tri = (jnp.arange(ts)[:, None] > jnp.arange(ts)[None, :]).astype(BF16)
    const = lambda i: (0, 0)
    tok = lambda i: (i, 0)
    aw = N_Q_HEADS * LANES
    return pl.pallas_call(
        functools.partial(_outproj_kernel, attn_w=ATTN_W, lru_w=lru_w),
        grid=(T // ts,),
        in_specs=[
            pl.BlockSpec((ts, aw), tok),
            pl.BlockSpec((ts, lru_w), tok),
            pl.BlockSpec((ts, D), tok),
            pl.BlockSpec((1, aw), const),
            pl.BlockSpec((1, lru_w), const),
            pl.BlockSpec((aw, D), const),
            pl.BlockSpec((lru_w, D), const),
            pl.BlockSpec((1, D), const),
            pl.BlockSpec((D, LANES), const),
            pl.BlockSpec((D, LANES), const),
            pl.BlockSpec((1, LANES), const),
            pl.BlockSpec((ts, ts), const),
        ],
        out_specs=[
            pl.BlockSpec((ts, D), tok),
            pl.BlockSpec((ts * SUBLANES, LANES), tok),
            pl.BlockSpec((ts, LANES), tok),
            pl.BlockSpec((ts, LANES), tok),
            pl.BlockSpec((1, LANES), const),
        ],
        out_shape=[
            jax.ShapeDtypeStruct((T, D), F32),
            jax.ShapeDtypeStruct((T * SUBLANES, LANES), F32),
            jax.ShapeDtypeStruct((T, LANES), jnp.int32),
            jax.ShapeDtypeStruct((T, LANES), F32),
            jax.ShapeDtypeStruct((1, LANES), F32),
        ],
        scratch_shapes=[pltpu.VMEM((1, LANES), F32)],
        compiler_params=_cparams(("arbitrary",)),
        name="outproj_router",
    )(attn, lru, x2, ag, lru_out_g.reshape(1, lru_w), wa, wl, norm2_g.reshape(1, D),
      wrh, wrl, br, tri)


def _plan_kernel(cnt_ref, pstart_ref, plan_ref):
    cnt = cnt_ref[...]
    lane = lax.broadcasted_iota(jnp.int32, cnt.shape, 1)
    padded = jnp.floor((cnt + (ROW_BLOCK - 1)) * (1.0 / ROW_BLOCK)) * ROW_BLOCK
    pend = padded
    d = 1
    while d < N_EXPERTS:
        pend = pend + jnp.where(lane >= d, pltpu.roll(pend, d, 1), 0.0)
        d *= 2
    pstart_ref[...] = pend - padded
    total = jnp.max(pend, axis=1, keepdims=True)

    shape = plan_ref.shape
    lanes = lax.broadcasted_iota(jnp.int32, shape, 1)
    is_expert = lanes < N_EXPERTS
    start = lax.broadcasted_iota(jnp.int32, shape, 0).astype(F32) * ROW_BLOCK

    def groups_ending_by(row):
        return jnp.sum(jnp.where(jnp.logical_and(pend <= row, is_expert), 1.0, 0.0), axis=1, keepdims=True)

    block_e = jnp.minimum(groups_ending_by(start), N_EXPERTS - 1.0)
    tail = jnp.max(jnp.where(jnp.logical_and(jnp.logical_and(pend == start + ROW_BLOCK, padded > 0.0),
                                             is_expert), 1.0, 0.0), axis=1, keepdims=True)
    fill = jnp.maximum(tail, jnp.where(start[:, 0:1] >= total, 1.0, 0.0))
    group_end = jnp.sum(jnp.where(lanes.astype(F32) == block_e, pend, 0.0), axis=1, keepdims=True)
    next_e = jnp.where(group_end < total,
                       jnp.minimum(groups_ending_by(group_end), N_EXPERTS - 1.0), -1.0)
    plan = jnp.where(lanes == 0, block_e,
                     jnp.where(lanes == 1, fill,
                               jnp.where(lanes == 2, next_e, total * (1.0 / ROW_BLOCK))))
    plan_ref[...] = plan.astype(jnp.int32)


def _routing_plan(cnt, n_blocks):
    assert ROW_BLOCK & (ROW_BLOCK - 1) == 0, "exact f32 division by the row block size"
    rows = -(-n_blocks // SUBLANES) * SUBLANES
    pstart, plan = pl.pallas_call(
        _plan_kernel,
        out_shape=[jax.ShapeDtypeStruct((1, LANES), F32),
                   jax.ShapeDtypeStruct((rows, LANES), jnp.int32)],
        name="routing_plan",
    )(cnt)
    return pstart, plan[:n_blocks, 0], plan[:n_blocks, 1], plan[:n_blocks, 2], plan[0:1, 3]


def _dest_kernel(route_ref, pstart_ref, dest_ref):
    route = route_ref[...]
    lane = lax.broadcasted_iota(jnp.int32, route.shape, 1)
    pstart = pstart_ref[...]
    dest = jnp.zeros(route.shape, jnp.int32)
    for k in range(TOP_K):
        start = jnp.sum(jnp.where(lane == route[:, k:k + 1], pstart, 0.0), axis=1, keepdims=True)
        dest = jnp.where(lane == k, start.astype(jnp.int32) + route[:, TOP_K + k:TOP_K + k + 1], dest)
    dest_ref[...] = dest


def _dest_rows(route, pstart):
    T = route.shape[0]
    ts = math.gcd(TS_DEST, T)
    dest = pl.pallas_call(
        _dest_kernel,
        grid=(T // ts,),
        in_specs=[pl.BlockSpec((ts, LANES), lambda i: (i, 0)),
                  pl.BlockSpec((1, LANES), lambda i: (0, 0))],
        out_specs=pl.BlockSpec((ts, LANES), lambda i: (i, 0)),
        out_shape=jax.ShapeDtypeStruct((T, LANES), jnp.int32),
        compiler_params=_cparams(("parallel",)),
        name="dest_rows",
    )(route, pstart)
    return dest[:, :TOP_K].reshape(T * TOP_K)


def _dispatch_kernel(fill_ref, dest_ref, x_ref, out_hbm, zero_ref, sem, zero_sem, *, ts, n_blocks):
    block_slabs = ROW_BLOCK * SUBLANES

    def fill_copy(b):
        off = pl.multiple_of(b * block_slabs, block_slabs)
        return pltpu.make_async_copy(zero_ref, out_hbm.at[pl.ds(off, block_slabs)], zero_sem)

    @pl.when(pl.program_id(0) == 0)
    def _():
        zero_ref[...] = jnp.zeros(zero_ref.shape, F32)

        def start(b, carry):
            @pl.when(fill_ref[b] != 0)
            def _():
                fill_copy(b).start()
            return carry

        def wait(b, carry):
            @pl.when(fill_ref[b] != 0)
            def _():
                fill_copy(b).wait()
            return carry

        lax.fori_loop(0, n_blocks, start, 0)
        lax.fori_loop(0, n_blocks, wait, 0)

    def issue(i, carry):
        for j in range(ISSUE_UNROLL):
            r = i * ISSUE_UNROLL + j
            for k in range(TOP_K):
                d = dest_ref[r * TOP_K + k]
                pltpu.make_async_copy(_slab(x_ref, r), _slab(out_hbm, d), sem).start(priority=k % 2)
        return carry

    lax.fori_loop(0, ts // ISSUE_UNROLL, issue, 0)
    for k in range(TOP_K):
        pltpu.make_async_copy(x_ref, out_hbm.at[pl.ds(0, ts * SUBLANES)], sem).wait()


def _dispatch(xn_slabs, fill, dest_flat, n_rows):
    T = xn_slabs.shape[0] // SUBLANES
    ts = min(TS_DISP, T)
    grid_spec = pltpu.PrefetchScalarGridSpec(
        num_scalar_prefetch=1,
        grid=(T // ts,),
        in_specs=[
            pl.BlockSpec((ts * TOP_K,), lambda i, fl: (i,), memory_space=pltpu.SMEM),
            pl.BlockSpec((ts * SUBLANES, LANES), lambda i, fl: (i, 0)),
        ],
        out_specs=pl.BlockSpec(memory_space=pl.ANY),
        scratch_shapes=[pltpu.VMEM((ROW_BLOCK * SUBLANES, LANES), F32),
                        pltpu.SemaphoreType.DMA, pltpu.SemaphoreType.DMA],
    )
    return pl.pallas_call(
        functools.partial(_dispatch_kernel, ts=ts, n_blocks=n_rows // ROW_BLOCK),
        grid_spec=grid_spec,
        out_shape=jax.ShapeDtypeStruct((n_rows * SUBLANES, LANES), xn_slabs.dtype),
        compiler_params=_cparams(("arbitrary",)),
        name="dispatch",
    )(fill, dest_flat, xn_slabs)


def _expert_kernel(be_ref, na_ref, nxt_ref, x_ref, wg_hbm, bg_ref, wu_hbm, bu_ref, wd_hbm, bd_ref,
                   y_ref, stage_ref, wb_ref, slot_ref, sems):
    i = pl.program_id(0)
    e = be_ref[i]
    w_hbm = (wg_hbm, wu_hbm, wd_hbm)

    def fetch(expert, slot, m):
        return pltpu.make_async_copy(w_hbm[m].at[expert], stage_ref.at[slot, m], sems.at[slot, m])

    @pl.when(i == 0)
    def _():
        slot_ref[0] = 0
        for m in range(3):
            fetch(e, 0, m).start()

    active = i < na_ref[0]
    first = jnp.logical_or(i == 0, e != be_ref[jnp.maximum(i - 1, 0)])

    @pl.when(jnp.logical_and(active, first))
    def _():
        slot = slot_ref[0]
        for m in range(3):
            fetch(e, slot, m).wait()
            wb_ref[m] = stage_ref[slot, m].astype(BF16)

        @pl.when(nxt_ref[i] >= 0)
        def _():
            for m in range(3):
                fetch(nxt_ref[i], 1 - slot, m).start()

        slot_ref[0] = 1 - slot

    @pl.when(active)
    def _():
        x = _slabs_to_rows(x_ref, ROW_BLOCK).astype(BF16)
        g = jnp.dot(x, wb_ref[0], preferred_element_type=F32) + bg_ref[0]
        u = jnp.dot(x, wb_ref[1], preferred_element_type=F32) + bu_ref[0]
        g = jnp.minimum(g, SWIGLU_LIMIT)
        u = jnp.clip(u, -SWIGLU_LIMIT, SWIGLU_LIMIT)
        glu = g * jax.nn.sigmoid(SWIGLU_ALPHA * g)
        y = jnp.dot(((u + 1.0) * glu).astype(BF16), wb_ref[2], preferred_element_type=F32) + bd_ref[0]
        _rows_to_slabs(y_ref, y)


def _experts(x_rows, block_e, n_active, next_e, w_gate, b_gate, w_up, b_up, w_down, b_down):
    E, D, FF = w_gate.shape
    assert D == FF, "the three expert matrices share one staging shape"
    block_slabs = ROW_BLOCK * SUBLANES
    n_blocks = x_rows.shape[0] // block_slabs

    def row_map(i, be, na, nx):
        return (jnp.minimum(i, na[0] - 1), 0)

    def b_map(i, be, na, nx):
        return (be[jnp.minimum(i, na[0] - 1)], 0, 0)

    grid_spec = pltpu.PrefetchScalarGridSpec(
        num_scalar_prefetch=3,
        grid=(n_blocks,),
        in_specs=[
            pl.BlockSpec((block_slabs, LANES), row_map),
            pl.BlockSpec(memory_space=pl.ANY),
            pl.BlockSpec((1, 1, FF), b_map),
            pl.BlockSpec(memory_space=pl.ANY),
            pl.BlockSpec((1, 1, FF), b_map),
            pl.BlockSpec(memory_space=pl.ANY),
            pl.BlockSpec((1, 1, D), b_map),
        ],
        out_specs=pl.BlockSpec((block_slabs, LANES), row_map),
        scratch_shapes=[
            pltpu.VMEM((2, 3, D, FF), F32),
            pltpu.VMEM((3, D, FF), BF16),
            pltpu.SMEM((1,), jnp.int32),
            pltpu.SemaphoreType.DMA((2, 3)),
        ],
    )
    return pl.pallas_call(
        _expert_kernel,
        grid_spec=grid_spec,
        out_shape=jax.ShapeDtypeStruct(x_rows.shape, F32),
        input_output_aliases={3: 0},
        compiler_params=pltpu.CompilerParams(dimension_semantics=("arbitrary",),
                                             vmem_limit_bytes=EXPERT_VMEM_LIMIT),
        name="experts",
    )(block_e, n_active, next_e, x_rows, w_gate, b_gate.reshape(E, 1, FF), w_up,
      b_up.reshape(E, 1, FF), w_down, b_down.reshape(E, 1, D))


def _combine_kernel(dest_ref, dest_next_ref, y_hbm, x1_ref, gates_ref, fg_ref, o_ref, bufs, sems,
                    *, ts, n_steps):
    i = pl.program_id(0)
    slot = i % 2

    def gather_tile(d_ref, s):
        def issue(it, carry):
            for j in range(ISSUE_UNROLL):
                r = it * ISSUE_UNROLL + j
                for k in range(TOP_K):
                    d = d_ref[r * TOP_K + k]
                    pltpu.make_async_copy(_slab(y_hbm, d), _slab(bufs.at[s, k], r),
                                          sems.at[s]).start(priority=k % 2)
            return carry

        lax.fori_loop(0, ts // ISSUE_UNROLL, issue, 0)

    @pl.when(i == 0)
    def _():
        gather_tile(dest_ref, 0)

    @pl.when(i + 1 < n_steps)
    def _():
        gather_tile(dest_next_ref, 1 - slot)

    for k in range(TOP_K):
        pltpu.make_async_copy(y_hbm.at[pl.ds(0, ts * SUBLANES)], bufs.at[slot, k], sems.at[slot]).wait()

    acc = x1_ref[...]
    gates = gates_ref[...]
    for k in range(TOP_K):
        acc = acc + _slabs_to_rows(bufs.at[slot, k], ts) * gates[:, k:k + 1]
    ms = jnp.mean(acc * acc, axis=-1, keepdims=True)
    o_ref[...] = acc * lax.rsqrt(ms + NORM_EPS) * fg_ref[...]


def _combine(y_rows, dest_flat, x1, gates, final_g):
    T, D = x1.shape
    ts = min(TS_COMB, T)
    n_steps = T // ts
    tok = lambda i: (i, 0)
    return pl.pallas_call(
        functools.partial(_combine_kernel, ts=ts, n_steps=n_steps),
        grid=(n_steps,),
        in_specs=[
            pl.BlockSpec((ts * TOP_K,), lambda i: (i,), memory_space=pltpu.SMEM),
            pl.BlockSpec((ts * TOP_K,), lambda i: (jnp.minimum(i + 1, n_steps - 1),),
                         memory_space=pltpu.SMEM),
            pl.BlockSpec(memory_space=pl.ANY),
            pl.BlockSpec((ts, D), tok),
            pl.BlockSpec((ts, LANES), tok),
            pl.BlockSpec((1, D), lambda i: (0, 0)),
        ],
        out_specs=pl.BlockSpec((ts, D), tok),
        out_shape=jax.ShapeDtypeStruct((T, D), F32),
        scratch_shapes=[pltpu.VMEM((2, TOP_K, ts * SUBLANES, LANES), F32),
                        pltpu.SemaphoreType.DMA((2,))],
        compiler_params=_cparams(("arbitrary",)),
        name="combine",
    )(dest_flat, dest_flat, y_rows, x1, gates, final_g.reshape(1, D))


def kernel(x, norm1_g, w_in, q_norm_g, k_norm_g, conv_w, conv_b, lru_wa, lru_ba, lru_wi, lru_bi,
           lru_lam, attn_out_g, lru_out_g, w_out, norm2_g, w_router, b_router, w_gate, b_gate,
           w_up, b_up, w_down, b_down, final_g):
    B, S, D = x.shape
    T = B * S
    assert w_in.shape[0] == 1, "single-layer trunk: the final norm is fused into the layer's combine"
    x2 = x.reshape(T, D)
    for l in range(1):
        qt, k, vt, lru_x, lru_gate = _inproj(x2, norm1_g[l], w_in[l], q_norm_g[l], k_norm_g[l], S)
        score_bound = (HEAD_DIM * Q_SCALE * jnp.max(jnp.abs(q_norm_g[l]))
                       * jnp.max(jnp.abs(k_norm_g[l])))
        lru_ops = _lru_operands(conv_w[l], conv_b[l], lru_wa[l], lru_ba[l], lru_wi[l], lru_bi[l],
                                lru_lam[l])
        attn, lru = _mixers(qt, k.reshape(B, S, -1), vt, lru_x.reshape(B, S, -1),
                            lru_gate.reshape(B, S, -1), lru_ops, score_bound, B, S)
        x1, xn3, route, gates, cnt = _outproj_router(
            attn.reshape(T, -1), lru.reshape(T, -1), x2, attn_out_g[l], lru_out_g[l], w_out[l],
            norm2_g[l], w_router[l], b_router[l])

        n_rows = T * TOP_K + N_EXPERTS * ROW_BLOCK
        pstart, block_e, fill, next_e, n_active = _routing_plan(cnt, n_rows // ROW_BLOCK)
        dest_flat = _dest_rows(route, pstart)
        x_rows = _dispatch(xn3, fill, dest_flat, n_rows)
        y_rows = _experts(x_rows, block_e, n_active, next_e, w_gate[l], b_gate[l], w_up[l], b_up[l],
                          w_down[l], b_down[l])
        x2 = _combine(y_rows, dest_flat, x1, gates, final_g)
    return x2.reshape(B, S, D)
```

```python
import functools
import math

import jax
import jax.numpy as jnp
import numpy as np
from jax import lax
from jax.experimental import pallas as pl
from jax.experimental.pallas import tpu as pltpu

F32 = jnp.float32
BF16 = jnp.bfloat16

GRID_W = 64
HEAD_DIM = 64
N_Q_HEADS = 8
N_KV_HEADS = 2
GQA_GROUP = N_Q_HEADS // N_KV_HEADS
ATTN_W = N_Q_HEADS * HEAD_DIM
KV_W = N_KV_HEADS * HEAD_DIM
LRU_BLOCKS = 8
LRU_C = 8.0
CONV_W = 4
CONV_PAD_L = 2
ROPE_THETA = 10000.0
ROPE_HALF = HEAD_DIM // 2
ROPE_M = ROPE_HALF // 2
N_EXPERTS = 32
TOP_K = 4
SWIGLU_ALPHA = 1.702
SWIGLU_LIMIT = 7.0
NORM_EPS = 1e-5
QK_EPS = 1e-6
LOG2_E = 1.4426950408889634
Q_SCALE = HEAD_DIM ** -0.5 * LOG2_E
SAFE_SCORE_LOG2 = 96.0

LANES = 128
SUBLANES = 8
BF16_SUBLANES = 16
PV_ROWS = HEAD_DIM + BF16_SUBLANES
VMEM_LIMIT = 48 * 1024 * 1024
EXPERT_VMEM_LIMIT = 56 * 1024 * 1024

TS_IN = 512
TQ = 256
TK = 256
KV_UNROLL = 8
HEADS_PER_STEP = 2
TC_LRU = 512
TS_OUT = 512
TS_DEST = 2048
ROW_BLOCK = 512
TS_DISP = 512
TS_COMB = 256
ISSUE_UNROLL = 8


def _cparams(sem):
    return pltpu.CompilerParams(dimension_semantics=sem, vmem_limit_bytes=VMEM_LIMIT)


def _inproj_kernel(x_ref, g1_ref, wt_ref, w_ref, qg_ref, kg_ref, cos_ref, sin_ref, cost_ref, sint_ref,
                   q_ref, k_ref, v_ref, lx_ref, lg_ref, *, lru_w):
    x = x_ref[...]
    ms = jnp.mean(x * x, axis=-1, keepdims=True)
    xn = (x * lax.rsqrt(ms + NORM_EPS) * g1_ref[...]).astype(BF16)
    ht = lax.dot_general(wt_ref[...], xn, (((1,), (1,)), ((), ())), preferred_element_type=F32)
    h = jnp.dot(xn, w_ref[...], preferred_element_type=F32)

    qw = N_Q_HEADS * LANES
    kw = N_KV_HEADS * LANES
    cost = cost_ref[...]
    sint = sint_ref[...]
    row = lax.broadcasted_iota(jnp.int32, cost.shape, 0)
    first_half_t = (row % ROPE_HALF) < ROPE_M
    qg = qg_ref[...]
    for c in range(N_Q_HEADS):
        sl = slice(c * LANES, (c + 1) * LANES)
        xc = ht[sl]
        hms = jnp.sum(xc * xc, axis=0, keepdims=True) * (1.0 / HEAD_DIM)
        xc = xc * lax.rsqrt(hms + QK_EPS) * qg
        partner = jnp.where(first_half_t, pltpu.roll(xc, LANES - ROPE_M, 0), pltpu.roll(xc, ROPE_M, 0))
        q_ref[0, sl, :] = ((xc * cost + partner * sint) * Q_SCALE).astype(BF16)
    for c in range(N_KV_HEADS):
        sl = slice(c * LANES, (c + 1) * LANES)
        v_ref[0, sl, :] = jnp.where(row >= HEAD_DIM, 1.0, ht[qw + c * LANES: qw + (c + 1) * LANES]).astype(BF16)

    cos = cos_ref[...]
    sin = sin_ref[...]
    lane = lax.broadcasted_iota(jnp.int32, cos.shape, 1)
    first_half = (lane % ROPE_HALF) < ROPE_M
    for c in range(N_KV_HEADS):
        sl = slice(c * LANES, (c + 1) * LANES)
        xc = h[:, sl]
        hms = jnp.sum(xc * xc, axis=-1, keepdims=True) * (1.0 / HEAD_DIM)
        xc = xc * lax.rsqrt(hms + QK_EPS) * kg_ref[...]
        partner = jnp.where(first_half, pltpu.roll(xc, LANES - ROPE_M, 1), pltpu.roll(xc, ROPE_M, 1))
        k_ref[:, sl] = (xc * cos + partner * sin).astype(BF16)
    lx_ref[...] = h[:, kw: kw + lru_w]
    lg_ref[...] = h[:, kw + lru_w: kw + 2 * lru_w]


def _pad_heads(w, n_heads):
    lead = w.shape[:-1]
    w = w.reshape(lead + (n_heads, HEAD_DIM))
    w = jnp.pad(w, [(0, 0)] * len(lead) + [(0, 0), (0, LANES - HEAD_DIM)])
    return w.reshape(lead + (n_heads * LANES,))


def _rope_tables(S):
    t = np.arange(S)
    rows = (t // GRID_W).astype(np.float32)
    cols = (t % GRID_W).astype(np.float32)
    inv_freq = (ROPE_THETA ** (-np.arange(ROPE_M, dtype=np.float32) / ROPE_M)).astype(np.float32)
    ar = rows[:, None] * inv_freq[None, :]
    ac = cols[:, None] * inv_freq[None, :]
    cos = np.concatenate([np.cos(ar), np.cos(ar), np.cos(ac), np.cos(ac)], axis=-1)
    sin = np.concatenate([-np.sin(ar), np.sin(ar), -np.sin(ac), np.sin(ac)], axis=-1)
    pad = [(0, 0), (0, LANES - HEAD_DIM)]
    cos = np.pad(cos, pad).astype(np.float32)
    sin = np.pad(sin, pad).astype(np.float32)
    return cos, sin, np.ascontiguousarray(cos.T), np.ascontiguousarray(sin.T)


def _inproj(x2, norm1_g, w_in, q_norm_g, k_norm_g, S):
    T, D = x2.shape
    lru_w = (w_in.shape[1] - ATTN_W - 2 * KV_W) // 2
    o0, o1, o2 = ATTN_W, ATTN_W + KV_W, ATTN_W + 2 * KV_W
    w_t = jnp.concatenate([_pad_heads(w_in[:, :o0], N_Q_HEADS),
                           _pad_heads(w_in[:, o1:o2], N_KV_HEADS)], axis=1).T.astype(BF16)
    w_rest = jnp.concatenate([_pad_heads(w_in[:, o0:o1], N_KV_HEADS), w_in[:, o2:]],
                             axis=1).astype(BF16)
    qg = _pad_heads(q_norm_g.reshape(1, HEAD_DIM), 1).reshape(LANES, 1)
    kg = _pad_heads(k_norm_g.reshape(1, HEAD_DIM), 1)
    cos, sin, cos_t, sin_t = _rope_tables(S)
    ts = TS_IN
    n_s = S // ts
    qw, kw = N_Q_HEADS * LANES, N_KV_HEADS * LANES
    const = lambda i: (0, 0)
    tok = lambda i: (i, 0)
    pos = lambda i: (i % n_s, 0)
    pos_t = lambda i: (0, i % n_s)
    tposed = lambda i: (i // n_s, 0, i % n_s)
    return pl.pallas_call(
        functools.partial(_inproj_kernel, lru_w=lru_w),
        grid=(T // ts,),
        in_specs=[
            pl.BlockSpec((ts, D), tok),
            pl.BlockSpec((1, D), const),
            pl.BlockSpec(w_t.shape, const),
            pl.BlockSpec(w_rest.shape, const),
            pl.BlockSpec((LANES, 1), const),
            pl.BlockSpec((1, LANES), const),
            pl.BlockSpec((ts, LANES), pos),
            pl.BlockSpec((ts, LANES), pos),
            pl.BlockSpec((LANES, ts), pos_t),
            pl.BlockSpec((LANES, ts), pos_t),
        ],
        out_specs=[
            pl.BlockSpec((1, qw, ts), tposed),
            pl.BlockSpec((ts, kw), tok),
            pl.BlockSpec((1, kw, ts), tposed),
            pl.BlockSpec((ts, lru_w), tok),
            pl.BlockSpec((ts, lru_w), tok),
        ],
        out_shape=[
            jax.ShapeDtypeStruct((T // S, qw, S), BF16),
            jax.ShapeDtypeStruct((T, kw), BF16),
            jax.ShapeDtypeStruct((T // S, kw, S), BF16),
            jax.ShapeDtypeStruct((T, lru_w), F32),
            jax.ShapeDtypeStruct((T, lru_w), F32),
        ],
        compiler_params=_cparams(("parallel",)),
        name="inproj",
    )(x2, norm1_g.reshape(1, D), w_t, w_rest, qg, kg, cos, sin, cos_t, sin_t)


def _attn_kernel(qt_ref, k_ref, vt_ref, o_ref, acc_ref, s_ref, p_ref, *, tq, tk, n_kv, kv_unroll):
    hp = HEADS_PER_STEP
    spt = GQA_GROUP // hp
    acc_ref[...] = jnp.zeros(acc_ref.shape, F32)

    def scores(j, sp):
        kt = k_ref[0, pl.ds(pl.multiple_of(j * tk, tk), tk), :]
        out = []
        for u in range(hp):
            g = sp * hp + u
            s = jnp.dot(kt, qt_ref[0, g * LANES:(g + 1) * LANES, :], preferred_element_type=F32)
            out.append((s, jnp.max(s, axis=0, keepdims=True)))
        return out

    def softmax_stage(sc, ms, sp):
        out = []
        for u, (s, s_max) in enumerate(sc):
            h = sp * hp + u
            m_new = jnp.maximum(ms[h], s_max)
            out.append((jnp.exp2(ms[h] - m_new), jnp.exp2(s - m_new).astype(BF16)))
            ms[h] = m_new
        return out

    def pv_stage(j, sp, ap):
        vt = vt_ref[0, 0:PV_ROWS, pl.ds(pl.multiple_of(j * tk, tk), tk)]
        for u, (alpha, p) in enumerate(ap):
            g = sp * hp + u
            acc_ref[g] = alpha * acc_ref[g] + jnp.dot(vt, p, preferred_element_type=F32)

    ms = [jnp.full((1, tq), -jnp.inf, F32)] * GQA_GROUP
    ap = softmax_stage(scores(0, 0), ms, 0)
    sc = scores(min(1 // spt, n_kv - 1), 1 % spt)
    for u in range(hp):
        s_ref[u] = sc[u][0]
        p_ref[u] = ap[u][1]

    def body(it, carry):
        ms = list(carry[:GQA_GROUP])
        ap = [(carry[GQA_GROUP + u], p_ref[u]) for u in range(hp)]
        sc = [(s_ref[u], carry[GQA_GROUP + hp + u]) for u in range(hp)]
        for n in range(kv_unroll * spt):
            j = it * kv_unroll + n // spt
            j_next = jnp.minimum(it * kv_unroll + (n + 2) // spt, n_kv - 1)
            sc_next = scores(j_next, (n + 2) % spt)
            ap_next = softmax_stage(sc, ms, (n + 1) % spt)
            pv_stage(j, n % spt, ap)
            sc, ap = sc_next, ap_next
        for u in range(hp):
            s_ref[u] = sc[u][0]
            p_ref[u] = ap[u][1]
        return tuple(ms) + tuple(a for a, _ in ap) + tuple(m for _, m in sc)

    lax.fori_loop(0, n_kv // kv_unroll, body,
                  tuple(ms) + tuple(a for a, _ in ap) + tuple(m for _, m in sc))
    _attn_finalize(acc_ref, o_ref, tq)


def _attn_finalize(acc_ref, o_ref, tq):
    pad = jnp.zeros((LANES - HEAD_DIM, tq), F32)
    for g in range(GQA_GROUP):
        acc = acc_ref[g]
        o = acc[0:HEAD_DIM] / acc[HEAD_DIM:HEAD_DIM + 1, :]
        o_ref[0, :, g * LANES:(g + 1) * LANES] = jnp.concatenate([o, pad], axis=0).T.astype(BF16)


def _attn_bounded_kernel(qt_ref, k_ref, vt_ref, o_ref, acc_ref, s_ref, p_ref, *, tq, tk, n_kv, kv_unroll):
    hp = HEADS_PER_STEP
    spt = GQA_GROUP // hp
    acc_ref[...] = jnp.zeros(acc_ref.shape, F32)

    def scores(j, sp):
        kt = k_ref[0, pl.ds(pl.multiple_of(j * tk, tk), tk), :]
        return [jnp.dot(kt, qt_ref[0, (sp * hp + u) * LANES:(sp * hp + u + 1) * LANES, :],
                        preferred_element_type=F32) for u in range(hp)]

    def probs(sc):
        return [jnp.exp2(s).astype(BF16) for s in sc]

    def pv_stage(j, sp, ps):
        vt = vt_ref[0, 0:PV_ROWS, pl.ds(pl.multiple_of(j * tk, tk), tk)]
        for u, p in enumerate(ps):
            acc_ref[sp * hp + u] += jnp.dot(vt, p, preferred_element_type=F32)

    ps = probs(scores(0, 0))
    sc = scores(min(1 // spt, n_kv - 1), 1 % spt)
    for u in range(hp):
        s_ref[u] = sc[u]
        p_ref[u] = ps[u]

    def body(it, carry):
        ps = [p_ref[u] for u in range(hp)]
        sc = [s_ref[u] for u in range(hp)]
        for n in range(kv_unroll * spt):
            j = it * kv_unroll + n // spt
            j_next = jnp.minimum(it * kv_unroll + (n + 2) // spt, n_kv - 1)
            sc_next = scores(j_next, (n + 2) % spt)
            ps_next = probs(sc)
            pv_stage(j, n % spt, ps)
            sc, ps = sc_next, ps_next
        for u in range(hp):
            s_ref[u] = sc[u]
            p_ref[u] = ps[u]
        return carry

    lax.fori_loop(0, n_kv // kv_unroll, body, 0)
    _attn_finalize(acc_ref, o_ref, tq)


def _attention(qt, k, vt, *, kernel, B, S):
    tq = min(TQ, S)
    tk = min(TK, S)
    gw = GQA_GROUP * LANES
    return pl.pallas_call(
        functools.partial(kernel, tq=tq, tk=tk, n_kv=S // tk,
                          kv_unroll=math.gcd(S // tk, KV_UNROLL)),
        grid=(B, N_KV_HEADS, S // tq),
        in_specs=[
            pl.BlockSpec((1, gw, tq), lambda b, h, i: (b, h, i)),
            pl.BlockSpec((1, S, LANES), lambda b, h, i: (b, 0, h)),
            pl.BlockSpec((1, LANES, S), lambda b, h, i: (b, h, 0)),
        ],
        out_specs=pl.BlockSpec((1, tq, gw), lambda b, h, i: (b, i, h)),
        out_shape=jax.ShapeDtypeStruct((B, S, N_Q_HEADS * LANES), BF16),
        scratch_shapes=[pltpu.VMEM((GQA_GROUP, PV_ROWS, tq), F32),
                        pltpu.VMEM((HEADS_PER_STEP, tk, tq), F32),
                        pltpu.VMEM((HEADS_PER_STEP, tk, tq), BF16)],
        compiler_params=_cparams(("parallel", "parallel", "parallel")),
        name=kernel.__name__.strip("_"),
    )(qt, k, vt)


def _scan_chunk(a, b, h_in, reverse):
    n = a.shape[0]
    n_groups = n // SUBLANES
    a = a.reshape(n_groups, SUBLANES, LANES)
    b = b.reshape(n_groups, SUBLANES, LANES)
    sub = lax.broadcasted_iota(jnp.int32, a.shape, 1)
    d = 1
    while d < SUBLANES:
        if reverse:
            keep = sub < SUBLANES - d
            shift = SUBLANES - d
        else:
            keep = sub >= d
            shift = d
        a_sh = jnp.where(keep, pltpu.roll(a, shift, 1), 1.0)
        b_sh = jnp.where(keep, pltpu.roll(b, shift, 1), 0.0)
        b = a * b_sh + b
        a = a * a_sh
        d *= 2
    a = a.reshape(n, LANES)
    b = b.reshape(n, LANES)
    order = range(n_groups - 1, -1, -1) if reverse else range(n_groups)
    edge = h_in
    out = [None] * n_groups
    for v in order:
        rows = slice(v * SUBLANES, (v + 1) * SUBLANES)
        hv = b[rows] + a[rows] * jnp.broadcast_to(edge, (SUBLANES, LANES))
        out[v] = hv
        edge = hv[0:1] if reverse else hv[SUBLANES - 1:SUBLANES]
    return jnp.concatenate(out, axis=0), edge


def _lru_pad_input(u_ref, up_ref, S):
    zeros = jnp.zeros((SUBLANES, LANES), F32)
    up_ref[0:SUBLANES, :] = zeros
    up_ref[S + SUBLANES:S + 2 * SUBLANES, :] = zeros
    up_ref[SUBLANES:S + SUBLANES, :] = u_ref[0]


def _lru_gates(up_ref, cw_ref, cb_ref, w_ref, bias_ref, lam_ref, t0, tc, d):
    cw = cw_ref[...]
    xc = cb_ref[...]
    for j in range(CONV_W):
        xc = xc + up_ref[pl.ds(t0 + SUBLANES + j - CONV_PAD_L, tc), :] * cw[j:j + 1, :]
    gw = 2 * LANES
    g = jnp.dot(xc.astype(BF16), w_ref[0, :, d * gw:(d + 1) * gw],
                preferred_element_type=F32) + bias_ref[0, :, d * gw:(d + 1) * gw]
    r = jax.nn.sigmoid(g[:, :LANES])
    i = jax.nn.sigmoid(g[:, LANES:])
    a = jnp.exp(-LRU_C * r * jax.nn.softplus(-lam_ref[d:d + 1, :]))
    y = 1.0 - a * a
    b = jnp.where(y > 0.0, y * lax.rsqrt(y), 0.0) * i * xc
    return a, b


def _lru_kernel(u_ref, gate_ref, cw_ref, cb_ref, w_ref, bias_ref, lam_ref, o_ref,
                up_ref, hf_ref, *, S, tc):
    _lru_pad_input(u_ref, up_ref, S)
    n_chunks = S // tc
    params = (up_ref, cw_ref, cb_ref, w_ref, bias_ref, lam_ref)

    def fwd(c, h):
        t0 = pl.multiple_of(c * tc, tc)
        hc, h_last = _scan_chunk(*_lru_gates(*params, t0, tc, 0), h, False)
        hf_ref[pl.ds(t0, tc), :] = hc
        return h_last

    lax.fori_loop(0, n_chunks, fwd, jnp.zeros((1, LANES), F32))

    def bwd(ci, h):
        t0 = pl.multiple_of((n_chunks - 1 - ci) * tc, tc)
        hc, h_last = _scan_chunk(*_lru_gates(*params, t0, tc, 1), h, True)
        gate = gate_ref[0, pl.ds(t0, tc), :]
        o_ref[0, pl.ds(t0, tc), :] = (hf_ref[pl.ds(t0, tc), :] + hc) * jax.nn.gelu(gate)
        return h_last

    lax.fori_loop(0, n_chunks, bwd, jnp.zeros((1, LANES), F32))


def _block_diag_pairs(w):
    nb, bw, _ = w.shape
    w = w.reshape(nb // 2, 2, bw, bw)
    z = jnp.zeros_like(w[:, 0])
    top = jnp.concatenate([w[:, 0], z], axis=-1)
    bot = jnp.concatenate([z, w[:, 1]], axis=-1)
    return jnp.concatenate([top, bot], axis=-2)


def _lru_operands(conv_w, conv_b, wa, ba, wi, bi, lam):
    C = conv_b.shape[0]
    nc = C // LANES
    w = jnp.concatenate([_block_diag_pairs(wa[0]), _block_diag_pairs(wi[0]),
                         _block_diag_pairs(wa[1]), _block_diag_pairs(wi[1])], axis=-1).astype(BF16)
    bias = jnp.stack([ba[0].reshape(nc, LANES), bi[0].reshape(nc, LANES),
                      ba[1].reshape(nc, LANES), bi[1].reshape(nc, LANES)], axis=1)
    return conv_w, conv_b.reshape(1, C), w, bias.reshape(nc, 1, 4 * LANES), lam


def _lru_specs(S, unit):
    seq = lambda *g: (unit(*g)[0], 0, unit(*g)[1])
    chan = lambda *g: (0, unit(*g)[1])
    blk = lambda *g: (unit(*g)[1], 0, 0)
    in_specs = [
        pl.BlockSpec((1, S, LANES), seq),
        pl.BlockSpec((1, S, LANES), seq),
        pl.BlockSpec((CONV_W, LANES), chan),
        pl.BlockSpec((1, LANES), chan),
        pl.BlockSpec((1, LANES, 4 * LANES), blk),
        pl.BlockSpec((1, 1, 4 * LANES), blk),
        pl.BlockSpec((2, LANES), chan),
    ]
    return in_specs, pl.BlockSpec((1, S, LANES), seq)


def _lru(lru_x, lru_gate, lru_ops, B, S):
    C = lru_x.shape[-1]
    tc = min(TC_LRU, S)
    in_specs, out_spec = _lru_specs(S, lambda b, c: (b, c))
    return pl.pallas_call(
        functools.partial(_lru_kernel, S=S, tc=tc),
        grid=(B, C // LANES),
        in_specs=in_specs,
        out_specs=out_spec,
        out_shape=jax.ShapeDtypeStruct((B, S, C), F32),
        scratch_shapes=[
            pltpu.VMEM((S + 2 * SUBLANES, LANES), F32),
            pltpu.VMEM((S, LANES), F32),
        ],
        compiler_params=_cparams(("parallel", "parallel")),
        name="rglru",
    )(lru_x, lru_gate, *lru_ops)


def _mixers(qt, k, vt, lru_x, lru_gate, lru_ops, score_bound, B, S):
    attn = lax.cond(score_bound <= SAFE_SCORE_LOG2,
                    functools.partial(_attention, kernel=_attn_bounded_kernel, B=B, S=S),
                    functools.partial(_attention, kernel=_attn_kernel, B=B, S=S), qt, k, vt)
    return attn, _lru(lru_x, lru_gate, lru_ops, B, S)


def _rows_to_slabs(ref, x):
    n = x.shape[0]
    for s in range(SUBLANES):
        ref[pl.ds(s, n, stride=SUBLANES), :] = x[:, s * LANES:(s + 1) * LANES]


def _slabs_to_rows(ref, n):
    return jnp.concatenate([ref[pl.ds(s, n, stride=SUBLANES), :] for s in range(SUBLANES)], axis=1)


def _slab(ref, r):
    return ref.at[pl.ds(pl.multiple_of(r * SUBLANES, SUBLANES), SUBLANES)]


def _outproj_kernel(a_ref, l_ref, x_ref, ag_ref, lg_ref, wa_ref, wl_ref, g2_ref,
                    wrh_ref, wrl_ref, br_ref, tri_ref,
                    x1_ref, xn3_ref, route_ref, gates_ref, cnt_ref, carry_ref, *, attn_w, lru_w):
    step = pl.program_id(0)

    @pl.when(step == 0)
    def _():
        carry_ref[...] = jnp.zeros_like(carry_ref)

    a = a_ref[...].astype(F32)
    ams = jnp.sum(a * a, axis=-1, keepdims=True) * (1.0 / attn_w)
    an = a * lax.rsqrt(ams + NORM_EPS) * ag_ref[...]
    l = l_ref[...]
    lms = jnp.sum(l * l, axis=-1, keepdims=True) * (1.0 / lru_w)
    ln = l * lax.rsqrt(lms + NORM_EPS) * lg_ref[...]
    mix = (jnp.dot(an.astype(BF16), wa_ref[...], preferred_element_type=F32)
           + jnp.dot(ln.astype(BF16), wl_ref[...], preferred_element_type=F32))
    x1 = x_ref[...] + mix
    x1_ref[...] = x1
    ms = jnp.mean(x1 * x1, axis=-1, keepdims=True)
    xn = x1 * lax.rsqrt(ms + NORM_EPS) * g2_ref[...]
    _rows_to_slabs(xn3_ref, xn)

    hi = xn.astype(BF16)
    lo = (xn - hi.astype(F32)).astype(BF16)
    logits = (jnp.dot(hi, wrh_ref[...], preferred_element_type=F32)
              + jnp.dot(lo, wrh_ref[...], preferred_element_type=F32)
              + jnp.dot(hi, wrl_ref[...], preferred_element_type=F32)) + br_ref[...]
    lane = lax.broadcasted_iota(jnp.int32, logits.shape, 1)
    neg = -jnp.inf
    work = jnp.where(lane < N_EXPERTS, logits, neg)
    sel = jnp.zeros(logits.shape, F32)
    idxs, vals = [], []
    for _ in range(TOP_K):
        m = jnp.max(work, axis=1, keepdims=True)
        idx = jnp.min(jnp.where(work == m, lane, LANES), axis=1, keepdims=True)
        hit = lane == idx
        work = jnp.where(hit, neg, work)
        sel = sel + hit.astype(F32)
        idxs.append(idx)
        vals.append(m)
    es = [jnp.exp(v - vals[0]) for v in vals]
    den = es[0] + es[1] + es[2] + es[3]

    prefix = jnp.dot(tri_ref[...], sel.astype(BF16), preferred_element_type=F32) + carry_ref[...]
    carry_ref[...] = carry_ref[...] + jnp.sum(sel, axis=0, keepdims=True)
    cnt_ref[...] = carry_ref[...]

    route = jnp.zeros(logits.shape, jnp.int32)
    gates = jnp.zeros(logits.shape, F32)
    for k in range(TOP_K):
        rank = jnp.sum(jnp.where(lane == idxs[k], prefix, 0.0), axis=1, keepdims=True).astype(jnp.int32)
        route = jnp.where(lane == k, idxs[k], route)
        route = jnp.where(lane == TOP_K + k, rank, route)
        gates = jnp.where(lane == k, es[k] / den, gates)
    route_ref[...] = route
    gates_ref[...] = gates


def _outproj_router(attn, lru, x2, attn_out_g, lru_out_g, w_out, norm2_g, w_router, b_router):
    T, D = x2.shape
    lru_w = lru.shape[-1]
    ts = min(TS_OUT, T)
    wa = w_out[:ATTN_W].reshape(N_Q_HEADS, HEAD_DIM, D)
    wa = jnp.pad(wa, ((0, 0), (0, LANES - HEAD_DIM), (0, 0))).reshape(N_Q_HEADS * LANES, D).astype(BF16)
    wl = w_out[ATTN_W:].astype(BF16)
    ag = _pad_heads(attn_out_g.reshape(1, ATTN_W), N_Q_HEADS)
    wr = jnp.pad(w_router, ((0, 0), (0, LANES - N_EXPERTS)))
    wrh = wr.astype(BF16)
    wrl = (wr - wrh.astype(F32)).astype(BF16)
    br = jnp.pad(b_router.reshape(1, N_EXPERTS), ((0, 0), (0, LANES - N_EXPERTS)))
    tri = (jnp.arange(ts)[:, None] > jnp.arange(ts)[None, :]).astype(BF16)
    const = lambda i: (0, 0)
    tok = lambda i: (i, 0)
    aw = N_Q_HEADS * LANES
    return pl.pallas_call(
        functools.partial(_outproj_kernel, attn_w=ATTN_W, lru_w=lru_w),
        grid=(T // ts,),
        in_specs=[
            pl.BlockSpec((ts, aw), tok),
            pl.BlockSpec((ts, lru_w), tok),
            pl.BlockSpec((ts, D), tok),
            pl.BlockSpec((1, aw), const),
            pl.BlockSpec((1, lru_w), const),
            pl.BlockSpec((aw, D), const),
            pl.BlockSpec((lru_w, D), const),
            pl.BlockSpec((1, D), const),
            pl.BlockSpec((D, LANES), const),
            pl.BlockSpec((D, LANES), const),
            pl.BlockSpec((1, LANES), const),
            pl.BlockSpec((ts, ts), const),
        ],
        out_specs=[
            pl.BlockSpec((ts, D), tok),
            pl.BlockSpec((ts * SUBLANES, LANES), tok),
            pl.BlockSpec((ts, LANES), tok),
            pl.BlockSpec((ts, LANES), tok),
            pl.BlockSpec((1, LANES), const),
        ],
        out_shape=[
            jax.ShapeDtypeStruct((T, D), F32),
            jax.ShapeDtypeStruct((T * SUBLANES, LANES), F32),
            jax.ShapeDtypeStruct((T, LANES), jnp.int32),
            jax.ShapeDtypeStruct((T, LANES), F32),
            jax.ShapeDtypeStruct((1, LANES), F32),
        ],
        scratch_shapes=[pltpu.VMEM((1, LANES), F32)],
        compiler_params=_cparams(("arbitrary",)),
        name="outproj_router",
    )(attn, lru, x2, ag, lru_out_g.reshape(1, lru_w), wa, wl, norm2_g.reshape(1, D),
      wrh, wrl, br, tri)


def _plan_kernel(cnt_ref, pstart_ref, plan_ref):
    cnt = cnt_ref[...]
    lane = lax.broadcasted_iota(jnp.int32, cnt.shape, 1)
    padded = jnp.floor((cnt + (ROW_BLOCK - 1)) * (1.0 / ROW_BLOCK)) * ROW_BLOCK
    pend = padded
    d = 1
    while d < N_EXPERTS:
        pend = pend + jnp.where(lane >= d, pltpu.roll(pend, d, 1), 0.0)
        d *= 2
    pstart_ref[...] = pend - padded
    total = jnp.max(pend, axis=1, keepdims=True)

    shape = plan_ref.shape
    lanes = lax.broadcasted_iota(jnp.int32, shape, 1)
    is_expert = lanes < N_EXPERTS
    start = lax.broadcasted_iota(jnp.int32, shape, 0).astype(F32) * ROW_BLOCK

    def groups_ending_by(row):
        return jnp.sum(jnp.where(jnp.logical_and(pend <= row, is_expert), 1.0, 0.0), axis=1, keepdims=True)

    block_e = jnp.minimum(groups_ending_by(start), N_EXPERTS - 1.0)
    tail = jnp.max(jnp.where(jnp.logical_and(jnp.logical_and(pend == start + ROW_BLOCK, padded > 0.0),
                                             is_expert), 1.0, 0.0), axis=1, keepdims=True)
    fill = jnp.maximum(tail, jnp.where(start[:, 0:1] >= total, 1.0, 0.0))
    group_end = jnp.sum(jnp.where(lanes.astype(F32) == block_e, pend, 0.0), axis=1, keepdims=True)
    next_e = jnp.where(group_end < total,
                       jnp.minimum(groups_ending_by(group_end), N_EXPERTS - 1.0), -1.0)
    plan = jnp.where(lanes == 0, block_e,
                     jnp.where(lanes == 1, fill,
                               jnp.where(lanes == 2, next_e, total * (1.0 / ROW_BLOCK))))
    plan_ref[...] = plan.astype(jnp.int32)


def _routing_plan(cnt, n_blocks):
    assert ROW_BLOCK & (ROW_BLOCK - 1) == 0, "exact f32 division by the row block size"
    rows = -(-n_blocks // SUBLANES) * SUBLANES
    pstart, plan = pl.pallas_call(
        _plan_kernel,
        out_shape=[jax.ShapeDtypeStruct((1, LANES), F32),
                   jax.ShapeDtypeStruct((rows, LANES), jnp.int32)],
        name="routing_plan",
    )(cnt)
    return pstart, plan[:n_blocks, 0], plan[:n_blocks, 1], plan[:n_blocks, 2], plan[0:1, 3]


def _dest_kernel(route_ref, pstart_ref, dest_ref):
    route = route_ref[...]
    lane = lax.broadcasted_iota(jnp.int32, route.shape, 1)
    pstart = pstart_ref[...]
    dest = jnp.zeros(route.shape, jnp.int32)
    for k in range(TOP_K):
        start = jnp.sum(jnp.where(lane == route[:, k:k + 1], pstart, 0.0), axis=1, keepdims=True)
        dest = jnp.where(lane == k, start.astype(jnp.int32) + route[:, TOP_K + k:TOP_K + k + 1], dest)
    dest_ref[...] = dest


def _dest_rows(route, pstart):
    T = route.shape[0]
    ts = math.gcd(TS_DEST, T)
    dest = pl.pallas_call(
        _dest_kernel,
        grid=(T // ts,),
        in_specs=[pl.BlockSpec((ts, LANES), lambda i: (i, 0)),
                  pl.BlockSpec((1, LANES), lambda i: (0, 0))],
        out_specs=pl.BlockSpec((ts, LANES), lambda i: (i, 0)),
        out_shape=jax.ShapeDtypeStruct((T, LANES), jnp.int32),
        compiler_params=_cparams(("parallel",)),
        name="dest_rows",
    )(route, pstart)
    return dest[:, :TOP_K].reshape(T * TOP_K)


def _dispatch_kernel(fill_ref, dest_ref, x_ref, out_hbm, zero_ref, sem, zero_sem, *, ts, n_blocks):
    block_slabs = ROW_BLOCK * SUBLANES

    def fill_copy(b):
        off = pl.multiple_of(b * block_slabs, block_slabs)
        return pltpu.make_async_copy(zero_ref, out_hbm.at[pl.ds(off, block_slabs)], zero_sem)

    @pl.when(pl.program_id(0) == 0)
    def _():
        zero_ref[...] = jnp.zeros(zero_ref.shape, F32)

        def start(b, carry):
            @pl.when(fill_ref[b] != 0)
            def _():
                fill_copy(b).start()
            return carry

        def wait(b, carry):
            @pl.when(fill_ref[b] != 0)
            def _():
                fill_copy(b).wait()
            return carry

        lax.fori_loop(0, n_blocks, start, 0)
        lax.fori_loop(0, n_blocks, wait, 0)

    def issue(i, carry):
        for j in range(ISSUE_UNROLL):
            r = i * ISSUE_UNROLL + j
            for k in range(TOP_K):
                d = dest_ref[r * TOP_K + k]
                pltpu.make_async_copy(_slab(x_ref, r), _slab(out_hbm, d), sem).start(priority=k % 2)
        return carry

    lax.fori_loop(0, ts // ISSUE_UNROLL, issue, 0)
    for k in range(TOP_K):
        pltpu.make_async_copy(x_ref, out_hbm.at[pl.ds(0, ts * SUBLANES)], sem).wait()


def _dispatch(xn_slabs, fill, dest_flat, n_rows):
    T = xn_slabs.shape[0] // SUBLANES
    ts = min(TS_DISP, T)
    grid_spec = pltpu.PrefetchScalarGridSpec(
        num_scalar_prefetch=1,
        grid=(T // ts,),
        in_specs=[
            pl.BlockSpec((ts * TOP_K,), lambda i, fl: (i,), memory_space=pltpu.SMEM),
            pl.BlockSpec((ts * SUBLANES, LANES), lambda i, fl: (i, 0)),
        ],
        out_specs=pl.BlockSpec(memory_space=pl.ANY),
        scratch_shapes=[pltpu.VMEM((ROW_BLOCK * SUBLANES, LANES), F32),
                        pltpu.SemaphoreType.DMA, pltpu.SemaphoreType.DMA],
    )
    return pl.pallas_call(
        functools.partial(_dispatch_kernel, ts=ts, n_blocks=n_rows // ROW_BLOCK),
        grid_spec=grid_spec,
        out_shape=jax.ShapeDtypeStruct((n_rows * SUBLANES, LANES), xn_slabs.dtype),
        compiler_params=_cparams(("arbitrary",)),
        name="dispatch",
    )(fill, dest_flat, xn_slabs)


def _expert_loop_kernel(be_ref, na_ref, nxt_ref, x_hbm, wg_hbm, bg_ref, wu_hbm, bu_ref, wd_hbm, bd_ref,
                        y_hbm, xbuf, ybuf, stage_ref, wb_ref, x_sems, y_sems, w_sems):
    w_hbm = (wg_hbm, wu_hbm, wd_hbm)
    block_slabs = ROW_BLOCK * SUBLANES
    n_active = na_ref[0]

    def rows(b):
        return pl.ds(pl.multiple_of(b * block_slabs, block_slabs), block_slabs)

    def x_copy(b, s):
        return pltpu.make_async_copy(x_hbm.at[rows(b)], xbuf.at[s], x_sems.at[s])

    def y_copy(b, s):
        return pltpu.make_async_copy(ybuf.at[s], y_hbm.at[rows(b)], y_sems.at[s])

    def fetch(expert, s, m):
        return pltpu.make_async_copy(w_hbm[m].at[expert], stage_ref.at[s, m], w_sems.at[s, m])

    x_copy(0, 0).start()
    for m in range(3):
        fetch(be_ref[0], 0, m).start()

    def body(b, wslot):
        s = b % 2
        e = be_ref[b]
        x_copy(b, s).wait()

        @pl.when(b + 1 < n_active)
        def _():
            x_copy(b + 1, 1 - s).start()

        first = jnp.logical_or(b == 0, e != be_ref[jnp.maximum(b - 1, 0)])

        @pl.when(first)
        def _():
            for m in range(3):
                fetch(e, wslot, m).wait()
                wb_ref[m] = stage_ref[wslot, m].astype(BF16)

            @pl.when(nxt_ref[b] >= 0)
            def _():
                for m in range(3):
                    fetch(nxt_ref[b], 1 - wslot, m).start()

        @pl.when(b >= 2)
        def _():
            y_copy(b - 2, s).wait()

        x = _slabs_to_rows(xbuf.at[s], ROW_BLOCK).astype(BF16)
        g = jnp.dot(x, wb_ref[0], preferred_element_type=F32) + bg_ref[e]
        u = jnp.dot(x, wb_ref[1], preferred_element_type=F32) + bu_ref[e]
        g = jnp.minimum(g, SWIGLU_LIMIT)
        u = jnp.clip(u, -SWIGLU_LIMIT, SWIGLU_LIMIT)
        glu = g * jax.nn.sigmoid(SWIGLU_ALPHA * g)
        y = jnp.dot(((u + 1.0) * glu).astype(BF16), wb_ref[2], preferred_element_type=F32) + bd_ref[e]
        _rows_to_slabs(ybuf.at[s], y)
        y_copy(b, s).start()
        return jnp.where(first, 1 - wslot, wslot)

    lax.fori_loop(0, n_active, body, 0)

    @pl.when(n_active >= 2)
    def _():
        y_copy(n_active - 2, n_active % 2).wait()

    y_copy(n_active - 1, (n_active - 1) % 2).wait()


def _experts(x_rows, block_e, n_active, next_e, w_gate, b_gate, w_up, b_up, w_down, b_down):
    E, D, FF = w_gate.shape
    assert D == FF, "the three expert matrices share one staging shape"
    block_slabs = ROW_BLOCK * SUBLANES
    whole = lambda i, be, na, nx: (0, 0, 0)

    grid_spec = pltpu.PrefetchScalarGridSpec(
        num_scalar_prefetch=3,
        grid=(1,),
        in_specs=[
            pl.BlockSpec(memory_space=pl.ANY),
            pl.BlockSpec(memory_space=pl.ANY),
            pl.BlockSpec((E, 1, FF), whole),
            pl.BlockSpec(memory_space=pl.ANY),
            pl.BlockSpec((E, 1, FF), whole),
            pl.BlockSpec(memory_space=pl.ANY),
            pl.BlockSpec((E, 1, D), whole),
        ],
        out_specs=pl.BlockSpec(memory_space=pl.ANY),
        scratch_shapes=[
            pltpu.VMEM((2, block_slabs, LANES), F32),
            pltpu.VMEM((2, block_slabs, LANES), F32),
            pltpu.VMEM((2, 3, D, FF), F32),
            pltpu.VMEM((3, D, FF), BF16),
            pltpu.SemaphoreType.DMA((2,)),
            pltpu.SemaphoreType.DMA((2,)),
            pltpu.SemaphoreType.DMA((2, 3)),
        ],
    )
    return pl.pallas_call(
        _expert_loop_kernel,
        grid_spec=grid_spec,
        out_shape=jax.ShapeDtypeStruct(x_rows.shape, F32),
        input_output_aliases={3: 0},
        compiler_params=pltpu.CompilerParams(dimension_semantics=("arbitrary",),
                                             vmem_limit_bytes=EXPERT_VMEM_LIMIT),
        name="experts",
    )(block_e, n_active, next_e, x_rows, w_gate, b_gate.reshape(E, 1, FF), w_up,
      b_up.reshape(E, 1, FF), w_down, b_down.reshape(E, 1, D))


def _combine_kernel(dest_ref, dest_next_ref, y_hbm, x1_ref, gates_ref, fg_ref, o_ref, bufs, sems,
                    *, ts, n_steps):
    i = pl.program_id(0)
    slot = i % 2

    def gather_tile(d_ref, s):
        def issue(it, carry):
            for j in range(ISSUE_UNROLL):
                r = it * ISSUE_UNROLL + j
                for k in range(TOP_K):
                    d = d_ref[r * TOP_K + k]
                    pltpu.make_async_copy(_slab(y_hbm, d), _slab(bufs.at[s, k], r),
                                          sems.at[s]).start(priority=k % 2)
            return carry

        lax.fori_loop(0, ts // ISSUE_UNROLL, issue, 0)

    @pl.when(i == 0)
    def _():
        gather_tile(dest_ref, 0)

    @pl.when(i + 1 < n_steps)
    def _():
        gather_tile(dest_next_ref, 1 - slot)

    for k in range(TOP_K):
        pltpu.make_async_copy(y_hbm.at[pl.ds(0, ts * SUBLANES)], bufs.at[slot, k], sems.at[slot]).wait()

    acc = x1_ref[...]
    gates = gates_ref[...]
    for k in range(TOP_K):
        acc = acc + _slabs_to_rows(bufs.at[slot, k], ts) * gates[:, k:k + 1]
    ms = jnp.mean(acc * acc, axis=-1, keepdims=True)
    o_ref[...] = acc * lax.rsqrt(ms + NORM_EPS) * fg_ref[...]


def _combine(y_rows, dest_flat, x1, gates, final_g):
    T, D = x1.shape
    ts = min(TS_COMB, T)
    n_steps = T // ts
    tok = lambda i: (i, 0)
    return pl.pallas_call(
        functools.partial(_combine_kernel, ts=ts, n_steps=n_steps),
        grid=(n_steps,),
        in_specs=[
            pl.BlockSpec((ts * TOP_K,), lambda i: (i,), memory_space=pltpu.SMEM),
            pl.BlockSpec((ts * TOP_K,), lambda i: (jnp.minimum(i + 1, n_steps - 1),),
                         memory_space=pltpu.SMEM),
            pl.BlockSpec(memory_space=pl.ANY),
            pl.BlockSpec((ts, D), tok),
            pl.BlockSpec((ts, LANES), tok),
            pl.BlockSpec((1, D), lambda i: (0, 0)),
        ],
        out_specs=pl.BlockSpec((ts, D), tok),
        out_shape=jax.ShapeDtypeStruct((T, D), F32),
        scratch_shapes=[pltpu.VMEM((2, TOP_K, ts * SUBLANES, LANES), F32),
                        pltpu.SemaphoreType.DMA((2,))],
        compiler_params=_cparams(("arbitrary",)),
        name="combine",
    )(dest_flat, dest_flat, y_rows, x1, gates, final_g.reshape(1, D))


def kernel(x, norm1_g, w_in, q_norm_g, k_norm_g, conv_w, conv_b, lru_wa, lru_ba, lru_wi, lru_bi,
           lru_lam, attn_out_g, lru_out_g, w_out, norm2_g, w_router, b_router, w_gate, b_gate,
           w_up, b_up, w_down, b_down, final_g):
    B, S, D = x.shape
    T = B * S
    assert w_in.shape[0] == 1, "single-layer trunk: the final norm is fused into the layer's combine"
    x2 = x.reshape(T, D)
    for l in range(1):
        qt, k, vt, lru_x, lru_gate = _inproj(x2, norm1_g[l], w_in[l], q_norm_g[l], k_norm_g[l], S)
        score_bound = (HEAD_DIM * Q_SCALE * jnp.max(jnp.abs(q_norm_g[l]))
                       * jnp.max(jnp.abs(k_norm_g[l])))
        lru_ops = _lru_operands(conv_w[l], conv_b[l], lru_wa[l], lru_ba[l], lru_wi[l], lru_bi[l],
                                lru_lam[l])
        attn, lru = _mixers(qt, k.reshape(B, S, -1), vt, lru_x.reshape(B, S, -1),
                            lru_gate.reshape(B, S, -1), lru_ops, score_bound, B, S)
        x1, xn3, route, gates, cnt = _outproj_router(
            attn.reshape(T, -1), lru.reshape(T, -1), x2, attn_out_g[l], lru_out_g[l], w_out[l],
            norm2_g[l], w_router[l], b_router[l])

        n_rows = T * TOP_K + N_EXPERTS * ROW_BLOCK
        pstart, block_e, fill, next_e, n_active = _routing_plan(cnt, n_rows // ROW_BLOCK)
        dest_flat = _dest_rows(route, pstart)
        x_rows = _dispatch(xn3, fill, dest_flat, n_rows)
        y_rows = _experts(x_rows, block_e, n_active, next_e, w_gate[l], b_gate[l], w_up[l], b_up[l],
                          w_down[l], b_down[l])
        x2 = _combine(y_rows, dest_flat, x1, gates, final_g)
    return x2.reshape(B, S, D)
```

```python
import functools
import math

import jax
import jax.numpy as jnp
import numpy as np
from jax import lax
from jax.experimental import pallas as pl
from jax.experimental.pallas import tpu as pltpu

F32 = jnp.float32
BF16 = jnp.bfloat16

GRID_W = 64
HEAD_DIM = 64
N_Q_HEADS = 8
N_KV_HEADS = 2
GQA_GROUP = N_Q_HEADS // N_KV_HEADS
ATTN_W = N_Q_HEADS * HEAD_DIM
KV_W = N_KV_HEADS * HEAD_DIM
LRU_BLOCKS = 8
LRU_C = 8.0
CONV_W = 4
CONV_PAD_L = 2
ROPE_THETA = 10000.0
ROPE_HALF = HEAD_DIM // 2
ROPE_M = ROPE_HALF // 2
N_EXPERTS = 32
TOP_K = 4
SWIGLU_ALPHA = 1.702
SWIGLU_LIMIT = 7.0
NORM_EPS = 1e-5
QK_EPS = 1e-6
LOG2_E = 1.4426950408889634
Q_SCALE = HEAD_DIM ** -0.5 * LOG2_E
SAFE_SCORE_LOG2 = 96.0

LANES = 128
SUBLANES = 8
BF16_SUBLANES = 16
PV_ROWS = HEAD_DIM + BF16_SUBLANES
VMEM_LIMIT = 48 * 1024 * 1024
EXPERT_VMEM_LIMIT = 56 * 1024 * 1024

TS_IN = 512
TQ = 256
TK = 256
KV_UNROLL = 8
HEADS_PER_STEP = 2
TC_LRU = 512
TS_OUT = 512
TS_DEST = 2048
ROW_BLOCK = 512
TS_DISP = 512
TS_COMB = 256
ISSUE_UNROLL = 8


def _cparams(sem):
    return pltpu.CompilerParams(dimension_semantics=sem, vmem_limit_bytes=VMEM_LIMIT)


def _inproj_kernel(x_ref, g1_ref, wt_ref, w_ref, qg_ref, kg_ref, cos_ref, sin_ref, cost_ref, sint_ref,
                   q_ref, k_ref, v_ref, lx_ref, lg_ref, *, lru_w):
    x = x_ref[...]
    ms = jnp.mean(x * x, axis=-1, keepdims=True)
    xn = (x * lax.rsqrt(ms + NORM_EPS) * g1_ref[...]).astype(BF16)
    ht = lax.dot_general(wt_ref[...], xn, (((1,), (1,)), ((), ())), preferred_element_type=F32)
    h = jnp.dot(xn, w_ref[...], preferred_element_type=F32)

    qw = N_Q_HEADS * LANES
    kw = N_KV_HEADS * LANES
    cost = cost_ref[...]
    sint = sint_ref[...]
    row = lax.broadcasted_iota(jnp.int32, cost.shape, 0)
    first_half_t = (row % ROPE_HALF) < ROPE_M
    qg = qg_ref[...]
    for c in range(N_Q_HEADS):
        sl = slice(c * LANES, (c + 1) * LANES)
        xc = ht[sl]
        hms = jnp.sum(xc * xc, axis=0, keepdims=True) * (1.0 / HEAD_DIM)
        xc = xc * lax.rsqrt(hms + QK_EPS) * qg
        partner = jnp.where(first_half_t, pltpu.roll(xc, LANES - ROPE_M, 0), pltpu.roll(xc, ROPE_M, 0))
        q_ref[0, sl, :] = ((xc * cost + partner * sint) * Q_SCALE).astype(BF16)
    for c in range(N_KV_HEADS):
        sl = slice(c * LANES, (c + 1) * LANES)
        v_ref[0, sl, :] = jnp.where(row >= HEAD_DIM, 1.0, ht[qw + c * LANES: qw + (c + 1) * LANES]).astype(BF16)

    cos = cos_ref[...]
    sin = sin_ref[...]
    lane = lax.broadcasted_iota(jnp.int32, cos.shape, 1)
    first_half = (lane % ROPE_HALF) < ROPE_M
    for c in range(N_KV_HEADS):
        sl = slice(c * LANES, (c + 1) * LANES)
        xc = h[:, sl]
        hms = jnp.sum(xc * xc, axis=-1, keepdims=True) * (1.0 / HEAD_DIM)
        xc = xc * lax.rsqrt(hms + QK_EPS) * kg_ref[...]
        partner = jnp.where(first_half, pltpu.roll(xc, LANES - ROPE_M, 1), pltpu.roll(xc, ROPE_M, 1))
        k_ref[:, sl] = (xc * cos + partner * sin).astype(BF16)
    lx_ref[...] = h[:, kw: kw + lru_w]
    lg_ref[...] = h[:, kw + lru_w: kw + 2 * lru_w]


def _pad_heads(w, n_heads):
    lead = w.shape[:-1]
    w = w.reshape(lead + (n_heads, HEAD_DIM))
    w = jnp.pad(w, [(0, 0)] * len(lead) + [(0, 0), (0, LANES - HEAD_DIM)])
    return w.reshape(lead + (n_heads * LANES,))


def _rope_tables(S):
    t = np.arange(S)
    rows = (t // GRID_W).astype(np.float32)
    cols = (t % GRID_W).astype(np.float32)
    inv_freq = (ROPE_THETA ** (-np.arange(ROPE_M, dtype=np.float32) / ROPE_M)).astype(np.float32)
    ar = rows[:, None] * inv_freq[None, :]
    ac = cols[:, None] * inv_freq[None, :]
    cos = np.concatenate([np.cos(ar), np.cos(ar), np.cos(ac), np.cos(ac)], axis=-1)
    sin = np.concatenate([-np.sin(ar), np.sin(ar), -np.sin(ac), np.sin(ac)], axis=-1)
    pad = [(0, 0), (0, LANES - HEAD_DIM)]
    cos = np.pad(cos, pad).astype(np.float32)
    sin = np.pad(sin, pad).astype(np.float32)
    return cos, sin, np.ascontiguousarray(cos.T), np.ascontiguousarray(sin.T)


def _inproj(x2, norm1_g, w_in, q_norm_g, k_norm_g, S):
    T, D = x2.shape
    lru_w = (w_in.shape[1] - ATTN_W - 2 * KV_W) // 2
    o0, o1, o2 = ATTN_W, ATTN_W + KV_W, ATTN_W + 2 * KV_W
    w_t = jnp.concatenate([_pad_heads(w_in[:, :o0], N_Q_HEADS),
                           _pad_heads(w_in[:, o1:o2], N_KV_HEADS)], axis=1).T.astype(BF16)
    w_rest = jnp.concatenate([_pad_heads(w_in[:, o0:o1], N_KV_HEADS), w_in[:, o2:]],
                             axis=1).astype(BF16)
    qg = _pad_heads(q_norm_g.reshape(1, HEAD_DIM), 1).reshape(LANES, 1)
    kg = _pad_heads(k_norm_g.reshape(1, HEAD_DIM), 1)
    cos, sin, cos_t, sin_t = _rope_tables(S)
    ts = TS_IN
    n_s = S // ts
    qw, kw = N_Q_HEADS * LANES, N_KV_HEADS * LANES
    const = lambda i: (0, 0)
    tok = lambda i: (i, 0)
    pos = lambda i: (i % n_s, 0)
    pos_t = lambda i: (0, i % n_s)
    tposed = lambda i: (i // n_s, 0, i % n_s)
    return pl.pallas_call(
        functools.partial(_inproj_kernel, lru_w=lru_w),
        grid=(T // ts,),
        in_specs=[
            pl.BlockSpec((ts, D), tok),
            pl.BlockSpec((1, D), const),
            pl.BlockSpec(w_t.shape, const),
            pl.BlockSpec(w_rest.shape, const),
            pl.BlockSpec((LANES, 1), const),
            pl.BlockSpec((1, LANES), const),
            pl.BlockSpec((ts, LANES), pos),
            pl.BlockSpec((ts, LANES), pos),
            pl.BlockSpec((LANES, ts), pos_t),
            pl.BlockSpec((LANES, ts), pos_t),
        ],
        out_specs=[
            pl.BlockSpec((1, qw, ts), tposed),
            pl.BlockSpec((ts, kw), tok),
            pl.BlockSpec((1, kw, ts), tposed),
            pl.BlockSpec((ts, lru_w), tok),
            pl.BlockSpec((ts, lru_w), tok),
        ],
        out_shape=[
            jax.ShapeDtypeStruct((T // S, qw, S), BF16),
            jax.ShapeDtypeStruct((T, kw), BF16),
            jax.ShapeDtypeStruct((T // S, kw, S), BF16),
            jax.ShapeDtypeStruct((T, lru_w), F32),
            jax.ShapeDtypeStruct((T, lru_w), F32),
        ],
        compiler_params=_cparams(("parallel",)),
        name="inproj",
    )(x2, norm1_g.reshape(1, D), w_t, w_rest, qg, kg, cos, sin, cos_t, sin_t)


def _attn_kernel(qt_ref, k_ref, vt_ref, o_ref, acc_ref, s_ref, p_ref, *, tq, tk, n_kv, kv_unroll):
    hp = HEADS_PER_STEP
    spt = GQA_GROUP // hp
    acc_ref[...] = jnp.zeros(acc_ref.shape, F32)

    def scores(j, sp):
        kt = k_ref[0, pl.ds(pl.multiple_of(j * tk, tk), tk), :]
        out = []
        for u in range(hp):
            g = sp * hp + u
            s = jnp.dot(kt, qt_ref[0, g * LANES:(g + 1) * LANES, :], preferred_element_type=F32)
            out.append((s, jnp.max(s, axis=0, keepdims=True)))
        return out

    def softmax_stage(sc, ms, sp):
        out = []
        for u, (s, s_max) in enumerate(sc):
            h = sp * hp + u
            m_new = jnp.maximum(ms[h], s_max)
            out.append((jnp.exp2(ms[h] - m_new), jnp.exp2(s - m_new).astype(BF16)))
            ms[h] = m_new
        return out

    def pv_stage(j, sp, ap):
        vt = vt_ref[0, 0:PV_ROWS, pl.ds(pl.multiple_of(j * tk, tk), tk)]
        for u, (alpha, p) in enumerate(ap):
            g = sp * hp + u
            acc_ref[g] = alpha * acc_ref[g] + jnp.dot(vt, p, preferred_element_type=F32)

    ms = [jnp.full((1, tq), -jnp.inf, F32)] * GQA_GROUP
    ap = softmax_stage(scores(0, 0), ms, 0)
    sc = scores(min(1 // spt, n_kv - 1), 1 % spt)
    for u in range(hp):
        s_ref[u] = sc[u][0]
        p_ref[u] = ap[u][1]

    def body(it, carry):
        ms = list(carry[:GQA_GROUP])
        ap = [(carry[GQA_GROUP + u], p_ref[u]) for u in range(hp)]
        sc = [(s_ref[u], carry[GQA_GROUP + hp + u]) for u in range(hp)]
        for n in range(kv_unroll * spt):
            j = it * kv_unroll + n // spt
            j_next = jnp.minimum(it * kv_unroll + (n + 2) // spt, n_kv - 1)
            sc_next = scores(j_next, (n + 2) % spt)
            ap_next = softmax_stage(sc, ms, (n + 1) % spt)
            pv_stage(j, n % spt, ap)
            sc, ap = sc_next, ap_next
        for u in range(hp):
            s_ref[u] = sc[u][0]
            p_ref[u] = ap[u][1]
        return tuple(ms) + tuple(a for a, _ in ap) + tuple(m for _, m in sc)

    lax.fori_loop(0, n_kv // kv_unroll, body,
                  tuple(ms) + tuple(a for a, _ in ap) + tuple(m for _, m in sc))
    _attn_finalize(acc_ref, o_ref, tq)


def _attn_finalize(acc_ref, o_ref, tq):
    pad = jnp.zeros((LANES - HEAD_DIM, tq), F32)
    for g in range(GQA_GROUP):
        acc = acc_ref[g]
        o = acc[0:HEAD_DIM] / acc[HEAD_DIM:HEAD_DIM + 1, :]
        o_ref[0, :, g * LANES:(g + 1) * LANES] = jnp.concatenate([o, pad], axis=0).T.astype(BF16)


def _attn_bounded_kernel(qt_ref, k_ref, vt_ref, o_ref, acc_ref, s_ref, p_ref, *, tq, tk, n_kv, kv_unroll):
    hp = HEADS_PER_STEP
    spt = GQA_GROUP // hp
    acc_ref[...] = jnp.zeros(acc_ref.shape, F32)

    def scores(j, sp):
        kt = k_ref[0, pl.ds(pl.multiple_of(j * tk, tk), tk), :]
        return [jnp.dot(kt, qt_ref[0, (sp * hp + u) * LANES:(sp * hp + u + 1) * LANES, :],
                        preferred_element_type=F32) for u in range(hp)]

    def probs(sc):
        return [jnp.exp2(s).astype(BF16) for s in sc]

    def pv_stage(j, sp, ps):
        vt = vt_ref[0, 0:PV_ROWS, pl.ds(pl.multiple_of(j * tk, tk), tk)]
        for u, p in enumerate(ps):
            acc_ref[sp * hp + u] += jnp.dot(vt, p, preferred_element_type=F32)

    ps = probs(scores(0, 0))
    sc = scores(min(1 // spt, n_kv - 1), 1 % spt)
    for u in range(hp):
        s_ref[u] = sc[u]
        p_ref[u] = ps[u]

    def body(it, carry):
        ps = [p_ref[u] for u in range(hp)]
        sc = [s_ref[u] for u in range(hp)]
        for n in range(kv_unroll * spt):
            j = it * kv_unroll + n // spt
            j_next = jnp.minimum(it * kv_unroll + (n + 2) // spt, n_kv - 1)
            sc_next = scores(j_next, (n + 2) % spt)
            ps_next = probs(sc)
            pv_stage(j, n % spt, ps)
            sc, ps = sc_next, ps_next
        for u in range(hp):
            s_ref[u] = sc[u]
            p_ref[u] = ps[u]
        return carry

    lax.fori_loop(0, n_kv // kv_unroll, body, 0)
    _attn_finalize(acc_ref, o_ref, tq)


def _attention(qt, k, vt, *, kernel, B, S):
    tq = min(TQ, S)
    tk = min(TK, S)
    gw = GQA_GROUP * LANES
    return pl.pallas_call(
        functools.partial(kernel, tq=tq, tk=tk, n_kv=S // tk,
                          kv_unroll=math.gcd(S // tk, KV_UNROLL)),
        grid=(B, N_KV_HEADS, S // tq),
        in_specs=[
            pl.BlockSpec((1, gw, tq), lambda b, h, i: (b, h, i)),
            pl.BlockSpec((1, S, LANES), lambda b, h, i: (b, 0, h)),
            pl.BlockSpec((1, LANES, S), lambda b, h, i: (b, h, 0)),
        ],
        out_specs=pl.BlockSpec((1, tq, gw), lambda b, h, i: (b, i, h)),
        out_shape=jax.ShapeDtypeStruct((B, S, N_Q_HEADS * LANES), BF16),
        scratch_shapes=[pltpu.VMEM((GQA_GROUP, PV_ROWS, tq), F32),
                        pltpu.VMEM((HEADS_PER_STEP, tk, tq), F32),
                        pltpu.VMEM((HEADS_PER_STEP, tk, tq), BF16)],
        compiler_params=_cparams(("parallel", "parallel", "parallel")),
        name=kernel.__name__.strip("_"),
    )(qt, k, vt)


def _scan_chunk(a, b, h_in, reverse):
    n = a.shape[0]
    n_groups = n // SUBLANES
    a = a.reshape(n_groups, SUBLANES, LANES)
    b = b.reshape(n_groups, SUBLANES, LANES)
    sub = lax.broadcasted_iota(jnp.int32, a.shape, 1)
    d = 1
    while d < SUBLANES:
        if reverse:
            keep = sub < SUBLANES - d
            shift = SUBLANES - d
        else:
            keep = sub >= d
            shift = d
        a_sh = jnp.where(keep, pltpu.roll(a, shift, 1), 1.0)
        b_sh = jnp.where(keep, pltpu.roll(b, shift, 1), 0.0)
        b = a * b_sh + b
        a = a * a_sh
        d *= 2
    a = a.reshape(n, LANES)
    b = b.reshape(n, LANES)
    order = range(n_groups - 1, -1, -1) if reverse else range(n_groups)
    edge = h_in
    out = [None] * n_groups
    for v in order:
        rows = slice(v * SUBLANES, (v + 1) * SUBLANES)
        hv = b[rows] + a[rows] * jnp.broadcast_to(edge, (SUBLANES, LANES))
        out[v] = hv
        edge = hv[0:1] if reverse else hv[SUBLANES - 1:SUBLANES]
    return jnp.concatenate(out, axis=0), edge


def _lru_pad_input(u_ref, up_ref, S):
    zeros = jnp.zeros((SUBLANES, LANES), F32)
    up_ref[0:SUBLANES, :] = zeros
    up_ref[S + SUBLANES:S + 2 * SUBLANES, :] = zeros
    up_ref[SUBLANES:S + SUBLANES, :] = u_ref[0]


def _lru_gates(up_ref, cw_ref, cb_ref, w_ref, bias_ref, lam_ref, t0, tc, d):
    cw = cw_ref[...]
    xc = cb_ref[...]
    for j in range(CONV_W):
        xc = xc + up_ref[pl.ds(t0 + SUBLANES + j - CONV_PAD_L, tc), :] * cw[j:j + 1, :]
    gw = 2 * LANES
    g = jnp.dot(xc.astype(BF16), w_ref[0, :, d * gw:(d + 1) * gw],
                preferred_element_type=F32) + bias_ref[0, :, d * gw:(d + 1) * gw]
    r = jax.nn.sigmoid(g[:, :LANES])
    i = jax.nn.sigmoid(g[:, LANES:])
    a = jnp.exp(-LRU_C * r * jax.nn.softplus(-lam_ref[d:d + 1, :]))
    y = 1.0 - a * a
    b = jnp.where(y > 0.0, y * lax.rsqrt(y), 0.0) * i * xc
    return a, b


def _lru_kernel(u_ref, gate_ref, cw_ref, cb_ref, w_ref, bias_ref, lam_ref, o_ref,
                up_ref, hf_ref, *, S, tc):
    _lru_pad_input(u_ref, up_ref, S)
    n_chunks = S // tc
    params = (up_ref, cw_ref, cb_ref, w_ref, bias_ref, lam_ref)

    def fwd(c, h):
        t0 = pl.multiple_of(c * tc, tc)
        hc, h_last = _scan_chunk(*_lru_gates(*params, t0, tc, 0), h, False)
        hf_ref[pl.ds(t0, tc), :] = hc
        return h_last

    lax.fori_loop(0, n_chunks, fwd, jnp.zeros((1, LANES), F32))

    def bwd(ci, h):
        t0 = pl.multiple_of((n_chunks - 1 - ci) * tc, tc)
        hc, h_last = _scan_chunk(*_lru_gates(*params, t0, tc, 1), h, True)
        gate = gate_ref[0, pl.ds(t0, tc), :]
        o_ref[0, pl.ds(t0, tc), :] = (hf_ref[pl.ds(t0, tc), :] + hc) * jax.nn.gelu(gate)
        return h_last

    lax.fori_loop(0, n_chunks, bwd, jnp.zeros((1, LANES), F32))


def _block_diag_pairs(w):
    nb, bw, _ = w.shape
    w = w.reshape(nb // 2, 2, bw, bw)
    z = jnp.zeros_like(w[:, 0])
    top = jnp.concatenate([w[:, 0], z], axis=-1)
    bot = jnp.concatenate([z, w[:, 1]], axis=-1)
    return jnp.concatenate([top, bot], axis=-2)


def _lru_operands(conv_w, conv_b, wa, ba, wi, bi, lam):
    C = conv_b.shape[0]
    nc = C // LANES
    w = jnp.concatenate([_block_diag_pairs(wa[0]), _block_diag_pairs(wi[0]),
                         _block_diag_pairs(wa[1]), _block_diag_pairs(wi[1])], axis=-1).astype(BF16)
    bias = jnp.stack([ba[0].reshape(nc, LANES), bi[0].reshape(nc, LANES),
                      ba[1].reshape(nc, LANES), bi[1].reshape(nc, LANES)], axis=1)
    return conv_w, conv_b.reshape(1, C), w, bias.reshape(nc, 1, 4 * LANES), lam


def _lru_specs(S, unit):
    seq = lambda *g: (unit(*g)[0], 0, unit(*g)[1])
    chan = lambda *g: (0, unit(*g)[1])
    blk = lambda *g: (unit(*g)[1], 0, 0)
    in_specs = [
        pl.BlockSpec((1, S, LANES), seq),
        pl.BlockSpec((1, S, LANES), seq),
        pl.BlockSpec((CONV_W, LANES), chan),
        pl.BlockSpec((1, LANES), chan),
        pl.BlockSpec((1, LANES, 4 * LANES), blk),
        pl.BlockSpec((1, 1, 4 * LANES), blk),
        pl.BlockSpec((2, LANES), chan),
    ]
    return in_specs, pl.BlockSpec((1, S, LANES), seq)


def _lru(lru_x, lru_gate, lru_ops, B, S):
    C = lru_x.shape[-1]
    tc = min(TC_LRU, S)
    in_specs, out_spec = _lru_specs(S, lambda b, c: (b, c))
    return pl.pallas_call(
        functools.partial(_lru_kernel, S=S, tc=tc),
        grid=(B, C // LANES),
        in_specs=in_specs,
        out_specs=out_spec,
        out_shape=jax.ShapeDtypeStruct((B, S, C), F32),
        scratch_shapes=[
            pltpu.VMEM((S + 2 * SUBLANES, LANES), F32),
            pltpu.VMEM((S, LANES), F32),
        ],
        compiler_params=_cparams(("parallel", "parallel")),
        name="rglru",
    )(lru_x, lru_gate, *lru_ops)


def _mixers(qt, k, vt, lru_x, lru_gate, lru_ops, score_bound, B, S):
    attn = lax.cond(score_bound <= SAFE_SCORE_LOG2,
                    functools.partial(_attention, kernel=_attn_bounded_kernel, B=B, S=S),
                    functools.partial(_attention, kernel=_attn_kernel, B=B, S=S), qt, k, vt)
    return attn, _lru(lru_x, lru_gate, lru_ops, B, S)


def _rows_to_slabs(ref, x):
    n = x.shape[0]
    for s in range(SUBLANES):
        ref[pl.ds(s, n, stride=SUBLANES), :] = x[:, s * LANES:(s + 1) * LANES]


def _slabs_to_rows(ref, n):
    return jnp.concatenate([ref[pl.ds(s, n, stride=SUBLANES), :] for s in range(SUBLANES)], axis=1)


def _slab(ref, r):
    return ref.at[pl.ds(pl.multiple_of(r * SUBLANES, SUBLANES), SUBLANES)]


def _outproj_kernel(a_ref, l_ref, x_ref, ag_ref, lg_ref, wa_ref, wl_ref, g2_ref,
                    wr_ref, br_ref, tri_ref,
                    x1_ref, xn3_ref, route_ref, gates_ref, cnt_ref, carry_ref, *, attn_w, lru_w):
    step = pl.program_id(0)

    @pl.when(step == 0)
    def _():
        carry_ref[...] = jnp.zeros_like(carry_ref)

    a = a_ref[...].astype(F32)
    ams = jnp.sum(a * a, axis=-1, keepdims=True) * (1.0 / attn_w)
    an = a * lax.rsqrt(ams + NORM_EPS) * ag_ref[...]
    l = l_ref[...]
    lms = jnp.sum(l * l, axis=-1, keepdims=True) * (1.0 / lru_w)
    ln = l * lax.rsqrt(lms + NORM_EPS) * lg_ref[...]
    mix = (jnp.dot(an.astype(BF16), wa_ref[...], preferred_element_type=F32)
           + jnp.dot(ln.astype(BF16), wl_ref[...], preferred_element_type=F32))
    x1 = x_ref[...] + mix
    x1_ref[...] = x1
    ms = jnp.mean(x1 * x1, axis=-1, keepdims=True)
    xn = x1 * lax.rsqrt(ms + NORM_EPS) * g2_ref[...]
    _rows_to_slabs(xn3_ref, xn)

    logits = jnp.dot(xn.astype(BF16), wr_ref[...], preferred_element_type=F32) + br_ref[...]
    lane = lax.broadcasted_iota(jnp.int32, logits.shape, 1)
    neg = -jnp.inf
    work = jnp.where(lane < N_EXPERTS, logits, neg)
    sel = jnp.zeros(logits.shape, F32)
    idxs, vals = [], []
    for _ in range(TOP_K):
        m = jnp.max(work, axis=1, keepdims=True)
        idx = jnp.min(jnp.where(work == m, lane, LANES), axis=1, keepdims=True)
        hit = lane == idx
        work = jnp.where(hit, neg, work)
        sel = sel + hit.astype(F32)
        idxs.append(idx)
        vals.append(m)
    es = [jnp.exp(v - vals[0]) for v in vals]
    den = es[0] + es[1] + es[2] + es[3]

    prefix = jnp.dot(tri_ref[...], sel.astype(BF16), preferred_element_type=F32) + carry_ref[...]
    carry_ref[...] = carry_ref[...] + jnp.sum(sel, axis=0, keepdims=True)
    cnt_ref[...] = carry_ref[...]

    route = jnp.zeros(logits.shape, jnp.int32)
    gates = jnp.zeros(logits.shape, F32)
    for k in range(TOP_K):
        rank = jnp.sum(jnp.where(lane == idxs[k], prefix, 0.0), axis=1, keepdims=True).astype(jnp.int32)
        route = jnp.where(lane == k, idxs[k], route)
        route = jnp.where(lane == TOP_K + k, rank, route)
        gates = jnp.where(lane == k, es[k] / den, gates)
    route_ref[...] = route
    gates_ref[...] = gates


def _outproj_router(attn, lru, x2, attn_out_g, lru_out_g, w_out, norm2_g, w_router, b_router):
    T, D = x2.shape
    lru_w = lru.shape[-1]
    ts = min(TS_OUT, T)
    wa = w_out[:ATTN_W].reshape(N_Q_HEADS, HEAD_DIM, D)
    wa = jnp.pad(wa, ((0, 0), (0, LANES - HEAD_DIM), (0, 0))).reshape(N_Q_HEADS * LANES, D).astype(BF16)
    wl = w_out[ATTN_W:].astype(BF16)
    ag = _pad_heads(attn_out_g.reshape(1, ATTN_W), N_Q_HEADS)
    wr = jnp.pad(w_router, ((0, 0), (0, LANES - N_EXPERTS))).astype(BF16)
    br =jnp.pad(b_router.reshape(1, N_EXPERTS), ((0, 0), (0, LANES - N_EXPERTS)))
    tri = (jnp.arange(ts)[:, None] > jnp.arange(ts)[None, :]).astype(BF16)
    const = lambda i: (0, 0)
    tok = lambda i: (i, 0)
    aw = N_Q_HEADS * LANES
    return pl.pallas_call(
        functools.partial(_outproj_kernel, attn_w=ATTN_W, lru_w=lru_w),
        grid=(T // ts,),
        in_specs=[
            pl.BlockSpec((ts, aw), tok),
            pl.BlockSpec((ts, lru_w), tok),
            pl.BlockSpec((ts, D), tok),
            pl.BlockSpec((1, aw), const),
            pl.BlockSpec((1, lru_w), const),
            pl.BlockSpec((aw, D), const),
            pl.BlockSpec((lru_w, D), const),
            pl.BlockSpec((1, D), const),
            pl.BlockSpec((D, LANES), const),
            pl.BlockSpec((1, LANES), const),
            pl.BlockSpec((ts, ts), const),
        ],
        out_specs=[
            pl.BlockSpec((ts, D), tok),
            pl.BlockSpec((ts * SUBLANES, LANES), tok),
            pl.BlockSpec((ts, LANES), tok),
            pl.BlockSpec((ts, LANES), tok),
            pl.BlockSpec((1, LANES), const),
        ],
        out_shape=[
            jax.ShapeDtypeStruct((T, D), F32),
            jax.ShapeDtypeStruct((T * SUBLANES, LANES), F32),
            jax.ShapeDtypeStruct((T, LANES), jnp.int32),
            jax.ShapeDtypeStruct((T, LANES), F32),
            jax.ShapeDtypeStruct((1, LANES), F32),
        ],
        scratch_shapes=[pltpu.VMEM((1, LANES), F32)],
        compiler_params=_cparams(("arbitrary",)),
        name="outproj_router",
    )(attn, lru, x2, ag, lru_out_g.reshape(1, lru_w), wa, wl, norm2_g.reshape(1, D),
      wr, br, tri)


def _plan_kernel(cnt_ref, pstart_ref, plan_ref):
    cnt = cnt_ref[...]
    lane = lax.broadcasted_iota(jnp.int32, cnt.shape, 1)
    padded = jnp.floor((cnt + (ROW_BLOCK - 1)) * (1.0 / ROW_BLOCK)) * ROW_BLOCK
    pend = padded
    d = 1
    while d < N_EXPERTS:
        pend = pend + jnp.where(lane >= d, pltpu.roll(pend, d, 1), 0.0)
        d *= 2
    pstart_ref[...] = pend - padded
    total = jnp.max(pend, axis=1, keepdims=True)

    shape = plan_ref.shape
    lanes = lax.broadcasted_iota(jnp.int32, shape, 1)
    is_expert = lanes < N_EXPERTS
    start = lax.broadcasted_iota(jnp.int32, shape, 0).astype(F32) * ROW_BLOCK

    def groups_ending_by(row):
        return jnp.sum(jnp.where(jnp.logical_and(pend <= row, is_expert), 1.0, 0.0), axis=1, keepdims=True)

    block_e = jnp.minimum(groups_ending_by(start), N_EXPERTS - 1.0)
    tail = jnp.max(jnp.where(jnp.logical_and(jnp.logical_and(pend == start + ROW_BLOCK, padded > 0.0),
                                             is_expert), 1.0, 0.0), axis=1, keepdims=True)
    fill = jnp.maximum(tail, jnp.where(start[:, 0:1] >= total, 1.0, 0.0))
    group_end = jnp.sum(jnp.where(lanes.astype(F32) == block_e, pend, 0.0), axis=1, keepdims=True)
    next_e = jnp.where(group_end < total,
                       jnp.minimum(groups_ending_by(group_end), N_EXPERTS - 1.0), -1.0)
    plan = jnp.where(lanes == 0, block_e,
                     jnp.where(lanes == 1, fill,
                               jnp.where(lanes == 2, next_e, total * (1.0 / ROW_BLOCK))))
    plan_ref[...] = plan.astype(jnp.int32)


def _routing_plan(cnt, n_blocks):
    assert ROW_BLOCK & (ROW_BLOCK - 1) == 0, "exact f32 division by the row block size"
    rows = -(-n_blocks // SUBLANES) * SUBLANES
    pstart, plan = pl.pallas_call(
        _plan_kernel,
        out_shape=[jax.ShapeDtypeStruct((1, LANES), F32),
                   jax.ShapeDtypeStruct((rows, LANES), jnp.int32)],
        name="routing_plan",
    )(cnt)
    return pstart, plan[:n_blocks, 0], plan[:n_blocks, 1], plan[:n_blocks, 2], plan[0:1, 3]


def _dest_kernel(route_ref, pstart_ref, dest_ref):
    route = route_ref[...]
    lane = lax.broadcasted_iota(jnp.int32, route.shape, 1)
    pstart = pstart_ref[...]
    dest = jnp.zeros(route.shape, jnp.int32)
    for k in range(TOP_K):
        start = jnp.sum(jnp.where(lane == route[:, k:k + 1], pstart, 0.0), axis=1, keepdims=True)
        dest = jnp.where(lane == k, start.astype(jnp.int32) + route[:, TOP_K + k:TOP_K + k + 1], dest)
    dest_ref[...] = dest


def _dest_rows(route, pstart):
    T = route.shape[0]
    ts = math.gcd(TS_DEST, T)
    dest = pl.pallas_call(
        _dest_kernel,
        grid=(T // ts,),
        in_specs=[pl.BlockSpec((ts, LANES), lambda i: (i, 0)),
                  pl.BlockSpec((1, LANES), lambda i: (0, 0))],
        out_specs=pl.BlockSpec((ts, LANES), lambda i: (i, 0)),
        out_shape=jax.ShapeDtypeStruct((T, LANES), jnp.int32),
        compiler_params=_cparams(("parallel",)),
        name="dest_rows",
    )(route, pstart)
    return dest[:, :TOP_K].reshape(T * TOP_K)


def _dispatch_kernel(fill_ref, dest_ref, x_ref, out_hbm, zero_ref, sem, zero_sem, *, ts, n_blocks):
    block_slabs = ROW_BLOCK * SUBLANES

    def fill_copy(b):
        off = pl.multiple_of(b * block_slabs, block_slabs)
        return pltpu.make_async_copy(zero_ref, out_hbm.at[pl.ds(off, block_slabs)], zero_sem)

    @pl.when(pl.program_id(0) == 0)
    def _():
        zero_ref[...] = jnp.zeros(zero_ref.shape, F32)

        def start(b, carry):
            @pl.when(fill_ref[b] != 0)
            def _():
                fill_copy(b).start()
            return carry

        def wait(b, carry):
            @pl.when(fill_ref[b] != 0)
            def _():
                fill_copy(b).wait()
            return carry

        lax.fori_loop(0, n_blocks, start, 0)
        lax.fori_loop(0, n_blocks, wait, 0)

    def issue(i, carry):
        for j in range(ISSUE_UNROLL):
            r = i * ISSUE_UNROLL + j
            for k in range(TOP_K):
                d = dest_ref[r * TOP_K + k]
                pltpu.make_async_copy(_slab(x_ref, r), _slab(out_hbm, d), sem).start(priority=k % 2)
        return carry

    lax.fori_loop(0, ts // ISSUE_UNROLL, issue, 0)
    for k in range(TOP_K):
        pltpu.make_async_copy(x_ref, out_hbm.at[pl.ds(0, ts * SUBLANES)], sem).wait()


def _dispatch(xn_slabs, fill, dest_flat, n_rows):
    T = xn_slabs.shape[0] // SUBLANES
    ts = min(TS_DISP, T)
    grid_spec = pltpu.PrefetchScalarGridSpec(
        num_scalar_prefetch=1,
        grid=(T // ts,),
        in_specs=[
            pl.BlockSpec((ts * TOP_K,), lambda i, fl: (i,), memory_space=pltpu.SMEM),
            pl.BlockSpec((ts * SUBLANES, LANES), lambda i, fl: (i, 0)),
        ],
        out_specs=pl.BlockSpec(memory_space=pl.ANY),
        scratch_shapes=[pltpu.VMEM((ROW_BLOCK * SUBLANES, LANES), F32),
                        pltpu.SemaphoreType.DMA, pltpu.SemaphoreType.DMA],
    )
    return pl.pallas_call(
        functools.partial(_dispatch_kernel, ts=ts, n_blocks=n_rows // ROW_BLOCK),
        grid_spec=grid_spec,
        out_shape=jax.ShapeDtypeStruct((n_rows * SUBLANES, LANES), xn_slabs.dtype),
        compiler_params=_cparams(("arbitrary",)),
        name="dispatch",
    )(fill, dest_flat, xn_slabs)


def _expert_loop_kernel(be_ref, na_ref, nxt_ref, x_hbm, wg_hbm, bg_ref, wu_hbm, bu_ref, wd_hbm, bd_ref,
                        y_hbm, xbuf, ybuf, stage_ref, wb_ref, x_sems, y_sems, w_sems):
    w_hbm = (wg_hbm, wu_hbm, wd_hbm)
    block_slabs = ROW_BLOCK * SUBLANES
    n_active = na_ref[0]

    def rows(b):
        return pl.ds(pl.multiple_of(b * block_slabs, block_slabs), block_slabs)

    def x_copy(b, s):
        return pltpu.make_async_copy(x_hbm.at[rows(b)], xbuf.at[s], x_sems.at[s])

    def y_copy(b, s):
        return pltpu.make_async_copy(ybuf.at[s], y_hbm.at[rows(b)], y_sems.at[s])

    def fetch(expert, s, m):
        return pltpu.make_async_copy(w_hbm[m].at[expert], stage_ref.at[s, m], w_sems.at[s, m])

    x_copy(0, 0).start()
    for m in range(3):
        fetch(be_ref[0], 0, m).start()

    def body(b, wslot):
        s = b % 2
        e = be_ref[b]
        x_copy(b, s).wait()

        @pl.when(b + 1 < n_active)
        def _():
            x_copy(b + 1, 1 - s).start()

        first = jnp.logical_or(b == 0, e != be_ref[jnp.maximum(b - 1, 0)])

        @pl.when(first)
        def _():
            for m in range(3):
                fetch(e, wslot, m).wait()
                wb_ref[m] = stage_ref[wslot, m].astype(BF16)

            @pl.when(nxt_ref[b] >= 0)
            def _():
                for m in range(3):
                    fetch(nxt_ref[b], 1 - wslot, m).start()

        @pl.when(b >= 2)
        def _():
            y_copy(b - 2, s).wait()

        x = _slabs_to_rows(xbuf.at[s], ROW_BLOCK).astype(BF16)
        g = jnp.dot(x, wb_ref[0], preferred_element_type=F32) + bg_ref[e]
        u = jnp.dot(x, wb_ref[1], preferred_element_type=F32) + bu_ref[e]
        g = jnp.minimum(g, SWIGLU_LIMIT)
        u = jnp.clip(u, -SWIGLU_LIMIT, SWIGLU_LIMIT)
        glu = g * jax.nn.sigmoid(SWIGLU_ALPHA * g)
        y = jnp.dot(((u + 1.0) * glu).astype(BF16), wb_ref[2], preferred_element_type=F32) + bd_ref[e]
        _rows_to_slabs(ybuf.at[s], y)
        y_copy(b, s).start()
        return jnp.where(first, 1 - wslot, wslot)

    lax.fori_loop(0, n_active, body, 0)

    @pl.when(n_active >= 2)
    def _():
        y_copy(n_active - 2, n_active % 2).wait()

    y_copy(n_active - 1, (n_active - 1) % 2).wait()


def _experts(x_rows, block_e, n_active, next_e, w_gate, b_gate, w_up, b_up, w_down, b_down):
    E, D, FF = w_gate.shape
    assert D == FF, "the three expert matrices share one staging shape"
    block_slabs = ROW_BLOCK * SUBLANES
    whole = lambda i, be, na, nx: (0, 0, 0)

    grid_spec = pltpu.PrefetchScalarGridSpec(
        num_scalar_prefetch=3,
        grid=(1,),
        in_specs=[
            pl.BlockSpec(memory_space=pl.ANY),
            pl.BlockSpec(memory_space=pl.ANY),
            pl.BlockSpec((E, 1, FF), whole),
            pl.BlockSpec(memory_space=pl.ANY),
            pl.BlockSpec((E, 1, FF), whole),
            pl.BlockSpec(memory_space=pl.ANY),
            pl.BlockSpec((E, 1, D), whole),
        ],
        out_specs=pl.BlockSpec(memory_space=pl.ANY),
        scratch_shapes=[
            pltpu.VMEM((2, block_slabs, LANES), F32),
            pltpu.VMEM((2, block_slabs, LANES), F32),
            pltpu.VMEM((2, 3, D, FF), F32),
            pltpu.VMEM((3, D, FF), BF16),
            pltpu.SemaphoreType.DMA((2,)),
            pltpu.SemaphoreType.DMA((2,)),
            pltpu.SemaphoreType.DMA((2, 3)),
        ],
    )
    return pl.pallas_call(
        _expert_loop_kernel,
        grid_spec=grid_spec,
        out_shape=jax.ShapeDtypeStruct(x_rows.shape, F32),
        input_output_aliases={3: 0},
        compiler_params=pltpu.CompilerParams(dimension_semantics=("arbitrary",),
                                             vmem_limit_bytes=EXPERT_VMEM_LIMIT),
        name="experts",
    )(block_e, n_active, next_e, x_rows, w_gate, b_gate.reshape(E, 1, FF), w_up,
      b_up.reshape(E, 1, FF), w_down, b_down.reshape(E, 1, D))


def _combine_kernel(dest_ref, dest_next_ref, y_hbm, x1_ref, gates_ref, fg_ref, o_ref, bufs, sems,
                    *, ts, n_steps):
    i = pl.program_id(0)
    slot = i % 2

    def gather_tile(d_ref, s):
        def issue(it, carry):
            for j in range(ISSUE_UNROLL):
                r = it * ISSUE_UNROLL + j
                for k in range(TOP_K):
                    d = d_ref[r * TOP_K + k]
                    pltpu.make_async_copy(_slab(y_hbm, d), _slab(bufs.at[s, k], r),
                                          sems.at[s]).start(priority=k % 2)
            return carry

        lax.fori_loop(0, ts // ISSUE_UNROLL, issue, 0)

    @pl.when(i == 0)
    def _():
        gather_tile(dest_ref, 0)

    @pl.when(i + 1 < n_steps)
    def _():
        gather_tile(dest_next_ref, 1 - slot)

    for k in range(TOP_K):
        pltpu.make_async_copy(y_hbm.at[pl.ds(0, ts * SUBLANES)], bufs.at[slot, k], sems.at[slot]).wait()

    acc = x1_ref[...]
    gates = gates_ref[...]
    for k in range(TOP_K):
        acc = acc + _slabs_to_rows(bufs.at[slot, k], ts) * gates[:, k:k + 1]
    ms = jnp.mean(acc * acc, axis=-1, keepdims=True)
    o_ref[...] = acc * lax.rsqrt(ms + NORM_EPS) * fg_ref[...]


def _combine(y_rows, dest_flat, x1, gates, final_g):
    T, D = x1.shape
    ts = min(TS_COMB, T)
    n_steps = T // ts
    tok = lambda i: (i, 0)
    return pl.pallas_call(
        functools.partial(_combine_kernel, ts=ts, n_steps=n_steps),
        grid=(n_steps,),
        in_specs=[
            pl.BlockSpec((ts * TOP_K,), lambda i: (i,), memory_space=pltpu.SMEM),
            pl.BlockSpec((ts * TOP_K,), lambda i: (jnp.minimum(i + 1, n_steps - 1),),
                         memory_space=pltpu.SMEM),
            pl.BlockSpec(memory_space=pl.ANY),
            pl.BlockSpec((ts, D), tok),
            pl.BlockSpec((ts, LANES), tok),
            pl.BlockSpec((1, D), lambda i: (0, 0)),
        ],
        out_specs=pl.BlockSpec((ts, D), tok),
        out_shape=jax.ShapeDtypeStruct((T, D), F32),
        scratch_shapes=[pltpu.VMEM((2, TOP_K, ts * SUBLANES, LANES), F32),
                        pltpu.SemaphoreType.DMA((2,))],
        compiler_params=_cparams(("arbitrary",)),
        name="combine",
    )(dest_flat, dest_flat, y_rows, x1, gates, final_g.reshape(1, D))


def kernel(x, norm1_g, w_in, q_norm_g, k_norm_g, conv_w, conv_b, lru_wa, lru_ba, lru_wi, lru_bi,
           lru_lam, attn_out_g, lru_out_g, w_out, norm2_g, w_router, b_router, w_gate, b_gate,
           w_up, b_up, w_down, b_down, final_g):
    B, S, D = x.shape
    T = B * S
    assert w_in.shape[0] == 1, "single-layer trunk: the final norm is fused into the layer's combine"
    assert D == SUBLANES * LANES, "a token row is moved as one (8, 128) f32 slab"
    assert S % max(TS_IN, TQ, TK, TC_LRU) == 0 and S % GRID_W == 0, "sequence tiles must divide S"
    x2 = x.reshape(T, D)
    for l in range(1):
        qt, k, vt, lru_x, lru_gate = _inproj(x2, norm1_g[l], w_in[l], q_norm_g[l], k_norm_g[l], S)
        score_bound = (HEAD_DIM * Q_SCALE * jnp.max(jnp.abs(q_norm_g[l]))
                       * jnp.max(jnp.abs(k_norm_g[l])))
        lru_ops = _lru_operands(conv_w[l], conv_b[l], lru_wa[l], lru_ba[l], lru_wi[l], lru_bi[l],
                                lru_lam[l])
        attn, lru = _mixers(qt, k.reshape(B, S, -1), vt, lru_x.reshape(B, S, -1),
                            lru_gate.reshape(B, S, -1), lru_ops, score_bound, B, S)
        x1, xn3, route, gates, cnt = _outproj_router(
            attn.reshape(T, -1), lru.reshape(T, -1), x2, attn_out_g[l], lru_out_g[l], w_out[l],
            norm2_g[l], w_router[l], b_router[l])

        n_rows = T * TOP_K + N_EXPERTS * ROW_BLOCK
        pstart, block_e, fill, next_e, n_active = _routing_plan(cnt, n_rows // ROW_BLOCK)
        dest_flat = _dest_rows(route, pstart)
        x_rows = _dispatch(xn3, fill, dest_flat, n_rows)
        y_rows = _experts(x_rows, block_e, n_active, next_e, w_gate[l], b_gate[l], w_up[l], b_up[l],
                          w_down[l], b_down[l])
        x2 = _combine(y_rows, dest_flat, x1, gates, final_g)
    return x2.reshape(B, S, D)
```

```python
import functools
import math

import jax
import jax.numpy as jnp
import numpy as np
from jax import lax
from jax.experimental import pallas as pl
from jax.experimental.pallas import tpu as pltpu

F32 = jnp.float32
BF16 = jnp.bfloat16

GRID_W = 64
HEAD_DIM = 64
N_Q_HEADS = 8
N_KV_HEADS = 2
GQA_GROUP = N_Q_HEADS // N_KV_HEADS
ATTN_W = N_Q_HEADS * HEAD_DIM
KV_W = N_KV_HEADS * HEAD_DIM
LRU_BLOCKS = 8
LRU_C = 8.0
CONV_W = 4
CONV_PAD_L = 2
ROPE_THETA = 10000.0
ROPE_HALF = HEAD_DIM // 2
ROPE_M = ROPE_HALF // 2
N_EXPERTS = 32
TOP_K = 4
SWIGLU_ALPHA = 1.702
SWIGLU_LIMIT = 7.0
NORM_EPS = 1e-5
QK_EPS = 1e-6
LOG2_E = 1.4426950408889634
Q_SCALE = HEAD_DIM ** -0.5 * LOG2_E
SAFE_SCORE_LOG2 = 96.0

LANES = 128
SUBLANES = 8
BF16_SUBLANES = 16
PV_ROWS = HEAD_DIM + BF16_SUBLANES
VMEM_LIMIT = 48 * 1024 * 1024
EXPERT_VMEM_LIMIT = 56 * 1024 * 1024

TS_IN = 512
TQ = 256
TK = 256
KV_UNROLL = 8
HEADS_PER_STEP = 2
TC_LRU = 512
TS_OUT = 512
TS_DEST = 2048
ROW_BLOCK = 512
TS_DISP = 1024
TS_COMB = 512
ISSUE_UNROLL = 8


def _cparams(sem):
    return pltpu.CompilerParams(dimension_semantics=sem, vmem_limit_bytes=VMEM_LIMIT)


def _inproj_kernel(x_ref, g1_ref, wt_ref, w_ref, qg_ref, kg_ref, cos_ref, sin_ref, cost_ref, sint_ref,
                   q_ref, k_ref, v_ref, lx_ref, lg_ref, *, lru_w):
    x = x_ref[...]
    ms = jnp.mean(x * x, axis=-1, keepdims=True)
    xn = (x * lax.rsqrt(ms + NORM_EPS) * g1_ref[...]).astype(BF16)
    ht = lax.dot_general(wt_ref[...], xn, (((1,), (1,)), ((), ())), preferred_element_type=F32)
    h = jnp.dot(xn, w_ref[...], preferred_element_type=F32)

    qw = N_Q_HEADS * LANES
    kw = N_KV_HEADS * LANES
    cost = cost_ref[...]
    sint = sint_ref[...]
    row = lax.broadcasted_iota(jnp.int32, cost.shape, 0)
    first_half_t = (row % ROPE_HALF) < ROPE_M
    qg = qg_ref[...]
    for c in range(N_Q_HEADS):
        sl = slice(c * LANES, (c + 1) * LANES)
        xc = ht[sl]
        hms = jnp.sum(xc * xc, axis=0, keepdims=True) * (1.0 / HEAD_DIM)
        xc = xc * lax.rsqrt(hms + QK_EPS) * qg
        partner = jnp.where(first_half_t, pltpu.roll(xc, LANES - ROPE_M, 0), pltpu.roll(xc, ROPE_M, 0))
        q_ref[0, sl, :] = ((xc * cost + partner * sint) * Q_SCALE).astype(BF16)
    for c in range(N_KV_HEADS):
        sl = slice(c * LANES, (c + 1) * LANES)
        v_ref[0, sl, :] = jnp.where(row >= HEAD_DIM, 1.0, ht[qw + c * LANES: qw + (c + 1) * LANES]).astype(BF16)

    cos = cos_ref[...]
    sin = sin_ref[...]
    lane = lax.broadcasted_iota(jnp.int32, cos.shape, 1)
    first_half = (lane % ROPE_HALF) < ROPE_M
    for c in range(N_KV_HEADS):
        sl = slice(c * LANES, (c + 1) * LANES)
        xc = h[:, sl]
        hms = jnp.sum(xc * xc, axis=-1, keepdims=True) * (1.0 / HEAD_DIM)
        xc = xc * lax.rsqrt(hms + QK_EPS) * kg_ref[...]
        partner = jnp.where(first_half, pltpu.roll(xc, LANES - ROPE_M, 1), pltpu.roll(xc, ROPE_M, 1))
        k_ref[:, sl] = (xc * cos + partner * sin).astype(BF16)
    lx_ref[...] = h[:, kw: kw + lru_w]
    lg_ref[...] = h[:, kw + lru_w: kw + 2 * lru_w]


def _pad_heads(w, n_heads):
    lead = w.shape[:-1]
    w = w.reshape(lead + (n_heads, HEAD_DIM))
    w = jnp.pad(w, [(0, 0)] * len(lead) + [(0, 0), (0, LANES - HEAD_DIM)])
    return w.reshape(lead + (n_heads * LANES,))


def _rope_tables(S):
    t = np.arange(S)
    rows = (t // GRID_W).astype(np.float32)
    cols = (t % GRID_W).astype(np.float32)
    inv_freq = (ROPE_THETA ** (-np.arange(ROPE_M, dtype=np.float32) / ROPE_M)).astype(np.float32)
    ar = rows[:, None] * inv_freq[None, :]
    ac = cols[:, None] * inv_freq[None, :]
    cos = np.concatenate([np.cos(ar), np.cos(ar), np.cos(ac), np.cos(ac)], axis=-1)
    sin = np.concatenate([-np.sin(ar), np.sin(ar), -np.sin(ac), np.sin(ac)], axis=-1)
    pad = [(0, 0), (0, LANES - HEAD_DIM)]
    cos = np.pad(cos, pad).astype(np.float32)
    sin = np.pad(sin, pad).astype(np.float32)
    return cos, sin, np.ascontiguousarray(cos.T), np.ascontiguousarray(sin.T)


def _inproj(x2, norm1_g, w_in, q_norm_g, k_norm_g, S):
    T, D = x2.shape
    lru_w = (w_in.shape[1] - ATTN_W - 2 * KV_W) // 2
    o0, o1, o2 = ATTN_W, ATTN_W + KV_W, ATTN_W + 2 * KV_W
    w_t = jnp.concatenate([_pad_heads(w_in[:, :o0], N_Q_HEADS),
                           _pad_heads(w_in[:, o1:o2], N_KV_HEADS)], axis=1).T.astype(BF16)
    w_rest = jnp.concatenate([_pad_heads(w_in[:, o0:o1], N_KV_HEADS), w_in[:, o2:]],
                             axis=1).astype(BF16)
    qg = _pad_heads(q_norm_g.reshape(1, HEAD_DIM), 1).reshape(LANES, 1)
    kg = _pad_heads(k_norm_g.reshape(1, HEAD_DIM), 1)
    cos, sin, cos_t, sin_t = _rope_tables(S)
    ts = TS_IN
    n_s = S // ts
    qw, kw = N_Q_HEADS * LANES, N_KV_HEADS * LANES
    const = lambda i: (0, 0)
    tok = lambda i: (i, 0)
    pos = lambda i: (i % n_s, 0)
    pos_t = lambda i: (0, i % n_s)
    tposed = lambda i: (i // n_s, 0, i % n_s)
    return pl.pallas_call(
        functools.partial(_inproj_kernel, lru_w=lru_w),
        grid=(T // ts,),
        in_specs=[
            pl.BlockSpec((ts, D), tok),
            pl.BlockSpec((1, D), const),
            pl.BlockSpec(w_t.shape, const),
            pl.BlockSpec(w_rest.shape, const),
            pl.BlockSpec((LANES, 1), const),
            pl.BlockSpec((1, LANES), const),
            pl.BlockSpec((ts, LANES), pos),
            pl.BlockSpec((ts, LANES), pos),
            pl.BlockSpec((LANES, ts), pos_t),
            pl.BlockSpec((LANES, ts), pos_t),
        ],
        out_specs=[
            pl.BlockSpec((1, qw, ts), tposed),
            pl.BlockSpec((ts, kw), tok),
            pl.BlockSpec((1, kw, ts), tposed),
            pl.BlockSpec((ts, lru_w), tok),
            pl.BlockSpec((ts, lru_w), tok),
        ],
        out_shape=[
            jax.ShapeDtypeStruct((T // S, qw, S), BF16),
            jax.ShapeDtypeStruct((T, kw), BF16),
            jax.ShapeDtypeStruct((T // S, kw, S), BF16),
            jax.ShapeDtypeStruct((T, lru_w), F32),
            jax.ShapeDtypeStruct((T, lru_w), F32),
        ],
        compiler_params=_cparams(("parallel",)),
        name="inproj",
    )(x2, norm1_g.reshape(1, D), w_t, w_rest, qg, kg, cos, sin, cos_t, sin_t)


def _attn_kernel(qt_ref, k_ref, vt_ref, o_ref, acc_ref, s_ref, p_ref, *, tq, tk, n_kv, kv_unroll):
    hp = HEADS_PER_STEP
    spt = GQA_GROUP // hp
    acc_ref[...] = jnp.zeros(acc_ref.shape, F32)

    def scores(j, sp):
        kt = k_ref[0, pl.ds(pl.multiple_of(j * tk, tk), tk), :]
        out = []
        for u in range(hp):
            g = sp * hp + u
            s = jnp.dot(kt, qt_ref[0, g * LANES:(g + 1) * LANES, :], preferred_element_type=F32)
            out.append((s, jnp.max(s, axis=0, keepdims=True)))
        return out

    def softmax_stage(sc, ms, sp):
        out = []
        for u, (s, s_max) in enumerate(sc):
            h = sp * hp + u
            m_new = jnp.maximum(ms[h], s_max)
            out.append((jnp.exp2(ms[h] - m_new), jnp.exp2(s - m_new).astype(BF16)))
            ms[h] = m_new
        return out

    def pv_stage(j, sp, ap):
        vt = vt_ref[0, 0:PV_ROWS, pl.ds(pl.multiple_of(j * tk, tk), tk)]
        for u, (alpha, p) in enumerate(ap):
            g = sp * hp + u
            acc_ref[g] = alpha * acc_ref[g] + jnp.dot(vt, p, preferred_element_type=F32)

    ms = [jnp.full((1, tq), -jnp.inf, F32)] * GQA_GROUP
    ap = softmax_stage(scores(0, 0), ms, 0)
    sc = scores(min(1 // spt, n_kv - 1), 1 % spt)
    for u in range(hp):
        s_ref[u] = sc[u][0]
        p_ref[u] = ap[u][1]

    def body(it, carry):
        ms = list(carry[:GQA_GROUP])
        ap = [(carry[GQA_GROUP + u], p_ref[u]) for u in range(hp)]
        sc = [(s_ref[u], carry[GQA_GROUP + hp + u]) for u in range(hp)]
        for n in range(kv_unroll * spt):
            j = it * kv_unroll + n // spt
            j_next = jnp.minimum(it * kv_unroll + (n + 2) // spt, n_kv - 1)
            sc_next = scores(j_next, (n + 2) % spt)
            ap_next = softmax_stage(sc, ms, (n + 1) % spt)
            pv_stage(j, n % spt, ap)
            sc, ap = sc_next, ap_next
        for u in range(hp):
            s_ref[u] = sc[u][0]
            p_ref[u] = ap[u][1]
        return tuple(ms) + tuple(a for a, _ in ap) + tuple(m for _, m in sc)

    lax.fori_loop(0, n_kv // kv_unroll, body,
                  tuple(ms) + tuple(a for a, _ in ap) + tuple(m for _, m in sc))
    _attn_finalize(acc_ref, o_ref, tq)


def _attn_finalize(acc_ref, o_ref, tq):
    pad = jnp.zeros((LANES - HEAD_DIM, tq), F32)
    for g in range(GQA_GROUP):
        acc = acc_ref[g]
        o = acc[0:HEAD_DIM] / acc[HEAD_DIM:HEAD_DIM + 1, :]
        o_ref[0, :, g * LANES:(g + 1) * LANES] = jnp.concatenate([o, pad], axis=0).T.astype(BF16)


def _attn_bounded_kernel(qt_ref, k_ref, vt_ref, o_ref, acc_ref, s_ref, p_ref, *, tq, tk, n_kv, kv_unroll):
    hp = HEADS_PER_STEP
    spt = GQA_GROUP // hp
    acc_ref[...] = jnp.zeros(acc_ref.shape, F32)

    def scores(j, sp):
        kt = k_ref[0, pl.ds(pl.multiple_of(j * tk, tk), tk), :]
        return [jnp.dot(kt, qt_ref[0, (sp * hp + u) * LANES:(sp * hp + u + 1) * LANES, :],
                        preferred_element_type=F32) for u in range(hp)]

    def probs(sc):
        return [jnp.exp2(s).astype(BF16) for s in sc]

    def pv_stage(j, sp, ps):
        vt = vt_ref[0, 0:PV_ROWS, pl.ds(pl.multiple_of(j * tk, tk), tk)]
        for u, p in enumerate(ps):
            acc_ref[sp * hp + u] += jnp.dot(vt, p, preferred_element_type=F32)

    ps = probs(scores(0, 0))
    sc = scores(min(1 // spt, n_kv - 1), 1 % spt)
    for u in range(hp):
        s_ref[u] = sc[u]
        p_ref[u] = ps[u]

    def body(it, carry):
        ps = [p_ref[u] for u in range(hp)]
        sc = [s_ref[u] for u in range(hp)]
        for n in range(kv_unroll * spt):
            j = it * kv_unroll + n // spt
            j_next = jnp.minimum(it * kv_unroll + (n + 2) // spt, n_kv - 1)
            sc_next = scores(j_next, (n + 2) % spt)
            ps_next = probs(sc)
            pv_stage(j, n % spt, ps)
            sc, ps = sc_next, ps_next
        for u in range(hp):
            s_ref[u] = sc[u]
            p_ref[u] = ps[u]
        return carry

    lax.fori_loop(0, n_kv // kv_unroll, body, 0)
    _attn_finalize(acc_ref, o_ref, tq)


def _attention(qt, k, vt, *, kernel, B, S):
    tq = min(TQ, S)
    tk = min(TK, S)
    gw = GQA_GROUP * LANES
    return pl.pallas_call(
        functools.partial(kernel, tq=tq, tk=tk, n_kv=S // tk,
                          kv_unroll=math.gcd(S // tk, KV_UNROLL)),
        grid=(B, N_KV_HEADS, S // tq),
        in_specs=[
            pl.BlockSpec((1, gw, tq), lambda b, h, i: (b, h, i)),
            pl.BlockSpec((1, S, LANES), lambda b, h, i: (b, 0, h)),
            pl.BlockSpec((1, LANES, S), lambda b, h, i: (b, h, 0)),
        ],
        out_specs=pl.BlockSpec((1, tq, gw), lambda b, h, i: (b, i, h)),
        out_shape=jax.ShapeDtypeStruct((B, S, N_Q_HEADS * LANES), BF16),
        scratch_shapes=[pltpu.VMEM((GQA_GROUP, PV_ROWS, tq), F32),
                        pltpu.VMEM((HEADS_PER_STEP, tk, tq), F32),
                        pltpu.VMEM((HEADS_PER_STEP, tk, tq), BF16)],
        compiler_params=_cparams(("parallel", "parallel", "parallel")),
        name=kernel.__name__.strip("_"),
    )(qt, k, vt)


def _scan_chunk(a, b, h_in, reverse):
    n = a.shape[0]
    n_groups = n // SUBLANES
    a = a.reshape(n_groups, SUBLANES, LANES)
    b = b.reshape(n_groups, SUBLANES, LANES)
    sub = lax.broadcasted_iota(jnp.int32, a.shape, 1)
    d = 1
    while d < SUBLANES:
        if reverse:
            keep = sub < SUBLANES - d
            shift = SUBLANES - d
        else:
            keep = sub >= d
            shift = d
        a_sh = jnp.where(keep, pltpu.roll(a, shift, 1), 1.0)
        b_sh = jnp.where(keep, pltpu.roll(b, shift, 1), 0.0)
        b = a * b_sh + b
        a = a * a_sh
        d *= 2
    a = a.reshape(n, LANES)
    b = b.reshape(n, LANES)
    order = range(n_groups - 1, -1, -1) if reverse else range(n_groups)
    edge = h_in
    out = [None] * n_groups
    for v in order:
        rows = slice(v * SUBLANES, (v + 1) * SUBLANES)
        hv = b[rows] + a[rows] * jnp.broadcast_to(edge, (SUBLANES, LANES))
        out[v] = hv
        edge = hv[0:1] if reverse else hv[SUBLANES - 1:SUBLANES]
    return jnp.concatenate(out, axis=0), edge


def _lru_pad_input(u_ref, up_ref, S):
    zeros = jnp.zeros((SUBLANES, LANES), F32)
    up_ref[0:SUBLANES, :] = zeros
    up_ref[S + SUBLANES:S + 2 * SUBLANES, :] = zeros
    up_ref[SUBLANES:S + SUBLANES, :] = u_ref[0]


def _lru_gates(up_ref, cw_ref, cb_ref, w_ref, bias_ref, lam_ref, t0, tc, d):
    cw = cw_ref[...]
    xc = cb_ref[...]
    for j in range(CONV_W):
        xc = xc + up_ref[pl.ds(t0 + SUBLANES + j - CONV_PAD_L, tc), :] * cw[j:j + 1, :]
    gw = 2 * LANES
    g = jnp.dot(xc.astype(BF16), w_ref[0, :, d * gw:(d + 1) * gw],
                preferred_element_type=F32) + bias_ref[0, :, d * gw:(d + 1) * gw]
    r = jax.nn.sigmoid(g[:, :LANES])
    i = jax.nn.sigmoid(g[:, LANES:])
    a = jnp.exp(-LRU_C * r * jax.nn.softplus(-lam_ref[d:d + 1, :]))
    y = 1.0 - a * a
    b = jnp.where(y > 0.0, y * lax.rsqrt(y), 0.0) * i * xc
    return a, b


def _lru_kernel(u_ref, gate_ref, cw_ref, cb_ref, w_ref, bias_ref, lam_ref, o_ref,
                up_ref, hf_ref, *, S, tc):
    _lru_pad_input(u_ref, up_ref, S)
    n_chunks = S // tc
    params = (up_ref, cw_ref, cb_ref, w_ref, bias_ref, lam_ref)

    def fwd(c, h):
        t0 = pl.multiple_of(c * tc, tc)
        hc, h_last = _scan_chunk(*_lru_gates(*params, t0, tc, 0), h, False)
        hf_ref[pl.ds(t0, tc), :] = hc
        return h_last

    lax.fori_loop(0, n_chunks, fwd, jnp.zeros((1, LANES), F32))

    def bwd(ci, h):
        t0 = pl.multiple_of((n_chunks - 1 - ci) * tc, tc)
        hc, h_last = _scan_chunk(*_lru_gates(*params, t0, tc, 1), h, True)
        gate = gate_ref[0, pl.ds(t0, tc), :]
        o_ref[0, pl.ds(t0, tc), :] = (hf_ref[pl.ds(t0, tc), :] + hc) * jax.nn.gelu(gate)
        return h_last

    lax.fori_loop(0, n_chunks, bwd, jnp.zeros((1, LANES), F32))


def _block_diag_pairs(w):
    nb, bw, _ = w.shape
    w = w.reshape(nb // 2, 2, bw, bw)
    z = jnp.zeros_like(w[:, 0])
    top = jnp.concatenate([w[:, 0], z], axis=-1)
    bot = jnp.concatenate([z, w[:, 1]], axis=-1)
    return jnp.concatenate([top, bot], axis=-2)


def _lru_operands(conv_w, conv_b, wa, ba, wi, bi, lam):
    C = conv_b.shape[0]
    nc = C // LANES
    w = jnp.concatenate([_block_diag_pairs(wa[0]), _block_diag_pairs(wi[0]),
                         _block_diag_pairs(wa[1]), _block_diag_pairs(wi[1])], axis=-1).astype(BF16)
    bias = jnp.stack([ba[0].reshape(nc, LANES), bi[0].reshape(nc, LANES),
                      ba[1].reshape(nc, LANES), bi[1].reshape(nc, LANES)], axis=1)
    return conv_w, conv_b.reshape(1, C), w, bias.reshape(nc, 1, 4 * LANES), lam


def _lru_specs(S, unit):
    seq = lambda *g: (unit(*g)[0], 0, unit(*g)[1])
    chan = lambda *g: (0, unit(*g)[1])
    blk = lambda *g: (unit(*g)[1], 0, 0)
    in_specs = [
        pl.BlockSpec((1, S, LANES), seq),
        pl.BlockSpec((1, S, LANES), seq),
        pl.BlockSpec((CONV_W, LANES), chan),
        pl.BlockSpec((1, LANES), chan),
        pl.BlockSpec((1, LANES, 4 * LANES), blk),
        pl.BlockSpec((1, 1, 4 * LANES), blk),
        pl.BlockSpec((2, LANES), chan),
    ]
    return in_specs, pl.BlockSpec((1, S, LANES), seq)


def _lru(lru_x, lru_gate, lru_ops, B, S):
    C = lru_x.shape[-1]
    tc = min(TC_LRU, S)
    in_specs, out_spec = _lru_specs(S, lambda b, c: (b, c))
    return pl.pallas_call(
        functools.partial(_lru_kernel, S=S, tc=tc),
        grid=(B, C // LANES),
        in_specs=in_specs,
        out_specs=out_spec,
        out_shape=jax.ShapeDtypeStruct((B, S, C), F32),
        scratch_shapes=[
            pltpu.VMEM((S + 2 * SUBLANES, LANES), F32),
            pltpu.VMEM((S, LANES), F32),
        ],
        compiler_params=_cparams(("parallel", "parallel")),
        name="rglru",
    )(lru_x, lru_gate, *lru_ops)


def _mixers(qt, k, vt, lru_x, lru_gate, lru_ops, score_bound, B, S):
    attn = lax.cond(score_bound <= SAFE_SCORE_LOG2,
                    functools.partial(_attention, kernel=_attn_bounded_kernel, B=B, S=S),
                    functools.partial(_attention, kernel=_attn_kernel, B=B, S=S), qt, k, vt)
    return attn, _lru(lru_x, lru_gate, lru_ops, B, S)


def _rows_to_slabs(ref, x):
    n = x.shape[0]
    for s in range(SUBLANES):
        ref[pl.ds(s, n, stride=SUBLANES), :] = x[:, s * LANES:(s + 1) * LANES]


def _slabs_to_rows(ref, n):
    return jnp.concatenate([ref[pl.ds(s, n, stride=SUBLANES), :] for s in range(SUBLANES)], axis=1)


def _slab(ref, r):
    return ref.at[pl.ds(pl.multiple_of(r * SUBLANES, SUBLANES), SUBLANES)]


def _outproj_kernel(a_ref, l_ref, x_ref, ag_ref, lg_ref, wa_ref, wl_ref, g2_ref,
                    wr_ref, br_ref, tri_ref,
                    x1_ref, xn3_ref, route_ref, gates_ref, cnt_ref, carry_ref, *, attn_w, lru_w):
    step = pl.program_id(0)

    @pl.when(step == 0)
    def _():
        carry_ref[...] = jnp.zeros_like(carry_ref)

    a = a_ref[...].astype(F32)
    ams = jnp.sum(a * a, axis=-1, keepdims=True) * (1.0 / attn_w)
    an = a * lax.rsqrt(ams + NORM_EPS) * ag_ref[...]
    l = l_ref[...]
    lms = jnp.sum(l * l, axis=-1, keepdims=True) * (1.0 / lru_w)
    ln = l * lax.rsqrt(lms + NORM_EPS) * lg_ref[...]
    mix = (jnp.dot(an.astype(BF16), wa_ref[...], preferred_element_type=F32)
           + jnp.dot(ln.astype(BF16), wl_ref[...], preferred_element_type=F32))
    x1 = x_ref[...] + mix
    x1_ref[...] = x1
    ms = jnp.mean(x1 * x1, axis=-1, keepdims=True)
    xn = x1 * lax.rsqrt(ms + NORM_EPS) * g2_ref[...]
    _rows_to_slabs(xn3_ref, xn)

    logits = jnp.dot(xn.astype(BF16), wr_ref[...], preferred_element_type=F32) + br_ref[...]
    lane = lax.broadcasted_iota(jnp.int32, logits.shape, 1)
    neg = -jnp.inf
    work = jnp.where(lane < N_EXPERTS, logits, neg)
    sel = jnp.zeros(logits.shape, F32)
    idxs, vals = [], []
    for _ in range(TOP_K):
        m = jnp.max(work, axis=1, keepdims=True)
        idx = jnp.min(jnp.where(work == m, lane, LANES), axis=1, keepdims=True)
        hit = lane == idx
        work = jnp.where(hit, neg, work)
        sel = sel + hit.astype(F32)
        idxs.append(idx)
        vals.append(m)
    es = [jnp.exp(v - vals[0]) for v in vals]
    den = es[0] + es[1] + es[2] + es[3]

    prefix = jnp.dot(tri_ref[...], sel.astype(BF16), preferred_element_type=F32) + carry_ref[...]
    carry_ref[...] = carry_ref[...] + jnp.sum(sel, axis=0, keepdims=True)
    cnt_ref[...] = carry_ref[...]

    route = jnp.zeros(logits.shape, jnp.int32)
    gates = jnp.zeros(logits.shape, F32)
    for k in range(TOP_K):
        rank = jnp.sum(jnp.where(lane == idxs[k], prefix, 0.0), axis=1, keepdims=True).astype(jnp.int32)
        route = jnp.where(lane == k, idxs[k], route)
        route = jnp.where(lane == TOP_K + k, rank, route)
        gates = jnp.where(lane == k, es[k] / den, gates)
    route_ref[...] = route
    gates_ref[...] = gates


def _outproj_router(attn, lru, x2, attn_out_g, lru_out_g, w_out, norm2_g, w_router, b_router):
    T, D = x2.shape
    lru_w = lru.shape[-1]
    ts = min(TS_OUT, T)
    wa = w_out[:ATTN_W].reshape(N_Q_HEADS, HEAD_DIM, D)
    wa = jnp.pad(wa, ((0, 0), (0, LANES - HEAD_DIM), (0, 0))).reshape(N_Q_HEADS * LANES, D).astype(BF16)
    wl = w_out[ATTN_W:].astype(BF16)
    ag = _pad_heads(attn_out_g.reshape(1, ATTN_W), N_Q_HEADS)
    wr = jnp.pad(w_router, ((0, 0), (0, LANES - N_EXPERTS))).astype(BF16)
    br =jnp.pad(b_router.reshape(1, N_EXPERTS), ((0, 0), (0, LANES - N_EXPERTS)))
    tri = (jnp.arange(ts)[:, None] > jnp.arange(ts)[None, :]).astype(BF16)
    const = lambda i: (0, 0)
    tok = lambda i: (i, 0)
    aw = N_Q_HEADS * LANES
    return pl.pallas_call(
        functools.partial(_outproj_kernel, attn_w=ATTN_W, lru_w=lru_w),
        grid=(T // ts,),
        in_specs=[
            pl.BlockSpec((ts, aw), tok),
            pl.BlockSpec((ts, lru_w), tok),
            pl.BlockSpec((ts, D), tok),
            pl.BlockSpec((1, aw), const),
            pl.BlockSpec((1, lru_w), const),
            pl.BlockSpec((aw, D), const),
            pl.BlockSpec((lru_w, D), const),
            pl.BlockSpec((1, D), const),
            pl.BlockSpec((D, LANES), const),
            pl.BlockSpec((1, LANES), const),
            pl.BlockSpec((ts, ts), const),
        ],
        out_specs=[
            pl.BlockSpec((ts, D), tok),
            pl.BlockSpec((ts * SUBLANES, LANES), tok),
            pl.BlockSpec((ts, LANES), tok),
            pl.BlockSpec((ts, LANES), tok),
            pl.BlockSpec((1, LANES), const),
        ],
        out_shape=[
            jax.ShapeDtypeStruct((T, D), F32),
            jax.ShapeDtypeStruct((T * SUBLANES, LANES), F32),
            jax.ShapeDtypeStruct((T, LANES), jnp.int32),
            jax.ShapeDtypeStruct((T, LANES), F32),
            jax.ShapeDtypeStruct((1, LANES), F32),
        ],
        scratch_shapes=[pltpu.VMEM((1, LANES), F32)],
        compiler_params=_cparams(("arbitrary",)),
        name="outproj_router",
    )(attn, lru, x2, ag, lru_out_g.reshape(1, lru_w), wa, wl, norm2_g.reshape(1, D),
      wr, br, tri)


def _plan_kernel(cnt_ref, pstart_ref, plan_ref):
    cnt = cnt_ref[...]
    lane = lax.broadcasted_iota(jnp.int32, cnt.shape, 1)
    padded = jnp.floor((cnt + (ROW_BLOCK - 1)) * (1.0 / ROW_BLOCK)) * ROW_BLOCK
    pend = padded
    d = 1
    while d < N_EXPERTS:
        pend = pend + jnp.where(lane >= d, pltpu.roll(pend, d, 1), 0.0)
        d *= 2
    pstart_ref[...] = pend - padded
    total = jnp.max(pend, axis=1, keepdims=True)

    shape = plan_ref.shape
    lanes = lax.broadcasted_iota(jnp.int32, shape, 1)
    is_expert = lanes < N_EXPERTS
    start = lax.broadcasted_iota(jnp.int32, shape, 0).astype(F32) * ROW_BLOCK

    def groups_ending_by(row):
        return jnp.sum(jnp.where(jnp.logical_and(pend <= row, is_expert), 1.0, 0.0), axis=1, keepdims=True)

    block_e = jnp.minimum(groups_ending_by(start), N_EXPERTS - 1.0)
    tail = jnp.max(jnp.where(jnp.logical_and(jnp.logical_and(pend == start + ROW_BLOCK, padded > 0.0),
                                             is_expert), 1.0, 0.0), axis=1, keepdims=True)
    fill = jnp.maximum(tail, jnp.where(start[:, 0:1] >= total, 1.0, 0.0))
    group_end = jnp.sum(jnp.where(lanes.astype(F32) == block_e, pend, 0.0), axis=1, keepdims=True)
    next_e = jnp.where(group_end < total,
                       jnp.minimum(groups_ending_by(group_end), N_EXPERTS - 1.0), -1.0)
    plan = jnp.where(lanes == 0, block_e,
                     jnp.where(lanes == 1, fill,
                               jnp.where(lanes == 2, next_e, total * (1.0 / ROW_BLOCK))))
    plan_ref[...] = plan.astype(jnp.int32)


def _routing_plan(cnt, n_blocks):
    assert ROW_BLOCK & (ROW_BLOCK - 1) == 0, "exact f32 division by the row block size"
    rows = -(-n_blocks // SUBLANES) * SUBLANES
    pstart, plan = pl.pallas_call(
        _plan_kernel,
        out_shape=[jax.ShapeDtypeStruct((1, LANES), F32),
                   jax.ShapeDtypeStruct((rows, LANES), jnp.int32)],
        name="routing_plan",
    )(cnt)
    return pstart, plan[:n_blocks, 0], plan[:n_blocks, 1], plan[:n_blocks, 2], plan[0:1, 3]


def _dest_kernel(route_ref, pstart_ref, dest_ref):
    route = route_ref[...]
    lane = lax.broadcasted_iota(jnp.int32, route.shape, 1)
    pstart = pstart_ref[...]
    dest = jnp.zeros(route.shape, jnp.int32)
    for k in range(TOP_K):
        start = jnp.sum(jnp.where(lane == route[:, k:k + 1], pstart, 0.0), axis=1, keepdims=True)
        dest = jnp.where(lane == k, start.astype(jnp.int32) + route[:, TOP_K + k:TOP_K + k + 1], dest)
    dest_ref[...] = dest


def _dest_rows(route, pstart):
    T = route.shape[0]
    ts = math.gcd(TS_DEST, T)
    dest = pl.pallas_call(
        _dest_kernel,
        grid=(T // ts,),
        in_specs=[pl.BlockSpec((ts, LANES), lambda i: (i, 0)),
                  pl.BlockSpec((1, LANES), lambda i: (0, 0))],
        out_specs=pl.BlockSpec((ts, LANES), lambda i: (i, 0)),
        out_shape=jax.ShapeDtypeStruct((T, LANES), jnp.int32),
        compiler_params=_cparams(("parallel",)),
        name="dest_rows",
    )(route, pstart)
    return dest[:, :TOP_K].reshape(T * TOP_K)


def _dispatch_kernel(fill_ref, dest_ref, x_ref, out_hbm, zero_ref, sem, zero_sem, *, ts, n_blocks):
    block_slabs = ROW_BLOCK * SUBLANES

    def fill_copy(b):
        off = pl.multiple_of(b * block_slabs, block_slabs)
        return pltpu.make_async_copy(zero_ref, out_hbm.at[pl.ds(off, block_slabs)], zero_sem)

    @pl.when(pl.program_id(0) == 0)
    def _():
        zero_ref[...] = jnp.zeros(zero_ref.shape, F32)

        def start(b, carry):
            @pl.when(fill_ref[b] != 0)
            def _():
                fill_copy(b).start()
            return carry

        def wait(b, carry):
            @pl.when(fill_ref[b] != 0)
            def _():
                fill_copy(b).wait()
            return carry

        lax.fori_loop(0, n_blocks, start, 0)
        lax.fori_loop(0, n_blocks, wait, 0)

    def issue(i, carry):
        for j in range(ISSUE_UNROLL):
            r = i * ISSUE_UNROLL + j
            for k in range(TOP_K):
                d = dest_ref[r * TOP_K + k]
                pltpu.make_async_copy(_slab(x_ref, r), _slab(out_hbm, d), sem).start(priority=k % 2)
        return carry

    lax.fori_loop(0, ts // ISSUE_UNROLL, issue, 0)
    for k in range(TOP_K):
        pltpu.make_async_copy(x_ref, out_hbm.at[pl.ds(0, ts * SUBLANES)], sem).wait()


def _dispatch(xn_slabs, fill, dest_flat, n_rows):
    T = xn_slabs.shape[0] // SUBLANES
    ts = min(TS_DISP, T)
    grid_spec = pltpu.PrefetchScalarGridSpec(
        num_scalar_prefetch=1,
        grid=(T // ts,),
        in_specs=[
            pl.BlockSpec((ts * TOP_K,), lambda i, fl: (i,), memory_space=pltpu.SMEM),
            pl.BlockSpec((ts * SUBLANES, LANES), lambda i, fl: (i, 0)),
        ],
        out_specs=pl.BlockSpec(memory_space=pl.ANY),
        scratch_shapes=[pltpu.VMEM((ROW_BLOCK * SUBLANES, LANES), F32),
                        pltpu.SemaphoreType.DMA, pltpu.SemaphoreType.DMA],
    )
    return pl.pallas_call(
        functools.partial(_dispatch_kernel, ts=ts, n_blocks=n_rows // ROW_BLOCK),
        grid_spec=grid_spec,
        out_shape=jax.ShapeDtypeStruct((n_rows * SUBLANES, LANES), xn_slabs.dtype),
        compiler_params=_cparams(("arbitrary",)),
        name="dispatch",
    )(fill, dest_flat, xn_slabs)


def _expert_loop_kernel(be_ref, na_ref, nxt_ref, x_hbm, wg_hbm, bg_ref, wu_hbm, bu_ref, wd_hbm, bd_ref,
                        y_hbm, xbuf, ybuf, stage_ref, wb_ref, x_sems, y_sems, w_sems):
    w_hbm = (wg_hbm, wu_hbm, wd_hbm)
    block_slabs = ROW_BLOCK * SUBLANES
    n_active = na_ref[0]

    def rows(b):
        return pl.ds(pl.multiple_of(b * block_slabs, block_slabs), block_slabs)

    def x_copy(b, s):
        return pltpu.make_async_copy(x_hbm.at[rows(b)], xbuf.at[s], x_sems.at[s])

    def y_copy(b, s):
        return pltpu.make_async_copy(ybuf.at[s], y_hbm.at[rows(b)], y_sems.at[s])

    def fetch(expert, s, m):
        return pltpu.make_async_copy(w_hbm[m].at[expert], stage_ref.at[s, m], w_sems.at[s, m])

    x_copy(0, 0).start()
    for m in range(3):
        fetch(be_ref[0], 0, m).start()

    def body(b, wslot):
        s = b % 2
        e = be_ref[b]
        x_copy(b, s).wait()

        @pl.when(b + 1 < n_active)
        def _():
            x_copy(b + 1, 1 - s).start()

        first = jnp.logical_or(b == 0, e != be_ref[jnp.maximum(b - 1, 0)])

        @pl.when(first)
        def _():
            for m in range(3):
                fetch(e, wslot, m).wait()
                wb_ref[m] = stage_ref[wslot, m].astype(BF16)

            @pl.when(nxt_ref[b] >= 0)
            def _():
                for m in range(3):
                    fetch(nxt_ref[b], 1 - wslot, m).start()

        @pl.when(b >= 2)
        def _():
            y_copy(b - 2, s).wait()

        x = _slabs_to_rows(xbuf.at[s], ROW_BLOCK).astype(BF16)
        g = jnp.dot(x, wb_ref[0], preferred_element_type=F32) + bg_ref[e]
        u = jnp.dot(x, wb_ref[1], preferred_element_type=F32) + bu_ref[e]
        g = jnp.minimum(g, SWIGLU_LIMIT)
        u = jnp.clip(u, -SWIGLU_LIMIT, SWIGLU_LIMIT)
        glu = g * jax.nn.sigmoid(SWIGLU_ALPHA * g)
        y = jnp.dot(((u + 1.0) * glu).astype(BF16), wb_ref[2], preferred_element_type=F32) + bd_ref[e]
        _rows_to_slabs(ybuf.at[s], y)
        y_copy(b, s).start()
        return jnp.where(first, 1 - wslot, wslot)

    lax.fori_loop(0, n_active, body, 0)

    @pl.when(n_active >= 2)
    def _():
        y_copy(n_active - 2, n_active % 2).wait()

    y_copy(n_active - 1, (n_active - 1) % 2).wait()


def _experts(x_rows, block_e, n_active, next_e, w_gate, b_gate, w_up, b_up, w_down, b_down):
    E, D, FF = w_gate.shape
    assert D == FF, "the three expert matrices share one staging shape"
    block_slabs = ROW_BLOCK * SUBLANES
    whole = lambda i, be, na, nx: (0, 0, 0)

    grid_spec = pltpu.PrefetchScalarGridSpec(
        num_scalar_prefetch=3,
        grid=(1,),
        in_specs=[
            pl.BlockSpec(memory_space=pl.ANY),
            pl.BlockSpec(memory_space=pl.ANY),
            pl.BlockSpec((E, 1, FF), whole),
            pl.BlockSpec(memory_space=pl.ANY),
            pl.BlockSpec((E, 1, FF), whole),
            pl.BlockSpec(memory_space=pl.ANY),
            pl.BlockSpec((E, 1, D), whole),
        ],
        out_specs=pl.BlockSpec(memory_space=pl.ANY),
        scratch_shapes=[
            pltpu.VMEM((2, block_slabs, LANES), F32),
            pltpu.VMEM((2, block_slabs, LANES), F32),
            pltpu.VMEM((2, 3, D, FF), F32),
            pltpu.VMEM((3, D, FF), BF16),
            pltpu.SemaphoreType.DMA((2,)),
            pltpu.SemaphoreType.DMA((2,)),
            pltpu.SemaphoreType.DMA((2, 3)),
        ],
    )
    return pl.pallas_call(
        _expert_loop_kernel,
        grid_spec=grid_spec,
        out_shape=jax.ShapeDtypeStruct(x_rows.shape, F32),
        input_output_aliases={3: 0},
        compiler_params=pltpu.CompilerParams(dimension_semantics=("arbitrary",),
                                             vmem_limit_bytes=EXPERT_VMEM_LIMIT),
        name="experts",
    )(block_e, n_active, next_e, x_rows, w_gate, b_gate.reshape(E, 1, FF), w_up,
      b_up.reshape(E, 1, FF), w_down, b_down.reshape(E, 1, D))


def _combine_kernel(dest_ref, dest_next_ref, y_hbm, x1_ref, gates_ref, fg_ref, o_ref, bufs, sems,
                    *, ts, n_steps):
    i = pl.program_id(0)
    slot = i % 2

    def gather_tile(d_ref, s):
        def issue(it, carry):
            for j in range(ISSUE_UNROLL):
                r = it * ISSUE_UNROLL + j
                for k in range(TOP_K):
                    d = d_ref[r * TOP_K + k]
                    pltpu.make_async_copy(_slab(y_hbm, d), _slab(bufs.at[s, k], r),
                                          sems.at[s]).start(priority=k % 2)
            return carry

        lax.fori_loop(0, ts // ISSUE_UNROLL, issue, 0)

    @pl.when(i == 0)
    def _():
        gather_tile(dest_ref, 0)

    @pl.when(i + 1 < n_steps)
    def _():
        gather_tile(dest_next_ref, 1 - slot)

    for k in range(TOP_K):
        pltpu.make_async_copy(y_hbm.at[pl.ds(0, ts * SUBLANES)], bufs.at[slot, k], sems.at[slot]).wait()

    acc = x1_ref[...]
    gates = gates_ref[...]
    for k in range(TOP_K):
        acc = acc + _slabs_to_rows(bufs.at[slot, k], ts) * gates[:, k:k + 1]
    ms = jnp.mean(acc * acc, axis=-1, keepdims=True)
    o_ref[...] = acc * lax.rsqrt(ms + NORM_EPS) * fg_ref[...]


def _combine(y_rows, dest_flat, x1, gates, final_g):
    T, D = x1.shape
    ts = min(TS_COMB, T)
    n_steps = T // ts
    tok = lambda i: (i, 0)
    return pl.pallas_call(
        functools.partial(_combine_kernel, ts=ts, n_steps=n_steps),
        grid=(n_steps,),
        in_specs=[
            pl.BlockSpec((ts * TOP_K,), lambda i: (i,), memory_space=pltpu.SMEM),
            pl.BlockSpec((ts * TOP_K,), lambda i: (jnp.minimum(i + 1, n_steps - 1),),
                         memory_space=pltpu.SMEM),
            pl.BlockSpec(memory_space=pl.ANY),
            pl.BlockSpec((ts, D), tok),
            pl.BlockSpec((ts, LANES), tok),
            pl.BlockSpec((1, D), lambda i: (0, 0)),
        ],
        out_specs=pl.BlockSpec((ts, D), tok),
        out_shape=jax.ShapeDtypeStruct((T, D), F32),
        scratch_shapes=[pltpu.VMEM((2, TOP_K, ts * SUBLANES, LANES), F32),
                        pltpu.SemaphoreType.DMA((2,))],
        compiler_params=_cparams(("arbitrary",)),
        name="combine",
    )(dest_flat, dest_flat, y_rows, x1, gates, final_g.reshape(1, D))


def kernel(x, norm1_g, w_in, q_norm_g, k_norm_g, conv_w, conv_b, lru_wa, lru_ba, lru_wi, lru_bi,
           lru_lam, attn_out_g, lru_out_g, w_out, norm2_g, w_router, b_router, w_gate, b_gate,
           w_up, b_up, w_down, b_down, final_g):
    B, S, D = x.shape
    T = B * S
    assert w_in.shape[0] == 1, "single-layer trunk: the final norm is fused into the layer's combine"
    assert D == SUBLANES * LANES, "a token row is moved as one (8, 128) f32 slab"
    assert S % max(TS_IN, TQ, TK, TC_LRU) == 0 and S % GRID_W == 0, "sequence tiles must divide S"
    x2 = x.reshape(T, D)
    for l in range(1):
        qt, k, vt, lru_x, lru_gate = _inproj(x2, norm1_g[l], w_in[l], q_norm_g[l], k_norm_g[l], S)
        score_bound = (HEAD_DIM * Q_SCALE * jnp.max(jnp.abs(q_norm_g[l]))
                       * jnp.max(jnp.abs(k_norm_g[l])))
        lru_ops = _lru_operands(conv_w[l], conv_b[l], lru_wa[l], lru_ba[l], lru_wi[l], lru_bi[l],
                                lru_lam[l])
        attn, lru = _mixers(qt, k.reshape(B, S, -1), vt, lru_x.reshape(B, S, -1),
                            lru_gate.reshape(B, S, -1), lru_ops, score_bound, B, S)
        x1, xn3, route, gates, cnt = _outproj_router(
            attn.reshape(T, -1), lru.reshape(T, -1), x2, attn_out_g[l], lru_out_g[l], w_out[l],
            norm2_g[l], w_router[l], b_router[l])

        n_rows = T * TOP_K + N_EXPERTS * ROW_BLOCK
        pstart, block_e, fill, next_e, n_active = _routing_plan(cnt, n_rows // ROW_BLOCK)
        dest_flat = _dest_rows(route, pstart)
        x_rows = _dispatch(xn3, fill, dest_flat, n_rows)
        y_rows = _experts(x_rows, block_e, n_active, next_e, w_gate[l], b_gate[l], w_up[l], b_up[l],
                          w_down[l], b_down[l])
        x2 = _combine(y_rows, dest_flat, x1, gates, final_g)
    return x2.reshape(B, S, D)
```

```python
import functools
import math

import jax
import jax.numpy as jnp
import numpy as np
from jax import lax
from jax.experimental import pallas as pl
from jax.experimental.pallas import tpu as pltpu

F32 = jnp.float32
BF16 = jnp.bfloat16

GRID_W = 64
HEAD_DIM = 64
N_Q_HEADS = 8
N_KV_HEADS = 2
GQA_GROUP = N_Q_HEADS // N_KV_HEADS
ATTN_W = N_Q_HEADS * HEAD_DIM
KV_W = N_KV_HEADS * HEAD_DIM
LRU_BLOCKS = 8
LRU_C = 8.0
CONV_W = 4
CONV_PAD_L = 2
ROPE_THETA = 10000.0
ROPE_HALF = HEAD_DIM // 2
ROPE_M = ROPE_HALF // 2
N_EXPERTS = 32
TOP_K = 4
SWIGLU_ALPHA = 1.702
SWIGLU_LIMIT = 7.0
NORM_EPS = 1e-5
QK_EPS = 1e-6
LOG2_E = 1.4426950408889634
Q_SCALE = HEAD_DIM ** -0.5 * LOG2_E
SAFE_SCORE_LOG2 = 96.0

LANES = 128
SUBLANES = 8
BF16_SUBLANES = 16
PV_ROWS = HEAD_DIM + BF16_SUBLANES
VMEM_LIMIT = 48 * 1024 * 1024
EXPERT_VMEM_LIMIT = 56 * 1024 * 1024

TS_IN = 512
TQ = 256
TK = 256
KV_UNROLL = 16
HEADS_PER_STEP = 2
TC_LRU = 512
TS_OUT = 512
TS_DEST = 2048
ROW_BLOCK = 512
TS_DISP = 1024
TS_COMB = 512
ISSUE_UNROLL = 8


def _cparams(sem):
    return pltpu.CompilerParams(dimension_semantics=sem, vmem_limit_bytes=VMEM_LIMIT)


def _inproj_kernel(x_ref, g1_ref, wt_ref, w_ref, qg_ref, kg_ref, cos_ref, sin_ref, cost_ref, sint_ref,
                   q_ref, k_ref, v_ref, lx_ref, lg_ref, *, lru_w):
    x = x_ref[...]
    ms = jnp.mean(x * x, axis=-1, keepdims=True)
    xn = (x * lax.rsqrt(ms + NORM_EPS) * g1_ref[...]).astype(BF16)
    ht = lax.dot_general(wt_ref[...], xn, (((1,), (1,)), ((), ())), preferred_element_type=F32)
    h = jnp.dot(xn, w_ref[...], preferred_element_type=F32)

    qw = N_Q_HEADS * LANES
    kw = N_KV_HEADS * LANES
    cost = cost_ref[...]
    sint = sint_ref[...]
    row = lax.broadcasted_iota(jnp.int32, cost.shape, 0)
    first_half_t = (row % ROPE_HALF) < ROPE_M
    qg = qg_ref[...]
    for c in range(N_Q_HEADS):
        sl = slice(c * LANES, (c + 1) * LANES)
        xc = ht[sl]
        hms = jnp.sum(xc * xc, axis=0, keepdims=True) * (1.0 / HEAD_DIM)
        xc = xc * lax.rsqrt(hms + QK_EPS) * qg
        partner = jnp.where(first_half_t, pltpu.roll(xc, LANES - ROPE_M, 0), pltpu.roll(xc, ROPE_M, 0))
        q_ref[0, sl, :] = ((xc * cost + partner * sint) * Q_SCALE).astype(BF16)
    for c in range(N_KV_HEADS):
        sl = slice(c * LANES, (c + 1) * LANES)
        v_ref[0, sl, :] = jnp.where(row >= HEAD_DIM, 1.0, ht[qw + c * LANES: qw + (c + 1) * LANES]).astype(BF16)

    cos = cos_ref[...]
    sin = sin_ref[...]
    lane = lax.broadcasted_iota(jnp.int32, cos.shape, 1)
    first_half = (lane % ROPE_HALF) < ROPE_M
    for c in range(N_KV_HEADS):
        sl = slice(c * LANES, (c + 1) * LANES)
        xc = h[:, sl]
        hms = jnp.sum(xc * xc, axis=-1, keepdims=True) * (1.0 / HEAD_DIM)
        xc = xc * lax.rsqrt(hms + QK_EPS) * kg_ref[...]
        partner = jnp.where(first_half, pltpu.roll(xc, LANES - ROPE_M, 1), pltpu.roll(xc, ROPE_M, 1))
        k_ref[:, sl] = (xc * cos + partner * sin).astype(BF16)
    lx_ref[...] = h[:, kw: kw + lru_w]
    lg_ref[...] = h[:, kw + lru_w: kw + 2 * lru_w]


def _pad_heads(w, n_heads):
    lead = w.shape[:-1]
    w = w.reshape(lead + (n_heads, HEAD_DIM))
    w = jnp.pad(w, [(0, 0)] * len(lead) + [(0, 0), (0, LANES - HEAD_DIM)])
    return w.reshape(lead + (n_heads * LANES,))


def _rope_tables(S):
    t = np.arange(S)
    rows = (t // GRID_W).astype(np.float32)
    cols = (t % GRID_W).astype(np.float32)
    inv_freq = (ROPE_THETA ** (-np.arange(ROPE_M, dtype=np.float32) / ROPE_M)).astype(np.float32)
    ar = rows[:, None] * inv_freq[None, :]
    ac = cols[:, None] * inv_freq[None, :]
    cos = np.concatenate([np.cos(ar), np.cos(ar), np.cos(ac), np.cos(ac)], axis=-1)
    sin = np.concatenate([-np.sin(ar), np.sin(ar), -np.sin(ac), np.sin(ac)], axis=-1)
    pad = [(0, 0), (0, LANES - HEAD_DIM)]
    cos = np.pad(cos, pad).astype(np.float32)
    sin = np.pad(sin, pad).astype(np.float32)
    return cos, sin, np.ascontiguousarray(cos.T), np.ascontiguousarray(sin.T)


def _inproj(x2, norm1_g, w_in, q_norm_g, k_norm_g, S):
    T, D = x2.shape
    lru_w = (w_in.shape[1] - ATTN_W - 2 * KV_W) // 2
    o0, o1, o2 = ATTN_W, ATTN_W + KV_W, ATTN_W + 2 * KV_W
    w_t = jnp.concatenate([_pad_heads(w_in[:, :o0], N_Q_HEADS),
                           _pad_heads(w_in[:, o1:o2], N_KV_HEADS)], axis=1).T.astype(BF16)
    w_rest = jnp.concatenate([_pad_heads(w_in[:, o0:o1], N_KV_HEADS), w_in[:, o2:]],
                             axis=1).astype(BF16)
    qg = _pad_heads(q_norm_g.reshape(1, HEAD_DIM), 1).reshape(LANES, 1)
    kg = _pad_heads(k_norm_g.reshape(1, HEAD_DIM), 1)
    cos, sin, cos_t, sin_t = _rope_tables(S)
    ts = TS_IN
    n_s = S // ts
    qw, kw = N_Q_HEADS * LANES, N_KV_HEADS * LANES
    const = lambda i: (0, 0)
    tok = lambda i: (i, 0)
    pos = lambda i: (i % n_s, 0)
    pos_t = lambda i: (0, i % n_s)
    tposed = lambda i: (i // n_s, 0, i % n_s)
    return pl.pallas_call(
        functools.partial(_inproj_kernel, lru_w=lru_w),
        grid=(T // ts,),
        in_specs=[
            pl.BlockSpec((ts, D), tok),
            pl.BlockSpec((1, D), const),
            pl.BlockSpec(w_t.shape, const),
            pl.BlockSpec(w_rest.shape, const),
            pl.BlockSpec((LANES, 1), const),
            pl.BlockSpec((1, LANES), const),
            pl.BlockSpec((ts, LANES), pos),
            pl.BlockSpec((ts, LANES), pos),
            pl.BlockSpec((LANES, ts), pos_t),
            pl.BlockSpec((LANES, ts), pos_t),
        ],
        out_specs=[
            pl.BlockSpec((1, qw, ts), tposed),
            pl.BlockSpec((ts, kw), tok),
            pl.BlockSpec((1, kw, ts), tposed),
            pl.BlockSpec((ts, lru_w), tok),
            pl.BlockSpec((ts, lru_w), tok),
        ],
        out_shape=[
            jax.ShapeDtypeStruct((T // S, qw, S), BF16),
            jax.ShapeDtypeStruct((T, kw), BF16),
            jax.ShapeDtypeStruct((T // S, kw, S), BF16),
            jax.ShapeDtypeStruct((T, lru_w), F32),
            jax.ShapeDtypeStruct((T, lru_w), F32),
        ],
        compiler_params=_cparams(("parallel",)),
        name="inproj",
    )(x2, norm1_g.reshape(1, D), w_t, w_rest, qg, kg, cos, sin, cos_t, sin_t)


def _attn_kernel(qt_ref, k_ref, vt_ref, o_ref, acc_ref, s_ref, p_ref, *, tq, tk, n_kv, kv_unroll):
    hp = HEADS_PER_STEP
    spt = GQA_GROUP // hp
    acc_ref[...] = jnp.zeros(acc_ref.shape, F32)

    def scores(j, sp):
        kt = k_ref[0, pl.ds(pl.multiple_of(j * tk, tk), tk), :]
        out = []
        for u in range(hp):
            g = sp * hp + u
            s = jnp.dot(kt, qt_ref[0, g * LANES:(g + 1) * LANES, :], preferred_element_type=F32)
            out.append((s, jnp.max(s, axis=0, keepdims=True)))
        return out

    def softmax_stage(sc, ms, sp):
        out = []
        for u, (s, s_max) in enumerate(sc):
            h = sp * hp + u
            m_new = jnp.maximum(ms[h], s_max)
            out.append((jnp.exp2(ms[h] - m_new), jnp.exp2(s - m_new).astype(BF16)))
            ms[h] = m_new
        return out

    def pv_stage(j, sp, ap):
        vt = vt_ref[0, 0:PV_ROWS, pl.ds(pl.multiple_of(j * tk, tk), tk)]
        for u, (alpha, p) in enumerate(ap):
            g = sp * hp + u
            acc_ref[g] = alpha * acc_ref[g] + jnp.dot(vt, p, preferred_element_type=F32)

    ms = [jnp.full((1, tq), -jnp.inf, F32)] * GQA_GROUP
    ap = softmax_stage(scores(0, 0), ms, 0)
    sc = scores(min(1 // spt, n_kv - 1), 1 % spt)
    for u in range(hp):
        s_ref[u] = sc[u][0]
        p_ref[u] = ap[u][1]

    def body(it, carry):
        ms = list(carry[:GQA_GROUP])
        ap = [(carry[GQA_GROUP + u], p_ref[u]) for u in range(hp)]
        sc = [(s_ref[u], carry[GQA_GROUP + hp + u]) for u in range(hp)]
        for n in range(kv_unroll * spt):
            j = it * kv_unroll + n // spt
            j_next = jnp.minimum(it * kv_unroll + (n + 2) // spt, n_kv - 1)
            sc_next = scores(j_next, (n + 2) % spt)
            ap_next = softmax_stage(sc, ms, (n + 1) % spt)
            pv_stage(j, n % spt, ap)
            sc, ap = sc_next, ap_next
        for u in range(hp):
            s_ref[u] = sc[u][0]
            p_ref[u] = ap[u][1]
        return tuple(ms) + tuple(a for a, _ in ap) + tuple(m for _, m in sc)

    lax.fori_loop(0, n_kv // kv_unroll, body,
                  tuple(ms) + tuple(a for a, _ in ap) + tuple(m for _, m in sc))
    _attn_finalize(acc_ref, o_ref, tq)


def _attn_finalize(acc_ref, o_ref, tq):
    pad = jnp.zeros((LANES - HEAD_DIM, tq), F32)
    for g in range(GQA_GROUP):
        acc = acc_ref[g]
        o = acc[0:HEAD_DIM] / acc[HEAD_DIM:HEAD_DIM + 1, :]
        o_ref[0, :, g * LANES:(g + 1) * LANES] = jnp.concatenate([o, pad], axis=0).T.astype(BF16)


def _attn_bounded_kernel(qt_ref, k_ref, vt_ref, o_ref, acc_ref, s_ref, p_ref, *, tq, tk, n_kv, kv_unroll):
    hp = HEADS_PER_STEP
    spt = GQA_GROUP // hp
    acc_ref[...] = jnp.zeros(acc_ref.shape, F32)

    def scores(j, sp):
        kt = k_ref[0, pl.ds(pl.multiple_of(j * tk, tk), tk), :]
        return [jnp.dot(kt, qt_ref[0, (sp * hp + u) * LANES:(sp * hp + u + 1) * LANES, :],
                        preferred_element_type=F32) for u in range(hp)]

    def probs(sc):
        return [jnp.exp2(s).astype(BF16) for s in sc]

    def pv_stage(j, sp, ps):
        vt = vt_ref[0, 0:PV_ROWS, pl.ds(pl.multiple_of(j * tk, tk), tk)]
        for u, p in enumerate(ps):
            acc_ref[sp * hp + u] += jnp.dot(vt, p, preferred_element_type=F32)

    ps = probs(scores(0, 0))
    sc = scores(min(1 // spt, n_kv - 1), 1 % spt)
    for u in range(hp):
        s_ref[u] = sc[u]
        p_ref[u] = ps[u]

    def body(it, carry):
        ps = [p_ref[u] for u in range(hp)]
        sc = [s_ref[u] for u in range(hp)]
        for n in range(kv_unroll * spt):
            j = it * kv_unroll + n // spt
            j_next = jnp.minimum(it * kv_unroll + (n + 2) // spt, n_kv - 1)
            sc_next = scores(j_next, (n + 2) % spt)
            ps_next = probs(sc)
            pv_stage(j, n % spt, ps)
            sc, ps = sc_next, ps_next
        for u in range(hp):
            s_ref[u] = sc[u]
            p_ref[u] = ps[u]
        return carry

    lax.fori_loop(0, n_kv // kv_unroll, body, 0)
    _attn_finalize(acc_ref, o_ref, tq)


def _attention(qt, k, vt, *, kernel, B, S):
    tq = min(TQ, S)
    tk = min(TK, S)
    gw = GQA_GROUP * LANES
    return pl.pallas_call(
        functools.partial(kernel, tq=tq, tk=tk, n_kv=S // tk,
                          kv_unroll=math.gcd(S // tk, KV_UNROLL)),
        grid=(B, N_KV_HEADS, S // tq),
        in_specs=[
            pl.BlockSpec((1, gw, tq), lambda b, h, i: (b, h, i)),
            pl.BlockSpec((1, S, LANES), lambda b, h, i: (b, 0, h)),
            pl.BlockSpec((1, LANES, S), lambda b, h, i: (b, h, 0)),
        ],
        out_specs=pl.BlockSpec((1, tq, gw), lambda b, h, i: (b, i, h)),
        out_shape=jax.ShapeDtypeStruct((B, S, N_Q_HEADS * LANES), BF16),
        scratch_shapes=[pltpu.VMEM((GQA_GROUP, PV_ROWS, tq), F32),
                        pltpu.VMEM((HEADS_PER_STEP, tk, tq), F32),
                        pltpu.VMEM((HEADS_PER_STEP, tk, tq), BF16)],
        compiler_params=_cparams(("parallel", "parallel", "parallel")),
        name=kernel.__name__.strip("_"),
    )(qt, k, vt)


def _scan_chunk(a, b, h_in, reverse):
    n = a.shape[0]
    n_groups = n // SUBLANES
    a = a.reshape(n_groups, SUBLANES, LANES)
    b = b.reshape(n_groups, SUBLANES, LANES)
    sub = lax.broadcasted_iota(jnp.int32, a.shape, 1)
    d = 1
    while d < SUBLANES:
        if reverse:
            keep = sub < SUBLANES - d
            shift = SUBLANES - d
        else:
            keep = sub >= d
            shift = d
        a_sh = jnp.where(keep, pltpu.roll(a, shift, 1), 1.0)
        b_sh = jnp.where(keep, pltpu.roll(b, shift, 1), 0.0)
        b = a * b_sh + b
        a = a * a_sh
        d *= 2
    a = a.reshape(n, LANES)
    b = b.reshape(n, LANES)
    order = range(n_groups - 1, -1, -1) if reverse else range(n_groups)
    edge = h_in
    out = [None] * n_groups
    for v in order:
        rows = slice(v * SUBLANES, (v + 1) * SUBLANES)
        hv = b[rows] + a[rows] * jnp.broadcast_to(edge, (SUBLANES, LANES))
        out[v] = hv
        edge = hv[0:1] if reverse else hv[SUBLANES - 1:SUBLANES]
    return jnp.concatenate(out, axis=0), edge


def _lru_pad_input(u_ref, up_ref, S):
    zeros = jnp.zeros((SUBLANES, LANES), F32)
    up_ref[0:SUBLANES, :] = zeros
    up_ref[S + SUBLANES:S + 2 * SUBLANES, :] = zeros
    up_ref[SUBLANES:S + SUBLANES, :] = u_ref[0]


def _lru_gates(up_ref, cw_ref, cb_ref, w_ref, bias_ref, lam_ref, t0, tc, d):
    cw = cw_ref[...]
    xc = cb_ref[...]
    for j in range(CONV_W):
        xc = xc + up_ref[pl.ds(t0 + SUBLANES + j - CONV_PAD_L, tc), :] * cw[j:j + 1, :]
    gw = 2 * LANES
    g = jnp.dot(xc.astype(BF16), w_ref[0, :, d * gw:(d + 1) * gw],
                preferred_element_type=F32) + bias_ref[0, :, d * gw:(d + 1) * gw]
    r = jax.nn.sigmoid(g[:, :LANES])
    i = jax.nn.sigmoid(g[:, LANES:])
    a = jnp.exp(-LRU_C * r * jax.nn.softplus(-lam_ref[d:d + 1, :]))
    y = 1.0 - a * a
    b = jnp.where(y > 0.0, y * lax.rsqrt(y), 0.0) * i * xc
    return a, b


def _lru_kernel(u_ref, gate_ref, cw_ref, cb_ref, w_ref, bias_ref, lam_ref, o_ref,
                up_ref, hf_ref, *, S, tc):
    _lru_pad_input(u_ref, up_ref, S)
    n_chunks = S // tc
    params = (up_ref, cw_ref, cb_ref, w_ref, bias_ref, lam_ref)

    def fwd(c, h):
        t0 = pl.multiple_of(c * tc, tc)
        hc, h_last = _scan_chunk(*_lru_gates(*params, t0, tc, 0), h, False)
        hf_ref[pl.ds(t0, tc), :] = hc
        return h_last

    lax.fori_loop(0, n_chunks, fwd, jnp.zeros((1, LANES), F32))

    def bwd(ci, h):
        t0 = pl.multiple_of((n_chunks - 1 - ci) * tc, tc)
        hc, h_last = _scan_chunk(*_lru_gates(*params, t0, tc, 1), h, True)
        gate = gate_ref[0, pl.ds(t0, tc), :]
        o_ref[0, pl.ds(t0, tc), :] = (hf_ref[pl.ds(t0, tc), :] + hc) * jax.nn.gelu(gate)
        return h_last

    lax.fori_loop(0, n_chunks, bwd, jnp.zeros((1, LANES), F32))


def _block_diag_pairs(w):
    nb, bw, _ = w.shape
    w = w.reshape(nb // 2, 2, bw, bw)
    z = jnp.zeros_like(w[:, 0])
    top = jnp.concatenate([w[:, 0], z], axis=-1)
    bot = jnp.concatenate([z, w[:, 1]], axis=-1)
    return jnp.concatenate([top, bot], axis=-2)


def _lru_operands(conv_w, conv_b, wa, ba, wi, bi, lam):
    C = conv_b.shape[0]
    nc = C // LANES
    w = jnp.concatenate([_block_diag_pairs(wa[0]), _block_diag_pairs(wi[0]),
                         _block_diag_pairs(wa[1]), _block_diag_pairs(wi[1])], axis=-1).astype(BF16)
    bias = jnp.stack([ba[0].reshape(nc, LANES), bi[0].reshape(nc, LANES),
                      ba[1].reshape(nc, LANES), bi[1].reshape(nc, LANES)], axis=1)
    return conv_w, conv_b.reshape(1, C), w, bias.reshape(nc, 1, 4 * LANES), lam


def _lru_specs(S, unit):
    seq = lambda *g: (unit(*g)[0], 0, unit(*g)[1])
    chan = lambda *g: (0, unit(*g)[1])
    blk = lambda *g: (unit(*g)[1], 0, 0)
    in_specs = [
        pl.BlockSpec((1, S, LANES), seq),
        pl.BlockSpec((1, S, LANES), seq),
        pl.BlockSpec((CONV_W, LANES), chan),
        pl.BlockSpec((1, LANES), chan),
        pl.BlockSpec((1, LANES, 4 * LANES), blk),
        pl.BlockSpec((1, 1, 4 * LANES), blk),
        pl.BlockSpec((2, LANES), chan),
    ]
    return in_specs, pl.BlockSpec((1, S, LANES), seq)


def _lru(lru_x, lru_gate, lru_ops, B, S):
    C = lru_x.shape[-1]
    tc = min(TC_LRU, S)
    in_specs, out_spec = _lru_specs(S, lambda b, c: (b, c))
    return pl.pallas_call(
        functools.partial(_lru_kernel, S=S, tc=tc),
        grid=(B, C // LANES),
        in_specs=in_specs,
        out_specs=out_spec,
        out_shape=jax.ShapeDtypeStruct((B, S, C), F32),
        scratch_shapes=[
            pltpu.VMEM((S + 2 * SUBLANES, LANES), F32),
            pltpu.VMEM((S, LANES), F32),
        ],
        compiler_params=_cparams(("parallel", "parallel")),
        name="rglru",
    )(lru_x, lru_gate, *lru_ops)


def _mixers(qt, k, vt, lru_x, lru_gate, lru_ops, score_bound, B, S):
    attn = lax.cond(score_bound <= SAFE_SCORE_LOG2,
                    functools.partial(_attention, kernel=_attn_bounded_kernel, B=B, S=S),
                    functools.partial(_attention, kernel=_attn_kernel, B=B, S=S), qt, k, vt)
    return attn, _lru(lru_x, lru_gate, lru_ops, B, S)


def _rows_to_slabs(ref, x):
    n = x.shape[0]
    for s in range(SUBLANES):
        ref[pl.ds(s, n, stride=SUBLANES), :] = x[:, s * LANES:(s + 1) * LANES]


def _slabs_to_rows(ref, n):
    return jnp.concatenate([ref[pl.ds(s, n, stride=SUBLANES), :] for s in range(SUBLANES)], axis=1)


def _slab(ref, r):
    return ref.at[pl.ds(pl.multiple_of(r * SUBLANES, SUBLANES), SUBLANES)]


def _outproj_kernel(a_ref, l_ref, x_ref, ag_ref, lg_ref, wa_ref, wl_ref, g2_ref,
                    wr_ref, br_ref, tri_ref,
                    x1_ref, xn3_ref, route_ref, gates_ref, cnt_ref, carry_ref, *, attn_w, lru_w):
    step = pl.program_id(0)

    @pl.when(step == 0)
    def _():
        carry_ref[...] = jnp.zeros_like(carry_ref)

    a = a_ref[...].astype(F32)
    ams = jnp.sum(a * a, axis=-1, keepdims=True) * (1.0 / attn_w)
    an = a * lax.rsqrt(ams + NORM_EPS) * ag_ref[...]
    l = l_ref[...]
    lms = jnp.sum(l * l, axis=-1, keepdims=True) * (1.0 / lru_w)
    ln = l * lax.rsqrt(lms + NORM_EPS) * lg_ref[...]
    mix = (jnp.dot(an.astype(BF16), wa_ref[...], preferred_element_type=F32)
           + jnp.dot(ln.astype(BF16), wl_ref[...], preferred_element_type=F32))
    x1 = x_ref[...] + mix
    x1_ref[...] = x1
    ms = jnp.mean(x1 * x1, axis=-1, keepdims=True)
    xn = x1 * lax.rsqrt(ms + NORM_EPS) * g2_ref[...]
    _rows_to_slabs(xn3_ref, xn)

    logits = jnp.dot(xn.astype(BF16), wr_ref[...], preferred_element_type=F32) + br_ref[...]
    lane = lax.broadcasted_iota(jnp.int32, logits.shape, 1)
    neg = -jnp.inf
    work = jnp.where(lane < N_EXPERTS, logits, neg)
    sel = jnp.zeros(logits.shape, F32)
    idxs, vals = [], []
    for _ in range(TOP_K):
        m = jnp.max(work, axis=1, keepdims=True)
        idx = jnp.min(jnp.where(work == m, lane, LANES), axis=1, keepdims=True)
        hit = lane == idx
        work = jnp.where(hit, neg, work)
        sel = sel + hit.astype(F32)
        idxs.append(idx)
        vals.append(m)
    es = [jnp.exp(v - vals[0]) for v in vals]
    den = es[0] + es[1] + es[2] + es[3]

    prefix = jnp.dot(tri_ref[...], sel.astype(BF16), preferred_element_type=F32) + carry_ref[...]
    carry_ref[...] = carry_ref[...] + jnp.sum(sel, axis=0, keepdims=True)
    cnt_ref[...] = carry_ref[...]

    route = jnp.zeros(logits.shape, jnp.int32)
    gates = jnp.zeros(logits.shape, F32)
    for k in range(TOP_K):
        rank = jnp.sum(jnp.where(lane == idxs[k], prefix, 0.0), axis=1, keepdims=True).astype(jnp.int32)
        route = jnp.where(lane == k, idxs[k], route)
        route = jnp.where(lane == TOP_K + k, rank, route)
        gates = jnp.where(lane == k, es[k] / den, gates)
    route_ref[...] = route
    gates_ref[...] = gates


def _outproj_router(attn, lru, x2, attn_out_g, lru_out_g, w_out, norm2_g, w_router, b_router):
    T, D = x2.shape
    lru_w = lru.shape[-1]
    ts = min(TS_OUT, T)
    wa = w_out[:ATTN_W].reshape(N_Q_HEADS, HEAD_DIM, D)
    wa = jnp.pad(wa, ((0, 0), (0, LANES - HEAD_DIM), (0, 0))).reshape(N_Q_HEADS * LANES, D).astype(BF16)
    wl = w_out[ATTN_W:].astype(BF16)
    ag = _pad_heads(attn_out_g.reshape(1, ATTN_W), N_Q_HEADS)
    wr = jnp.pad(w_router, ((0, 0), (0, LANES - N_EXPERTS))).astype(BF16)
    br =jnp.pad(b_router.reshape(1, N_EXPERTS), ((0, 0), (0, LANES - N_EXPERTS)))
    tri = (jnp.arange(ts)[:, None] > jnp.arange(ts)[None, :]).astype(BF16)
    const = lambda i: (0, 0)
    tok = lambda i: (i, 0)
    aw = N_Q_HEADS * LANES
    return pl.pallas_call(
        functools.partial(_outproj_kernel, attn_w=ATTN_W, lru_w=lru_w),
        grid=(T // ts,),
        in_specs=[
            pl.BlockSpec((ts, aw), tok),
            pl.BlockSpec((ts, lru_w), tok),
            pl.BlockSpec((ts, D), tok),
            pl.BlockSpec((1, aw), const),
            pl.BlockSpec((1, lru_w), const),
            pl.BlockSpec((aw, D), const),
            pl.BlockSpec((lru_w, D), const),
            pl.BlockSpec((1, D), const),
            pl.BlockSpec((D, LANES), const),
            pl.BlockSpec((1, LANES), const),
            pl.BlockSpec((ts, ts), const),
        ],
        out_specs=[
            pl.BlockSpec((ts, D), tok),
            pl.BlockSpec((ts * SUBLANES, LANES), tok),
            pl.BlockSpec((ts, LANES), tok),
            pl.BlockSpec((ts, LANES), tok),
            pl.BlockSpec((1, LANES), const),
        ],
        out_shape=[
            jax.ShapeDtypeStruct((T, D), F32),
            jax.ShapeDtypeStruct((T * SUBLANES, LANES), F32),
            jax.ShapeDtypeStruct((T, LANES), jnp.int32),
            jax.ShapeDtypeStruct((T, LANES), F32),
            jax.ShapeDtypeStruct((1, LANES), F32),
        ],
        scratch_shapes=[pltpu.VMEM((1, LANES), F32)],
        compiler_params=_cparams(("arbitrary",)),
        name="outproj_router",
    )(attn, lru, x2, ag, lru_out_g.reshape(1, lru_w), wa, wl, norm2_g.reshape(1, D),
      wr, br, tri)


def _plan_kernel(cnt_ref, pstart_ref, plan_ref):
    cnt = cnt_ref[...]
    lane = lax.broadcasted_iota(jnp.int32, cnt.shape, 1)
    padded = jnp.floor((cnt + (ROW_BLOCK - 1)) * (1.0 / ROW_BLOCK)) * ROW_BLOCK
    pend = padded
    d = 1
    while d < N_EXPERTS:
        pend = pend + jnp.where(lane >= d, pltpu.roll(pend, d, 1), 0.0)
        d *= 2
    pstart_ref[...] = pend - padded
    total = jnp.max(pend, axis=1, keepdims=True)

    shape = plan_ref.shape
    lanes = lax.broadcasted_iota(jnp.int32, shape, 1)
    is_expert = lanes < N_EXPERTS
    start = lax.broadcasted_iota(jnp.int32, shape, 0).astype(F32) * ROW_BLOCK

    def groups_ending_by(row):
        return jnp.sum(jnp.where(jnp.logical_and(pend <= row, is_expert), 1.0, 0.0), axis=1, keepdims=True)

    block_e = jnp.minimum(groups_ending_by(start), N_EXPERTS - 1.0)
    tail = jnp.max(jnp.where(jnp.logical_and(jnp.logical_and(pend == start + ROW_BLOCK, padded > 0.0),
                                             is_expert), 1.0, 0.0), axis=1, keepdims=True)
    fill = jnp.maximum(tail, jnp.where(start[:, 0:1] >= total, 1.0, 0.0))
    group_end = jnp.sum(jnp.where(lanes.astype(F32) == block_e, pend, 0.0), axis=1, keepdims=True)
    next_e = jnp.where(group_end < total,
                       jnp.minimum(groups_ending_by(group_end), N_EXPERTS - 1.0), -1.0)
    plan = jnp.where(lanes == 0, block_e,
                     jnp.where(lanes == 1, fill,
                               jnp.where(lanes == 2, next_e, total * (1.0 / ROW_BLOCK))))
    plan_ref[...] = plan.astype(jnp.int32)


def _routing_plan(cnt, n_blocks):
    assert ROW_BLOCK & (ROW_BLOCK - 1) == 0, "exact f32 division by the row block size"
    rows = -(-n_blocks // SUBLANES) * SUBLANES
    pstart, plan = pl.pallas_call(
        _plan_kernel,
        out_shape=[jax.ShapeDtypeStruct((1, LANES), F32),
                   jax.ShapeDtypeStruct((rows, LANES), jnp.int32)],
        name="routing_plan",
    )(cnt)
    return pstart, plan[:n_blocks, 0], plan[:n_blocks, 1], plan[:n_blocks, 2], plan[0:1, 3]


def _dest_kernel(route_ref, pstart_ref, dest_ref):
    route = route_ref[...]
    lane = lax.broadcasted_iota(jnp.int32, route.shape, 1)
    pstart = pstart_ref[...]
    dest = jnp.zeros(route.shape, jnp.int32)
    for k in range(TOP_K):
        start = jnp.sum(jnp.where(lane == route[:, k:k + 1], pstart, 0.0), axis=1, keepdims=True)
        dest = jnp.where(lane == k, start.astype(jnp.int32) + route[:, TOP_K + k:TOP_K + k + 1], dest)
    dest_ref[...] = dest


def _dest_rows(route, pstart):
    T = route.shape[0]
    ts = math.gcd(TS_DEST, T)
    dest = pl.pallas_call(
        _dest_kernel,
        grid=(T // ts,),
        in_specs=[pl.BlockSpec((ts, LANES), lambda i: (i, 0)),
                  pl.BlockSpec((1, LANES), lambda i: (0, 0))],
        out_specs=pl.BlockSpec((ts, LANES), lambda i: (i, 0)),
        out_shape=jax.ShapeDtypeStruct((T, LANES), jnp.int32),
        compiler_params=_cparams(("parallel",)),
        name="dest_rows",
    )(route, pstart)
    return dest[:, :TOP_K].reshape(T * TOP_K)


def _dispatch_kernel(fill_ref, dest_ref, x_ref, out_hbm, zero_ref, sem, zero_sem, *, ts, n_blocks):
    block_slabs = ROW_BLOCK * SUBLANES

    def fill_copy(b):
        off = pl.multiple_of(b * block_slabs, block_slabs)
        return pltpu.make_async_copy(zero_ref, out_hbm.at[pl.ds(off, block_slabs)], zero_sem)

    @pl.when(pl.program_id(0) == 0)
    def _():
        zero_ref[...] = jnp.zeros(zero_ref.shape, F32)

        def start(b, carry):
            @pl.when(fill_ref[b] != 0)
            def _():
                fill_copy(b).start()
            return carry

        def wait(b, carry):
            @pl.when(fill_ref[b] != 0)
            def _():
                fill_copy(b).wait()
            return carry

        lax.fori_loop(0, n_blocks, start, 0)
        lax.fori_loop(0, n_blocks, wait, 0)

    def issue(i, carry):
        for j in range(ISSUE_UNROLL):
            r = i * ISSUE_UNROLL + j
            for k in range(TOP_K):
                d = dest_ref[r * TOP_K + k]
                pltpu.make_async_copy(_slab(x_ref, r), _slab(out_hbm, d), sem).start(priority=k % 2)
        return carry

    lax.fori_loop(0, ts // ISSUE_UNROLL, issue, 0)
    for k in range(TOP_K):
        pltpu.make_async_copy(x_ref, out_hbm.at[pl.ds(0, ts * SUBLANES)], sem).wait()


def _dispatch(xn_slabs, fill, dest_flat, n_rows):
    T = xn_slabs.shape[0] // SUBLANES
    ts = min(TS_DISP, T)
    grid_spec = pltpu.PrefetchScalarGridSpec(
        num_scalar_prefetch=1,
        grid=(T // ts,),
        in_specs=[
            pl.BlockSpec((ts * TOP_K,), lambda i, fl: (i,), memory_space=pltpu.SMEM),
            pl.BlockSpec((ts * SUBLANES, LANES), lambda i, fl: (i, 0)),
        ],
        out_specs=pl.BlockSpec(memory_space=pl.ANY),
        scratch_shapes=[pltpu.VMEM((ROW_BLOCK * SUBLANES, LANES), F32),
                        pltpu.SemaphoreType.DMA, pltpu.SemaphoreType.DMA],
    )
    return pl.pallas_call(
        functools.partial(_dispatch_kernel, ts=ts, n_blocks=n_rows // ROW_BLOCK),
        grid_spec=grid_spec,
        out_shape=jax.ShapeDtypeStruct((n_rows * SUBLANES, LANES), xn_slabs.dtype),
        compiler_params=_cparams(("arbitrary",)),
        name="dispatch",
    )(fill, dest_flat, xn_slabs)


def _expert_loop_kernel(be_ref, na_ref, nxt_ref, x_hbm, wg_hbm, bg_ref, wu_hbm, bu_ref, wd_hbm, bd_ref,
                        y_hbm, xbuf, ybuf, stage_ref, wb_ref, x_sems, y_sems, w_sems):
    w_hbm = (wg_hbm, wu_hbm, wd_hbm)
    block_slabs = ROW_BLOCK * SUBLANES
    n_active = na_ref[0]

    def rows(b):
        return pl.ds(pl.multiple_of(b * block_slabs, block_slabs), block_slabs)

    def x_copy(b, s):
        return pltpu.make_async_copy(x_hbm.at[rows(b)], xbuf.at[s], x_sems.at[s])

    def y_copy(b, s):
        return pltpu.make_async_copy(ybuf.at[s], y_hbm.at[rows(b)], y_sems.at[s])

    def fetch(expert, s, m):
        return pltpu.make_async_copy(w_hbm[m].at[expert], stage_ref.at[s, m], w_sems.at[s, m])

    x_copy(0, 0).start()
    for m in range(3):
        fetch(be_ref[0], 0, m).start()

    def body(b, wslot):
        s = b % 2
        e = be_ref[b]
        x_copy(b, s).wait()

        @pl.when(b + 1 < n_active)
        def _():
            x_copy(b + 1, 1 - s).start()

        first = jnp.logical_or(b == 0, e != be_ref[jnp.maximum(b - 1, 0)])

        @pl.when(first)
        def _():
            for m in range(3):
                fetch(e, wslot, m).wait()
                wb_ref[m] = stage_ref[wslot, m].astype(BF16)

            @pl.when(nxt_ref[b] >= 0)
            def _():
                for m in range(3):
                    fetch(nxt_ref[b], 1 - wslot, m).start()

        @pl.when(b >= 2)
        def _():
            y_copy(b - 2, s).wait()

        x = _slabs_to_rows(xbuf.at[s], ROW_BLOCK).astype(BF16)
        g = jnp.dot(x, wb_ref[0], preferred_element_type=F32) + bg_ref[e]
        u = jnp.dot(x, wb_ref[1], preferred_element_type=F32) + bu_ref[e]
        g = jnp.minimum(g, SWIGLU_LIMIT)
        u = jnp.clip(u, -SWIGLU_LIMIT, SWIGLU_LIMIT)
        glu = g * jax.nn.sigmoid(SWIGLU_ALPHA * g)
        y = jnp.dot(((u + 1.0) * glu).astype(BF16), wb_ref[2], preferred_element_type=F32) + bd_ref[e]
        _rows_to_slabs(ybuf.at[s], y)
        y_copy(b, s).start()
        return jnp.where(first, 1 - wslot, wslot)

    lax.fori_loop(0, n_active, body, 0)

    @pl.when(n_active >= 2)
    def _():
        y_copy(n_active - 2, n_active % 2).wait()

    y_copy(n_active - 1, (n_active - 1) % 2).wait()


def _experts(x_rows, block_e, n_active, next_e, w_gate, b_gate, w_up, b_up, w_down, b_down):
    E, D, FF = w_gate.shape
    assert D == FF, "the three expert matrices share one staging shape"
    block_slabs = ROW_BLOCK * SUBLANES
    whole = lambda i, be, na, nx: (0, 0, 0)

    grid_spec = pltpu.PrefetchScalarGridSpec(
        num_scalar_prefetch=3,
        grid=(1,),
        in_specs=[
            pl.BlockSpec(memory_space=pl.ANY),
            pl.BlockSpec(memory_space=pl.ANY),
            pl.BlockSpec((E, 1, FF), whole),
            pl.BlockSpec(memory_space=pl.ANY),
            pl.BlockSpec((E, 1, FF), whole),
            pl.BlockSpec(memory_space=pl.ANY),
            pl.BlockSpec((E, 1, D), whole),
        ],
        out_specs=pl.BlockSpec(memory_space=pl.ANY),
        scratch_shapes=[
            pltpu.VMEM((2, block_slabs, LANES), F32),
            pltpu.VMEM((2, block_slabs, LANES), F32),
            pltpu.VMEM((2, 3, D, FF), F32),
            pltpu.VMEM((3, D, FF), BF16),
            pltpu.SemaphoreType.DMA((2,)),
            pltpu.SemaphoreType.DMA((2,)),
            pltpu.SemaphoreType.DMA((2, 3)),
        ],
    )
    return pl.pallas_call(
        _expert_loop_kernel,
        grid_spec=grid_spec,
        out_shape=jax.ShapeDtypeStruct(x_rows.shape, F32),
        input_output_aliases={3: 0},
        compiler_params=pltpu.CompilerParams(dimension_semantics=("arbitrary",),
                                             vmem_limit_bytes=EXPERT_VMEM_LIMIT),
        name="experts",
    )(block_e, n_active, next_e, x_rows, w_gate, b_gate.reshape(E, 1, FF), w_up,
      b_up.reshape(E, 1, FF), w_down, b_down.reshape(E, 1, D))


def _combine_kernel(dest_ref, dest_next_ref, y_hbm, x1_ref, gates_ref, fg_ref, o_ref, bufs, sems,
                    *, ts, n_steps):
    i = pl.program_id(0)
    slot = i % 2

    def gather_tile(d_ref, s):
        def issue(it, carry):
            for j in range(ISSUE_UNROLL):
                r = it * ISSUE_UNROLL + j
                for k in range(TOP_K):
                    d = d_ref[r * TOP_K + k]
                    pltpu.make_async_copy(_slab(y_hbm, d), _slab(bufs.at[s, k], r),
                                          sems.at[s]).start(priority=k % 2)
            return carry

        lax.fori_loop(0, ts // ISSUE_UNROLL, issue, 0)

    @pl.when(i == 0)
    def _():
        gather_tile(dest_ref, 0)

    @pl.when(i + 1 < n_steps)
    def _():
        gather_tile(dest_next_ref, 1 - slot)

    for k in range(TOP_K):
        pltpu.make_async_copy(y_hbm.at[pl.ds(0, ts * SUBLANES)], bufs.at[slot, k], sems.at[slot]).wait()

    acc = x1_ref[...]
    gates = gates_ref[...]
    for k in range(TOP_K):
        acc = acc + _slabs_to_rows(bufs.at[slot, k], ts) * gates[:, k:k + 1]
    ms = jnp.mean(acc * acc, axis=-1, keepdims=True)
    o_ref[...] = acc * lax.rsqrt(ms + NORM_EPS) * fg_ref[...]


def _combine(y_rows, dest_flat, x1, gates, final_g):
    T, D = x1.shape
    ts = min(TS_COMB, T)
    n_steps = T // ts
    tok = lambda i: (i, 0)
    return pl.pallas_call(
        functools.partial(_combine_kernel, ts=ts, n_steps=n_steps),
        grid=(n_steps,),
        in_specs=[
            pl.BlockSpec((ts * TOP_K,), lambda i: (i,), memory_space=pltpu.SMEM),
            pl.BlockSpec((ts * TOP_K,), lambda i: (jnp.minimum(i + 1, n_steps - 1),),
                         memory_space=pltpu.SMEM),
            pl.BlockSpec(memory_space=pl.ANY),
            pl.BlockSpec((ts, D), tok),
            pl.BlockSpec((ts, LANES), tok),
            pl.BlockSpec((1, D), lambda i: (0, 0)),
        ],
        out_specs=pl.BlockSpec((ts, D), tok),
        out_shape=jax.ShapeDtypeStruct((T, D), F32),
        scratch_shapes=[pltpu.VMEM((2, TOP_K, ts * SUBLANES, LANES), F32),
                        pltpu.SemaphoreType.DMA((2,))],
        compiler_params=_cparams(("arbitrary",)),
        name="combine",
    )(dest_flat, dest_flat, y_rows, x1, gates, final_g.reshape(1, D))


def kernel(x, norm1_g, w_in, q_norm_g, k_norm_g, conv_w, conv_b, lru_wa, lru_ba, lru_wi, lru_bi,
           lru_lam, attn_out_g, lru_out_g, w_out, norm2_g, w_router, b_router, w_gate, b_gate,
           w_up, b_up, w_down, b_down, final_g):
    B, S, D = x.shape
    T = B * S
    assert w_in.shape[0] == 1, "single-layer trunk: the final norm is fused into the layer's combine"
    assert D == SUBLANES * LANES, "a token row is moved as one (8, 128) f32 slab"
    assert S % max(TS_IN, TQ, TK, TC_LRU) == 0 and S % GRID_W == 0, "sequence tiles must divide S"
    x2 = x.reshape(T, D)
    for l in range(1):
        qt, k, vt, lru_x, lru_gate = _inproj(x2, norm1_g[l], w_in[l], q_norm_g[l], k_norm_g[l], S)
        score_bound = (HEAD_DIM * Q_SCALE * jnp.max(jnp.abs(q_norm_g[l]))
                       * jnp.max(jnp.abs(k_norm_g[l])))
        lru_ops = _lru_operands(conv_w[l], conv_b[l], lru_wa[l], lru_ba[l], lru_wi[l], lru_bi[l],
                                lru_lam[l])
        attn, lru = _mixers(qt, k.reshape(B, S, -1), vt, lru_x.reshape(B, S, -1),
                            lru_gate.reshape(B, S, -1), lru_ops, score_bound, B, S)
        x1, xn3, route, gates, cnt = _outproj_router(
            attn.reshape(T, -1), lru.reshape(T, -1), x2, attn_out_g[l], lru_out_g[l], w_out[l],
            norm2_g[l], w_router[l], b_router[l])

        n_rows = T * TOP_K + N_EXPERTS * ROW_BLOCK
        pstart, block_e, fill, next_e, n_active = _routing_plan(cnt, n_rows // ROW_BLOCK)
        dest_flat = _dest_rows(route, pstart)
        x_rows = _dispatch(xn3, fill, dest_flat, n_rows)
        y_rows = _experts(x_rows, block_e, n_active, next_e, w_gate[l], b_gate[l], w_up[l], b_up[l],
                          w_down[l], b_down[l])
        x2 = _combine(y_rows, dest_flat, x1, gates, final_g)
    return x2.reshape(B, S, D)
```

```python
import functools
import math

import jax
import jax.numpy as jnp
import numpy as np
from jax import lax
from jax.experimental import pallas as pl
from jax.experimental.pallas import tpu as pltpu

F32 = jnp.float32
BF16 = jnp.bfloat16

GRID_W = 64
HEAD_DIM = 64
N_Q_HEADS = 8
N_KV_HEADS = 2
GQA_GROUP = N_Q_HEADS // N_KV_HEADS
ATTN_W = N_Q_HEADS * HEAD_DIM
KV_W = N_KV_HEADS * HEAD_DIM
LRU_BLOCKS = 8
LRU_C = 8.0
CONV_W = 4
CONV_PAD_L = 2
ROPE_THETA = 10000.0
ROPE_HALF = HEAD_DIM // 2
ROPE_M = ROPE_HALF // 2
N_EXPERTS = 32
TOP_K = 4
SWIGLU_ALPHA = 1.702
SWIGLU_LIMIT = 7.0
NORM_EPS = 1e-5
QK_EPS = 1e-6
LOG2_E = 1.4426950408889634
Q_SCALE = HEAD_DIM ** -0.5 * LOG2_E
SAFE_SCORE_LOG2 = 96.0

LANES = 128
SUBLANES = 8
BF16_SUBLANES = 16
PV_ROWS = HEAD_DIM + BF16_SUBLANES
VMEM_LIMIT = 48 * 1024 * 1024
EXPERT_VMEM_LIMIT = 56 * 1024 * 1024

TS_IN = 512
TQ = 256
TK = 256
KV_UNROLL = 32
HEADS_PER_STEP = 2
TC_LRU = 512
TS_OUT = 512
TS_DEST = 2048
ROW_BLOCK = 512
TS_DISP = 1024
TS_COMB = 512
ISSUE_UNROLL = 8


def _cparams(sem):
    return pltpu.CompilerParams(dimension_semantics=sem, vmem_limit_bytes=VMEM_LIMIT)


def _inproj_kernel(x_ref, g1_ref, wt_ref, w_ref, qg_ref, kg_ref, cos_ref, sin_ref, cost_ref, sint_ref,
                   q_ref, k_ref, v_ref, lx_ref, lg_ref, *, lru_w):
    x = x_ref[...]
    ms = jnp.mean(x * x, axis=-1, keepdims=True)
    xn = (x * lax.rsqrt(ms + NORM_EPS) * g1_ref[...]).astype(BF16)
    ht = lax.dot_general(wt_ref[...], xn, (((1,), (1,)), ((), ())), preferred_element_type=F32)
    h = jnp.dot(xn, w_ref[...], preferred_element_type=F32)

    qw = N_Q_HEADS * LANES
    kw = N_KV_HEADS * LANES
    cost = cost_ref[...]
    sint = sint_ref[...]
    row = lax.broadcasted_iota(jnp.int32, cost.shape, 0)
    first_half_t = (row % ROPE_HALF) < ROPE_M
    qg = qg_ref[...]
    for c in range(N_Q_HEADS):
        sl = slice(c * LANES, (c + 1) * LANES)
        xc = ht[sl]
        hms = jnp.sum(xc * xc, axis=0, keepdims=True) * (1.0 / HEAD_DIM)
        xc = xc * lax.rsqrt(hms + QK_EPS) * qg
        partner = jnp.where(first_half_t, pltpu.roll(xc, LANES - ROPE_M, 0), pltpu.roll(xc, ROPE_M, 0))
        q_ref[0, sl, :] = ((xc * cost + partner * sint) * Q_SCALE).astype(BF16)
    for c in range(N_KV_HEADS):
        sl = slice(c * LANES, (c + 1) * LANES)
        v_ref[0, sl, :] = jnp.where(row >= HEAD_DIM, 1.0, ht[qw + c * LANES: qw + (c + 1) * LANES]).astype(BF16)

    cos = cos_ref[...]
    sin = sin_ref[...]
    lane = lax.broadcasted_iota(jnp.int32, cos.shape, 1)
    first_half = (lane % ROPE_HALF) < ROPE_M
    for c in range(N_KV_HEADS):
        sl = slice(c * LANES, (c + 1) * LANES)
        xc = h[:, sl]
        hms = jnp.sum(xc * xc, axis=-1, keepdims=True) * (1.0 / HEAD_DIM)
        xc = xc * lax.rsqrt(hms + QK_EPS) * kg_ref[...]
        partner = jnp.where(first_half, pltpu.roll(xc, LANES - ROPE_M, 1), pltpu.roll(xc, ROPE_M, 1))
        k_ref[:, sl] = (xc * cos + partner * sin).astype(BF16)
    lx_ref[...] = h[:, kw: kw + lru_w]
    lg_ref[...] = h[:, kw + lru_w: kw + 2 * lru_w]


def _pad_heads(w, n_heads):
    lead = w.shape[:-1]
    w = w.reshape(lead + (n_heads, HEAD_DIM))
    w = jnp.pad(w, [(0, 0)] * len(lead) + [(0, 0), (0, LANES - HEAD_DIM)])
    return w.reshape(lead + (n_heads * LANES,))


def _rope_tables(S):
    t = np.arange(S)
    rows = (t // GRID_W).astype(np.float32)
    cols = (t % GRID_W).astype(np.float32)
    inv_freq = (ROPE_THETA ** (-np.arange(ROPE_M, dtype=np.float32) / ROPE_M)).astype(np.float32)
    ar = rows[:, None] * inv_freq[None, :]
    ac = cols[:, None] * inv_freq[None, :]
    cos = np.concatenate([np.cos(ar), np.cos(ar), np.cos(ac), np.cos(ac)], axis=-1)
    sin = np.concatenate([-np.sin(ar), np.sin(ar), -np.sin(ac), np.sin(ac)], axis=-1)
    pad = [(0, 0), (0, LANES - HEAD_DIM)]
    cos = np.pad(cos, pad).astype(np.float32)
    sin = np.pad(sin, pad).astype(np.float32)
    return cos, sin, np.ascontiguousarray(cos.T), np.ascontiguousarray(sin.T)


def _inproj(x2, norm1_g, w_in, q_norm_g, k_norm_g, S):
    T, D = x2.shape
    lru_w = (w_in.shape[1] - ATTN_W - 2 * KV_W) // 2
    o0, o1, o2 = ATTN_W, ATTN_W + KV_W, ATTN_W + 2 * KV_W
    w_t = jnp.concatenate([_pad_heads(w_in[:, :o0], N_Q_HEADS),
                           _pad_heads(w_in[:, o1:o2], N_KV_HEADS)], axis=1).T.astype(BF16)
    w_rest = jnp.concatenate([_pad_heads(w_in[:, o0:o1], N_KV_HEADS), w_in[:, o2:]],
                             axis=1).astype(BF16)
    qg = _pad_heads(q_norm_g.reshape(1, HEAD_DIM), 1).reshape(LANES, 1)
    kg = _pad_heads(k_norm_g.reshape(1, HEAD_DIM), 1)
    cos, sin, cos_t, sin_t = _rope_tables(S)
    ts = TS_IN
    n_s = S // ts
    qw, kw = N_Q_HEADS * LANES, N_KV_HEADS * LANES
    const = lambda i: (0, 0)
    tok = lambda i: (i, 0)
    pos = lambda i: (i % n_s, 0)
    pos_t = lambda i: (0, i % n_s)
    tposed = lambda i: (i // n_s, 0, i % n_s)
    return pl.pallas_call(
        functools.partial(_inproj_kernel, lru_w=lru_w),
        grid=(T // ts,),
        in_specs=[
            pl.BlockSpec((ts, D), tok),
            pl.BlockSpec((1, D), const),
            pl.BlockSpec(w_t.shape, const),
            pl.BlockSpec(w_rest.shape, const),
            pl.BlockSpec((LANES, 1), const),
            pl.BlockSpec((1, LANES), const),
            pl.BlockSpec((ts, LANES), pos),
            pl.BlockSpec((ts, LANES), pos),
            pl.BlockSpec((LANES, ts), pos_t),
            pl.BlockSpec((LANES, ts), pos_t),
        ],
        out_specs=[
            pl.BlockSpec((1, qw, ts), tposed),
            pl.BlockSpec((ts, kw), tok),
            pl.BlockSpec((1, kw, ts), tposed),
            pl.BlockSpec((ts, lru_w), tok),
            pl.BlockSpec((ts, lru_w), tok),
        ],
        out_shape=[
            jax.ShapeDtypeStruct((T // S, qw, S), BF16),
            jax.ShapeDtypeStruct((T, kw), BF16),
            jax.ShapeDtypeStruct((T // S, kw, S), BF16),
            jax.ShapeDtypeStruct((T, lru_w), F32),
            jax.ShapeDtypeStruct((T, lru_w), F32),
        ],
        compiler_params=_cparams(("parallel",)),
        name="inproj",
    )(x2, norm1_g.reshape(1, D), w_t, w_rest, qg, kg, cos, sin, cos_t, sin_t)


def _attn_kernel(qt_ref, k_ref, vt_ref, o_ref, acc_ref, s_ref, p_ref, *, tq, tk, n_kv, kv_unroll):
    hp = HEADS_PER_STEP
    spt = GQA_GROUP // hp
    acc_ref[...] = jnp.zeros(acc_ref.shape, F32)

    def scores(j, sp):
        kt = k_ref[0, pl.ds(pl.multiple_of(j * tk, tk), tk), :]
        out = []
        for u in range(hp):
            g = sp * hp + u
            s = jnp.dot(kt, qt_ref[0, g * LANES:(g + 1) * LANES, :], preferred_element_type=F32)
            out.append((s, jnp.max(s, axis=0, keepdims=True)))
        return out

    def softmax_stage(sc, ms, sp):
        out = []
        for u, (s, s_max) in enumerate(sc):
            h = sp * hp + u
            m_new = jnp.maximum(ms[h], s_max)
            out.append((jnp.exp2(ms[h] - m_new), jnp.exp2(s - m_new).astype(BF16)))
            ms[h] = m_new
        return out

    def pv_stage(j, sp, ap):
        vt = vt_ref[0, 0:PV_ROWS, pl.ds(pl.multiple_of(j * tk, tk), tk)]
        for u, (alpha, p) in enumerate(ap):
            g = sp * hp + u
            acc_ref[g] = alpha * acc_ref[g] + jnp.dot(vt, p, preferred_element_type=F32)

    ms = [jnp.full((1, tq), -jnp.inf, F32)] * GQA_GROUP
    ap = softmax_stage(scores(0, 0), ms, 0)
    sc = scores(min(1 // spt, n_kv - 1), 1 % spt)
    for u in range(hp):
        s_ref[u] = sc[u][0]
        p_ref[u] = ap[u][1]

    def body(it, carry):
        ms = list(carry[:GQA_GROUP])
        ap = [(carry[GQA_GROUP + u], p_ref[u]) for u in range(hp)]
        sc = [(s_ref[u], carry[GQA_GROUP + hp + u]) for u in range(hp)]
        for n in range(kv_unroll * spt):
            j = it * kv_unroll + n // spt
            j_next = jnp.minimum(it * kv_unroll + (n + 2) // spt, n_kv - 1)
            sc_next = scores(j_next, (n + 2) % spt)
            ap_next = softmax_stage(sc, ms, (n + 1) % spt)
            pv_stage(j, n % spt, ap)
            sc, ap = sc_next, ap_next
        for u in range(hp):
            s_ref[u] = sc[u][0]
            p_ref[u] = ap[u][1]
        return tuple(ms) + tuple(a for a, _ in ap) + tuple(m for _, m in sc)

    lax.fori_loop(0, n_kv // kv_unroll, body,
                  tuple(ms) + tuple(a for a, _ in ap) + tuple(m for _, m in sc))
    _attn_finalize(acc_ref, o_ref, tq)


def _attn_finalize(acc_ref, o_ref, tq):
    pad = jnp.zeros((LANES - HEAD_DIM, tq), F32)
    for g in range(GQA_GROUP):
        acc = acc_ref[g]
        o = acc[0:HEAD_DIM] / acc[HEAD_DIM:HEAD_DIM + 1, :]
        o_ref[0, :, g * LANES:(g + 1) * LANES] = jnp.concatenate([o, pad], axis=0).T.astype(BF16)


def _attn_bounded_kernel(qt_ref, k_ref, vt_ref, o_ref, acc_ref, s_ref, p_ref, *, tq, tk, n_kv, kv_unroll):
    hp = HEADS_PER_STEP
    spt = GQA_GROUP // hp
    acc_ref[...] = jnp.zeros(acc_ref.shape, F32)

    def scores(j, sp):
        kt = k_ref[0, pl.ds(pl.multiple_of(j * tk, tk), tk), :]
        return [jnp.dot(kt, qt_ref[0, (sp * hp + u) * LANES:(sp * hp + u + 1) * LANES, :],
                        preferred_element_type=F32) for u in range(hp)]

    def probs(sc):
        return [jnp.exp2(s).astype(BF16) for s in sc]

    def pv_stage(j, sp, ps):
        vt = vt_ref[0, 0:PV_ROWS, pl.ds(pl.multiple_of(j * tk, tk), tk)]
        for u, p in enumerate(ps):
            acc_ref[sp * hp + u] += jnp.dot(vt, p, preferred_element_type=F32)

    ps = probs(scores(0, 0))
    sc = scores(min(1 // spt, n_kv - 1), 1 % spt)
    for u in range(hp):
        s_ref[u] = sc[u]
        p_ref[u] = ps[u]

    def body(it, carry):
        ps = [p_ref[u] for u in range(hp)]
        sc = [s_ref[u] for u in range(hp)]
        for n in range(kv_unroll * spt):
            j = it * kv_unroll + n // spt
            j_next = jnp.minimum(it * kv_unroll + (n + 2) // spt, n_kv - 1)
            sc_next = scores(j_next, (n + 2) % spt)
            ps_next = probs(sc)
            pv_stage(j, n % spt, ps)
            sc, ps = sc_next, ps_next
        for u in range(hp):
            s_ref[u] = sc[u]
            p_ref[u] = ps[u]
        return carry

    lax.fori_loop(0, n_kv // kv_unroll, body, 0)
    _attn_finalize(acc_ref, o_ref, tq)


def _attention(qt, k, vt, *, kernel, B, S):
    tq = min(TQ, S)
    tk = min(TK, S)
    gw = GQA_GROUP * LANES
    return pl.pallas_call(
        functools.partial(kernel, tq=tq, tk=tk, n_kv=S // tk,
                          kv_unroll=math.gcd(S // tk, KV_UNROLL)),
        grid=(B, N_KV_HEADS, S // tq),
        in_specs=[
            pl.BlockSpec((1, gw, tq), lambda b, h, i: (b, h, i)),
            pl.BlockSpec((1, S, LANES), lambda b, h, i: (b, 0, h)),
            pl.BlockSpec((1, LANES, S), lambda b, h, i: (b, h, 0)),
        ],
        out_specs=pl.BlockSpec((1, tq, gw), lambda b, h, i: (b, i, h)),
        out_shape=jax.ShapeDtypeStruct((B, S, N_Q_HEADS * LANES), BF16),
        scratch_shapes=[pltpu.VMEM((GQA_GROUP, PV_ROWS, tq), F32),
                        pltpu.VMEM((HEADS_PER_STEP, tk, tq), F32),
                        pltpu.VMEM((HEADS_PER_STEP, tk, tq), BF16)],
        compiler_params=_cparams(("parallel", "parallel", "parallel")),
        name=kernel.__name__.strip("_"),
    )(qt, k, vt)


def _scan_chunk(a, b, h_in, reverse):
    n = a.shape[0]
    n_groups = n // SUBLANES
    a = a.reshape(n_groups, SUBLANES, LANES)
    b = b.reshape(n_groups, SUBLANES, LANES)
    sub = lax.broadcasted_iota(jnp.int32, a.shape, 1)
    d = 1
    while d < SUBLANES:
        if reverse:
            keep = sub < SUBLANES - d
            shift = SUBLANES - d
        else:
            keep = sub >= d
            shift = d
        a_sh = jnp.where(keep, pltpu.roll(a, shift, 1), 1.0)
        b_sh = jnp.where(keep, pltpu.roll(b, shift, 1), 0.0)
        b = a * b_sh + b
        a = a * a_sh
        d *= 2
    a = a.reshape(n, LANES)
    b = b.reshape(n, LANES)
    order = range(n_groups - 1, -1, -1) if reverse else range(n_groups)
    edge = h_in
    out = [None] * n_groups
    for v in order:
        rows = slice(v * SUBLANES, (v + 1) * SUBLANES)
        hv = b[rows] + a[rows] * jnp.broadcast_to(edge, (SUBLANES, LANES))
        out[v] = hv
        edge = hv[0:1] if reverse else hv[SUBLANES - 1:SUBLANES]
    return jnp.concatenate(out, axis=0), edge


def _lru_pad_input(u_ref, up_ref, S):
    zeros = jnp.zeros((SUBLANES, LANES), F32)
    up_ref[0:SUBLANES, :] = zeros
    up_ref[S + SUBLANES:S + 2 * SUBLANES, :] = zeros
    up_ref[SUBLANES:S + SUBLANES, :] = u_ref[0]


def _lru_gates(up_ref, cw_ref, cb_ref, w_ref, bias_ref, lam_ref, t0, tc, d):
    cw = cw_ref[...]
    xc = cb_ref[...]
    for j in range(CONV_W):
        xc = xc + up_ref[pl.ds(t0 + SUBLANES + j - CONV_PAD_L, tc), :] * cw[j:j + 1, :]
    gw = 2 * LANES
    g = jnp.dot(xc.astype(BF16), w_ref[0, :, d * gw:(d + 1) * gw],
                preferred_element_type=F32) + bias_ref[0, :, d * gw:(d + 1) * gw]
    r = jax.nn.sigmoid(g[:, :LANES])
    i = jax.nn.sigmoid(g[:, LANES:])
    a = jnp.exp(-LRU_C * r * jax.nn.softplus(-lam_ref[d:d + 1, :]))
    y = 1.0 - a * a
    b = jnp.where(y > 0.0, y * lax.rsqrt(y), 0.0) * i * xc
    return a, b


def _lru_kernel(u_ref, gate_ref, cw_ref, cb_ref, w_ref, bias_ref, lam_ref, o_ref,
                up_ref, hf_ref, *, S, tc):
    _lru_pad_input(u_ref, up_ref, S)
    n_chunks = S // tc
    params = (up_ref, cw_ref, cb_ref, w_ref, bias_ref, lam_ref)

    def fwd(c, h):
        t0 = pl.multiple_of(c * tc, tc)
        hc, h_last = _scan_chunk(*_lru_gates(*params, t0, tc, 0), h, False)
        hf_ref[pl.ds(t0, tc), :] = hc
        return h_last

    lax.fori_loop(0, n_chunks, fwd, jnp.zeros((1, LANES), F32))

    def bwd(ci, h):
        t0 = pl.multiple_of((n_chunks - 1 - ci) * tc, tc)
        hc, h_last = _scan_chunk(*_lru_gates(*params, t0, tc, 1), h, True)
        gate = gate_ref[0, pl.ds(t0, tc), :]
        o_ref[0, pl.ds(t0, tc), :] = (hf_ref[pl.ds(t0, tc), :] + hc) * jax.nn.gelu(gate)
        return h_last

    lax.fori_loop(0, n_chunks, bwd, jnp.zeros((1, LANES), F32))


def _block_diag_pairs(w):
    nb, bw, _ = w.shape
    w = w.reshape(nb // 2, 2, bw, bw)
    z = jnp.zeros_like(w[:, 0])
    top = jnp.concatenate([w[:, 0], z], axis=-1)
    bot = jnp.concatenate([z, w[:, 1]], axis=-1)
    return jnp.concatenate([top, bot], axis=-2)


def _lru_operands(conv_w, conv_b, wa, ba, wi, bi, lam):
    C = conv_b.shape[0]
    nc = C // LANES
    w = jnp.concatenate([_block_diag_pairs(wa[0]), _block_diag_pairs(wi[0]),
                         _block_diag_pairs(wa[1]), _block_diag_pairs(wi[1])], axis=-1).astype(BF16)
    bias = jnp.stack([ba[0].reshape(nc, LANES), bi[0].reshape(nc, LANES),
                      ba[1].reshape(nc, LANES), bi[1].reshape(nc, LANES)], axis=1)
    return conv_w, conv_b.reshape(1, C), w, bias.reshape(nc, 1, 4 * LANES), lam


def _lru_specs(S, unit):
    seq = lambda *g: (unit(*g)[0], 0, unit(*g)[1])
    chan = lambda *g: (0, unit(*g)[1])
    blk = lambda *g: (unit(*g)[1], 0, 0)
    in_specs = [
        pl.BlockSpec((1, S, LANES), seq),
        pl.BlockSpec((1, S, LANES), seq),
        pl.BlockSpec((CONV_W, LANES), chan),
        pl.BlockSpec((1, LANES), chan),
        pl.BlockSpec((1, LANES, 4 * LANES), blk),
        pl.BlockSpec((1, 1, 4 * LANES), blk),
        pl.BlockSpec((2, LANES), chan),
    ]
    return in_specs, pl.BlockSpec((1, S, LANES), seq)


def _lru(lru_x, lru_gate, lru_ops, B, S):
    C = lru_x.shape[-1]
    tc = min(TC_LRU, S)
    in_specs, out_spec = _lru_specs(S, lambda b, c: (b, c))
    return pl.pallas_call(
        functools.partial(_lru_kernel, S=S, tc=tc),
        grid=(B, C // LANES),
        in_specs=in_specs,
        out_specs=out_spec,
        out_shape=jax.ShapeDtypeStruct((B, S, C), F32),
        scratch_shapes=[
            pltpu.VMEM((S + 2 * SUBLANES, LANES), F32),
            pltpu.VMEM((S, LANES), F32),
        ],
        compiler_params=_cparams(("parallel", "parallel")),
        name="rglru",
    )(lru_x, lru_gate, *lru_ops)


def _mixers(qt, k, vt, lru_x, lru_gate, lru_ops, score_bound, B, S):
    attn = lax.cond(score_bound <= SAFE_SCORE_LOG2,
                    functools.partial(_attention, kernel=_attn_bounded_kernel, B=B, S=S),
                    functools.partial(_attention, kernel=_attn_kernel, B=B, S=S), qt, k, vt)
    return attn, _lru(lru_x, lru_gate, lru_ops, B, S)


def _rows_to_slabs(ref, x):
    n = x.shape[0]
    for s in range(SUBLANES):
        ref[pl.ds(s, n, stride=SUBLANES), :] = x[:, s * LANES:(s + 1) * LANES]


def _slabs_to_rows(ref, n):
    return jnp.concatenate([ref[pl.ds(s, n, stride=SUBLANES), :] for s in range(SUBLANES)], axis=1)


def _slab(ref, r):
    return ref.at[pl.ds(pl.multiple_of(r * SUBLANES, SUBLANES), SUBLANES)]


def _outproj_kernel(a_ref, l_ref, x_ref, ag_ref, lg_ref, wa_ref, wl_ref, g2_ref,
                    wr_ref, br_ref, tri_ref,
                    x1_ref, xn3_ref, route_ref, gates_ref, cnt_ref, carry_ref, *, attn_w, lru_w):
    step = pl.program_id(0)

    @pl.when(step == 0)
    def _():
        carry_ref[...] = jnp.zeros_like(carry_ref)

    a = a_ref[...].astype(F32)
    ams = jnp.sum(a * a, axis=-1, keepdims=True) * (1.0 / attn_w)
    an = a * lax.rsqrt(ams + NORM_EPS) * ag_ref[...]
    l = l_ref[...]
    lms = jnp.sum(l * l, axis=-1, keepdims=True) * (1.0 / lru_w)
    ln = l * lax.rsqrt(lms + NORM_EPS) * lg_ref[...]
    mix = (jnp.dot(an.astype(BF16), wa_ref[...], preferred_element_type=F32)
           + jnp.dot(ln.astype(BF16), wl_ref[...], preferred_element_type=F32))
    x1 = x_ref[...] + mix
    x1_ref[...] = x1
    ms = jnp.mean(x1 * x1, axis=-1, keepdims=True)
    xn = x1 * lax.rsqrt(ms + NORM_EPS) * g2_ref[...]
    _rows_to_slabs(xn3_ref, xn)

    logits = jnp.dot(xn.astype(BF16), wr_ref[...], preferred_element_type=F32) + br_ref[...]
    lane = lax.broadcasted_iota(jnp.int32, logits.shape, 1)
    neg = -jnp.inf
    work = jnp.where(lane < N_EXPERTS, logits, neg)
    sel = jnp.zeros(logits.shape, F32)
    idxs, vals = [], []
    for _ in range(TOP_K):
        m = jnp.max(work, axis=1, keepdims=True)
        idx = jnp.min(jnp.where(work == m, lane, LANES), axis=1, keepdims=True)
        hit = lane == idx
        work = jnp.where(hit, neg, work)
        sel = sel + hit.astype(F32)
        idxs.append(idx)
        vals.append(m)
    es = [jnp.exp(v - vals[0]) for v in vals]
    den = es[0] + es[1] + es[2] + es[3]

    prefix = jnp.dot(tri_ref[...], sel.astype(BF16), preferred_element_type=F32) + carry_ref[...]
    carry_ref[...] = carry_ref[...] + jnp.sum(sel, axis=0, keepdims=True)
    cnt_ref[...] = carry_ref[...]

    route = jnp.zeros(logits.shape, jnp.int32)
    gates = jnp.zeros(logits.shape, F32)
    for k in range(TOP_K):
        rank = jnp.sum(jnp.where(lane == idxs[k], prefix, 0.0), axis=1, keepdims=True).astype(jnp.int32)
        route = jnp.where(lane == k, idxs[k], route)
        route = jnp.where(lane == TOP_K + k, rank, route)
        gates = jnp.where(lane == k, es[k] / den, gates)
    route_ref[...] = route
    gates_ref[...] = gates


def _outproj_router(attn, lru, x2, attn_out_g, lru_out_g, w_out, norm2_g, w_router, b_router):
    T, D = x2.shape
    lru_w = lru.shape[-1]
    ts = min(TS_OUT, T)
    wa = w_out[:ATTN_W].reshape(N_Q_HEADS, HEAD_DIM, D)
    wa = jnp.pad(wa, ((0, 0), (0, LANES - HEAD_DIM), (0, 0))).reshape(N_Q_HEADS * LANES, D).astype(BF16)
    wl = w_out[ATTN_W:].astype(BF16)
    ag = _pad_heads(attn_out_g.reshape(1, ATTN_W), N_Q_HEADS)
    wr = jnp.pad(w_router, ((0, 0), (0, LANES - N_EXPERTS))).astype(BF16)
    br =jnp.pad(b_router.reshape(1, N_EXPERTS), ((0, 0), (0, LANES - N_EXPERTS)))
    tri = (jnp.arange(ts)[:, None] > jnp.arange(ts)[None, :]).astype(BF16)
    const = lambda i: (0, 0)
    tok = lambda i: (i, 0)
    aw = N_Q_HEADS * LANES
    return pl.pallas_call(
        functools.partial(_outproj_kernel, attn_w=ATTN_W, lru_w=lru_w),
        grid=(T // ts,),
        in_specs=[
            pl.BlockSpec((ts, aw), tok),
            pl.BlockSpec((ts, lru_w), tok),
            pl.BlockSpec((ts, D), tok),
            pl.BlockSpec((1, aw), const),
            pl.BlockSpec((1, lru_w), const),
            pl.BlockSpec((aw, D), const),
            pl.BlockSpec((lru_w, D), const),
            pl.BlockSpec((1, D), const),
            pl.BlockSpec((D, LANES), const),
            pl.BlockSpec((1, LANES), const),
            pl.BlockSpec((ts, ts), const),
        ],
        out_specs=[
            pl.BlockSpec((ts, D), tok),
            pl.BlockSpec((ts * SUBLANES, LANES), tok),
            pl.BlockSpec((ts, LANES), tok),
            pl.BlockSpec((ts, LANES), tok),
            pl.BlockSpec((1, LANES), const),
        ],
        out_shape=[
            jax.ShapeDtypeStruct((T, D), F32),
            jax.ShapeDtypeStruct((T * SUBLANES, LANES), F32),
            jax.ShapeDtypeStruct((T, LANES), jnp.int32),
            jax.ShapeDtypeStruct((T, LANES), F32),
            jax.ShapeDtypeStruct((1, LANES), F32),
        ],
        scratch_shapes=[pltpu.VMEM((1, LANES), F32)],
        compiler_params=_cparams(("arbitrary",)),
        name="outproj_router",
    )(attn, lru, x2, ag, lru_out_g.reshape(1, lru_w), wa, wl, norm2_g.reshape(1, D),
      wr, br, tri)


def _plan_kernel(cnt_ref, pstart_ref, plan_ref):
    cnt = cnt_ref[...]
    lane = lax.broadcasted_iota(jnp.int32, cnt.shape, 1)
    padded = jnp.floor((cnt + (ROW_BLOCK - 1)) * (1.0 / ROW_BLOCK)) * ROW_BLOCK
    pend = padded
    d = 1
    while d < N_EXPERTS:
        pend = pend + jnp.where(lane >= d, pltpu.roll(pend, d, 1), 0.0)
        d *= 2
    pstart_ref[...] = pend - padded
    total = jnp.max(pend, axis=1, keepdims=True)

    shape = plan_ref.shape
    lanes = lax.broadcasted_iota(jnp.int32, shape, 1)
    is_expert = lanes < N_EXPERTS
    start = lax.broadcasted_iota(jnp.int32, shape, 0).astype(F32) * ROW_BLOCK

    def groups_ending_by(row):
        return jnp.sum(jnp.where(jnp.logical_and(pend <= row, is_expert), 1.0, 0.0), axis=1, keepdims=True)

    block_e = jnp.minimum(groups_ending_by(start), N_EXPERTS - 1.0)
    tail = jnp.max(jnp.where(jnp.logical_and(jnp.logical_and(pend == start + ROW_BLOCK, padded > 0.0),
                                             is_expert), 1.0, 0.0), axis=1, keepdims=True)
    fill = jnp.maximum(tail, jnp.where(start[:, 0:1] >= total, 1.0, 0.0))
    group_end = jnp.sum(jnp.where(lanes.astype(F32) == block_e, pend, 0.0), axis=1, keepdims=True)
    next_e = jnp.where(group_end < total,
                       jnp.minimum(groups_ending_by(group_end), N_EXPERTS - 1.0), -1.0)
    plan = jnp.where(lanes == 0, block_e,
                     jnp.where(lanes == 1, fill,
                               jnp.where(lanes == 2, next_e, total * (1.0 / ROW_BLOCK))))
    plan_ref[...] = plan.astype(jnp.int32)


def _routing_plan(cnt, n_blocks):
    assert ROW_BLOCK & (ROW_BLOCK - 1) == 0, "exact f32 division by the row block size"
    rows = -(-n_blocks // SUBLANES) * SUBLANES
    pstart, plan = pl.pallas_call(
        _plan_kernel,
        out_shape=[jax.ShapeDtypeStruct((1, LANES), F32),
                   jax.ShapeDtypeStruct((rows, LANES), jnp.int32)],
        name="routing_plan",
    )(cnt)
    return pstart, plan[:n_blocks, 0], plan[:n_blocks, 1], plan[:n_blocks, 2], plan[0:1, 3]


def _dest_kernel(route_ref, pstart_ref, dest_ref):
    route = route_ref[...]
    lane = lax.broadcasted_iota(jnp.int32, route.shape, 1)
    pstart = pstart_ref[...]
    dest = jnp.zeros(route.shape, jnp.int32)
    for k in range(TOP_K):
        start = jnp.sum(jnp.where(lane == route[:, k:k + 1], pstart, 0.0), axis=1, keepdims=True)
        dest = jnp.where(lane == k, start.astype(jnp.int32) + route[:, TOP_K + k:TOP_K + k + 1], dest)
    dest_ref[...] = dest


def _dest_rows(route, pstart):
    T = route.shape[0]
    ts = math.gcd(TS_DEST, T)
    dest = pl.pallas_call(
        _dest_kernel,
        grid=(T // ts,),
        in_specs=[pl.BlockSpec((ts, LANES), lambda i: (i, 0)),
                  pl.BlockSpec((1, LANES), lambda i: (0, 0))],
        out_specs=pl.BlockSpec((ts, LANES), lambda i: (i, 0)),
        out_shape=jax.ShapeDtypeStruct((T, LANES), jnp.int32),
        compiler_params=_cparams(("parallel",)),
        name="dest_rows",
    )(route, pstart)
    return dest[:, :TOP_K].reshape(T * TOP_K)


def _dispatch_kernel(fill_ref, dest_ref, x_ref, out_hbm, zero_ref, sem, zero_sem, *, ts, n_blocks):
    block_slabs = ROW_BLOCK * SUBLANES

    def fill_copy(b):
        off = pl.multiple_of(b * block_slabs, block_slabs)
        return pltpu.make_async_copy(zero_ref, out_hbm.at[pl.ds(off, block_slabs)], zero_sem)

    @pl.when(pl.program_id(0) == 0)
    def _():
        zero_ref[...] = jnp.zeros(zero_ref.shape, F32)

        def start(b, carry):
            @pl.when(fill_ref[b] != 0)
            def _():
                fill_copy(b).start()
            return carry

        def wait(b, carry):
            @pl.when(fill_ref[b] != 0)
            def _():
                fill_copy(b).wait()
            return carry

        lax.fori_loop(0, n_blocks, start, 0)
        lax.fori_loop(0, n_blocks, wait, 0)

    def issue(i, carry):
        for j in range(ISSUE_UNROLL):
            r = i * ISSUE_UNROLL + j
            for k in range(TOP_K):
                d = dest_ref[r * TOP_K + k]
                pltpu.make_async_copy(_slab(x_ref, r), _slab(out_hbm, d), sem).start(priority=k % 2)
        return carry

    lax.fori_loop(0, ts // ISSUE_UNROLL, issue, 0)
    for k in range(TOP_K):
        pltpu.make_async_copy(x_ref, out_hbm.at[pl.ds(0, ts * SUBLANES)], sem).wait()


def _dispatch(xn_slabs, fill, dest_flat, n_rows):
    T = xn_slabs.shape[0] // SUBLANES
    ts = min(TS_DISP, T)
    grid_spec = pltpu.PrefetchScalarGridSpec(
        num_scalar_prefetch=1,
        grid=(T // ts,),
        in_specs=[
            pl.BlockSpec((ts * TOP_K,), lambda i, fl: (i,), memory_space=pltpu.SMEM),
            pl.BlockSpec((ts * SUBLANES, LANES), lambda i, fl: (i, 0)),
        ],
        out_specs=pl.BlockSpec(memory_space=pl.ANY),
        scratch_shapes=[pltpu.VMEM((ROW_BLOCK * SUBLANES, LANES), F32),
                        pltpu.SemaphoreType.DMA, pltpu.SemaphoreType.DMA],
    )
    return pl.pallas_call(
        functools.partial(_dispatch_kernel, ts=ts, n_blocks=n_rows // ROW_BLOCK),
        grid_spec=grid_spec,
        out_shape=jax.ShapeDtypeStruct((n_rows * SUBLANES, LANES), xn_slabs.dtype),
        compiler_params=_cparams(("arbitrary",)),
        name="dispatch",
    )(fill, dest_flat, xn_slabs)


def _expert_loop_kernel(be_ref, na_ref, nxt_ref, x_hbm, wg_hbm, bg_ref, wu_hbm, bu_ref, wd_hbm, bd_ref,
                        y_hbm, xbuf, ybuf, stage_ref, wb_ref, x_sems, y_sems, w_sems):
    w_hbm = (wg_hbm, wu_hbm, wd_hbm)
    block_slabs = ROW_BLOCK * SUBLANES
    n_active = na_ref[0]

    def rows(b):
        return pl.ds(pl.multiple_of(b * block_slabs, block_slabs), block_slabs)

    def x_copy(b, s):
        return pltpu.make_async_copy(x_hbm.at[rows(b)], xbuf.at[s], x_sems.at[s])

    def y_copy(b, s):
        return pltpu.make_async_copy(ybuf.at[s], y_hbm.at[rows(b)], y_sems.at[s])

    def fetch(expert, s, m):
        return pltpu.make_async_copy(w_hbm[m].at[expert], stage_ref.at[s, m], w_sems.at[s, m])

    x_copy(0, 0).start()
    for m in range(3):
        fetch(be_ref[0], 0, m).start()

    def body(b, wslot):
        s = b % 2
        e = be_ref[b]
        x_copy(b, s).wait()

        @pl.when(b + 1 < n_active)
        def _():
            x_copy(b + 1, 1 - s).start()

        first = jnp.logical_or(b == 0, e != be_ref[jnp.maximum(b - 1, 0)])

        @pl.when(first)
        def _():
            for m in range(3):
                fetch(e, wslot, m).wait()
                wb_ref[m] = stage_ref[wslot, m].astype(BF16)

            @pl.when(nxt_ref[b] >= 0)
            def _():
                for m in range(3):
                    fetch(nxt_ref[b], 1 - wslot, m).start()

        @pl.when(b >= 2)
        def _():
            y_copy(b - 2, s).wait()

        x = _slabs_to_rows(xbuf.at[s], ROW_BLOCK).astype(BF16)
        g = jnp.dot(x, wb_ref[0], preferred_element_type=F32) + bg_ref[e]
        u = jnp.dot(x, wb_ref[1], preferred_element_type=F32) + bu_ref[e]
        g = jnp.minimum(g, SWIGLU_LIMIT)
        u = jnp.clip(u, -SWIGLU_LIMIT, SWIGLU_LIMIT)
        glu = g * jax.nn.sigmoid(SWIGLU_ALPHA * g)
        y = jnp.dot(((u + 1.0) * glu).astype(BF16), wb_ref[2], preferred_element_type=F32) + bd_ref[e]
        _rows_to_slabs(ybuf.at[s], y)
        y_copy(b, s).start()
        return jnp.where(first, 1 - wslot, wslot)

    lax.fori_loop(0, n_active, body, 0)

    @pl.when(n_active >= 2)
    def _():
        y_copy(n_active - 2, n_active % 2).wait()

    y_copy(n_active - 1, (n_active - 1) % 2).wait()


def _experts(x_rows, block_e, n_active, next_e, w_gate, b_gate, w_up, b_up, w_down, b_down):
    E, D, FF = w_gate.shape
    assert D == FF, "the three expert matrices share one staging shape"
    block_slabs = ROW_BLOCK * SUBLANES
    whole = lambda i, be, na, nx: (0, 0, 0)

    grid_spec = pltpu.PrefetchScalarGridSpec(
        num_scalar_prefetch=3,
        grid=(1,),
        in_specs=[
            pl.BlockSpec(memory_space=pl.ANY),
            pl.BlockSpec(memory_space=pl.ANY),
            pl.BlockSpec((E, 1, FF), whole),
            pl.BlockSpec(memory_space=pl.ANY),
            pl.BlockSpec((E, 1, FF), whole),
            pl.BlockSpec(memory_space=pl.ANY),
            pl.BlockSpec((E, 1, D), whole),
        ],
        out_specs=pl.BlockSpec(memory_space=pl.ANY),
        scratch_shapes=[
            pltpu.VMEM((2, block_slabs, LANES), F32),
            pltpu.VMEM((2, block_slabs, LANES), F32),
            pltpu.VMEM((2, 3, D, FF), F32),
            pltpu.VMEM((3, D, FF), BF16),
            pltpu.SemaphoreType.DMA((2,)),
            pltpu.SemaphoreType.DMA((2,)),
            pltpu.SemaphoreType.DMA((2, 3)),
        ],
    )
    return pl.pallas_call(
        _expert_loop_kernel,
        grid_spec=grid_spec,
        out_shape=jax.ShapeDtypeStruct(x_rows.shape, F32),
        input_output_aliases={3: 0},
        compiler_params=pltpu.CompilerParams(dimension_semantics=("arbitrary",),
                                             vmem_limit_bytes=EXPERT_VMEM_LIMIT),
        name="experts",
    )(block_e, n_active, next_e, x_rows, w_gate, b_gate.reshape(E, 1, FF), w_up,
      b_up.reshape(E, 1, FF), w_down, b_down.reshape(E, 1, D))


def _combine_kernel(dest_ref, dest_next_ref, y_hbm, x1_ref, gates_ref, fg_ref, o_ref, bufs, sems,
                    *, ts, n_steps):
    i = pl.program_id(0)
    slot = i % 2

    def gather_tile(d_ref, s):
        def issue(it, carry):
            for j in range(ISSUE_UNROLL):
                r = it * ISSUE_UNROLL + j
                for k in range(TOP_K):
                    d = d_ref[r * TOP_K + k]
                    pltpu.make_async_copy(_slab(y_hbm, d), _slab(bufs.at[s, k], r),
                                          sems.at[s]).start(priority=k % 2)
            return carry

        lax.fori_loop(0, ts // ISSUE_UNROLL, issue, 0)

    @pl.when(i == 0)
    def _():
        gather_tile(dest_ref, 0)

    @pl.when(i + 1 < n_steps)
    def _():
        gather_tile(dest_next_ref, 1 - slot)

    for k in range(TOP_K):
        pltpu.make_async_copy(y_hbm.at[pl.ds(0, ts * SUBLANES)], bufs.at[slot, k], sems.at[slot]).wait()

    acc = x1_ref[...]
    gates = gates_ref[...]
    for k in range(TOP_K):
        acc = acc + _slabs_to_rows(bufs.at[slot, k], ts) * gates[:, k:k + 1]
    ms = jnp.mean(acc * acc, axis=-1, keepdims=True)
    o_ref[...] = acc * lax.rsqrt(ms + NORM_EPS) * fg_ref[...]


def _combine(y_rows, dest_flat, x1, gates, final_g):
    T, D = x1.shape
    ts = min(TS_COMB, T)
    n_steps = T // ts
    tok = lambda i: (i, 0)
    return pl.pallas_call(
        functools.partial(_combine_kernel, ts=ts, n_steps=n_steps),
        grid=(n_steps,),
        in_specs=[
            pl.BlockSpec((ts * TOP_K,), lambda i: (i,), memory_space=pltpu.SMEM),
            pl.BlockSpec((ts * TOP_K,), lambda i: (jnp.minimum(i + 1, n_steps - 1),),
                         memory_space=pltpu.SMEM),
            pl.BlockSpec(memory_space=pl.ANY),
            pl.BlockSpec((ts, D), tok),
            pl.BlockSpec((ts, LANES), tok),
            pl.BlockSpec((1, D), lambda i: (0, 0)),
        ],
        out_specs=pl.BlockSpec((ts, D), tok),
        out_shape=jax.ShapeDtypeStruct((T, D), F32),
        scratch_shapes=[pltpu.VMEM((2, TOP_K, ts * SUBLANES, LANES), F32),
                        pltpu.SemaphoreType.DMA((2,))],
        compiler_params=_cparams(("arbitrary",)),
        name="combine",
    )(dest_flat, dest_flat, y_rows, x1, gates, final_g.reshape(1, D))


def kernel(x, norm1_g, w_in, q_norm_g, k_norm_g, conv_w, conv_b, lru_wa, lru_ba, lru_wi, lru_bi,
           lru_lam, attn_out_g, lru_out_g, w_out, norm2_g, w_router, b_router, w_gate, b_gate,
           w_up, b_up, w_down, b_down, final_g):
    B, S, D = x.shape
    T = B * S
    assert w_in.shape[0] == 1, "single-layer trunk: the final norm is fused into the layer's combine"
    assert D == SUBLANES * LANES, "a token row is moved as one (8, 128) f32 slab"
    assert S % max(TS_IN, TQ, TK, TC_LRU) == 0 and S % GRID_W == 0, "sequence tiles must divide S"
    x2 = x.reshape(T, D)
    for l in range(1):
        qt, k, vt, lru_x, lru_gate = _inproj(x2, norm1_g[l], w_in[l], q_norm_g[l], k_norm_g[l], S)
        score_bound = (HEAD_DIM * Q_SCALE * jnp.max(jnp.abs(q_norm_g[l]))
                       * jnp.max(jnp.abs(k_norm_g[l])))
        lru_ops = _lru_operands(conv_w[l], conv_b[l], lru_wa[l], lru_ba[l], lru_wi[l], lru_bi[l],
                                lru_lam[l])
        attn, lru = _mixers(qt, k.reshape(B, S, -1), vt, lru_x.reshape(B, S, -1),
                            lru_gate.reshape(B, S, -1), lru_ops, score_bound, B, S)
        x1, xn3, route, gates, cnt = _outproj_router(
            attn.reshape(T, -1), lru.reshape(T, -1), x2, attn_out_g[l], lru_out_g[l], w_out[l],
            norm2_g[l], w_router[l], b_router[l])

        n_rows = T * TOP_K + N_EXPERTS * ROW_BLOCK
        pstart, block_e, fill, next_e, n_active = _routing_plan(cnt, n_rows // ROW_BLOCK)
        dest_flat = _dest_rows(route, pstart)
        x_rows = _dispatch(xn3, fill, dest_flat, n_rows)
        y_rows = _experts(x_rows, block_e, n_active, next_e, w_gate[l], b_gate[l], w_up[l], b_up[l],
                          w_down[l], b_down[l])
        x2 = _combine(y_rows, dest_flat, x1, gates, final_g)
    return x2.reshape(B, S, D)
```

```python
import functools
import math

import jax
import jax.numpy as jnp
import numpy as np
from jax import lax
from jax.experimental import pallas as pl
from jax.experimental.pallas import tpu as pltpu

F32 = jnp.float32
BF16 = jnp.bfloat16

GRID_W = 64
HEAD_DIM = 64
N_Q_HEADS = 8
N_KV_HEADS = 2
GQA_GROUP = N_Q_HEADS // N_KV_HEADS
ATTN_W = N_Q_HEADS * HEAD_DIM
KV_W = N_KV_HEADS * HEAD_DIM
LRU_BLOCKS = 8
LRU_C = 8.0
CONV_W = 4
CONV_PAD_L = 2
ROPE_THETA = 10000.0
ROPE_HALF = HEAD_DIM // 2
ROPE_M = ROPE_HALF // 2
N_EXPERTS = 32
TOP_K = 4
SWIGLU_ALPHA = 1.702
SWIGLU_LIMIT = 7.0
NORM_EPS = 1e-5
QK_EPS = 1e-6
LOG2_E = 1.4426950408889634
Q_SCALE = HEAD_DIM ** -0.5 * LOG2_E
SAFE_SCORE_LOG2 = 96.0

LANES = 128
SUBLANES = 8
BF16_SUBLANES = 16
PV_ROWS = HEAD_DIM + BF16_SUBLANES
VMEM_LIMIT = 48 * 1024 * 1024
EXPERT_VMEM_LIMIT = 56 * 1024 * 1024

TS_IN = 512
TQ = 256
TK = 256
KV_UNROLL = 32
HEADS_PER_STEP = 2
TC_LRU = 512
LRU_UNROLL = 4
TS_OUT = 512
TS_DEST = 2048
ROW_BLOCK = 512
TS_DISP = 1024
TS_COMB = 512
ISSUE_UNROLL = 8


def _cparams(sem):
    return pltpu.CompilerParams(dimension_semantics=sem, vmem_limit_bytes=VMEM_LIMIT)


def _inproj_kernel(x_ref, g1_ref, wt_ref, w_ref, qg_ref, kg_ref, cos_ref, sin_ref, cost_ref, sint_ref,
                   q_ref, k_ref, v_ref, lx_ref, lg_ref, *, lru_w):
    x = x_ref[...]
    ms = jnp.mean(x * x, axis=-1, keepdims=True)
    xn = (x * lax.rsqrt(ms + NORM_EPS) * g1_ref[...]).astype(BF16)
    ht = lax.dot_general(wt_ref[...], xn, (((1,), (1,)), ((), ())), preferred_element_type=F32)
    h = jnp.dot(xn, w_ref[...], preferred_element_type=F32)

    qw = N_Q_HEADS * LANES
    kw = N_KV_HEADS * LANES
    cost = cost_ref[...]
    sint = sint_ref[...]
    row = lax.broadcasted_iota(jnp.int32, cost.shape, 0)
    first_half_t = (row % ROPE_HALF) < ROPE_M
    qg = qg_ref[...]
    for c in range(N_Q_HEADS):
        sl = slice(c * LANES, (c + 1) * LANES)
        xc = ht[sl]
        hms = jnp.sum(xc * xc, axis=0, keepdims=True) * (1.0 / HEAD_DIM)
        xc = xc * lax.rsqrt(hms + QK_EPS) * qg
        partner = jnp.where(first_half_t, pltpu.roll(xc, LANES - ROPE_M, 0), pltpu.roll(xc, ROPE_M, 0))
        q_ref[0, sl, :] = ((xc * cost + partner * sint) * Q_SCALE).astype(BF16)
    for c in range(N_KV_HEADS):
        sl = slice(c * LANES, (c + 1) * LANES)
        v_ref[0, sl, :] = jnp.where(row >= HEAD_DIM, 1.0, ht[qw + c * LANES: qw + (c + 1) * LANES]).astype(BF16)

    cos = cos_ref[...]
    sin = sin_ref[...]
    lane = lax.broadcasted_iota(jnp.int32, cos.shape, 1)
    first_half = (lane % ROPE_HALF) < ROPE_M
    for c in range(N_KV_HEADS):
        sl = slice(c * LANES, (c + 1) * LANES)
        xc = h[:, sl]
        hms = jnp.sum(xc * xc, axis=-1, keepdims=True) * (1.0 / HEAD_DIM)
        xc = xc * lax.rsqrt(hms + QK_EPS) * kg_ref[...]
        partner = jnp.where(first_half, pltpu.roll(xc, LANES - ROPE_M, 1), pltpu.roll(xc, ROPE_M, 1))
        k_ref[:, sl] = (xc * cos + partner * sin).astype(BF16)
    lx_ref[...] = h[:, kw: kw + lru_w]
    lg_ref[...] = h[:, kw + lru_w: kw + 2 * lru_w]


def _pad_heads(w, n_heads):
    lead = w.shape[:-1]
    w = w.reshape(lead + (n_heads, HEAD_DIM))
    w = jnp.pad(w, [(0, 0)] * len(lead) + [(0, 0), (0, LANES - HEAD_DIM)])
    return w.reshape(lead + (n_heads * LANES,))


def _rope_tables(S):
    t = np.arange(S)
    rows = (t // GRID_W).astype(np.float32)
    cols = (t % GRID_W).astype(np.float32)
    inv_freq = (ROPE_THETA ** (-np.arange(ROPE_M, dtype=np.float32) / ROPE_M)).astype(np.float32)
    ar = rows[:, None] * inv_freq[None, :]
    ac = cols[:, None] * inv_freq[None, :]
    cos = np.concatenate([np.cos(ar), np.cos(ar), np.cos(ac), np.cos(ac)], axis=-1)
    sin = np.concatenate([-np.sin(ar), np.sin(ar), -np.sin(ac), np.sin(ac)], axis=-1)
    pad = [(0, 0), (0, LANES - HEAD_DIM)]
    cos = np.pad(cos, pad).astype(np.float32)
    sin = np.pad(sin, pad).astype(np.float32)
    return cos, sin, np.ascontiguousarray(cos.T), np.ascontiguousarray(sin.T)


def _inproj(x2, norm1_g, w_in, q_norm_g, k_norm_g, S):
    T, D = x2.shape
    lru_w = (w_in.shape[1] - ATTN_W - 2 * KV_W) // 2
    o0, o1, o2 = ATTN_W, ATTN_W + KV_W, ATTN_W + 2 * KV_W
    w_t = jnp.concatenate([_pad_heads(w_in[:, :o0], N_Q_HEADS),
                           _pad_heads(w_in[:, o1:o2], N_KV_HEADS)], axis=1).T.astype(BF16)
    w_rest = jnp.concatenate([_pad_heads(w_in[:, o0:o1], N_KV_HEADS), w_in[:, o2:]],
                             axis=1).astype(BF16)
    qg = _pad_heads(q_norm_g.reshape(1, HEAD_DIM), 1).reshape(LANES, 1)
    kg = _pad_heads(k_norm_g.reshape(1, HEAD_DIM), 1)
    cos, sin, cos_t, sin_t = _rope_tables(S)
    ts = TS_IN
    n_s = S // ts
    qw, kw = N_Q_HEADS * LANES, N_KV_HEADS * LANES
    const = lambda i: (0, 0)
    tok = lambda i: (i, 0)
    pos = lambda i: (i % n_s, 0)
    pos_t = lambda i: (0, i % n_s)
    tposed = lambda i: (i // n_s, 0, i % n_s)
    return pl.pallas_call(
        functools.partial(_inproj_kernel, lru_w=lru_w),
        grid=(T // ts,),
        in_specs=[
            pl.BlockSpec((ts, D), tok),
            pl.BlockSpec((1, D), const),
            pl.BlockSpec(w_t.shape, const),
            pl.BlockSpec(w_rest.shape, const),
            pl.BlockSpec((LANES, 1), const),
            pl.BlockSpec((1, LANES), const),
            pl.BlockSpec((ts, LANES), pos),
            pl.BlockSpec((ts, LANES), pos),
            pl.BlockSpec((LANES, ts), pos_t),
            pl.BlockSpec((LANES, ts), pos_t),
        ],
        out_specs=[
            pl.BlockSpec((1, qw, ts), tposed),
            pl.BlockSpec((ts, kw), tok),
            pl.BlockSpec((1, kw, ts), tposed),
            pl.BlockSpec((ts, lru_w), tok),
            pl.BlockSpec((ts, lru_w), tok),
        ],
        out_shape=[
            jax.ShapeDtypeStruct((T // S, qw, S), BF16),
            jax.ShapeDtypeStruct((T, kw), BF16),
            jax.ShapeDtypeStruct((T // S, kw, S), BF16),
            jax.ShapeDtypeStruct((T, lru_w), F32),
            jax.ShapeDtypeStruct((T, lru_w), F32),
        ],
        compiler_params=_cparams(("parallel",)),
        name="inproj",
    )(x2, norm1_g.reshape(1, D), w_t, w_rest, qg, kg, cos, sin, cos_t, sin_t)


def _attn_kernel(qt_ref, k_ref, vt_ref, o_ref, acc_ref, s_ref, p_ref, *, tq, tk, n_kv, kv_unroll):
    hp = HEADS_PER_STEP
    spt = GQA_GROUP // hp
    acc_ref[...] = jnp.zeros(acc_ref.shape, F32)

    def scores(j, sp):
        kt = k_ref[0, pl.ds(pl.multiple_of(j * tk, tk), tk), :]
        out = []
        for u in range(hp):
            g = sp * hp + u
            s = jnp.dot(kt, qt_ref[0, g * LANES:(g + 1) * LANES, :], preferred_element_type=F32)
            out.append((s, jnp.max(s, axis=0, keepdims=True)))
        return out

    def softmax_stage(sc, ms, sp):
        out = []
        for u, (s, s_max) in enumerate(sc):
            h = sp * hp + u
            m_new = jnp.maximum(ms[h], s_max)
            out.append((jnp.exp2(ms[h] - m_new), jnp.exp2(s - m_new).astype(BF16)))
            ms[h] = m_new
        return out

    def pv_stage(j, sp, ap):
        vt = vt_ref[0, 0:PV_ROWS, pl.ds(pl.multiple_of(j * tk, tk), tk)]
        for u, (alpha, p) in enumerate(ap):
            g = sp * hp + u
            acc_ref[g] = alpha * acc_ref[g] + jnp.dot(vt, p, preferred_element_type=F32)

    ms = [jnp.full((1, tq), -jnp.inf, F32)] * GQA_GROUP
    ap = softmax_stage(scores(0, 0), ms, 0)
    sc = scores(min(1 // spt, n_kv - 1), 1 % spt)
    for u in range(hp):
        s_ref[u] = sc[u][0]
        p_ref[u] = ap[u][1]

    def body(it, carry):
        ms = list(carry[:GQA_GROUP])
        ap = [(carry[GQA_GROUP + u], p_ref[u]) for u in range(hp)]
        sc = [(s_ref[u], carry[GQA_GROUP + hp + u]) for u in range(hp)]
        for n in range(kv_unroll * spt):
            j = it * kv_unroll + n // spt
            j_next = jnp.minimum(it * kv_unroll + (n + 2) // spt, n_kv - 1)
            sc_next = scores(j_next, (n + 2) % spt)
            ap_next = softmax_stage(sc, ms, (n + 1) % spt)
            pv_stage(j, n % spt, ap)
            sc, ap = sc_next, ap_next
        for u in range(hp):
            s_ref[u] = sc[u][0]
            p_ref[u] = ap[u][1]
        return tuple(ms) + tuple(a for a, _ in ap) + tuple(m for _, m in sc)

    lax.fori_loop(0, n_kv // kv_unroll, body,
                  tuple(ms) + tuple(a for a, _ in ap) + tuple(m for _, m in sc))
    _attn_finalize(acc_ref, o_ref, tq)


def _attn_finalize(acc_ref, o_ref, tq):
    pad = jnp.zeros((LANES - HEAD_DIM, tq), F32)
    for g in range(GQA_GROUP):
        acc = acc_ref[g]
        o = acc[0:HEAD_DIM] / acc[HEAD_DIM:HEAD_DIM + 1, :]
        o_ref[0, :, g * LANES:(g + 1) * LANES] = jnp.concatenate([o, pad], axis=0).T.astype(BF16)


def _attn_bounded_kernel(qt_ref, k_ref, vt_ref, o_ref, acc_ref, s_ref, p_ref, *, tq, tk, n_kv, kv_unroll):
    hp = HEADS_PER_STEP
    spt = GQA_GROUP // hp
    acc_ref[...] = jnp.zeros(acc_ref.shape, F32)

    def scores(j, sp):
        kt = k_ref[0, pl.ds(pl.multiple_of(j * tk, tk), tk), :]
        return [jnp.dot(kt, qt_ref[0, (sp * hp + u) * LANES:(sp * hp + u + 1) * LANES, :],
                        preferred_element_type=F32) for u in range(hp)]

    def probs(sc):
        return [jnp.exp2(s).astype(BF16) for s in sc]

    def pv_stage(j, sp, ps):
        vt = vt_ref[0, 0:PV_ROWS, pl.ds(pl.multiple_of(j * tk, tk), tk)]
        for u, p in enumerate(ps):
            acc_ref[sp * hp + u] += jnp.dot(vt, p, preferred_element_type=F32)

    ps = probs(scores(0, 0))
    sc = scores(min(1 // spt, n_kv - 1), 1 % spt)
    for u in range(hp):
        s_ref[u] = sc[u]
        p_ref[u] = ps[u]

    def body(it, carry):
        ps = [p_ref[u] for u in range(hp)]
        sc = [s_ref[u] for u in range(hp)]
        for n in range(kv_unroll * spt):
            j = it * kv_unroll + n // spt
            j_next = jnp.minimum(it * kv_unroll + (n + 2) // spt, n_kv - 1)
            sc_next = scores(j_next, (n + 2) % spt)
            ps_next = probs(sc)
            pv_stage(j, n % spt, ps)
            sc, ps = sc_next, ps_next
        for u in range(hp):
            s_ref[u] = sc[u]
            p_ref[u] = ps[u]
        return carry

    lax.fori_loop(0, n_kv // kv_unroll, body, 0)
    _attn_finalize(acc_ref, o_ref, tq)


def _attention(qt, k, vt, *, kernel, B, S):
    tq = min(TQ, S)
    tk = min(TK, S)
    gw = GQA_GROUP * LANES
    return pl.pallas_call(
        functools.partial(kernel, tq=tq, tk=tk, n_kv=S // tk,
                          kv_unroll=math.gcd(S // tk, KV_UNROLL)),
        grid=(B, N_KV_HEADS, S // tq),
        in_specs=[
            pl.BlockSpec((1, gw, tq), lambda b, h, i: (b, h, i)),
            pl.BlockSpec((1, S, LANES), lambda b, h, i: (b, 0, h)),
            pl.BlockSpec((1, LANES, S), lambda b, h, i: (b, h, 0)),
        ],
        out_specs=pl.BlockSpec((1, tq, gw), lambda b, h, i: (b, i, h)),
        out_shape=jax.ShapeDtypeStruct((B, S, N_Q_HEADS * LANES), BF16),
        scratch_shapes=[pltpu.VMEM((GQA_GROUP, PV_ROWS, tq), F32),
                        pltpu.VMEM((HEADS_PER_STEP, tk, tq), F32),
                        pltpu.VMEM((HEADS_PER_STEP, tk, tq), BF16)],
        compiler_params=_cparams(("parallel", "parallel", "parallel")),
        name=kernel.__name__.strip("_"),
    )(qt, k, vt)


def _scan_chunk(a, b, h_in, reverse):
    n = a.shape[0]
    n_groups = n // SUBLANES
    a = a.reshape(n_groups, SUBLANES, LANES)
    b = b.reshape(n_groups, SUBLANES, LANES)
    sub = lax.broadcasted_iota(jnp.int32, a.shape, 1)
    d = 1
    while d < SUBLANES:
        if reverse:
            keep = sub < SUBLANES - d
            shift = SUBLANES - d
        else:
            keep = sub >= d
            shift = d
        a_sh = jnp.where(keep, pltpu.roll(a, shift, 1), 1.0)
        b_sh = jnp.where(keep, pltpu.roll(b, shift, 1), 0.0)
        b = a * b_sh + b
        a = a * a_sh
        d *= 2
    a = a.reshape(n, LANES)
    b = b.reshape(n, LANES)
    order = range(n_groups - 1, -1, -1) if reverse else range(n_groups)
    edge = h_in
    out = [None] * n_groups
    for v in order:
        rows = slice(v * SUBLANES, (v + 1) * SUBLANES)
        hv = b[rows] + a[rows] * jnp.broadcast_to(edge, (SUBLANES, LANES))
        out[v] = hv
        edge = hv[0:1] if reverse else hv[SUBLANES - 1:SUBLANES]
    return jnp.concatenate(out, axis=0), edge


def _lru_pad_input(u_ref, up_ref, S):
    zeros = jnp.zeros((SUBLANES, LANES), F32)
    up_ref[0:SUBLANES, :] = zeros
    up_ref[S + SUBLANES:S + 2 * SUBLANES, :] = zeros
    up_ref[SUBLANES:S + SUBLANES, :] = u_ref[0]


def _lru_gates(up_ref, cw_ref, cb_ref, w_ref, bias_ref, lam_ref, t0, tc, d):
    cw = cw_ref[...]
    xc = cb_ref[...]
    for j in range(CONV_W):
        xc = xc + up_ref[pl.ds(t0 + SUBLANES + j - CONV_PAD_L, tc), :] * cw[j:j + 1, :]
    gw = 2 * LANES
    g = jnp.dot(xc.astype(BF16), w_ref[0, :, d * gw:(d + 1) * gw],
                preferred_element_type=F32) + bias_ref[0, :, d * gw:(d + 1) * gw]
    r = jax.nn.sigmoid(g[:, :LANES])
    i = jax.nn.sigmoid(g[:, LANES:])
    a = jnp.exp(-LRU_C * r * jax.nn.softplus(-lam_ref[d:d + 1, :]))
    y = 1.0 - a * a
    b = jnp.where(y > 0.0, y * lax.rsqrt(y), 0.0) * i * xc
    return a, b


def _lru_kernel(u_ref, gate_ref, cw_ref, cb_ref, w_ref, bias_ref, lam_ref, o_ref,
                up_ref, hf_ref, *, S, tc):
    _lru_pad_input(u_ref, up_ref, S)
    n_chunks = S // tc
    params = (up_ref, cw_ref, cb_ref, w_ref, bias_ref, lam_ref)

    def fwd(c, h):
        t0 = pl.multiple_of(c * tc, tc)
        hc, h_last = _scan_chunk(*_lru_gates(*params, t0, tc, 0), h, False)
        hf_ref[pl.ds(t0, tc), :] = hc
        return h_last

    unroll = math.gcd(n_chunks, LRU_UNROLL)

    def grouped(step):
        def body(i, h):
            for j in range(unroll):
                h = step(i * unroll + j, h)
            return h
        return body

    lax.fori_loop(0, n_chunks // unroll, grouped(fwd), jnp.zeros((1, LANES), F32))

    def bwd(ci, h):
        t0 = pl.multiple_of((n_chunks - 1 - ci) * tc, tc)
        hc, h_last = _scan_chunk(*_lru_gates(*params, t0, tc, 1), h, True)
        gate = gate_ref[0, pl.ds(t0, tc), :]
        o_ref[0, pl.ds(t0, tc), :] = (hf_ref[pl.ds(t0, tc), :] + hc) * jax.nn.gelu(gate)
        return h_last

    lax.fori_loop(0, n_chunks // unroll, grouped(bwd), jnp.zeros((1, LANES), F32))


def _block_diag_pairs(w):
    nb, bw, _ = w.shape
    w = w.reshape(nb // 2, 2, bw, bw)
    z = jnp.zeros_like(w[:, 0])
    top = jnp.concatenate([w[:, 0], z], axis=-1)
    bot = jnp.concatenate([z, w[:, 1]], axis=-1)
    return jnp.concatenate([top, bot], axis=-2)


def _lru_operands(conv_w, conv_b, wa, ba, wi, bi, lam):
    C = conv_b.shape[0]
    nc = C // LANES
    w = jnp.concatenate([_block_diag_pairs(wa[0]), _block_diag_pairs(wi[0]),
                         _block_diag_pairs(wa[1]), _block_diag_pairs(wi[1])], axis=-1).astype(BF16)
    bias = jnp.stack([ba[0].reshape(nc, LANES), bi[0].reshape(nc, LANES),
                      ba[1].reshape(nc, LANES), bi[1].reshape(nc, LANES)], axis=1)
    return conv_w, conv_b.reshape(1, C), w, bias.reshape(nc, 1, 4 * LANES), lam


def _lru_specs(S, unit):
    seq = lambda *g: (unit(*g)[0], 0, unit(*g)[1])
    chan = lambda *g: (0, unit(*g)[1])
    blk = lambda *g: (unit(*g)[1], 0, 0)
    in_specs = [
        pl.BlockSpec((1, S, LANES), seq),
        pl.BlockSpec((1, S, LANES), seq),
        pl.BlockSpec((CONV_W, LANES), chan),
        pl.BlockSpec((1, LANES), chan),
        pl.BlockSpec((1, LANES, 4 * LANES), blk),
        pl.BlockSpec((1, 1, 4 * LANES), blk),
        pl.BlockSpec((2, LANES), chan),
    ]
    return in_specs, pl.BlockSpec((1, S, LANES), seq)


def _lru(lru_x, lru_gate, lru_ops, B, S):
    C = lru_x.shape[-1]
    tc = min(TC_LRU, S)
    in_specs, out_spec = _lru_specs(S, lambda b, c: (b, c))
    return pl.pallas_call(
        functools.partial(_lru_kernel, S=S, tc=tc),
        grid=(B, C // LANES),
        in_specs=in_specs,
        out_specs=out_spec,
        out_shape=jax.ShapeDtypeStruct((B, S, C), F32),
        scratch_shapes=[
            pltpu.VMEM((S + 2 * SUBLANES, LANES), F32),
            pltpu.VMEM((S, LANES), F32),
        ],
        compiler_params=_cparams(("parallel", "parallel")),
        name="rglru",
    )(lru_x, lru_gate, *lru_ops)


def _mixers(qt, k, vt, lru_x, lru_gate, lru_ops, score_bound, B, S):
    attn = lax.cond(score_bound <= SAFE_SCORE_LOG2,
                    functools.partial(_attention, kernel=_attn_bounded_kernel, B=B, S=S),
                    functools.partial(_attention, kernel=_attn_kernel, B=B, S=S), qt, k, vt)
    return attn, _lru(lru_x, lru_gate, lru_ops, B, S)


def _rows_to_slabs(ref, x):
    n = x.shape[0]
    for s in range(SUBLANES):
        ref[pl.ds(s, n, stride=SUBLANES), :] = x[:, s * LANES:(s + 1) * LANES]


def _slabs_to_rows(ref, n):
    return jnp.concatenate([ref[pl.ds(s, n, stride=SUBLANES), :] for s in range(SUBLANES)], axis=1)


def _slab(ref, r):
    return ref.at[pl.ds(pl.multiple_of(r * SUBLANES, SUBLANES), SUBLANES)]


def _outproj_kernel(a_ref, l_ref, x_ref, ag_ref, lg_ref, wa_ref, wl_ref, g2_ref,
                    wr_ref, br_ref, tri_ref,
                    x1_ref, xn3_ref, route_ref, gates_ref, cnt_ref, carry_ref, *, attn_w, lru_w):
    step = pl.program_id(0)

    @pl.when(step == 0)
    def _():
        carry_ref[...] = jnp.zeros_like(carry_ref)

    a = a_ref[...].astype(F32)
    ams = jnp.sum(a * a, axis=-1, keepdims=True) * (1.0 / attn_w)
    an = a * lax.rsqrt(ams + NORM_EPS) * ag_ref[...]
    l = l_ref[...]
    lms = jnp.sum(l * l, axis=-1, keepdims=True) * (1.0 / lru_w)
    ln = l * lax.rsqrt(lms + NORM_EPS) * lg_ref[...]
    mix = (jnp.dot(an.astype(BF16), wa_ref[...], preferred_element_type=F32)
           + jnp.dot(ln.astype(BF16), wl_ref[...], preferred_element_type=F32))
    x1 = x_ref[...] + mix
    x1_ref[...] = x1
    ms = jnp.mean(x1 * x1, axis=-1, keepdims=True)
    xn = x1 * lax.rsqrt(ms + NORM_EPS) * g2_ref[...]
    _rows_to_slabs(xn3_ref, xn)

    logits = jnp.dot(xn.astype(BF16), wr_ref[...], preferred_element_type=F32) + br_ref[...]
    lane = lax.broadcasted_iota(jnp.int32, logits.shape, 1)
    neg = -jnp.inf
    work = jnp.where(lane < N_EXPERTS, logits, neg)
    sel = jnp.zeros(logits.shape, F32)
    idxs, vals = [], []
    for _ in range(TOP_K):
        m = jnp.max(work, axis=1, keepdims=True)
        idx = jnp.min(jnp.where(work == m, lane, LANES), axis=1, keepdims=True)
        hit = lane == idx
        work = jnp.where(hit, neg, work)
        sel = sel + hit.astype(F32)
        idxs.append(idx)
        vals.append(m)
    es = [jnp.exp(v - vals[0]) for v in vals]
    den = es[0] + es[1] + es[2] + es[3]

    prefix = jnp.dot(tri_ref[...], sel.astype(BF16), preferred_element_type=F32) + carry_ref[...]
    carry_ref[...] = carry_ref[...] + jnp.sum(sel, axis=0, keepdims=True)
    cnt_ref[...] = carry_ref[...]

    route = jnp.zeros(logits.shape, jnp.int32)
    gates = jnp.zeros(logits.shape, F32)
    for k in range(TOP_K):
        rank = jnp.sum(jnp.where(lane == idxs[k], prefix, 0.0), axis=1, keepdims=True).astype(jnp.int32)
        route = jnp.where(lane == k, idxs[k], route)
        route = jnp.where(lane == TOP_K + k, rank, route)
        gates = jnp.where(lane == k, es[k] / den, gates)
    route_ref[...] = route
    gates_ref[...] = gates


def _outproj_router(attn, lru, x2, attn_out_g, lru_out_g, w_out, norm2_g, w_router, b_router):
    T, D = x2.shape
    lru_w = lru.shape[-1]
    ts = min(TS_OUT, T)
    wa = w_out[:ATTN_W].reshape(N_Q_HEADS, HEAD_DIM, D)
    wa = jnp.pad(wa, ((0, 0), (0, LANES - HEAD_DIM), (0, 0))).reshape(N_Q_HEADS * LANES, D).astype(BF16)
    wl = w_out[ATTN_W:].astype(BF16)
    ag = _pad_heads(attn_out_g.reshape(1, ATTN_W), N_Q_HEADS)
    wr = jnp.pad(w_router, ((0, 0), (0, LANES - N_EXPERTS))).astype(BF16)
    br =jnp.pad(b_router.reshape(1, N_EXPERTS), ((0, 0), (0, LANES - N_EXPERTS)))
    tri = (jnp.arange(ts)[:, None] > jnp.arange(ts)[None, :]).astype(BF16)
    const = lambda i: (0, 0)
    tok = lambda i: (i, 0)
    aw = N_Q_HEADS * LANES
    return pl.pallas_call(
        functools.partial(_outproj_kernel, attn_w=ATTN_W, lru_w=lru_w),
        grid=(T // ts,),
        in_specs=[
            pl.BlockSpec((ts, aw), tok),
            pl.BlockSpec((ts, lru_w), tok),
            pl.BlockSpec((ts, D), tok),
            pl.BlockSpec((1, aw), const),
            pl.BlockSpec((1, lru_w), const),
            pl.BlockSpec((aw, D), const),
            pl.BlockSpec((lru_w, D), const),
            pl.BlockSpec((1, D), const),
            pl.BlockSpec((D, LANES), const),
            pl.BlockSpec((1, LANES), const),
            pl.BlockSpec((ts, ts), const),
        ],
        out_specs=[
            pl.BlockSpec((ts, D), tok),
            pl.BlockSpec((ts * SUBLANES, LANES), tok),
            pl.BlockSpec((ts, LANES), tok),
            pl.BlockSpec((ts, LANES), tok),
            pl.BlockSpec((1, LANES), const),
        ],
        out_shape=[
            jax.ShapeDtypeStruct((T, D), F32),
            jax.ShapeDtypeStruct((T * SUBLANES, LANES), F32),
            jax.ShapeDtypeStruct((T, LANES), jnp.int32),
            jax.ShapeDtypeStruct((T, LANES), F32),
            jax.ShapeDtypeStruct((1, LANES), F32),
        ],
        scratch_shapes=[pltpu.VMEM((1, LANES), F32)],
        compiler_params=_cparams(("arbitrary",)),
        name="outproj_router",
    )(attn, lru, x2, ag, lru_out_g.reshape(1, lru_w), wa, wl, norm2_g.reshape(1, D),
      wr, br, tri)


def _plan_kernel(cnt_ref, pstart_ref, plan_ref):
    cnt = cnt_ref[...]
    lane = lax.broadcasted_iota(jnp.int32, cnt.shape, 1)
    padded = jnp.floor((cnt + (ROW_BLOCK - 1)) * (1.0 / ROW_BLOCK)) * ROW_BLOCK
    pend = padded
    d = 1
    while d < N_EXPERTS:
        pend = pend + jnp.where(lane >= d, pltpu.roll(pend, d, 1), 0.0)
        d *= 2
    pstart_ref[...] = pend - padded
    total = jnp.max(pend, axis=1, keepdims=True)

    shape = plan_ref.shape
    lanes = lax.broadcasted_iota(jnp.int32, shape, 1)
    is_expert = lanes < N_EXPERTS
    start = lax.broadcasted_iota(jnp.int32, shape, 0).astype(F32) * ROW_BLOCK

    def groups_ending_by(row):
        return jnp.sum(jnp.where(jnp.logical_and(pend <= row, is_expert), 1.0, 0.0), axis=1, keepdims=True)

    block_e = jnp.minimum(groups_ending_by(start), N_EXPERTS - 1.0)
    tail = jnp.max(jnp.where(jnp.logical_and(jnp.logical_and(pend == start + ROW_BLOCK, padded > 0.0),
                                             is_expert), 1.0, 0.0), axis=1, keepdims=True)
    fill = jnp.maximum(tail, jnp.where(start[:, 0:1] >= total, 1.0, 0.0))
    group_end = jnp.sum(jnp.where(lanes.astype(F32) == block_e, pend, 0.0), axis=1, keepdims=True)
    next_e = jnp.where(group_end < total,
                       jnp.minimum(groups_ending_by(group_end), N_EXPERTS - 1.0), -1.0)
    plan = jnp.where(lanes == 0, block_e,
                     jnp.where(lanes == 1, fill,
                               jnp.where(lanes == 2, next_e, total * (1.0 / ROW_BLOCK))))
    plan_ref[...] = plan.astype(jnp.int32)


def _routing_plan(cnt, n_blocks):
    assert ROW_BLOCK & (ROW_BLOCK - 1) == 0, "exact f32 division by the row block size"
    rows = -(-n_blocks // SUBLANES) * SUBLANES
    pstart, plan = pl.pallas_call(
        _plan_kernel,
        out_shape=[jax.ShapeDtypeStruct((1, LANES), F32),
                   jax.ShapeDtypeStruct((rows, LANES), jnp.int32)],
        name="routing_plan",
    )(cnt)
    return pstart, plan[:n_blocks, 0], plan[:n_blocks, 1], plan[:n_blocks, 2], plan[0:1, 3]


def _dest_kernel(route_ref, pstart_ref, dest_ref):
    route = route_ref[...]
    lane = lax.broadcasted_iota(jnp.int32, route.shape, 1)
    pstart = pstart_ref[...]
    dest = jnp.zeros(route.shape, jnp.int32)
    for k in range(TOP_K):
        start = jnp.sum(jnp.where(lane == route[:, k:k + 1], pstart, 0.0), axis=1, keepdims=True)
        dest = jnp.where(lane == k, start.astype(jnp.int32) + route[:, TOP_K + k:TOP_K + k + 1], dest)
    dest_ref[...] = dest


def _dest_rows(route, pstart):
    T = route.shape[0]
    ts = math.gcd(TS_DEST, T)
    dest = pl.pallas_call(
        _dest_kernel,
        grid=(T // ts,),
        in_specs=[pl.BlockSpec((ts, LANES), lambda i: (i, 0)),
                  pl.BlockSpec((1, LANES), lambda i: (0, 0))],
        out_specs=pl.BlockSpec((ts, LANES), lambda i: (i, 0)),
        out_shape=jax.ShapeDtypeStruct((T, LANES), jnp.int32),
        compiler_params=_cparams(("parallel",)),
        name="dest_rows",
    )(route, pstart)
    return dest[:, :TOP_K].reshape(T * TOP_K)


def _dispatch_kernel(fill_ref, dest_ref, x_ref, out_hbm, zero_ref, sem, zero_sem, *, ts, n_blocks):
    block_slabs = ROW_BLOCK * SUBLANES

    def fill_copy(b):
        off = pl.multiple_of(b * block_slabs, block_slabs)
        return pltpu.make_async_copy(zero_ref, out_hbm.at[pl.ds(off, block_slabs)], zero_sem)

    @pl.when(pl.program_id(0) == 0)
    def _():
        zero_ref[...] = jnp.zeros(zero_ref.shape, F32)

        def start(b, carry):
            @pl.when(fill_ref[b] != 0)
            def _():
                fill_copy(b).start()
            return carry

        def wait(b, carry):
            @pl.when(fill_ref[b] != 0)
            def _():
                fill_copy(b).wait()
            return carry

        lax.fori_loop(0, n_blocks, start, 0)
        lax.fori_loop(0, n_blocks, wait, 0)

    def issue(i, carry):
        for j in range(ISSUE_UNROLL):
            r = i * ISSUE_UNROLL + j
            for k in range(TOP_K):
                d = dest_ref[r * TOP_K + k]
                pltpu.make_async_copy(_slab(x_ref, r), _slab(out_hbm, d), sem).start(priority=k % 2)
        return carry

    lax.fori_loop(0, ts // ISSUE_UNROLL, issue, 0)
    for k in range(TOP_K):
        pltpu.make_async_copy(x_ref, out_hbm.at[pl.ds(0, ts * SUBLANES)], sem).wait()


def _dispatch(xn_slabs, fill, dest_flat, n_rows):
    T = xn_slabs.shape[0] // SUBLANES
    ts = min(TS_DISP, T)
    grid_spec = pltpu.PrefetchScalarGridSpec(
        num_scalar_prefetch=1,
        grid=(T // ts,),
        in_specs=[
            pl.BlockSpec((ts * TOP_K,), lambda i, fl: (i,), memory_space=pltpu.SMEM),
            pl.BlockSpec((ts * SUBLANES, LANES), lambda i, fl: (i, 0)),
        ],
        out_specs=pl.BlockSpec(memory_space=pl.ANY),
        scratch_shapes=[pltpu.VMEM((ROW_BLOCK * SUBLANES, LANES), F32),
                        pltpu.SemaphoreType.DMA, pltpu.SemaphoreType.DMA],
    )
    return pl.pallas_call(
        functools.partial(_dispatch_kernel, ts=ts, n_blocks=n_rows // ROW_BLOCK),
        grid_spec=grid_spec,
        out_shape=jax.ShapeDtypeStruct((n_rows * SUBLANES, LANES), xn_slabs.dtype),
        compiler_params=_cparams(("arbitrary",)),
        name="dispatch",
    )(fill, dest_flat, xn_slabs)


def _expert_loop_kernel(be_ref, na_ref, nxt_ref, x_hbm, wg_hbm, bg_ref, wu_hbm, bu_ref, wd_hbm, bd_ref,
                        y_hbm, xbuf, ybuf, stage_ref, wb_ref, x_sems, y_sems, w_sems):
    w_hbm = (wg_hbm, wu_hbm, wd_hbm)
    block_slabs = ROW_BLOCK * SUBLANES
    n_active = na_ref[0]

    def rows(b):
        return pl.ds(pl.multiple_of(b * block_slabs, block_slabs), block_slabs)

    def x_copy(b, s):
        return pltpu.make_async_copy(x_hbm.at[rows(b)], xbuf.at[s], x_sems.at[s])

    def y_copy(b, s):
        return pltpu.make_async_copy(ybuf.at[s], y_hbm.at[rows(b)], y_sems.at[s])

    def fetch(expert, s, m):
        return pltpu.make_async_copy(w_hbm[m].at[expert], stage_ref.at[s, m], w_sems.at[s, m])

    x_copy(0, 0).start()
    for m in range(3):
        fetch(be_ref[0], 0, m).start()

    def body(b, wslot):
        s = b % 2
        e = be_ref[b]
        x_copy(b, s).wait()

        @pl.when(b + 1 < n_active)
        def _():
            x_copy(b + 1, 1 - s).start()

        first = jnp.logical_or(b == 0, e != be_ref[jnp.maximum(b - 1, 0)])

        @pl.when(first)
        def _():
            for m in range(3):
                fetch(e, wslot, m).wait()
                wb_ref[m] = stage_ref[wslot, m].astype(BF16)

            @pl.when(nxt_ref[b] >= 0)
            def _():
                for m in range(3):
                    fetch(nxt_ref[b], 1 - wslot, m).start()

        @pl.when(b >= 2)
        def _():
            y_copy(b - 2, s).wait()

        x = _slabs_to_rows(xbuf.at[s], ROW_BLOCK).astype(BF16)
        g = jnp.dot(x, wb_ref[0], preferred_element_type=F32) + bg_ref[e]
        u = jnp.dot(x, wb_ref[1], preferred_element_type=F32) + bu_ref[e]
        g = jnp.minimum(g, SWIGLU_LIMIT)
        u = jnp.clip(u, -SWIGLU_LIMIT, SWIGLU_LIMIT)
        glu = g * jax.nn.sigmoid(SWIGLU_ALPHA * g)
        y = jnp.dot(((u + 1.0) * glu).astype(BF16), wb_ref[2], preferred_element_type=F32) + bd_ref[e]
        _rows_to_slabs(ybuf.at[s], y)
        y_copy(b, s).start()
        return jnp.where(first, 1 - wslot, wslot)

    lax.fori_loop(0, n_active, body, 0)

    @pl.when(n_active >= 2)
    def _():
        y_copy(n_active - 2, n_active % 2).wait()

    y_copy(n_active - 1, (n_active - 1) % 2).wait()


def _experts(x_rows, block_e, n_active, next_e, w_gate, b_gate, w_up, b_up, w_down, b_down):
    E, D, FF = w_gate.shape
    assert D == FF, "the three expert matrices share one staging shape"
    block_slabs = ROW_BLOCK * SUBLANES
    whole = lambda i, be, na, nx: (0, 0, 0)

    grid_spec = pltpu.PrefetchScalarGridSpec(
        num_scalar_prefetch=3,
        grid=(1,),
        in_specs=[
            pl.BlockSpec(memory_space=pl.ANY),
            pl.BlockSpec(memory_space=pl.ANY),
            pl.BlockSpec((E, 1, FF), whole),
            pl.BlockSpec(memory_space=pl.ANY),
            pl.BlockSpec((E, 1, FF), whole),
            pl.BlockSpec(memory_space=pl.ANY),
            pl.BlockSpec((E, 1, D), whole),
        ],
        out_specs=pl.BlockSpec(memory_space=pl.ANY),
        scratch_shapes=[
            pltpu.VMEM((2, block_slabs, LANES), F32),
            pltpu.VMEM((2, block_slabs, LANES), F32),
            pltpu.VMEM((2, 3, D, FF), F32),
            pltpu.VMEM((3, D, FF), BF16),
            pltpu.SemaphoreType.DMA((2,)),
            pltpu.SemaphoreType.DMA((2,)),
            pltpu.SemaphoreType.DMA((2, 3)),
        ],
    )
    return pl.pallas_call(
        _expert_loop_kernel,
        grid_spec=grid_spec,
        out_shape=jax.ShapeDtypeStruct(x_rows.shape, F32),
        input_output_aliases={3: 0},
        compiler_params=pltpu.CompilerParams(dimension_semantics=("arbitrary",),
                                             vmem_limit_bytes=EXPERT_VMEM_LIMIT),
        name="experts",
    )(block_e, n_active, next_e, x_rows, w_gate, b_gate.reshape(E, 1, FF), w_up,
      b_up.reshape(E, 1, FF), w_down, b_down.reshape(E, 1, D))


def _combine_kernel(dest_ref, dest_next_ref, y_hbm, x1_ref, gates_ref, fg_ref, o_ref, bufs, sems,
                    *, ts, n_steps):
    i = pl.program_id(0)
    slot = i % 2

    def gather_tile(d_ref, s):
        def issue(it, carry):
            for j in range(ISSUE_UNROLL):
                r = it * ISSUE_UNROLL + j
                for k in range(TOP_K):
                    d = d_ref[r * TOP_K + k]
                    pltpu.make_async_copy(_slab(y_hbm, d), _slab(bufs.at[s, k], r),
                                          sems.at[s]).start(priority=k % 2)
            return carry

        lax.fori_loop(0, ts // ISSUE_UNROLL, issue, 0)

    @pl.when(i == 0)
    def _():
        gather_tile(dest_ref, 0)

    @pl.when(i + 1 < n_steps)
    def _():
        gather_tile(dest_next_ref, 1 - slot)

    for k in range(TOP_K):
        pltpu.make_async_copy(y_hbm.at[pl.ds(0, ts * SUBLANES)], bufs.at[slot, k], sems.at[slot]).wait()

    acc = x1_ref[...]
    gates = gates_ref[...]
    for k in range(TOP_K):
        acc = acc + _slabs_to_rows(bufs.at[slot, k], ts) * gates[:, k:k + 1]
    ms = jnp.mean(acc * acc, axis=-1, keepdims=True)
    o_ref[...] = acc * lax.rsqrt(ms + NORM_EPS) * fg_ref[...]


def _combine(y_rows, dest_flat, x1, gates, final_g):
    T, D = x1.shape
    ts = min(TS_COMB, T)
    n_steps = T // ts
    tok = lambda i: (i, 0)
    return pl.pallas_call(
        functools.partial(_combine_kernel, ts=ts, n_steps=n_steps),
        grid=(n_steps,),
        in_specs=[
            pl.BlockSpec((ts * TOP_K,), lambda i: (i,), memory_space=pltpu.SMEM),
            pl.BlockSpec((ts * TOP_K,), lambda i: (jnp.minimum(i + 1, n_steps - 1),),
                         memory_space=pltpu.SMEM),
            pl.BlockSpec(memory_space=pl.ANY),
            pl.BlockSpec((ts, D), tok),
            pl.BlockSpec((ts, LANES), tok),
            pl.BlockSpec((1, D), lambda i: (0, 0)),
        ],
        out_specs=pl.BlockSpec((ts, D), tok),
        out_shape=jax.ShapeDtypeStruct((T, D), F32),
        scratch_shapes=[pltpu.VMEM((2, TOP_K, ts * SUBLANES, LANES), F32),
                        pltpu.SemaphoreType.DMA((2,))],
        compiler_params=_cparams(("arbitrary",)),
        name="combine",
    )(dest_flat, dest_flat, y_rows, x1, gates, final_g.reshape(1, D))


def kernel(x, norm1_g, w_in, q_norm_g, k_norm_g, conv_w, conv_b, lru_wa, lru_ba, lru_wi, lru_bi,
           lru_lam, attn_out_g, lru_out_g, w_out, norm2_g, w_router, b_router, w_gate, b_gate,
           w_up, b_up, w_down, b_down, final_g):
    B, S, D = x.shape
    T = B * S
    assert w_in.shape[0] == 1, "single-layer trunk: the final norm is fused into the layer's combine"
    assert D == SUBLANES * LANES, "a token row is moved as one (8, 128) f32 slab"
    assert S % max(TS_IN, TQ, TK, TC_LRU) == 0 and S % GRID_W == 0, "sequence tiles must divide S"
    x2 = x.reshape(T, D)
    for l in range(1):
        qt, k, vt, lru_x, lru_gate = _inproj(x2, norm1_g[l], w_in[l], q_norm_g[l], k_norm_g[l], S)
        score_bound = (HEAD_DIM * Q_SCALE * jnp.max(jnp.abs(q_norm_g[l]))
                       * jnp.max(jnp.abs(k_norm_g[l])))
        lru_ops = _lru_operands(conv_w[l], conv_b[l], lru_wa[l], lru_ba[l], lru_wi[l], lru_bi[l],
                                lru_lam[l])
        attn, lru = _mixers(qt, k.reshape(B, S, -1), vt, lru_x.reshape(B, S, -1),
                            lru_gate.reshape(B, S, -1), lru_ops, score_bound, B, S)
        x1, xn3, route, gates, cnt = _outproj_router(
            attn.reshape(T, -1), lru.reshape(T, -1), x2, attn_out_g[l], lru_out_g[l], w_out[l],
            norm2_g[l], w_router[l], b_router[l])

        n_rows = T * TOP_K + N_EXPERTS * ROW_BLOCK
        pstart, block_e, fill, next_e, n_active = _routing_plan(cnt, n_rows // ROW_BLOCK)
        dest_flat = _dest_rows(route, pstart)
        x_rows = _dispatch(xn3, fill, dest_flat, n_rows)
        y_rows = _experts(x_rows, block_e, n_active, next_e, w_gate[l], b_gate[l], w_up[l], b_up[l],
                          w_down[l], b_down[l])
        x2 = _combine(y_rows, dest_flat, x1, gates, final_g)
    return x2.reshape(B, S, D)
```

```python
import functools
import math

import jax
import jax.numpy as jnp
import numpy as np
from jax import lax
from jax.experimental import pallas as pl
from jax.experimental.pallas import tpu as pltpu

F32 = jnp.float32
BF16 = jnp.bfloat16

GRID_W = 64
HEAD_DIM = 64
N_Q_HEADS = 8
N_KV_HEADS = 2
GQA_GROUP = N_Q_HEADS // N_KV_HEADS
ATTN_W = N_Q_HEADS * HEAD_DIM
KV_W = N_KV_HEADS * HEAD_DIM
LRU_BLOCKS = 8
LRU_C = 8.0
CONV_W = 4
CONV_PAD_L = 2
ROPE_THETA = 10000.0
ROPE_HALF = HEAD_DIM // 2
ROPE_M = ROPE_HALF // 2
N_EXPERTS = 32
TOP_K = 4
SWIGLU_ALPHA = 1.702
SWIGLU_LIMIT = 7.0
NORM_EPS = 1e-5
QK_EPS = 1e-6
LOG2_E = 1.4426950408889634
Q_SCALE = HEAD_DIM ** -0.5 * LOG2_E
SAFE_SCORE_LOG2 = 96.0

LANES = 128
SUBLANES = 8
BF16_SUBLANES = 16
PV_ROWS = HEAD_DIM + BF16_SUBLANES
VMEM_LIMIT = 48 * 1024 * 1024
EXPERT_VMEM_LIMIT = 56 * 1024 * 1024

TS_IN = 512
TQ = 256
TK = 256
KV_UNROLL = 32
HEADS_PER_STEP = 2
TC_LRU = 512
LRU_UNROLL = 4
TS_OUT = 512
TS_DEST = 2048
ROW_BLOCK = 256
TS_DISP = 1024
TS_COMB = 512
ISSUE_UNROLL = 8


def _cparams(sem):
    return pltpu.CompilerParams(dimension_semantics=sem, vmem_limit_bytes=VMEM_LIMIT)


def _inproj_kernel(x_ref, g1_ref, wt_ref, w_ref, qg_ref, kg_ref, cos_ref, sin_ref, cost_ref, sint_ref,
                   q_ref, k_ref, v_ref, lx_ref, lg_ref, *, lru_w):
    x = x_ref[...]
    ms = jnp.mean(x * x, axis=-1, keepdims=True)
    xn = (x * lax.rsqrt(ms + NORM_EPS) * g1_ref[...]).astype(BF16)
    ht = lax.dot_general(wt_ref[...], xn, (((1,), (1,)), ((), ())), preferred_element_type=F32)
    h = jnp.dot(xn, w_ref[...], preferred_element_type=F32)

    qw = N_Q_HEADS * LANES
    kw = N_KV_HEADS * LANES
    cost = cost_ref[...]
    sint = sint_ref[...]
    row = lax.broadcasted_iota(jnp.int32, cost.shape, 0)
    first_half_t = (row % ROPE_HALF) < ROPE_M
    qg = qg_ref[...]
    for c in range(N_Q_HEADS):
        sl = slice(c * LANES, (c + 1) * LANES)
        xc = ht[sl]
        hms = jnp.sum(xc * xc, axis=0, keepdims=True) * (1.0 / HEAD_DIM)
        xc = xc * lax.rsqrt(hms + QK_EPS) * qg
        partner = jnp.where(first_half_t, pltpu.roll(xc, LANES - ROPE_M, 0), pltpu.roll(xc, ROPE_M, 0))
        q_ref[0, sl, :] = ((xc * cost + partner * sint) * Q_SCALE).astype(BF16)
    for c in range(N_KV_HEADS):
        sl = slice(c * LANES, (c + 1) * LANES)
        v_ref[0, sl, :] = jnp.where(row >= HEAD_DIM, 1.0, ht[qw + c * LANES: qw + (c + 1) * LANES]).astype(BF16)

    cos = cos_ref[...]
    sin = sin_ref[...]
    lane = lax.broadcasted_iota(jnp.int32, cos.shape, 1)
    first_half = (lane % ROPE_HALF) < ROPE_M
    for c in range(N_KV_HEADS):
        sl = slice(c * LANES, (c + 1) * LANES)
        xc = h[:, sl]
        hms = jnp.sum(xc * xc, axis=-1, keepdims=True) * (1.0 / HEAD_DIM)
        xc = xc * lax.rsqrt(hms + QK_EPS) * kg_ref[...]
        partner = jnp.where(first_half, pltpu.roll(xc, LANES - ROPE_M, 1), pltpu.roll(xc, ROPE_M, 1))
        k_ref[:, sl] = (xc * cos + partner * sin).astype(BF16)
    lx_ref[...] = h[:, kw: kw + lru_w]
    lg_ref[...] = h[:, kw + lru_w: kw + 2 * lru_w]


def _pad_heads(w, n_heads):
    lead = w.shape[:-1]
    w = w.reshape(lead + (n_heads, HEAD_DIM))
    w = jnp.pad(w, [(0, 0)] * len(lead) + [(0, 0), (0, LANES - HEAD_DIM)])
    return w.reshape(lead + (n_heads * LANES,))


def _rope_tables(S):
    t = np.arange(S)
    rows = (t // GRID_W).astype(np.float32)
    cols = (t % GRID_W).astype(np.float32)
    inv_freq = (ROPE_THETA ** (-np.arange(ROPE_M, dtype=np.float32) / ROPE_M)).astype(np.float32)
    ar = rows[:, None] * inv_freq[None, :]
    ac = cols[:, None] * inv_freq[None, :]
    cos = np.concatenate([np.cos(ar), np.cos(ar), np.cos(ac), np.cos(ac)], axis=-1)
    sin = np.concatenate([-np.sin(ar), np.sin(ar), -np.sin(ac), np.sin(ac)], axis=-1)
    pad = [(0, 0), (0, LANES - HEAD_DIM)]
    cos = np.pad(cos, pad).astype(np.float32)
    sin = np.pad(sin, pad).astype(np.float32)
    return cos, sin, np.ascontiguousarray(cos.T), np.ascontiguousarray(sin.T)


def _inproj(x2, norm1_g, w_in, q_norm_g, k_norm_g, S):
    T, D = x2.shape
    lru_w = (w_in.shape[1] - ATTN_W - 2 * KV_W) // 2
    o0, o1, o2 = ATTN_W, ATTN_W + KV_W, ATTN_W + 2 * KV_W
    w_t = jnp.concatenate([_pad_heads(w_in[:, :o0], N_Q_HEADS),
                           _pad_heads(w_in[:, o1:o2], N_KV_HEADS)], axis=1).T.astype(BF16)
    w_rest = jnp.concatenate([_pad_heads(w_in[:, o0:o1], N_KV_HEADS), w_in[:, o2:]],
                             axis=1).astype(BF16)
    qg = _pad_heads(q_norm_g.reshape(1, HEAD_DIM), 1).reshape(LANES, 1)
    kg = _pad_heads(k_norm_g.reshape(1, HEAD_DIM), 1)
    cos, sin, cos_t, sin_t = _rope_tables(S)
    ts = TS_IN
    n_s = S // ts
    qw, kw = N_Q_HEADS * LANES, N_KV_HEADS * LANES
    const = lambda i: (0, 0)
    tok = lambda i: (i, 0)
    pos = lambda i: (i % n_s, 0)
    pos_t = lambda i: (0, i % n_s)
    tposed = lambda i: (i // n_s, 0, i % n_s)
    return pl.pallas_call(
        functools.partial(_inproj_kernel, lru_w=lru_w),
        grid=(T // ts,),
        in_specs=[
            pl.BlockSpec((ts, D), tok),
            pl.BlockSpec((1, D), const),
            pl.BlockSpec(w_t.shape, const),
            pl.BlockSpec(w_rest.shape, const),
            pl.BlockSpec((LANES, 1), const),
            pl.BlockSpec((1, LANES), const),
            pl.BlockSpec((ts, LANES), pos),
            pl.BlockSpec((ts, LANES), pos),
            pl.BlockSpec((LANES, ts), pos_t),
            pl.BlockSpec((LANES, ts), pos_t),
        ],
        out_specs=[
            pl.BlockSpec((1, qw, ts), tposed),
            pl.BlockSpec((ts, kw), tok),
            pl.BlockSpec((1, kw, ts), tposed),
            pl.BlockSpec((ts, lru_w), tok),
            pl.BlockSpec((ts, lru_w), tok),
        ],
        out_shape=[
            jax.ShapeDtypeStruct((T // S, qw, S), BF16),
            jax.ShapeDtypeStruct((T, kw), BF16),
            jax.ShapeDtypeStruct((T // S, kw, S), BF16),
            jax.ShapeDtypeStruct((T, lru_w), F32),
            jax.ShapeDtypeStruct((T, lru_w), F32),
        ],
        compiler_params=_cparams(("parallel",)),
        name="inproj",
    )(x2, norm1_g.reshape(1, D), w_t, w_rest, qg, kg, cos, sin, cos_t, sin_t)


def _attn_kernel(qt_ref, k_ref, vt_ref, o_ref, acc_ref, s_ref, p_ref, *, tq, tk, n_kv, kv_unroll):
    hp = HEADS_PER_STEP
    spt = GQA_GROUP // hp
    acc_ref[...] = jnp.zeros(acc_ref.shape, F32)

    def scores(j, sp):
        kt = k_ref[0, pl.ds(pl.multiple_of(j * tk, tk), tk), :]
        out = []
        for u in range(hp):
            g = sp * hp + u
            s = jnp.dot(kt, qt_ref[0, g * LANES:(g + 1) * LANES, :], preferred_element_type=F32)
            out.append((s, jnp.max(s, axis=0, keepdims=True)))
        return out

    def softmax_stage(sc, ms, sp):
        out = []
        for u, (s, s_max) in enumerate(sc):
            h = sp * hp + u
            m_new = jnp.maximum(ms[h], s_max)
            out.append((jnp.exp2(ms[h] - m_new), jnp.exp2(s - m_new).astype(BF16)))
            ms[h] = m_new
        return out

    def pv_stage(j, sp, ap):
        vt = vt_ref[0, 0:PV_ROWS, pl.ds(pl.multiple_of(j * tk, tk), tk)]
        for u, (alpha, p) in enumerate(ap):
            g = sp * hp + u
            acc_ref[g] = alpha * acc_ref[g] + jnp.dot(vt, p, preferred_element_type=F32)

    ms = [jnp.full((1, tq), -jnp.inf, F32)] * GQA_GROUP
    ap = softmax_stage(scores(0, 0), ms, 0)
    sc = scores(min(1 // spt, n_kv - 1), 1 % spt)
    for u in range(hp):
        s_ref[u] = sc[u][0]
        p_ref[u] = ap[u][1]

    def body(it, carry):
        ms = list(carry[:GQA_GROUP])
        ap = [(carry[GQA_GROUP + u], p_ref[u]) for u in range(hp)]
        sc = [(s_ref[u], carry[GQA_GROUP + hp + u]) for u in range(hp)]
        for n in range(kv_unroll * spt):
            j = it * kv_unroll + n // spt
            j_next = jnp.minimum(it * kv_unroll + (n + 2) // spt, n_kv - 1)
            sc_next = scores(j_next, (n + 2) % spt)
            ap_next = softmax_stage(sc, ms, (n + 1) % spt)
            pv_stage(j, n % spt, ap)
            sc, ap = sc_next, ap_next
        for u in range(hp):
            s_ref[u] = sc[u][0]
            p_ref[u] = ap[u][1]
        return tuple(ms) + tuple(a for a, _ in ap) + tuple(m for _, m in sc)

    lax.fori_loop(0, n_kv // kv_unroll, body,
                  tuple(ms) + tuple(a for a, _ in ap) + tuple(m for _, m in sc))
    _attn_finalize(acc_ref, o_ref, tq)


def _attn_finalize(acc_ref, o_ref, tq):
    pad = jnp.zeros((LANES - HEAD_DIM, tq), F32)
    for g in range(GQA_GROUP):
        acc = acc_ref[g]
        o = acc[0:HEAD_DIM] / acc[HEAD_DIM:HEAD_DIM + 1, :]
        o_ref[0, :, g * LANES:(g + 1) * LANES] = jnp.concatenate([o, pad], axis=0).T.astype(BF16)


def _attn_bounded_kernel(qt_ref, k_ref, vt_ref, o_ref, acc_ref, s_ref, p_ref, *, tq, tk, n_kv, kv_unroll):
    hp = HEADS_PER_STEP
    spt = GQA_GROUP // hp
    acc_ref[...] = jnp.zeros(acc_ref.shape, F32)

    def scores(j, sp):
        kt = k_ref[0, pl.ds(pl.multiple_of(j * tk, tk), tk), :]
        return [jnp.dot(kt, qt_ref[0, (sp * hp + u) * LANES:(sp * hp + u + 1) * LANES, :],
                        preferred_element_type=F32) for u in range(hp)]

    def probs(sc):
        return [jnp.exp2(s).astype(BF16) for s in sc]

    def pv_stage(j, sp, ps):
        vt = vt_ref[0, 0:PV_ROWS, pl.ds(pl.multiple_of(j * tk, tk), tk)]
        for u, p in enumerate(ps):
            acc_ref[sp * hp + u] += jnp.dot(vt, p, preferred_element_type=F32)

    ps = probs(scores(0, 0))
    sc = scores(min(1 // spt, n_kv - 1), 1 % spt)
    for u in range(hp):
        s_ref[u] = sc[u]
        p_ref[u] = ps[u]

    def body(it, carry):
        ps = [p_ref[u] for u in range(hp)]
        sc = [s_ref[u] for u in range(hp)]
        for n in range(kv_unroll * spt):
            j = it * kv_unroll + n // spt
            j_next = jnp.minimum(it * kv_unroll + (n + 2) // spt, n_kv - 1)
            sc_next = scores(j_next, (n + 2) % spt)
            ps_next = probs(sc)
            pv_stage(j, n % spt, ps)
            sc, ps = sc_next, ps_next
        for u in range(hp):
            s_ref[u] = sc[u]
            p_ref[u] = ps[u]
        return carry

    lax.fori_loop(0, n_kv // kv_unroll, body, 0)
    _attn_finalize(acc_ref, o_ref, tq)


def _attention(qt, k, vt, *, kernel, B, S):
    tq = min(TQ, S)
    tk = min(TK, S)
    gw = GQA_GROUP * LANES
    return pl.pallas_call(
        functools.partial(kernel, tq=tq, tk=tk, n_kv=S // tk,
                          kv_unroll=math.gcd(S // tk, KV_UNROLL)),
        grid=(B, N_KV_HEADS, S // tq),
        in_specs=[
            pl.BlockSpec((1, gw, tq), lambda b, h, i: (b, h, i)),
            pl.BlockSpec((1, S, LANES), lambda b, h, i: (b, 0, h)),
            pl.BlockSpec((1, LANES, S), lambda b, h, i: (b, h, 0)),
        ],
        out_specs=pl.BlockSpec((1, tq, gw), lambda b, h, i: (b, i, h)),
        out_shape=jax.ShapeDtypeStruct((B, S, N_Q_HEADS * LANES), BF16),
        scratch_shapes=[pltpu.VMEM((GQA_GROUP, PV_ROWS, tq), F32),
                        pltpu.VMEM((HEADS_PER_STEP, tk, tq), F32),
                        pltpu.VMEM((HEADS_PER_STEP, tk, tq), BF16)],
        compiler_params=_cparams(("parallel", "parallel", "parallel")),
        name=kernel.__name__.strip("_"),
    )(qt, k, vt)


def _scan_chunk(a, b, h_in, reverse):
    n = a.shape[0]
    n_groups = n // SUBLANES
    a = a.reshape(n_groups, SUBLANES, LANES)
    b = b.reshape(n_groups, SUBLANES, LANES)
    sub = lax.broadcasted_iota(jnp.int32, a.shape, 1)
    d = 1
    while d < SUBLANES:
        if reverse:
            keep = sub < SUBLANES - d
            shift = SUBLANES - d
        else:
            keep = sub >= d
            shift = d
        a_sh = jnp.where(keep, pltpu.roll(a, shift, 1), 1.0)
        b_sh = jnp.where(keep, pltpu.roll(b, shift, 1), 0.0)
        b = a * b_sh + b
        a = a * a_sh
        d *= 2
    a = a.reshape(n, LANES)
    b = b.reshape(n, LANES)
    order = range(n_groups - 1, -1, -1) if reverse else range(n_groups)
    edge = h_in
    out = [None] * n_groups
    for v in order:
        rows = slice(v * SUBLANES, (v + 1) * SUBLANES)
        hv = b[rows] + a[rows] * jnp.broadcast_to(edge, (SUBLANES, LANES))
        out[v] = hv
        edge = hv[0:1] if reverse else hv[SUBLANES - 1:SUBLANES]
    return jnp.concatenate(out, axis=0), edge


def _lru_pad_input(u_ref, up_ref, S):
    zeros = jnp.zeros((SUBLANES, LANES), F32)
    up_ref[0:SUBLANES, :] = zeros
    up_ref[S + SUBLANES:S + 2 * SUBLANES, :] = zeros
    up_ref[SUBLANES:S + SUBLANES, :] = u_ref[0]


def _lru_gates(up_ref, cw_ref, cb_ref, w_ref, bias_ref, lam_ref, t0, tc, d):
    cw = cw_ref[...]
    xc = cb_ref[...]
    for j in range(CONV_W):
        xc = xc + up_ref[pl.ds(t0 + SUBLANES + j - CONV_PAD_L, tc), :] * cw[j:j + 1, :]
    gw = 2 * LANES
    g = jnp.dot(xc.astype(BF16), w_ref[0, :, d * gw:(d + 1) * gw],
                preferred_element_type=F32) + bias_ref[0, :, d * gw:(d + 1) * gw]
    r = jax.nn.sigmoid(g[:, :LANES])
    i = jax.nn.sigmoid(g[:, LANES:])
    a = jnp.exp(-LRU_C * r * jax.nn.softplus(-lam_ref[d:d + 1, :]))
    y = 1.0 - a * a
    b = jnp.where(y > 0.0, y * lax.rsqrt(y), 0.0) * i * xc
    return a, b


def _lru_kernel(u_ref, gate_ref, cw_ref, cb_ref, w_ref, bias_ref, lam_ref, o_ref,
                up_ref, hf_ref, *, S, tc):
    _lru_pad_input(u_ref, up_ref, S)
    n_chunks = S // tc
    params = (up_ref, cw_ref, cb_ref, w_ref, bias_ref, lam_ref)

    def fwd(c, h):
        t0 = pl.multiple_of(c * tc, tc)
        hc, h_last = _scan_chunk(*_lru_gates(*params, t0, tc, 0), h, False)
        hf_ref[pl.ds(t0, tc), :] = hc
        return h_last

    unroll = math.gcd(n_chunks, LRU_UNROLL)

    def grouped(step):
        def body(i, h):
            for j in range(unroll):
                h = step(i * unroll + j, h)
            return h
        return body

    lax.fori_loop(0, n_chunks // unroll, grouped(fwd), jnp.zeros((1, LANES), F32))

    def bwd(ci, h):
        t0 = pl.multiple_of((n_chunks - 1 - ci) * tc, tc)
        hc, h_last = _scan_chunk(*_lru_gates(*params, t0, tc, 1), h, True)
        gate = gate_ref[0, pl.ds(t0, tc), :]
        o_ref[0, pl.ds(t0, tc), :] = (hf_ref[pl.ds(t0, tc), :] + hc) * jax.nn.gelu(gate)
        return h_last

    lax.fori_loop(0, n_chunks // unroll, grouped(bwd), jnp.zeros((1, LANES), F32))


def _block_diag_pairs(w):
    nb, bw, _ = w.shape
    w = w.reshape(nb // 2, 2, bw, bw)
    z = jnp.zeros_like(w[:, 0])
    top = jnp.concatenate([w[:, 0], z], axis=-1)
    bot = jnp.concatenate([z, w[:, 1]], axis=-1)
    return jnp.concatenate([top, bot], axis=-2)


def _lru_operands(conv_w, conv_b, wa, ba, wi, bi, lam):
    C = conv_b.shape[0]
    nc = C // LANES
    w = jnp.concatenate([_block_diag_pairs(wa[0]), _block_diag_pairs(wi[0]),
                         _block_diag_pairs(wa[1]), _block_diag_pairs(wi[1])], axis=-1).astype(BF16)
    bias = jnp.stack([ba[0].reshape(nc, LANES), bi[0].reshape(nc, LANES),
                      ba[1].reshape(nc, LANES), bi[1].reshape(nc, LANES)], axis=1)
    return conv_w, conv_b.reshape(1, C), w, bias.reshape(nc, 1, 4 * LANES), lam


def _lru_specs(S, unit):
    seq = lambda *g: (unit(*g)[0], 0, unit(*g)[1])
    chan = lambda *g: (0, unit(*g)[1])
    blk = lambda *g: (unit(*g)[1], 0, 0)
    in_specs = [
        pl.BlockSpec((1, S, LANES), seq),
        pl.BlockSpec((1, S, LANES), seq),
        pl.BlockSpec((CONV_W, LANES), chan),
        pl.BlockSpec((1, LANES), chan),
        pl.BlockSpec((1, LANES, 4 * LANES), blk),
        pl.BlockSpec((1, 1, 4 * LANES), blk),
        pl.BlockSpec((2, LANES), chan),
    ]
    return in_specs, pl.BlockSpec((1, S, LANES), seq)


def _lru(lru_x, lru_gate, lru_ops, B, S):
    C = lru_x.shape[-1]
    tc = min(TC_LRU, S)
    in_specs, out_spec = _lru_specs(S, lambda b, c: (b, c))
    return pl.pallas_call(
        functools.partial(_lru_kernel, S=S, tc=tc),
        grid=(B, C // LANES),
        in_specs=in_specs,
        out_specs=out_spec,
        out_shape=jax.ShapeDtypeStruct((B, S, C), F32),
        scratch_shapes=[
            pltpu.VMEM((S + 2 * SUBLANES, LANES), F32),
            pltpu.VMEM((S, LANES), F32),
        ],
        compiler_params=_cparams(("parallel", "parallel")),
        name="rglru",
    )(lru_x, lru_gate, *lru_ops)


def _mixers(qt, k, vt, lru_x, lru_gate, lru_ops, score_bound, B, S):
    attn = lax.cond(score_bound <= SAFE_SCORE_LOG2,
                    functools.partial(_attention, kernel=_attn_bounded_kernel, B=B, S=S),
                    functools.partial(_attention, kernel=_attn_kernel, B=B, S=S), qt, k, vt)
    return attn, _lru(lru_x, lru_gate, lru_ops, B, S)


def _rows_to_slabs(ref, x):
    n = x.shape[0]
    for s in range(SUBLANES):
        ref[pl.ds(s, n, stride=SUBLANES), :] = x[:, s * LANES:(s + 1) * LANES]


def _slabs_to_rows(ref, n):
    return jnp.concatenate([ref[pl.ds(s, n, stride=SUBLANES), :] for s in range(SUBLANES)], axis=1)


def _slab(ref, r):
    return ref.at[pl.ds(pl.multiple_of(r * SUBLANES, SUBLANES), SUBLANES)]


def _outproj_kernel(a_ref, l_ref, x_ref, ag_ref, lg_ref, wa_ref, wl_ref, g2_ref,
                    wr_ref, br_ref, tri_ref,
                    x1_ref, xn3_ref, route_ref, gates_ref, cnt_ref, carry_ref, *, attn_w, lru_w):
    step = pl.program_id(0)

    @pl.when(step == 0)
    def _():
        carry_ref[...] = jnp.zeros_like(carry_ref)

    a = a_ref[...].astype(F32)
    ams = jnp.sum(a * a, axis=-1, keepdims=True) * (1.0 / attn_w)
    an = a * lax.rsqrt(ams + NORM_EPS) * ag_ref[...]
    l = l_ref[...]
    lms = jnp.sum(l * l, axis=-1, keepdims=True) * (1.0 / lru_w)
    ln = l * lax.rsqrt(lms + NORM_EPS) * lg_ref[...]
    mix = (jnp.dot(an.astype(BF16), wa_ref[...], preferred_element_type=F32)
           + jnp.dot(ln.astype(BF16), wl_ref[...], preferred_element_type=F32))
    x1 = x_ref[...] + mix
    x1_ref[...] = x1
    ms = jnp.mean(x1 * x1, axis=-1, keepdims=True)
    xn = x1 * lax.rsqrt(ms + NORM_EPS) * g2_ref[...]
    _rows_to_slabs(xn3_ref, xn)

    logits = jnp.dot(xn.astype(BF16), wr_ref[...], preferred_element_type=F32) + br_ref[...]
    lane = lax.broadcasted_iota(jnp.int32, logits.shape, 1)
    neg = -jnp.inf
    work = jnp.where(lane < N_EXPERTS, logits, neg)
    sel = jnp.zeros(logits.shape, F32)
    idxs, vals = [], []
    for _ in range(TOP_K):
        m = jnp.max(work, axis=1, keepdims=True)
        idx = jnp.min(jnp.where(work == m, lane, LANES), axis=1, keepdims=True)
        hit = lane == idx
        work = jnp.where(hit, neg, work)
        sel = sel + hit.astype(F32)
        idxs.append(idx)
        vals.append(m)
    es = [jnp.exp(v - vals[0]) for v in vals]
    den = es[0] + es[1] + es[2] + es[3]

    prefix = jnp.dot(tri_ref[...], sel.astype(BF16), preferred_element_type=F32) + carry_ref[...]
    carry_ref[...] = carry_ref[...] + jnp.sum(sel, axis=0, keepdims=True)
    cnt_ref[...] = carry_ref[...]

    route = jnp.zeros(logits.shape, jnp.int32)
    gates = jnp.zeros(logits.shape, F32)
    for k in range(TOP_K):
        rank = jnp.sum(jnp.where(lane == idxs[k], prefix, 0.0), axis=1, keepdims=True).astype(jnp.int32)
        route = jnp.where(lane == k, idxs[k], route)
        route = jnp.where(lane == TOP_K + k, rank, route)
        gates = jnp.where(lane == k, es[k] / den, gates)
    route_ref[...] = route
    gates_ref[...] = gates


def _outproj_router(attn, lru, x2, attn_out_g, lru_out_g, w_out, norm2_g, w_router, b_router):
    T, D = x2.shape
    lru_w = lru.shape[-1]
    ts = min(TS_OUT, T)
    wa = w_out[:ATTN_W].reshape(N_Q_HEADS, HEAD_DIM, D)
    wa = jnp.pad(wa, ((0, 0), (0, LANES - HEAD_DIM), (0, 0))).reshape(N_Q_HEADS * LANES, D).astype(BF16)
    wl = w_out[ATTN_W:].astype(BF16)
    ag = _pad_heads(attn_out_g.reshape(1, ATTN_W), N_Q_HEADS)
    wr = jnp.pad(w_router, ((0, 0), (0, LANES - N_EXPERTS))).astype(BF16)
    br =jnp.pad(b_router.reshape(1, N_EXPERTS), ((0, 0), (0, LANES - N_EXPERTS)))
    tri = (jnp.arange(ts)[:, None] > jnp.arange(ts)[None, :]).astype(BF16)
    const = lambda i: (0, 0)
    tok = lambda i: (i, 0)
    aw = N_Q_HEADS * LANES
    return pl.pallas_call(
        functools.partial(_outproj_kernel, attn_w=ATTN_W, lru_w=lru_w),
        grid=(T // ts,),
        in_specs=[
            pl.BlockSpec((ts, aw), tok),
            pl.BlockSpec((ts, lru_w), tok),
            pl.BlockSpec((ts, D), tok),
            pl.BlockSpec((1, aw), const),
            pl.BlockSpec((1, lru_w), const),
            pl.BlockSpec((aw, D), const),
            pl.BlockSpec((lru_w, D), const),
            pl.BlockSpec((1, D), const),
            pl.BlockSpec((D, LANES), const),
            pl.BlockSpec((1, LANES), const),
            pl.BlockSpec((ts, ts), const),
        ],
        out_specs=[
            pl.BlockSpec((ts, D), tok),
            pl.BlockSpec((ts * SUBLANES, LANES), tok),
            pl.BlockSpec((ts, LANES), tok),
            pl.BlockSpec((ts, LANES), tok),
            pl.BlockSpec((1, LANES), const),
        ],
        out_shape=[
            jax.ShapeDtypeStruct((T, D), F32),
            jax.ShapeDtypeStruct((T * SUBLANES, LANES), F32),
            jax.ShapeDtypeStruct((T, LANES), jnp.int32),
            jax.ShapeDtypeStruct((T, LANES), F32),
            jax.ShapeDtypeStruct((1, LANES), F32),
        ],
        scratch_shapes=[pltpu.VMEM((1, LANES), F32)],
        compiler_params=_cparams(("arbitrary",)),
        name="outproj_router",
    )(attn, lru, x2, ag, lru_out_g.reshape(1, lru_w), wa, wl, norm2_g.reshape(1, D),
      wr, br, tri)


def _plan_kernel(cnt_ref, pstart_ref, plan_ref):
    cnt = cnt_ref[...]
    lane = lax.broadcasted_iota(jnp.int32, cnt.shape, 1)
    padded = jnp.floor((cnt + (ROW_BLOCK - 1)) * (1.0 / ROW_BLOCK)) * ROW_BLOCK
    pend = padded
    d = 1
    while d < N_EXPERTS:
        pend = pend + jnp.where(lane >= d, pltpu.roll(pend, d, 1), 0.0)
        d *= 2
    pstart_ref[...] = pend - padded
    total = jnp.max(pend, axis=1, keepdims=True)

    shape = plan_ref.shape
    lanes = lax.broadcasted_iota(jnp.int32, shape, 1)
    is_expert = lanes < N_EXPERTS
    start = lax.broadcasted_iota(jnp.int32, shape, 0).astype(F32) * ROW_BLOCK

    def groups_ending_by(row):
        return jnp.sum(jnp.where(jnp.logical_and(pend <= row, is_expert), 1.0, 0.0), axis=1, keepdims=True)

    block_e = jnp.minimum(groups_ending_by(start), N_EXPERTS - 1.0)
    tail = jnp.max(jnp.where(jnp.logical_and(jnp.logical_and(pend == start + ROW_BLOCK, padded > 0.0),
                                             is_expert), 1.0, 0.0), axis=1, keepdims=True)
    fill = jnp.maximum(tail, jnp.where(start[:, 0:1] >= total, 1.0, 0.0))
    group_end = jnp.sum(jnp.where(lanes.astype(F32) == block_e, pend, 0.0), axis=1, keepdims=True)
    next_e = jnp.where(group_end < total,
                       jnp.minimum(groups_ending_by(group_end), N_EXPERTS - 1.0), -1.0)
    plan = jnp.where(lanes == 0, block_e,
                     jnp.where(lanes == 1, fill,
                               jnp.where(lanes == 2, next_e, total * (1.0 / ROW_BLOCK))))
    plan_ref[...] = plan.astype(jnp.int32)


def _routing_plan(cnt, n_blocks):
    assert ROW_BLOCK & (ROW_BLOCK - 1) == 0, "exact f32 division by the row block size"
    rows = -(-n_blocks // SUBLANES) * SUBLANES
    pstart, plan = pl.pallas_call(
        _plan_kernel,
        out_shape=[jax.ShapeDtypeStruct((1, LANES), F32),
                   jax.ShapeDtypeStruct((rows, LANES), jnp.int32)],
        name="routing_plan",
    )(cnt)
    return pstart, plan[:n_blocks, 0], plan[:n_blocks, 1], plan[:n_blocks, 2], plan[0:1, 3]


def _dest_kernel(route_ref, pstart_ref, dest_ref):
    route = route_ref[...]
    lane = lax.broadcasted_iota(jnp.int32, route.shape, 1)
    pstart = pstart_ref[...]
    dest = jnp.zeros(route.shape, jnp.int32)
    for k in range(TOP_K):
        start = jnp.sum(jnp.where(lane == route[:, k:k + 1], pstart, 0.0), axis=1, keepdims=True)
        dest = jnp.where(lane == k, start.astype(jnp.int32) + route[:, TOP_K + k:TOP_K + k + 1], dest)
    dest_ref[...] = dest


def _dest_rows(route, pstart):
    T = route.shape[0]
    ts = math.gcd(TS_DEST, T)
    dest = pl.pallas_call(
        _dest_kernel,
        grid=(T // ts,),
        in_specs=[pl.BlockSpec((ts, LANES), lambda i: (i, 0)),
                  pl.BlockSpec((1, LANES), lambda i: (0, 0))],
        out_specs=pl.BlockSpec((ts, LANES), lambda i: (i, 0)),
        out_shape=jax.ShapeDtypeStruct((T, LANES), jnp.int32),
        compiler_params=_cparams(("parallel",)),
        name="dest_rows",
    )(route, pstart)
    return dest[:, :TOP_K].reshape(T * TOP_K)


def _dispatch_kernel(fill_ref, dest_ref, x_ref, out_hbm, zero_ref, sem, zero_sem, *, ts, n_blocks):
    block_slabs = ROW_BLOCK * SUBLANES

    def fill_copy(b):
        off = pl.multiple_of(b * block_slabs, block_slabs)
        return pltpu.make_async_copy(zero_ref, out_hbm.at[pl.ds(off, block_slabs)], zero_sem)

    @pl.when(pl.program_id(0) == 0)
    def _():
        zero_ref[...] = jnp.zeros(zero_ref.shape, F32)

        def start(b, carry):
            @pl.when(fill_ref[b] != 0)
            def _():
                fill_copy(b).start()
            return carry

        def wait(b, carry):
            @pl.when(fill_ref[b] != 0)
            def _():
                fill_copy(b).wait()
            return carry

        lax.fori_loop(0, n_blocks, start, 0)
        lax.fori_loop(0, n_blocks, wait, 0)

    def issue(i, carry):
        for j in range(ISSUE_UNROLL):
            r = i * ISSUE_UNROLL + j
            for k in range(TOP_K):
                d = dest_ref[r * TOP_K + k]
                pltpu.make_async_copy(_slab(x_ref, r), _slab(out_hbm, d), sem).start(priority=k % 2)
        return carry

    lax.fori_loop(0, ts // ISSUE_UNROLL, issue, 0)
    for k in range(TOP_K):
        pltpu.make_async_copy(x_ref, out_hbm.at[pl.ds(0, ts * SUBLANES)], sem).wait()


def _dispatch(xn_slabs, fill, dest_flat, n_rows):
    T = xn_slabs.shape[0] // SUBLANES
    ts = min(TS_DISP, T)
    grid_spec = pltpu.PrefetchScalarGridSpec(
        num_scalar_prefetch=1,
        grid=(T // ts,),
        in_specs=[
            pl.BlockSpec((ts * TOP_K,), lambda i, fl: (i,), memory_space=pltpu.SMEM),
            pl.BlockSpec((ts * SUBLANES, LANES), lambda i, fl: (i, 0)),
        ],
        out_specs=pl.BlockSpec(memory_space=pl.ANY),
        scratch_shapes=[pltpu.VMEM((ROW_BLOCK * SUBLANES, LANES), F32),
                        pltpu.SemaphoreType.DMA, pltpu.SemaphoreType.DMA],
    )
    return pl.pallas_call(
        functools.partial(_dispatch_kernel, ts=ts, n_blocks=n_rows // ROW_BLOCK),
        grid_spec=grid_spec,
        out_shape=jax.ShapeDtypeStruct((n_rows * SUBLANES, LANES), xn_slabs.dtype),
        compiler_params=_cparams(("arbitrary",)),
        name="dispatch",
    )(fill, dest_flat, xn_slabs)


def _expert_loop_kernel(be_ref, na_ref, nxt_ref, x_hbm, wg_hbm, bg_ref, wu_hbm, bu_ref, wd_hbm, bd_ref,
                        y_hbm, xbuf, ybuf, stage_ref, wb_ref, x_sems, y_sems, w_sems):
    w_hbm = (wg_hbm, wu_hbm, wd_hbm)
    block_slabs = ROW_BLOCK * SUBLANES
    n_active = na_ref[0]

    def rows(b):
        return pl.ds(pl.multiple_of(b * block_slabs, block_slabs), block_slabs)

    def x_copy(b, s):
        return pltpu.make_async_copy(x_hbm.at[rows(b)], xbuf.at[s], x_sems.at[s])

    def y_copy(b, s):
        return pltpu.make_async_copy(ybuf.at[s], y_hbm.at[rows(b)], y_sems.at[s])

    def fetch(expert, s, m):
        return pltpu.make_async_copy(w_hbm[m].at[expert], stage_ref.at[s, m], w_sems.at[s, m])

    x_copy(0, 0).start()
    for m in range(3):
        fetch(be_ref[0], 0, m).start()

    def body(b, wslot):
        s = b % 2
        e = be_ref[b]
        x_copy(b, s).wait()

        @pl.when(b + 1 < n_active)
        def _():
            x_copy(b + 1, 1 - s).start()

        first = jnp.logical_or(b == 0, e != be_ref[jnp.maximum(b - 1, 0)])

        @pl.when(first)
        def _():
            for m in range(3):
                fetch(e, wslot, m).wait()
                wb_ref[m] = stage_ref[wslot, m].astype(BF16)

            @pl.when(nxt_ref[b] >= 0)
            def _():
                for m in range(3):
                    fetch(nxt_ref[b], 1 - wslot, m).start()

        @pl.when(b >= 2)
        def _():
            y_copy(b - 2, s).wait()

        x = _slabs_to_rows(xbuf.at[s], ROW_BLOCK).astype(BF16)
        g = jnp.dot(x, wb_ref[0], preferred_element_type=F32) + bg_ref[e]
        u = jnp.dot(x, wb_ref[1], preferred_element_type=F32) + bu_ref[e]
        g = jnp.minimum(g, SWIGLU_LIMIT)
        u = jnp.clip(u, -SWIGLU_LIMIT, SWIGLU_LIMIT)
        glu = g * jax.nn.sigmoid(SWIGLU_ALPHA * g)
        y = jnp.dot(((u + 1.0) * glu).astype(BF16), wb_ref[2], preferred_element_type=F32) + bd_ref[e]
        _rows_to_slabs(ybuf.at[s], y)
        y_copy(b, s).start()
        return jnp.where(first, 1 - wslot, wslot)

    lax.fori_loop(0, n_active, body, 0)

    @pl.when(n_active >= 2)
    def _():
        y_copy(n_active - 2, n_active % 2).wait()

    y_copy(n_active - 1, (n_active - 1) % 2).wait()


def _experts(x_rows, block_e, n_active, next_e, w_gate, b_gate, w_up, b_up, w_down, b_down):
    E, D, FF = w_gate.shape
    assert D == FF, "the three expert matrices share one staging shape"
    block_slabs = ROW_BLOCK * SUBLANES
    whole = lambda i, be, na, nx: (0, 0, 0)

    grid_spec = pltpu.PrefetchScalarGridSpec(
        num_scalar_prefetch=3,
        grid=(1,),
        in_specs=[
            pl.BlockSpec(memory_space=pl.ANY),
            pl.BlockSpec(memory_space=pl.ANY),
            pl.BlockSpec((E, 1, FF), whole),
            pl.BlockSpec(memory_space=pl.ANY),
            pl.BlockSpec((E, 1, FF), whole),
            pl.BlockSpec(memory_space=pl.ANY),
            pl.BlockSpec((E, 1, D), whole),
        ],
        out_specs=pl.BlockSpec(memory_space=pl.ANY),
        scratch_shapes=[
            pltpu.VMEM((2, block_slabs, LANES), F32),
            pltpu.VMEM((2, block_slabs, LANES), F32),
            pltpu.VMEM((2, 3, D, FF), F32),
            pltpu.VMEM((3, D, FF), BF16),
            pltpu.SemaphoreType.DMA((2,)),
            pltpu.SemaphoreType.DMA((2,)),
            pltpu.SemaphoreType.DMA((2, 3)),
        ],
    )
    return pl.pallas_call(
        _expert_loop_kernel,
        grid_spec=grid_spec,
        out_shape=jax.ShapeDtypeStruct(x_rows.shape, F32),
        input_output_aliases={3: 0},
        compiler_params=pltpu.CompilerParams(dimension_semantics=("arbitrary",),
                                             vmem_limit_bytes=EXPERT_VMEM_LIMIT),
        name="experts",
    )(block_e, n_active, next_e, x_rows, w_gate, b_gate.reshape(E, 1, FF), w_up,
      b_up.reshape(E, 1, FF), w_down, b_down.reshape(E, 1, D))


def _combine_kernel(dest_ref, dest_next_ref, y_hbm, x1_ref, gates_ref, fg_ref, o_ref, bufs, sems,
                    *, ts, n_steps):
    i = pl.program_id(0)
    slot = i % 2

    def gather_tile(d_ref, s):
        def issue(it, carry):
            for j in range(ISSUE_UNROLL):
                r = it * ISSUE_UNROLL + j
                for k in range(TOP_K):
                    d = d_ref[r * TOP_K + k]
                    pltpu.make_async_copy(_slab(y_hbm, d), _slab(bufs.at[s, k], r),
                                          sems.at[s]).start(priority=k % 2)
            return carry

        lax.fori_loop(0, ts // ISSUE_UNROLL, issue, 0)

    @pl.when(i == 0)
    def _():
        gather_tile(dest_ref, 0)

    @pl.when(i + 1 < n_steps)
    def _():
        gather_tile(dest_next_ref, 1 - slot)

    for k in range(TOP_K):
        pltpu.make_async_copy(y_hbm.at[pl.ds(0, ts * SUBLANES)], bufs.at[slot, k], sems.at[slot]).wait()

    acc = x1_ref[...]
    gates = gates_ref[...]
    for k in range(TOP_K):
        acc = acc + _slabs_to_rows(bufs.at[slot, k], ts) * gates[:, k:k + 1]
    ms = jnp.mean(acc * acc, axis=-1, keepdims=True)
    o_ref[...] = acc * lax.rsqrt(ms + NORM_EPS) * fg_ref[...]


def _combine(y_rows, dest_flat, x1, gates, final_g):
    T, D = x1.shape
    ts = min(TS_COMB, T)
    n_steps = T // ts
    tok = lambda i: (i, 0)
    return pl.pallas_call(
        functools.partial(_combine_kernel, ts=ts, n_steps=n_steps),
        grid=(n_steps,),
        in_specs=[
            pl.BlockSpec((ts * TOP_K,), lambda i: (i,), memory_space=pltpu.SMEM),
            pl.BlockSpec((ts * TOP_K,), lambda i: (jnp.minimum(i + 1, n_steps - 1),),
                         memory_space=pltpu.SMEM),
            pl.BlockSpec(memory_space=pl.ANY),
            pl.BlockSpec((ts, D), tok),
            pl.BlockSpec((ts, LANES), tok),
            pl.BlockSpec((1, D), lambda i: (0, 0)),
        ],
        out_specs=pl.BlockSpec((ts, D), tok),
        out_shape=jax.ShapeDtypeStruct((T, D), F32),
        scratch_shapes=[pltpu.VMEM((2, TOP_K, ts * SUBLANES, LANES), F32),
                        pltpu.SemaphoreType.DMA((2,))],
        compiler_params=_cparams(("arbitrary",)),
        name="combine",
    )(dest_flat, dest_flat, y_rows, x1, gates, final_g.reshape(1, D))


def kernel(x, norm1_g, w_in, q_norm_g, k_norm_g, conv_w, conv_b, lru_wa, lru_ba, lru_wi, lru_bi,
           lru_lam, attn_out_g, lru_out_g, w_out, norm2_g, w_router, b_router, w_gate, b_gate,
           w_up, b_up, w_down, b_down, final_g):
    B, S, D = x.shape
    T = B * S
    assert w_in.shape[0] == 1, "single-layer trunk: the final norm is fused into the layer's combine"
    assert D == SUBLANES * LANES, "a token row is moved as one (8, 128) f32 slab"
    assert S % max(TS_IN, TQ, TK, TC_LRU) == 0 and S % GRID_W == 0, "sequence tiles must divide S"
    x2 = x.reshape(T, D)
    for l in range(1):
        qt, k, vt, lru_x, lru_gate = _inproj(x2, norm1_g[l], w_in[l], q_norm_g[l], k_norm_g[l], S)
        score_bound = (HEAD_DIM * Q_SCALE * jnp.max(jnp.abs(q_norm_g[l]))
                       * jnp.max(jnp.abs(k_norm_g[l])))
        lru_ops = _lru_operands(conv_w[l], conv_b[l], lru_wa[l], lru_ba[l], lru_wi[l], lru_bi[l],
                                lru_lam[l])
        attn, lru = _mixers(qt, k.reshape(B, S, -1), vt, lru_x.reshape(B, S, -1),
                            lru_gate.reshape(B, S, -1), lru_ops, score_bound, B, S)
        x1, xn3, route, gates, cnt = _outproj_router(
            attn.reshape(T, -1), lru.reshape(T, -1), x2, attn_out_g[l], lru_out_g[l], w_out[l],
            norm2_g[l], w_router[l], b_router[l])

        n_rows = T * TOP_K + N_EXPERTS * ROW_BLOCK
        pstart, block_e, fill, next_e, n_active = _routing_plan(cnt, n_rows // ROW_BLOCK)
        dest_flat = _dest_rows(route, pstart)
        x_rows = _dispatch(xn3, fill, dest_flat, n_rows)
        y_rows = _experts(x_rows, block_e, n_active, next_e, w_gate[l], b_gate[l], w_up[l], b_up[l],
                          w_down[l], b_down[l])
        x2 = _combine(y_rows, dest_flat, x1, gates, final_g)
    return x2.reshape(B, S, D)
```

```python
import functools
import math

import jax
import jax.numpy as jnp
import numpy as np
from jax import lax
from jax.experimental import pallas as pl
from jax.experimental.pallas import tpu as pltpu

F32 = jnp.float32
BF16 = jnp.bfloat16

GRID_W = 64
HEAD_DIM = 64
N_Q_HEADS = 8
N_KV_HEADS = 2
GQA_GROUP = N_Q_HEADS // N_KV_HEADS
ATTN_W = N_Q_HEADS * HEAD_DIM
KV_W = N_KV_HEADS * HEAD_DIM
LRU_BLOCKS = 8
LRU_C = 8.0
CONV_W = 4
CONV_PAD_L = 2
ROPE_THETA = 10000.0
ROPE_HALF = HEAD_DIM // 2
ROPE_M = ROPE_HALF // 2
N_EXPERTS = 32
TOP_K = 4
SWIGLU_ALPHA = 1.702
SWIGLU_LIMIT = 7.0
NORM_EPS = 1e-5
QK_EPS = 1e-6
LOG2_E = 1.4426950408889634
Q_SCALE = HEAD_DIM ** -0.5 * LOG2_E
SAFE_SCORE_LOG2 = 96.0

LANES = 128
SUBLANES = 8
BF16_SUBLANES = 16
PV_ROWS = HEAD_DIM + BF16_SUBLANES
VMEM_LIMIT = 48 * 1024 * 1024
EXPERT_VMEM_LIMIT = 56 * 1024 * 1024

TS_IN = 512
TQ = 256
TK = 256
KV_UNROLL = 32
HEADS_PER_STEP = 2
Q_TILES_PER_STEP = 2
TC_LRU = 512
LRU_UNROLL = 4
TS_OUT = 512
TS_DEST = 2048
ROW_BLOCK = 512
TS_DISP = 1024
TS_COMB = 512
ISSUE_UNROLL = 8


def _cparams(sem):
    return pltpu.CompilerParams(dimension_semantics=sem, vmem_limit_bytes=VMEM_LIMIT)


def _inproj_kernel(x_ref, g1_ref, wt_ref, w_ref, qg_ref, kg_ref, cos_ref, sin_ref, cost_ref, sint_ref,
                   q_ref, k_ref, v_ref, lx_ref, lg_ref, *, lru_w):
    x = x_ref[...]
    ms = jnp.mean(x * x, axis=-1, keepdims=True)
    xn = (x * lax.rsqrt(ms + NORM_EPS) * g1_ref[...]).astype(BF16)
    ht = lax.dot_general(wt_ref[...], xn, (((1,), (1,)), ((), ())), preferred_element_type=F32)
    h = jnp.dot(xn, w_ref[...], preferred_element_type=F32)

    qw = N_Q_HEADS * LANES
    kw = N_KV_HEADS * LANES
    cost = cost_ref[...]
    sint = sint_ref[...]
    row = lax.broadcasted_iota(jnp.int32, cost.shape, 0)
    first_half_t = (row % ROPE_HALF) < ROPE_M
    qg = qg_ref[...]
    for c in range(N_Q_HEADS):
        sl = slice(c * LANES, (c + 1) * LANES)
        xc = ht[sl]
        hms = jnp.sum(xc * xc, axis=0, keepdims=True) * (1.0 / HEAD_DIM)
        xc = xc * lax.rsqrt(hms + QK_EPS) * qg
        partner = jnp.where(first_half_t, pltpu.roll(xc, LANES - ROPE_M, 0), pltpu.roll(xc, ROPE_M, 0))
        q_ref[0, sl, :] = ((xc * cost + partner * sint) * Q_SCALE).astype(BF16)
    for c in range(N_KV_HEADS):
        sl = slice(c * LANES, (c + 1) * LANES)
        v_ref[0, sl, :] = jnp.where(row >= HEAD_DIM, 1.0, ht[qw + c * LANES: qw + (c + 1) * LANES]).astype(BF16)

    cos = cos_ref[...]
    sin = sin_ref[...]
    lane = lax.broadcasted_iota(jnp.int32, cos.shape, 1)
    first_half = (lane % ROPE_HALF) < ROPE_M
    for c in range(N_KV_HEADS):
        sl = slice(c * LANES, (c + 1) * LANES)
        xc = h[:, sl]
        hms = jnp.sum(xc * xc, axis=-1, keepdims=True) * (1.0 / HEAD_DIM)
        xc = xc * lax.rsqrt(hms + QK_EPS) * kg_ref[...]
        partner = jnp.where(first_half, pltpu.roll(xc, LANES - ROPE_M, 1), pltpu.roll(xc, ROPE_M, 1))
        k_ref[:, sl] = (xc * cos + partner * sin).astype(BF16)
    lx_ref[...] = h[:, kw: kw + lru_w]
    lg_ref[...] = h[:, kw + lru_w: kw + 2 * lru_w]


def _pad_heads(w, n_heads):
    lead = w.shape[:-1]
    w = w.reshape(lead + (n_heads, HEAD_DIM))
    w = jnp.pad(w, [(0, 0)] * len(lead) + [(0, 0), (0, LANES - HEAD_DIM)])
    return w.reshape(lead + (n_heads * LANES,))


def _rope_tables(S):
    t = np.arange(S)
    rows = (t // GRID_W).astype(np.float32)
    cols = (t % GRID_W).astype(np.float32)
    inv_freq = (ROPE_THETA ** (-np.arange(ROPE_M, dtype=np.float32) / ROPE_M)).astype(np.float32)
    ar = rows[:, None] * inv_freq[None, :]
    ac = cols[:, None] * inv_freq[None, :]
    cos = np.concatenate([np.cos(ar), np.cos(ar), np.cos(ac), np.cos(ac)], axis=-1)
    sin = np.concatenate([-np.sin(ar), np.sin(ar), -np.sin(ac), np.sin(ac)], axis=-1)
    pad = [(0, 0), (0, LANES - HEAD_DIM)]
    cos = np.pad(cos, pad).astype(np.float32)
    sin = np.pad(sin, pad).astype(np.float32)
    return cos, sin, np.ascontiguousarray(cos.T), np.ascontiguousarray(sin.T)


def _inproj(x2, norm1_g, w_in, q_norm_g, k_norm_g, S):
    T, D = x2.shape
    lru_w = (w_in.shape[1] - ATTN_W - 2 * KV_W) // 2
    o0, o1, o2 = ATTN_W, ATTN_W + KV_W, ATTN_W + 2 * KV_W
    w_t = jnp.concatenate([_pad_heads(w_in[:, :o0], N_Q_HEADS),
                           _pad_heads(w_in[:, o1:o2], N_KV_HEADS)], axis=1).T.astype(BF16)
    w_rest = jnp.concatenate([_pad_heads(w_in[:, o0:o1], N_KV_HEADS), w_in[:, o2:]],
                             axis=1).astype(BF16)
    qg = _pad_heads(q_norm_g.reshape(1, HEAD_DIM), 1).reshape(LANES, 1)
    kg = _pad_heads(k_norm_g.reshape(1, HEAD_DIM), 1)
    cos, sin, cos_t, sin_t = _rope_tables(S)
    ts = TS_IN
    n_s = S // ts
    qw, kw = N_Q_HEADS * LANES, N_KV_HEADS * LANES
    const = lambda i: (0, 0)
    tok = lambda i: (i, 0)
    pos = lambda i: (i % n_s, 0)
    pos_t = lambda i: (0, i % n_s)
    tposed = lambda i: (i // n_s, 0, i % n_s)
    return pl.pallas_call(
        functools.partial(_inproj_kernel, lru_w=lru_w),
        grid=(T // ts,),
        in_specs=[
            pl.BlockSpec((ts, D), tok),
            pl.BlockSpec((1, D), const),
            pl.BlockSpec(w_t.shape, const),
            pl.BlockSpec(w_rest.shape, const),
            pl.BlockSpec((LANES, 1), const),
            pl.BlockSpec((1, LANES), const),
            pl.BlockSpec((ts, LANES), pos),
            pl.BlockSpec((ts, LANES), pos),
            pl.BlockSpec((LANES, ts), pos_t),
            pl.BlockSpec((LANES, ts), pos_t),
        ],
        out_specs=[
            pl.BlockSpec((1, qw, ts), tposed),
            pl.BlockSpec((ts, kw), tok),
            pl.BlockSpec((1, kw, ts), tposed),
            pl.BlockSpec((ts, lru_w), tok),
            pl.BlockSpec((ts, lru_w), tok),
        ],
        out_shape=[
            jax.ShapeDtypeStruct((T // S, qw, S), BF16),
            jax.ShapeDtypeStruct((T, kw), BF16),
            jax.ShapeDtypeStruct((T // S, kw, S), BF16),
            jax.ShapeDtypeStruct((T, lru_w), F32),
            jax.ShapeDtypeStruct((T, lru_w), F32),
        ],
        compiler_params=_cparams(("parallel",)),
        name="inproj",
    )(x2, norm1_g.reshape(1, D), w_t, w_rest, qg, kg, cos, sin, cos_t, sin_t)


def _attn_kernel(qt_ref, k_ref, vt_ref, o_ref, acc_ref, s_ref, p_ref, *, tq, tk, n_kv, kv_unroll):
    hp = HEADS_PER_STEP
    spt = GQA_GROUP // hp
    acc_ref[...] = jnp.zeros(acc_ref.shape, F32)

    def scores(j, sp):
        kt = k_ref[0, pl.ds(pl.multiple_of(j * tk, tk), tk), :]
        out = []
        for u in range(hp):
            g = sp * hp + u
            s = jnp.dot(kt, qt_ref[0, g * LANES:(g + 1) * LANES, :], preferred_element_type=F32)
            out.append((s, jnp.max(s, axis=0, keepdims=True)))
        return out

    def softmax_stage(sc, ms, sp):
        out = []
        for u, (s, s_max) in enumerate(sc):
            h = sp * hp + u
            m_new = jnp.maximum(ms[h], s_max)
            out.append((jnp.exp2(ms[h] - m_new), jnp.exp2(s - m_new).astype(BF16)))
            ms[h] = m_new
        return out

    def pv_stage(j, sp, ap):
        vt = vt_ref[0, 0:PV_ROWS, pl.ds(pl.multiple_of(j * tk, tk), tk)]
        for u, (alpha, p) in enumerate(ap):
            g = sp * hp + u
            acc_ref[g] = alpha * acc_ref[g] + jnp.dot(vt, p, preferred_element_type=F32)

    ms = [jnp.full((1, tq), -jnp.inf, F32)] * GQA_GROUP
    ap = softmax_stage(scores(0, 0), ms, 0)
    sc = scores(min(1 // spt, n_kv - 1), 1 % spt)
    for u in range(hp):
        s_ref[u] = sc[u][0]
        p_ref[u] = ap[u][1]

    def body(it, carry):
        ms = list(carry[:GQA_GROUP])
        ap = [(carry[GQA_GROUP + u], p_ref[u]) for u in range(hp)]
        sc = [(s_ref[u], carry[GQA_GROUP + hp + u]) for u in range(hp)]
        for n in range(kv_unroll * spt):
            j = it * kv_unroll + n // spt
            j_next = jnp.minimum(it * kv_unroll + (n + 2) // spt, n_kv - 1)
            sc_next = scores(j_next, (n + 2) % spt)
            ap_next = softmax_stage(sc, ms, (n + 1) % spt)
            pv_stage(j, n % spt, ap)
            sc, ap = sc_next, ap_next
        for u in range(hp):
            s_ref[u] = sc[u][0]
            p_ref[u] = ap[u][1]
        return tuple(ms) + tuple(a for a, _ in ap) + tuple(m for _, m in sc)

    lax.fori_loop(0, n_kv // kv_unroll, body,
                  tuple(ms) + tuple(a for a, _ in ap) + tuple(m for _, m in sc))
    _attn_finalize(acc_ref, o_ref, tq)


def _attn_finalize(acc_ref, o_ref, tq, row0=0):
    pad = jnp.zeros((LANES - HEAD_DIM, tq), F32)
    for g in range(GQA_GROUP):
        acc = acc_ref[g]
        o = acc[0:HEAD_DIM] / acc[HEAD_DIM:HEAD_DIM + 1, :]
        o_ref[0, row0:row0 + tq, g * LANES:(g + 1) * LANES] = (
            jnp.concatenate([o, pad], axis=0).T.astype(BF16))


def _attn_bounded_kernel(qt_ref, k_ref, vt_ref, o_ref, acc_ref, s_ref, p_ref, *, tq, tk, n_kv, kv_unroll,
                         q_tiles):
    for t in range(q_tiles):
        _attn_bounded_tile(qt_ref, k_ref, vt_ref, o_ref, acc_ref.at[t], s_ref.at[t], p_ref.at[t],
                           tq=tq, tk=tk, n_kv=n_kv, kv_unroll=kv_unroll, q0=t * tq)


def _attn_bounded_tile(qt_ref, k_ref, vt_ref, o_ref, acc_ref, s_ref, p_ref, *, tq, tk, n_kv, kv_unroll,
                       q0):
    hp = HEADS_PER_STEP
    spt = GQA_GROUP // hp
    acc_ref[...] = jnp.zeros(acc_ref.shape, F32)

    def scores(j, sp):
        kt = k_ref[0, pl.ds(pl.multiple_of(j * tk, tk), tk), :]
        return [jnp.dot(kt, qt_ref[0, (sp * hp + u) * LANES:(sp * hp + u + 1) * LANES, q0:q0 + tq],
                        preferred_element_type=F32) for u in range(hp)]

    def probs(sc):
        return [jnp.exp2(s).astype(BF16) for s in sc]

    def pv_stage(j, sp, ps):
        vt = vt_ref[0, 0:PV_ROWS, pl.ds(pl.multiple_of(j * tk, tk), tk)]
        for u, p in enumerate(ps):
            acc_ref[sp * hp + u] += jnp.dot(vt, p, preferred_element_type=F32)

    ps = probs(scores(0, 0))
    sc = scores(min(1 // spt, n_kv - 1), 1 % spt)
    for u in range(hp):
        s_ref[u] = sc[u]
        p_ref[u] = ps[u]

    def body(it, carry):
        ps = [p_ref[u] for u in range(hp)]
        sc = [s_ref[u] for u in range(hp)]
        for n in range(kv_unroll * spt):
            j = it * kv_unroll + n // spt
            j_next = jnp.minimum(it * kv_unroll + (n + 2) // spt, n_kv - 1)
            sc_next = scores(j_next, (n + 2) % spt)
            ps_next = probs(sc)
            pv_stage(j, n % spt, ps)
            sc, ps = sc_next, ps_next
        for u in range(hp):
            s_ref[u] = sc[u]
            p_ref[u] = ps[u]
        return carry

    lax.fori_loop(0, n_kv // kv_unroll, body, 0)
    _attn_finalize(acc_ref, o_ref, tq, q0)


def _attention(qt, k, vt, *, bounded, B, S):
    tq = min(TQ, S)
    tk = min(TK, S)
    gw = GQA_GROUP * LANES
    tiling = dict(tq=tq, tk=tk, n_kv=S // tk, kv_unroll=math.gcd(S // tk, KV_UNROLL))
    if bounded:
        q_tiles = math.gcd(S // tq, Q_TILES_PER_STEP)
        body = functools.partial(_attn_bounded_kernel, q_tiles=q_tiles, **tiling)
        scratch = [pltpu.VMEM((q_tiles, GQA_GROUP, PV_ROWS, tq), F32),
                   pltpu.VMEM((q_tiles, HEADS_PER_STEP, tk, tq), F32),
                   pltpu.VMEM((q_tiles, HEADS_PER_STEP, tk, tq), BF16)]
    else:
        q_tiles = 1
        body = functools.partial(_attn_kernel, **tiling)
        scratch = [pltpu.VMEM((GQA_GROUP, PV_ROWS, tq), F32),
                   pltpu.VMEM((HEADS_PER_STEP, tk, tq), F32),
                   pltpu.VMEM((HEADS_PER_STEP, tk, tq), BF16)]
    tqs = q_tiles * tq
    return pl.pallas_call(
        body,
        grid=(B, N_KV_HEADS, S // tqs),
        in_specs=[
            pl.BlockSpec((1, gw, tqs), lambda b, h, i: (b, h, i)),
            pl.BlockSpec((1, S, LANES), lambda b, h, i: (b, 0, h)),
            pl.BlockSpec((1, LANES, S), lambda b, h, i: (b, h, 0)),
        ],
        out_specs=pl.BlockSpec((1, tqs, gw), lambda b, h, i: (b, i, h)),
        out_shape=jax.ShapeDtypeStruct((B, S, N_Q_HEADS * LANES), BF16),
        scratch_shapes=scratch,
        compiler_params=_cparams(("parallel", "parallel", "parallel")),
        name="attention_bounded" if bounded else "attention",
    )(qt, k, vt)


def _scan_chunk(a, b, h_in, reverse):
    n = a.shape[0]
    n_groups = n // SUBLANES
    a = a.reshape(n_groups, SUBLANES, LANES)
    b = b.reshape(n_groups, SUBLANES, LANES)
    sub = lax.broadcasted_iota(jnp.int32, a.shape, 1)
    d = 1
    while d < SUBLANES:
        if reverse:
            keep = sub < SUBLANES - d
            shift = SUBLANES - d
        else:
            keep = sub >= d
            shift = d
        a_sh = jnp.where(keep, pltpu.roll(a, shift, 1), 1.0)
        b_sh = jnp.where(keep, pltpu.roll(b, shift, 1), 0.0)
        b = a * b_sh + b
        a = a * a_sh
        d *= 2
    a = a.reshape(n, LANES)
    b = b.reshape(n, LANES)
    order = range(n_groups - 1, -1, -1) if reverse else range(n_groups)
    edge = h_in
    out = [None] * n_groups
    for v in order:
        rows = slice(v * SUBLANES, (v + 1) * SUBLANES)
        hv = b[rows] + a[rows] * jnp.broadcast_to(edge, (SUBLANES, LANES))
        out[v] = hv
        edge = hv[0:1] if reverse else hv[SUBLANES - 1:SUBLANES]
    return jnp.concatenate(out, axis=0), edge


def _lru_pad_input(u_ref, up_ref, S):
    zeros = jnp.zeros((SUBLANES, LANES), F32)
    up_ref[0:SUBLANES, :] = zeros
    up_ref[S + SUBLANES:S + 2 * SUBLANES, :] = zeros
    up_ref[SUBLANES:S + SUBLANES, :] = u_ref[0]


def _lru_gates(up_ref, cw_ref, cb_ref, w_ref, bias_ref, lam_ref, t0, tc, d):
    cw = cw_ref[...]
    xc = cb_ref[...]
    for j in range(CONV_W):
        xc = xc + up_ref[pl.ds(t0 + SUBLANES + j - CONV_PAD_L, tc), :] * cw[j:j + 1, :]
    gw = 2 * LANES
    g = jnp.dot(xc.astype(BF16), w_ref[0, :, d * gw:(d + 1) * gw],
                preferred_element_type=F32) + bias_ref[0, :, d * gw:(d + 1) * gw]
    r = jax.nn.sigmoid(g[:, :LANES])
    i = jax.nn.sigmoid(g[:, LANES:])
    a = jnp.exp(-LRU_C * r * jax.nn.softplus(-lam_ref[d:d + 1, :]))
    y = 1.0 - a * a
    b = jnp.where(y > 0.0, y * lax.rsqrt(y), 0.0) * i * xc
    return a, b


def _lru_kernel(u_ref, gate_ref, cw_ref, cb_ref, w_ref, bias_ref, lam_ref, o_ref,
                up_ref, hf_ref, *, S, tc):
    _lru_pad_input(u_ref, up_ref, S)
    n_chunks = S // tc
    params = (up_ref, cw_ref, cb_ref, w_ref, bias_ref, lam_ref)

    def fwd(c, h):
        t0 = pl.multiple_of(c * tc, tc)
        hc, h_last = _scan_chunk(*_lru_gates(*params, t0, tc, 0), h, False)
        hf_ref[pl.ds(t0, tc), :] = hc
        return h_last

    unroll = math.gcd(n_chunks, LRU_UNROLL)

    def grouped(step):
        def body(i, h):
            for j in range(unroll):
                h = step(i * unroll + j, h)
            return h
        return body

    lax.fori_loop(0, n_chunks // unroll, grouped(fwd), jnp.zeros((1, LANES), F32))

    def bwd(ci, h):
        t0 = pl.multiple_of((n_chunks - 1 - ci) * tc, tc)
        hc, h_last = _scan_chunk(*_lru_gates(*params, t0, tc, 1), h, True)
        gate = gate_ref[0, pl.ds(t0, tc), :]
        o_ref[0, pl.ds(t0, tc), :] = (hf_ref[pl.ds(t0, tc), :] + hc) * jax.nn.gelu(gate)
        return h_last

    lax.fori_loop(0, n_chunks // unroll, grouped(bwd), jnp.zeros((1, LANES), F32))


def _block_diag_pairs(w):
    nb, bw, _ = w.shape
    w = w.reshape(nb // 2, 2, bw, bw)
    z = jnp.zeros_like(w[:, 0])
    top = jnp.concatenate([w[:, 0], z], axis=-1)
    bot = jnp.concatenate([z, w[:, 1]], axis=-1)
    return jnp.concatenate([top, bot], axis=-2)


def _lru_operands(conv_w, conv_b, wa, ba, wi, bi, lam):
    C = conv_b.shape[0]
    nc = C // LANES
    w = jnp.concatenate([_block_diag_pairs(wa[0]), _block_diag_pairs(wi[0]),
                         _block_diag_pairs(wa[1]), _block_diag_pairs(wi[1])], axis=-1).astype(BF16)
    bias = jnp.stack([ba[0].reshape(nc, LANES), bi[0].reshape(nc, LANES),
                      ba[1].reshape(nc, LANES), bi[1].reshape(nc, LANES)], axis=1)
    return conv_w, conv_b.reshape(1, C), w, bias.reshape(nc, 1, 4 * LANES), lam


def _lru_specs(S, unit):
    seq = lambda *g: (unit(*g)[0], 0, unit(*g)[1])
    chan = lambda *g: (0, unit(*g)[1])
    blk = lambda *g: (unit(*g)[1], 0, 0)
    in_specs = [
        pl.BlockSpec((1, S, LANES), seq),
        pl.BlockSpec((1, S, LANES), seq),
        pl.BlockSpec((CONV_W, LANES), chan),
        pl.BlockSpec((1, LANES), chan),
        pl.BlockSpec((1, LANES, 4 * LANES), blk),
        pl.BlockSpec((1, 1, 4 * LANES), blk),
        pl.BlockSpec((2, LANES), chan),
    ]
    return in_specs, pl.BlockSpec((1, S, LANES), seq)


def _lru(lru_x, lru_gate, lru_ops, B, S):
    C = lru_x.shape[-1]
    tc = min(TC_LRU, S)
    in_specs, out_spec = _lru_specs(S, lambda b, c: (b, c))
    return pl.pallas_call(
        functools.partial(_lru_kernel, S=S, tc=tc),
        grid=(B, C // LANES),
        in_specs=in_specs,
        out_specs=out_spec,
        out_shape=jax.ShapeDtypeStruct((B, S, C), F32),
        scratch_shapes=[
            pltpu.VMEM((S + 2 * SUBLANES, LANES), F32),
            pltpu.VMEM((S, LANES), F32),
        ],
        compiler_params=_cparams(("parallel", "parallel")),
        name="rglru",
    )(lru_x, lru_gate, *lru_ops)


def _mixers(qt, k, vt, lru_x, lru_gate, lru_ops, score_bound, B, S):
    attn = lax.cond(score_bound <= SAFE_SCORE_LOG2,
                    functools.partial(_attention, bounded=True, B=B, S=S),
                    functools.partial(_attention, bounded=False, B=B, S=S), qt, k, vt)
    return attn, _lru(lru_x, lru_gate, lru_ops, B, S)


def _rows_to_slabs(ref, x):
    n = x.shape[0]
    for s in range(SUBLANES):
        ref[pl.ds(s, n, stride=SUBLANES), :] = x[:, s * LANES:(s + 1) * LANES]


def _slabs_to_rows(ref, n):
    return jnp.concatenate([ref[pl.ds(s, n, stride=SUBLANES), :] for s in range(SUBLANES)], axis=1)


def _slab(ref, r):
    return ref.at[pl.ds(pl.multiple_of(r * SUBLANES, SUBLANES), SUBLANES)]


def _outproj_kernel(a_ref, l_ref, x_ref, ag_ref, lg_ref, wa_ref, wl_ref, g2_ref,
                    wr_ref, br_ref, tri_ref,
                    x1_ref, xn3_ref, route_ref, gates_ref, cnt_ref, carry_ref, *, attn_w, lru_w):
    step = pl.program_id(0)

    @pl.when(step == 0)
    def _():
        carry_ref[...] = jnp.zeros_like(carry_ref)

    a = a_ref[...].astype(F32)
    ams = jnp.sum(a * a, axis=-1, keepdims=True) * (1.0 / attn_w)
    an = a * lax.rsqrt(ams + NORM_EPS) * ag_ref[...]
    l = l_ref[...]
    lms = jnp.sum(l * l, axis=-1, keepdims=True) * (1.0 / lru_w)
    ln = l * lax.rsqrt(lms + NORM_EPS) * lg_ref[...]
    mix = (jnp.dot(an.astype(BF16), wa_ref[...], preferred_element_type=F32)
           + jnp.dot(ln.astype(BF16), wl_ref[...], preferred_element_type=F32))
    x1 = x_ref[...] + mix
    x1_ref[...] = x1
    ms = jnp.mean(x1 * x1, axis=-1, keepdims=True)
    xn = x1 * lax.rsqrt(ms + NORM_EPS) * g2_ref[...]
    _rows_to_slabs(xn3_ref, xn)

    logits = jnp.dot(xn.astype(BF16), wr_ref[...], preferred_element_type=F32) + br_ref[...]
    lane = lax.broadcasted_iota(jnp.int32, logits.shape, 1)
    neg = -jnp.inf
    work = jnp.where(lane < N_EXPERTS, logits, neg)
    sel = jnp.zeros(logits.shape, F32)
    idxs, vals = [], []
    for _ in range(TOP_K):
        m = jnp.max(work, axis=1, keepdims=True)
        idx = jnp.min(jnp.where(work == m, lane, LANES), axis=1, keepdims=True)
        hit = lane == idx
        work = jnp.where(hit, neg, work)
        sel = sel + hit.astype(F32)
        idxs.append(idx)
        vals.append(m)
    es = [jnp.exp(v - vals[0]) for v in vals]
    den = es[0] + es[1] + es[2] + es[3]

    prefix = jnp.dot(tri_ref[...], sel.astype(BF16), preferred_element_type=F32) + carry_ref[...]
    carry_ref[...] = carry_ref[...] + jnp.sum(sel, axis=0, keepdims=True)
    cnt_ref[...] = carry_ref[...]

    route = jnp.zeros(logits.shape, jnp.int32)
    gates = jnp.zeros(logits.shape, F32)
    for k in range(TOP_K):
        rank = jnp.sum(jnp.where(lane == idxs[k], prefix, 0.0), axis=1, keepdims=True).astype(jnp.int32)
        route = jnp.where(lane == k, idxs[k], route)
        route = jnp.where(lane == TOP_K + k, rank, route)
        gates = jnp.where(lane == k, es[k] / den, gates)
    route_ref[...] = route
    gates_ref[...] = gates


def _outproj_router(attn, lru, x2, attn_out_g, lru_out_g, w_out, norm2_g, w_router, b_router):
    T, D = x2.shape
    lru_w = lru.shape[-1]
    ts = min(TS_OUT, T)
    wa = w_out[:ATTN_W].reshape(N_Q_HEADS, HEAD_DIM, D)
    wa = jnp.pad(wa, ((0, 0), (0, LANES - HEAD_DIM), (0, 0))).reshape(N_Q_HEADS * LANES, D).astype(BF16)
    wl = w_out[ATTN_W:].astype(BF16)
    ag = _pad_heads(attn_out_g.reshape(1, ATTN_W), N_Q_HEADS)
    wr = jnp.pad(w_router, ((0, 0), (0, LANES - N_EXPERTS))).astype(BF16)
    br =jnp.pad(b_router.reshape(1, N_EXPERTS), ((0, 0), (0, LANES - N_EXPERTS)))
    tri = (jnp.arange(ts)[:, None] > jnp.arange(ts)[None, :]).astype(BF16)
    const = lambda i: (0, 0)
    tok = lambda i: (i, 0)
    aw = N_Q_HEADS * LANES
    return pl.pallas_call(
        functools.partial(_outproj_kernel, attn_w=ATTN_W, lru_w=lru_w),
        grid=(T // ts,),
        in_specs=[
            pl.BlockSpec((ts, aw), tok),
            pl.BlockSpec((ts, lru_w), tok),
            pl.BlockSpec((ts, D), tok),
            pl.BlockSpec((1, aw), const),
            pl.BlockSpec((1, lru_w), const),
            pl.BlockSpec((aw, D), const),
            pl.BlockSpec((lru_w, D), const),
            pl.BlockSpec((1, D), const),
            pl.BlockSpec((D, LANES), const),
            pl.BlockSpec((1, LANES), const),
            pl.BlockSpec((ts, ts), const),
        ],
        out_specs=[
            pl.BlockSpec((ts, D), tok),
            pl.BlockSpec((ts * SUBLANES, LANES), tok),
            pl.BlockSpec((ts, LANES), tok),
            pl.BlockSpec((ts, LANES), tok),
            pl.BlockSpec((1, LANES), const),
        ],
        out_shape=[
            jax.ShapeDtypeStruct((T, D), F32),
            jax.ShapeDtypeStruct((T * SUBLANES, LANES), F32),
            jax.ShapeDtypeStruct((T, LANES), jnp.int32),
            jax.ShapeDtypeStruct((T, LANES), F32),
            jax.ShapeDtypeStruct((1, LANES), F32),
        ],
        scratch_shapes=[pltpu.VMEM((1, LANES), F32)],
        compiler_params=_cparams(("arbitrary",)),
        name="outproj_router",
    )(attn, lru, x2, ag, lru_out_g.reshape(1, lru_w), wa, wl, norm2_g.reshape(1, D),
      wr, br, tri)


def _plan_kernel(cnt_ref, pstart_ref, plan_ref):
    cnt = cnt_ref[...]
    lane = lax.broadcasted_iota(jnp.int32, cnt.shape, 1)
    padded = jnp.floor((cnt + (ROW_BLOCK - 1)) * (1.0 / ROW_BLOCK)) * ROW_BLOCK
    pend = padded
    d = 1
    while d < N_EXPERTS:
        pend = pend + jnp.where(lane >= d, pltpu.roll(pend, d, 1), 0.0)
        d *= 2
    pstart_ref[...] = pend - padded
    total = jnp.max(pend, axis=1, keepdims=True)

    shape = plan_ref.shape
    lanes = lax.broadcasted_iota(jnp.int32, shape, 1)
    is_expert = lanes < N_EXPERTS
    start = lax.broadcasted_iota(jnp.int32, shape, 0).astype(F32) * ROW_BLOCK

    def groups_ending_by(row):
        return jnp.sum(jnp.where(jnp.logical_and(pend <= row, is_expert), 1.0, 0.0), axis=1, keepdims=True)

    block_e = jnp.minimum(groups_ending_by(start), N_EXPERTS - 1.0)
    tail = jnp.max(jnp.where(jnp.logical_and(jnp.logical_and(pend == start + ROW_BLOCK, padded > 0.0),
                                             is_expert), 1.0, 0.0), axis=1, keepdims=True)
    fill = jnp.maximum(tail, jnp.where(start[:, 0:1] >= total, 1.0, 0.0))
    group_end = jnp.sum(jnp.where(lanes.astype(F32) == block_e, pend, 0.0), axis=1, keepdims=True)
    next_e = jnp.where(group_end < total,
                       jnp.minimum(groups_ending_by(group_end), N_EXPERTS - 1.0), -1.0)
    plan = jnp.where(lanes == 0, block_e,
                     jnp.where(lanes == 1, fill,
                               jnp.where(lanes == 2, next_e, total * (1.0 / ROW_BLOCK))))
    plan_ref[...] = plan.astype(jnp.int32)


def _routing_plan(cnt, n_blocks):
    assert ROW_BLOCK & (ROW_BLOCK - 1) == 0, "exact f32 division by the row block size"
    rows = -(-n_blocks // SUBLANES) * SUBLANES
    pstart, plan = pl.pallas_call(
        _plan_kernel,
        out_shape=[jax.ShapeDtypeStruct((1, LANES), F32),
                   jax.ShapeDtypeStruct((rows, LANES), jnp.int32)],
        name="routing_plan",
    )(cnt)
    return pstart, plan[:n_blocks, 0], plan[:n_blocks, 1], plan[:n_blocks, 2], plan[0:1, 3]


def _dest_kernel(route_ref, pstart_ref, dest_ref):
    route = route_ref[...]
    lane = lax.broadcasted_iota(jnp.int32, route.shape, 1)
    pstart = pstart_ref[...]
    dest = jnp.zeros(route.shape, jnp.int32)
    for k in range(TOP_K):
        start = jnp.sum(jnp.where(lane == route[:, k:k + 1], pstart, 0.0), axis=1, keepdims=True)
        dest = jnp.where(lane == k, start.astype(jnp.int32) + route[:, TOP_K + k:TOP_K + k + 1], dest)
    dest_ref[...] = dest


def _dest_rows(route, pstart):
    T = route.shape[0]
    ts = math.gcd(TS_DEST, T)
    dest = pl.pallas_call(
        _dest_kernel,
        grid=(T // ts,),
        in_specs=[pl.BlockSpec((ts, LANES), lambda i: (i, 0)),
                  pl.BlockSpec((1, LANES), lambda i: (0, 0))],
        out_specs=pl.BlockSpec((ts, LANES), lambda i: (i, 0)),
        out_shape=jax.ShapeDtypeStruct((T, LANES), jnp.int32),
        compiler_params=_cparams(("parallel",)),
        name="dest_rows",
    )(route, pstart)
    return dest[:, :TOP_K].reshape(T * TOP_K)


def _dispatch_kernel(fill_ref, dest_ref, x_ref, out_hbm, zero_ref, sem, zero_sem, *, ts, n_blocks):
    block_slabs = ROW_BLOCK * SUBLANES

    def fill_copy(b):
        off = pl.multiple_of(b * block_slabs, block_slabs)
        return pltpu.make_async_copy(zero_ref, out_hbm.at[pl.ds(off, block_slabs)], zero_sem)

    @pl.when(pl.program_id(0) == 0)
    def _():
        zero_ref[...] = jnp.zeros(zero_ref.shape, F32)

        def start(b, carry):
            @pl.when(fill_ref[b] != 0)
            def _():
                fill_copy(b).start()
            return carry

        def wait(b, carry):
            @pl.when(fill_ref[b] != 0)
            def _():
                fill_copy(b).wait()
            return carry

        lax.fori_loop(0, n_blocks, start, 0)
        lax.fori_loop(0, n_blocks, wait, 0)

    def issue(i, carry):
        for j in range(ISSUE_UNROLL):
            r = i * ISSUE_UNROLL + j
            for k in range(TOP_K):
                d = dest_ref[r * TOP_K + k]
                pltpu.make_async_copy(_slab(x_ref, r), _slab(out_hbm, d), sem).start(priority=k % 2)
        return carry

    lax.fori_loop(0, ts // ISSUE_UNROLL, issue, 0)
    for k in range(TOP_K):
        pltpu.make_async_copy(x_ref, out_hbm.at[pl.ds(0, ts * SUBLANES)], sem).wait()


def _dispatch(xn_slabs, fill, dest_flat, n_rows):
    T = xn_slabs.shape[0] // SUBLANES
    ts = min(TS_DISP, T)
    grid_spec = pltpu.PrefetchScalarGridSpec(
        num_scalar_prefetch=1,
        grid=(T // ts,),
        in_specs=[
            pl.BlockSpec((ts * TOP_K,), lambda i, fl: (i,), memory_space=pltpu.SMEM),
            pl.BlockSpec((ts * SUBLANES, LANES), lambda i, fl: (i, 0)),
        ],
        out_specs=pl.BlockSpec(memory_space=pl.ANY),
        scratch_shapes=[pltpu.VMEM((ROW_BLOCK * SUBLANES, LANES), F32),
                        pltpu.SemaphoreType.DMA, pltpu.SemaphoreType.DMA],
    )
    return pl.pallas_call(
        functools.partial(_dispatch_kernel, ts=ts, n_blocks=n_rows // ROW_BLOCK),
        grid_spec=grid_spec,
        out_shape=jax.ShapeDtypeStruct((n_rows * SUBLANES, LANES), xn_slabs.dtype),
        compiler_params=_cparams(("arbitrary",)),
        name="dispatch",
    )(fill, dest_flat, xn_slabs)


def _expert_loop_kernel(be_ref, na_ref, nxt_ref, x_hbm, wg_hbm, bg_ref, wu_hbm, bu_ref, wd_hbm, bd_ref,
                        y_hbm, xbuf, ybuf, stage_ref, wb_ref, x_sems, y_sems, w_sems):
    w_hbm = (wg_hbm, wu_hbm, wd_hbm)
    block_slabs = ROW_BLOCK * SUBLANES
    n_active = na_ref[0]

    def rows(b):
        return pl.ds(pl.multiple_of(b * block_slabs, block_slabs), block_slabs)

    def x_copy(b, s):
        return pltpu.make_async_copy(x_hbm.at[rows(b)], xbuf.at[s], x_sems.at[s])

    def y_copy(b, s):
        return pltpu.make_async_copy(ybuf.at[s], y_hbm.at[rows(b)], y_sems.at[s])

    def fetch(expert, s, m):
        return pltpu.make_async_copy(w_hbm[m].at[expert], stage_ref.at[s, m], w_sems.at[s, m])

    x_copy(0, 0).start()
    for m in range(3):
        fetch(be_ref[0], 0, m).start()

    def body(b, wslot):
        s = b % 2
        e = be_ref[b]
        x_copy(b, s).wait()

        @pl.when(b + 1 < n_active)
        def _():
            x_copy(b + 1, 1 - s).start()

        first = jnp.logical_or(b == 0, e != be_ref[jnp.maximum(b - 1, 0)])

        @pl.when(first)
        def _():
            for m in range(3):
                fetch(e, wslot, m).wait()
                wb_ref[m] = stage_ref[wslot, m].astype(BF16)

            @pl.when(nxt_ref[b] >= 0)
            def _():
                for m in range(3):
                    fetch(nxt_ref[b], 1 - wslot, m).start()

        @pl.when(b >= 2)
        def _():
            y_copy(b - 2, s).wait()

        x = _slabs_to_rows(xbuf.at[s], ROW_BLOCK).astype(BF16)
        g = jnp.dot(x, wb_ref[0], preferred_element_type=F32) + bg_ref[e]
        u = jnp.dot(x, wb_ref[1], preferred_element_type=F32) + bu_ref[e]
        g = jnp.minimum(g, SWIGLU_LIMIT)
        u = jnp.clip(u, -SWIGLU_LIMIT, SWIGLU_LIMIT)
        glu = g * jax.nn.sigmoid(SWIGLU_ALPHA * g)
        y = jnp.dot(((u + 1.0) * glu).astype(BF16), wb_ref[2], preferred_element_type=F32) + bd_ref[e]
        _rows_to_slabs(ybuf.at[s], y)
        y_copy(b, s).start()
        return jnp.where(first, 1 - wslot, wslot)

    lax.fori_loop(0, n_active, body, 0)

    @pl.when(n_active >= 2)
    def _():
        y_copy(n_active - 2, n_active % 2).wait()

    y_copy(n_active - 1, (n_active - 1) % 2).wait()


def _experts(x_rows, block_e, n_active, next_e, w_gate, b_gate, w_up, b_up, w_down, b_down):
    E, D, FF = w_gate.shape
    assert D == FF, "the three expert matrices share one staging shape"
    block_slabs = ROW_BLOCK * SUBLANES
    whole = lambda i, be, na, nx: (0, 0, 0)

    grid_spec = pltpu.PrefetchScalarGridSpec(
        num_scalar_prefetch=3,
        grid=(1,),
        in_specs=[
            pl.BlockSpec(memory_space=pl.ANY),
            pl.BlockSpec(memory_space=pl.ANY),
            pl.BlockSpec((E, 1, FF), whole),
            pl.BlockSpec(memory_space=pl.ANY),
            pl.BlockSpec((E, 1, FF), whole),
            pl.BlockSpec(memory_space=pl.ANY),
            pl.BlockSpec((E, 1, D), whole),
        ],
        out_specs=pl.BlockSpec(memory_space=pl.ANY),
        scratch_shapes=[
            pltpu.VMEM((2, block_slabs, LANES), F32),
            pltpu.VMEM((2, block_slabs, LANES), F32),
            pltpu.VMEM((2, 3, D, FF), F32),
            pltpu.VMEM((3, D, FF), BF16),
            pltpu.SemaphoreType.DMA((2,)),
            pltpu.SemaphoreType.DMA((2,)),
            pltpu.SemaphoreType.DMA((2, 3)),
        ],
    )
    return pl.pallas_call(
        _expert_loop_kernel,
        grid_spec=grid_spec,
        out_shape=jax.ShapeDtypeStruct(x_rows.shape, F32),
        input_output_aliases={3: 0},
        compiler_params=pltpu.CompilerParams(dimension_semantics=("arbitrary",),
                                             vmem_limit_bytes=EXPERT_VMEM_LIMIT),
        name="experts",
    )(block_e, n_active, next_e, x_rows, w_gate, b_gate.reshape(E, 1, FF), w_up,
      b_up.reshape(E, 1, FF), w_down, b_down.reshape(E, 1, D))


def _combine_kernel(dest_ref, dest_next_ref, y_hbm, x1_ref, gates_ref, fg_ref, o_ref, bufs, sems,
                    *, ts, n_steps):
    i = pl.program_id(0)
    slot = i % 2

    def gather_tile(d_ref, s):
        def issue(it, carry):
            for j in range(ISSUE_UNROLL):
                r = it * ISSUE_UNROLL + j
                for k in range(TOP_K):
                    d = d_ref[r * TOP_K + k]
                    pltpu.make_async_copy(_slab(y_hbm, d), _slab(bufs.at[s, k], r),
                                          sems.at[s]).start(priority=k % 2)
            return carry

        lax.fori_loop(0, ts // ISSUE_UNROLL, issue, 0)

    @pl.when(i == 0)
    def _():
        gather_tile(dest_ref, 0)

    @pl.when(i + 1 < n_steps)
    def _():
        gather_tile(dest_next_ref, 1 - slot)

    for k in range(TOP_K):
        pltpu.make_async_copy(y_hbm.at[pl.ds(0, ts * SUBLANES)], bufs.at[slot, k], sems.at[slot]).wait()

    acc = x1_ref[...]
    gates = gates_ref[...]
    for k in range(TOP_K):
        acc = acc + _slabs_to_rows(bufs.at[slot, k], ts) * gates[:, k:k + 1]
    ms = jnp.mean(acc * acc, axis=-1, keepdims=True)
    o_ref[...] = acc * lax.rsqrt(ms + NORM_EPS) * fg_ref[...]


def _combine(y_rows, dest_flat, x1, gates, final_g):
    T, D = x1.shape
    ts = min(TS_COMB, T)
    n_steps = T // ts
    tok = lambda i: (i, 0)
    return pl.pallas_call(
        functools.partial(_combine_kernel, ts=ts, n_steps=n_steps),
        grid=(n_steps,),
        in_specs=[
            pl.BlockSpec((ts * TOP_K,), lambda i: (i,), memory_space=pltpu.SMEM),
            pl.BlockSpec((ts * TOP_K,), lambda i: (jnp.minimum(i + 1, n_steps - 1),),
                         memory_space=pltpu.SMEM),
            pl.BlockSpec(memory_space=pl.ANY),
            pl.BlockSpec((ts, D), tok),
            pl.BlockSpec((ts, LANES), tok),
            pl.BlockSpec((1, D), lambda i: (0, 0)),
        ],
        out_specs=pl.BlockSpec((ts, D), tok),
        out_shape=jax.ShapeDtypeStruct((T, D), F32),
        scratch_shapes=[pltpu.VMEM((2, TOP_K, ts * SUBLANES, LANES), F32),
                        pltpu.SemaphoreType.DMA((2,))],
        compiler_params=_cparams(("arbitrary",)),
        name="combine",
    )(dest_flat, dest_flat, y_rows, x1, gates, final_g.reshape(1, D))


def kernel(x, norm1_g, w_in, q_norm_g, k_norm_g, conv_w, conv_b, lru_wa, lru_ba, lru_wi, lru_bi,
           lru_lam, attn_out_g, lru_out_g, w_out, norm2_g, w_router, b_router, w_gate, b_gate,
           w_up, b_up, w_down, b_down, final_g):
    B, S, D = x.shape
    T = B * S
    assert w_in.shape[0] == 1, "single-layer trunk: the final norm is fused into the layer's combine"
    assert D == SUBLANES * LANES, "a token row is moved as one (8, 128) f32 slab"
    assert S % max(TS_IN, TQ, TK, TC_LRU) == 0 and S % GRID_W == 0, "sequence tiles must divide S"
    x2 = x.reshape(T, D)
    for l in range(1):
        qt, k, vt, lru_x, lru_gate = _inproj(x2, norm1_g[l], w_in[l], q_norm_g[l], k_norm_g[l], S)
        score_bound = (HEAD_DIM * Q_SCALE * jnp.max(jnp.abs(q_norm_g[l]))
                       * jnp.max(jnp.abs(k_norm_g[l])))
        lru_ops = _lru_operands(conv_w[l], conv_b[l], lru_wa[l], lru_ba[l], lru_wi[l], lru_bi[l],
                                lru_lam[l])
        attn, lru = _mixers(qt, k.reshape(B, S, -1), vt, lru_x.reshape(B, S, -1),
                            lru_gate.reshape(B, S, -1), lru_ops, score_bound, B, S)
        x1, xn3, route, gates, cnt = _outproj_router(
            attn.reshape(T, -1), lru.reshape(T, -1), x2, attn_out_g[l], lru_out_g[l], w_out[l],
            norm2_g[l], w_router[l], b_router[l])

        n_rows = T * TOP_K + N_EXPERTS * ROW_BLOCK
        pstart, block_e, fill, next_e, n_active = _routing_plan(cnt, n_rows // ROW_BLOCK)
        dest_flat = _dest_rows(route, pstart)
        x_rows = _dispatch(xn3, fill, dest_flat, n_rows)
        y_rows = _experts(x_rows, block_e, n_active, next_e, w_gate[l], b_gate[l], w_up[l], b_up[l],
                          w_down[l], b_down[l])
        x2 = _combine(y_rows, dest_flat, x1, gates, final_g)
    return x2.reshape(B, S, D)
```

```python
import functools
import math

import jax
import jax.numpy as jnp
import numpy as np
from jax import lax
from jax.experimental import pallas as pl
from jax.experimental.pallas import tpu as pltpu

F32 = jnp.float32
BF16 = jnp.bfloat16

GRID_W = 64
HEAD_DIM = 64
N_Q_HEADS = 8
N_KV_HEADS = 2
GQA_GROUP = N_Q_HEADS // N_KV_HEADS
ATTN_W = N_Q_HEADS * HEAD_DIM
KV_W = N_KV_HEADS * HEAD_DIM
LRU_BLOCKS = 8
LRU_C = 8.0
CONV_W = 4
CONV_PAD_L = 2
ROPE_THETA = 10000.0
ROPE_HALF = HEAD_DIM // 2
ROPE_M = ROPE_HALF // 2
N_EXPERTS = 32
TOP_K = 4
SWIGLU_ALPHA = 1.702
SWIGLU_LIMIT = 7.0
NORM_EPS = 1e-5
QK_EPS = 1e-6
LOG2_E = 1.4426950408889634
Q_SCALE = HEAD_DIM ** -0.5 * LOG2_E
SAFE_SCORE_LOG2 = 96.0

LANES = 128
SUBLANES = 8
BF16_SUBLANES = 16
PV_ROWS = HEAD_DIM + BF16_SUBLANES
VMEM_LIMIT = 48 * 1024 * 1024
EXPERT_VMEM_LIMIT = 56 * 1024 * 1024

TS_IN = 512
TQ = 256
TK = 256
KV_UNROLL = 32
HEADS_PER_STEP = 2
Q_TILES_PER_STEP = 4
TC_LRU = 512
LRU_UNROLL = 4
TS_OUT = 512
TS_DEST = 2048
ROW_BLOCK = 512
TS_DISP = 1024
TS_COMB = 512
ISSUE_UNROLL = 8


def _cparams(sem):
    return pltpu.CompilerParams(dimension_semantics=sem, vmem_limit_bytes=VMEM_LIMIT)


def _inproj_kernel(x_ref, g1_ref, wt_ref, w_ref, qg_ref, kg_ref, cos_ref, sin_ref, cost_ref, sint_ref,
                   q_ref, k_ref, v_ref, lx_ref, lg_ref, *, lru_w):
    x = x_ref[...]
    ms = jnp.mean(x * x, axis=-1, keepdims=True)
    xn = (x * lax.rsqrt(ms + NORM_EPS) * g1_ref[...]).astype(BF16)
    ht = lax.dot_general(wt_ref[...], xn, (((1,), (1,)), ((), ())), preferred_element_type=F32)
    h = jnp.dot(xn, w_ref[...], preferred_element_type=F32)

    qw = N_Q_HEADS * LANES
    kw = N_KV_HEADS * LANES
    cost = cost_ref[...]
    sint = sint_ref[...]
    row = lax.broadcasted_iota(jnp.int32, cost.shape, 0)
    first_half_t = (row % ROPE_HALF) < ROPE_M
    qg = qg_ref[...]
    for c in range(N_Q_HEADS):
        sl = slice(c * LANES, (c + 1) * LANES)
        xc = ht[sl]
        hms = jnp.sum(xc * xc, axis=0, keepdims=True) * (1.0 / HEAD_DIM)
        xc = xc * lax.rsqrt(hms + QK_EPS) * qg
        partner = jnp.where(first_half_t, pltpu.roll(xc, LANES - ROPE_M, 0), pltpu.roll(xc, ROPE_M, 0))
        q_ref[0, sl, :] = ((xc * cost + partner * sint) * Q_SCALE).astype(BF16)
    for c in range(N_KV_HEADS):
        sl = slice(c * LANES, (c + 1) * LANES)
        v_ref[0, sl, :] = jnp.where(row >= HEAD_DIM, 1.0, ht[qw + c * LANES: qw + (c + 1) * LANES]).astype(BF16)

    cos = cos_ref[...]
    sin = sin_ref[...]
    lane = lax.broadcasted_iota(jnp.int32, cos.shape, 1)
    first_half = (lane % ROPE_HALF) < ROPE_M
    for c in range(N_KV_HEADS):
        sl = slice(c * LANES, (c + 1) * LANES)
        xc = h[:, sl]
        hms = jnp.sum(xc * xc, axis=-1, keepdims=True) * (1.0 / HEAD_DIM)
        xc = xc * lax.rsqrt(hms + QK_EPS) * kg_ref[...]
        partner = jnp.where(first_half, pltpu.roll(xc, LANES - ROPE_M, 1), pltpu.roll(xc, ROPE_M, 1))
        k_ref[:, sl] = (xc * cos + partner * sin).astype(BF16)
    lx_ref[...] = h[:, kw: kw + lru_w]
    lg_ref[...] = h[:, kw + lru_w: kw + 2 * lru_w]


def _pad_heads(w, n_heads):
    lead = w.shape[:-1]
    w = w.reshape(lead + (n_heads, HEAD_DIM))
    w = jnp.pad(w, [(0, 0)] * len(lead) + [(0, 0), (0, LANES - HEAD_DIM)])
    return w.reshape(lead + (n_heads * LANES,))


def _rope_tables(S):
    t = np.arange(S)
    rows = (t // GRID_W).astype(np.float32)
    cols = (t % GRID_W).astype(np.float32)
    inv_freq = (ROPE_THETA ** (-np.arange(ROPE_M, dtype=np.float32) / ROPE_M)).astype(np.float32)
    ar = rows[:, None] * inv_freq[None, :]
    ac = cols[:, None] * inv_freq[None, :]
    cos = np.concatenate([np.cos(ar), np.cos(ar), np.cos(ac), np.cos(ac)], axis=-1)
    sin = np.concatenate([-np.sin(ar), np.sin(ar), -np.sin(ac), np.sin(ac)], axis=-1)
    pad = [(0, 0), (0, LANES - HEAD_DIM)]
    cos = np.pad(cos, pad).astype(np.float32)
    sin = np.pad(sin, pad).astype(np.float32)
    return cos, sin, np.ascontiguousarray(cos.T), np.ascontiguousarray(sin.T)


def _inproj(x2, norm1_g, w_in, q_norm_g, k_norm_g, S):
    T, D = x2.shape
    lru_w = (w_in.shape[1] - ATTN_W - 2 * KV_W) // 2
    o0, o1, o2 = ATTN_W, ATTN_W + KV_W, ATTN_W + 2 * KV_W
    w_t = jnp.concatenate([_pad_heads(w_in[:, :o0], N_Q_HEADS),
                           _pad_heads(w_in[:, o1:o2], N_KV_HEADS)], axis=1).T.astype(BF16)
    w_rest = jnp.concatenate([_pad_heads(w_in[:, o0:o1], N_KV_HEADS), w_in[:, o2:]],
                             axis=1).astype(BF16)
    qg = _pad_heads(q_norm_g.reshape(1, HEAD_DIM), 1).reshape(LANES, 1)
    kg = _pad_heads(k_norm_g.reshape(1, HEAD_DIM), 1)
    cos, sin, cos_t, sin_t = _rope_tables(S)
    ts = TS_IN
    n_s = S // ts
    qw, kw = N_Q_HEADS * LANES, N_KV_HEADS * LANES
    const = lambda i: (0, 0)
    tok = lambda i: (i, 0)
    pos = lambda i: (i % n_s, 0)
    pos_t = lambda i: (0, i % n_s)
    tposed = lambda i: (i // n_s, 0, i % n_s)
    return pl.pallas_call(
        functools.partial(_inproj_kernel, lru_w=lru_w),
        grid=(T // ts,),
        in_specs=[
            pl.BlockSpec((ts, D), tok),
            pl.BlockSpec((1, D), const),
            pl.BlockSpec(w_t.shape, const),
            pl.BlockSpec(w_rest.shape, const),
            pl.BlockSpec((LANES, 1), const),
            pl.BlockSpec((1, LANES), const),
            pl.BlockSpec((ts, LANES), pos),
            pl.BlockSpec((ts, LANES), pos),
            pl.BlockSpec((LANES, ts), pos_t),
            pl.BlockSpec((LANES, ts), pos_t),
        ],
        out_specs=[
            pl.BlockSpec((1, qw, ts), tposed),
            pl.BlockSpec((ts, kw), tok),
            pl.BlockSpec((1, kw, ts), tposed),
            pl.BlockSpec((ts, lru_w), tok),
            pl.BlockSpec((ts, lru_w), tok),
        ],
        out_shape=[
            jax.ShapeDtypeStruct((T // S, qw, S), BF16),
            jax.ShapeDtypeStruct((T, kw), BF16),
            jax.ShapeDtypeStruct((T // S, kw, S), BF16),
            jax.ShapeDtypeStruct((T, lru_w), F32),
            jax.ShapeDtypeStruct((T, lru_w), F32),
        ],
        compiler_params=_cparams(("parallel",)),
        name="inproj",
    )(x2, norm1_g.reshape(1, D), w_t, w_rest, qg, kg, cos, sin, cos_t, sin_t)


def _attn_kernel(qt_ref, k_ref, vt_ref, o_ref, acc_ref, s_ref, p_ref, *, tq, tk, n_kv, kv_unroll):
    hp = HEADS_PER_STEP
    spt = GQA_GROUP // hp
    acc_ref[...] = jnp.zeros(acc_ref.shape, F32)

    def scores(j, sp):
        kt = k_ref[0, pl.ds(pl.multiple_of(j * tk, tk), tk), :]
        out = []
        for u in range(hp):
            g = sp * hp + u
            s = jnp.dot(kt, qt_ref[0, g * LANES:(g + 1) * LANES, :], preferred_element_type=F32)
            out.append((s, jnp.max(s, axis=0, keepdims=True)))
        return out

    def softmax_stage(sc, ms, sp):
        out = []
        for u, (s, s_max) in enumerate(sc):
            h = sp * hp + u
            m_new = jnp.maximum(ms[h], s_max)
            out.append((jnp.exp2(ms[h] - m_new), jnp.exp2(s - m_new).astype(BF16)))
            ms[h] = m_new
        return out

    def pv_stage(j, sp, ap):
        vt = vt_ref[0, 0:PV_ROWS, pl.ds(pl.multiple_of(j * tk, tk), tk)]
        for u, (alpha, p) in enumerate(ap):
            g = sp * hp + u
            acc_ref[g] = alpha * acc_ref[g] + jnp.dot(vt, p, preferred_element_type=F32)

    ms = [jnp.full((1, tq), -jnp.inf, F32)] * GQA_GROUP
    ap = softmax_stage(scores(0, 0), ms, 0)
    sc = scores(min(1 // spt, n_kv - 1), 1 % spt)
    for u in range(hp):
        s_ref[u] = sc[u][0]
        p_ref[u] = ap[u][1]

    def body(it, carry):
        ms = list(carry[:GQA_GROUP])
        ap = [(carry[GQA_GROUP + u], p_ref[u]) for u in range(hp)]
        sc = [(s_ref[u], carry[GQA_GROUP + hp + u]) for u in range(hp)]
        for n in range(kv_unroll * spt):
            j = it * kv_unroll + n // spt
            j_next = jnp.minimum(it * kv_unroll + (n + 2) // spt, n_kv - 1)
            sc_next = scores(j_next, (n + 2) % spt)
            ap_next = softmax_stage(sc, ms, (n + 1) % spt)
            pv_stage(j, n % spt, ap)
            sc, ap = sc_next, ap_next
        for u in range(hp):
            s_ref[u] = sc[u][0]
            p_ref[u] = ap[u][1]
        return tuple(ms) + tuple(a for a, _ in ap) + tuple(m for _, m in sc)

    lax.fori_loop(0, n_kv // kv_unroll, body,
                  tuple(ms) + tuple(a for a, _ in ap) + tuple(m for _, m in sc))
    _attn_finalize(acc_ref, o_ref, tq)


def _attn_finalize(acc_ref, o_ref, tq, row0=0):
    pad = jnp.zeros((LANES - HEAD_DIM, tq), F32)
    for g in range(GQA_GROUP):
        acc = acc_ref[g]
        o = acc[0:HEAD_DIM] / acc[HEAD_DIM:HEAD_DIM + 1, :]
        o_ref[0, row0:row0 + tq, g * LANES:(g + 1) * LANES] = (
            jnp.concatenate([o, pad], axis=0).T.astype(BF16))


def _attn_bounded_kernel(qt_ref, k_ref, vt_ref, o_ref, acc_ref, s_ref, p_ref, *, tq, tk, n_kv, kv_unroll,
                         q_tiles):
    for t in range(q_tiles):
        _attn_bounded_tile(qt_ref, k_ref, vt_ref, o_ref, acc_ref.at[t], s_ref.at[t], p_ref.at[t],
                           tq=tq, tk=tk, n_kv=n_kv, kv_unroll=kv_unroll, q0=t * tq)


def _attn_bounded_tile(qt_ref, k_ref, vt_ref, o_ref, acc_ref, s_ref, p_ref, *, tq, tk, n_kv, kv_unroll,
                       q0):
    hp = HEADS_PER_STEP
    spt = GQA_GROUP // hp
    acc_ref[...] = jnp.zeros(acc_ref.shape, F32)

    def scores(j, sp):
        kt = k_ref[0, pl.ds(pl.multiple_of(j * tk, tk), tk), :]
        return [jnp.dot(kt, qt_ref[0, (sp * hp + u) * LANES:(sp * hp + u + 1) * LANES, q0:q0 + tq],
                        preferred_element_type=F32) for u in range(hp)]

    def probs(sc):
        return [jnp.exp2(s).astype(BF16) for s in sc]

    def pv_stage(j, sp, ps):
        vt = vt_ref[0, 0:PV_ROWS, pl.ds(pl.multiple_of(j * tk, tk), tk)]
        for u, p in enumerate(ps):
            acc_ref[sp * hp + u] += jnp.dot(vt, p, preferred_element_type=F32)

    ps = probs(scores(0, 0))
    sc = scores(min(1 // spt, n_kv - 1), 1 % spt)
    for u in range(hp):
        s_ref[u] = sc[u]
        p_ref[u] = ps[u]

    def body(it, carry):
        ps = [p_ref[u] for u in range(hp)]
        sc = [s_ref[u] for u in range(hp)]
        for n in range(kv_unroll * spt):
            j = it * kv_unroll + n // spt
            j_next = jnp.minimum(it * kv_unroll + (n + 2) // spt, n_kv - 1)
            sc_next = scores(j_next, (n + 2) % spt)
            ps_next = probs(sc)
            pv_stage(j, n % spt, ps)
            sc, ps = sc_next, ps_next
        for u in range(hp):
            s_ref[u] = sc[u]
            p_ref[u] = ps[u]
        return carry

    lax.fori_loop(0, n_kv // kv_unroll, body, 0)
    _attn_finalize(acc_ref, o_ref, tq, q0)


def _attention(qt, k, vt, *, bounded, B, S):
    tq = min(TQ, S)
    tk = min(TK, S)
    gw = GQA_GROUP * LANES
    tiling = dict(tq=tq, tk=tk, n_kv=S // tk, kv_unroll=math.gcd(S // tk, KV_UNROLL))
    if bounded:
        q_tiles = math.gcd(S // tq, Q_TILES_PER_STEP)
        body = functools.partial(_attn_bounded_kernel, q_tiles=q_tiles, **tiling)
        scratch = [pltpu.VMEM((q_tiles, GQA_GROUP, PV_ROWS, tq), F32),
                   pltpu.VMEM((q_tiles, HEADS_PER_STEP, tk, tq), F32),
                   pltpu.VMEM((q_tiles, HEADS_PER_STEP, tk, tq), BF16)]
    else:
        q_tiles = 1
        body = functools.partial(_attn_kernel, **tiling)
        scratch = [pltpu.VMEM((GQA_GROUP, PV_ROWS, tq), F32),
                   pltpu.VMEM((HEADS_PER_STEP, tk, tq), F32),
                   pltpu.VMEM((HEADS_PER_STEP, tk, tq), BF16)]
    tqs = q_tiles * tq
    return pl.pallas_call(
        body,
        grid=(B, N_KV_HEADS, S // tqs),
        in_specs=[
            pl.BlockSpec((1, gw, tqs), lambda b, h, i: (b, h, i)),
            pl.BlockSpec((1, S, LANES), lambda b, h, i: (b, 0, h)),
            pl.BlockSpec((1, LANES, S), lambda b, h, i: (b, h, 0)),
        ],
        out_specs=pl.BlockSpec((1, tqs, gw), lambda b, h, i: (b, i, h)),
        out_shape=jax.ShapeDtypeStruct((B, S, N_Q_HEADS * LANES), BF16),
        scratch_shapes=scratch,
        compiler_params=_cparams(("parallel", "parallel", "parallel")),
        name="attention_bounded" if bounded else "attention",
    )(qt, k, vt)


def _scan_chunk(a, b, h_in, reverse):
    n = a.shape[0]
    n_groups = n // SUBLANES
    a = a.reshape(n_groups, SUBLANES, LANES)
    b = b.reshape(n_groups, SUBLANES, LANES)
    sub = lax.broadcasted_iota(jnp.int32, a.shape, 1)
    d = 1
    while d < SUBLANES:
        if reverse:
            keep = sub < SUBLANES - d
            shift = SUBLANES - d
        else:
            keep = sub >= d
            shift = d
        a_sh = jnp.where(keep, pltpu.roll(a, shift, 1), 1.0)
        b_sh = jnp.where(keep, pltpu.roll(b, shift, 1), 0.0)
        b = a * b_sh + b
        a = a * a_sh
        d *= 2
    a = a.reshape(n, LANES)
    b = b.reshape(n, LANES)
    order = range(n_groups - 1, -1, -1) if reverse else range(n_groups)
    edge = h_in
    out = [None] * n_groups
    for v in order:
        rows = slice(v * SUBLANES, (v + 1) * SUBLANES)
        hv = b[rows] + a[rows] * jnp.broadcast_to(edge, (SUBLANES, LANES))
        out[v] = hv
        edge = hv[0:1] if reverse else hv[SUBLANES - 1:SUBLANES]
    return jnp.concatenate(out, axis=0), edge


def _lru_pad_input(u_ref, up_ref, S):
    zeros = jnp.zeros((SUBLANES, LANES), F32)
    up_ref[0:SUBLANES, :] = zeros
    up_ref[S + SUBLANES:S + 2 * SUBLANES, :] = zeros
    up_ref[SUBLANES:S + SUBLANES, :] = u_ref[0]


def _lru_gates(up_ref, cw_ref, cb_ref, w_ref, bias_ref, lam_ref, t0, tc, d):
    cw = cw_ref[...]
    xc = cb_ref[...]
    for j in range(CONV_W):
        xc = xc + up_ref[pl.ds(t0 + SUBLANES + j - CONV_PAD_L, tc), :] * cw[j:j + 1, :]
    gw = 2 * LANES
    g = jnp.dot(xc.astype(BF16), w_ref[0, :, d * gw:(d + 1) * gw],
                preferred_element_type=F32) + bias_ref[0, :, d * gw:(d + 1) * gw]
    r = jax.nn.sigmoid(g[:, :LANES])
    i = jax.nn.sigmoid(g[:, LANES:])
    a = jnp.exp(-LRU_C * r * jax.nn.softplus(-lam_ref[d:d + 1, :]))
    y = 1.0 - a * a
    b = jnp.where(y > 0.0, y * lax.rsqrt(y), 0.0) * i * xc
    return a, b


def _lru_kernel(u_ref, gate_ref, cw_ref, cb_ref, w_ref, bias_ref, lam_ref, o_ref,
                up_ref, hf_ref, *, S, tc):
    _lru_pad_input(u_ref, up_ref, S)
    n_chunks = S // tc
    params = (up_ref, cw_ref, cb_ref, w_ref, bias_ref, lam_ref)

    def fwd(c, h):
        t0 = pl.multiple_of(c * tc, tc)
        hc, h_last = _scan_chunk(*_lru_gates(*params, t0, tc, 0), h, False)
        hf_ref[pl.ds(t0, tc), :] = hc
        return h_last

    unroll = math.gcd(n_chunks, LRU_UNROLL)

    def grouped(step):
        def body(i, h):
            for j in range(unroll):
                h = step(i * unroll + j, h)
            return h
        return body

    lax.fori_loop(0, n_chunks // unroll, grouped(fwd), jnp.zeros((1, LANES), F32))

    def bwd(ci, h):
        t0 = pl.multiple_of((n_chunks - 1 - ci) * tc, tc)
        hc, h_last = _scan_chunk(*_lru_gates(*params, t0, tc, 1), h, True)
        gate = gate_ref[0, pl.ds(t0, tc), :]
        o_ref[0, pl.ds(t0, tc), :] = (hf_ref[pl.ds(t0, tc), :] + hc) * jax.nn.gelu(gate)
        return h_last

    lax.fori_loop(0, n_chunks // unroll, grouped(bwd), jnp.zeros((1, LANES), F32))


def _block_diag_pairs(w):
    nb, bw, _ = w.shape
    w = w.reshape(nb // 2, 2, bw, bw)
    z = jnp.zeros_like(w[:, 0])
    top = jnp.concatenate([w[:, 0], z], axis=-1)
    bot = jnp.concatenate([z, w[:, 1]], axis=-1)
    return jnp.concatenate([top, bot], axis=-2)


def _lru_operands(conv_w, conv_b, wa, ba, wi, bi, lam):
    C = conv_b.shape[0]
    nc = C // LANES
    w = jnp.concatenate([_block_diag_pairs(wa[0]), _block_diag_pairs(wi[0]),
                         _block_diag_pairs(wa[1]), _block_diag_pairs(wi[1])], axis=-1).astype(BF16)
    bias = jnp.stack([ba[0].reshape(nc, LANES), bi[0].reshape(nc, LANES),
                      ba[1].reshape(nc, LANES), bi[1].reshape(nc, LANES)], axis=1)
    return conv_w, conv_b.reshape(1, C), w, bias.reshape(nc, 1, 4 * LANES), lam


def _lru_specs(S, unit):
    seq = lambda *g: (unit(*g)[0], 0, unit(*g)[1])
    chan = lambda *g: (0, unit(*g)[1])
    blk = lambda *g: (unit(*g)[1], 0, 0)
    in_specs = [
        pl.BlockSpec((1, S, LANES), seq),
        pl.BlockSpec((1, S, LANES), seq),
        pl.BlockSpec((CONV_W, LANES), chan),
        pl.BlockSpec((1, LANES), chan),
        pl.BlockSpec((1, LANES, 4 * LANES), blk),
        pl.BlockSpec((1, 1, 4 * LANES), blk),
        pl.BlockSpec((2, LANES), chan),
    ]
    return in_specs, pl.BlockSpec((1, S, LANES), seq)


def _lru(lru_x, lru_gate, lru_ops, B, S):
    C = lru_x.shape[-1]
    tc = min(TC_LRU, S)
    in_specs, out_spec = _lru_specs(S, lambda b, c: (b, c))
    return pl.pallas_call(
        functools.partial(_lru_kernel, S=S, tc=tc),
        grid=(B, C // LANES),
        in_specs=in_specs,
        out_specs=out_spec,
        out_shape=jax.ShapeDtypeStruct((B, S, C), F32),
        scratch_shapes=[
            pltpu.VMEM((S + 2 * SUBLANES, LANES), F32),
            pltpu.VMEM((S, LANES), F32),
        ],
        compiler_params=_cparams(("parallel", "parallel")),
        name="rglru",
    )(lru_x, lru_gate, *lru_ops)


def _mixers(qt, k, vt, lru_x, lru_gate, lru_ops, score_bound, B, S):
    attn = lax.cond(score_bound <= SAFE_SCORE_LOG2,
                    functools.partial(_attention, bounded=True, B=B, S=S),
                    functools.partial(_attention, bounded=False, B=B, S=S), qt, k, vt)
    return attn, _lru(lru_x, lru_gate, lru_ops, B, S)


def _rows_to_slabs(ref, x):
    n = x.shape[0]
    for s in range(SUBLANES):
        ref[pl.ds(s, n, stride=SUBLANES), :] = x[:, s * LANES:(s + 1) * LANES]


def _slabs_to_rows(ref, n):
    return jnp.concatenate([ref[pl.ds(s, n, stride=SUBLANES), :] for s in range(SUBLANES)], axis=1)


def _slab(ref, r):
    return ref.at[pl.ds(pl.multiple_of(r * SUBLANES, SUBLANES), SUBLANES)]


def _outproj_kernel(a_ref, l_ref, x_ref, ag_ref, lg_ref, wa_ref, wl_ref, g2_ref,
                    wr_ref, br_ref, tri_ref,
                    x1_ref, xn3_ref, route_ref, gates_ref, cnt_ref, carry_ref, *, attn_w, lru_w):
    step = pl.program_id(0)

    @pl.when(step == 0)
    def _():
        carry_ref[...] = jnp.zeros_like(carry_ref)

    a = a_ref[...].astype(F32)
    ams = jnp.sum(a * a, axis=-1, keepdims=True) * (1.0 / attn_w)
    an = a * lax.rsqrt(ams + NORM_EPS) * ag_ref[...]
    l = l_ref[...]
    lms = jnp.sum(l * l, axis=-1, keepdims=True) * (1.0 / lru_w)
    ln = l * lax.rsqrt(lms + NORM_EPS) * lg_ref[...]
    mix = (jnp.dot(an.astype(BF16), wa_ref[...], preferred_element_type=F32)
           + jnp.dot(ln.astype(BF16), wl_ref[...], preferred_element_type=F32))
    x1 = x_ref[...] + mix
    x1_ref[...] = x1
    ms = jnp.mean(x1 * x1, axis=-1, keepdims=True)
    xn = x1 * lax.rsqrt(ms + NORM_EPS) * g2_ref[...]
    _rows_to_slabs(xn3_ref, xn)

    logits = jnp.dot(xn.astype(BF16), wr_ref[...], preferred_element_type=F32) + br_ref[...]
    lane = lax.broadcasted_iota(jnp.int32, logits.shape, 1)
    neg = -jnp.inf
    work = jnp.where(lane < N_EXPERTS, logits, neg)
    sel = jnp.zeros(logits.shape, F32)
    idxs, vals = [], []
    for _ in range(TOP_K):
        m = jnp.max(work, axis=1, keepdims=True)
        idx = jnp.min(jnp.where(work == m, lane, LANES), axis=1, keepdims=True)
        hit = lane == idx
        work = jnp.where(hit, neg, work)
        sel = sel + hit.astype(F32)
        idxs.append(idx)
        vals.append(m)
    es = [jnp.exp(v - vals[0]) for v in vals]
    den = es[0] + es[1] + es[2] + es[3]

    prefix = jnp.dot(tri_ref[...], sel.astype(BF16), preferred_element_type=F32) + carry_ref[...]
    carry_ref[...] = carry_ref[...] + jnp.sum(sel, axis=0, keepdims=True)
    cnt_ref[...] = carry_ref[...]

    route = jnp.zeros(logits.shape, jnp.int32)
    gates = jnp.zeros(logits.shape, F32)
    for k in range(TOP_K):
        rank = jnp.sum(jnp.where(lane == idxs[k], prefix, 0.0), axis=1, keepdims=True).astype(jnp.int32)
        route = jnp.where(lane == k, idxs[k], route)
        route = jnp.where(lane == TOP_K + k, rank, route)
        gates = jnp.where(lane == k, es[k] / den, gates)
    route_ref[...] = route
    gates_ref[...] = gates


def _outproj_router(attn, lru, x2, attn_out_g, lru_out_g, w_out, norm2_g, w_router, b_router):
    T, D = x2.shape
    lru_w = lru.shape[-1]
    ts = min(TS_OUT, T)
    wa = w_out[:ATTN_W].reshape(N_Q_HEADS, HEAD_DIM, D)
    wa = jnp.pad(wa, ((0, 0), (0, LANES - HEAD_DIM), (0, 0))).reshape(N_Q_HEADS * LANES, D).astype(BF16)
    wl = w_out[ATTN_W:].astype(BF16)
    ag = _pad_heads(attn_out_g.reshape(1, ATTN_W), N_Q_HEADS)
    wr = jnp.pad(w_router, ((0, 0), (0, LANES - N_EXPERTS))).astype(BF16)
    br =jnp.pad(b_router.reshape(1, N_EXPERTS), ((0, 0), (0, LANES - N_EXPERTS)))
    tri = (jnp.arange(ts)[:, None] > jnp.arange(ts)[None, :]).astype(BF16)
    const = lambda i: (0, 0)
    tok = lambda i: (i, 0)
    aw = N_Q_HEADS * LANES
    return pl.pallas_call(
        functools.partial(_outproj_kernel, attn_w=ATTN_W, lru_w=lru_w),
        grid=(T // ts,),
        in_specs=[
            pl.BlockSpec((ts, aw), tok),
            pl.BlockSpec((ts, lru_w), tok),
            pl.BlockSpec((ts, D), tok),
            pl.BlockSpec((1, aw), const),
            pl.BlockSpec((1, lru_w), const),
            pl.BlockSpec((aw, D), const),
            pl.BlockSpec((lru_w, D), const),
            pl.BlockSpec((1, D), const),
            pl.BlockSpec((D, LANES), const),
            pl.BlockSpec((1, LANES), const),
            pl.BlockSpec((ts, ts), const),
        ],
        out_specs=[
            pl.BlockSpec((ts, D), tok),
            pl.BlockSpec((ts * SUBLANES, LANES), tok),
            pl.BlockSpec((ts, LANES), tok),
            pl.BlockSpec((ts, LANES), tok),
            pl.BlockSpec((1, LANES), const),
        ],
        out_shape=[
            jax.ShapeDtypeStruct((T, D), F32),
            jax.ShapeDtypeStruct((T * SUBLANES, LANES), F32),
            jax.ShapeDtypeStruct((T, LANES), jnp.int32),
            jax.ShapeDtypeStruct((T, LANES), F32),
            jax.ShapeDtypeStruct((1, LANES), F32),
        ],
        scratch_shapes=[pltpu.VMEM((1, LANES), F32)],
        compiler_params=_cparams(("arbitrary",)),
        name="outproj_router",
    )(attn, lru, x2, ag, lru_out_g.reshape(1, lru_w), wa, wl, norm2_g.reshape(1, D),
      wr, br, tri)


def _plan_kernel(cnt_ref, pstart_ref, plan_ref):
    cnt = cnt_ref[...]
    lane = lax.broadcasted_iota(jnp.int32, cnt.shape, 1)
    padded = jnp.floor((cnt + (ROW_BLOCK - 1)) * (1.0 / ROW_BLOCK)) * ROW_BLOCK
    pend = padded
    d = 1
    while d < N_EXPERTS:
        pend = pend + jnp.where(lane >= d, pltpu.roll(pend, d, 1), 0.0)
        d *= 2
    pstart_ref[...] = pend - padded
    total = jnp.max(pend, axis=1, keepdims=True)

    shape = plan_ref.shape
    lanes = lax.broadcasted_iota(jnp.int32, shape, 1)
    is_expert = lanes < N_EXPERTS
    start = lax.broadcasted_iota(jnp.int32, shape, 0).astype(F32) * ROW_BLOCK

    def groups_ending_by(row):
        return jnp.sum(jnp.where(jnp.logical_and(pend <= row, is_expert), 1.0, 0.0), axis=1, keepdims=True)

    block_e = jnp.minimum(groups_ending_by(start), N_EXPERTS - 1.0)
    tail = jnp.max(jnp.where(jnp.logical_and(jnp.logical_and(pend == start + ROW_BLOCK, padded > 0.0),
                                             is_expert), 1.0, 0.0), axis=1, keepdims=True)
    fill = jnp.maximum(tail, jnp.where(start[:, 0:1] >= total, 1.0, 0.0))
    group_end = jnp.sum(jnp.where(lanes.astype(F32) == block_e, pend, 0.0), axis=1, keepdims=True)
    next_e = jnp.where(group_end < total,
                       jnp.minimum(groups_ending_by(group_end), N_EXPERTS - 1.0), -1.0)
    plan = jnp.where(lanes == 0, block_e,
                     jnp.where(lanes == 1, fill,
                               jnp.where(lanes == 2, next_e, total * (1.0 / ROW_BLOCK))))
    plan_ref[...] = plan.astype(jnp.int32)


def _routing_plan(cnt, n_blocks):
    assert ROW_BLOCK & (ROW_BLOCK - 1) == 0, "exact f32 division by the row block size"
    rows = -(-n_blocks // SUBLANES) * SUBLANES
    pstart, plan = pl.pallas_call(
        _plan_kernel,
        out_shape=[jax.ShapeDtypeStruct((1, LANES), F32),
                   jax.ShapeDtypeStruct((rows, LANES), jnp.int32)],
        name="routing_plan",
    )(cnt)
    return pstart, plan[:n_blocks, 0], plan[:n_blocks, 1], plan[:n_blocks, 2], plan[0:1, 3]


def _dest_kernel(route_ref, pstart_ref, dest_ref):
    route = route_ref[...]
    lane = lax.broadcasted_iota(jnp.int32, route.shape, 1)
    pstart = pstart_ref[...]
    dest = jnp.zeros(route.shape, jnp.int32)
    for k in range(TOP_K):
        start = jnp.sum(jnp.where(lane == route[:, k:k + 1], pstart, 0.0), axis=1, keepdims=True)
        dest = jnp.where(lane == k, start.astype(jnp.int32) + route[:, TOP_K + k:TOP_K + k + 1], dest)
    dest_ref[...] = dest


def _dest_rows(route, pstart):
    T = route.shape[0]
    ts = math.gcd(TS_DEST, T)
    dest = pl.pallas_call(
        _dest_kernel,
        grid=(T // ts,),
        in_specs=[pl.BlockSpec((ts, LANES), lambda i: (i, 0)),
                  pl.BlockSpec((1, LANES), lambda i: (0, 0))],
        out_specs=pl.BlockSpec((ts, LANES), lambda i: (i, 0)),
        out_shape=jax.ShapeDtypeStruct((T, LANES), jnp.int32),
        compiler_params=_cparams(("parallel",)),
        name="dest_rows",
    )(route, pstart)
    return dest[:, :TOP_K].reshape(T * TOP_K)


def _dispatch_kernel(fill_ref, dest_ref, x_ref, out_hbm, zero_ref, sem, zero_sem, *, ts, n_blocks):
    block_slabs = ROW_BLOCK * SUBLANES

    def fill_copy(b):
        off = pl.multiple_of(b * block_slabs, block_slabs)
        return pltpu.make_async_copy(zero_ref, out_hbm.at[pl.ds(off, block_slabs)], zero_sem)

    @pl.when(pl.program_id(0) == 0)
    def _():
        zero_ref[...] = jnp.zeros(zero_ref.shape, F32)

        def start(b, carry):
            @pl.when(fill_ref[b] != 0)
            def _():
                fill_copy(b).start()
            return carry

        def wait(b, carry):
            @pl.when(fill_ref[b] != 0)
            def _():
                fill_copy(b).wait()
            return carry

        lax.fori_loop(0, n_blocks, start, 0)
        lax.fori_loop(0, n_blocks, wait, 0)

    def issue(i, carry):
        for j in range(ISSUE_UNROLL):
            r = i * ISSUE_UNROLL + j
            for k in range(TOP_K):
                d = dest_ref[r * TOP_K + k]
                pltpu.make_async_copy(_slab(x_ref, r), _slab(out_hbm, d), sem).start(priority=k % 2)
        return carry

    lax.fori_loop(0, ts // ISSUE_UNROLL, issue, 0)
    for k in range(TOP_K):
        pltpu.make_async_copy(x_ref, out_hbm.at[pl.ds(0, ts * SUBLANES)], sem).wait()


def _dispatch(xn_slabs, fill, dest_flat, n_rows):
    T = xn_slabs.shape[0] // SUBLANES
    ts = min(TS_DISP, T)
    grid_spec = pltpu.PrefetchScalarGridSpec(
        num_scalar_prefetch=1,
        grid=(T // ts,),
        in_specs=[
            pl.BlockSpec((ts * TOP_K,), lambda i, fl: (i,), memory_space=pltpu.SMEM),
            pl.BlockSpec((ts * SUBLANES, LANES), lambda i, fl: (i, 0)),
        ],
        out_specs=pl.BlockSpec(memory_space=pl.ANY),
        scratch_shapes=[pltpu.VMEM((ROW_BLOCK * SUBLANES, LANES), F32),
                        pltpu.SemaphoreType.DMA, pltpu.SemaphoreType.DMA],
    )
    return pl.pallas_call(
        functools.partial(_dispatch_kernel, ts=ts, n_blocks=n_rows // ROW_BLOCK),
        grid_spec=grid_spec,
        out_shape=jax.ShapeDtypeStruct((n_rows * SUBLANES, LANES), xn_slabs.dtype),
        compiler_params=_cparams(("arbitrary",)),
        name="dispatch",
    )(fill, dest_flat, xn_slabs)


def _expert_loop_kernel(be_ref, na_ref, nxt_ref, x_hbm, wg_hbm, bg_ref, wu_hbm, bu_ref, wd_hbm, bd_ref,
                        y_hbm, xbuf, ybuf, stage_ref, wb_ref, x_sems, y_sems, w_sems):
    w_hbm = (wg_hbm, wu_hbm, wd_hbm)
    block_slabs = ROW_BLOCK * SUBLANES
    n_active = na_ref[0]

    def rows(b):
        return pl.ds(pl.multiple_of(b * block_slabs, block_slabs), block_slabs)

    def x_copy(b, s):
        return pltpu.make_async_copy(x_hbm.at[rows(b)], xbuf.at[s], x_sems.at[s])

    def y_copy(b, s):
        return pltpu.make_async_copy(ybuf.at[s], y_hbm.at[rows(b)], y_sems.at[s])

    def fetch(expert, s, m):
        return pltpu.make_async_copy(w_hbm[m].at[expert], stage_ref.at[s, m], w_sems.at[s, m])

    x_copy(0, 0).start()
    for m in range(3):
        fetch(be_ref[0], 0, m).start()

    def body(b, wslot):
        s = b % 2
        e = be_ref[b]
        x_copy(b, s).wait()

        @pl.when(b + 1 < n_active)
        def _():
            x_copy(b + 1, 1 - s).start()

        first = jnp.logical_or(b == 0, e != be_ref[jnp.maximum(b - 1, 0)])

        @pl.when(first)
        def _():
            for m in range(3):
                fetch(e, wslot, m).wait()
                wb_ref[m] = stage_ref[wslot, m].astype(BF16)

            @pl.when(nxt_ref[b] >= 0)
            def _():
                for m in range(3):
                    fetch(nxt_ref[b], 1 - wslot, m).start()

        @pl.when(b >= 2)
        def _():
            y_copy(b - 2, s).wait()

        x = _slabs_to_rows(xbuf.at[s], ROW_BLOCK).astype(BF16)
        g = jnp.dot(x, wb_ref[0], preferred_element_type=F32) + bg_ref[e]
        u = jnp.dot(x, wb_ref[1], preferred_element_type=F32) + bu_ref[e]
        g = jnp.minimum(g, SWIGLU_LIMIT)
        u = jnp.clip(u, -SWIGLU_LIMIT, SWIGLU_LIMIT)
        glu = g * jax.nn.sigmoid(SWIGLU_ALPHA * g)
        y = jnp.dot(((u + 1.0) * glu).astype(BF16), wb_ref[2], preferred_element_type=F32) + bd_ref[e]
        _rows_to_slabs(ybuf.at[s], y)
        y_copy(b, s).start()
        return jnp.where(first, 1 - wslot, wslot)

    lax.fori_loop(0, n_active, body, 0)

    @pl.when(n_active >= 2)
    def _():
        y_copy(n_active - 2, n_active % 2).wait()

    y_copy(n_active - 1, (n_active - 1) % 2).wait()


def _experts(x_rows, block_e, n_active, next_e, w_gate, b_gate, w_up, b_up, w_down, b_down):
    E, D, FF = w_gate.shape
    assert D == FF, "the three expert matrices share one staging shape"
    block_slabs = ROW_BLOCK * SUBLANES
    whole = lambda i, be, na, nx: (0, 0, 0)

    grid_spec = pltpu.PrefetchScalarGridSpec(
        num_scalar_prefetch=3,
        grid=(1,),
        in_specs=[
            pl.BlockSpec(memory_space=pl.ANY),
            pl.BlockSpec(memory_space=pl.ANY),
            pl.BlockSpec((E, 1, FF), whole),
            pl.BlockSpec(memory_space=pl.ANY),
            pl.BlockSpec((E, 1, FF), whole),
            pl.BlockSpec(memory_space=pl.ANY),
            pl.BlockSpec((E, 1, D), whole),
        ],
        out_specs=pl.BlockSpec(memory_space=pl.ANY),
        scratch_shapes=[
            pltpu.VMEM((2, block_slabs, LANES), F32),
            pltpu.VMEM((2, block_slabs, LANES), F32),
            pltpu.VMEM((2, 3, D, FF), F32),
            pltpu.VMEM((3, D, FF), BF16),
            pltpu.SemaphoreType.DMA((2,)),
            pltpu.SemaphoreType.DMA((2,)),
            pltpu.SemaphoreType.DMA((2, 3)),
        ],
    )
    return pl.pallas_call(
        _expert_loop_kernel,
        grid_spec=grid_spec,
        out_shape=jax.ShapeDtypeStruct(x_rows.shape, F32),
        input_output_aliases={3: 0},
        compiler_params=pltpu.CompilerParams(dimension_semantics=("arbitrary",),
                                             vmem_limit_bytes=EXPERT_VMEM_LIMIT),
        name="experts",
    )(block_e, n_active, next_e, x_rows, w_gate, b_gate.reshape(E, 1, FF), w_up,
      b_up.reshape(E, 1, FF), w_down, b_down.reshape(E, 1, D))


def _combine_kernel(dest_ref, dest_next_ref, y_hbm, x1_ref, gates_ref, fg_ref, o_ref, bufs, sems,
                    *, ts, n_steps):
    i = pl.program_id(0)
    slot = i % 2

    def gather_tile(d_ref, s):
        def issue(it, carry):
            for j in range(ISSUE_UNROLL):
                r = it * ISSUE_UNROLL + j
                for k in range(TOP_K):
                    d = d_ref[r * TOP_K + k]
                    pltpu.make_async_copy(_slab(y_hbm, d), _slab(bufs.at[s, k], r),
                                          sems.at[s]).start(priority=k % 2)
            return carry

        lax.fori_loop(0, ts // ISSUE_UNROLL, issue, 0)

    @pl.when(i == 0)
    def _():
        gather_tile(dest_ref, 0)

    @pl.when(i + 1 < n_steps)
    def _():
        gather_tile(dest_next_ref, 1 - slot)

    for k in range(TOP_K):
        pltpu.make_async_copy(y_hbm.at[pl.ds(0, ts * SUBLANES)], bufs.at[slot, k], sems.at[slot]).wait()

    acc = x1_ref[...]
    gates = gates_ref[...]
    for k in range(TOP_K):
        acc = acc + _slabs_to_rows(bufs.at[slot, k], ts) * gates[:, k:k + 1]
    ms = jnp.mean(acc * acc, axis=-1, keepdims=True)
    o_ref[...] = acc * lax.rsqrt(ms + NORM_EPS) * fg_ref[...]


def _combine(y_rows, dest_flat, x1, gates, final_g):
    T, D = x1.shape
    ts = min(TS_COMB, T)
    n_steps = T // ts
    tok = lambda i: (i, 0)
    return pl.pallas_call(
        functools.partial(_combine_kernel, ts=ts, n_steps=n_steps),
        grid=(n_steps,),
        in_specs=[
            pl.BlockSpec((ts * TOP_K,), lambda i: (i,), memory_space=pltpu.SMEM),
            pl.BlockSpec((ts * TOP_K,), lambda i: (jnp.minimum(i + 1, n_steps - 1),),
                         memory_space=pltpu.SMEM),
            pl.BlockSpec(memory_space=pl.ANY),
            pl.BlockSpec((ts, D), tok),
            pl.BlockSpec((ts, LANES), tok),
            pl.BlockSpec((1, D), lambda i: (0, 0)),
        ],
        out_specs=pl.BlockSpec((ts, D), tok),
        out_shape=jax.ShapeDtypeStruct((T, D), F32),
        scratch_shapes=[pltpu.VMEM((2, TOP_K, ts * SUBLANES, LANES), F32),
                        pltpu.SemaphoreType.DMA((2,))],
        compiler_params=_cparams(("arbitrary",)),
        name="combine",
    )(dest_flat, dest_flat, y_rows, x1, gates, final_g.reshape(1, D))


def kernel(x, norm1_g, w_in, q_norm_g, k_norm_g, conv_w, conv_b, lru_wa, lru_ba, lru_wi, lru_bi,
           lru_lam, attn_out_g, lru_out_g, w_out, norm2_g, w_router, b_router, w_gate, b_gate,
           w_up, b_up, w_down, b_down, final_g):
    B, S, D = x.shape
    T = B * S
    assert w_in.shape[0] == 1, "single-layer trunk: the final norm is fused into the layer's combine"
    assert D == SUBLANES * LANES, "a token row is moved as one (8, 128) f32 slab"
    assert S % max(TS_IN, TQ, TK, TC_LRU) == 0 and S % GRID_W == 0, "sequence tiles must divide S"
    x2 = x.reshape(T, D)
    for l in range(1):
        qt, k, vt, lru_x, lru_gate = _inproj(x2, norm1_g[l], w_in[l], q_norm_g[l], k_norm_g[l], S)
        score_bound = (HEAD_DIM * Q_SCALE * jnp.max(jnp.abs(q_norm_g[l]))
                       * jnp.max(jnp.abs(k_norm_g[l])))
        lru_ops = _lru_operands(conv_w[l], conv_b[l], lru_wa[l], lru_ba[l], lru_wi[l], lru_bi[l],
                                lru_lam[l])
        attn, lru = _mixers(qt, k.reshape(B, S, -1), vt, lru_x.reshape(B, S, -1),
                            lru_gate.reshape(B, S, -1), lru_ops, score_bound, B, S)
        x1, xn3, route, gates, cnt = _outproj_router(
            attn.reshape(T, -1), lru.reshape(T, -1), x2, attn_out_g[l], lru_out_g[l], w_out[l],
            norm2_g[l], w_router[l], b_router[l])

        n_rows = T * TOP_K + N_EXPERTS * ROW_BLOCK
        pstart, block_e, fill, next_e, n_active = _routing_plan(cnt, n_rows // ROW_BLOCK)
        dest_flat = _dest_rows(route, pstart)
        x_rows = _dispatch(xn3, fill, dest_flat, n_rows)
        y_rows = _experts(x_rows, block_e, n_active, next_e, w_gate[l], b_gate[l], w_up[l], b_up[l],
                          w_down[l], b_down[l])
        x2 = _combine(y_rows, dest_flat, x1, gates, final_g)
    return x2.reshape(B, S, D)
```

```python
import functools
import math

import jax
import jax.numpy as jnp
import numpy as np
from jax import lax
from jax.experimental import pallas as pl
from jax.experimental.pallas import tpu as pltpu

F32 = jnp.float32
BF16 = jnp.bfloat16

GRID_W = 64
HEAD_DIM = 64
N_Q_HEADS = 8
N_KV_HEADS = 2
GQA_GROUP = N_Q_HEADS // N_KV_HEADS
ATTN_W = N_Q_HEADS * HEAD_DIM
KV_W = N_KV_HEADS * HEAD_DIM
LRU_BLOCKS = 8
LRU_C = 8.0
CONV_W = 4
CONV_PAD_L = 2
ROPE_THETA = 10000.0
ROPE_HALF = HEAD_DIM // 2
ROPE_M = ROPE_HALF // 2
N_EXPERTS = 32
TOP_K = 4
SWIGLU_ALPHA = 1.702
SWIGLU_LIMIT = 7.0
NORM_EPS = 1e-5
QK_EPS = 1e-6
LOG2_E = 1.4426950408889634
Q_SCALE = HEAD_DIM ** -0.5 * LOG2_E
SAFE_SCORE_LOG2 = 96.0

LANES = 128
SUBLANES = 8
BF16_SUBLANES = 16
PV_ROWS = HEAD_DIM + BF16_SUBLANES
VMEM_LIMIT = 48 * 1024 * 1024
EXPERT_VMEM_LIMIT = 56 * 1024 * 1024

TS_IN = 512
TQ = 256
TK = 256
KV_UNROLL = 32
HEADS_PER_STEP = 2
Q_TILES_PER_STEP = 4
TC_LRU = 512
LRU_UNROLL = 8
TS_OUT = 512
TS_DEST = 2048
ROW_BLOCK = 512
TS_DISP = 1024
TS_COMB = 256
ISSUE_UNROLL = 8


def _cparams(sem):
    return pltpu.CompilerParams(dimension_semantics=sem, vmem_limit_bytes=VMEM_LIMIT)


def _inproj_kernel(x_ref, g1_ref, wt_ref, w_ref, qg_ref, kg_ref, cos_ref, sin_ref, cost_ref, sint_ref,
                   q_ref, k_ref, v_ref, lx_ref, lg_ref, *, lru_w):
    x = x_ref[...]
    ms = jnp.mean(x * x, axis=-1, keepdims=True)
    xn = (x * lax.rsqrt(ms + NORM_EPS) * g1_ref[...]).astype(BF16)
    ht = lax.dot_general(wt_ref[...], xn, (((1,), (1,)), ((), ())), preferred_element_type=F32)
    h = jnp.dot(xn, w_ref[...], preferred_element_type=F32)

    qw = N_Q_HEADS * LANES
    kw = N_KV_HEADS * LANES
    cost = cost_ref[...]
    sint = sint_ref[...]
    row = lax.broadcasted_iota(jnp.int32, cost.shape, 0)
    first_half_t = (row % ROPE_HALF) < ROPE_M
    qg = qg_ref[...]
    for c in range(N_Q_HEADS):
        sl = slice(c * LANES, (c + 1) * LANES)
        xc = ht[sl]
        hms = jnp.sum(xc * xc, axis=0, keepdims=True) * (1.0 / HEAD_DIM)
        xc = xc * lax.rsqrt(hms + QK_EPS) * qg
        partner = jnp.where(first_half_t, pltpu.roll(xc, LANES - ROPE_M, 0), pltpu.roll(xc, ROPE_M, 0))
        q_ref[0, sl, :] = ((xc * cost + partner * sint) * Q_SCALE).astype(BF16)
    for c in range(N_KV_HEADS):
        sl = slice(c * LANES, (c + 1) * LANES)
        v_ref[0, sl, :] = jnp.where(row >= HEAD_DIM, 1.0, ht[qw + c * LANES: qw + (c + 1) * LANES]).astype(BF16)

    cos = cos_ref[...]
    sin = sin_ref[...]
    lane = lax.broadcasted_iota(jnp.int32, cos.shape, 1)
    first_half = (lane % ROPE_HALF) < ROPE_M
    for c in range(N_KV_HEADS):
        sl = slice(c * LANES, (c + 1) * LANES)
        xc = h[:, sl]
        hms = jnp.sum(xc * xc, axis=-1, keepdims=True) * (1.0 / HEAD_DIM)
        xc = xc * lax.rsqrt(hms + QK_EPS) * kg_ref[...]
        partner = jnp.where(first_half, pltpu.roll(xc, LANES - ROPE_M, 1), pltpu.roll(xc, ROPE_M, 1))
        k_ref[:, sl] = (xc * cos + partner * sin).astype(BF16)
    lx_ref[...] = h[:, kw: kw + lru_w]
    lg_ref[...] = h[:, kw + lru_w: kw + 2 * lru_w]


def _pad_heads(w, n_heads):
    lead = w.shape[:-1]
    w = w.reshape(lead + (n_heads, HEAD_DIM))
    w = jnp.pad(w, [(0, 0)] * len(lead) + [(0, 0), (0, LANES - HEAD_DIM)])
    return w.reshape(lead + (n_heads * LANES,))


def _rope_tables(S):
    t = np.arange(S)
    rows = (t // GRID_W).astype(np.float32)
    cols = (t % GRID_W).astype(np.float32)
    inv_freq = (ROPE_THETA ** (-np.arange(ROPE_M, dtype=np.float32) / ROPE_M)).astype(np.float32)
    ar = rows[:, None] * inv_freq[None, :]
    ac = cols[:, None] * inv_freq[None, :]
    cos = np.concatenate([np.cos(ar), np.cos(ar), np.cos(ac), np.cos(ac)], axis=-1)
    sin = np.concatenate([-np.sin(ar), np.sin(ar), -np.sin(ac), np.sin(ac)], axis=-1)
    pad = [(0, 0), (0, LANES - HEAD_DIM)]
    cos = np.pad(cos, pad).astype(np.float32)
    sin = np.pad(sin, pad).astype(np.float32)
    return cos, sin, np.ascontiguousarray(cos.T), np.ascontiguousarray(sin.T)


def _inproj(x2, norm1_g, w_in, q_norm_g, k_norm_g, S):
    T, D = x2.shape
    lru_w = (w_in.shape[1] - ATTN_W - 2 * KV_W) // 2
    o0, o1, o2 = ATTN_W, ATTN_W + KV_W, ATTN_W + 2 * KV_W
    w_t = jnp.concatenate([_pad_heads(w_in[:, :o0], N_Q_HEADS),
                           _pad_heads(w_in[:, o1:o2], N_KV_HEADS)], axis=1).T.astype(BF16)
    w_rest = jnp.concatenate([_pad_heads(w_in[:, o0:o1], N_KV_HEADS), w_in[:, o2:]],
                             axis=1).astype(BF16)
    qg = _pad_heads(q_norm_g.reshape(1, HEAD_DIM), 1).reshape(LANES, 1)
    kg = _pad_heads(k_norm_g.reshape(1, HEAD_DIM), 1)
    cos, sin, cos_t, sin_t = _rope_tables(S)
    ts = TS_IN
    n_s = S // ts
    qw, kw = N_Q_HEADS * LANES, N_KV_HEADS * LANES
    const = lambda i: (0, 0)
    tok = lambda i: (i, 0)
    pos = lambda i: (i % n_s, 0)
    pos_t = lambda i: (0, i % n_s)
    tposed = lambda i: (i // n_s, 0, i % n_s)
    return pl.pallas_call(
        functools.partial(_inproj_kernel, lru_w=lru_w),
        grid=(T // ts,),
        in_specs=[
            pl.BlockSpec((ts, D), tok),
            pl.BlockSpec((1, D), const),
            pl.BlockSpec(w_t.shape, const),
            pl.BlockSpec(w_rest.shape, const),
            pl.BlockSpec((LANES, 1), const),
            pl.BlockSpec((1, LANES), const),
            pl.BlockSpec((ts, LANES), pos),
            pl.BlockSpec((ts, LANES), pos),
            pl.BlockSpec((LANES, ts), pos_t),
            pl.BlockSpec((LANES, ts), pos_t),
        ],
        out_specs=[
            pl.BlockSpec((1, qw, ts), tposed),
            pl.BlockSpec((ts, kw), tok),
            pl.BlockSpec((1, kw, ts), tposed),
            pl.BlockSpec((ts, lru_w), tok),
            pl.BlockSpec((ts, lru_w), tok),
        ],
        out_shape=[
            jax.ShapeDtypeStruct((T // S, qw, S), BF16),
            jax.ShapeDtypeStruct((T, kw), BF16),
            jax.ShapeDtypeStruct((T // S, kw, S), BF16),
            jax.ShapeDtypeStruct((T, lru_w), F32),
            jax.ShapeDtypeStruct((T, lru_w), F32),
        ],
        compiler_params=_cparams(("parallel",)),
        name="inproj",
    )(x2, norm1_g.reshape(1, D), w_t, w_rest, qg, kg, cos, sin, cos_t, sin_t)


def _attn_kernel(qt_ref, k_ref, vt_ref, o_ref, acc_ref, s_ref, p_ref, *, tq, tk, n_kv, kv_unroll):
    hp = HEADS_PER_STEP
    spt = GQA_GROUP // hp
    acc_ref[...] = jnp.zeros(acc_ref.shape, F32)

    def scores(j, sp):
        kt = k_ref[0, pl.ds(pl.multiple_of(j * tk, tk), tk), :]
        out = []
        for u in range(hp):
            g = sp * hp + u
            s = jnp.dot(kt, qt_ref[0, g * LANES:(g + 1) * LANES, :], preferred_element_type=F32)
            out.append((s, jnp.max(s, axis=0, keepdims=True)))
        return out

    def softmax_stage(sc, ms, sp):
        out = []
        for u, (s, s_max) in enumerate(sc):
            h = sp * hp + u
            m_new = jnp.maximum(ms[h], s_max)
            out.append((jnp.exp2(ms[h] - m_new), jnp.exp2(s - m_new).astype(BF16)))
            ms[h] = m_new
        return out

    def pv_stage(j, sp, ap):
        vt = vt_ref[0, 0:PV_ROWS, pl.ds(pl.multiple_of(j * tk, tk), tk)]
        for u, (alpha, p) in enumerate(ap):
            g = sp * hp + u
            acc_ref[g] = alpha * acc_ref[g] + jnp.dot(vt, p, preferred_element_type=F32)

    ms = [jnp.full((1, tq), -jnp.inf, F32)] * GQA_GROUP
    ap = softmax_stage(scores(0, 0), ms, 0)
    sc = scores(min(1 // spt, n_kv - 1), 1 % spt)
    for u in range(hp):
        s_ref[u] = sc[u][0]
        p_ref[u] = ap[u][1]

    def body(it, carry):
        ms = list(carry[:GQA_GROUP])
        ap = [(carry[GQA_GROUP + u], p_ref[u]) for u in range(hp)]
        sc = [(s_ref[u], carry[GQA_GROUP + hp + u]) for u in range(hp)]
        for n in range(kv_unroll * spt):
            j = it * kv_unroll + n // spt
            j_next = jnp.minimum(it * kv_unroll + (n + 2) // spt, n_kv - 1)
            sc_next = scores(j_next, (n + 2) % spt)
            ap_next = softmax_stage(sc, ms, (n + 1) % spt)
            pv_stage(j, n % spt, ap)
            sc, ap = sc_next, ap_next
        for u in range(hp):
            s_ref[u] = sc[u][0]
            p_ref[u] = ap[u][1]
        return tuple(ms) + tuple(a for a, _ in ap) + tuple(m for _, m in sc)

    lax.fori_loop(0, n_kv // kv_unroll, body,
                  tuple(ms) + tuple(a for a, _ in ap) + tuple(m for _, m in sc))
    _attn_finalize(acc_ref, o_ref, tq)


def _attn_finalize(acc_ref, o_ref, tq, row0=0):
    pad = jnp.zeros((LANES - HEAD_DIM, tq), F32)
    for g in range(GQA_GROUP):
        acc = acc_ref[g]
        o = acc[0:HEAD_DIM] / acc[HEAD_DIM:HEAD_DIM + 1, :]
        o_ref[0, row0:row0 + tq, g * LANES:(g + 1) * LANES] = (
            jnp.concatenate([o, pad], axis=0).T.astype(BF16))


def _attn_bounded_kernel(qt_ref, k_ref, vt_ref, o_ref, acc_ref, s_ref, p_ref, *, tq, tk, n_kv, kv_unroll,
                         q_tiles):
    for t in range(q_tiles):
        _attn_bounded_tile(qt_ref, k_ref, vt_ref, o_ref, acc_ref.at[t], s_ref.at[t], p_ref.at[t],
                           tq=tq, tk=tk, n_kv=n_kv, kv_unroll=kv_unroll, q0=t * tq)


def _attn_bounded_tile(qt_ref, k_ref, vt_ref, o_ref, acc_ref, s_ref, p_ref, *, tq, tk, n_kv, kv_unroll,
                       q0):
    hp = HEADS_PER_STEP
    spt = GQA_GROUP // hp
    acc_ref[...] = jnp.zeros(acc_ref.shape, F32)

    def scores(j, sp):
        kt = k_ref[0, pl.ds(pl.multiple_of(j * tk, tk), tk), :]
        return [jnp.dot(kt, qt_ref[0, (sp * hp + u) * LANES:(sp * hp + u + 1) * LANES, q0:q0 + tq],
                        preferred_element_type=F32) for u in range(hp)]

    def probs(sc):
        return [jnp.exp2(s).astype(BF16) for s in sc]

    def pv_stage(j, sp, ps):
        vt = vt_ref[0, 0:PV_ROWS, pl.ds(pl.multiple_of(j * tk, tk), tk)]
        for u, p in enumerate(ps):
            acc_ref[sp * hp + u] += jnp.dot(vt, p, preferred_element_type=F32)

    ps = probs(scores(0, 0))
    sc = scores(min(1 // spt, n_kv - 1), 1 % spt)
    for u in range(hp):
        s_ref[u] = sc[u]
        p_ref[u] = ps[u]

    def body(it, carry):
        ps = [p_ref[u] for u in range(hp)]
        sc = [s_ref[u] for u in range(hp)]
        for n in range(kv_unroll * spt):
            j = it * kv_unroll + n // spt
            j_next = jnp.minimum(it * kv_unroll + (n + 2) // spt, n_kv - 1)
            sc_next = scores(j_next, (n + 2) % spt)
            ps_next = probs(sc)
            pv_stage(j, n % spt, ps)
            sc, ps = sc_next, ps_next
        for u in range(hp):
            s_ref[u] = sc[u]
            p_ref[u] = ps[u]
        return carry

    lax.fori_loop(0, n_kv // kv_unroll, body, 0)
    _attn_finalize(acc_ref, o_ref, tq, q0)


def _attention(qt, k, vt, *, bounded, B, S):
    tq = min(TQ, S)
    tk = min(TK, S)
    gw = GQA_GROUP * LANES
    tiling = dict(tq=tq, tk=tk, n_kv=S // tk, kv_unroll=math.gcd(S // tk, KV_UNROLL))
    if bounded:
        q_tiles = math.gcd(S // tq, Q_TILES_PER_STEP)
        body = functools.partial(_attn_bounded_kernel, q_tiles=q_tiles, **tiling)
        scratch = [pltpu.VMEM((q_tiles, GQA_GROUP, PV_ROWS, tq), F32),
                   pltpu.VMEM((q_tiles, HEADS_PER_STEP, tk, tq), F32),
                   pltpu.VMEM((q_tiles, HEADS_PER_STEP, tk, tq), BF16)]
    else:
        q_tiles = 1
        body = functools.partial(_attn_kernel, **tiling)
        scratch = [pltpu.VMEM((GQA_GROUP, PV_ROWS, tq), F32),
                   pltpu.VMEM((HEADS_PER_STEP, tk, tq), F32),
                   pltpu.VMEM((HEADS_PER_STEP, tk, tq), BF16)]
    tqs = q_tiles * tq
    return pl.pallas_call(
        body,
        grid=(B, N_KV_HEADS, S // tqs),
        in_specs=[
            pl.BlockSpec((1, gw, tqs), lambda b, h, i: (b, h, i)),
            pl.BlockSpec((1, S, LANES), lambda b, h, i: (b, 0, h)),
            pl.BlockSpec((1, LANES, S), lambda b, h, i: (b, h, 0)),
        ],
        out_specs=pl.BlockSpec((1, tqs, gw), lambda b, h, i: (b, i, h)),
        out_shape=jax.ShapeDtypeStruct((B, S, N_Q_HEADS * LANES), BF16),
        scratch_shapes=scratch,
        compiler_params=_cparams(("parallel", "parallel", "parallel")),
        name="attention_bounded" if bounded else "attention",
    )(qt, k, vt)


def _scan_chunk(a, b, h_in, reverse):
    n = a.shape[0]
    n_groups = n // SUBLANES
    a = a.reshape(n_groups, SUBLANES, LANES)
    b = b.reshape(n_groups, SUBLANES, LANES)
    sub = lax.broadcasted_iota(jnp.int32, a.shape, 1)
    d = 1
    while d < SUBLANES:
        if reverse:
            keep = sub < SUBLANES - d
            shift = SUBLANES - d
        else:
            keep = sub >= d
            shift = d
        a_sh = jnp.where(keep, pltpu.roll(a, shift, 1), 1.0)
        b_sh = jnp.where(keep, pltpu.roll(b, shift, 1), 0.0)
        b = a * b_sh + b
        a = a * a_sh
        d *= 2
    a = a.reshape(n, LANES)
    b = b.reshape(n, LANES)
    order = range(n_groups - 1, -1, -1) if reverse else range(n_groups)
    edge = h_in
    out = [None] * n_groups
    for v in order:
        rows = slice(v * SUBLANES, (v + 1) * SUBLANES)
        hv = b[rows] + a[rows] * jnp.broadcast_to(edge, (SUBLANES, LANES))
        out[v] = hv
        edge = hv[0:1] if reverse else hv[SUBLANES - 1:SUBLANES]
    return jnp.concatenate(out, axis=0), edge


def _lru_pad_input(u_ref, up_ref, S):
    zeros = jnp.zeros((SUBLANES, LANES), F32)
    up_ref[0:SUBLANES, :] = zeros
    up_ref[S + SUBLANES:S + 2 * SUBLANES, :] = zeros
    up_ref[SUBLANES:S + SUBLANES, :] = u_ref[0]


def _lru_gates(up_ref, cw_ref, cb_ref, w_ref, bias_ref, lam_ref, t0, tc, d):
    cw = cw_ref[...]
    xc = cb_ref[...]
    for j in range(CONV_W):
        xc = xc + up_ref[pl.ds(t0 + SUBLANES + j - CONV_PAD_L, tc), :] * cw[j:j + 1, :]
    gw = 2 * LANES
    g = jnp.dot(xc.astype(BF16), w_ref[0, :, d * gw:(d + 1) * gw],
                preferred_element_type=F32) + bias_ref[0, :, d * gw:(d + 1) * gw]
    r = jax.nn.sigmoid(g[:, :LANES])
    i = jax.nn.sigmoid(g[:, LANES:])
    a = jnp.exp(-LRU_C * r * jax.nn.softplus(-lam_ref[d:d + 1, :]))
    y = 1.0 - a * a
    b = jnp.where(y > 0.0, y * lax.rsqrt(y), 0.0) * i * xc
    return a, b


def _lru_kernel(u_ref, gate_ref, cw_ref, cb_ref, w_ref, bias_ref, lam_ref, o_ref,
                up_ref, hf_ref, *, S, tc):
    _lru_pad_input(u_ref, up_ref, S)
    n_chunks = S // tc
    params = (up_ref, cw_ref, cb_ref, w_ref, bias_ref, lam_ref)

    def fwd(c, h):
        t0 = pl.multiple_of(c * tc, tc)
        hc, h_last = _scan_chunk(*_lru_gates(*params, t0, tc, 0), h, False)
        hf_ref[pl.ds(t0, tc), :] = hc
        return h_last

    unroll = math.gcd(n_chunks, LRU_UNROLL)

    def grouped(step):
        def body(i, h):
            for j in range(unroll):
                h = step(i * unroll + j, h)
            return h
        return body

    lax.fori_loop(0, n_chunks // unroll, grouped(fwd), jnp.zeros((1, LANES), F32))

    def bwd(ci, h):
        t0 = pl.multiple_of((n_chunks - 1 - ci) * tc, tc)
        hc, h_last = _scan_chunk(*_lru_gates(*params, t0, tc, 1), h, True)
        gate = gate_ref[0, pl.ds(t0, tc), :]
        o_ref[0, pl.ds(t0, tc), :] = (hf_ref[pl.ds(t0, tc), :] + hc) * jax.nn.gelu(gate)
        return h_last

    lax.fori_loop(0, n_chunks // unroll, grouped(bwd), jnp.zeros((1, LANES), F32))


def _block_diag_pairs(w):
    nb, bw, _ = w.shape
    w = w.reshape(nb // 2, 2, bw, bw)
    z = jnp.zeros_like(w[:, 0])
    top = jnp.concatenate([w[:, 0], z], axis=-1)
    bot = jnp.concatenate([z, w[:, 1]], axis=-1)
    return jnp.concatenate([top, bot], axis=-2)


def _lru_operands(conv_w, conv_b, wa, ba, wi, bi, lam):
    C = conv_b.shape[0]
    nc = C // LANES
    w = jnp.concatenate([_block_diag_pairs(wa[0]), _block_diag_pairs(wi[0]),
                         _block_diag_pairs(wa[1]), _block_diag_pairs(wi[1])], axis=-1).astype(BF16)
    bias = jnp.stack([ba[0].reshape(nc, LANES), bi[0].reshape(nc, LANES),
                      ba[1].reshape(nc, LANES), bi[1].reshape(nc, LANES)], axis=1)
    return conv_w, conv_b.reshape(1, C), w, bias.reshape(nc, 1, 4 * LANES), lam


def _lru_specs(S, unit):
    seq = lambda *g: (unit(*g)[0], 0, unit(*g)[1])
    chan = lambda *g: (0, unit(*g)[1])
    blk = lambda *g: (unit(*g)[1], 0, 0)
    in_specs = [
        pl.BlockSpec((1, S, LANES), seq),
        pl.BlockSpec((1, S, LANES), seq),
        pl.BlockSpec((CONV_W, LANES), chan),
        pl.BlockSpec((1, LANES), chan),
        pl.BlockSpec((1, LANES, 4 * LANES), blk),
        pl.BlockSpec((1, 1, 4 * LANES), blk),
        pl.BlockSpec((2, LANES), chan),
    ]
    return in_specs, pl.BlockSpec((1, S, LANES), seq)


def _lru(lru_x, lru_gate, lru_ops, B, S):
    C = lru_x.shape[-1]
    tc = min(TC_LRU, S)
    in_specs, out_spec = _lru_specs(S, lambda b, c: (b, c))
    return pl.pallas_call(
        functools.partial(_lru_kernel, S=S, tc=tc),
        grid=(B, C // LANES),
        in_specs=in_specs,
        out_specs=out_spec,
        out_shape=jax.ShapeDtypeStruct((B, S, C), F32),
        scratch_shapes=[
            pltpu.VMEM((S + 2 * SUBLANES, LANES), F32),
            pltpu.VMEM((S, LANES), F32),
        ],
        compiler_params=_cparams(("parallel", "parallel")),
        name="rglru",
    )(lru_x, lru_gate, *lru_ops)


def _mixers(qt, k, vt, lru_x, lru_gate, lru_ops, score_bound, B, S):
    attn = lax.cond(score_bound <= SAFE_SCORE_LOG2,
                    functools.partial(_attention, bounded=True, B=B, S=S),
                    functools.partial(_attention, bounded=False, B=B, S=S), qt, k, vt)
    return attn, _lru(lru_x, lru_gate, lru_ops, B, S)


def _rows_to_slabs(ref, x):
    n = x.shape[0]
    for s in range(SUBLANES):
        ref[pl.ds(s, n, stride=SUBLANES), :] = x[:, s * LANES:(s + 1) * LANES]


def _slabs_to_rows(ref, n):
    return jnp.concatenate([ref[pl.ds(s, n, stride=SUBLANES), :] for s in range(SUBLANES)], axis=1)


def _slab(ref, r):
    return ref.at[pl.ds(pl.multiple_of(r * SUBLANES, SUBLANES), SUBLANES)]


def _outproj_kernel(a_ref, l_ref, x_ref, ag_ref, lg_ref, wa_ref, wl_ref, g2_ref,
                    wr_ref, br_ref, tri_ref,
                    x1_ref, xn3_ref, route_ref, gates_ref, cnt_ref, carry_ref, *, attn_w, lru_w):
    step = pl.program_id(0)

    @pl.when(step == 0)
    def _():
        carry_ref[...] = jnp.zeros_like(carry_ref)

    a = a_ref[...].astype(F32)
    ams = jnp.sum(a * a, axis=-1, keepdims=True) * (1.0 / attn_w)
    an = a * lax.rsqrt(ams + NORM_EPS) * ag_ref[...]
    l = l_ref[...]
    lms = jnp.sum(l * l, axis=-1, keepdims=True) * (1.0 / lru_w)
    ln = l * lax.rsqrt(lms + NORM_EPS) * lg_ref[...]
    mix = (jnp.dot(an.astype(BF16), wa_ref[...], preferred_element_type=F32)
           + jnp.dot(ln.astype(BF16), wl_ref[...], preferred_element_type=F32))
    x1 = x_ref[...] + mix
    x1_ref[...] = x1
    ms = jnp.mean(x1 * x1, axis=-1, keepdims=True)
    xn = x1 * lax.rsqrt(ms + NORM_EPS) * g2_ref[...]
    _rows_to_slabs(xn3_ref, xn)

    logits = jnp.dot(xn.astype(BF16), wr_ref[...], preferred_element_type=F32) + br_ref[...]
    lane = lax.broadcasted_iota(jnp.int32, logits.shape, 1)
    neg = -jnp.inf
    work = jnp.where(lane < N_EXPERTS, logits, neg)
    sel = jnp.zeros(logits.shape, F32)
    idxs, vals = [], []
    for _ in range(TOP_K):
        m = jnp.max(work, axis=1, keepdims=True)
        idx = jnp.min(jnp.where(work == m, lane, LANES), axis=1, keepdims=True)
        hit = lane == idx
        work = jnp.where(hit, neg, work)
        sel = sel + hit.astype(F32)
        idxs.append(idx)
        vals.append(m)
    es = [jnp.exp(v - vals[0]) for v in vals]
    den = es[0] + es[1] + es[2] + es[3]

    prefix = jnp.dot(tri_ref[...], sel.astype(BF16), preferred_element_type=F32) + carry_ref[...]
    carry_ref[...] = carry_ref[...] + jnp.sum(sel, axis=0, keepdims=True)
    cnt_ref[...] = carry_ref[...]

    route = jnp.zeros(logits.shape, jnp.int32)
    gates = jnp.zeros(logits.shape, F32)
    for k in range(TOP_K):
        rank = jnp.sum(jnp.where(lane == idxs[k], prefix, 0.0), axis=1, keepdims=True).astype(jnp.int32)
        route = jnp.where(lane == k, idxs[k], route)
        route = jnp.where(lane == TOP_K + k, rank, route)
        gates = jnp.where(lane == k, es[k] / den, gates)
    route_ref[...] = route
    gates_ref[...] = gates


def _outproj_router(attn, lru, x2, attn_out_g, lru_out_g, w_out, norm2_g, w_router, b_router):
    T, D = x2.shape
    lru_w = lru.shape[-1]
    ts = min(TS_OUT, T)
    wa = w_out[:ATTN_W].reshape(N_Q_HEADS, HEAD_DIM, D)
    wa = jnp.pad(wa, ((0, 0), (0, LANES - HEAD_DIM), (0, 0))).reshape(N_Q_HEADS * LANES, D).astype(BF16)
    wl = w_out[ATTN_W:].astype(BF16)
    ag = _pad_heads(attn_out_g.reshape(1, ATTN_W), N_Q_HEADS)
    wr = jnp.pad(w_router, ((0, 0), (0, LANES - N_EXPERTS))).astype(BF16)
    br =jnp.pad(b_router.reshape(1, N_EXPERTS), ((0, 0), (0, LANES - N_EXPERTS)))
    tri = (jnp.arange(ts)[:, None] > jnp.arange(ts)[None, :]).astype(BF16)
    const = lambda i: (0, 0)
    tok = lambda i: (i, 0)
    aw = N_Q_HEADS * LANES
    return pl.pallas_call(
        functools.partial(_outproj_kernel, attn_w=ATTN_W, lru_w=lru_w),
        grid=(T // ts,),
        in_specs=[
            pl.BlockSpec((ts, aw), tok),
            pl.BlockSpec((ts, lru_w), tok),
            pl.BlockSpec((ts, D), tok),
            pl.BlockSpec((1, aw), const),
            pl.BlockSpec((1, lru_w), const),
            pl.BlockSpec((aw, D), const),
            pl.BlockSpec((lru_w, D), const),
            pl.BlockSpec((1, D), const),
            pl.BlockSpec((D, LANES), const),
            pl.BlockSpec((1, LANES), const),
            pl.BlockSpec((ts, ts), const),
        ],
        out_specs=[
            pl.BlockSpec((ts, D), tok),
            pl.BlockSpec((ts * SUBLANES, LANES), tok),
            pl.BlockSpec((ts, LANES), tok),
            pl.BlockSpec((ts, LANES), tok),
            pl.BlockSpec((1, LANES), const),
        ],
        out_shape=[
            jax.ShapeDtypeStruct((T, D), F32),
            jax.ShapeDtypeStruct((T * SUBLANES, LANES), F32),
            jax.ShapeDtypeStruct((T, LANES), jnp.int32),
            jax.ShapeDtypeStruct((T, LANES), F32),
            jax.ShapeDtypeStruct((1, LANES), F32),
        ],
        scratch_shapes=[pltpu.VMEM((1, LANES), F32)],
        compiler_params=_cparams(("arbitrary",)),
        name="outproj_router",
    )(attn, lru, x2, ag, lru_out_g.reshape(1, lru_w), wa, wl, norm2_g.reshape(1, D),
      wr, br, tri)


def _plan_kernel(cnt_ref, pstart_ref, plan_ref):
    cnt = cnt_ref[...]
    lane = lax.broadcasted_iota(jnp.int32, cnt.shape, 1)
    padded = jnp.floor((cnt + (ROW_BLOCK - 1)) * (1.0 / ROW_BLOCK)) * ROW_BLOCK
    pend = padded
    d = 1
    while d < N_EXPERTS:
        pend = pend + jnp.where(lane >= d, pltpu.roll(pend, d, 1), 0.0)
        d *= 2
    pstart_ref[...] = pend - padded
    total = jnp.max(pend, axis=1, keepdims=True)

    shape = plan_ref.shape
    lanes = lax.broadcasted_iota(jnp.int32, shape, 1)
    is_expert = lanes < N_EXPERTS
    start = lax.broadcasted_iota(jnp.int32, shape, 0).astype(F32) * ROW_BLOCK

    def groups_ending_by(row):
        return jnp.sum(jnp.where(jnp.logical_and(pend <= row, is_expert), 1.0, 0.0), axis=1, keepdims=True)

    block_e = jnp.minimum(groups_ending_by(start), N_EXPERTS - 1.0)
    tail = jnp.max(jnp.where(jnp.logical_and(jnp.logical_and(pend == start + ROW_BLOCK, padded > 0.0),
                                             is_expert), 1.0, 0.0), axis=1, keepdims=True)
    fill = jnp.maximum(tail, jnp.where(start[:, 0:1] >= total, 1.0, 0.0))
    group_end = jnp.sum(jnp.where(lanes.astype(F32) == block_e, pend, 0.0), axis=1, keepdims=True)
    next_e = jnp.where(group_end < total,
                       jnp.minimum(groups_ending_by(group_end), N_EXPERTS - 1.0), -1.0)
    plan = jnp.where(lanes == 0, block_e,
                     jnp.where(lanes == 1, fill,
                               jnp.where(lanes == 2, next_e, total * (1.0 / ROW_BLOCK))))
    plan_ref[...] = plan.astype(jnp.int32)


def _routing_plan(cnt, n_blocks):
    assert ROW_BLOCK & (ROW_BLOCK - 1) == 0, "exact f32 division by the row block size"
    rows = -(-n_blocks // SUBLANES) * SUBLANES
    pstart, plan = pl.pallas_call(
        _plan_kernel,
        out_shape=[jax.ShapeDtypeStruct((1, LANES), F32),
                   jax.ShapeDtypeStruct((rows, LANES), jnp.int32)],
        name="routing_plan",
    )(cnt)
    return pstart, plan[:n_blocks, 0], plan[:n_blocks, 1], plan[:n_blocks, 2], plan[0:1, 3]


def _dest_kernel(route_ref, pstart_ref, dest_ref):
    route = route_ref[...]
    lane = lax.broadcasted_iota(jnp.int32, route.shape, 1)
    pstart = pstart_ref[...]
    dest = jnp.zeros(route.shape, jnp.int32)
    for k in range(TOP_K):
        start = jnp.sum(jnp.where(lane == route[:, k:k + 1], pstart, 0.0), axis=1, keepdims=True)
        dest = jnp.where(lane == k, start.astype(jnp.int32) + route[:, TOP_K + k:TOP_K + k + 1], dest)
    dest_ref[...] = dest


def _dest_rows(route, pstart):
    T = route.shape[0]
    ts = math.gcd(TS_DEST, T)
    dest = pl.pallas_call(
        _dest_kernel,
        grid=(T // ts,),
        in_specs=[pl.BlockSpec((ts, LANES), lambda i: (i, 0)),
                  pl.BlockSpec((1, LANES), lambda i: (0, 0))],
        out_specs=pl.BlockSpec((ts, LANES), lambda i: (i, 0)),
        out_shape=jax.ShapeDtypeStruct((T, LANES), jnp.int32),
        compiler_params=_cparams(("parallel",)),
        name="dest_rows",
    )(route, pstart)
    return dest[:, :TOP_K].reshape(T * TOP_K)


def _dispatch_kernel(fill_ref, dest_ref, x_ref, out_hbm, zero_ref, sem, zero_sem, *, ts, n_blocks):
    block_slabs = ROW_BLOCK * SUBLANES

    def fill_copy(b):
        off = pl.multiple_of(b * block_slabs, block_slabs)
        return pltpu.make_async_copy(zero_ref, out_hbm.at[pl.ds(off, block_slabs)], zero_sem)

    @pl.when(pl.program_id(0) == 0)
    def _():
        zero_ref[...] = jnp.zeros(zero_ref.shape, F32)

        def start(b, carry):
            @pl.when(fill_ref[b] != 0)
            def _():
                fill_copy(b).start()
            return carry

        def wait(b, carry):
            @pl.when(fill_ref[b] != 0)
            def _():
                fill_copy(b).wait()
            return carry

        lax.fori_loop(0, n_blocks, start, 0)
        lax.fori_loop(0, n_blocks, wait, 0)

    def issue(i, carry):
        for j in range(ISSUE_UNROLL):
            r = i * ISSUE_UNROLL + j
            for k in range(TOP_K):
                d = dest_ref[r * TOP_K + k]
                pltpu.make_async_copy(_slab(x_ref, r), _slab(out_hbm, d), sem).start(priority=k % 2)
        return carry

    lax.fori_loop(0, ts // ISSUE_UNROLL, issue, 0)
    for k in range(TOP_K):
        pltpu.make_async_copy(x_ref, out_hbm.at[pl.ds(0, ts * SUBLANES)], sem).wait()


def _dispatch(xn_slabs, fill, dest_flat, n_rows):
    T = xn_slabs.shape[0] // SUBLANES
    ts = min(TS_DISP, T)
    grid_spec = pltpu.PrefetchScalarGridSpec(
        num_scalar_prefetch=1,
        grid=(T // ts,),
        in_specs=[
            pl.BlockSpec((ts * TOP_K,), lambda i, fl: (i,), memory_space=pltpu.SMEM),
            pl.BlockSpec((ts * SUBLANES, LANES), lambda i, fl: (i, 0)),
        ],
        out_specs=pl.BlockSpec(memory_space=pl.ANY),
        scratch_shapes=[pltpu.VMEM((ROW_BLOCK * SUBLANES, LANES), F32),
                        pltpu.SemaphoreType.DMA, pltpu.SemaphoreType.DMA],
    )
    return pl.pallas_call(
        functools.partial(_dispatch_kernel, ts=ts, n_blocks=n_rows // ROW_BLOCK),
        grid_spec=grid_spec,
        out_shape=jax.ShapeDtypeStruct((n_rows * SUBLANES, LANES), xn_slabs.dtype),
        compiler_params=_cparams(("arbitrary",)),
        name="dispatch",
    )(fill, dest_flat, xn_slabs)


def _expert_loop_kernel(be_ref, na_ref, nxt_ref, x_hbm, wg_hbm, bg_ref, wu_hbm, bu_ref, wd_hbm, bd_ref,
                        y_hbm, xbuf, ybuf, stage_ref, wb_ref, x_sems, y_sems, w_sems):
    w_hbm = (wg_hbm, wu_hbm, wd_hbm)
    block_slabs = ROW_BLOCK * SUBLANES
    n_active = na_ref[0]

    def rows(b):
        return pl.ds(pl.multiple_of(b * block_slabs, block_slabs), block_slabs)

    def x_copy(b, s):
        return pltpu.make_async_copy(x_hbm.at[rows(b)], xbuf.at[s], x_sems.at[s])

    def y_copy(b, s):
        return pltpu.make_async_copy(ybuf.at[s], y_hbm.at[rows(b)], y_sems.at[s])

    def fetch(expert, s, m):
        return pltpu.make_async_copy(w_hbm[m].at[expert], stage_ref.at[s, m], w_sems.at[s, m])

    x_copy(0, 0).start()
    for m in range(3):
        fetch(be_ref[0], 0, m).start()

    def body(b, wslot):
        s = b % 2
        e = be_ref[b]
        x_copy(b, s).wait()

        @pl.when(b + 1 < n_active)
        def _():
            x_copy(b + 1, 1 - s).start()

        first = jnp.logical_or(b == 0, e != be_ref[jnp.maximum(b - 1, 0)])

        @pl.when(first)
        def _():
            for m in range(3):
                fetch(e, wslot, m).wait()
                wb_ref[m] = stage_ref[wslot, m].astype(BF16)

            @pl.when(nxt_ref[b] >= 0)
            def _():
                for m in range(3):
                    fetch(nxt_ref[b], 1 - wslot, m).start()

        @pl.when(b >= 2)
        def _():
            y_copy(b - 2, s).wait()

        x = _slabs_to_rows(xbuf.at[s], ROW_BLOCK).astype(BF16)
        g = jnp.dot(x, wb_ref[0], preferred_element_type=F32) + bg_ref[e]
        u = jnp.dot(x, wb_ref[1], preferred_element_type=F32) + bu_ref[e]
        g = jnp.minimum(g, SWIGLU_LIMIT)
        u = jnp.clip(u, -SWIGLU_LIMIT, SWIGLU_LIMIT)
        glu = g * jax.nn.sigmoid(SWIGLU_ALPHA * g)
        y = jnp.dot(((u + 1.0) * glu).astype(BF16), wb_ref[2], preferred_element_type=F32) + bd_ref[e]
        _rows_to_slabs(ybuf.at[s], y)
        y_copy(b, s).start()
        return jnp.where(first, 1 - wslot, wslot)

    lax.fori_loop(0, n_active, body, 0)

    @pl.when(n_active >= 2)
    def _():
        y_copy(n_active - 2, n_active % 2).wait()

    y_copy(n_active - 1, (n_active - 1) % 2).wait()


def _experts(x_rows, block_e, n_active, next_e, w_gate, b_gate, w_up, b_up, w_down, b_down):
    E, D, FF = w_gate.shape
    assert D == FF, "the three expert matrices share one staging shape"
    block_slabs = ROW_BLOCK * SUBLANES
    whole = lambda i, be, na, nx: (0, 0, 0)

    grid_spec = pltpu.PrefetchScalarGridSpec(
        num_scalar_prefetch=3,
        grid=(1,),
        in_specs=[
            pl.BlockSpec(memory_space=pl.ANY),
            pl.BlockSpec(memory_space=pl.ANY),
            pl.BlockSpec((E, 1, FF), whole),
            pl.BlockSpec(memory_space=pl.ANY),
            pl.BlockSpec((E, 1, FF), whole),
            pl.BlockSpec(memory_space=pl.ANY),
            pl.BlockSpec((E, 1, D), whole),
        ],
        out_specs=pl.BlockSpec(memory_space=pl.ANY),
        scratch_shapes=[
            pltpu.VMEM((2, block_slabs, LANES), F32),
            pltpu.VMEM((2, block_slabs, LANES), F32),
            pltpu.VMEM((2, 3, D, FF), F32),
            pltpu.VMEM((3, D, FF), BF16),
            pltpu.SemaphoreType.DMA((2,)),
            pltpu.SemaphoreType.DMA((2,)),
            pltpu.SemaphoreType.DMA((2, 3)),
        ],
    )
    return pl.pallas_call(
        _expert_loop_kernel,
        grid_spec=grid_spec,
        out_shape=jax.ShapeDtypeStruct(x_rows.shape, F32),
        input_output_aliases={3: 0},
        compiler_params=pltpu.CompilerParams(dimension_semantics=("arbitrary",),
                                             vmem_limit_bytes=EXPERT_VMEM_LIMIT),
        name="experts",
    )(block_e, n_active, next_e, x_rows, w_gate, b_gate.reshape(E, 1, FF), w_up,
      b_up.reshape(E, 1, FF), w_down, b_down.reshape(E, 1, D))


def _combine_kernel(dest_ref, dest_next_ref, y_hbm, x1_ref, gates_ref, fg_ref, o_ref, bufs, sems,
                    *, ts, n_steps):
    i = pl.program_id(0)
    slot = i % 2

    def gather_tile(d_ref, s):
        def issue(it, carry):
            for j in range(ISSUE_UNROLL):
                r = it * ISSUE_UNROLL + j
                for k in range(TOP_K):
                    d = d_ref[r * TOP_K + k]
                    pltpu.make_async_copy(_slab(y_hbm, d), _slab(bufs.at[s, k], r),
                                          sems.at[s]).start(priority=k % 2)
            return carry

        lax.fori_loop(0, ts // ISSUE_UNROLL, issue, 0)

    @pl.when(i == 0)
    def _():
        gather_tile(dest_ref, 0)

    @pl.when(i + 1 < n_steps)
    def _():
        gather_tile(dest_next_ref, 1 - slot)

    for k in range(TOP_K):
        pltpu.make_async_copy(y_hbm.at[pl.ds(0, ts * SUBLANES)], bufs.at[slot, k], sems.at[slot]).wait()

    acc = x1_ref[...]
    gates = gates_ref[...]
    for k in range(TOP_K):
        acc = acc + _slabs_to_rows(bufs.at[slot, k], ts) * gates[:, k:k + 1]
    ms = jnp.mean(acc * acc, axis=-1, keepdims=True)
    o_ref[...] = acc * lax.rsqrt(ms + NORM_EPS) * fg_ref[...]


def _combine(y_rows, dest_flat, x1, gates, final_g):
    T, D = x1.shape
    ts = min(TS_COMB, T)
    n_steps = T // ts
    tok = lambda i: (i, 0)
    return pl.pallas_call(
        functools.partial(_combine_kernel, ts=ts, n_steps=n_steps),
        grid=(n_steps,),
        in_specs=[
            pl.BlockSpec((ts * TOP_K,), lambda i: (i,), memory_space=pltpu.SMEM),
            pl.BlockSpec((ts * TOP_K,), lambda i: (jnp.minimum(i + 1, n_steps - 1),),
                         memory_space=pltpu.SMEM),
            pl.BlockSpec(memory_space=pl.ANY),
            pl.BlockSpec((ts, D), tok),
            pl.BlockSpec((ts, LANES), tok),
            pl.BlockSpec((1, D), lambda i: (0, 0)),
        ],
        out_specs=pl.BlockSpec((ts, D), tok),
        out_shape=jax.ShapeDtypeStruct((T, D), F32),
        scratch_shapes=[pltpu.VMEM((2, TOP_K, ts * SUBLANES, LANES), F32),
                        pltpu.SemaphoreType.DMA((2,))],
        compiler_params=_cparams(("arbitrary",)),
        name="combine",
    )(dest_flat, dest_flat, y_rows, x1, gates, final_g.reshape(1, D))


def kernel(x, norm1_g, w_in, q_norm_g, k_norm_g, conv_w, conv_b, lru_wa, lru_ba, lru_wi, lru_bi,
           lru_lam, attn_out_g, lru_out_g, w_out, norm2_g, w_router, b_router, w_gate, b_gate,
           w_up, b_up, w_down, b_down, final_g):
    B, S, D = x.shape
    T = B * S
    assert w_in.shape[0] == 1, "single-layer trunk: the final norm is fused into the layer's combine"
    assert D == SUBLANES * LANES, "a token row is moved as one (8, 128) f32 slab"
    assert S % max(TS_IN, TQ, TK, TC_LRU) == 0 and S % GRID_W == 0, "sequence tiles must divide S"
    x2 = x.reshape(T, D)
    for l in range(1):
        qt, k, vt, lru_x, lru_gate = _inproj(x2, norm1_g[l], w_in[l], q_norm_g[l], k_norm_g[l], S)
        score_bound = (HEAD_DIM * Q_SCALE * jnp.max(jnp.abs(q_norm_g[l]))
                       * jnp.max(jnp.abs(k_norm_g[l])))
        lru_ops = _lru_operands(conv_w[l], conv_b[l], lru_wa[l], lru_ba[l], lru_wi[l], lru_bi[l],
                                lru_lam[l])
        attn, lru = _mixers(qt, k.reshape(B, S, -1), vt, lru_x.reshape(B, S, -1),
                            lru_gate.reshape(B, S, -1), lru_ops, score_bound, B, S)
        x1, xn3, route, gates, cnt = _outproj_router(
            attn.reshape(T, -1), lru.reshape(T, -1), x2, attn_out_g[l], lru_out_g[l], w_out[l],
            norm2_g[l], w_router[l], b_router[l])

        n_rows = T * TOP_K + N_EXPERTS * ROW_BLOCK
        pstart, block_e, fill, next_e, n_active = _routing_plan(cnt, n_rows // ROW_BLOCK)
        dest_flat = _dest_rows(route, pstart)
        x_rows = _dispatch(xn3, fill, dest_flat, n_rows)
        y_rows = _experts(x_rows, block_e, n_active, next_e, w_gate[l], b_gate[l], w_up[l], b_up[l],
                          w_down[l], b_down[l])
        x2 = _combine(y_rows, dest_flat, x1, gates, final_g)
    return x2.reshape(B, S, D)
```
